```python
import math
import jax, jax.numpy as jnp
from jax import lax
import numpy as np

D_MODEL = 1024
BATCH = 2
SEQ = 8192
DEPTH = 2

N_A = DEPTH // 2
N_B = DEPTH - N_A
N_DENSE = (DEPTH + 1) // 2
N_MOE = DEPTH // 2

LA_HEADS = 8
LA_DK = 128
LA_DV = 128
QK_W = LA_HEADS * LA_DK
V_W = LA_HEADS * LA_DV
QKV_W = 2 * QK_W + V_W
A_PROJ_W = QKV_W + V_W + 2 * LA_HEADS
CONV_W = 4
CHUNK = 64

SW_HEADS = 16
SW_KV_HEADS = 4
SW_GROUP = SW_HEADS // SW_KV_HEADS
SW_HD = 64
Q_W = SW_HEADS * SW_HD
KV_W = SW_KV_HEADS * SW_HD
WINDOW = 128
BLOCK = 128

N_BUCKETS = 32
MAX_DIST = 128

D_FF = 3584
N_EXPERTS = 8
TOP_K = 2

EPS = 1e-6
NEG_INF = -1e30

kernel_name = "yoco_gdn_swa_sink_moe"


def rms_norm(x, g):
    xf = x.astype(jnp.float32)
    y = xf * lax.rsqrt(jnp.mean(xf * xf, axis=-1, keepdims=True) + EPS)
    return (y * g.astype(jnp.float32)).astype(x.dtype)


def l2_norm(x):
    return x * lax.rsqrt(jnp.sum(x * x, axis=-1, keepdims=True) + EPS)


def causal_depthwise_conv(x, w):
    c = x.shape[-1]
    return lax.conv_general_dilated(
        x, w[:, None, :].astype(x.dtype), window_strides=(1,),
        padding=[(CONV_W - 1, 0)], dimension_numbers=('NWC', 'WIO', 'NWC'),
        feature_group_count=c)


def chunk_gated_delta_rule(q, k, v, g, beta):
    f32 = jnp.float32
    b_, s_, h_, dk = q.shape
    dv = v.shape[-1]
    n = s_ // CHUNK

    def chunks(t):
        t = t.astype(f32).reshape((b_, n, CHUNK, h_) + t.shape[3:])
        return jnp.moveaxis(t, 3, 1)

    q = chunks(q) * (dk ** -0.5)
    k = chunks(k)
    v = chunks(v)
    beta = chunks(beta)
    g = jnp.cumsum(chunks(g), axis=-1)
    k_beta = k * beta[..., None]
    v_beta = v * beta[..., None]

    idx = np.arange(CHUNK)
    lower_incl = idx[:, None] >= idx[None, :]
    strict = idx[:, None] > idx[None, :]
    diff = g[..., :, None] - g[..., None, :]
    decay = jnp.where(lower_incl, jnp.exp(jnp.where(lower_incl, diff, 0.0)), 0.0)

    a = jnp.einsum('bhncd,bhnsd->bhncs', k_beta, k) * decay
    a = jnp.where(strict, a, 0.0) + jnp.eye(CHUNK, dtype=f32)
    rhs = jnp.concatenate([v_beta, k_beta * jnp.exp(g)[..., None]], axis=-1)
    sol = lax.linalg.triangular_solve(a, rhs, left_side=True, lower=True, unit_diagonal=True)
    u, w = sol[..., :dv], sol[..., dv:]

    attn_intra = jnp.einsum('bhncd,bhnsd->bhncs', q, k) * decay
    qe = q * jnp.exp(g)[..., None]
    g_last = g[..., -1]
    k_tail = k * jnp.exp(g_last[..., None] - g)[..., None]

    def step(state, inp):
        qe_i, w_i, u_i, a_i, k_i, gl_i = inp
        v_new = u_i - jnp.einsum('bhck,bhkv->bhcv', w_i, state)
        o = jnp.einsum('bhck,bhkv->bhcv', qe_i, state) + jnp.einsum('bhcs,bhsv->bhcv', a_i, v_new)
        state = state * jnp.exp(gl_i)[..., None, None] + jnp.einsum('bhck,bhcv->bhkv', k_i, v_new)
        return state, o

    xs = tuple(jnp.moveaxis(t, 2, 0) for t in (qe, w, u, attn_intra, k_tail, g_last))
    s0 = jnp.zeros((b_, h_, dk, dv), f32)
    _, o = lax.scan(step, s0, xs)
    return jnp.transpose(o, (1, 0, 3, 2, 4)).reshape(b_, s_, h_, dv)


def gated_deltanet(h, w_in, conv_w, a_log, dt_bias, g_onorm, w_out):
    b_, s_, _ = h.shape
    proj = h @ w_in
    qkv = jax.nn.silu(causal_depthwise_conv(proj[..., :QKV_W], conv_w))
    z = proj[..., QKV_W:QKV_W + V_W].reshape(b_, s_, LA_HEADS, LA_DV)
    b_in = proj[..., QKV_W + V_W:QKV_W + V_W + LA_HEADS].astype(jnp.float32)
    a_in = proj[..., QKV_W + V_W + LA_HEADS:].astype(jnp.float32)
    q = l2_norm(qkv[..., :QK_W].reshape(b_, s_, LA_HEADS, LA_DK).astype(jnp.float32))
    k = l2_norm(qkv[..., QK_W:2 * QK_W].reshape(b_, s_, LA_HEADS, LA_DK).astype(jnp.float32))
    v = qkv[..., 2 * QK_W:].reshape(b_, s_, LA_HEADS, LA_DV)
    beta = jax.nn.sigmoid(b_in)
    g = -jnp.exp(a_log.astype(jnp.float32)) * jax.nn.softplus(a_in + dt_bias.astype(jnp.float32))
    o = chunk_gated_delta_rule(q, k, v, g, beta)
    o = rms_norm(o, g_onorm) * jax.nn.silu(z.astype(jnp.float32))
    return o.reshape(b_, s_, V_W).astype(h.dtype) @ w_out


def shared_kv(h, g_kv, w_kv, g_knorm):
    b_, s_, _ = h.shape
    kv = rms_norm(h, g_kv) @ w_kv
    k = rms_norm(kv[..., :KV_W].reshape(b_, s_, SW_KV_HEADS, SW_HD), g_knorm)
    v = kv[..., KV_W:].reshape(b_, s_, SW_KV_HEADS, SW_HD)
    return k, v


def t5_bucket_np(dist):
    max_exact = N_BUCKETS // 2
    n = np.maximum(dist, 0)
    safe = np.maximum(n, 1).astype(np.float32)
    large = max_exact + (np.log(safe / max_exact) / np.log(MAX_DIST / max_exact)
                         * (N_BUCKETS - max_exact)).astype(np.int32)
    large = np.minimum(large, N_BUCKETS - 1)
    return np.where(n < max_exact, n, large).astype(np.int32)


def sliding_window_attention(h, k, v, w_q, g_qnorm, sinks, rel_bias, w_o):
    f32 = jnp.float32
    b_, s_, _ = h.shape
    nb = s_ // BLOCK
    q = rms_norm((h @ w_q).reshape(b_, s_, SW_KV_HEADS, SW_GROUP, SW_HD), g_qnorm)
    qb = q.reshape(b_, nb, BLOCK, SW_KV_HEADS, SW_GROUP, SW_HD)

    def band(t):
        tb = t.reshape(b_, nb, BLOCK, SW_KV_HEADS, SW_HD)
        prev = jnp.pad(tb[:, :-1], ((0, 0), (1, 0), (0, 0), (0, 0), (0, 0)))
        return jnp.concatenate([prev, tb], axis=2)

    kb, vb = band(k), band(v)
    scores = jnp.einsum('bnqhgd,bnkhd->bhgnqk', qb, kb).astype(f32) * (SW_HD ** -0.5)

    qi = np.arange(BLOCK)[:, None] + BLOCK
    kj = np.arange(2 * BLOCK)[None, :]
    dist = qi - kj
    band_ok = (dist >= 0) & (dist < WINDOW)
    key_pos = np.arange(nb)[:, None, None] * BLOCK - BLOCK + kj[None]
    mask = band_ok[None] & (key_pos >= 0)
    bias = rel_bias.astype(f32)[t5_bucket_np(dist)]
    bias = jnp.transpose(bias, (2, 0, 1)).reshape(SW_KV_HEADS, SW_GROUP, 1, BLOCK, 2 * BLOCK)
    scores = jnp.where(mask, scores + bias, NEG_INF)

    sink = sinks.astype(f32).reshape(SW_KV_HEADS, SW_GROUP, 1, 1, 1)
    m = jnp.maximum(jnp.max(scores, axis=-1, keepdims=True), sink)
    p = jnp.exp(scores - m)
    denom = jnp.sum(p, axis=-1, keepdims=True) + jnp.exp(sink - m)
    out = jnp.einsum('bhgnqk,bnkhd->bnqhgd', p / denom, vb.astype(f32))
    return out.reshape(b_, s_, Q_W).astype(h.dtype) @ w_o


def swiglu(h, w_gate, w_up, w_down):
    return (jax.nn.silu(h @ w_gate) * (h @ w_up)) @ w_down


def moe_swiglu(h, w_router, w_gate, w_up, w_down):
    logits = (h @ w_router).astype(jnp.float32)
    top_vals, top_idx = lax.top_k(logits, TOP_K)
    top_w = jax.nn.softmax(top_vals, axis=-1)
    gates = jnp.sum(jax.nn.one_hot(top_idx, N_EXPERTS, dtype=jnp.float32) * top_w[..., None], axis=-2)
    gates = gates.astype(h.dtype)
    y = jnp.zeros_like(h)
    for e in range(N_EXPERTS):
        y = y + gates[..., e:e + 1] * swiglu(h, w_gate[e], w_up[e], w_down[e])
    return y


def setup_inputs(seed: int = 0) -> dict:
    key = jax.random.key(seed)
    ks = iter(jax.random.split(key, 32))
    D = D_MODEL

    def nrm(shape, scale):
        return jax.random.normal(next(ks), shape, jnp.float32) * scale

    def gain(shape):
        return 1.0 + nrm(shape, 0.02)

    x = nrm((BATCH, SEQ, D), 1.0)
    a_norm = gain((N_A, D))
    a_w_in = nrm((N_A, D, A_PROJ_W), D ** -0.5)
    a_conv = nrm((N_A, CONV_W, QKV_W), CONV_W ** -0.5)
    a_log_decay = jnp.log(jax.random.uniform(next(ks), (N_A, LA_HEADS), jnp.float32, 1.0, 16.0))
    dt = jnp.exp(jax.random.uniform(next(ks), (N_A, LA_HEADS), jnp.float32,
                                    math.log(1e-3), math.log(1e-1)))
    a_dt_bias = dt + jnp.log(-jnp.expm1(-dt))
    a_out_norm = gain((N_A, LA_DV))
    a_w_out = nrm((N_A, V_W, D), V_W ** -0.5)
    kv_norm = gain((D,))
    kv_w = nrm((D, 2 * KV_W), D ** -0.5)
    k_norm = gain((SW_HD,))
    b_norm = gain((N_B, D))
    b_w_q = nrm((N_B, D, Q_W), D ** -0.5)
    q_norm = gain((N_B, SW_HD))
    b_sinks = nrm((N_B, SW_HEADS), 1.0)
    b_w_o = nrm((N_B, Q_W, D), Q_W ** -0.5)
    rel_bias = nrm((N_BUCKETS, SW_HEADS), 0.5)
    ffn_norm = gain((DEPTH, D))
    dense_w_gate = nrm((N_DENSE, D, D_FF), D ** -0.5)
    dense_w_up = nrm((N_DENSE, D, D_FF), D ** -0.5)
    dense_w_down = nrm((N_DENSE, D_FF, D), D_FF ** -0.5)
    moe_router = nrm((N_MOE, D, N_EXPERTS), D ** -0.5)
    moe_w_gate = nrm((N_MOE, N_EXPERTS, D, D_FF), D ** -0.5)
    moe_w_up = nrm((N_MOE, N_EXPERTS, D, D_FF), D ** -0.5)
    moe_w_down = nrm((N_MOE, N_EXPERTS, D_FF, D), D_FF ** -0.5)
    return {"x": x, "a_norm": a_norm, "a_w_in": a_w_in, "a_conv": a_conv,
            "a_log_decay": a_log_decay, "a_dt_bias": a_dt_bias, "a_out_norm": a_out_norm,
            "a_w_out": a_w_out, "kv_norm": kv_norm, "kv_w": kv_w, "k_norm": k_norm,
            "b_norm": b_norm, "b_w_q": b_w_q, "q_norm": q_norm, "b_sinks": b_sinks,
            "b_w_o": b_w_o, "rel_bias": rel_bias, "ffn_norm": ffn_norm,
            "dense_w_gate": dense_w_gate, "dense_w_up": dense_w_up, "dense_w_down": dense_w_down,
            "moe_router": moe_router, "moe_w_gate": moe_w_gate, "moe_w_up": moe_w_up,
            "moe_w_down": moe_w_down}


def reference(x, a_norm, a_w_in, a_conv, a_log_decay, a_dt_bias, a_out_norm, a_w_out,
              kv_norm, kv_w, k_norm, b_norm, b_w_q, q_norm, b_sinks, b_w_o, rel_bias,
              ffn_norm, dense_w_gate, dense_w_up, dense_w_down,
              moe_router, moe_w_gate, moe_w_up, moe_w_down):
    h = x
    k_sh = None
    v_sh = None
    for layer in range(DEPTH):
        if layer < N_A:
            i = layer
            h = h + gated_deltanet(rms_norm(h, a_norm[i]), a_w_in[i], a_conv[i], a_log_decay[i],
                                   a_dt_bias[i], a_out_norm[i], a_w_out[i])
        else:
            j = layer - N_A
            if j == 0:
                k_sh, v_sh = shared_kv(h, kv_norm, kv_w, k_norm)
            h = h + sliding_window_attention(rms_norm(h, b_norm[j]), k_sh, v_sh, b_w_q[j],
                                             q_norm[j], b_sinks[j], rel_bias, b_w_o[j])
        hn = rms_norm(h, ffn_norm[layer])
        if layer % 2 == 0:
            e = layer // 2
            h = h + swiglu(hn, dense_w_gate[e], dense_w_up[e], dense_w_down[e])
        else:
            e = layer // 2
            h = h + moe_swiglu(hn, moe_router[e], moe_w_gate[e], moe_w_up[e], moe_w_down[e])
    return h
```

```python
import functools

import numpy as np
import jax
import jax.numpy as jnp
from jax import lax
from jax.experimental import pallas as pl
from jax.experimental.pallas import tpu as pltpu

F32 = jnp.float32
BF16 = jnp.bfloat16

EPS = 1e-6
NEG_INF = -1e30

LA_HEADS = 8
LA_D = 128
CONV_W = 4
CHUNK = 64
SW_HEADS = 16
SW_KV_HEADS = 4
SW_GROUP = SW_HEADS // SW_KV_HEADS
SW_HD = 64
WINDOW = 128
N_BUCKETS = 32
MAX_DIST = 128
N_EXPERTS = 8

LANES = 128
GDN_BLOCK = 2 * CHUNK
HALO = 8

VMEM_LIMIT = 56 * 1024 * 1024


def _cparams(sem):
    return pltpu.CompilerParams(dimension_semantics=sem, vmem_limit_bytes=VMEM_LIMIT)


def _silu(x):
    return x * (1.0 / (1.0 + jnp.exp(-x)))


def _dot(a, b):
    return jnp.dot(a, b, preferred_element_type=F32)


def _dot_nt(a, b):
    return lax.dot_general(a, b, (((1,), (1,)), ((), ())), preferred_element_type=F32)


def _norm_matmul_kernel(x_ref, g_ref, w_ref, o_ref, xn_ref):
    @pl.when(pl.program_id(1) == 0)
    def _():
        x = x_ref[...]
        ms = jnp.mean(x * x, axis=-1, keepdims=True)
        xn_ref[...] = ((x * lax.rsqrt(ms + EPS)) * g_ref[...]).astype(BF16)

    o_ref[...] = _dot(xn_ref[...], w_ref[...]).astype(o_ref.dtype)


def norm_matmul(x, gain, w, out_dtype, tm, tn, name):
    t, d = x.shape
    n = w.shape[1]
    tm, tn = min(tm, t), min(tn, n)
    return pl.pallas_call(
        _norm_matmul_kernel,
        out_shape=jax.ShapeDtypeStruct((t, n), out_dtype),
        grid=(t // tm, n // tn),
        in_specs=[pl.BlockSpec((tm, d), lambda i, j: (i, 0)),
                  pl.BlockSpec((1, d), lambda i, j: (0, 0)),
                  pl.BlockSpec((d, tn), lambda i, j: (0, j))],
        out_specs=pl.BlockSpec((tm, tn), lambda i, j: (i, j)),
        scratch_shapes=[pltpu.VMEM((tm, d), BF16)],
        compiler_params=_cparams(("parallel", "arbitrary")),
        name=name,
    )(x, gain.reshape(1, d), w)


def _matmul_res_kernel(a_ref, w_ref, r_ref, o_ref):
    o_ref[...] = r_ref[...] + _dot(a_ref[...], w_ref[...])


def matmul_residual(a, w, res, tm, tn, name):
    t, k = a.shape
    n = w.shape[1]
    tm, tn = min(tm, t), min(tn, n)
    return pl.pallas_call(
        _matmul_res_kernel,
        out_shape=jax.ShapeDtypeStruct((t, n), F32),
        grid=(t // tm, n // tn),
        in_specs=[pl.BlockSpec((tm, k), lambda i, j: (i, 0)),
                  pl.BlockSpec((k, tn), lambda i, j: (0, j)),
                  pl.BlockSpec((tm, tn), lambda i, j: (i, j))],
        out_specs=pl.BlockSpec((tm, tn), lambda i, j: (i, j)),
        compiler_params=_cparams(("parallel", "parallel")),
        name=name,
    )(a, w, res)


def _gdn_kernel(proj_ref, gates_ref, convw_ref, hp_ref, onorm_ref, o_ref, xs_ref, state_ref):
    nh, d, c = LA_HEADS, LA_D, CHUNK
    blk = GDN_BLOCK
    qkv_w = 3 * nh * d

    @pl.when(pl.program_id(1) == 0)
    def _():
        state_ref[...] = jnp.zeros_like(state_ref)
        xs_ref[0:HALO, :] = jnp.zeros((HALO, qkv_w), F32)

    xs_ref[HALO:HALO + blk, :] = proj_ref[:, 0:qkv_w].astype(F32)

    gates = gates_ref[...]
    a_log = hp_ref[0:1, :]
    dt_bias = hp_ref[1:2, :]
    beta = 1.0 / (1.0 + jnp.exp(-gates))
    sp_in = gates + dt_bias
    softplus = jnp.maximum(sp_in, 0.0) + jnp.log(1.0 + jnp.exp(-jnp.abs(sp_in)))
    g = -jnp.exp(a_log) * softplus

    row = lax.broadcasted_iota(jnp.int32, (blk, blk), 0)
    col = lax.broadcasted_iota(jnp.int32, (blk, blk), 1)
    tri = jnp.where((row >= col) & ((row // c) == (col // c)), 1.0, 0.0).astype(BF16)
    g_hi = g.astype(BF16)
    g_r1 = g - g_hi.astype(F32)
    g_mid = g_r1.astype(BF16)
    g_lo = (g_r1 - g_mid.astype(F32)).astype(BF16)
    gc = _dot(tri, g_hi) + _dot(tri, g_mid) + _dot(tri, g_lo)
    gc_t = gc.T

    ci = lax.broadcasted_iota(jnp.int32, (c, c), 0)
    cj = lax.broadcasted_iota(jnp.int32, (c, c), 1)
    lower_incl = ci >= cj
    strict = ci > cj
    eye_c = jnp.where(ci == cj, 1.0, 0.0).astype(F32)
    di = lax.broadcasted_iota(jnp.int32, (d, d), 0)
    dj = lax.broadcasted_iota(jnp.int32, (d, d), 1)
    eye_d = jnp.where(di == dj, 1.0, 0.0).astype(BF16)

    onorm = onorm_ref[...]

    def conv_silu(col0):
        acc = None
        for j in range(CONV_W):
            r0 = HALO - (CONV_W - 1) + j
            term = convw_ref[j:j + 1, col0:col0 + d] * xs_ref[r0:r0 + blk, col0:col0 + d]
            acc = term if acc is None else acc + term
        return _silu(acc)

    for h in range(nh):
        qf = conv_silu(h * d)
        kf = conv_silu(nh * d + h * d)
        vf = conv_silu(2 * nh * d + h * d)
        qf = qf * lax.rsqrt(jnp.sum(qf * qf, axis=-1, keepdims=True) + EPS) * (d ** -0.5)
        kf = kf * lax.rsqrt(jnp.sum(kf * kf, axis=-1, keepdims=True) + EPS)
        z = proj_ref[:, qkv_w + h * d:qkv_w + (h + 1) * d].astype(F32)

        for ck in range(blk // c):
            r = ck * c
            q = qf[r:r + c]
            k = kf[r:r + c]
            v = vf[r:r + c]
            g_col = gc[r:r + c, nh + h:nh + h + 1]
            g_row = gc_t[nh + h:nh + h + 1, r:r + c]
            g_last = gc[r + c - 1:r + c, nh + h:nh + h + 1]
            b_col = beta[r:r + c, h:h + 1]

            decay = jnp.where(lower_incl, jnp.exp(jnp.where(lower_incl, g_col - g_row, 0.0)), 0.0)
            k_beta = k * b_col
            k_b = k.astype(BF16)
            lhs = jnp.concatenate([k_beta.astype(BF16), q.astype(BF16), eye_d], axis=0)
            kk = _dot_nt(lhs, k_b)
            a_mat = jnp.where(strict, kk[0:c] * decay, 0.0)
            attn = kk[c:2 * c] * decay
            k_t = kk[2 * c:2 * c + d]

            x_b = (-a_mat).astype(BF16)
            y = _dot(x_b, x_b)
            p = eye_c - a_mat
            n_levels = int(np.log2(c))
            for lvl in range(1, n_levels):
                y_b = y.astype(BF16)
                if lvl + 1 < n_levels:
                    zz = _dot(jnp.concatenate([y_b, p.astype(BF16)], axis=0), y_b)
                    y = zz[0:c]
                    p = p + zz[c:2 * c]
                else:
                    p = p + _dot(p.astype(BF16), y_b)

            e_col = jnp.exp(g_col)
            rhs = jnp.concatenate([(v * b_col).astype(BF16), (k_beta * e_col).astype(BF16)], axis=1)
            uw = _dot(p.astype(BF16), rhs)
            u = uw[:, 0:d]
            w = uw[:, d:2 * d]

            s = state_ref[h]
            ws_qs = _dot(jnp.concatenate([w.astype(BF16), (q * e_col).astype(BF16)], axis=0), s.astype(BF16))
            v_new = u - ws_qs[0:c]
            k_tail_t = k_t * jnp.exp(g_last - g_row)
            av_kv = _dot(jnp.concatenate([attn.astype(BF16), k_tail_t.astype(BF16)], axis=0), v_new.astype(BF16))
            o = ws_qs[c:2 * c] + av_kv[0:c]
            state_ref[h] = s * jnp.exp(g_last) + av_kv[c:c + d]

            o = (o * lax.rsqrt(jnp.mean(o * o, axis=-1, keepdims=True) + EPS)) * onorm
            o_ref[r:r + c, h * d:(h + 1) * d] = (o * _silu(z[r:r + c])).astype(o_ref.dtype)

    xs_ref[0:HALO, :] = xs_ref[blk:blk + HALO, :]


def gdn_core(proj, gates, conv_w, a_log, dt_bias, out_norm, batch, seq):
    t = proj.shape[0]
    nh, d = LA_HEADS, LA_D
    blk = GDN_BLOCK
    nblk = seq // blk
    hp = jnp.zeros((8, LANES), F32)
    hp = hp.at[0, nh:2 * nh].set(a_log.astype(F32)).at[1, nh:2 * nh].set(dt_bias.astype(F32))
    return pl.pallas_call(
        _gdn_kernel,
        out_shape=jax.ShapeDtypeStruct((t, nh * d), BF16),
        grid=(batch, nblk),
        in_specs=[pl.BlockSpec((blk, 4 * nh * d), lambda b, n: (b * nblk + n, 0)),
                  pl.BlockSpec((blk, LANES), lambda b, n: (b * nblk + n, 0)),
                  pl.BlockSpec((CONV_W, 3 * nh * d), lambda b, n: (0, 0)),
                  pl.BlockSpec((8, LANES), lambda b, n: (0, 0)),
                  pl.BlockSpec((1, d), lambda b, n: (0, 0))],
        out_specs=pl.BlockSpec((blk, nh * d), lambda b, n: (b * nblk + n, 0)),
        scratch_shapes=[pltpu.VMEM((HALO + blk, 3 * nh * d), F32),
                        pltpu.VMEM((nh, d, d), F32)],
        compiler_params=_cparams(("parallel", "arbitrary")),
        name="gdn_core",
    )(proj, gates, conv_w.astype(F32), hp, out_norm.reshape(1, d).astype(F32))


def _ffn_kernel(x_ref, g_ref, wg_ref, wu_ref, wd_ref, o_ref, xn_ref, acc_ref):
    j = pl.program_id(1)

    @pl.when(j == 0)
    def _():
        x = x_ref[...]
        ms = jnp.mean(x * x, axis=-1, keepdims=True)
        xn_ref[...] = ((x * lax.rsqrt(ms + EPS)) * g_ref[...]).astype(BF16)
        acc_ref[...] = jnp.zeros_like(acc_ref)

    xn = xn_ref[...]
    hid = _silu(_dot(xn, wg_ref[...])) * _dot(xn, wu_ref[...])
    acc_ref[...] += _dot(hid.astype(BF16), wd_ref[...])

    @pl.when(j == pl.num_programs(1) - 1)
    def _():
        o_ref[...] = x_ref[...] + acc_ref[...]


def ffn_dense(x, gain, wg, wu, wd, tm, tf):
    t, d = x.shape
    f = wg.shape[1]
    tm = min(tm, t)
    return pl.pallas_call(
        _ffn_kernel,
        out_shape=jax.ShapeDtypeStruct((t, d), F32),
        grid=(t // tm, f // tf),
        in_specs=[pl.BlockSpec((tm, d), lambda i, j: (i, 0)),
                  pl.BlockSpec((1, d), lambda i, j: (0, 0)),
                  pl.BlockSpec((d, tf), lambda i, j: (0, j)),
                  pl.BlockSpec((d, tf), lambda i, j: (0, j)),
                  pl.BlockSpec((tf, d), lambda i, j: (j, 0))],
        out_specs=pl.BlockSpec((tm, d), lambda i, j: (i, 0)),
        scratch_shapes=[pltpu.VMEM((tm, d), BF16), pltpu.VMEM((tm, d), F32)],
        compiler_params=_cparams(("parallel", "arbitrary")),
        name="ffn_dense",
    )(x, gain.reshape(1, d), wg, wu, wd)


def _t5_bucket_np(dist):
    max_exact = N_BUCKETS // 2
    n = np.maximum(dist, 0)
    safe = np.maximum(n, 1).astype(np.float32)
    large = max_exact + (np.log(safe / max_exact) / np.log(MAX_DIST / max_exact)
                         * (N_BUCKETS - max_exact)).astype(np.int32)
    large = np.minimum(large, N_BUCKETS - 1)
    return np.where(n < max_exact, n, large).astype(np.int32)


def _bias_kernel(bucket_ref, rb_ref, o_ref):
    bucket = bucket_ref[...]
    for h in range(SW_HEADS):
        acc = jnp.zeros(bucket.shape, F32)
        for b in range(N_BUCKETS):
            acc = jnp.where(bucket == b, rb_ref[b, h], acc)
        o_ref[h] = acc


def bias_table(rel_bias):
    qi = np.arange(WINDOW)[:, None] + WINDOW
    kj = np.arange(2 * WINDOW)[None, :]
    bucket = jnp.asarray(_t5_bucket_np(qi - kj))
    return pl.pallas_call(
        _bias_kernel,
        out_shape=jax.ShapeDtypeStruct((SW_HEADS, WINDOW, 2 * WINDOW), F32),
        in_specs=[pl.BlockSpec(memory_space=pltpu.VMEM), pl.BlockSpec(memory_space=pltpu.SMEM)],
        out_specs=pl.BlockSpec(memory_space=pltpu.VMEM),
        name="t5_bias_table",
    )(bucket, rel_bias.astype(F32))


def _swa_kernel(q_ref, kvp_ref, kvc_ref, bias_ref, qn_ref, kn_ref, sink_ref, o_ref):
    n = pl.program_id(1)
    blk, hd = WINDOW, SW_HD
    kv_w = SW_KV_HEADS * hd
    qi = lax.broadcasted_iota(jnp.int32, (blk, 2 * blk), 0) + blk
    kj = lax.broadcasted_iota(jnp.int32, (blk, 2 * blk), 1)
    dist = qi - kj
    first_key = jnp.where(n > 0, 0, blk)
    mask = (dist >= 0) & (dist < WINDOW) & (kj >= first_key)
    qn = qn_ref[...]
    kn = kn_ref[...]

    def head_norm(x, gain):
        return (x * lax.rsqrt(jnp.mean(x * x, axis=-1, keepdims=True) + EPS)) * gain

    for g in range(SW_KV_HEADS):
        kp = kvp_ref[:, g * hd:(g + 1) * hd].astype(F32)
        kc = kvc_ref[:, g * hd:(g + 1) * hd].astype(F32)
        k = jnp.concatenate([head_norm(kp, kn), head_norm(kc, kn)], axis=0).astype(BF16)
        v = jnp.concatenate([kvp_ref[:, kv_w + g * hd:kv_w + (g + 1) * hd],
                             kvc_ref[:, kv_w + g * hd:kv_w + (g + 1) * hd]], axis=0)
        for m in range(SW_GROUP):
            hq = g * SW_GROUP + m
            q = head_norm(q_ref[:, hq * hd:(hq + 1) * hd].astype(F32), qn) * (hd ** -0.5)
            s = _dot_nt(q.astype(BF16), k) + bias_ref[hq]
            s = jnp.where(mask, s, NEG_INF)
            sink = sink_ref[hq]
            mx = jnp.maximum(jnp.max(s, axis=-1, keepdims=True), sink)
            p = jnp.exp(s - mx)
            denom = jnp.sum(p, axis=-1, keepdims=True) + jnp.exp(sink - mx)
            o = _dot((p / denom).astype(BF16), v)
            o_ref[:, hq * hd:(hq + 1) * hd] = o.astype(o_ref.dtype)


def swa_attention(q, kv, bias, q_norm, k_norm, sinks, batch, seq):
    t = q.shape[0]
    blk = WINDOW
    nb = seq // blk
    qw = SW_HEADS * SW_HD
    kvw = 2 * SW_KV_HEADS * SW_HD
    return pl.pallas_call(
        _swa_kernel,
        out_shape=jax.ShapeDtypeStruct((t, qw), BF16),
        grid=(batch, nb),
        in_specs=[pl.BlockSpec((blk, qw), lambda b, n: (b * nb + n, 0)),
                  pl.BlockSpec((blk, kvw), lambda b, n: (b * nb + jnp.maximum(n - 1, 0), 0)),
                  pl.BlockSpec((blk, kvw), lambda b, n: (b * nb + n, 0)),
                  pl.BlockSpec((SW_HEADS, blk, 2 * blk), lambda b, n: (0, 0, 0)),
                  pl.BlockSpec((1, SW_HD), lambda b, n: (0, 0)),
                  pl.BlockSpec((1, SW_HD), lambda b, n: (0, 0)),
                  pl.BlockSpec(memory_space=pltpu.SMEM)],
        out_specs=pl.BlockSpec((blk, qw), lambda b, n: (b * nb + n, 0)),
        compiler_params=_cparams(("parallel", "parallel")),
        name="swa_attention",
    )(q, kv, kv, bias, q_norm.reshape(1, SW_HD).astype(F32), k_norm.reshape(1, SW_HD).astype(F32),
      sinks.astype(F32))


def _router_kernel(x_ref, g_ref, wr_ref, xn_ref, gates_ref):
    x = x_ref[...]
    ms = jnp.mean(x * x, axis=-1, keepdims=True)
    xn32 = (x * lax.rsqrt(ms + EPS)) * g_ref[...]
    xn = xn32.astype(BF16)
    xn_ref[...] = xn
    xn_lo = (xn32 - xn.astype(F32)).astype(BF16)
    p_hi = _dot(xn, wr_ref[...])
    p_lo = _dot(xn_lo, wr_ref[...])
    logits = p_hi + pltpu.roll(p_hi, LANES - N_EXPERTS, 1) + p_lo
    lane = lax.broadcasted_iota(jnp.int32, logits.shape, 1).astype(F32)
    valid = lane < N_EXPERTS
    big = float(LANES)
    l1 = jnp.where(valid, logits, -jnp.inf)
    m1 = jnp.max(l1, axis=-1, keepdims=True)
    i1 = jnp.min(jnp.where(l1 == m1, lane, big), axis=-1, keepdims=True)
    l2 = jnp.where(lane == i1, -jnp.inf, l1)
    m2 = jnp.max(l2, axis=-1, keepdims=True)
    i2 = jnp.min(jnp.where(l2 == m2, lane, big), axis=-1, keepdims=True)
    e2 = jnp.exp(m2 - m1)
    w1 = 1.0 / (1.0 + e2)
    w2 = e2 / (1.0 + e2)
    gates_ref[...] = jnp.where(lane == i1, w1, jnp.where(lane == i2, w2, 0.0))


def router(x, gain, w_router, tm):
    t, d = x.shape
    e = w_router.shape[1]
    w_hi = w_router.astype(BF16)
    w_lo = (w_router - w_hi.astype(F32)).astype(BF16)
    wr = jnp.zeros((d, LANES), BF16).at[:, 0:e].set(w_hi).at[:, e:2 * e].set(w_lo)
    tm = min(tm, t)
    return pl.pallas_call(
        _router_kernel,
        out_shape=(jax.ShapeDtypeStruct((t, d), BF16), jax.ShapeDtypeStruct((t, LANES), F32)),
        grid=(t // tm,),
        in_specs=[pl.BlockSpec((tm, d), lambda i: (i, 0)),
                  pl.BlockSpec((1, d), lambda i: (0, 0)),
                  pl.BlockSpec((d, LANES), lambda i: (0, 0))],
        out_specs=(pl.BlockSpec((tm, d), lambda i: (i, 0)), pl.BlockSpec((tm, LANES), lambda i: (i, 0))),
        compiler_params=_cparams(("parallel",)),
        name="router",
    )(x, gain.reshape(1, d), wr)


def _moe_dense_kernel(x_ref, xn_ref, gates_ref, wg_ref, wu_ref, wd_ref, o_ref, acc_ref):
    e = pl.program_id(1)
    j = pl.program_id(2)

    @pl.when((e == 0) & (j == 0))
    def _():
        acc_ref[...] = jnp.zeros_like(acc_ref)

    xn = xn_ref[...]
    gates = gates_ref[...]
    lane = lax.broadcasted_iota(jnp.int32, gates.shape, 1)
    ge = jnp.sum(jnp.where(lane == e, gates, 0.0), axis=-1, keepdims=True)
    hid = _silu(_dot(xn, wg_ref[0])) * _dot(xn, wu_ref[0])
    acc_ref[...] += ge * _dot(hid.astype(BF16), wd_ref[0])

    @pl.when((e == pl.num_programs(1) - 1) & (j == pl.num_programs(2) - 1))
    def _():
        o_ref[...] = x_ref[...] + acc_ref[...]


def moe_dense(x, xn, gates, wg, wu, wd, tm, tf):
    t, d = x.shape
    ne, _, f = wg.shape
    tm = min(tm, t)
    return pl.pallas_call(
        _moe_dense_kernel,
        out_shape=jax.ShapeDtypeStruct((t, d), F32),
        grid=(t // tm, ne, f // tf),
        in_specs=[pl.BlockSpec((tm, d), lambda i, e, j: (i, 0)),
                  pl.BlockSpec((tm, d), lambda i, e, j: (i, 0)),
                  pl.BlockSpec((tm, LANES), lambda i, e, j: (i, 0)),
                  pl.BlockSpec((1, d, tf), lambda i, e, j: (e, 0, j)),
                  pl.BlockSpec((1, d, tf), lambda i, e, j: (e, 0, j)),
                  pl.BlockSpec((1, tf, d), lambda i, e, j: (e, j, 0))],
        out_specs=pl.BlockSpec((tm, d), lambda i, e, j: (i, 0)),
        scratch_shapes=[pltpu.VMEM((tm, d), F32)],
        compiler_params=_cparams(("parallel", "arbitrary", "arbitrary")),
        name="moe_dense",
    )(x, xn, gates, wg, wu, wd)


def moe_layer(h, gain, w_router, wg, wu, wd):
    xn, gates = router(h, gain, w_router, 1024)
    return moe_dense(h, xn, gates, wg.astype(BF16), wu.astype(BF16), wd.astype(BF16), 1024, 512)


def kernel(x, a_norm, a_w_in, a_conv, a_log_decay, a_dt_bias, a_out_norm, a_w_out, kv_norm, kv_w, k_norm,
           b_norm, b_w_q, q_norm, b_sinks, b_w_o, rel_bias, ffn_norm, dense_w_gate, dense_w_up, dense_w_down,
           moe_router, moe_w_gate, moe_w_up, moe_w_down):
    batch, seq, d = x.shape
    t = batch * seq
    nh, hd = LA_HEADS, LA_D
    main_w = 4 * nh * hd
    h0 = x.reshape(t, d)

    w_in = a_w_in[0]
    w_main = w_in[:, 0:main_w].astype(BF16)
    w_gate = jnp.zeros((d, LANES), BF16).at[:, 0:2 * nh].set(w_in[:, main_w:main_w + 2 * nh].astype(BF16))
    proj = norm_matmul(h0, a_norm[0], w_main, BF16, 1024, 1024, "gdn_in_proj")
    gates = norm_matmul(h0, a_norm[0], w_gate, F32, 1024, LANES, "gdn_gate_proj")
    o = gdn_core(proj, gates, a_conv[0], a_log_decay[0], a_dt_bias[0], a_out_norm[0], batch, seq)
    h1 = matmul_residual(o, a_w_out[0].astype(BF16), h0, 1024, 1024, "gdn_out_proj")

    h2 = ffn_dense(h1, ffn_norm[0], dense_w_gate[0].astype(BF16), dense_w_up[0].astype(BF16),
                   dense_w_down[0].astype(BF16), 1024, 512)

    kv = norm_matmul(h2, kv_norm, kv_w.astype(BF16), BF16, 1024, 512, "kv_proj")
    q = norm_matmul(h2, b_norm[0], b_w_q[0].astype(BF16), BF16, 1024, 1024, "q_proj")
    bias = bias_table(rel_bias)
    attn = swa_attention(q, kv, bias, q_norm[0], k_norm, b_sinks[0], batch, seq)
    h3 = matmul_residual(attn, b_w_o[0].astype(BF16), h2, 1024, 1024, "attn_out_proj")

    h4 = moe_layer(h3, ffn_norm[1], moe_router[0], moe_w_gate[0], moe_w_up[0], moe_w_down[0])
    return h4.reshape(batch, seq, d)
```

```python
import functools

import numpy as np
import jax
import jax.numpy as jnp
from jax import lax
from jax.experimental import pallas as pl
from jax.experimental.pallas import tpu as pltpu

F32 = jnp.float32
BF16 = jnp.bfloat16

EPS = 1e-6
NEG_INF = -1e30

LA_HEADS = 8
LA_D = 128
CONV_W = 4
CHUNK = 64
SW_HEADS = 16
SW_KV_HEADS = 4
SW_GROUP = SW_HEADS // SW_KV_HEADS
SW_HD = 64
WINDOW = 128
N_BUCKETS = 32
MAX_DIST = 128
N_EXPERTS = 8

LANES = 128
GDN_BLOCK = 2 * CHUNK
HALO = 8

VMEM_LIMIT = 56 * 1024 * 1024


def _cparams(sem):
    return pltpu.CompilerParams(dimension_semantics=sem, vmem_limit_bytes=VMEM_LIMIT)


def _silu(x):
    return x * (1.0 / (1.0 + jnp.exp(-x)))


def _dot(a, b):
    return jnp.dot(a, b, preferred_element_type=F32)


def _dot_nt(a, b):
    return lax.dot_general(a, b, (((1,), (1,)), ((), ())), preferred_element_type=F32)


def _norm_matmul_kernel(x_ref, g_ref, w_ref, o_ref, xn_ref):
    @pl.when(pl.program_id(1) == 0)
    def _():
        x = x_ref[...]
        ms = jnp.mean(x * x, axis=-1, keepdims=True)
        xn_ref[...] = ((x * lax.rsqrt(ms + EPS)) * g_ref[...]).astype(BF16)

    o_ref[...] = _dot(xn_ref[...], w_ref[...]).astype(o_ref.dtype)


def norm_matmul(x, gain, w, out_dtype, tm, tn, name):
    t, d = x.shape
    n = w.shape[1]
    tm, tn = min(tm, t), min(tn, n)
    return pl.pallas_call(
        _norm_matmul_kernel,
        out_shape=jax.ShapeDtypeStruct((t, n), out_dtype),
        grid=(t // tm, n // tn),
        in_specs=[pl.BlockSpec((tm, d), lambda i, j: (i, 0)),
                  pl.BlockSpec((1, d), lambda i, j: (0, 0)),
                  pl.BlockSpec((d, tn), lambda i, j: (0, j))],
        out_specs=pl.BlockSpec((tm, tn), lambda i, j: (i, j)),
        scratch_shapes=[pltpu.VMEM((tm, d), BF16)],
        compiler_params=_cparams(("parallel", "arbitrary")),
        name=name,
    )(x, gain.reshape(1, d), w)


def _matmul_res_kernel(a_ref, w_ref, r_ref, o_ref):
    o_ref[...] = r_ref[...] + _dot(a_ref[...], w_ref[...])


def matmul_residual(a, w, res, tm, tn, name):
    t, k = a.shape
    n = w.shape[1]
    tm, tn = min(tm, t), min(tn, n)
    return pl.pallas_call(
        _matmul_res_kernel,
        out_shape=jax.ShapeDtypeStruct((t, n), F32),
        grid=(t // tm, n // tn),
        in_specs=[pl.BlockSpec((tm, k), lambda i, j: (i, 0)),
                  pl.BlockSpec((k, tn), lambda i, j: (0, j)),
                  pl.BlockSpec((tm, tn), lambda i, j: (i, j))],
        out_specs=pl.BlockSpec((tm, tn), lambda i, j: (i, j)),
        compiler_params=_cparams(("parallel", "parallel")),
        name=name,
    )(a, w, res)


def _gdn_kernel(proj_ref, gates_ref, convw_ref, hp_ref, onorm_ref, o_ref, xs_ref, state_ref):
    nh, d, c = LA_HEADS, LA_D, CHUNK
    blk = GDN_BLOCK
    qkv_w = 3 * nh * d

    @pl.when(pl.program_id(1) == 0)
    def _():
        state_ref[...] = jnp.zeros_like(state_ref)
        xs_ref[0:HALO, :] = jnp.zeros((HALO, qkv_w), F32)

    xs_ref[HALO:HALO + blk, :] = proj_ref[:, 0:qkv_w].astype(F32)

    gates = gates_ref[...]
    a_log = hp_ref[0:1, :]
    dt_bias = hp_ref[1:2, :]
    beta = 1.0 / (1.0 + jnp.exp(-gates))
    sp_in = gates + dt_bias
    softplus = jnp.maximum(sp_in, 0.0) + jnp.log(1.0 + jnp.exp(-jnp.abs(sp_in)))
    g = -jnp.exp(a_log) * softplus

    row = lax.broadcasted_iota(jnp.int32, (blk, blk), 0)
    col = lax.broadcasted_iota(jnp.int32, (blk, blk), 1)
    tri = jnp.where((row >= col) & ((row // c) == (col // c)), 1.0, 0.0).astype(BF16)
    g_hi = g.astype(BF16)
    g_r1 = g - g_hi.astype(F32)
    g_mid = g_r1.astype(BF16)
    g_lo = (g_r1 - g_mid.astype(F32)).astype(BF16)
    gc = _dot(tri, g_hi) + _dot(tri, g_mid) + _dot(tri, g_lo)
    gc_t = gc.T

    ci = lax.broadcasted_iota(jnp.int32, (c, c), 0)
    cj = lax.broadcasted_iota(jnp.int32, (c, c), 1)
    lower_incl = ci >= cj
    strict = ci > cj
    eye_c = jnp.where(ci == cj, 1.0, 0.0).astype(F32)
    di = lax.broadcasted_iota(jnp.int32, (d, d), 0)
    dj = lax.broadcasted_iota(jnp.int32, (d, d), 1)
    eye_d = jnp.where(di == dj, 1.0, 0.0).astype(BF16)

    onorm = onorm_ref[...]

    def conv_silu(col0):
        acc = None
        for j in range(CONV_W):
            r0 = HALO - (CONV_W - 1) + j
            term = convw_ref[j:j + 1, col0:col0 + d] * xs_ref[r0:r0 + blk, col0:col0 + d]
            acc = term if acc is None else acc + term
        return _silu(acc)

    for h in range(nh):
        qf = conv_silu(h * d)
        kf = conv_silu(nh * d + h * d)
        vf = conv_silu(2 * nh * d + h * d)
        qf = qf * lax.rsqrt(jnp.sum(qf * qf, axis=-1, keepdims=True) + EPS) * (d ** -0.5)
        kf = kf * lax.rsqrt(jnp.sum(kf * kf, axis=-1, keepdims=True) + EPS)
        z = proj_ref[:, qkv_w + h * d:qkv_w + (h + 1) * d].astype(F32)

        for ck in range(blk // c):
            r = ck * c
            q = qf[r:r + c]
            k = kf[r:r + c]
            v = vf[r:r + c]
            g_col = gc[r:r + c, nh + h:nh + h + 1]
            g_row = gc_t[nh + h:nh + h + 1, r:r + c]
            g_last = gc[r + c - 1:r + c, nh + h:nh + h + 1]
            b_col = beta[r:r + c, h:h + 1]

            decay = jnp.where(lower_incl, jnp.exp(jnp.where(lower_incl, g_col - g_row, 0.0)), 0.0)
            k_beta = k * b_col
            k_b = k.astype(BF16)
            lhs = jnp.concatenate([k_beta.astype(BF16), q.astype(BF16), eye_d], axis=0)
            kk = _dot_nt(lhs, k_b)
            a_mat = jnp.where(strict, kk[0:c] * decay, 0.0)
            attn = kk[c:2 * c] * decay
            k_t = kk[2 * c:2 * c + d]

            x_b = (-a_mat).astype(BF16)
            y = _dot(x_b, x_b)
            p = eye_c - a_mat
            n_levels = int(np.log2(c))
            for lvl in range(1, n_levels):
                y_b = y.astype(BF16)
                if lvl + 1 < n_levels:
                    zz = _dot(jnp.concatenate([y_b, p.astype(BF16)], axis=0), y_b)
                    y = zz[0:c]
                    p = p + zz[c:2 * c]
                else:
                    p = p + _dot(p.astype(BF16), y_b)

            e_col = jnp.exp(g_col)
            rhs = jnp.concatenate([(v * b_col).astype(BF16), (k_beta * e_col).astype(BF16)], axis=1)
            uw = _dot(p.astype(BF16), rhs)
            u = uw[:, 0:d]
            w = uw[:, d:2 * d]

            s = state_ref[h]
            ws_qs = _dot(jnp.concatenate([w.astype(BF16), (q * e_col).astype(BF16)], axis=0), s.astype(BF16))
            v_new = u - ws_qs[0:c]
            k_tail_t = k_t * jnp.exp(g_last - g_row)
            av_kv = _dot(jnp.concatenate([attn.astype(BF16), k_tail_t.astype(BF16)], axis=0), v_new.astype(BF16))
            o = ws_qs[c:2 * c] + av_kv[0:c]
            state_ref[h] = s * jnp.exp(g_last) + av_kv[c:c + d]

            o = (o * lax.rsqrt(jnp.mean(o * o, axis=-1, keepdims=True) + EPS)) * onorm
            o_ref[r:r + c, h * d:(h + 1) * d] = (o * _silu(z[r:r + c])).astype(o_ref.dtype)

    xs_ref[0:HALO, :] = xs_ref[blk:blk + HALO, :]


def gdn_core(proj, gates, conv_w, a_log, dt_bias, out_norm, batch, seq):
    t = proj.shape[0]
    nh, d = LA_HEADS, LA_D
    blk = GDN_BLOCK
    nblk = seq // blk
    hp = jnp.zeros((8, LANES), F32)
    hp = hp.at[0, nh:2 * nh].set(a_log.astype(F32)).at[1, nh:2 * nh].set(dt_bias.astype(F32))
    return pl.pallas_call(
        _gdn_kernel,
        out_shape=jax.ShapeDtypeStruct((t, nh * d), BF16),
        grid=(batch, nblk),
        in_specs=[pl.BlockSpec((blk, 4 * nh * d), lambda b, n: (b * nblk + n, 0)),
                  pl.BlockSpec((blk, LANES), lambda b, n: (b * nblk + n, 0)),
                  pl.BlockSpec((CONV_W, 3 * nh * d), lambda b, n: (0, 0)),
                  pl.BlockSpec((8, LANES), lambda b, n: (0, 0)),
                  pl.BlockSpec((1, d), lambda b, n: (0, 0))],
        out_specs=pl.BlockSpec((blk, nh * d), lambda b, n: (b * nblk + n, 0)),
        scratch_shapes=[pltpu.VMEM((HALO + blk, 3 * nh * d), F32),
                        pltpu.VMEM((nh, d, d), F32)],
        compiler_params=_cparams(("parallel", "arbitrary")),
        name="gdn_core",
    )(proj, gates, conv_w.astype(F32), hp, out_norm.reshape(1, d).astype(F32))


def _ffn_kernel(x_ref, g_ref, wg_ref, wu_ref, wd_ref, o_ref, xn_ref, acc_ref):
    j = pl.program_id(1)

    @pl.when(j == 0)
    def _():
        x = x_ref[...]
        ms = jnp.mean(x * x, axis=-1, keepdims=True)
        xn_ref[...] = ((x * lax.rsqrt(ms + EPS)) * g_ref[...]).astype(BF16)
        acc_ref[...] = jnp.zeros_like(acc_ref)

    xn = xn_ref[...]
    hid = _silu(_dot(xn, wg_ref[...])) * _dot(xn, wu_ref[...])
    acc_ref[...] += _dot(hid.astype(BF16), wd_ref[...])

    @pl.when(j == pl.num_programs(1) - 1)
    def _():
        o_ref[...] = x_ref[...] + acc_ref[...]


def ffn_dense(x, gain, wg, wu, wd, tm, tf):
    t, d = x.shape
    f = wg.shape[1]
    tm = min(tm, t)
    return pl.pallas_call(
        _ffn_kernel,
        out_shape=jax.ShapeDtypeStruct((t, d), F32),
        grid=(t // tm, f // tf),
        in_specs=[pl.BlockSpec((tm, d), lambda i, j: (i, 0)),
                  pl.BlockSpec((1, d), lambda i, j: (0, 0)),
                  pl.BlockSpec((d, tf), lambda i, j: (0, j)),
                  pl.BlockSpec((d, tf), lambda i, j: (0, j)),
                  pl.BlockSpec((tf, d), lambda i, j: (j, 0))],
        out_specs=pl.BlockSpec((tm, d), lambda i, j: (i, 0)),
        scratch_shapes=[pltpu.VMEM((tm, d), BF16), pltpu.VMEM((tm, d), F32)],
        compiler_params=_cparams(("parallel", "arbitrary")),
        name="ffn_dense",
    )(x, gain.reshape(1, d), wg, wu, wd)


def _t5_bucket_np(dist):
    max_exact = N_BUCKETS // 2
    n = np.maximum(dist, 0)
    safe = np.maximum(n, 1).astype(np.float32)
    large = max_exact + (np.log(safe / max_exact) / np.log(MAX_DIST / max_exact)
                         * (N_BUCKETS - max_exact)).astype(np.int32)
    large = np.minimum(large, N_BUCKETS - 1)
    return np.where(n < max_exact, n, large).astype(np.int32)


def _bias_kernel(bucket_ref, rb_ref, o_ref):
    bucket = bucket_ref[...]
    for h in range(SW_HEADS):
        acc = jnp.zeros(bucket.shape, F32)
        for b in range(N_BUCKETS):
            acc = jnp.where(bucket == b, rb_ref[b, h], acc)
        o_ref[h] = acc


def bias_table(rel_bias):
    qi = np.arange(WINDOW)[:, None] + WINDOW
    kj = np.arange(2 * WINDOW)[None, :]
    bucket = jnp.asarray(_t5_bucket_np(qi - kj))
    return pl.pallas_call(
        _bias_kernel,
        out_shape=jax.ShapeDtypeStruct((SW_HEADS, WINDOW, 2 * WINDOW), F32),
        in_specs=[pl.BlockSpec(memory_space=pltpu.VMEM), pl.BlockSpec(memory_space=pltpu.SMEM)],
        out_specs=pl.BlockSpec(memory_space=pltpu.VMEM),
        name="t5_bias_table",
    )(bucket, rel_bias.astype(F32))


def _swa_kernel(q_ref, kvp_ref, kvc_ref, bias_ref, qn_ref, kn_ref, sink_ref, o_ref):
    n = pl.program_id(1)
    blk, hd = WINDOW, SW_HD
    kv_w = SW_KV_HEADS * hd
    qi = lax.broadcasted_iota(jnp.int32, (blk, 2 * blk), 0) + blk
    kj = lax.broadcasted_iota(jnp.int32, (blk, 2 * blk), 1)
    dist = qi - kj
    first_key = jnp.where(n > 0, 0, blk)
    mask = (dist >= 0) & (dist < WINDOW) & (kj >= first_key)
    qn = qn_ref[...]
    kn = kn_ref[...]

    def head_norm(x, gain):
        return (x * lax.rsqrt(jnp.mean(x * x, axis=-1, keepdims=True) + EPS)) * gain

    for g in range(SW_KV_HEADS):
        kp = kvp_ref[:, g * hd:(g + 1) * hd].astype(F32)
        kc = kvc_ref[:, g * hd:(g + 1) * hd].astype(F32)
        k = jnp.concatenate([head_norm(kp, kn), head_norm(kc, kn)], axis=0).astype(BF16)
        v = jnp.concatenate([kvp_ref[:, kv_w + g * hd:kv_w + (g + 1) * hd],
                             kvc_ref[:, kv_w + g * hd:kv_w + (g + 1) * hd]], axis=0)
        for m in range(SW_GROUP):
            hq = g * SW_GROUP + m
            q = head_norm(q_ref[:, hq * hd:(hq + 1) * hd].astype(F32), qn) * (hd ** -0.5)
            s = _dot_nt(q.astype(BF16), k) + bias_ref[hq]
            s = jnp.where(mask, s, NEG_INF)
            sink = sink_ref[hq]
            mx = jnp.maximum(jnp.max(s, axis=-1, keepdims=True), sink)
            p = jnp.exp(s - mx)
            denom = jnp.sum(p, axis=-1, keepdims=True) + jnp.exp(sink - mx)
            o = _dot((p / denom).astype(BF16), v)
            o_ref[:, hq * hd:(hq + 1) * hd] = o.astype(o_ref.dtype)


def swa_attention(q, kv, bias, q_norm, k_norm, sinks, batch, seq):
    t = q.shape[0]
    blk = WINDOW
    nb = seq // blk
    qw = SW_HEADS * SW_HD
    kvw = 2 * SW_KV_HEADS * SW_HD
    return pl.pallas_call(
        _swa_kernel,
        out_shape=jax.ShapeDtypeStruct((t, qw), BF16),
        grid=(batch, nb),
        in_specs=[pl.BlockSpec((blk, qw), lambda b, n: (b * nb + n, 0)),
                  pl.BlockSpec((blk, kvw), lambda b, n: (b * nb + jnp.maximum(n - 1, 0), 0)),
                  pl.BlockSpec((blk, kvw), lambda b, n: (b * nb + n, 0)),
                  pl.BlockSpec((SW_HEADS, blk, 2 * blk), lambda b, n: (0, 0, 0)),
                  pl.BlockSpec((1, SW_HD), lambda b, n: (0, 0)),
                  pl.BlockSpec((1, SW_HD), lambda b, n: (0, 0)),
                  pl.BlockSpec(memory_space=pltpu.SMEM)],
        out_specs=pl.BlockSpec((blk, qw), lambda b, n: (b * nb + n, 0)),
        compiler_params=_cparams(("parallel", "parallel")),
        name="swa_attention",
    )(q, kv, kv, bias, q_norm.reshape(1, SW_HD).astype(F32), k_norm.reshape(1, SW_HD).astype(F32),
      sinks.astype(F32))


def _route_kernel(x_ref, g_ref, wr_ref, r_ref, wt_ref, cnt_ref, sel_s, gw_s, cnt_s, start_s, run_s, *, tile_rows):
    ne = N_EXPERTS
    p = pl.program_id(0)
    i = pl.program_id(1)
    tm = x_ref.shape[0]
    sub = lax.broadcasted_iota(jnp.int32, (ne, tm), 0).astype(F32)

    @pl.when(p == 0)
    def _():
        @pl.when(i == 0)
        def _():
            cnt_s[...] = jnp.zeros_like(cnt_s)

        x = x_ref[...]
        ms = jnp.mean(x * x, axis=-1, keepdims=True)
        xn32 = (x * lax.rsqrt(ms + EPS)) * g_ref[...]
        xn_hi = xn32.astype(BF16)
        xn_lo = (xn32 - xn_hi.astype(F32)).astype(BF16)
        p_hi = _dot_nt(wr_ref[...], xn_hi)
        p_lo = _dot_nt(wr_ref[...], xn_lo)
        logits = p_hi[0:ne] + p_hi[ne:2 * ne] + p_lo[0:ne]
        m1 = jnp.max(logits, axis=0, keepdims=True)
        i1 = jnp.min(jnp.where(logits == m1, sub, float(ne)), axis=0, keepdims=True)
        l2 = jnp.where(sub == i1, -jnp.inf, logits)
        m2 = jnp.max(l2, axis=0, keepdims=True)
        i2 = jnp.min(jnp.where(l2 == m2, sub, float(ne)), axis=0, keepdims=True)
        e2 = jnp.exp(m2 - m1)
        w1 = 1.0 / (1.0 + e2)
        w2 = e2 / (1.0 + e2)
        sel = jnp.where((sub == i1) | (sub == i2), 1.0, 0.0)
        sel_s[i] = sel
        gw_s[i] = jnp.where(sub == i1, w1, jnp.where(sub == i2, w2, 0.0))
        cnt_s[...] += jnp.sum(sel, axis=1, keepdims=True)

    @pl.when(p == 1)
    def _():
        @pl.when(i == 0)
        def _():
            cnt = cnt_s[...]
            padded = jnp.floor((cnt + (tile_rows - 1)) * (1.0 / tile_rows)) * tile_rows
            sub8 = lax.broadcasted_iota(jnp.int32, cnt.shape, 0)
            start = jnp.zeros_like(cnt)
            for e in range(ne - 1):
                start = start + jnp.where(sub8 > e, padded[e:e + 1, :], 0.0)
            start_s[...] = start
            run_s[...] = jnp.zeros_like(run_s)
            cnt_ref[...] = cnt

        sel = sel_s[i]
        gw = gw_s[i]
        ti = lax.broadcasted_iota(jnp.int32, (tm, tm), 0)
        tj = lax.broadcasted_iota(jnp.int32, (tm, tm), 1)
        tri = jnp.where(ti <= tj, 1.0, 0.0).astype(BF16)
        csum = _dot(sel.astype(BF16), tri)
        slot = start_s[:, 0:1] + run_s[:, 0:1] + csum - sel
        run_s[...] += csum[:, tm - 1:tm]
        ia = jnp.min(jnp.where(sel > 0.0, sub, float(ne)), axis=0, keepdims=True)
        ib = jnp.max(jnp.where(sel > 0.0, sub, -1.0), axis=0, keepdims=True)
        pick_a = sub == ia
        pick_b = sub == ib
        rows = [jnp.sum(jnp.where(pick_a, slot, 0.0), axis=0, keepdims=True),
                jnp.sum(jnp.where(pick_b, slot, 0.0), axis=0, keepdims=True),
                jnp.sum(jnp.where(pick_a, gw, 0.0), axis=0, keepdims=True),
                jnp.sum(jnp.where(pick_b, gw, 0.0), axis=0, keepdims=True)]
        r_ref[...] = jnp.concatenate(rows + [jnp.zeros((ne - 4, tm), F32)], axis=0)
        wpad = jnp.concatenate(rows[2:4] + [jnp.zeros((LANES - 2, tm), F32)], axis=0)
        wt_ref[...] = wpad.T


def moe_route(x, gain, w_router, tm, tile_rows):
    t, d = x.shape
    ne = w_router.shape[1]
    assert ne == N_EXPERTS
    w_hi = w_router.astype(BF16)
    w_lo = (w_router - w_hi.astype(F32)).astype(BF16)
    wr = jnp.concatenate([w_hi.T, w_lo.T], axis=0)
    tm = min(tm, t)
    nt = t // tm
    return pl.pallas_call(
        functools.partial(_route_kernel, tile_rows=tile_rows),
        out_shape=(jax.ShapeDtypeStruct((ne, t), F32), jax.ShapeDtypeStruct((t, LANES), F32),
                   jax.ShapeDtypeStruct((ne, LANES), F32)),
        grid=(2, nt),
        in_specs=[pl.BlockSpec((tm, d), lambda p, i: (i * (1 - p) + (nt - 1) * p, 0)),
                  pl.BlockSpec((1, d), lambda p, i: (0, 0)),
                  pl.BlockSpec((2 * ne, d), lambda p, i: (0, 0))],
        out_specs=(pl.BlockSpec((ne, tm), lambda p, i: (0, i * p)),
                   pl.BlockSpec((tm, LANES), lambda p, i: (i * p, 0)),
                   pl.BlockSpec((ne, LANES), lambda p, i: (0, 0))),
        scratch_shapes=[pltpu.VMEM((nt, ne, tm), F32), pltpu.VMEM((nt, ne, tm), F32),
                        pltpu.VMEM((ne, LANES), F32), pltpu.VMEM((ne, LANES), F32), pltpu.VMEM((ne, LANES), F32)],
        compiler_params=_cparams(("arbitrary", "arbitrary")),
        name="moe_route",
    )(x, gain.reshape(1, d), wr)


def _dispatch_kernel(zf_ref, slots_ref, x_ref, g_ref, xs_ref, xn_s, zero_s, sem, zsem, *, tile_rows):
    tm = x_ref.shape[0]

    @pl.when(pl.program_id(0) == 0)
    def _():
        zero_s[...] = jnp.zeros_like(zero_s)

        def zero_copy(e):
            row0 = pl.multiple_of(zf_ref[e], tile_rows)
            return pltpu.make_async_copy(zero_s, xs_ref.at[pl.ds(row0, tile_rows)], zsem)

        for e in range(zf_ref.shape[0]):
            @pl.when(zf_ref[e] >= 0)
            def _():
                zero_copy(e).start()
        for e in range(zf_ref.shape[0]):
            @pl.when(zf_ref[e] >= 0)
            def _():
                zero_copy(e).wait()

    x = x_ref[...]
    ms = jnp.mean(x * x, axis=-1, keepdims=True)
    xn_s[...] = (x * lax.rsqrt(ms + EPS)) * g_ref[...]

    def row_copy(r, k):
        return pltpu.make_async_copy(xn_s.at[pl.ds(r, 1)], xs_ref.at[pl.ds(slots_ref[0, k, r], 1)], sem)

    def start(r, c):
        row_copy(r, 0).start()
        row_copy(r, 1).start()
        return c

    def wait(r, c):
        row_copy(r, 0).wait()
        row_copy(r, 1).wait()
        return c

    lax.fori_loop(0, tm, start, 0, unroll=8)
    lax.fori_loop(0, tm, wait, 0, unroll=8)


def moe_dispatch(x, gain, slots, zf_rows, n_slots, tm, tile_rows):
    t, d = x.shape
    tm = min(tm, t)
    nt = t // tm
    slots3 = slots.reshape(2, nt, tm).transpose(1, 0, 2)
    grid_spec = pltpu.PrefetchScalarGridSpec(
        num_scalar_prefetch=1,
        grid=(nt,),
        in_specs=[pl.BlockSpec((1, 2, tm), lambda i, zf: (i, 0, 0), memory_space=pltpu.SMEM),
                  pl.BlockSpec((tm, d), lambda i, zf: (i, 0)),
                  pl.BlockSpec((1, d), lambda i, zf: (0, 0))],
        out_specs=pl.BlockSpec(memory_space=pl.ANY),
        scratch_shapes=[pltpu.VMEM((tm, d), F32), pltpu.VMEM((tile_rows, d), F32),
                        pltpu.SemaphoreType.DMA, pltpu.SemaphoreType.DMA],
    )
    return pl.pallas_call(
        functools.partial(_dispatch_kernel, tile_rows=tile_rows),
        out_shape=jax.ShapeDtypeStruct((n_slots, d), F32),
        grid_spec=grid_spec,
        compiler_params=_cparams(("arbitrary",)),
        name="moe_dispatch",
    )(zf_rows, slots3, x, gain.reshape(1, d))


def _moe_grouped_kernel(te_ref, nv_ref, x_ref, wg_ref, wu_ref, wd_ref, o_ref, xb_ref, acc_ref):
    i = pl.program_id(0)
    j = pl.program_id(1)
    valid = i < nv_ref[0]

    @pl.when(valid & (j == 0))
    def _():
        xb_ref[...] = x_ref[...].astype(BF16)
        acc_ref[...] = jnp.zeros_like(acc_ref)

    @pl.when(valid)
    def _():
        xb = xb_ref[...]
        hid = _silu(_dot(xb, wg_ref[0])) * _dot(xb, wu_ref[0])
        acc_ref[...] += _dot(hid.astype(BF16), wd_ref[0])

    last = j == pl.num_programs(1) - 1

    @pl.when(valid & last)
    def _():
        o_ref[...] = acc_ref[...]

    @pl.when(jnp.logical_not(valid) & last)
    def _():
        o_ref[...] = jnp.zeros_like(o_ref)


def moe_grouped(xs, tile_expert, n_valid, wg, wu, wd, tile_rows, tf):
    n_slots, d = xs.shape
    ne, _, f = wg.shape
    n_tiles = n_slots // tile_rows
    nf = f // tf

    def x_map(i, j, te, nv):
        return (jnp.minimum(i, nv[0] - 1), 0)

    def w_in_map(i, j, te, nv):
        return (te[i], 0, jnp.where(i < nv[0], j, nf - 1))

    def w_out_map(i, j, te, nv):
        return (te[i], jnp.where(i < nv[0], j, nf - 1), 0)

    grid_spec = pltpu.PrefetchScalarGridSpec(
        num_scalar_prefetch=2,
        grid=(n_tiles, nf),
        in_specs=[pl.BlockSpec((tile_rows, d), x_map),
                  pl.BlockSpec((1, d, tf), w_in_map),
                  pl.BlockSpec((1, d, tf), w_in_map),
                  pl.BlockSpec((1, tf, d), w_out_map)],
        out_specs=pl.BlockSpec((tile_rows, d), lambda i, j, te, nv: (i, 0)),
        scratch_shapes=[pltpu.VMEM((tile_rows, d), BF16), pltpu.VMEM((tile_rows, d), F32)],
    )
    return pl.pallas_call(
        _moe_grouped_kernel,
        out_shape=jax.ShapeDtypeStruct((n_slots, d), F32),
        grid_spec=grid_spec,
        compiler_params=_cparams(("arbitrary", "arbitrary")),
        name="moe_grouped",
    )(tile_expert, n_valid, xs, wg, wu, wd)


def _combine_kernel(slots_ref, h_ref, wt_ref, ys_ref, o_ref, buf, sem):
    tm = h_ref.shape[0]

    def row_copy(r, k):
        return pltpu.make_async_copy(ys_ref.at[pl.ds(slots_ref[0, k, r], 1)], buf.at[k, pl.ds(r, 1)], sem)

    def start(r, c):
        row_copy(r, 0).start()
        row_copy(r, 1).start()
        return c

    def wait(r, c):
        row_copy(r, 0).wait()
        row_copy(r, 1).wait()
        return c

    lax.fori_loop(0, tm, start, 0, unroll=8)
    lax.fori_loop(0, tm, wait, 0, unroll=8)
    wt = wt_ref[...]
    o_ref[...] = h_ref[...] + wt[:, 0:1] * buf[0] + wt[:, 1:2] * buf[1]


def moe_combine(h, wt, slots, ys, tm):
    t, d = h.shape
    tm = min(tm, t)
    nt = t // tm
    slots3 = slots.reshape(2, nt, tm).transpose(1, 0, 2)
    return pl.pallas_call(
        _combine_kernel,
        out_shape=jax.ShapeDtypeStruct((t, d), F32),
        grid=(nt,),
        in_specs=[pl.BlockSpec((1, 2, tm), lambda i: (i, 0, 0), memory_space=pltpu.SMEM),
                  pl.BlockSpec((tm, d), lambda i: (i, 0)),
                  pl.BlockSpec((tm, LANES), lambda i: (i, 0)),
                  pl.BlockSpec(memory_space=pl.ANY)],
        out_specs=pl.BlockSpec((tm, d), lambda i: (i, 0)),
        scratch_shapes=[pltpu.VMEM((2, tm, d), F32), pltpu.SemaphoreType.DMA],
        compiler_params=_cparams(("arbitrary",)),
        name="moe_combine",
    )(slots3, h, wt, ys)


MOE_TILE_ROWS = 512
TOP_K = 2


def moe_layer(h, gain, w_router, wg, wu, wd):
    t, d = h.shape
    ne = w_router.shape[1]
    tr = MOE_TILE_ROWS
    n_tiles = -(-(TOP_K * t + ne * (tr - 1)) // tr)
    n_slots = n_tiles * tr

    r, wt, cnt = moe_route(h, gain, w_router, 512, tr)
    slots = r[0:2].astype(jnp.int32)

    counts = cnt[:, 0].astype(jnp.int32)
    padded = ((counts + (tr - 1)) // tr) * tr
    ends = jnp.cumsum(padded)
    n_valid = (ends[-1] // tr).astype(jnp.int32)
    tile_expert = jnp.searchsorted(ends, jnp.arange(n_tiles, dtype=jnp.int32) * tr, side="right")
    tile_expert = jnp.minimum(tile_expert, ne - 1).astype(jnp.int32)
    tile_expert = jnp.where(jnp.arange(n_tiles) < n_valid, tile_expert, tile_expert[jnp.maximum(n_valid - 1, 0)])
    tail = jnp.arange(TOP_K * t // tr, n_tiles, dtype=jnp.int32)
    zf_rows = jnp.concatenate([jnp.where(padded > 0, ends - tr, -1),
                               jnp.where(tail >= n_valid, tail * tr, -1)]).astype(jnp.int32)

    xs = moe_dispatch(h, gain, slots, zf_rows, n_slots, 512, tr)
    ys = moe_grouped(xs, tile_expert, n_valid.reshape(1), wg.astype(BF16), wu.astype(BF16), wd.astype(BF16),
                     tr, 512)
    return moe_combine(h, wt, slots, ys, 256)


def kernel(x, a_norm, a_w_in, a_conv, a_log_decay, a_dt_bias, a_out_norm, a_w_out, kv_norm, kv_w, k_norm,
           b_norm, b_w_q, q_norm, b_sinks, b_w_o, rel_bias, ffn_norm, dense_w_gate, dense_w_up, dense_w_down,
           moe_router, moe_w_gate, moe_w_up, moe_w_down):
    batch, seq, d = x.shape
    t = batch * seq
    nh, hd = LA_HEADS, LA_D
    main_w = 4 * nh * hd
    h0 = x.reshape(t, d)

    w_in = a_w_in[0]
    w_main = w_in[:, 0:main_w].astype(BF16)
    w_gate = jnp.zeros((d, LANES), BF16).at[:, 0:2 * nh].set(w_in[:, main_w:main_w + 2 * nh].astype(BF16))
    proj = norm_matmul(h0, a_norm[0], w_main, BF16, 1024, 1024, "gdn_in_proj")
    gates = norm_matmul(h0, a_norm[0], w_gate, F32, 1024, LANES, "gdn_gate_proj")
    o = gdn_core(proj, gates, a_conv[0], a_log_decay[0], a_dt_bias[0], a_out_norm[0], batch, seq)
    h1 = matmul_residual(o, a_w_out[0].astype(BF16), h0, 1024, 1024, "gdn_out_proj")

    h2 = ffn_dense(h1, ffn_norm[0], dense_w_gate[0].astype(BF16), dense_w_up[0].astype(BF16),
                   dense_w_down[0].astype(BF16), 1024, 512)

    kv = norm_matmul(h2, kv_norm, kv_w.astype(BF16), BF16, 1024, 512, "kv_proj")
    q = norm_matmul(h2, b_norm[0], b_w_q[0].astype(BF16), BF16, 1024, 1024, "q_proj")
    bias = bias_table(rel_bias)
    attn = swa_attention(q, kv, bias, q_norm[0], k_norm, b_sinks[0], batch, seq)
    h3 = matmul_residual(attn, b_w_o[0].astype(BF16), h2, 1024, 1024, "attn_out_proj")

    h4 = moe_layer(h3, ffn_norm[1], moe_router[0], moe_w_gate[0], moe_w_up[0], moe_w_down[0])
    return h4.reshape(batch, seq, d)
```

```python
import functools

import numpy as np
import jax
import jax.numpy as jnp
from jax import lax
from jax.experimental import pallas as pl
from jax.experimental.pallas import tpu as pltpu

F32 = jnp.float32
BF16 = jnp.bfloat16

EPS = 1e-6
NEG_INF = -1e30

LA_HEADS = 8
LA_D = 128
CONV_W = 4
CHUNK = 64
SW_HEADS = 16
SW_KV_HEADS = 4
SW_GROUP = SW_HEADS // SW_KV_HEADS
SW_HD = 64
WINDOW = 128
N_BUCKETS = 32
MAX_DIST = 128
N_EXPERTS = 8

LANES = 128
GDN_BLOCK = 2 * CHUNK
HALO = 8

VMEM_LIMIT = 56 * 1024 * 1024


def _cparams(sem):
    return pltpu.CompilerParams(dimension_semantics=sem, vmem_limit_bytes=VMEM_LIMIT)


def _silu(x):
    return x * (1.0 / (1.0 + jnp.exp(-x)))


def _dot(a, b):
    return jnp.dot(a, b, preferred_element_type=F32)


def _dot_nt(a, b):
    return lax.dot_general(a, b, (((1,), (1,)), ((), ())), preferred_element_type=F32)


def _norm_matmul_kernel(x_ref, g_ref, w_ref, o_ref, xn_ref):
    @pl.when(pl.program_id(1) == 0)
    def _():
        x = x_ref[...]
        ms = jnp.mean(x * x, axis=-1, keepdims=True)
        xn_ref[...] = ((x * lax.rsqrt(ms + EPS)) * g_ref[...]).astype(BF16)

    o_ref[...] = _dot(xn_ref[...], w_ref[...]).astype(o_ref.dtype)


def norm_matmul(x, gain, w, out_dtype, tm, tn, name):
    t, d = x.shape
    n = w.shape[1]
    tm, tn = min(tm, t), min(tn, n)
    return pl.pallas_call(
        _norm_matmul_kernel,
        out_shape=jax.ShapeDtypeStruct((t, n), out_dtype),
        grid=(t // tm, n // tn),
        in_specs=[pl.BlockSpec((tm, d), lambda i, j: (i, 0)),
                  pl.BlockSpec((1, d), lambda i, j: (0, 0)),
                  pl.BlockSpec((d, tn), lambda i, j: (0, j))],
        out_specs=pl.BlockSpec((tm, tn), lambda i, j: (i, j)),
        scratch_shapes=[pltpu.VMEM((tm, d), BF16)],
        compiler_params=_cparams(("parallel", "arbitrary")),
        name=name,
    )(x, gain.reshape(1, d), w)


def _matmul_res_kernel(a_ref, w_ref, r_ref, o_ref):
    o_ref[...] = r_ref[...] + _dot(a_ref[...], w_ref[...])


def matmul_residual(a, w, res, tm, tn, name):
    t, k = a.shape
    n = w.shape[1]
    tm, tn = min(tm, t), min(tn, n)
    return pl.pallas_call(
        _matmul_res_kernel,
        out_shape=jax.ShapeDtypeStruct((t, n), F32),
        grid=(t // tm, n // tn),
        in_specs=[pl.BlockSpec((tm, k), lambda i, j: (i, 0)),
                  pl.BlockSpec((k, tn), lambda i, j: (0, j)),
                  pl.BlockSpec((tm, tn), lambda i, j: (i, j))],
        out_specs=pl.BlockSpec((tm, tn), lambda i, j: (i, j)),
        compiler_params=_cparams(("parallel", "parallel")),
        name=name,
    )(a, w, res)


def _gdn_kernel(proj_ref, gates_ref, convw_ref, hp_ref, onorm_ref, o_ref, xs_ref, state_ref):
    nh, d, c = LA_HEADS, LA_D, CHUNK
    blk = GDN_BLOCK
    qkv_w = 3 * nh * d

    @pl.when(pl.program_id(1) == 0)
    def _():
        state_ref[...] = jnp.zeros_like(state_ref)
        xs_ref[0:HALO, :] = jnp.zeros((HALO, qkv_w), F32)

    xs_ref[HALO:HALO + blk, :] = proj_ref[:, 0:qkv_w].astype(F32)

    gates = gates_ref[...]
    a_log = hp_ref[0:1, :]
    dt_bias = hp_ref[1:2, :]
    beta = 1.0 / (1.0 + jnp.exp(-gates))
    sp_in = gates + dt_bias
    softplus = jnp.maximum(sp_in, 0.0) + jnp.log(1.0 + jnp.exp(-jnp.abs(sp_in)))
    g = -jnp.exp(a_log) * softplus

    row = lax.broadcasted_iota(jnp.int32, (blk, blk), 0)
    col = lax.broadcasted_iota(jnp.int32, (blk, blk), 1)
    tri = jnp.where((row >= col) & ((row // c) == (col // c)), 1.0, 0.0).astype(BF16)
    g_hi = g.astype(BF16)
    g_r1 = g - g_hi.astype(F32)
    g_mid = g_r1.astype(BF16)
    g_lo = (g_r1 - g_mid.astype(F32)).astype(BF16)
    gc = _dot(tri, g_hi) + _dot(tri, g_mid) + _dot(tri, g_lo)
    gc_t = gc.T

    ci = lax.broadcasted_iota(jnp.int32, (c, c), 0)
    cj = lax.broadcasted_iota(jnp.int32, (c, c), 1)
    lower_incl = ci >= cj
    strict = ci > cj
    eye_c = jnp.where(ci == cj, 1.0, 0.0).astype(F32)
    di = lax.broadcasted_iota(jnp.int32, (d, d), 0)
    dj = lax.broadcasted_iota(jnp.int32, (d, d), 1)
    eye_d = jnp.where(di == dj, 1.0, 0.0).astype(BF16)

    onorm = onorm_ref[...]

    def conv_silu(col0):
        acc = None
        for j in range(CONV_W):
            r0 = HALO - (CONV_W - 1) + j
            term = convw_ref[j:j + 1, col0:col0 + d] * xs_ref[r0:r0 + blk, col0:col0 + d]
            acc = term if acc is None else acc + term
        return _silu(acc)

    n_ck = blk // c
    chains = [(h, ck) for h in range(nh) for ck in range(n_ck)]
    heads = []
    for h in range(nh):
        qf = conv_silu(h * d)
        kf = conv_silu(nh * d + h * d)
        vf = conv_silu(2 * nh * d + h * d)
        qf = qf * lax.rsqrt(jnp.sum(qf * qf, axis=-1, keepdims=True) + EPS) * (d ** -0.5)
        kf = kf * lax.rsqrt(jnp.sum(kf * kf, axis=-1, keepdims=True) + EPS)
        heads.append((qf, kf, vf))

    st = {}
    for (h, ck) in chains:
        r = ck * c
        qf, kf, vf = heads[h]
        q, k, v = qf[r:r + c], kf[r:r + c], vf[r:r + c]
        g_col = gc[r:r + c, nh + h:nh + h + 1]
        g_row = gc_t[nh + h:nh + h + 1, r:r + c]
        g_last = gc[r + c - 1:r + c, nh + h:nh + h + 1]
        b_col = beta[r:r + c, h:h + 1]
        decay = jnp.where(lower_incl, jnp.exp(jnp.where(lower_incl, g_col - g_row, 0.0)), 0.0)
        k_beta = k * b_col
        e_col = jnp.exp(g_col)
        lhs = jnp.concatenate([k_beta.astype(BF16), q.astype(BF16), eye_d], axis=0)
        kk = _dot_nt(lhs, k.astype(BF16))
        a_mat = jnp.where(strict, kk[0:c] * decay, 0.0)
        st[(h, ck)] = dict(
            a=a_mat, attn=(kk[c:2 * c] * decay).astype(BF16),
            k_tail_t=(kk[2 * c:2 * c + d] * jnp.exp(g_last - g_row)).astype(BF16),
            rhs=jnp.concatenate([(v * b_col).astype(BF16), (k_beta * e_col).astype(BF16)], axis=1),
            qe=(q * e_col).astype(BF16), e_last=jnp.exp(g_last))

    for key in chains:
        x_b = (-st[key]["a"]).astype(BF16)
        st[key]["y"] = _dot(x_b, x_b)
        st[key]["p"] = eye_c - st[key]["a"]
    n_levels = int(np.log2(c))
    for lvl in range(1, n_levels):
        for key in chains:
            y_b = st[key]["y"].astype(BF16)
            p = st[key]["p"]
            if lvl + 1 < n_levels:
                zz = _dot(jnp.concatenate([y_b, p.astype(BF16)], axis=0), y_b)
                st[key]["y"] = zz[0:c]
                st[key]["p"] = p + zz[c:2 * c]
            else:
                st[key]["p"] = p + _dot(p.astype(BF16), y_b)
    for key in chains:
        st[key]["uw"] = _dot(st[key]["p"].astype(BF16), st[key]["rhs"])

    for ck in range(n_ck):
        r = ck * c
        s_old = [state_ref[h] for h in range(nh)]
        ws_qs = []
        for h in range(nh):
            cur = st[(h, ck)]
            lhs = jnp.concatenate([cur["uw"][:, d:2 * d].astype(BF16), cur["qe"]], axis=0)
            ws_qs.append(_dot(lhs, s_old[h].astype(BF16)))
        for h in range(nh):
            cur = st[(h, ck)]
            v_new = cur["uw"][:, 0:d] - ws_qs[h][0:c]
            av_kv = _dot(jnp.concatenate([cur["attn"], cur["k_tail_t"]], axis=0), v_new.astype(BF16))
            state_ref[h] = s_old[h] * cur["e_last"] + av_kv[c:c + d]
            o = ws_qs[h][c:2 * c] + av_kv[0:c]
            o = (o * lax.rsqrt(jnp.mean(o * o, axis=-1, keepdims=True) + EPS)) * onorm
            z = proj_ref[r:r + c, qkv_w + h * d:qkv_w + (h + 1) * d].astype(F32)
            o_ref[r:r + c, h * d:(h + 1) * d] = (o * _silu(z)).astype(o_ref.dtype)

    xs_ref[0:HALO, :] = xs_ref[blk:blk + HALO, :]


def gdn_core(proj, gates, conv_w, a_log, dt_bias, out_norm, batch, seq):
    t = proj.shape[0]
    nh, d = LA_HEADS, LA_D
    blk = GDN_BLOCK
    nblk = seq // blk
    hp = jnp.zeros((8, LANES), F32)
    hp = hp.at[0, nh:2 * nh].set(a_log.astype(F32)).at[1, nh:2 * nh].set(dt_bias.astype(F32))
    return pl.pallas_call(
        _gdn_kernel,
        out_shape=jax.ShapeDtypeStruct((t, nh * d), BF16),
        grid=(batch, nblk),
        in_specs=[pl.BlockSpec((blk, 4 * nh * d), lambda b, n: (b * nblk + n, 0)),
                  pl.BlockSpec((blk, LANES), lambda b, n: (b * nblk + n, 0)),
                  pl.BlockSpec((CONV_W, 3 * nh * d), lambda b, n: (0, 0)),
                  pl.BlockSpec((8, LANES), lambda b, n: (0, 0)),
                  pl.BlockSpec((1, d), lambda b, n: (0, 0))],
        out_specs=pl.BlockSpec((blk, nh * d), lambda b, n: (b * nblk + n, 0)),
        scratch_shapes=[pltpu.VMEM((HALO + blk, 3 * nh * d), F32),
                        pltpu.VMEM((nh, d, d), F32)],
        compiler_params=_cparams(("parallel", "arbitrary")),
        name="gdn_core",
    )(proj, gates, conv_w.astype(F32), hp, out_norm.reshape(1, d).astype(F32))


def _ffn_kernel(x_ref, g_ref, wg_ref, wu_ref, wd_ref, o_ref, xn_ref, acc_ref):
    j = pl.program_id(1)

    @pl.when(j == 0)
    def _():
        x = x_ref[...]
        ms = jnp.mean(x * x, axis=-1, keepdims=True)
        xn_ref[...] = ((x * lax.rsqrt(ms + EPS)) * g_ref[...]).astype(BF16)
        acc_ref[...] = jnp.zeros_like(acc_ref)

    xn = xn_ref[...]
    hid = _silu(_dot(xn, wg_ref[...])) * _dot(xn, wu_ref[...])
    acc_ref[...] += _dot(hid.astype(BF16), wd_ref[...])

    @pl.when(j == pl.num_programs(1) - 1)
    def _():
        o_ref[...] = x_ref[...] + acc_ref[...]


def ffn_dense(x, gain, wg, wu, wd, tm, tf):
    t, d = x.shape
    f = wg.shape[1]
    tm = min(tm, t)
    return pl.pallas_call(
        _ffn_kernel,
        out_shape=jax.ShapeDtypeStruct((t, d), F32),
        grid=(t // tm, f // tf),
        in_specs=[pl.BlockSpec((tm, d), lambda i, j: (i, 0)),
                  pl.BlockSpec((1, d), lambda i, j: (0, 0)),
                  pl.BlockSpec((d, tf), lambda i, j: (0, j)),
                  pl.BlockSpec((d, tf), lambda i, j: (0, j)),
                  pl.BlockSpec((tf, d), lambda i, j: (j, 0))],
        out_specs=pl.BlockSpec((tm, d), lambda i, j: (i, 0)),
        scratch_shapes=[pltpu.VMEM((tm, d), BF16), pltpu.VMEM((tm, d), F32)],
        compiler_params=_cparams(("parallel", "arbitrary")),
        name="ffn_dense",
    )(x, gain.reshape(1, d), wg, wu, wd)


def _t5_bucket_np(dist):
    max_exact = N_BUCKETS // 2
    n = np.maximum(dist, 0)
    safe = np.maximum(n, 1).astype(np.float32)
    large = max_exact + (np.log(safe / max_exact) / np.log(MAX_DIST / max_exact)
                         * (N_BUCKETS - max_exact)).astype(np.int32)
    large = np.minimum(large, N_BUCKETS - 1)
    return np.where(n < max_exact, n, large).astype(np.int32)


def _bias_kernel(bucket_ref, rb_ref, o_ref):
    bucket = bucket_ref[...]
    for h in range(SW_HEADS):
        acc = jnp.zeros(bucket.shape, F32)
        for b in range(N_BUCKETS):
            acc = jnp.where(bucket == b, rb_ref[b, h], acc)
        o_ref[h] = acc


def bias_table(rel_bias):
    qi = np.arange(WINDOW)[:, None] + WINDOW
    kj = np.arange(2 * WINDOW)[None, :]
    bucket = jnp.asarray(_t5_bucket_np(qi - kj))
    return pl.pallas_call(
        _bias_kernel,
        out_shape=jax.ShapeDtypeStruct((SW_HEADS, WINDOW, 2 * WINDOW), F32),
        in_specs=[pl.BlockSpec(memory_space=pltpu.VMEM), pl.BlockSpec(memory_space=pltpu.SMEM)],
        out_specs=pl.BlockSpec(memory_space=pltpu.VMEM),
        name="t5_bias_table",
    )(bucket, rel_bias.astype(F32))


def _swa_kernel(q_ref, kvp_ref, kvc_ref, bias_ref, qn_ref, kn_ref, sink_ref, o_ref):
    n = pl.program_id(1)
    blk, hd = WINDOW, SW_HD
    kv_w = SW_KV_HEADS * hd
    qi = lax.broadcasted_iota(jnp.int32, (blk, 2 * blk), 0) + blk
    kj = lax.broadcasted_iota(jnp.int32, (blk, 2 * blk), 1)
    dist = qi - kj
    first_key = jnp.where(n > 0, 0, blk)
    mask = (dist >= 0) & (dist < WINDOW) & (kj >= first_key)
    qn = qn_ref[...]
    kn = kn_ref[...]

    def head_norm(x, gain):
        return (x * lax.rsqrt(jnp.mean(x * x, axis=-1, keepdims=True) + EPS)) * gain

    ks, vs = [], []
    for g in range(SW_KV_HEADS):
        kp = kvp_ref[:, g * hd:(g + 1) * hd].astype(F32)
        kc = kvc_ref[:, g * hd:(g + 1) * hd].astype(F32)
        ks.append(jnp.concatenate([head_norm(kp, kn), head_norm(kc, kn)], axis=0).astype(BF16))
        vs.append(jnp.concatenate([kvp_ref[:, kv_w + g * hd:kv_w + (g + 1) * hd],
                                   kvc_ref[:, kv_w + g * hd:kv_w + (g + 1) * hd]], axis=0))
    scores = []
    for hq in range(SW_HEADS):
        q = head_norm(q_ref[:, hq * hd:(hq + 1) * hd].astype(F32), qn) * (hd ** -0.5)
        scores.append(_dot_nt(q.astype(BF16), ks[hq // SW_GROUP]))
    probs = []
    for hq in range(SW_HEADS):
        s = jnp.where(mask, scores[hq] + bias_ref[hq], NEG_INF)
        sink = sink_ref[hq]
        mx = jnp.maximum(jnp.max(s, axis=-1, keepdims=True), sink)
        p = jnp.exp(s - mx)
        denom = jnp.sum(p, axis=-1, keepdims=True) + jnp.exp(sink - mx)
        probs.append((p / denom).astype(BF16))
    for hq in range(SW_HEADS):
        o = _dot(probs[hq], vs[hq // SW_GROUP])
        o_ref[:, hq * hd:(hq + 1) * hd] = o.astype(o_ref.dtype)


def swa_attention(q, kv, bias, q_norm, k_norm, sinks, batch, seq):
    t = q.shape[0]
    blk = WINDOW
    nb = seq // blk
    qw = SW_HEADS * SW_HD
    kvw = 2 * SW_KV_HEADS * SW_HD
    return pl.pallas_call(
        _swa_kernel,
        out_shape=jax.ShapeDtypeStruct((t, qw), BF16),
        grid=(batch, nb),
        in_specs=[pl.BlockSpec((blk, qw), lambda b, n: (b * nb + n, 0)),
                  pl.BlockSpec((blk, kvw), lambda b, n: (b * nb + jnp.maximum(n - 1, 0), 0)),
                  pl.BlockSpec((blk, kvw), lambda b, n: (b * nb + n, 0)),
                  pl.BlockSpec((SW_HEADS, blk, 2 * blk), lambda b, n: (0, 0, 0)),
                  pl.BlockSpec((1, SW_HD), lambda b, n: (0, 0)),
                  pl.BlockSpec((1, SW_HD), lambda b, n: (0, 0)),
                  pl.BlockSpec(memory_space=pltpu.SMEM)],
        out_specs=pl.BlockSpec((blk, qw), lambda b, n: (b * nb + n, 0)),
        compiler_params=_cparams(("parallel", "parallel")),
        name="swa_attention",
    )(q, kv, kv, bias, q_norm.reshape(1, SW_HD).astype(F32), k_norm.reshape(1, SW_HD).astype(F32),
      sinks.astype(F32))


def _route_kernel(x_ref, g_ref, wr_ref, r_ref, wt_ref, cnt_ref, sel_s, gw_s, cnt_s, start_s, run_s, *, tile_rows):
    ne = N_EXPERTS
    p = pl.program_id(0)
    i = pl.program_id(1)
    tm = x_ref.shape[0]
    sub = lax.broadcasted_iota(jnp.int32, (ne, tm), 0).astype(F32)

    @pl.when(p == 0)
    def _():
        @pl.when(i == 0)
        def _():
            cnt_s[...] = jnp.zeros_like(cnt_s)

        x = x_ref[...]
        ms = jnp.mean(x * x, axis=-1, keepdims=True)
        xn32 = (x * lax.rsqrt(ms + EPS)) * g_ref[...]
        xn_hi = xn32.astype(BF16)
        xn_lo = (xn32 - xn_hi.astype(F32)).astype(BF16)
        p_hi = _dot_nt(wr_ref[...], xn_hi)
        p_lo = _dot_nt(wr_ref[...], xn_lo)
        logits = p_hi[0:ne] + p_hi[ne:2 * ne] + p_lo[0:ne]
        m1 = jnp.max(logits, axis=0, keepdims=True)
        i1 = jnp.min(jnp.where(logits == m1, sub, float(ne)), axis=0, keepdims=True)
        l2 = jnp.where(sub == i1, -jnp.inf, logits)
        m2 = jnp.max(l2, axis=0, keepdims=True)
        i2 = jnp.min(jnp.where(l2 == m2, sub, float(ne)), axis=0, keepdims=True)
        e2 = jnp.exp(m2 - m1)
        w1 = 1.0 / (1.0 + e2)
        w2 = e2 / (1.0 + e2)
        sel = jnp.where((sub == i1) | (sub == i2), 1.0, 0.0)
        sel_s[i] = sel
        gw_s[i] = jnp.where(sub == i1, w1, jnp.where(sub == i2, w2, 0.0))
        cnt_s[...] += jnp.sum(sel, axis=1, keepdims=True)

    @pl.when(p == 1)
    def _():
        @pl.when(i == 0)
        def _():
            cnt = cnt_s[...]
            padded = jnp.floor((cnt + (tile_rows - 1)) * (1.0 / tile_rows)) * tile_rows
            sub8 = lax.broadcasted_iota(jnp.int32, cnt.shape, 0)
            start = jnp.zeros_like(cnt)
            for e in range(ne - 1):
                start = start + jnp.where(sub8 > e, padded[e:e + 1, :], 0.0)
            start_s[...] = start
            run_s[...] = jnp.zeros_like(run_s)
            cnt_ref[...] = cnt

        sel = sel_s[i]
        gw = gw_s[i]
        ti = lax.broadcasted_iota(jnp.int32, (tm, tm), 0)
        tj = lax.broadcasted_iota(jnp.int32, (tm, tm), 1)
        tri = jnp.where(ti <= tj, 1.0, 0.0).astype(BF16)
        csum = _dot(sel.astype(BF16), tri)
        slot = start_s[:, 0:1] + run_s[:, 0:1] + csum - sel
        run_s[...] += csum[:, tm - 1:tm]
        ia = jnp.min(jnp.where(sel > 0.0, sub, float(ne)), axis=0, keepdims=True)
        ib = jnp.max(jnp.where(sel > 0.0, sub, -1.0), axis=0, keepdims=True)
        pick_a = sub == ia
        pick_b = sub == ib
        rows = [jnp.sum(jnp.where(pick_a, slot, 0.0), axis=0, keepdims=True),
                jnp.sum(jnp.where(pick_b, slot, 0.0), axis=0, keepdims=True),
                jnp.sum(jnp.where(pick_a, gw, 0.0), axis=0, keepdims=True),
                jnp.sum(jnp.where(pick_b, gw, 0.0), axis=0, keepdims=True)]
        r_ref[...] = jnp.concatenate(rows + [jnp.zeros((ne - 4, tm), F32)], axis=0)
        wpad = jnp.concatenate(rows[2:4] + [jnp.zeros((LANES - 2, tm), F32)], axis=0)
        wt_ref[...] = wpad.T


def moe_route(x, gain, w_router, tm, tile_rows):
    t, d = x.shape
    ne = w_router.shape[1]
    assert ne == N_EXPERTS
    w_hi = w_router.astype(BF16)
    w_lo = (w_router - w_hi.astype(F32)).astype(BF16)
    wr = jnp.concatenate([w_hi.T, w_lo.T], axis=0)
    tm = min(tm, t)
    nt = t // tm
    return pl.pallas_call(
        functools.partial(_route_kernel, tile_rows=tile_rows),
        out_shape=(jax.ShapeDtypeStruct((ne, t), F32), jax.ShapeDtypeStruct((t, LANES), F32),
                   jax.ShapeDtypeStruct((ne, LANES), F32)),
        grid=(2, nt),
        in_specs=[pl.BlockSpec((tm, d), lambda p, i: (i * (1 - p) + (nt - 1) * p, 0)),
                  pl.BlockSpec((1, d), lambda p, i: (0, 0)),
                  pl.BlockSpec((2 * ne, d), lambda p, i: (0, 0))],
        out_specs=(pl.BlockSpec((ne, tm), lambda p, i: (0, i * p)),
                   pl.BlockSpec((tm, LANES), lambda p, i: (i * p, 0)),
                   pl.BlockSpec((ne, LANES), lambda p, i: (0, 0))),
        scratch_shapes=[pltpu.VMEM((nt, ne, tm), F32), pltpu.VMEM((nt, ne, tm), F32),
                        pltpu.VMEM((ne, LANES), F32), pltpu.VMEM((ne, LANES), F32), pltpu.VMEM((ne, LANES), F32)],
        compiler_params=_cparams(("arbitrary", "arbitrary")),
        name="moe_route",
    )(x, gain.reshape(1, d), wr)


def _dispatch_kernel(zf_ref, slots_ref, x_ref, g_ref, xs_ref, xn_s, zero_s, sem, zsem, *, tile_rows):
    tm = x_ref.shape[0]

    @pl.when(pl.program_id(0) == 0)
    def _():
        zero_s[...] = jnp.zeros_like(zero_s)

        def zero_copy(e):
            row0 = pl.multiple_of(zf_ref[e], tile_rows)
            return pltpu.make_async_copy(zero_s, xs_ref.at[pl.ds(row0, tile_rows)], zsem)

        for e in range(zf_ref.shape[0]):
            @pl.when(zf_ref[e] >= 0)
            def _():
                zero_copy(e).start()
        for e in range(zf_ref.shape[0]):
            @pl.when(zf_ref[e] >= 0)
            def _():
                zero_copy(e).wait()

    x = x_ref[...]
    ms = jnp.mean(x * x, axis=-1, keepdims=True)
    xn_s[...] = (x * lax.rsqrt(ms + EPS)) * g_ref[...]

    def row_copy(r, k):
        return pltpu.make_async_copy(xn_s.at[pl.ds(r, 1)], xs_ref.at[pl.ds(slots_ref[0, k, r], 1)], sem)

    def start(r, c):
        row_copy(r, 0).start()
        row_copy(r, 1).start()
        return c

    def wait(r, c):
        row_copy(r, 0).wait()
        row_copy(r, 1).wait()
        return c

    lax.fori_loop(0, tm, start, 0, unroll=8)
    lax.fori_loop(0, tm, wait, 0, unroll=8)


def moe_dispatch(x, gain, slots, zf_rows, n_slots, tm, tile_rows):
    t, d = x.shape
    tm = min(tm, t)
    nt = t // tm
    slots3 = slots.reshape(2, nt, tm).transpose(1, 0, 2)
    grid_spec = pltpu.PrefetchScalarGridSpec(
        num_scalar_prefetch=1,
        grid=(nt,),
        in_specs=[pl.BlockSpec((1, 2, tm), lambda i, zf: (i, 0, 0), memory_space=pltpu.SMEM),
                  pl.BlockSpec((tm, d), lambda i, zf: (i, 0)),
                  pl.BlockSpec((1, d), lambda i, zf: (0, 0))],
        out_specs=pl.BlockSpec(memory_space=pl.ANY),
        scratch_shapes=[pltpu.VMEM((tm, d), F32), pltpu.VMEM((tile_rows, d), F32),
                        pltpu.SemaphoreType.DMA, pltpu.SemaphoreType.DMA],
    )
    return pl.pallas_call(
        functools.partial(_dispatch_kernel, tile_rows=tile_rows),
        out_shape=jax.ShapeDtypeStruct((n_slots, d), F32),
        grid_spec=grid_spec,
        compiler_params=_cparams(("arbitrary",)),
        name="moe_dispatch",
    )(zf_rows, slots3, x, gain.reshape(1, d))


def _moe_grouped_kernel(te_ref, nv_ref, x_ref, wg_ref, wu_ref, wd_ref, o_ref, xb_ref, acc_ref):
    i = pl.program_id(0)
    j = pl.program_id(1)
    valid = i < nv_ref[0]

    @pl.when(valid & (j == 0))
    def _():
        xb_ref[...] = x_ref[...].astype(BF16)
        acc_ref[...] = jnp.zeros_like(acc_ref)

    @pl.when(valid)
    def _():
        xb = xb_ref[...]
        hid = _silu(_dot(xb, wg_ref[0])) * _dot(xb, wu_ref[0])
        acc_ref[...] += _dot(hid.astype(BF16), wd_ref[0])

    last = j == pl.num_programs(1) - 1

    @pl.when(valid & last)
    def _():
        o_ref[...] = acc_ref[...]

    @pl.when(jnp.logical_not(valid) & last)
    def _():
        o_ref[...] = jnp.zeros_like(o_ref)


def moe_grouped(xs, tile_expert, n_valid, wg, wu, wd, tile_rows, tf):
    n_slots, d = xs.shape
    ne, _, f = wg.shape
    n_tiles = n_slots // tile_rows
    nf = f // tf

    def x_map(i, j, te, nv):
        return (jnp.minimum(i, nv[0] - 1), 0)

    def w_in_map(i, j, te, nv):
        return (te[i], 0, jnp.where(i < nv[0], j, nf - 1))

    def w_out_map(i, j, te, nv):
        return (te[i], jnp.where(i < nv[0], j, nf - 1), 0)

    grid_spec = pltpu.PrefetchScalarGridSpec(
        num_scalar_prefetch=2,
        grid=(n_tiles, nf),
        in_specs=[pl.BlockSpec((tile_rows, d), x_map),
                  pl.BlockSpec((1, d, tf), w_in_map),
                  pl.BlockSpec((1, d, tf), w_in_map),
                  pl.BlockSpec((1, tf, d), w_out_map)],
        out_specs=pl.BlockSpec((tile_rows, d), lambda i, j, te, nv: (i, 0)),
        scratch_shapes=[pltpu.VMEM((tile_rows, d), BF16), pltpu.VMEM((tile_rows, d), F32)],
    )
    return pl.pallas_call(
        _moe_grouped_kernel,
        out_shape=jax.ShapeDtypeStruct((n_slots, d), F32),
        grid_spec=grid_spec,
        compiler_params=_cparams(("arbitrary", "arbitrary")),
        name="moe_grouped",
    )(tile_expert, n_valid, xs, wg, wu, wd)


def _combine_kernel(slots_ref, h_ref, wt_ref, ys_ref, o_ref, buf, sem):
    tm = h_ref.shape[0]

    def row_copy(r, k):
        return pltpu.make_async_copy(ys_ref.at[pl.ds(slots_ref[0, k, r], 1)], buf.at[k, pl.ds(r, 1)], sem)

    def start(r, c):
        row_copy(r, 0).start()
        row_copy(r, 1).start()
        return c

    def wait(r, c):
        row_copy(r, 0).wait()
        row_copy(r, 1).wait()
        return c

    lax.fori_loop(0, tm, start, 0, unroll=8)
    lax.fori_loop(0, tm, wait, 0, unroll=8)
    wt = wt_ref[...]
    o_ref[...] = h_ref[...] + wt[:, 0:1] * buf[0] + wt[:, 1:2] * buf[1]


def moe_combine(h, wt, slots, ys, tm):
    t, d = h.shape
    tm = min(tm, t)
    nt = t // tm
    slots3 = slots.reshape(2, nt, tm).transpose(1, 0, 2)
    return pl.pallas_call(
        _combine_kernel,
        out_shape=jax.ShapeDtypeStruct((t, d), F32),
        grid=(nt,),
        in_specs=[pl.BlockSpec((1, 2, tm), lambda i: (i, 0, 0), memory_space=pltpu.SMEM),
                  pl.BlockSpec((tm, d), lambda i: (i, 0)),
                  pl.BlockSpec((tm, LANES), lambda i: (i, 0)),
                  pl.BlockSpec(memory_space=pl.ANY)],
        out_specs=pl.BlockSpec((tm, d), lambda i: (i, 0)),
        scratch_shapes=[pltpu.VMEM((2, tm, d), F32), pltpu.SemaphoreType.DMA],
        compiler_params=_cparams(("arbitrary",)),
        name="moe_combine",
    )(slots3, h, wt, ys)


MOE_TILE_ROWS = 512
TOP_K = 2


def moe_layer(h, gain, w_router, wg, wu, wd):
    t, d = h.shape
    ne = w_router.shape[1]
    tr = MOE_TILE_ROWS
    n_tiles = -(-(TOP_K * t + ne * (tr - 1)) // tr)
    n_slots = n_tiles * tr

    r, wt, cnt = moe_route(h, gain, w_router, 512, tr)
    slots = r[0:2].astype(jnp.int32)

    counts = cnt[:, 0].astype(jnp.int32)
    padded = ((counts + (tr - 1)) // tr) * tr
    ends = jnp.cumsum(padded)
    n_valid = (ends[-1] // tr).astype(jnp.int32)
    tile_expert = jnp.searchsorted(ends, jnp.arange(n_tiles, dtype=jnp.int32) * tr, side="right")
    tile_expert = jnp.minimum(tile_expert, ne - 1).astype(jnp.int32)
    tile_expert = jnp.where(jnp.arange(n_tiles) < n_valid, tile_expert, tile_expert[jnp.maximum(n_valid - 1, 0)])
    tail = jnp.arange(TOP_K * t // tr, n_tiles, dtype=jnp.int32)
    zf_rows = jnp.concatenate([jnp.where(padded > 0, ends - tr, -1),
                               jnp.where(tail >= n_valid, tail * tr, -1)]).astype(jnp.int32)

    xs = moe_dispatch(h, gain, slots, zf_rows, n_slots, 512, tr)
    ys = moe_grouped(xs, tile_expert, n_valid.reshape(1), wg.astype(BF16), wu.astype(BF16), wd.astype(BF16),
                     tr, 512)
    return moe_combine(h, wt, slots, ys, 256)


def kernel(x, a_norm, a_w_in, a_conv, a_log_decay, a_dt_bias, a_out_norm, a_w_out, kv_norm, kv_w, k_norm,
           b_norm, b_w_q, q_norm, b_sinks, b_w_o, rel_bias, ffn_norm, dense_w_gate, dense_w_up, dense_w_down,
           moe_router, moe_w_gate, moe_w_up, moe_w_down):
    batch, seq, d = x.shape
    t = batch * seq
    nh, hd = LA_HEADS, LA_D
    main_w = 4 * nh * hd
    h0 = x.reshape(t, d)

    w_in = a_w_in[0]
    w_main = w_in[:, 0:main_w].astype(BF16)
    w_gate = jnp.zeros((d, LANES), BF16).at[:, 0:2 * nh].set(w_in[:, main_w:main_w + 2 * nh].astype(BF16))
    proj = norm_matmul(h0, a_norm[0], w_main, BF16, 1024, 1024, "gdn_in_proj")
    gates = norm_matmul(h0, a_norm[0], w_gate, F32, 1024, LANES, "gdn_gate_proj")
    o = gdn_core(proj, gates, a_conv[0], a_log_decay[0], a_dt_bias[0], a_out_norm[0], batch, seq)
    h1 = matmul_residual(o, a_w_out[0].astype(BF16), h0, 1024, 1024, "gdn_out_proj")

    h2 = ffn_dense(h1, ffn_norm[0], dense_w_gate[0].astype(BF16), dense_w_up[0].astype(BF16),
                   dense_w_down[0].astype(BF16), 1024, 512)

    kv = norm_matmul(h2, kv_norm, kv_w.astype(BF16), BF16, 1024, 512, "kv_proj")
    q = norm_matmul(h2, b_norm[0], b_w_q[0].astype(BF16), BF16, 1024, 1024, "q_proj")
    bias = bias_table(rel_bias)
    attn = swa_attention(q, kv, bias, q_norm[0], k_norm, b_sinks[0], batch, seq)
    h3 = matmul_residual(attn, b_w_o[0].astype(BF16), h2, 1024, 1024, "attn_out_proj")

    h4 = moe_layer(h3, ffn_norm[1], moe_router[0], moe_w_gate[0], moe_w_up[0], moe_w_down[0])
    return h4.reshape(batch, seq, d)
```

```python
import functools

import numpy as np
import jax
import jax.numpy as jnp
from jax import lax
from jax.experimental import pallas as pl
from jax.experimental.pallas import tpu as pltpu

F32 = jnp.float32
BF16 = jnp.bfloat16

EPS = 1e-6
NEG_INF = -1e30

LA_HEADS = 8
LA_D = 128
CONV_W = 4
CHUNK = 64
SW_HEADS = 16
SW_KV_HEADS = 4
SW_GROUP = SW_HEADS // SW_KV_HEADS
SW_HD = 64
WINDOW = 128
N_BUCKETS = 32
MAX_DIST = 128
N_EXPERTS = 8

LANES = 128
GDN_BLOCK = 2 * CHUNK
HALO = 8

VMEM_LIMIT = 56 * 1024 * 1024


def _cparams(sem):
    return pltpu.CompilerParams(dimension_semantics=sem, vmem_limit_bytes=VMEM_LIMIT)


def _silu(x):
    return x * (1.0 / (1.0 + jnp.exp(-x)))


def _dot(a, b):
    return jnp.dot(a, b, preferred_element_type=F32)


def _dot_nt(a, b):
    return lax.dot_general(a, b, (((1,), (1,)), ((), ())), preferred_element_type=F32)


def _norm_matmul_kernel(x_ref, g_ref, w_ref, o_ref, xn_ref):
    @pl.when(pl.program_id(1) == 0)
    def _():
        x = x_ref[...]
        ms = jnp.mean(x * x, axis=-1, keepdims=True)
        xn_ref[...] = ((x * lax.rsqrt(ms + EPS)) * g_ref[...]).astype(BF16)

    o_ref[...] = _dot(xn_ref[...], w_ref[...]).astype(o_ref.dtype)


def norm_matmul(x, gain, w, out_dtype, tm, tn, name):
    t, d = x.shape
    n = w.shape[1]
    tm, tn = min(tm, t), min(tn, n)
    return pl.pallas_call(
        _norm_matmul_kernel,
        out_shape=jax.ShapeDtypeStruct((t, n), out_dtype),
        grid=(t // tm, n // tn),
        in_specs=[pl.BlockSpec((tm, d), lambda i, j: (i, 0)),
                  pl.BlockSpec((1, d), lambda i, j: (0, 0)),
                  pl.BlockSpec((d, tn), lambda i, j: (0, j))],
        out_specs=pl.BlockSpec((tm, tn), lambda i, j: (i, j)),
        scratch_shapes=[pltpu.VMEM((tm, d), BF16)],
        compiler_params=_cparams(("parallel", "arbitrary")),
        name=name,
    )(x, gain.reshape(1, d), w)


def _matmul_res_kernel(a_ref, w_ref, r_ref, o_ref):
    o_ref[...] = r_ref[...] + _dot(a_ref[...], w_ref[...])


def matmul_residual(a, w, res, tm, tn, name):
    t, k = a.shape
    n = w.shape[1]
    tm, tn = min(tm, t), min(tn, n)
    return pl.pallas_call(
        _matmul_res_kernel,
        out_shape=jax.ShapeDtypeStruct((t, n), F32),
        grid=(t // tm, n // tn),
        in_specs=[pl.BlockSpec((tm, k), lambda i, j: (i, 0)),
                  pl.BlockSpec((k, tn), lambda i, j: (0, j)),
                  pl.BlockSpec((tm, tn), lambda i, j: (i, j))],
        out_specs=pl.BlockSpec((tm, tn), lambda i, j: (i, j)),
        compiler_params=_cparams(("parallel", "parallel")),
        name=name,
    )(a, w, res)


def _gdn_kernel(proj_ref, gates_ref, convw_ref, hp_ref, onorm_ref, o_ref, xs_ref, state_ref):
    nh, d, c = LA_HEADS, LA_D, CHUNK
    blk = GDN_BLOCK
    qkv_w = 3 * nh * d

    @pl.when(pl.program_id(1) == 0)
    def _():
        state_ref[...] = jnp.zeros_like(state_ref)
        xs_ref[0:HALO, :] = jnp.zeros((HALO, qkv_w), F32)

    xs_ref[HALO:HALO + blk, :] = proj_ref[:, 0:qkv_w].astype(F32)

    gates = gates_ref[...]
    a_log = hp_ref[0:1, :]
    dt_bias = hp_ref[1:2, :]
    beta = 1.0 / (1.0 + jnp.exp(-gates))
    sp_in = gates + dt_bias
    softplus = jnp.maximum(sp_in, 0.0) + jnp.log(1.0 + jnp.exp(-jnp.abs(sp_in)))
    g = -jnp.exp(a_log) * softplus

    row = lax.broadcasted_iota(jnp.int32, (blk, blk), 0)
    col = lax.broadcasted_iota(jnp.int32, (blk, blk), 1)
    tri = jnp.where((row >= col) & ((row // c) == (col // c)), 1.0, 0.0).astype(BF16)
    g_hi = g.astype(BF16)
    g_r1 = g - g_hi.astype(F32)
    g_mid = g_r1.astype(BF16)
    g_lo = (g_r1 - g_mid.astype(F32)).astype(BF16)
    gc = _dot(tri, g_hi) + _dot(tri, g_mid) + _dot(tri, g_lo)
    gc_t = gc.T

    ci = lax.broadcasted_iota(jnp.int32, (c, c), 0)
    cj = lax.broadcasted_iota(jnp.int32, (c, c), 1)
    lower_incl = ci >= cj
    strict = ci > cj
    eye_c = jnp.where(ci == cj, 1.0, 0.0).astype(F32)
    di = lax.broadcasted_iota(jnp.int32, (d, d), 0)
    dj = lax.broadcasted_iota(jnp.int32, (d, d), 1)
    eye_d = jnp.where(di == dj, 1.0, 0.0).astype(BF16)

    onorm = onorm_ref[...]

    def conv_silu(col0):
        acc = None
        for j in range(CONV_W):
            r0 = HALO - (CONV_W - 1) + j
            term = convw_ref[j:j + 1, col0:col0 + d] * xs_ref[r0:r0 + blk, col0:col0 + d]
            acc = term if acc is None else acc + term
        return _silu(acc)

    n_ck = blk // c
    chains = [(h, ck) for h in range(nh) for ck in range(n_ck)]
    heads = []
    for h in range(nh):
        qf = conv_silu(h * d)
        kf = conv_silu(nh * d + h * d)
        vf = conv_silu(2 * nh * d + h * d)
        qf = qf * lax.rsqrt(jnp.sum(qf * qf, axis=-1, keepdims=True) + EPS) * (d ** -0.5)
        kf = kf * lax.rsqrt(jnp.sum(kf * kf, axis=-1, keepdims=True) + EPS)
        heads.append((qf, kf, vf))

    st = {}
    for (h, ck) in chains:
        r = ck * c
        qf, kf, vf = heads[h]
        q, k, v = qf[r:r + c], kf[r:r + c], vf[r:r + c]
        g_col = gc[r:r + c, nh + h:nh + h + 1]
        g_row = gc_t[nh + h:nh + h + 1, r:r + c]
        g_last = gc[r + c - 1:r + c, nh + h:nh + h + 1]
        b_col = beta[r:r + c, h:h + 1]
        decay = jnp.where(lower_incl, jnp.exp(jnp.where(lower_incl, g_col - g_row, 0.0)), 0.0)
        k_beta = k * b_col
        e_col = jnp.exp(g_col)
        lhs = jnp.concatenate([k_beta.astype(BF16), q.astype(BF16), eye_d], axis=0)
        kk = _dot_nt(lhs, k.astype(BF16))
        a_mat = jnp.where(strict, kk[0:c] * decay, 0.0)
        st[(h, ck)] = dict(
            a=a_mat, attn=(kk[c:2 * c] * decay).astype(BF16),
            k_tail_t=(kk[2 * c:2 * c + d] * jnp.exp(g_last - g_row)).astype(BF16),
            rhs=jnp.concatenate([(v * b_col).astype(BF16), (k_beta * e_col).astype(BF16)], axis=1),
            qe=(q * e_col).astype(BF16), e_last=jnp.exp(g_last))

    for key in chains:
        x_b = (-st[key]["a"]).astype(BF16)
        st[key]["y"] = _dot(x_b, x_b)
        st[key]["p"] = eye_c - st[key]["a"]
    n_levels = int(np.log2(c))
    for lvl in range(1, n_levels):
        for key in chains:
            y_b = st[key]["y"].astype(BF16)
            p = st[key]["p"]
            if lvl + 1 < n_levels:
                zz = _dot(jnp.concatenate([y_b, p.astype(BF16)], axis=0), y_b)
                st[key]["y"] = zz[0:c]
                st[key]["p"] = p + zz[c:2 * c]
            else:
                st[key]["p"] = p + _dot(p.astype(BF16), y_b)
    for key in chains:
        st[key]["uw"] = _dot(st[key]["p"].astype(BF16), st[key]["rhs"])

    for ck in range(n_ck):
        r = ck * c
        s_old = [state_ref[h] for h in range(nh)]
        ws_qs = []
        for h in range(nh):
            cur = st[(h, ck)]
            lhs = jnp.concatenate([cur["uw"][:, d:2 * d].astype(BF16), cur["qe"]], axis=0)
            ws_qs.append(_dot(lhs, s_old[h].astype(BF16)))
        for h in range(nh):
            cur = st[(h, ck)]
            v_new = cur["uw"][:, 0:d] - ws_qs[h][0:c]
            av_kv = _dot(jnp.concatenate([cur["attn"], cur["k_tail_t"]], axis=0), v_new.astype(BF16))
            state_ref[h] = s_old[h] * cur["e_last"] + av_kv[c:c + d]
            o = ws_qs[h][c:2 * c] + av_kv[0:c]
            o = (o * lax.rsqrt(jnp.mean(o * o, axis=-1, keepdims=True) + EPS)) * onorm
            z = proj_ref[r:r + c, qkv_w + h * d:qkv_w + (h + 1) * d].astype(F32)
            o_ref[r:r + c, h * d:(h + 1) * d] = (o * _silu(z)).astype(o_ref.dtype)

    xs_ref[0:HALO, :] = xs_ref[blk:blk + HALO, :]


def gdn_core(proj, gates, conv_w, a_log, dt_bias, out_norm, batch, seq):
    t = proj.shape[0]
    nh, d = LA_HEADS, LA_D
    blk = GDN_BLOCK
    nblk = seq // blk
    hp = jnp.zeros((8, LANES), F32)
    hp = hp.at[0, nh:2 * nh].set(a_log.astype(F32)).at[1, nh:2 * nh].set(dt_bias.astype(F32))
    return pl.pallas_call(
        _gdn_kernel,
        out_shape=jax.ShapeDtypeStruct((t, nh * d), BF16),
        grid=(batch, nblk),
        in_specs=[pl.BlockSpec((blk, 4 * nh * d), lambda b, n: (b * nblk + n, 0)),
                  pl.BlockSpec((blk, LANES), lambda b, n: (b * nblk + n, 0)),
                  pl.BlockSpec((CONV_W, 3 * nh * d), lambda b, n: (0, 0)),
                  pl.BlockSpec((8, LANES), lambda b, n: (0, 0)),
                  pl.BlockSpec((1, d), lambda b, n: (0, 0))],
        out_specs=pl.BlockSpec((blk, nh * d), lambda b, n: (b * nblk + n, 0)),
        scratch_shapes=[pltpu.VMEM((HALO + blk, 3 * nh * d), F32),
                        pltpu.VMEM((nh, d, d), F32)],
        compiler_params=_cparams(("parallel", "arbitrary")),
        name="gdn_core",
    )(proj, gates, conv_w.astype(F32), hp, out_norm.reshape(1, d).astype(F32))


def _swiglu_kernel(te_ref, first_ref, nv_ref, x_ref, g_ref, wg_ref, wu_ref, wd_ref, o_ref,
                   xb_ref, acc_ref, wg_c, wu_c, wd_c, *, pre_norm):
    i = pl.program_id(0)
    j = pl.program_id(1)
    valid = i < nv_ref[0]
    last = j == pl.num_programs(1) - 1

    @pl.when(valid & (first_ref[i] == 1))
    def _():
        wg_c[j] = wg_ref[0].astype(BF16)
        wu_c[j] = wu_ref[0].astype(BF16)
        wd_c[j] = wd_ref[0].astype(BF16)

    @pl.when(valid & (j == 0))
    def _():
        x = x_ref[...].astype(F32)
        if pre_norm:
            ms = jnp.mean(x * x, axis=-1, keepdims=True)
            x = (x * lax.rsqrt(ms + EPS)) * g_ref[...]
        xb_ref[...] = x.astype(BF16)
        acc_ref[...] = jnp.zeros_like(acc_ref)

    @pl.when(valid)
    def _():
        xb = xb_ref[...]
        hid = _silu(_dot(xb, wg_c[j])) * _dot(xb, wu_c[j])
        acc_ref[...] += _dot(hid.astype(BF16), wd_c[j])

    @pl.when(valid & last)
    def _():
        if pre_norm:
            o_ref[...] = (x_ref[...] + acc_ref[...]).astype(o_ref.dtype)
        else:
            o_ref[...] = acc_ref[...].astype(o_ref.dtype)

    @pl.when(jnp.logical_not(valid) & last)
    def _():
        o_ref[...] = jnp.zeros_like(o_ref)


def expert_swiglu(x, gain, tile_expert, tile_first, n_valid, wg, wu, wd, tile_rows, tf, out_dtype, pre_norm, name):
    n_rows, d = x.shape
    ne, _, f = wg.shape
    n_tiles = n_rows // tile_rows
    nf = f // tf

    def x_map(i, j, te, fi, nv):
        return (jnp.minimum(i, nv[0] - 1), 0)

    def w_in_map(i, j, te, fi, nv):
        return (te[i], 0, jnp.where(fi[i] == 1, j, nf - 1))

    def w_out_map(i, j, te, fi, nv):
        return (te[i], jnp.where(fi[i] == 1, j, nf - 1), 0)

    grid_spec = pltpu.PrefetchScalarGridSpec(
        num_scalar_prefetch=3,
        grid=(n_tiles, nf),
        in_specs=[pl.BlockSpec((tile_rows, d), x_map),
                  pl.BlockSpec((1, d), lambda i, j, te, fi, nv: (0, 0)),
                  pl.BlockSpec((1, d, tf), w_in_map),
                  pl.BlockSpec((1, d, tf), w_in_map),
                  pl.BlockSpec((1, tf, d), w_out_map)],
        out_specs=pl.BlockSpec((tile_rows, d), lambda i, j, te, fi, nv: (i, 0)),
        scratch_shapes=[pltpu.VMEM((tile_rows, d), BF16), pltpu.VMEM((tile_rows, d), F32),
                        pltpu.VMEM((nf, d, tf), BF16), pltpu.VMEM((nf, d, tf), BF16), pltpu.VMEM((nf, tf, d), BF16)],
    )
    return pl.pallas_call(
        functools.partial(_swiglu_kernel, pre_norm=pre_norm),
        out_shape=jax.ShapeDtypeStruct((n_rows, d), out_dtype),
        grid_spec=grid_spec,
        compiler_params=_cparams(("arbitrary", "arbitrary")),
        name=name,
    )(tile_expert, tile_first, n_valid, x, gain.reshape(1, d).astype(F32), wg, wu, wd)


def ffn_dense(x, gain, wg, wu, wd, tm, tf):
    t = x.shape[0]
    n_tiles = t // tm
    tile_first = jnp.zeros((n_tiles,), jnp.int32).at[0].set(1)
    return expert_swiglu(x, gain, jnp.zeros((n_tiles,), jnp.int32), tile_first, jnp.full((1,), n_tiles, jnp.int32),
                         wg[None], wu[None], wd[None], tm, tf, F32, True, "ffn_dense")


def _t5_bucket_np(dist):
    max_exact = N_BUCKETS // 2
    n = np.maximum(dist, 0)
    safe = np.maximum(n, 1).astype(np.float32)
    large = max_exact + (np.log(safe / max_exact) / np.log(MAX_DIST / max_exact)
                         * (N_BUCKETS - max_exact)).astype(np.int32)
    large = np.minimum(large, N_BUCKETS - 1)
    return np.where(n < max_exact, n, large).astype(np.int32)


def _bias_kernel(bucket_ref, rb_ref, o_ref):
    bucket = bucket_ref[...]
    for h in range(SW_HEADS):
        acc = jnp.zeros(bucket.shape, F32)
        for b in range(N_BUCKETS):
            acc = jnp.where(bucket == b, rb_ref[b, h], acc)
        o_ref[h] = acc


def bias_table(rel_bias):
    qi = np.arange(WINDOW)[:, None] + WINDOW
    kj = np.arange(2 * WINDOW)[None, :]
    bucket = jnp.asarray(_t5_bucket_np(qi - kj))
    return pl.pallas_call(
        _bias_kernel,
        out_shape=jax.ShapeDtypeStruct((SW_HEADS, WINDOW, 2 * WINDOW), F32),
        in_specs=[pl.BlockSpec(memory_space=pltpu.VMEM), pl.BlockSpec(memory_space=pltpu.SMEM)],
        out_specs=pl.BlockSpec(memory_space=pltpu.VMEM),
        name="t5_bias_table",
    )(bucket, rel_bias.astype(F32))


def _swa_kernel(q_ref, kvp_ref, kvc_ref, bias_ref, qn_ref, kn_ref, sink_ref, o_ref):
    n = pl.program_id(1)
    blk, hd = WINDOW, SW_HD
    kv_w = SW_KV_HEADS * hd
    qi = lax.broadcasted_iota(jnp.int32, (blk, 2 * blk), 0) + blk
    kj = lax.broadcasted_iota(jnp.int32, (blk, 2 * blk), 1)
    dist = qi - kj
    first_key = jnp.where(n > 0, 0, blk)
    mask = (dist >= 0) & (dist < WINDOW) & (kj >= first_key)
    qn = qn_ref[...]
    kn = kn_ref[...]

    def head_norm(x, gain):
        return (x * lax.rsqrt(jnp.mean(x * x, axis=-1, keepdims=True) + EPS)) * gain

    ks, vs = [], []
    for g in range(SW_KV_HEADS):
        kp = kvp_ref[:, g * hd:(g + 1) * hd].astype(F32)
        kc = kvc_ref[:, g * hd:(g + 1) * hd].astype(F32)
        ks.append(jnp.concatenate([head_norm(kp, kn), head_norm(kc, kn)], axis=0).astype(BF16))
        vs.append(jnp.concatenate([kvp_ref[:, kv_w + g * hd:kv_w + (g + 1) * hd],
                                   kvc_ref[:, kv_w + g * hd:kv_w + (g + 1) * hd]], axis=0))
    scores = []
    for hq in range(SW_HEADS):
        q = head_norm(q_ref[:, hq * hd:(hq + 1) * hd].astype(F32), qn) * (hd ** -0.5)
        scores.append(_dot_nt(q.astype(BF16), ks[hq // SW_GROUP]))
    probs = []
    for hq in range(SW_HEADS):
        s = jnp.where(mask, scores[hq] + bias_ref[hq], NEG_INF)
        sink = sink_ref[hq]
        mx = jnp.maximum(jnp.max(s, axis=-1, keepdims=True), sink)
        p = jnp.exp(s - mx)
        denom = jnp.sum(p, axis=-1, keepdims=True) + jnp.exp(sink - mx)
        probs.append((p / denom).astype(BF16))
    for hq in range(SW_HEADS):
        o = _dot(probs[hq], vs[hq // SW_GROUP])
        o_ref[:, hq * hd:(hq + 1) * hd] = o.astype(o_ref.dtype)


def swa_attention(q, kv, bias, q_norm, k_norm, sinks, batch, seq):
    t = q.shape[0]
    blk = WINDOW
    nb = seq // blk
    qw = SW_HEADS * SW_HD
    kvw = 2 * SW_KV_HEADS * SW_HD
    return pl.pallas_call(
        _swa_kernel,
        out_shape=jax.ShapeDtypeStruct((t, qw), BF16),
        grid=(batch, nb),
        in_specs=[pl.BlockSpec((blk, qw), lambda b, n: (b * nb + n, 0)),
                  pl.BlockSpec((blk, kvw), lambda b, n: (b * nb + jnp.maximum(n - 1, 0), 0)),
                  pl.BlockSpec((blk, kvw), lambda b, n: (b * nb + n, 0)),
                  pl.BlockSpec((SW_HEADS, blk, 2 * blk), lambda b, n: (0, 0, 0)),
                  pl.BlockSpec((1, SW_HD), lambda b, n: (0, 0)),
                  pl.BlockSpec((1, SW_HD), lambda b, n: (0, 0)),
                  pl.BlockSpec(memory_space=pltpu.SMEM)],
        out_specs=pl.BlockSpec((blk, qw), lambda b, n: (b * nb + n, 0)),
        compiler_params=_cparams(("parallel", "parallel")),
        name="swa_attention",
    )(q, kv, kv, bias, q_norm.reshape(1, SW_HD).astype(F32), k_norm.reshape(1, SW_HD).astype(F32),
      sinks.astype(F32))


def _route_kernel(x_ref, g_ref, wr_ref, r_ref, wt_ref, cnt_ref, sel_s, gw_s, cnt_s, start_s, run_s, *, tile_rows):
    ne = N_EXPERTS
    p = pl.program_id(0)
    i = pl.program_id(1)
    tm = x_ref.shape[0]
    sub = lax.broadcasted_iota(jnp.int32, (ne, tm), 0).astype(F32)

    @pl.when(p == 0)
    def _():
        @pl.when(i == 0)
        def _():
            cnt_s[...] = jnp.zeros_like(cnt_s)

        x = x_ref[...]
        ms = jnp.mean(x * x, axis=-1, keepdims=True)
        xn32 = (x * lax.rsqrt(ms + EPS)) * g_ref[...]
        xn_hi = xn32.astype(BF16)
        xn_lo = (xn32 - xn_hi.astype(F32)).astype(BF16)
        p_hi = _dot_nt(wr_ref[...], xn_hi)
        p_lo = _dot_nt(wr_ref[...], xn_lo)
        logits = p_hi[0:ne] + p_hi[ne:2 * ne] + p_lo[0:ne]
        m1 = jnp.max(logits, axis=0, keepdims=True)
        i1 = jnp.min(jnp.where(logits == m1, sub, float(ne)), axis=0, keepdims=True)
        l2 = jnp.where(sub == i1, -jnp.inf, logits)
        m2 = jnp.max(l2, axis=0, keepdims=True)
        i2 = jnp.min(jnp.where(l2 == m2, sub, float(ne)), axis=0, keepdims=True)
        e2 = jnp.exp(m2 - m1)
        w1 = 1.0 / (1.0 + e2)
        w2 = e2 / (1.0 + e2)
        sel = jnp.where((sub == i1) | (sub == i2), 1.0, 0.0)
        sel_s[i] = sel
        gw_s[i] = jnp.where(sub == i1, w1, jnp.where(sub == i2, w2, 0.0))
        cnt_s[...] += jnp.sum(sel, axis=1, keepdims=True)

    @pl.when(p == 1)
    def _():
        @pl.when(i == 0)
        def _():
            cnt = cnt_s[...]
            padded = jnp.floor((cnt + (tile_rows - 1)) * (1.0 / tile_rows)) * tile_rows
            sub8 = lax.broadcasted_iota(jnp.int32, cnt.shape, 0)
            start = jnp.zeros_like(cnt)
            for e in range(ne - 1):
                start = start + jnp.where(sub8 > e, padded[e:e + 1, :], 0.0)
            start_s[...] = start
            run_s[...] = jnp.zeros_like(run_s)
            cnt_ref[...] = cnt

        sel = sel_s[i]
        gw = gw_s[i]
        ti = lax.broadcasted_iota(jnp.int32, (tm, tm), 0)
        tj = lax.broadcasted_iota(jnp.int32, (tm, tm), 1)
        tri = jnp.where(ti <= tj, 1.0, 0.0).astype(BF16)
        csum = _dot(sel.astype(BF16), tri)
        slot = start_s[:, 0:1] + run_s[:, 0:1] + csum - sel
        run_s[...] += csum[:, tm - 1:tm]
        ia = jnp.min(jnp.where(sel > 0.0, sub, float(ne)), axis=0, keepdims=True)
        ib = jnp.max(jnp.where(sel > 0.0, sub, -1.0), axis=0, keepdims=True)
        pick_a = sub == ia
        pick_b = sub == ib
        rows = [jnp.sum(jnp.where(pick_a, slot, 0.0), axis=0, keepdims=True),
                jnp.sum(jnp.where(pick_b, slot, 0.0), axis=0, keepdims=True),
                jnp.sum(jnp.where(pick_a, gw, 0.0), axis=0, keepdims=True),
                jnp.sum(jnp.where(pick_b, gw, 0.0), axis=0, keepdims=True)]
        r_ref[...] = jnp.concatenate(rows + [jnp.zeros((ne - 4, tm), F32)], axis=0)
        wpad = jnp.concatenate(rows[2:4] + [jnp.zeros((LANES - 2, tm), F32)], axis=0)
        wt_ref[...] = wpad.T


def moe_route(x, gain, w_router, tm, tile_rows):
    t, d = x.shape
    ne = w_router.shape[1]
    assert ne == N_EXPERTS
    w_hi = w_router.astype(BF16)
    w_lo = (w_router - w_hi.astype(F32)).astype(BF16)
    wr = jnp.concatenate([w_hi.T, w_lo.T], axis=0)
    tm = min(tm, t)
    nt = t // tm
    return pl.pallas_call(
        functools.partial(_route_kernel, tile_rows=tile_rows),
        out_shape=(jax.ShapeDtypeStruct((ne, t), F32), jax.ShapeDtypeStruct((t, LANES), F32),
                   jax.ShapeDtypeStruct((ne, LANES), F32)),
        grid=(2, nt),
        in_specs=[pl.BlockSpec((tm, d), lambda p, i: (i * (1 - p) + (nt - 1) * p, 0)),
                  pl.BlockSpec((1, d), lambda p, i: (0, 0)),
                  pl.BlockSpec((2 * ne, d), lambda p, i: (0, 0))],
        out_specs=(pl.BlockSpec((ne, tm), lambda p, i: (0, i * p)),
                   pl.BlockSpec((tm, LANES), lambda p, i: (i * p, 0)),
                   pl.BlockSpec((ne, LANES), lambda p, i: (0, 0))),
        scratch_shapes=[pltpu.VMEM((nt, ne, tm), F32), pltpu.VMEM((nt, ne, tm), F32),
                        pltpu.VMEM((ne, LANES), F32), pltpu.VMEM((ne, LANES), F32), pltpu.VMEM((ne, LANES), F32)],
        compiler_params=_cparams(("arbitrary", "arbitrary")),
        name="moe_route",
    )(x, gain.reshape(1, d), wr)


def _dispatch_kernel(zf_ref, slots_ref, x_ref, g_ref, xs_ref, xn_s, zero_s, sem, zsem, *, tile_rows):
    tm = x_ref.shape[0]

    @pl.when(pl.program_id(0) == 0)
    def _():
        zero_s[...] = jnp.zeros_like(zero_s)

        def zero_copy(e):
            row0 = pl.multiple_of(zf_ref[e], tile_rows)
            return pltpu.make_async_copy(zero_s, xs_ref.at[pl.ds(row0, tile_rows)], zsem)

        for e in range(zf_ref.shape[0]):
            @pl.when(zf_ref[e] >= 0)
            def _():
                zero_copy(e).start()
        for e in range(zf_ref.shape[0]):
            @pl.when(zf_ref[e] >= 0)
            def _():
                zero_copy(e).wait()

    x = x_ref[...]
    ms = jnp.mean(x * x, axis=-1, keepdims=True)
    xn_s[...] = (x * lax.rsqrt(ms + EPS)) * g_ref[...]

    def row_copy(r, k):
        return pltpu.make_async_copy(xn_s.at[pl.ds(r, 1)], xs_ref.at[pl.ds(slots_ref[0, k, r], 1)], sem)

    def start(r, c):
        row_copy(r, 0).start()
        row_copy(r, 1).start()
        return c

    def wait(r, c):
        row_copy(r, 0).wait()
        row_copy(r, 1).wait()
        return c

    lax.fori_loop(0, tm, start, 0, unroll=8)
    lax.fori_loop(0, tm, wait, 0, unroll=8)


def moe_dispatch(x, gain, slots, zf_rows, n_slots, tm, tile_rows):
    t, d = x.shape
    tm = min(tm, t)
    nt = t // tm
    slots3 = slots.reshape(2, nt, tm).transpose(1, 0, 2)
    grid_spec = pltpu.PrefetchScalarGridSpec(
        num_scalar_prefetch=1,
        grid=(nt,),
        in_specs=[pl.BlockSpec((1, 2, tm), lambda i, zf: (i, 0, 0), memory_space=pltpu.SMEM),
                  pl.BlockSpec((tm, d), lambda i, zf: (i, 0)),
                  pl.BlockSpec((1, d), lambda i, zf: (0, 0))],
        out_specs=pl.BlockSpec(memory_space=pl.ANY),
        scratch_shapes=[pltpu.VMEM((tm, d), F32), pltpu.VMEM((tile_rows, d), F32),
                        pltpu.SemaphoreType.DMA, pltpu.SemaphoreType.DMA],
    )
    return pl.pallas_call(
        functools.partial(_dispatch_kernel, tile_rows=tile_rows),
        out_shape=jax.ShapeDtypeStruct((n_slots, d), F32),
        grid_spec=grid_spec,
        compiler_params=_cparams(("arbitrary",)),
        name="moe_dispatch",
    )(zf_rows, slots3, x, gain.reshape(1, d))


def _combine_kernel(slots_ref, h_ref, wt_ref, ys_ref, o_ref, buf, sem):
    tm = h_ref.shape[0]

    def row_copy(r, k):
        return pltpu.make_async_copy(ys_ref.at[pl.ds(slots_ref[0, k, r], 1)], buf.at[k, pl.ds(r, 1)], sem)

    def start(r, c):
        row_copy(r, 0).start()
        row_copy(r, 1).start()
        return c

    def wait(r, c):
        row_copy(r, 0).wait()
        row_copy(r, 1).wait()
        return c

    lax.fori_loop(0, tm, start, 0, unroll=8)
    lax.fori_loop(0, tm, wait, 0, unroll=8)
    wt = wt_ref[...]
    o_ref[...] = h_ref[...] + wt[:, 0:1] * buf[0] + wt[:, 1:2] * buf[1]


def moe_combine(h, wt, slots, ys, tm):
    t, d = h.shape
    tm = min(tm, t)
    nt = t // tm
    slots3 = slots.reshape(2, nt, tm).transpose(1, 0, 2)
    return pl.pallas_call(
        _combine_kernel,
        out_shape=jax.ShapeDtypeStruct((t, d), F32),
        grid=(nt,),
        in_specs=[pl.BlockSpec((1, 2, tm), lambda i: (i, 0, 0), memory_space=pltpu.SMEM),
                  pl.BlockSpec((tm, d), lambda i: (i, 0)),
                  pl.BlockSpec((tm, LANES), lambda i: (i, 0)),
                  pl.BlockSpec(memory_space=pl.ANY)],
        out_specs=pl.BlockSpec((tm, d), lambda i: (i, 0)),
        scratch_shapes=[pltpu.VMEM((2, tm, d), F32), pltpu.SemaphoreType.DMA],
        compiler_params=_cparams(("arbitrary",)),
        name="moe_combine",
    )(slots3, h, wt, ys)


MOE_TILE_ROWS = 512
TOP_K = 2


def moe_layer(h, gain, w_router, wg, wu, wd):
    t, d = h.shape
    ne = w_router.shape[1]
    tr = MOE_TILE_ROWS
    n_tiles = -(-(TOP_K * t + ne * (tr - 1)) // tr)
    n_slots = n_tiles * tr

    r, wt, cnt = moe_route(h, gain, w_router, 512, tr)
    slots = r[0:2].astype(jnp.int32)

    counts = cnt[:, 0].astype(jnp.int32)
    padded = ((counts + (tr - 1)) // tr) * tr
    ends = jnp.cumsum(padded)
    n_valid = (ends[-1] // tr).astype(jnp.int32)
    tile_expert = jnp.searchsorted(ends, jnp.arange(n_tiles, dtype=jnp.int32) * tr, side="right")
    tile_expert = jnp.minimum(tile_expert, ne - 1).astype(jnp.int32)
    tile_expert = jnp.where(jnp.arange(n_tiles) < n_valid, tile_expert, tile_expert[jnp.maximum(n_valid - 1, 0)])
    prev_expert = jnp.concatenate([jnp.full((1,), -1, jnp.int32), tile_expert[:-1]])
    tile_first = (tile_expert != prev_expert).astype(jnp.int32)
    tail = jnp.arange(TOP_K * t // tr, n_tiles, dtype=jnp.int32)
    zf_rows = jnp.concatenate([jnp.where(padded > 0, ends - tr, -1),
                               jnp.where(tail >= n_valid, tail * tr, -1)]).astype(jnp.int32)

    xs = moe_dispatch(h, gain, slots, zf_rows, n_slots, 512, tr)
    ys = expert_swiglu(xs, gain, tile_expert, tile_first, n_valid.reshape(1), wg, wu, wd, tr, 512, F32, False,
                       "moe_experts")
    return moe_combine(h, wt, slots, ys, 256)


def kernel(x, a_norm, a_w_in, a_conv, a_log_decay, a_dt_bias, a_out_norm, a_w_out, kv_norm, kv_w, k_norm,
           b_norm, b_w_q, q_norm, b_sinks, b_w_o, rel_bias, ffn_norm, dense_w_gate, dense_w_up, dense_w_down,
           moe_router, moe_w_gate, moe_w_up, moe_w_down):
    batch, seq, d = x.shape
    t = batch * seq
    nh, hd = LA_HEADS, LA_D
    main_w = 4 * nh * hd
    h0 = x.reshape(t, d)

    w_in = a_w_in[0]
    w_main = w_in[:, 0:main_w].astype(BF16)
    w_gate = jnp.zeros((d, LANES), BF16).at[:, 0:2 * nh].set(w_in[:, main_w:main_w + 2 * nh].astype(BF16))
    proj = norm_matmul(h0, a_norm[0], w_main, BF16, 1024, 1024, "gdn_in_proj")
    gates = norm_matmul(h0, a_norm[0], w_gate, F32, 1024, LANES, "gdn_gate_proj")
    o = gdn_core(proj, gates, a_conv[0], a_log_decay[0], a_dt_bias[0], a_out_norm[0], batch, seq)
    h1 = matmul_residual(o, a_w_out[0].astype(BF16), h0, 1024, 1024, "gdn_out_proj")

    h2 = ffn_dense(h1, ffn_norm[0], dense_w_gate[0], dense_w_up[0], dense_w_down[0], 512, 512)

    kv = norm_matmul(h2, kv_norm, kv_w.astype(BF16), BF16, 1024, 512, "kv_proj")
    q = norm_matmul(h2, b_norm[0], b_w_q[0].astype(BF16), BF16, 1024, 1024, "q_proj")
    bias = bias_table(rel_bias)
    attn = swa_attention(q, kv, bias, q_norm[0], k_norm, b_sinks[0], batch, seq)
    h3 = matmul_residual(attn, b_w_o[0].astype(BF16), h2, 1024, 1024, "attn_out_proj")

    h4 = moe_layer(h3, ffn_norm[1], moe_router[0], moe_w_gate[0], moe_w_up[0], moe_w_down[0])
    return h4.reshape(batch, seq, d)
```

```python
import functools

import numpy as np
import jax
import jax.numpy as jnp
from jax import lax
from jax.experimental import pallas as pl
from jax.experimental.pallas import tpu as pltpu

F32 = jnp.float32
BF16 = jnp.bfloat16

EPS = 1e-6
NEG_INF = -1e30

LA_HEADS = 8
LA_D = 128
CONV_W = 4
CHUNK = 64
SW_HEADS = 16
SW_KV_HEADS = 4
SW_GROUP = SW_HEADS // SW_KV_HEADS
SW_HD = 64
WINDOW = 128
N_BUCKETS = 32
MAX_DIST = 128
N_EXPERTS = 8

LANES = 128
GDN_BLOCK = 2 * CHUNK
HALO = 8

VMEM_LIMIT = 56 * 1024 * 1024


def _cparams(sem):
    return pltpu.CompilerParams(dimension_semantics=sem, vmem_limit_bytes=VMEM_LIMIT)


def _silu(x):
    return x * (1.0 / (1.0 + jnp.exp(-x)))


def _dot(a, b):
    return jnp.dot(a, b, preferred_element_type=F32)


def _dot_nt(a, b):
    return lax.dot_general(a, b, (((1,), (1,)), ((), ())), preferred_element_type=F32)


def _norm_matmul_kernel(x_ref, g_ref, w_ref, o_ref, xn_ref):
    @pl.when(pl.program_id(1) == 0)
    def _():
        x = x_ref[...]
        ms = jnp.mean(x * x, axis=-1, keepdims=True)
        xn_ref[...] = ((x * lax.rsqrt(ms + EPS)) * g_ref[...]).astype(BF16)

    o_ref[...] = _dot(xn_ref[...], w_ref[...]).astype(o_ref.dtype)


def norm_matmul(x, gain, w, out_dtype, tm, tn, name):
    t, d = x.shape
    n = w.shape[1]
    tm, tn = min(tm, t), min(tn, n)
    return pl.pallas_call(
        _norm_matmul_kernel,
        out_shape=jax.ShapeDtypeStruct((t, n), out_dtype),
        grid=(t // tm, n // tn),
        in_specs=[pl.BlockSpec((tm, d), lambda i, j: (i, 0)),
                  pl.BlockSpec((1, d), lambda i, j: (0, 0)),
                  pl.BlockSpec((d, tn), lambda i, j: (0, j))],
        out_specs=pl.BlockSpec((tm, tn), lambda i, j: (i, j)),
        scratch_shapes=[pltpu.VMEM((tm, d), BF16)],
        compiler_params=_cparams(("parallel", "arbitrary")),
        name=name,
    )(x, gain.reshape(1, d), w)


def _matmul_res_kernel(a_ref, w_ref, r_ref, o_ref):
    o_ref[...] = r_ref[...] + _dot(a_ref[...], w_ref[...])


def matmul_residual(a, w, res, tm, tn, name):
    t, k = a.shape
    n = w.shape[1]
    tm, tn = min(tm, t), min(tn, n)
    return pl.pallas_call(
        _matmul_res_kernel,
        out_shape=jax.ShapeDtypeStruct((t, n), F32),
        grid=(t // tm, n // tn),
        in_specs=[pl.BlockSpec((tm, k), lambda i, j: (i, 0)),
                  pl.BlockSpec((k, tn), lambda i, j: (0, j)),
                  pl.BlockSpec((tm, tn), lambda i, j: (i, j))],
        out_specs=pl.BlockSpec((tm, tn), lambda i, j: (i, j)),
        compiler_params=_cparams(("parallel", "parallel")),
        name=name,
    )(a, w, res)


def _gdn_kernel(proj_ref, gates_ref, convw_ref, hp_ref, onorm_ref, o_ref, xs_ref, state_ref):
    nh, d, c = LA_HEADS, LA_D, CHUNK
    blk = GDN_BLOCK
    qkv_w = 3 * nh * d

    @pl.when(pl.program_id(1) == 0)
    def _():
        state_ref[...] = jnp.zeros_like(state_ref)
        xs_ref[0:HALO, :] = jnp.zeros((HALO, qkv_w), F32)

    xs_ref[HALO:HALO + blk, :] = proj_ref[:, 0:qkv_w].astype(F32)

    gates = gates_ref[...]
    a_log = hp_ref[0:1, :]
    dt_bias = hp_ref[1:2, :]
    beta = 1.0 / (1.0 + jnp.exp(-gates))
    sp_in = gates + dt_bias
    softplus = jnp.maximum(sp_in, 0.0) + jnp.log(1.0 + jnp.exp(-jnp.abs(sp_in)))
    g = -jnp.exp(a_log) * softplus

    row = lax.broadcasted_iota(jnp.int32, (blk, blk), 0)
    col = lax.broadcasted_iota(jnp.int32, (blk, blk), 1)
    tri = jnp.where((row >= col) & ((row // c) == (col // c)), 1.0, 0.0).astype(BF16)
    g_hi = g.astype(BF16)
    g_r1 = g - g_hi.astype(F32)
    g_mid = g_r1.astype(BF16)
    g_lo = (g_r1 - g_mid.astype(F32)).astype(BF16)
    gc = _dot(tri, g_hi) + _dot(tri, g_mid) + _dot(tri, g_lo)
    gc_t = gc.T

    ci = lax.broadcasted_iota(jnp.int32, (c, c), 0)
    cj = lax.broadcasted_iota(jnp.int32, (c, c), 1)
    lower_incl = ci >= cj
    strict = ci > cj
    eye_c = jnp.where(ci == cj, 1.0, 0.0).astype(F32)
    di = lax.broadcasted_iota(jnp.int32, (d, d), 0)
    dj = lax.broadcasted_iota(jnp.int32, (d, d), 1)
    eye_d = jnp.where(di == dj, 1.0, 0.0).astype(BF16)

    onorm = onorm_ref[...]

    def conv_silu(col0):
        acc = None
        for j in range(CONV_W):
            r0 = HALO - (CONV_W - 1) + j
            term = convw_ref[j:j + 1, col0:col0 + d] * xs_ref[r0:r0 + blk, col0:col0 + d]
            acc = term if acc is None else acc + term
        return _silu(acc)

    n_ck = blk // c
    chains = [(h, ck) for h in range(nh) for ck in range(n_ck)]
    heads = []
    for h in range(nh):
        qf = conv_silu(h * d)
        kf = conv_silu(nh * d + h * d)
        vf = conv_silu(2 * nh * d + h * d)
        qf = qf * lax.rsqrt(jnp.sum(qf * qf, axis=-1, keepdims=True) + EPS) * (d ** -0.5)
        kf = kf * lax.rsqrt(jnp.sum(kf * kf, axis=-1, keepdims=True) + EPS)
        heads.append((qf, kf, vf))

    st = {}
    for (h, ck) in chains:
        r = ck * c
        qf, kf, vf = heads[h]
        q, k, v = qf[r:r + c], kf[r:r + c], vf[r:r + c]
        g_col = gc[r:r + c, nh + h:nh + h + 1]
        g_row = gc_t[nh + h:nh + h + 1, r:r + c]
        g_last = gc[r + c - 1:r + c, nh + h:nh + h + 1]
        b_col = beta[r:r + c, h:h + 1]
        decay = jnp.where(lower_incl, jnp.exp(jnp.where(lower_incl, g_col - g_row, 0.0)), 0.0)
        k_beta = k * b_col
        e_col = jnp.exp(g_col)
        lhs = jnp.concatenate([k_beta.astype(BF16), q.astype(BF16), eye_d], axis=0)
        kk = _dot_nt(lhs, k.astype(BF16))
        a_mat = jnp.where(strict, kk[0:c] * decay, 0.0)
        st[(h, ck)] = dict(
            a=a_mat, attn=(kk[c:2 * c] * decay).astype(BF16),
            k_tail_t=(kk[2 * c:2 * c + d] * jnp.exp(g_last - g_row)).astype(BF16),
            rhs=jnp.concatenate([(v * b_col).astype(BF16), (k_beta * e_col).astype(BF16)], axis=1),
            qe=(q * e_col).astype(BF16), e_last=jnp.exp(g_last))

    for key in chains:
        x_b = (-st[key]["a"]).astype(BF16)
        st[key]["y"] = _dot(x_b, x_b)
        st[key]["p"] = eye_c - st[key]["a"]
    n_levels = int(np.log2(c))
    for lvl in range(1, n_levels):
        for key in chains:
            y_b = st[key]["y"].astype(BF16)
            p = st[key]["p"]
            if lvl + 1 < n_levels:
                zz = _dot(jnp.concatenate([y_b, p.astype(BF16)], axis=0), y_b)
                st[key]["y"] = zz[0:c]
                st[key]["p"] = p + zz[c:2 * c]
            else:
                st[key]["p"] = p + _dot(p.astype(BF16), y_b)
    for key in chains:
        st[key]["uw"] = _dot(st[key]["p"].astype(BF16), st[key]["rhs"])

    for ck in range(n_ck):
        r = ck * c
        s_old = [state_ref[h] for h in range(nh)]
        ws_qs = []
        for h in range(nh):
            cur = st[(h, ck)]
            lhs = jnp.concatenate([cur["uw"][:, d:2 * d].astype(BF16), cur["qe"]], axis=0)
            ws_qs.append(_dot(lhs, s_old[h].astype(BF16)))
        for h in range(nh):
            cur = st[(h, ck)]
            v_new = cur["uw"][:, 0:d] - ws_qs[h][0:c]
            av_kv = _dot(jnp.concatenate([cur["attn"], cur["k_tail_t"]], axis=0), v_new.astype(BF16))
            state_ref[h] = s_old[h] * cur["e_last"] + av_kv[c:c + d]
            o = ws_qs[h][c:2 * c] + av_kv[0:c]
            o = (o * lax.rsqrt(jnp.mean(o * o, axis=-1, keepdims=True) + EPS)) * onorm
            z = proj_ref[r:r + c, qkv_w + h * d:qkv_w + (h + 1) * d].astype(F32)
            o_ref[r:r + c, h * d:(h + 1) * d] = (o * _silu(z)).astype(o_ref.dtype)

    xs_ref[0:HALO, :] = xs_ref[blk:blk + HALO, :]


def gdn_core(proj, gates, conv_w, a_log, dt_bias, out_norm, batch, seq):
    t = proj.shape[0]
    nh, d = LA_HEADS, LA_D
    blk = GDN_BLOCK
    nblk = seq // blk
    hp = jnp.zeros((8, LANES), F32)
    hp = hp.at[0, nh:2 * nh].set(a_log.astype(F32)).at[1, nh:2 * nh].set(dt_bias.astype(F32))
    return pl.pallas_call(
        _gdn_kernel,
        out_shape=jax.ShapeDtypeStruct((t, nh * d), BF16),
        grid=(batch, nblk),
        in_specs=[pl.BlockSpec((blk, 4 * nh * d), lambda b, n: (b * nblk + n, 0)),
                  pl.BlockSpec((blk, LANES), lambda b, n: (b * nblk + n, 0)),
                  pl.BlockSpec((CONV_W, 3 * nh * d), lambda b, n: (0, 0)),
                  pl.BlockSpec((8, LANES), lambda b, n: (0, 0)),
                  pl.BlockSpec((1, d), lambda b, n: (0, 0))],
        out_specs=pl.BlockSpec((blk, nh * d), lambda b, n: (b * nblk + n, 0)),
        scratch_shapes=[pltpu.VMEM((HALO + blk, 3 * nh * d), F32),
                        pltpu.VMEM((nh, d, d), F32)],
        compiler_params=_cparams(("parallel", "arbitrary")),
        name="gdn_core",
    )(proj, gates, conv_w.astype(F32), hp, out_norm.reshape(1, d).astype(F32))


def _swiglu_kernel(te_ref, first_ref, nv_ref, x_ref, g_ref, wg_hbm, wu_hbm, wd_hbm, o_ref,
                   wg_c, wu_c, wd_c, stage_in, stage_out, sems, *, pre_norm, tf):
    i = pl.program_id(0)
    nf = wg_c.shape[0]
    e = te_ref[i]
    valid = i < nv_ref[0]

    def chunk_copies(j, slot):
        cols = pl.ds(j * tf, tf)
        return (pltpu.make_async_copy(wg_hbm.at[e, :, cols], stage_in.at[slot, 0], sems.at[slot, 0]),
                pltpu.make_async_copy(wu_hbm.at[e, :, cols], stage_in.at[slot, 1], sems.at[slot, 1]),
                pltpu.make_async_copy(wd_hbm.at[e, cols, :], stage_out.at[slot], sems.at[slot, 2]))

    def prepare_rows():
        x = x_ref[...].astype(F32)
        if pre_norm:
            ms = jnp.mean(x * x, axis=-1, keepdims=True)
            x = (x * lax.rsqrt(ms + EPS)) * g_ref[...]
        return x.astype(BF16)

    def chunk(xb, j):
        hid = _silu(_dot(xb, wg_c[j])) * _dot(xb, wu_c[j])
        return _dot(hid.astype(BF16), wd_c[j])

    def finish(acc):
        if pre_norm:
            o_ref[...] = (x_ref[...] + acc).astype(o_ref.dtype)
        else:
            o_ref[...] = acc.astype(o_ref.dtype)

    @pl.when(valid & (first_ref[i] == 1))
    def _():
        for c in chunk_copies(0, 0):
            c.start()
        xb = prepare_rows()
        acc = None
        for j in range(nf):
            slot = j % 2
            if j + 1 < nf:
                for c in chunk_copies(j + 1, 1 - slot):
                    c.start()
            for c in chunk_copies(j, slot):
                c.wait()
            wg_c[j] = stage_in[slot, 0].astype(BF16)
            wu_c[j] = stage_in[slot, 1].astype(BF16)
            wd_c[j] = stage_out[slot].astype(BF16)
            y = chunk(xb, j)
            acc = y if acc is None else acc + y
        finish(acc)

    @pl.when(valid & (first_ref[i] != 1))
    def _():
        xb = prepare_rows()
        acc = None
        for j in range(nf):
            y = chunk(xb, j)
            acc = y if acc is None else acc + y
        finish(acc)

    @pl.when(jnp.logical_not(valid))
    def _():
        o_ref[...] = jnp.zeros_like(o_ref)


def expert_swiglu(x, gain, tile_expert, tile_first, n_valid, wg, wu, wd, tile_rows, tf, out_dtype, pre_norm, name):
    n_rows, d = x.shape
    ne, _, f = wg.shape
    n_tiles = n_rows // tile_rows
    nf = f // tf
    grid_spec = pltpu.PrefetchScalarGridSpec(
        num_scalar_prefetch=3,
        grid=(n_tiles,),
        in_specs=[pl.BlockSpec((tile_rows, d), lambda i, te, fi, nv: (jnp.minimum(i, nv[0] - 1), 0)),
                  pl.BlockSpec((1, d), lambda i, te, fi, nv: (0, 0)),
                  pl.BlockSpec(memory_space=pl.ANY),
                  pl.BlockSpec(memory_space=pl.ANY),
                  pl.BlockSpec(memory_space=pl.ANY)],
        out_specs=pl.BlockSpec((tile_rows, d), lambda i, te, fi, nv: (i, 0)),
        scratch_shapes=[pltpu.VMEM((nf, d, tf), BF16), pltpu.VMEM((nf, d, tf), BF16), pltpu.VMEM((nf, tf, d), BF16),
                        pltpu.VMEM((2, 2, d, tf), F32), pltpu.VMEM((2, tf, d), F32),
                        pltpu.SemaphoreType.DMA((2, 3))],
    )
    return pl.pallas_call(
        functools.partial(_swiglu_kernel, pre_norm=pre_norm, tf=tf),
        out_shape=jax.ShapeDtypeStruct((n_rows, d), out_dtype),
        grid_spec=grid_spec,
        compiler_params=_cparams(("arbitrary",)),
        name=name,
    )(tile_expert, tile_first, n_valid, x, gain.reshape(1, d).astype(F32), wg, wu, wd)


def ffn_dense(x, gain, wg, wu, wd, tm, tf):
    t = x.shape[0]
    n_tiles = t // tm
    tile_first = jnp.zeros((n_tiles,), jnp.int32).at[0].set(1)
    return expert_swiglu(x, gain, jnp.zeros((n_tiles,), jnp.int32), tile_first, jnp.full((1,), n_tiles, jnp.int32),
                         wg[None], wu[None], wd[None], tm, tf, F32, True, "ffn_dense")


def _t5_bucket_np(dist):
    max_exact = N_BUCKETS // 2
    n = np.maximum(dist, 0)
    safe = np.maximum(n, 1).astype(np.float32)
    large = max_exact + (np.log(safe / max_exact) / np.log(MAX_DIST / max_exact)
                         * (N_BUCKETS - max_exact)).astype(np.int32)
    large = np.minimum(large, N_BUCKETS - 1)
    return np.where(n < max_exact, n, large).astype(np.int32)


def _bias_kernel(bucket_ref, rb_ref, o_ref):
    bucket = bucket_ref[...]
    for h in range(SW_HEADS):
        acc = jnp.zeros(bucket.shape, F32)
        for b in range(N_BUCKETS):
            acc = jnp.where(bucket == b, rb_ref[b, h], acc)
        o_ref[h] = acc


def bias_table(rel_bias):
    qi = np.arange(WINDOW)[:, None] + WINDOW
    kj = np.arange(2 * WINDOW)[None, :]
    bucket = jnp.asarray(_t5_bucket_np(qi - kj))
    return pl.pallas_call(
        _bias_kernel,
        out_shape=jax.ShapeDtypeStruct((SW_HEADS, WINDOW, 2 * WINDOW), F32),
        in_specs=[pl.BlockSpec(memory_space=pltpu.VMEM), pl.BlockSpec(memory_space=pltpu.SMEM)],
        out_specs=pl.BlockSpec(memory_space=pltpu.VMEM),
        name="t5_bias_table",
    )(bucket, rel_bias.astype(F32))


def _swa_kernel(q_ref, kvp_ref, kvc_ref, bias_ref, qn_ref, kn_ref, sink_ref, o_ref):
    n = pl.program_id(1)
    blk, hd = WINDOW, SW_HD
    kv_w = SW_KV_HEADS * hd
    qi = lax.broadcasted_iota(jnp.int32, (blk, 2 * blk), 0) + blk
    kj = lax.broadcasted_iota(jnp.int32, (blk, 2 * blk), 1)
    dist = qi - kj
    first_key = jnp.where(n > 0, 0, blk)
    mask = (dist >= 0) & (dist < WINDOW) & (kj >= first_key)
    gw = 2 * LANES
    gi = lax.broadcasted_iota(jnp.int32, (gw, gw), 0)
    gj = lax.broadcasted_iota(jnp.int32, (gw, gw), 1)
    group_ones = jnp.where((gi // hd) == (gj // hd), 1.0, 0.0).astype(BF16)
    lane = lax.broadcasted_iota(jnp.int32, (1, LANES), 1)
    low_half = lane < hd

    def head_norm(x, gain):
        cols = []
        for c0 in range(0, x.shape[1], gw):
            xc = x[:, c0:c0 + gw]
            ss = _dot((xc * xc).astype(BF16), group_ones)
            cols.append(xc * lax.rsqrt(ss * (1.0 / hd) + EPS))
        return jnp.concatenate(cols, axis=1) * gain

    def dup_half(x, half):
        swapped = pltpu.roll(x, hd, 1)
        return jnp.where(low_half == (half == 0), x, swapped)

    qn = head_norm(q_ref[...].astype(F32), qn_ref[...]) * (hd ** -0.5)
    half_sel = [jnp.where(low_half, 1.0, 0.0), jnp.where(low_half, 0.0, 1.0)]
    k_all = jnp.concatenate([kvp_ref[:, 0:kv_w], kvc_ref[:, 0:kv_w]], axis=0).astype(F32)
    kn = head_norm(k_all, kn_ref[...])
    v_all = jnp.concatenate([kvp_ref[:, kv_w:2 * kv_w], kvc_ref[:, kv_w:2 * kv_w]], axis=0).astype(F32)
    ks, vs = [], []
    for g in range(SW_KV_HEADS):
        c0 = (g // 2) * LANES
        ks.append(dup_half(kn[:, c0:c0 + LANES], g % 2).astype(BF16))
        vs.append(dup_half(v_all[:, c0:c0 + LANES], g % 2).astype(BF16))

    scores = []
    for hq in range(SW_HEADS):
        c0 = (hq // 2) * LANES
        q_h = (qn[:, c0:c0 + LANES] * half_sel[hq % 2]).astype(BF16)
        scores.append(_dot_nt(q_h, ks[hq // SW_GROUP]))
    probs = []
    for hq in range(SW_HEADS):
        s = jnp.where(mask, scores[hq] + bias_ref[hq], NEG_INF)
        sink = sink_ref[hq]
        mx = jnp.maximum(jnp.max(s, axis=-1, keepdims=True), sink)
        p = jnp.exp(s - mx)
        denom = jnp.sum(p, axis=-1, keepdims=True) + jnp.exp(sink - mx)
        probs.append((p / denom).astype(BF16))
    outs = [_dot(probs[hq], vs[hq // SW_GROUP]) for hq in range(SW_HEADS)]
    for c in range(SW_HEADS // 2):
        o_ref[:, c * LANES:(c + 1) * LANES] = jnp.where(low_half, outs[2 * c], outs[2 * c + 1]).astype(o_ref.dtype)


def swa_attention(q, kv, bias, q_norm, k_norm, sinks, batch, seq):
    t = q.shape[0]
    blk = WINDOW
    nb = seq // blk
    qw = SW_HEADS * SW_HD
    kvw = 2 * SW_KV_HEADS * SW_HD
    return pl.pallas_call(
        _swa_kernel,
        out_shape=jax.ShapeDtypeStruct((t, qw), BF16),
        grid=(batch, nb),
        in_specs=[pl.BlockSpec((blk, qw), lambda b, n: (b * nb + n, 0)),
                  pl.BlockSpec((blk, kvw), lambda b, n: (b * nb + jnp.maximum(n - 1, 0), 0)),
                  pl.BlockSpec((blk, kvw), lambda b, n: (b * nb + n, 0)),
                  pl.BlockSpec((SW_HEADS, blk, 2 * blk), lambda b, n: (0, 0, 0)),
                  pl.BlockSpec((1, qw), lambda b, n: (0, 0)),
                  pl.BlockSpec((1, kvw // 2), lambda b, n: (0, 0)),
                  pl.BlockSpec(memory_space=pltpu.SMEM)],
        out_specs=pl.BlockSpec((blk, qw), lambda b, n: (b * nb + n, 0)),
        compiler_params=_cparams(("parallel", "parallel")),
        name="swa_attention",
    )(q, kv, kv, bias, jnp.tile(q_norm.astype(F32), SW_HEADS).reshape(1, qw),
      jnp.tile(k_norm.astype(F32), SW_KV_HEADS).reshape(1, kvw // 2), sinks.astype(F32))


def _route_kernel(x_ref, g_ref, wr_ref, r_ref, wt_ref, cnt_ref, sel_s, gw_s, cnt_s, start_s, run_s, *, tile_rows):
    ne = N_EXPERTS
    p = pl.program_id(0)
    i = pl.program_id(1)
    tm = x_ref.shape[0]
    sub = lax.broadcasted_iota(jnp.int32, (ne, tm), 0).astype(F32)

    @pl.when(p == 0)
    def _():
        @pl.when(i == 0)
        def _():
            cnt_s[...] = jnp.zeros_like(cnt_s)

        x = x_ref[...]
        ms = jnp.mean(x * x, axis=-1, keepdims=True)
        xn32 = (x * lax.rsqrt(ms + EPS)) * g_ref[...]
        xn_hi = xn32.astype(BF16)
        xn_lo = (xn32 - xn_hi.astype(F32)).astype(BF16)
        p_hi = _dot_nt(wr_ref[...], xn_hi)
        p_lo = _dot_nt(wr_ref[...], xn_lo)
        logits = p_hi[0:ne] + p_hi[ne:2 * ne] + p_lo[0:ne]
        m1 = jnp.max(logits, axis=0, keepdims=True)
        i1 = jnp.min(jnp.where(logits == m1, sub, float(ne)), axis=0, keepdims=True)
        l2 = jnp.where(sub == i1, -jnp.inf, logits)
        m2 = jnp.max(l2, axis=0, keepdims=True)
        i2 = jnp.min(jnp.where(l2 == m2, sub, float(ne)), axis=0, keepdims=True)
        e2 = jnp.exp(m2 - m1)
        w1 = 1.0 / (1.0 + e2)
        w2 = e2 / (1.0 + e2)
        sel = jnp.where((sub == i1) | (sub == i2), 1.0, 0.0)
        sel_s[i] = sel
        gw_s[i] = jnp.where(sub == i1, w1, jnp.where(sub == i2, w2, 0.0))
        cnt_s[...] += jnp.sum(sel, axis=1, keepdims=True)

    @pl.when(p == 1)
    def _():
        @pl.when(i == 0)
        def _():
            cnt = cnt_s[...]
            padded = jnp.floor((cnt + (tile_rows - 1)) * (1.0 / tile_rows)) * tile_rows
            sub8 = lax.broadcasted_iota(jnp.int32, cnt.shape, 0)
            start = jnp.zeros_like(cnt)
            for e in range(ne - 1):
                start = start + jnp.where(sub8 > e, padded[e:e + 1, :], 0.0)
            start_s[...] = start
            run_s[...] = jnp.zeros_like(run_s)
            cnt_ref[...] = cnt

        sel = sel_s[i]
        gw = gw_s[i]
        ti = lax.broadcasted_iota(jnp.int32, (tm, tm), 0)
        tj = lax.broadcasted_iota(jnp.int32, (tm, tm), 1)
        tri = jnp.where(ti <= tj, 1.0, 0.0).astype(BF16)
        csum = _dot(sel.astype(BF16), tri)
        slot = start_s[:, 0:1] + run_s[:, 0:1] + csum - sel
        run_s[...] += csum[:, tm - 1:tm]
        ia = jnp.min(jnp.where(sel > 0.0, sub, float(ne)), axis=0, keepdims=True)
        ib = jnp.max(jnp.where(sel > 0.0, sub, -1.0), axis=0, keepdims=True)
        pick_a = sub == ia
        pick_b = sub == ib
        rows = [jnp.sum(jnp.where(pick_a, slot, 0.0), axis=0, keepdims=True),
                jnp.sum(jnp.where(pick_b, slot, 0.0), axis=0, keepdims=True),
                jnp.sum(jnp.where(pick_a, gw, 0.0), axis=0, keepdims=True),
                jnp.sum(jnp.where(pick_b, gw, 0.0), axis=0, keepdims=True)]
        r_ref[...] = jnp.concatenate(rows + [jnp.zeros((ne - 4, tm), F32)], axis=0)
        wpad = jnp.concatenate(rows[2:4] + [jnp.zeros((LANES - 2, tm), F32)], axis=0)
        wt_ref[...] = wpad.T


def moe_route(x, gain, w_router, tm, tile_rows):
    t, d = x.shape
    ne = w_router.shape[1]
    assert ne == N_EXPERTS
    w_hi = w_router.astype(BF16)
    w_lo = (w_router - w_hi.astype(F32)).astype(BF16)
    wr = jnp.concatenate([w_hi.T, w_lo.T], axis=0)
    tm = min(tm, t)
    nt = t // tm
    return pl.pallas_call(
        functools.partial(_route_kernel, tile_rows=tile_rows),
        out_shape=(jax.ShapeDtypeStruct((ne, t), F32), jax.ShapeDtypeStruct((t, LANES), F32),
                   jax.ShapeDtypeStruct((ne, LANES), F32)),
        grid=(2, nt),
        in_specs=[pl.BlockSpec((tm, d), lambda p, i: (i * (1 - p) + (nt - 1) * p, 0)),
                  pl.BlockSpec((1, d), lambda p, i: (0, 0)),
                  pl.BlockSpec((2 * ne, d), lambda p, i: (0, 0))],
        out_specs=(pl.BlockSpec((ne, tm), lambda p, i: (0, i * p)),
                   pl.BlockSpec((tm, LANES), lambda p, i: (i * p, 0)),
                   pl.BlockSpec((ne, LANES), lambda p, i: (0, 0))),
        scratch_shapes=[pltpu.VMEM((nt, ne, tm), F32), pltpu.VMEM((nt, ne, tm), F32),
                        pltpu.VMEM((ne, LANES), F32), pltpu.VMEM((ne, LANES), F32), pltpu.VMEM((ne, LANES), F32)],
        compiler_params=_cparams(("arbitrary", "arbitrary")),
        name="moe_route",
    )(x, gain.reshape(1, d), wr)


def _dispatch_kernel(zf_ref, slots_ref, x_ref, g_ref, xs_ref, xn_s, zero_s, sem, zsem, *, tile_rows):
    tm = x_ref.shape[0]

    @pl.when(pl.program_id(0) == 0)
    def _():
        zero_s[...] = jnp.zeros_like(zero_s)

        def zero_copy(e):
            row0 = pl.multiple_of(zf_ref[e], tile_rows)
            return pltpu.make_async_copy(zero_s, xs_ref.at[pl.ds(row0, tile_rows)], zsem)

        for e in range(zf_ref.shape[0]):
            @pl.when(zf_ref[e] >= 0)
            def _():
                zero_copy(e).start()
        for e in range(zf_ref.shape[0]):
            @pl.when(zf_ref[e] >= 0)
            def _():
                zero_copy(e).wait()

    x = x_ref[...]
    ms = jnp.mean(x * x, axis=-1, keepdims=True)
    xn_s[...] = (x * lax.rsqrt(ms + EPS)) * g_ref[...]

    def row_copy(r, k):
        return pltpu.make_async_copy(xn_s.at[pl.ds(r, 1)], xs_ref.at[pl.ds(slots_ref[0, k, r], 1)], sem)

    def start(r, c):
        row_copy(r, 0).start()
        row_copy(r, 1).start()
        return c

    def wait(r, c):
        row_copy(r, 0).wait()
        row_copy(r, 1).wait()
        return c

    lax.fori_loop(0, tm, start, 0, unroll=8)
    lax.fori_loop(0, tm, wait, 0, unroll=8)


def moe_dispatch(x, gain, slots, zf_rows, n_slots, tm, tile_rows):
    t, d = x.shape
    tm = min(tm, t)
    nt = t // tm
    slots3 = slots.reshape(2, nt, tm).transpose(1, 0, 2)
    grid_spec = pltpu.PrefetchScalarGridSpec(
        num_scalar_prefetch=1,
        grid=(nt,),
        in_specs=[pl.BlockSpec((1, 2, tm), lambda i, zf: (i, 0, 0), memory_space=pltpu.SMEM),
                  pl.BlockSpec((tm, d), lambda i, zf: (i, 0)),
                  pl.BlockSpec((1, d), lambda i, zf: (0, 0))],
        out_specs=pl.BlockSpec(memory_space=pl.ANY),
        scratch_shapes=[pltpu.VMEM((tm, d), F32), pltpu.VMEM((tile_rows, d), F32),
                        pltpu.SemaphoreType.DMA, pltpu.SemaphoreType.DMA],
    )
    return pl.pallas_call(
        functools.partial(_dispatch_kernel, tile_rows=tile_rows),
        out_shape=jax.ShapeDtypeStruct((n_slots, d), F32),
        grid_spec=grid_spec,
        compiler_params=_cparams(("arbitrary",)),
        name="moe_dispatch",
    )(zf_rows, slots3, x, gain.reshape(1, d))


def _combine_kernel(slots_ref, h_ref, wt_ref, ys_ref, o_ref, buf, sem):
    tm = h_ref.shape[0]

    def row_copy(r, k):
        return pltpu.make_async_copy(ys_ref.at[pl.ds(slots_ref[0, k, r], 1)], buf.at[k, pl.ds(r, 1)], sem)

    def start(r, c):
        row_copy(r, 0).start()
        row_copy(r, 1).start()
        return c

    def wait(r, c):
        row_copy(r, 0).wait()
        row_copy(r, 1).wait()
        return c

    lax.fori_loop(0, tm, start, 0, unroll=8)
    lax.fori_loop(0, tm, wait, 0, unroll=8)
    wt = wt_ref[...]
    o_ref[...] = h_ref[...] + wt[:, 0:1] * buf[0] + wt[:, 1:2] * buf[1]


def moe_combine(h, wt, slots, ys, tm):
    t, d = h.shape
    tm = min(tm, t)
    nt = t // tm
    slots3 = slots.reshape(2, nt, tm).transpose(1, 0, 2)
    return pl.pallas_call(
        _combine_kernel,
        out_shape=jax.ShapeDtypeStruct((t, d), F32),
        grid=(nt,),
        in_specs=[pl.BlockSpec((1, 2, tm), lambda i: (i, 0, 0), memory_space=pltpu.SMEM),
                  pl.BlockSpec((tm, d), lambda i: (i, 0)),
                  pl.BlockSpec((tm, LANES), lambda i: (i, 0)),
                  pl.BlockSpec(memory_space=pl.ANY)],
        out_specs=pl.BlockSpec((tm, d), lambda i: (i, 0)),
        scratch_shapes=[pltpu.VMEM((2, tm, d), F32), pltpu.SemaphoreType.DMA],
        compiler_params=_cparams(("arbitrary",)),
        name="moe_combine",
    )(slots3, h, wt, ys)


MOE_TILE_ROWS = 512
TOP_K = 2


def moe_layer(h, gain, w_router, wg, wu, wd):
    t, d = h.shape
    ne = w_router.shape[1]
    tr = MOE_TILE_ROWS
    n_tiles = -(-(TOP_K * t + ne * (tr - 1)) // tr)
    n_slots = n_tiles * tr

    r, wt, cnt = moe_route(h, gain, w_router, 512, tr)
    slots = r[0:2].astype(jnp.int32)

    counts = cnt[:, 0].astype(jnp.int32)
    padded = ((counts + (tr - 1)) // tr) * tr
    ends = jnp.cumsum(padded)
    n_valid = (ends[-1] // tr).astype(jnp.int32)
    tile_expert = jnp.searchsorted(ends, jnp.arange(n_tiles, dtype=jnp.int32) * tr, side="right")
    tile_expert = jnp.minimum(tile_expert, ne - 1).astype(jnp.int32)
    tile_expert = jnp.where(jnp.arange(n_tiles) < n_valid, tile_expert, tile_expert[jnp.maximum(n_valid - 1, 0)])
    prev_expert = jnp.concatenate([jnp.full((1,), -1, jnp.int32), tile_expert[:-1]])
    tile_first = (tile_expert != prev_expert).astype(jnp.int32)
    tail = jnp.arange(TOP_K * t // tr, n_tiles, dtype=jnp.int32)
    zf_rows = jnp.concatenate([jnp.where(padded > 0, ends - tr, -1),
                               jnp.where(tail >= n_valid, tail * tr, -1)]).astype(jnp.int32)

    xs = moe_dispatch(h, gain, slots, zf_rows, n_slots, 512, tr)
    ys = expert_swiglu(xs, gain, tile_expert, tile_first, n_valid.reshape(1), wg, wu, wd, tr, 512, F32, False,
                       "moe_experts")
    return moe_combine(h, wt, slots, ys, 256)


def kernel(x, a_norm, a_w_in, a_conv, a_log_decay, a_dt_bias, a_out_norm, a_w_out, kv_norm, kv_w, k_norm,
           b_norm, b_w_q, q_norm, b_sinks, b_w_o, rel_bias, ffn_norm, dense_w_gate, dense_w_up, dense_w_down,
           moe_router, moe_w_gate, moe_w_up, moe_w_down):
    batch, seq, d = x.shape
    t = batch * seq
    nh, hd = LA_HEADS, LA_D
    main_w = 4 * nh * hd
    h0 = x.reshape(t, d)

    w_in = a_w_in[0]
    w_main = w_in[:, 0:main_w].astype(BF16)
    w_gate = jnp.zeros((d, LANES), BF16).at[:, 0:2 * nh].set(w_in[:, main_w:main_w + 2 * nh].astype(BF16))
    proj = norm_matmul(h0, a_norm[0], w_main, BF16, 1024, 1024, "gdn_in_proj")
    gates = norm_matmul(h0, a_norm[0], w_gate, F32, 1024, LANES, "gdn_gate_proj")
    o = gdn_core(proj, gates, a_conv[0], a_log_decay[0], a_dt_bias[0], a_out_norm[0], batch, seq)
    h1 = matmul_residual(o, a_w_out[0].astype(BF16), h0, 1024, 1024, "gdn_out_proj")

    h2 = ffn_dense(h1, ffn_norm[0], dense_w_gate[0], dense_w_up[0], dense_w_down[0], 512, 512)

    kv = norm_matmul(h2, kv_norm, kv_w.astype(BF16), BF16, 1024, 512, "kv_proj")
    q = norm_matmul(h2, b_norm[0], b_w_q[0].astype(BF16), BF16, 1024, 1024, "q_proj")
    bias = bias_table(rel_bias)
    attn = swa_attention(q, kv, bias, q_norm[0], k_norm, b_sinks[0], batch, seq)
    h3 = matmul_residual(attn, b_w_o[0].astype(BF16), h2, 1024, 1024, "attn_out_proj")

    h4 = moe_layer(h3, ffn_norm[1], moe_router[0], moe_w_gate[0], moe_w_up[0], moe_w_down[0])
    return h4.reshape(batch, seq, d)
```

```python
import functools

import numpy as np
import jax
import jax.numpy as jnp
from jax import lax
from jax.experimental import pallas as pl
from jax.experimental.pallas import tpu as pltpu

F32 = jnp.float32
BF16 = jnp.bfloat16

EPS = 1e-6
NEG_INF = -1e30

LA_HEADS = 8
LA_D = 128
CONV_W = 4
CHUNK = 64
SW_HEADS = 16
SW_KV_HEADS = 4
SW_GROUP = SW_HEADS // SW_KV_HEADS
SW_HD = 64
WINDOW = 128
N_BUCKETS = 32
MAX_DIST = 128
N_EXPERTS = 8

LANES = 128
GDN_BLOCK = 2 * CHUNK
HALO = 8

VMEM_LIMIT = 56 * 1024 * 1024


def _cparams(sem):
    return pltpu.CompilerParams(dimension_semantics=sem, vmem_limit_bytes=VMEM_LIMIT)


def _silu(x):
    return x * (1.0 / (1.0 + jnp.exp(-x)))


def _dot(a, b):
    return jnp.dot(a, b, preferred_element_type=F32)


def _dot_nt(a, b):
    return lax.dot_general(a, b, (((1,), (1,)), ((), ())), preferred_element_type=F32)


def _norm_matmul_kernel(*refs, n_groups):
    x_ref = refs[0]
    g_refs = refs[1:1 + n_groups]
    w_refs = refs[1 + n_groups:1 + 2 * n_groups]
    o_refs = refs[1 + 2 * n_groups:1 + 3 * n_groups]
    x = x_ref[...]
    xr = x * lax.rsqrt(jnp.mean(x * x, axis=-1, keepdims=True) + EPS)
    for g_ref, w_ref, o_ref in zip(g_refs, w_refs, o_refs):
        o_ref[...] = _dot((xr * g_ref[...]).astype(BF16), w_ref[...]).astype(o_ref.dtype)


def norm_matmul(x, groups, tm, name):
    t, d = x.shape
    tm = min(tm, t)
    gains = [g.reshape(1, d).astype(F32) for g, _, _ in groups]
    ws = [w for _, w, _ in groups]
    return pl.pallas_call(
        functools.partial(_norm_matmul_kernel, n_groups=len(groups)),
        out_shape=[jax.ShapeDtypeStruct((t, w.shape[1]), dt) for _, w, dt in groups],
        grid=(t // tm,),
        in_specs=([pl.BlockSpec((tm, d), lambda i: (i, 0))]
                  + [pl.BlockSpec((1, d), lambda i: (0, 0)) for _ in groups]
                  + [pl.BlockSpec(w.shape, lambda i: (0, 0)) for w in ws]),
        out_specs=[pl.BlockSpec((tm, w.shape[1]), lambda i: (i, 0)) for w in ws],
        compiler_params=_cparams(("parallel",)),
        name=name,
    )(x, *gains, *ws)


def _matmul_res_kernel(a_ref, w_ref, r_ref, o_ref):
    o_ref[...] = r_ref[...] + _dot(a_ref[...], w_ref[...])


def matmul_residual(a, w, res, tm, tn, name):
    t, k = a.shape
    n = w.shape[1]
    tm, tn = min(tm, t), min(tn, n)
    return pl.pallas_call(
        _matmul_res_kernel,
        out_shape=jax.ShapeDtypeStruct((t, n), F32),
        grid=(t // tm, n // tn),
        in_specs=[pl.BlockSpec((tm, k), lambda i, j: (i, 0)),
                  pl.BlockSpec((k, tn), lambda i, j: (0, j)),
                  pl.BlockSpec((tm, tn), lambda i, j: (i, j))],
        out_specs=pl.BlockSpec((tm, tn), lambda i, j: (i, j)),
        compiler_params=_cparams(("parallel", "parallel")),
        name=name,
    )(a, w, res)


def _gdn_kernel(proj_ref, gates_ref, convw_ref, hp_ref, onorm_ref, o_ref, xs_ref, state_ref):
    nh, d, c = LA_HEADS, LA_D, CHUNK
    blk = GDN_BLOCK
    qkv_w = 3 * nh * d

    @pl.when(pl.program_id(1) == 0)
    def _():
        state_ref[...] = jnp.zeros_like(state_ref)
        xs_ref[0:HALO, :] = jnp.zeros((HALO, qkv_w), F32)

    xs_ref[HALO:HALO + blk, :] = proj_ref[:, 0:qkv_w].astype(F32)

    gates = gates_ref[...]
    a_log = hp_ref[0:1, :]
    dt_bias = hp_ref[1:2, :]
    beta = 1.0 / (1.0 + jnp.exp(-gates))
    sp_in = gates + dt_bias
    softplus = jnp.maximum(sp_in, 0.0) + jnp.log(1.0 + jnp.exp(-jnp.abs(sp_in)))
    g = -jnp.exp(a_log) * softplus

    row = lax.broadcasted_iota(jnp.int32, (blk, blk), 0)
    col = lax.broadcasted_iota(jnp.int32, (blk, blk), 1)
    tri = jnp.where((row >= col) & ((row // c) == (col // c)), 1.0, 0.0).astype(BF16)
    g_hi = g.astype(BF16)
    g_r1 = g - g_hi.astype(F32)
    g_mid = g_r1.astype(BF16)
    g_lo = (g_r1 - g_mid.astype(F32)).astype(BF16)
    gc = _dot(tri, g_hi) + _dot(tri, g_mid) + _dot(tri, g_lo)
    gc_t = gc.T

    ci = lax.broadcasted_iota(jnp.int32, (c, c), 0)
    cj = lax.broadcasted_iota(jnp.int32, (c, c), 1)
    lower_incl = ci >= cj
    strict = ci > cj
    eye_c = jnp.where(ci == cj, 1.0, 0.0).astype(F32)
    di = lax.broadcasted_iota(jnp.int32, (d, d), 0)
    dj = lax.broadcasted_iota(jnp.int32, (d, d), 1)
    eye_d = jnp.where(di == dj, 1.0, 0.0).astype(BF16)

    onorm = onorm_ref[...]

    def conv_silu(col0):
        acc = None
        for j in range(CONV_W):
            r0 = HALO - (CONV_W - 1) + j
            term = convw_ref[j:j + 1, col0:col0 + d] * xs_ref[r0:r0 + blk, col0:col0 + d]
            acc = term if acc is None else acc + term
        return _silu(acc)

    n_ck = blk // c
    chains = [(h, ck) for h in range(nh) for ck in range(n_ck)]
    heads = []
    for h in range(nh):
        qf = conv_silu(h * d)
        kf = conv_silu(nh * d + h * d)
        vf = conv_silu(2 * nh * d + h * d)
        qf = qf * lax.rsqrt(jnp.sum(qf * qf, axis=-1, keepdims=True) + EPS) * (d ** -0.5)
        kf = kf * lax.rsqrt(jnp.sum(kf * kf, axis=-1, keepdims=True) + EPS)
        heads.append((qf, kf, vf))

    st = {}
    for (h, ck) in chains:
        r = ck * c
        qf, kf, vf = heads[h]
        q, k, v = qf[r:r + c], kf[r:r + c], vf[r:r + c]
        g_col = gc[r:r + c, nh + h:nh + h + 1]
        g_row = gc_t[nh + h:nh + h + 1, r:r + c]
        g_last = gc[r + c - 1:r + c, nh + h:nh + h + 1]
        b_col = beta[r:r + c, h:h + 1]
        decay = jnp.where(lower_incl, jnp.exp(jnp.where(lower_incl, g_col - g_row, 0.0)), 0.0)
        k_beta = k * b_col
        e_col = jnp.exp(g_col)
        lhs = jnp.concatenate([k_beta.astype(BF16), q.astype(BF16), eye_d], axis=0)
        kk = _dot_nt(lhs, k.astype(BF16))
        a_mat = jnp.where(strict, kk[0:c] * decay, 0.0)
        st[(h, ck)] = dict(
            a=a_mat, attn=(kk[c:2 * c] * decay).astype(BF16),
            k_tail_t=(kk[2 * c:2 * c + d] * jnp.exp(g_last - g_row)).astype(BF16),
            rhs=jnp.concatenate([(v * b_col).astype(BF16), (k_beta * e_col).astype(BF16)], axis=1),
            qe=(q * e_col).astype(BF16), e_last=jnp.exp(g_last))

    for key in chains:
        x_b = (-st[key]["a"]).astype(BF16)
        st[key]["y"] = _dot(x_b, x_b)
        st[key]["p"] = eye_c - st[key]["a"]
    n_levels = int(np.log2(c))
    for lvl in range(1, n_levels):
        for key in chains:
            y_b = st[key]["y"].astype(BF16)
            p = st[key]["p"]
            if lvl + 1 < n_levels:
                zz = _dot(jnp.concatenate([y_b, p.astype(BF16)], axis=0), y_b)
                st[key]["y"] = zz[0:c]
                st[key]["p"] = p + zz[c:2 * c]
            else:
                st[key]["p"] = p + _dot(p.astype(BF16), y_b)
    for key in chains:
        st[key]["uw"] = _dot(st[key]["p"].astype(BF16), st[key]["rhs"])

    for ck in range(n_ck):
        r = ck * c
        s_old = [state_ref[h] for h in range(nh)]
        ws_qs = []
        for h in range(nh):
            cur = st[(h, ck)]
            lhs = jnp.concatenate([cur["uw"][:, d:2 * d].astype(BF16), cur["qe"]], axis=0)
            ws_qs.append(_dot(lhs, s_old[h].astype(BF16)))
        for h in range(nh):
            cur = st[(h, ck)]
            v_new = cur["uw"][:, 0:d] - ws_qs[h][0:c]
            av_kv = _dot(jnp.concatenate([cur["attn"], cur["k_tail_t"]], axis=0), v_new.astype(BF16))
            state_ref[h] = s_old[h] * cur["e_last"] + av_kv[c:c + d]
            o = ws_qs[h][c:2 * c] + av_kv[0:c]
            o = (o * lax.rsqrt(jnp.mean(o * o, axis=-1, keepdims=True) + EPS)) * onorm
            z = proj_ref[r:r + c, qkv_w + h * d:qkv_w + (h + 1) * d].astype(F32)
            o_ref[r:r + c, h * d:(h + 1) * d] = (o * _silu(z)).astype(o_ref.dtype)

    xs_ref[0:HALO, :] = xs_ref[blk:blk + HALO, :]


def gdn_core(proj, gates, conv_w, a_log, dt_bias, out_norm, batch, seq):
    t = proj.shape[0]
    nh, d = LA_HEADS, LA_D
    blk = GDN_BLOCK
    nblk = seq // blk
    hp = jnp.zeros((8, LANES), F32)
    hp = hp.at[0, nh:2 * nh].set(a_log.astype(F32)).at[1, nh:2 * nh].set(dt_bias.astype(F32))
    return pl.pallas_call(
        _gdn_kernel,
        out_shape=jax.ShapeDtypeStruct((t, nh * d), BF16),
        grid=(batch, nblk),
        in_specs=[pl.BlockSpec((blk, 4 * nh * d), lambda b, n: (b * nblk + n, 0)),
                  pl.BlockSpec((blk, LANES), lambda b, n: (b * nblk + n, 0)),
                  pl.BlockSpec((CONV_W, 3 * nh * d), lambda b, n: (0, 0)),
                  pl.BlockSpec((8, LANES), lambda b, n: (0, 0)),
                  pl.BlockSpec((1, d), lambda b, n: (0, 0))],
        out_specs=pl.BlockSpec((blk, nh * d), lambda b, n: (b * nblk + n, 0)),
        scratch_shapes=[pltpu.VMEM((HALO + blk, 3 * nh * d), F32),
                        pltpu.VMEM((nh, d, d), F32)],
        compiler_params=_cparams(("parallel", "arbitrary")),
        name="gdn_core",
    )(proj, gates, conv_w.astype(F32), hp, out_norm.reshape(1, d).astype(F32))


def _swiglu_kernel(te_ref, first_ref, nv_ref, x_ref, g_ref, wg_hbm, wu_hbm, wd_hbm, o_ref,
                   wg_c, wu_c, wd_c, stage_in, stage_out, sems, *, pre_norm, tf):
    i = pl.program_id(0)
    nf = wg_c.shape[0]
    e = te_ref[i]
    valid = i < nv_ref[0]

    def chunk_copies(j, slot):
        cols = pl.ds(j * tf, tf)
        return (pltpu.make_async_copy(wg_hbm.at[e, :, cols], stage_in.at[slot, 0], sems.at[slot, 0]),
                pltpu.make_async_copy(wu_hbm.at[e, :, cols], stage_in.at[slot, 1], sems.at[slot, 1]),
                pltpu.make_async_copy(wd_hbm.at[e, cols, :], stage_out.at[slot], sems.at[slot, 2]))

    def prepare_rows():
        x = x_ref[...].astype(F32)
        if pre_norm:
            ms = jnp.mean(x * x, axis=-1, keepdims=True)
            x = (x * lax.rsqrt(ms + EPS)) * g_ref[...]
        return x.astype(BF16)

    def chunk(xb, j):
        hid = _silu(_dot(xb, wg_c[j])) * _dot(xb, wu_c[j])
        return _dot(hid.astype(BF16), wd_c[j])

    def finish(acc):
        if pre_norm:
            o_ref[...] = (x_ref[...] + acc).astype(o_ref.dtype)
        else:
            o_ref[...] = acc.astype(o_ref.dtype)

    @pl.when(valid & (first_ref[i] == 1))
    def _():
        for c in chunk_copies(0, 0):
            c.start()
        xb = prepare_rows()
        acc = None
        for j in range(nf):
            slot = j % 2
            if j + 1 < nf:
                for c in chunk_copies(j + 1, 1 - slot):
                    c.start()
            for c in chunk_copies(j, slot):
                c.wait()
            wg_c[j] = stage_in[slot, 0].astype(BF16)
            wu_c[j] = stage_in[slot, 1].astype(BF16)
            wd_c[j] = stage_out[slot].astype(BF16)
            y = chunk(xb, j)
            acc = y if acc is None else acc + y
        finish(acc)

    @pl.when(valid & (first_ref[i] != 1))
    def _():
        xb = prepare_rows()
        acc = None
        for j in range(nf):
            y = chunk(xb, j)
            acc = y if acc is None else acc + y
        finish(acc)

    @pl.when(jnp.logical_not(valid))
    def _():
        o_ref[...] = jnp.zeros_like(o_ref)


def expert_swiglu(x, gain, tile_expert, tile_first, n_valid, wg, wu, wd, tile_rows, tf, out_dtype, pre_norm, name):
    n_rows, d = x.shape
    ne, _, f = wg.shape
    n_tiles = n_rows // tile_rows
    nf = f // tf
    grid_spec = pltpu.PrefetchScalarGridSpec(
        num_scalar_prefetch=3,
        grid=(n_tiles,),
        in_specs=[pl.BlockSpec((tile_rows, d), lambda i, te, fi, nv: (jnp.minimum(i, nv[0] - 1), 0)),
                  pl.BlockSpec((1, d), lambda i, te, fi, nv: (0, 0)),
                  pl.BlockSpec(memory_space=pl.ANY),
                  pl.BlockSpec(memory_space=pl.ANY),
                  pl.BlockSpec(memory_space=pl.ANY)],
        out_specs=pl.BlockSpec((tile_rows, d), lambda i, te, fi, nv: (i, 0)),
        scratch_shapes=[pltpu.VMEM((nf, d, tf), BF16), pltpu.VMEM((nf, d, tf), BF16), pltpu.VMEM((nf, tf, d), BF16),
                        pltpu.VMEM((2, 2, d, tf), F32), pltpu.VMEM((2, tf, d), F32),
                        pltpu.SemaphoreType.DMA((2, 3))],
    )
    return pl.pallas_call(
        functools.partial(_swiglu_kernel, pre_norm=pre_norm, tf=tf),
        out_shape=jax.ShapeDtypeStruct((n_rows, d), out_dtype),
        grid_spec=grid_spec,
        compiler_params=_cparams(("arbitrary",)),
        name=name,
    )(tile_expert, tile_first, n_valid, x, gain.reshape(1, d).astype(F32), wg, wu, wd)


def ffn_dense(x, gain, wg, wu, wd, tm, tf):
    t = x.shape[0]
    n_tiles = t // tm
    tile_first = jnp.zeros((n_tiles,), jnp.int32).at[0].set(1)
    return expert_swiglu(x, gain, jnp.zeros((n_tiles,), jnp.int32), tile_first, jnp.full((1,), n_tiles, jnp.int32),
                         wg[None], wu[None], wd[None], tm, tf, F32, True, "ffn_dense")


def _t5_bucket_np(dist):
    max_exact = N_BUCKETS // 2
    n = np.maximum(dist, 0)
    safe = np.maximum(n, 1).astype(np.float32)
    large = max_exact + (np.log(safe / max_exact) / np.log(MAX_DIST / max_exact)
                         * (N_BUCKETS - max_exact)).astype(np.int32)
    large = np.minimum(large, N_BUCKETS - 1)
    return np.where(n < max_exact, n, large).astype(np.int32)


def _bias_kernel(bucket_ref, rb_ref, o_ref):
    bucket = bucket_ref[...]
    for h in range(SW_HEADS):
        acc = jnp.zeros(bucket.shape, F32)
        for b in range(N_BUCKETS):
            acc = jnp.where(bucket == b, rb_ref[b, h], acc)
        o_ref[h] = acc


def bias_table(rel_bias):
    qi = np.arange(WINDOW)[:, None] + WINDOW
    kj = np.arange(2 * WINDOW)[None, :]
    bucket = jnp.asarray(_t5_bucket_np(qi - kj))
    return pl.pallas_call(
        _bias_kernel,
        out_shape=jax.ShapeDtypeStruct((SW_HEADS, WINDOW, 2 * WINDOW), F32),
        in_specs=[pl.BlockSpec(memory_space=pltpu.VMEM), pl.BlockSpec(memory_space=pltpu.SMEM)],
        out_specs=pl.BlockSpec(memory_space=pltpu.VMEM),
        name="t5_bias_table",
    )(bucket, rel_bias.astype(F32))


def _swa_kernel(q_ref, kvp_ref, kvc_ref, bias_ref, qn_ref, kn_ref, sink_ref, o_ref):
    n = pl.program_id(1)
    blk, hd = WINDOW, SW_HD
    kv_w = SW_KV_HEADS * hd
    qi = lax.broadcasted_iota(jnp.int32, (blk, 2 * blk), 0) + blk
    kj = lax.broadcasted_iota(jnp.int32, (blk, 2 * blk), 1)
    dist = qi - kj
    first_key = jnp.where(n > 0, 0, blk)
    mask = (dist >= 0) & (dist < WINDOW) & (kj >= first_key)
    gw = 2 * LANES
    gi = lax.broadcasted_iota(jnp.int32, (gw, gw), 0)
    gj = lax.broadcasted_iota(jnp.int32, (gw, gw), 1)
    group_ones = jnp.where((gi // hd) == (gj // hd), 1.0, 0.0).astype(BF16)
    lane = lax.broadcasted_iota(jnp.int32, (1, LANES), 1)
    low_half = lane < hd

    def head_norm(x, gain):
        cols = []
        for c0 in range(0, x.shape[1], gw):
            xc = x[:, c0:c0 + gw]
            ss = _dot((xc * xc).astype(BF16), group_ones)
            cols.append(xc * lax.rsqrt(ss * (1.0 / hd) + EPS))
        return jnp.concatenate(cols, axis=1) * gain

    def dup_half(x, half):
        swapped = pltpu.roll(x, hd, 1)
        return jnp.where(low_half == (half == 0), x, swapped)

    qn = head_norm(q_ref[...].astype(F32), qn_ref[...]) * (hd ** -0.5)
    half_sel = [jnp.where(low_half, 1.0, 0.0), jnp.where(low_half, 0.0, 1.0)]
    k_all = jnp.concatenate([kvp_ref[:, 0:kv_w], kvc_ref[:, 0:kv_w]], axis=0).astype(F32)
    kn = head_norm(k_all, kn_ref[...])
    v_all = jnp.concatenate([kvp_ref[:, kv_w:2 * kv_w], kvc_ref[:, kv_w:2 * kv_w]], axis=0).astype(F32)
    ks, vs = [], []
    for g in range(SW_KV_HEADS):
        c0 = (g // 2) * LANES
        ks.append(dup_half(kn[:, c0:c0 + LANES], g % 2).astype(BF16))
        vs.append(dup_half(v_all[:, c0:c0 + LANES], g % 2).astype(BF16))

    scores = []
    for hq in range(SW_HEADS):
        c0 = (hq // 2) * LANES
        q_h = (qn[:, c0:c0 + LANES] * half_sel[hq % 2]).astype(BF16)
        scores.append(_dot_nt(q_h, ks[hq // SW_GROUP]))
    probs = []
    for hq in range(SW_HEADS):
        s = jnp.where(mask, scores[hq] + bias_ref[hq], NEG_INF)
        sink = sink_ref[hq]
        mx = jnp.maximum(jnp.max(s, axis=-1, keepdims=True), sink)
        p = jnp.exp(s - mx)
        denom = jnp.sum(p, axis=-1, keepdims=True) + jnp.exp(sink - mx)
        probs.append((p / denom).astype(BF16))
    outs = [_dot(probs[hq], vs[hq // SW_GROUP]) for hq in range(SW_HEADS)]
    for c in range(SW_HEADS // 2):
        o_ref[:, c * LANES:(c + 1) * LANES] = jnp.where(low_half, outs[2 * c], outs[2 * c + 1]).astype(o_ref.dtype)


def swa_attention(q, kv, bias, q_norm, k_norm, sinks, batch, seq):
    t = q.shape[0]
    blk = WINDOW
    nb = seq // blk
    qw = SW_HEADS * SW_HD
    kvw = 2 * SW_KV_HEADS * SW_HD
    return pl.pallas_call(
        _swa_kernel,
        out_shape=jax.ShapeDtypeStruct((t, qw), BF16),
        grid=(batch, nb),
        in_specs=[pl.BlockSpec((blk, qw), lambda b, n: (b * nb + n, 0)),
                  pl.BlockSpec((blk, kvw), lambda b, n: (b * nb + jnp.maximum(n - 1, 0), 0)),
                  pl.BlockSpec((blk, kvw), lambda b, n: (b * nb + n, 0)),
                  pl.BlockSpec((SW_HEADS, blk, 2 * blk), lambda b, n: (0, 0, 0)),
                  pl.BlockSpec((1, qw), lambda b, n: (0, 0)),
                  pl.BlockSpec((1, kvw // 2), lambda b, n: (0, 0)),
                  pl.BlockSpec(memory_space=pltpu.SMEM)],
        out_specs=pl.BlockSpec((blk, qw), lambda b, n: (b * nb + n, 0)),
        compiler_params=_cparams(("parallel", "parallel")),
        name="swa_attention",
    )(q, kv, kv, bias, jnp.tile(q_norm.astype(F32), SW_HEADS).reshape(1, qw),
      jnp.tile(k_norm.astype(F32), SW_KV_HEADS).reshape(1, kvw // 2), sinks.astype(F32))


def _route_kernel(x_ref, g_ref, wr_ref, r_ref, wt_ref, cnt_ref, sel_s, gw_s, cnt_s, start_s, run_s, *, tile_rows):
    ne = N_EXPERTS
    p = pl.program_id(0)
    i = pl.program_id(1)
    tm = x_ref.shape[0]
    sub = lax.broadcasted_iota(jnp.int32, (ne, tm), 0).astype(F32)

    @pl.when(p == 0)
    def _():
        @pl.when(i == 0)
        def _():
            cnt_s[...] = jnp.zeros_like(cnt_s)

        x = x_ref[...]
        ms = jnp.mean(x * x, axis=-1, keepdims=True)
        xn32 = (x * lax.rsqrt(ms + EPS)) * g_ref[...]
        xn_hi = xn32.astype(BF16)
        xn_lo = (xn32 - xn_hi.astype(F32)).astype(BF16)
        p_hi = _dot_nt(wr_ref[...], xn_hi)
        p_lo = _dot_nt(wr_ref[...], xn_lo)
        logits = p_hi[0:ne] + p_hi[ne:2 * ne] + p_lo[0:ne]
        m1 = jnp.max(logits, axis=0, keepdims=True)
        i1 = jnp.min(jnp.where(logits == m1, sub, float(ne)), axis=0, keepdims=True)
        l2 = jnp.where(sub == i1, -jnp.inf, logits)
        m2 = jnp.max(l2, axis=0, keepdims=True)
        i2 = jnp.min(jnp.where(l2 == m2, sub, float(ne)), axis=0, keepdims=True)
        e2 = jnp.exp(m2 - m1)
        w1 = 1.0 / (1.0 + e2)
        w2 = e2 / (1.0 + e2)
        sel = jnp.where((sub == i1) | (sub == i2), 1.0, 0.0)
        sel_s[i] = sel
        gw_s[i] = jnp.where(sub == i1, w1, jnp.where(sub == i2, w2, 0.0))
        cnt_s[...] += jnp.sum(sel, axis=1, keepdims=True)

    @pl.when(p == 1)
    def _():
        @pl.when(i == 0)
        def _():
            cnt = cnt_s[...]
            padded = jnp.floor((cnt + (tile_rows - 1)) * (1.0 / tile_rows)) * tile_rows
            sub8 = lax.broadcasted_iota(jnp.int32, cnt.shape, 0)
            start = jnp.zeros_like(cnt)
            for e in range(ne - 1):
                start = start + jnp.where(sub8 > e, padded[e:e + 1, :], 0.0)
            start_s[...] = start
            run_s[...] = jnp.zeros_like(run_s)
            cnt_ref[...] = cnt

        sel = sel_s[i]
        gw = gw_s[i]
        ti = lax.broadcasted_iota(jnp.int32, (tm, tm), 0)
        tj = lax.broadcasted_iota(jnp.int32, (tm, tm), 1)
        tri = jnp.where(ti <= tj, 1.0, 0.0).astype(BF16)
        csum = _dot(sel.astype(BF16), tri)
        slot = start_s[:, 0:1] + run_s[:, 0:1] + csum - sel
        run_s[...] += csum[:, tm - 1:tm]
        ia = jnp.min(jnp.where(sel > 0.0, sub, float(ne)), axis=0, keepdims=True)
        ib = jnp.max(jnp.where(sel > 0.0, sub, -1.0), axis=0, keepdims=True)
        pick_a = sub == ia
        pick_b = sub == ib
        rows = [jnp.sum(jnp.where(pick_a, slot, 0.0), axis=0, keepdims=True),
                jnp.sum(jnp.where(pick_b, slot, 0.0), axis=0, keepdims=True),
                jnp.sum(jnp.where(pick_a, gw, 0.0), axis=0, keepdims=True),
                jnp.sum(jnp.where(pick_b, gw, 0.0), axis=0, keepdims=True)]
        r_ref[...] = jnp.concatenate(rows + [jnp.zeros((ne - 4, tm), F32)], axis=0)
        wpad = jnp.concatenate(rows[2:4] + [jnp.zeros((LANES - 2, tm), F32)], axis=0)
        wt_ref[...] = wpad.T


def moe_route(x, gain, w_router, tm, tile_rows):
    t, d = x.shape
    ne = w_router.shape[1]
    assert ne == N_EXPERTS
    w_hi = w_router.astype(BF16)
    w_lo = (w_router - w_hi.astype(F32)).astype(BF16)
    wr = jnp.concatenate([w_hi.T, w_lo.T], axis=0)
    tm = min(tm, t)
    nt = t // tm
    return pl.pallas_call(
        functools.partial(_route_kernel, tile_rows=tile_rows),
        out_shape=(jax.ShapeDtypeStruct((ne, t), F32), jax.ShapeDtypeStruct((t, LANES), F32),
                   jax.ShapeDtypeStruct((ne, LANES), F32)),
        grid=(2, nt),
        in_specs=[pl.BlockSpec((tm, d), lambda p, i: (i * (1 - p) + (nt - 1) * p, 0)),
                  pl.BlockSpec((1, d), lambda p, i: (0, 0)),
                  pl.BlockSpec((2 * ne, d), lambda p, i: (0, 0))],
        out_specs=(pl.BlockSpec((ne, tm), lambda p, i: (0, i * p)),
                   pl.BlockSpec((tm, LANES), lambda p, i: (i * p, 0)),
                   pl.BlockSpec((ne, LANES), lambda p, i: (0, 0))),
        scratch_shapes=[pltpu.VMEM((nt, ne, tm), F32), pltpu.VMEM((nt, ne, tm), F32),
                        pltpu.VMEM((ne, LANES), F32), pltpu.VMEM((ne, LANES), F32), pltpu.VMEM((ne, LANES), F32)],
        compiler_params=_cparams(("arbitrary", "arbitrary")),
        name="moe_route",
    )(x, gain.reshape(1, d), wr)


def _dispatch_kernel(zf_ref, slots_ref, x_ref, g_ref, xs_ref, xn_s, zero_s, sem, zsem, *, tile_rows):
    tm = x_ref.shape[0]

    @pl.when(pl.program_id(0) == 0)
    def _():
        zero_s[...] = jnp.zeros_like(zero_s)

        def zero_copy(e):
            row0 = pl.multiple_of(zf_ref[e], tile_rows)
            return pltpu.make_async_copy(zero_s, xs_ref.at[pl.ds(row0, tile_rows)], zsem)

        for e in range(zf_ref.shape[0]):
            @pl.when(zf_ref[e] >= 0)
            def _():
                zero_copy(e).start()
        for e in range(zf_ref.shape[0]):
            @pl.when(zf_ref[e] >= 0)
            def _():
                zero_copy(e).wait()

    x = x_ref[...]
    ms = jnp.mean(x * x, axis=-1, keepdims=True)
    xn_s[...] = (x * lax.rsqrt(ms + EPS)) * g_ref[...]

    def row_copy(r, k):
        return pltpu.make_async_copy(xn_s.at[pl.ds(r, 1)], xs_ref.at[pl.ds(slots_ref[0, k, r], 1)], sem)

    def start(r, c):
        row_copy(r, 0).start()
        row_copy(r, 1).start()
        return c

    def wait(r, c):
        row_copy(r, 0).wait()
        row_copy(r, 1).wait()
        return c

    lax.fori_loop(0, tm, start, 0, unroll=8)
    lax.fori_loop(0, tm, wait, 0, unroll=8)


def moe_dispatch(x, gain, slots, zf_rows, n_slots, tm, tile_rows):
    t, d = x.shape
    tm = min(tm, t)
    nt = t // tm
    slots3 = slots.reshape(2, nt, tm).transpose(1, 0, 2)
    grid_spec = pltpu.PrefetchScalarGridSpec(
        num_scalar_prefetch=1,
        grid=(nt,),
        in_specs=[pl.BlockSpec((1, 2, tm), lambda i, zf: (i, 0, 0), memory_space=pltpu.SMEM),
                  pl.BlockSpec((tm, d), lambda i, zf: (i, 0)),
                  pl.BlockSpec((1, d), lambda i, zf: (0, 0))],
        out_specs=pl.BlockSpec(memory_space=pl.ANY),
        scratch_shapes=[pltpu.VMEM((tm, d), F32), pltpu.VMEM((tile_rows, d), F32),
                        pltpu.SemaphoreType.DMA, pltpu.SemaphoreType.DMA],
    )
    return pl.pallas_call(
        functools.partial(_dispatch_kernel, tile_rows=tile_rows),
        out_shape=jax.ShapeDtypeStruct((n_slots, d), F32),
        grid_spec=grid_spec,
        compiler_params=_cparams(("arbitrary",)),
        name="moe_dispatch",
    )(zf_rows, slots3, x, gain.reshape(1, d))


def _combine_kernel(slots_ref, h_ref, wt_ref, ys_ref, o_ref, buf, sem):
    tm = h_ref.shape[0]

    def row_copy(r, k):
        return pltpu.make_async_copy(ys_ref.at[pl.ds(slots_ref[0, k, r], 1)], buf.at[k, pl.ds(r, 1)], sem)

    def start(r, c):
        row_copy(r, 0).start()
        row_copy(r, 1).start()
        return c

    def wait(r, c):
        row_copy(r, 0).wait()
        row_copy(r, 1).wait()
        return c

    lax.fori_loop(0, tm, start, 0, unroll=8)
    lax.fori_loop(0, tm, wait, 0, unroll=8)
    wt = wt_ref[...]
    o_ref[...] = h_ref[...] + wt[:, 0:1] * buf[0] + wt[:, 1:2] * buf[1]


def moe_combine(h, wt, slots, ys, tm):
    t, d = h.shape
    tm = min(tm, t)
    nt = t // tm
    slots3 = slots.reshape(2, nt, tm).transpose(1, 0, 2)
    return pl.pallas_call(
        _combine_kernel,
        out_shape=jax.ShapeDtypeStruct((t, d), F32),
        grid=(nt,),
        in_specs=[pl.BlockSpec((1, 2, tm), lambda i: (i, 0, 0), memory_space=pltpu.SMEM),
                  pl.BlockSpec((tm, d), lambda i: (i, 0)),
                  pl.BlockSpec((tm, LANES), lambda i: (i, 0)),
                  pl.BlockSpec(memory_space=pl.ANY)],
        out_specs=pl.BlockSpec((tm, d), lambda i: (i, 0)),
        scratch_shapes=[pltpu.VMEM((2, tm, d), F32), pltpu.SemaphoreType.DMA],
        compiler_params=_cparams(("arbitrary",)),
        name="moe_combine",
    )(slots3, h, wt, ys)


MOE_TILE_ROWS = 512
TOP_K = 2


def moe_layer(h, gain, w_router, wg, wu, wd):
    t, d = h.shape
    ne = w_router.shape[1]
    tr = MOE_TILE_ROWS
    n_tiles = -(-(TOP_K * t + ne * (tr - 1)) // tr)
    n_slots = n_tiles * tr

    r, wt, cnt = moe_route(h, gain, w_router, 512, tr)
    slots = r[0:2].astype(jnp.int32)

    counts = cnt[:, 0].astype(jnp.int32)
    padded = ((counts + (tr - 1)) // tr) * tr
    ends = jnp.cumsum(padded)
    n_valid = (ends[-1] // tr).astype(jnp.int32)
    tile_row0 = jnp.arange(n_tiles, dtype=jnp.int32) * tr
    tile_expert = jnp.sum((tile_row0[:, None] >= ends[None, :]).astype(jnp.int32), axis=1)
    tile_expert = jnp.minimum(tile_expert, ne - 1)
    tile_expert = jnp.where(jnp.arange(n_tiles) < n_valid, tile_expert, tile_expert[jnp.maximum(n_valid - 1, 0)])
    prev_expert = jnp.concatenate([jnp.full((1,), -1, jnp.int32), tile_expert[:-1]])
    tile_first = (tile_expert != prev_expert).astype(jnp.int32)
    tail = jnp.arange(TOP_K * t // tr, n_tiles, dtype=jnp.int32)
    zf_rows = jnp.concatenate([jnp.where(padded > 0, ends - tr, -1),
                               jnp.where(tail >= n_valid, tail * tr, -1)]).astype(jnp.int32)

    xs = moe_dispatch(h, gain, slots, zf_rows, n_slots, 512, tr)
    ys = expert_swiglu(xs, gain, tile_expert, tile_first, n_valid.reshape(1), wg, wu, wd, tr, 512, F32, False,
                       "moe_experts")
    return moe_combine(h, wt, slots, ys, 256)


def kernel(x, a_norm, a_w_in, a_conv, a_log_decay, a_dt_bias, a_out_norm, a_w_out, kv_norm, kv_w, k_norm,
           b_norm, b_w_q, q_norm, b_sinks, b_w_o, rel_bias, ffn_norm, dense_w_gate, dense_w_up, dense_w_down,
           moe_router, moe_w_gate, moe_w_up, moe_w_down):
    batch, seq, d = x.shape
    t = batch * seq
    nh, hd = LA_HEADS, LA_D
    main_w = 4 * nh * hd
    h0 = x.reshape(t, d)

    w_in = a_w_in[0]
    w_main = w_in[:, 0:main_w].astype(BF16)
    w_gate = jnp.zeros((d, LANES), BF16).at[:, 0:2 * nh].set(w_in[:, main_w:main_w + 2 * nh].astype(BF16))
    proj, gates = norm_matmul(h0, [(a_norm[0], w_main, BF16), (a_norm[0], w_gate, F32)], 512, "gdn_in_proj")
    o = gdn_core(proj, gates, a_conv[0], a_log_decay[0], a_dt_bias[0], a_out_norm[0], batch, seq)
    h1 = matmul_residual(o, a_w_out[0].astype(BF16), h0, 1024, 1024, "gdn_out_proj")

    h2 = ffn_dense(h1, ffn_norm[0], dense_w_gate[0], dense_w_up[0], dense_w_down[0], 512, 512)

    kv, q = norm_matmul(h2, [(kv_norm, kv_w.astype(BF16), BF16), (b_norm[0], b_w_q[0].astype(BF16), BF16)],
                        1024, "qkv_proj")
    bias = bias_table(rel_bias)
    attn = swa_attention(q, kv, bias, q_norm[0], k_norm, b_sinks[0], batch, seq)
    h3 = matmul_residual(attn, b_w_o[0].astype(BF16), h2, 1024, 1024, "attn_out_proj")

    h4 = moe_layer(h3, ffn_norm[1], moe_router[0], moe_w_gate[0], moe_w_up[0], moe_w_down[0])
    return h4.reshape(batch, seq, d)
```

```python
import functools

import numpy as np
import jax
import jax.numpy as jnp
from jax import lax
from jax.experimental import pallas as pl
from jax.experimental.pallas import tpu as pltpu

F32 = jnp.float32
BF16 = jnp.bfloat16

EPS = 1e-6
NEG_INF = -1e30

LA_HEADS = 8
LA_D = 128
CONV_W = 4
CHUNK = 64
SW_HEADS = 16
SW_KV_HEADS = 4
SW_GROUP = SW_HEADS // SW_KV_HEADS
SW_HD = 64
WINDOW = 128
N_BUCKETS = 32
MAX_DIST = 128
N_EXPERTS = 8

LANES = 128
GDN_BLOCK = 2 * CHUNK
HALO = 8

VMEM_LIMIT = 56 * 1024 * 1024


def _cparams(sem):
    return pltpu.CompilerParams(dimension_semantics=sem, vmem_limit_bytes=VMEM_LIMIT)


def _silu(x):
    return x * (1.0 / (1.0 + jnp.exp(-x)))


def _dot(a, b):
    return jnp.dot(a, b, preferred_element_type=F32)


def _dot_nt(a, b):
    return lax.dot_general(a, b, (((1,), (1,)), ((), ())), preferred_element_type=F32)


def _norm_matmul_kernel(*refs, n_groups):
    x_ref = refs[0]
    g_refs = refs[1:1 + n_groups]
    w_refs = refs[1 + n_groups:1 + 2 * n_groups]
    o_refs = refs[1 + 2 * n_groups:1 + 3 * n_groups]
    x = x_ref[...]
    xr = x * lax.rsqrt(jnp.mean(x * x, axis=-1, keepdims=True) + EPS)
    for g_ref, w_ref, o_ref in zip(g_refs, w_refs, o_refs):
        o_ref[...] = _dot((xr * g_ref[...]).astype(BF16), w_ref[...]).astype(o_ref.dtype)


def norm_matmul(x, groups, tm, name):
    t, d = x.shape
    tm = min(tm, t)
    gains = [g.reshape(1, d).astype(F32) for g, _, _ in groups]
    ws = [w for _, w, _ in groups]
    return pl.pallas_call(
        functools.partial(_norm_matmul_kernel, n_groups=len(groups)),
        out_shape=[jax.ShapeDtypeStruct((t, w.shape[1]), dt) for _, w, dt in groups],
        grid=(t // tm,),
        in_specs=([pl.BlockSpec((tm, d), lambda i: (i, 0))]
                  + [pl.BlockSpec((1, d), lambda i: (0, 0)) for _ in groups]
                  + [pl.BlockSpec(w.shape, lambda i: (0, 0)) for w in ws]),
        out_specs=[pl.BlockSpec((tm, w.shape[1]), lambda i: (i, 0)) for w in ws],
        compiler_params=_cparams(("parallel",)),
        name=name,
    )(x, *gains, *ws)


def _matmul_res_kernel(a_ref, w_ref, r_ref, o_ref):
    o_ref[...] = r_ref[...] + _dot(a_ref[...], w_ref[...])


def matmul_residual(a, w, res, tm, tn, name):
    t, k = a.shape
    n = w.shape[1]
    tm, tn = min(tm, t), min(tn, n)
    return pl.pallas_call(
        _matmul_res_kernel,
        out_shape=jax.ShapeDtypeStruct((t, n), F32),
        grid=(t // tm, n // tn),
        in_specs=[pl.BlockSpec((tm, k), lambda i, j: (i, 0)),
                  pl.BlockSpec((k, tn), lambda i, j: (0, j)),
                  pl.BlockSpec((tm, tn), lambda i, j: (i, j))],
        out_specs=pl.BlockSpec((tm, tn), lambda i, j: (i, j)),
        compiler_params=_cparams(("parallel", "parallel")),
        name=name,
    )(a, w, res)


def _gdn_kernel(proj_ref, gates_ref, convw_ref, hp_ref, onorm_ref, o_ref,
                xs_ref, state_ref, q_s, k_s, v_s, z_s, gc_s, gct_s, beta_s):
    n = pl.program_id(1)

    @pl.when(n == 0)
    def _():
        xs_ref[0:HALO, :] = jnp.zeros((HALO, xs_ref.shape[1]), F32)
        for ref in (q_s, k_s, v_s, z_s, gc_s, gct_s, beta_s):
            ref[1] = jnp.zeros(ref.shape[1:], ref.dtype)

    @pl.when(n <= 1)
    def _():
        state_ref[...] = jnp.zeros_like(state_ref)

    args = (proj_ref, gates_ref, convw_ref, hp_ref, onorm_ref, o_ref, xs_ref, state_ref,
            q_s, k_s, v_s, z_s, gc_s, gct_s, beta_s)

    @pl.when(lax.rem(n, 2) == 0)
    def _():
        _gdn_step(*args, slot_w=0, slot_r=1)

    @pl.when(lax.rem(n, 2) == 1)
    def _():
        _gdn_step(*args, slot_w=1, slot_r=0)


def _gdn_step(proj_ref, gates_ref, convw_ref, hp_ref, onorm_ref, o_ref, xs_ref, state_ref,
              q_s, k_s, v_s, z_s, gc_s, gct_s, beta_s, *, slot_w, slot_r):
    nh, d, c = LA_HEADS, LA_D, CHUNK
    blk = GDN_BLOCK
    qkv_w = 3 * nh * d

    gc = gc_s[slot_r]
    gc_t = gct_s[slot_r]
    beta = beta_s[slot_r]

    xs_ref[HALO:HALO + blk, :] = proj_ref[:, 0:qkv_w].astype(F32)

    def front_gates():
        _gdn_front_gates(gates_ref, hp_ref, gc_s, gct_s, beta_s, slot_w)

    ci = lax.broadcasted_iota(jnp.int32, (c, c), 0)
    cj = lax.broadcasted_iota(jnp.int32, (c, c), 1)
    lower_incl = ci >= cj
    strict = ci > cj
    eye_c = jnp.where(ci == cj, 1.0, 0.0).astype(F32)
    di = lax.broadcasted_iota(jnp.int32, (d, d), 0)
    dj = lax.broadcasted_iota(jnp.int32, (d, d), 1)
    eye_d = jnp.where(di == dj, 1.0, 0.0).astype(BF16)

    onorm = onorm_ref[...]

    sub_halo = lax.broadcasted_iota(jnp.int32, (HALO, d), 0)

    def conv_silu(col0):
        x_cur = xs_ref[HALO:HALO + blk, col0:col0 + d]
        x_tail = xs_ref[0:HALO, col0:col0 + d]
        acc = convw_ref[CONV_W - 1:CONV_W, col0:col0 + d] * x_cur
        for j in range(CONV_W - 1):
            s = CONV_W - 1 - j
            rolled = pltpu.roll(x_cur, s, 0)
            head = jnp.where(sub_halo < s, pltpu.roll(x_tail, s, 0), rolled[0:HALO])
            shifted = jnp.concatenate([head, rolled[HALO:blk]], axis=0)
            acc = acc + convw_ref[j:j + 1, col0:col0 + d] * shifted
        return _silu(acc)

    def front_head(h):
        qf = conv_silu(h * d)
        kf = conv_silu(nh * d + h * d)
        vf = conv_silu(2 * nh * d + h * d)
        q_s[slot_w, :, h * d:(h + 1) * d] = qf * (lax.rsqrt(jnp.sum(qf * qf, axis=-1, keepdims=True) + EPS)
                                                  * (d ** -0.5))
        k_s[slot_w, :, h * d:(h + 1) * d] = kf * lax.rsqrt(jnp.sum(kf * kf, axis=-1, keepdims=True) + EPS)
        v_s[slot_w, :, h * d:(h + 1) * d] = vf
        z_s[slot_w, :, h * d:(h + 1) * d] = proj_ref[:, qkv_w + h * d:qkv_w + (h + 1) * d]

    front_tasks = [front_gates] + [functools.partial(front_head, h) for h in range(nh)]

    def run_front_task():
        if front_tasks:
            front_tasks.pop(0)()

    n_ck = blk // c
    chains = [(h, ck) for h in range(nh) for ck in range(n_ck)]

    st = {}
    for (h, ck) in chains:
        r = ck * c
        q = q_s[slot_r, r:r + c, h * d:(h + 1) * d]
        k = k_s[slot_r, r:r + c, h * d:(h + 1) * d]
        v = v_s[slot_r, r:r + c, h * d:(h + 1) * d]
        g_col = gc[r:r + c, nh + h:nh + h + 1]
        g_row = gc_t[nh + h:nh + h + 1, r:r + c]
        g_last = gc[r + c - 1:r + c, nh + h:nh + h + 1]
        b_col = beta[r:r + c, h:h + 1]
        decay = jnp.where(lower_incl, jnp.exp2(jnp.where(lower_incl, g_col - g_row, 0.0)), 0.0)
        k_beta = k * b_col
        e_col = jnp.exp2(g_col)
        lhs = jnp.concatenate([k_beta.astype(BF16), q.astype(BF16), eye_d], axis=0)
        kk = _dot_nt(lhs, k.astype(BF16))
        a_mat = jnp.where(strict, kk[0:c] * decay, 0.0)
        st[(h, ck)] = dict(
            a=a_mat, attn=(kk[c:2 * c] * decay).astype(BF16),
            k_tail_t=(kk[2 * c:2 * c + d] * jnp.exp2(g_last - g_row)).astype(BF16),
            rhs=jnp.concatenate([(v * b_col).astype(BF16), (k_beta * e_col).astype(BF16)], axis=1),
            qe=(q * e_col).astype(BF16), e_last=jnp.exp2(g_last))
    run_front_task()

    for key in chains:
        x_b = (-st[key]["a"]).astype(BF16)
        st[key]["y"] = _dot(x_b, x_b)
        st[key]["p"] = eye_c - st[key]["a"]
    run_front_task()
    n_levels = int(np.log2(c))
    for lvl in range(1, n_levels):
        for key in chains:
            y_b = st[key]["y"].astype(BF16)
            p = st[key]["p"]
            if lvl + 1 < n_levels:
                zz = _dot(jnp.concatenate([y_b, p.astype(BF16)], axis=0), y_b)
                st[key]["y"] = zz[0:c]
                st[key]["p"] = p + zz[c:2 * c]
            else:
                st[key]["p"] = p + _dot(p.astype(BF16), y_b)
        run_front_task()
    for key in chains:
        st[key]["uw"] = _dot(st[key]["p"].astype(BF16), st[key]["rhs"])
    run_front_task()

    for ck in range(n_ck):
        r = ck * c
        s_old = [state_ref[h] for h in range(nh)]
        ws_qs = []
        for h in range(nh):
            cur = st[(h, ck)]
            lhs = jnp.concatenate([cur["uw"][:, d:2 * d].astype(BF16), cur["qe"]], axis=0)
            ws_qs.append(_dot(lhs, s_old[h].astype(BF16)))
        run_front_task()
        for h in range(nh):
            cur = st[(h, ck)]
            v_new = cur["uw"][:, 0:d] - ws_qs[h][0:c]
            av_kv = _dot(jnp.concatenate([cur["attn"], cur["k_tail_t"]], axis=0), v_new.astype(BF16))
            state_ref[h] = s_old[h] * cur["e_last"] + av_kv[c:c + d]
            o = ws_qs[h][c:2 * c] + av_kv[0:c]
            o = (o * lax.rsqrt(jnp.mean(o * o, axis=-1, keepdims=True) + EPS)) * onorm
            z = z_s[slot_r, r:r + c, h * d:(h + 1) * d].astype(F32)
            o_ref[r:r + c, h * d:(h + 1) * d] = (o * _silu(z)).astype(o_ref.dtype)
    while front_tasks:
        run_front_task()

    xs_ref[0:HALO, :] = xs_ref[blk:blk + HALO, :]


def _gdn_front_gates(gates_ref, hp_ref, gc_s, gct_s, beta_s, slot_w):
    blk, c = GDN_BLOCK, CHUNK
    gates = gates_ref[...]
    a_log = hp_ref[0:1, :]
    dt_bias = hp_ref[1:2, :]
    beta = 1.0 / (1.0 + jnp.exp(-gates))
    sp_in = gates + dt_bias
    softplus = jnp.maximum(sp_in, 0.0) + jnp.log(1.0 + jnp.exp(-jnp.abs(sp_in)))
    g = (-jnp.exp(a_log) * softplus) * float(np.log2(np.e))

    row = lax.broadcasted_iota(jnp.int32, (blk, blk), 0)
    col = lax.broadcasted_iota(jnp.int32, (blk, blk), 1)
    tri = jnp.where((row >= col) & ((row // c) == (col // c)), 1.0, 0.0).astype(BF16)
    g_hi = g.astype(BF16)
    g_r1 = g - g_hi.astype(F32)
    g_mid = g_r1.astype(BF16)
    g_lo = (g_r1 - g_mid.astype(F32)).astype(BF16)
    gc = _dot(tri, g_hi) + _dot(tri, g_mid) + _dot(tri, g_lo)
    gc_s[slot_w] = gc
    gct_s[slot_w] = gc.T
    beta_s[slot_w] = beta


def gdn_core(proj, gates, conv_w, a_log, dt_bias, out_norm, batch, seq):
    t = proj.shape[0]
    nh, d = LA_HEADS, LA_D
    blk = GDN_BLOCK
    nblk = seq // blk
    hp = jnp.zeros((8, LANES), F32)
    hp = hp.at[0, nh:2 * nh].set(a_log.astype(F32)).at[1, nh:2 * nh].set(dt_bias.astype(F32))

    def in_map(b, n):
        return (b * nblk + jnp.minimum(n, nblk - 1), 0)

    return pl.pallas_call(
        _gdn_kernel,
        out_shape=jax.ShapeDtypeStruct((t, nh * d), BF16),
        grid=(batch, nblk + 1),
        in_specs=[pl.BlockSpec((blk, 4 * nh * d), in_map),
                  pl.BlockSpec((blk, LANES), in_map),
                  pl.BlockSpec((CONV_W, 3 * nh * d), lambda b, n: (0, 0)),
                  pl.BlockSpec((8, LANES), lambda b, n: (0, 0)),
                  pl.BlockSpec((1, d), lambda b, n: (0, 0))],
        out_specs=pl.BlockSpec((blk, nh * d), lambda b, n: (b * nblk + jnp.maximum(n - 1, 0), 0)),
        scratch_shapes=[pltpu.VMEM((HALO + blk, 3 * nh * d), F32),
                        pltpu.VMEM((nh, d, d), F32),
                        pltpu.VMEM((2, blk, nh * d), F32), pltpu.VMEM((2, blk, nh * d), F32),
                        pltpu.VMEM((2, blk, nh * d), F32), pltpu.VMEM((2, blk, nh * d), BF16),
                        pltpu.VMEM((2, blk, LANES), F32), pltpu.VMEM((2, LANES, blk), F32),
                        pltpu.VMEM((2, blk, LANES), F32)],
        compiler_params=_cparams(("arbitrary", "arbitrary")),
        name="gdn_core",
    )(proj, gates, conv_w.astype(F32), hp, out_norm.reshape(1, d).astype(F32))


def _swiglu_kernel(te_ref, first_ref, nv_ref, x_ref, g_ref, wg_hbm, wu_hbm, wd_hbm, o_ref,
                   wg_c, wu_c, wd_c, stage_in, stage_out, sems, *, pre_norm, tf):
    i = pl.program_id(0)
    nf = wg_c.shape[0]
    e = te_ref[i]
    valid = i < nv_ref[0]

    def chunk_copies(j, slot):
        cols = pl.ds(j * tf, tf)
        return (pltpu.make_async_copy(wg_hbm.at[e, :, cols], stage_in.at[slot, 0], sems.at[slot, 0]),
                pltpu.make_async_copy(wu_hbm.at[e, :, cols], stage_in.at[slot, 1], sems.at[slot, 1]),
                pltpu.make_async_copy(wd_hbm.at[e, cols, :], stage_out.at[slot], sems.at[slot, 2]))

    def prepare_rows():
        x = x_ref[...].astype(F32)
        if pre_norm:
            ms = jnp.mean(x * x, axis=-1, keepdims=True)
            x = (x * lax.rsqrt(ms + EPS)) * g_ref[...]
        return x.astype(BF16)

    def chunk(xb, j):
        hid = _silu(_dot(xb, wg_c[j])) * _dot(xb, wu_c[j])
        return _dot(hid.astype(BF16), wd_c[j])

    def finish(acc):
        if pre_norm:
            o_ref[...] = (x_ref[...] + acc).astype(o_ref.dtype)
        else:
            o_ref[...] = acc.astype(o_ref.dtype)

    @pl.when(valid & (first_ref[i] == 1))
    def _():
        for c in chunk_copies(0, 0):
            c.start()
        xb = prepare_rows()
        acc = None
        for j in range(nf):
            slot = j % 2
            if j + 1 < nf:
                for c in chunk_copies(j + 1, 1 - slot):
                    c.start()
            for c in chunk_copies(j, slot):
                c.wait()
            wg_c[j] = stage_in[slot, 0].astype(BF16)
            wu_c[j] = stage_in[slot, 1].astype(BF16)
            wd_c[j] = stage_out[slot].astype(BF16)
            y = chunk(xb, j)
            acc = y if acc is None else acc + y
        finish(acc)

    @pl.when(valid & (first_ref[i] != 1))
    def _():
        xb = prepare_rows()
        acc = None
        for j in range(nf):
            y = chunk(xb, j)
            acc = y if acc is None else acc + y
        finish(acc)

    @pl.when(jnp.logical_not(valid))
    def _():
        o_ref[...] = jnp.zeros_like(o_ref)


def expert_swiglu(x, gain, tile_expert, tile_first, n_valid, wg, wu, wd, tile_rows, tf, out_dtype, pre_norm, name):
    n_rows, d = x.shape
    ne, _, f = wg.shape
    n_tiles = n_rows // tile_rows
    nf = f // tf
    grid_spec = pltpu.PrefetchScalarGridSpec(
        num_scalar_prefetch=3,
        grid=(n_tiles,),
        in_specs=[pl.BlockSpec((tile_rows, d), lambda i, te, fi, nv: (jnp.minimum(i, nv[0] - 1), 0)),
                  pl.BlockSpec((1, d), lambda i, te, fi, nv: (0, 0)),
                  pl.BlockSpec(memory_space=pl.ANY),
                  pl.BlockSpec(memory_space=pl.ANY),
                  pl.BlockSpec(memory_space=pl.ANY)],
        out_specs=pl.BlockSpec((tile_rows, d), lambda i, te, fi, nv: (i, 0)),
        scratch_shapes=[pltpu.VMEM((nf, d, tf), BF16), pltpu.VMEM((nf, d, tf), BF16), pltpu.VMEM((nf, tf, d), BF16),
                        pltpu.VMEM((2, 2, d, tf), F32), pltpu.VMEM((2, tf, d), F32),
                        pltpu.SemaphoreType.DMA((2, 3))],
    )
    return pl.pallas_call(
        functools.partial(_swiglu_kernel, pre_norm=pre_norm, tf=tf),
        out_shape=jax.ShapeDtypeStruct((n_rows, d), out_dtype),
        grid_spec=grid_spec,
        compiler_params=_cparams(("arbitrary",)),
        name=name,
    )(tile_expert, tile_first, n_valid, x, gain.reshape(1, d).astype(F32), wg, wu, wd)


def ffn_dense(x, gain, wg, wu, wd, tm, tf):
    t = x.shape[0]
    n_tiles = t // tm
    tile_first = jnp.zeros((n_tiles,), jnp.int32).at[0].set(1)
    return expert_swiglu(x, gain, jnp.zeros((n_tiles,), jnp.int32), tile_first, jnp.full((1,), n_tiles, jnp.int32),
                         wg[None], wu[None], wd[None], tm, tf, F32, True, "ffn_dense")


def _t5_bucket_np(dist):
    max_exact = N_BUCKETS // 2
    n = np.maximum(dist, 0)
    safe = np.maximum(n, 1).astype(np.float32)
    large = max_exact + (np.log(safe / max_exact) / np.log(MAX_DIST / max_exact)
                         * (N_BUCKETS - max_exact)).astype(np.int32)
    large = np.minimum(large, N_BUCKETS - 1)
    return np.where(n < max_exact, n, large).astype(np.int32)


def _bias_kernel(bucket_ref, rb_ref, o_ref):
    bucket = bucket_ref[...]
    for h in range(SW_HEADS):
        acc = jnp.zeros(bucket.shape, F32)
        for b in range(N_BUCKETS):
            acc = jnp.where(bucket == b, rb_ref[b, h], acc)
        o_ref[h] = acc


def bias_table(rel_bias):
    qi = np.arange(WINDOW)[:, None] + WINDOW
    kj = np.arange(2 * WINDOW)[None, :]
    bucket = jnp.asarray(_t5_bucket_np(qi - kj))
    return pl.pallas_call(
        _bias_kernel,
        out_shape=jax.ShapeDtypeStruct((SW_HEADS, WINDOW, 2 * WINDOW), F32),
        in_specs=[pl.BlockSpec(memory_space=pltpu.VMEM), pl.BlockSpec(memory_space=pltpu.SMEM)],
        out_specs=pl.BlockSpec(memory_space=pltpu.VMEM),
        name="t5_bias_table",
    )(bucket, rel_bias.astype(F32))


def _swa_kernel(q_ref, kvp_ref, kvc_ref, bias_ref, qn_ref, kn_ref, sink_ref, o_ref):
    n = pl.program_id(1)
    blk, hd = WINDOW, SW_HD
    kv_w = SW_KV_HEADS * hd
    qi = lax.broadcasted_iota(jnp.int32, (blk, 2 * blk), 0) + blk
    kj = lax.broadcasted_iota(jnp.int32, (blk, 2 * blk), 1)
    dist = qi - kj
    first_key = jnp.where(n > 0, 0, blk)
    mask = (dist >= 0) & (dist < WINDOW) & (kj >= first_key)
    gw = 2 * LANES
    gi = lax.broadcasted_iota(jnp.int32, (gw, gw), 0)
    gj = lax.broadcasted_iota(jnp.int32, (gw, gw), 1)
    group_ones = jnp.where((gi // hd) == (gj // hd), 1.0, 0.0).astype(BF16)
    lane = lax.broadcasted_iota(jnp.int32, (1, LANES), 1)
    low_half = lane < hd

    def head_norm(x, gain):
        cols = []
        for c0 in range(0, x.shape[1], gw):
            xc = x[:, c0:c0 + gw]
            ss = _dot((xc * xc).astype(BF16), group_ones)
            cols.append(xc * lax.rsqrt(ss * (1.0 / hd) + EPS))
        return jnp.concatenate(cols, axis=1) * gain

    def dup_half(x, half):
        swapped = pltpu.roll(x, hd, 1)
        return jnp.where(low_half == (half == 0), x, swapped)

    qn = head_norm(q_ref[...].astype(F32), qn_ref[...]) * (hd ** -0.5)
    half_sel = [jnp.where(low_half, 1.0, 0.0), jnp.where(low_half, 0.0, 1.0)]
    k_all = jnp.concatenate([kvp_ref[:, 0:kv_w], kvc_ref[:, 0:kv_w]], axis=0).astype(F32)
    kn = head_norm(k_all, kn_ref[...])
    v_all = jnp.concatenate([kvp_ref[:, kv_w:2 * kv_w], kvc_ref[:, kv_w:2 * kv_w]], axis=0).astype(F32)
    ks, vs = [], []
    for g in range(SW_KV_HEADS):
        c0 = (g // 2) * LANES
        ks.append(dup_half(kn[:, c0:c0 + LANES], g % 2).astype(BF16))
        vs.append(dup_half(v_all[:, c0:c0 + LANES], g % 2).astype(BF16))

    scores = []
    for hq in range(SW_HEADS):
        c0 = (hq // 2) * LANES
        q_h = (qn[:, c0:c0 + LANES] * half_sel[hq % 2]).astype(BF16)
        scores.append(_dot_nt(q_h, ks[hq // SW_GROUP]))
    probs = []
    for hq in range(SW_HEADS):
        s = jnp.where(mask, scores[hq] + bias_ref[hq], NEG_INF)
        sink = sink_ref[hq]
        mx = jnp.maximum(jnp.max(s, axis=-1, keepdims=True), sink)
        p = jnp.exp(s - mx)
        denom = jnp.sum(p, axis=-1, keepdims=True) + jnp.exp(sink - mx)
        probs.append((p / denom).astype(BF16))
    outs = [_dot(probs[hq], vs[hq // SW_GROUP]) for hq in range(SW_HEADS)]
    for c in range(SW_HEADS // 2):
        o_ref[:, c * LANES:(c + 1) * LANES] = jnp.where(low_half, outs[2 * c], outs[2 * c + 1]).astype(o_ref.dtype)


def swa_attention(q, kv, bias, q_norm, k_norm, sinks, batch, seq):
    t = q.shape[0]
    blk = WINDOW
    nb = seq // blk
    qw = SW_HEADS * SW_HD
    kvw = 2 * SW_KV_HEADS * SW_HD
    return pl.pallas_call(
        _swa_kernel,
        out_shape=jax.ShapeDtypeStruct((t, qw), BF16),
        grid=(batch, nb),
        in_specs=[pl.BlockSpec((blk, qw), lambda b, n: (b * nb + n, 0)),
                  pl.BlockSpec((blk, kvw), lambda b, n: (b * nb + jnp.maximum(n - 1, 0), 0)),
                  pl.BlockSpec((blk, kvw), lambda b, n: (b * nb + n, 0)),
                  pl.BlockSpec((SW_HEADS, blk, 2 * blk), lambda b, n: (0, 0, 0)),
                  pl.BlockSpec((1, qw), lambda b, n: (0, 0)),
                  pl.BlockSpec((1, kvw // 2), lambda b, n: (0, 0)),
                  pl.BlockSpec(memory_space=pltpu.SMEM)],
        out_specs=pl.BlockSpec((blk, qw), lambda b, n: (b * nb + n, 0)),
        compiler_params=_cparams(("parallel", "parallel")),
        name="swa_attention",
    )(q, kv, kv, bias, jnp.tile(q_norm.astype(F32), SW_HEADS).reshape(1, qw),
      jnp.tile(k_norm.astype(F32), SW_KV_HEADS).reshape(1, kvw // 2), sinks.astype(F32))


def _route_kernel(x_ref, g_ref, wr_ref, r_ref, wt_ref, cnt_ref, sel_s, gw_s, cnt_s, start_s, run_s, *, tile_rows):
    ne = N_EXPERTS
    p = pl.program_id(0)
    i = pl.program_id(1)
    tm = x_ref.shape[0]
    sub = lax.broadcasted_iota(jnp.int32, (ne, tm), 0).astype(F32)

    @pl.when(p == 0)
    def _():
        @pl.when(i == 0)
        def _():
            cnt_s[...] = jnp.zeros_like(cnt_s)

        x = x_ref[...]
        ms = jnp.mean(x * x, axis=-1, keepdims=True)
        xn32 = (x * lax.rsqrt(ms + EPS)) * g_ref[...]
        xn_hi = xn32.astype(BF16)
        xn_lo = (xn32 - xn_hi.astype(F32)).astype(BF16)
        p_hi = _dot_nt(wr_ref[...], xn_hi)
        p_lo = _dot_nt(wr_ref[...], xn_lo)
        logits = p_hi[0:ne] + p_hi[ne:2 * ne] + p_lo[0:ne]
        m1 = jnp.max(logits, axis=0, keepdims=True)
        i1 = jnp.min(jnp.where(logits == m1, sub, float(ne)), axis=0, keepdims=True)
        l2 = jnp.where(sub == i1, -jnp.inf, logits)
        m2 = jnp.max(l2, axis=0, keepdims=True)
        i2 = jnp.min(jnp.where(l2 == m2, sub, float(ne)), axis=0, keepdims=True)
        e2 = jnp.exp(m2 - m1)
        w1 = 1.0 / (1.0 + e2)
        w2 = e2 / (1.0 + e2)
        sel = jnp.where((sub == i1) | (sub == i2), 1.0, 0.0)
        sel_s[i] = sel
        gw_s[i] = jnp.where(sub == i1, w1, jnp.where(sub == i2, w2, 0.0))
        cnt_s[...] += jnp.sum(sel, axis=1, keepdims=True)

    @pl.when(p == 1)
    def _():
        @pl.when(i == 0)
        def _():
            cnt = cnt_s[...]
            padded = jnp.floor((cnt + (tile_rows - 1)) * (1.0 / tile_rows)) * tile_rows
            sub8 = lax.broadcasted_iota(jnp.int32, cnt.shape, 0)
            start = jnp.zeros_like(cnt)
            for e in range(ne - 1):
                start = start + jnp.where(sub8 > e, padded[e:e + 1, :], 0.0)
            start_s[...] = start
            run_s[...] = jnp.zeros_like(run_s)
            cnt_ref[...] = cnt

        sel = sel_s[i]
        gw = gw_s[i]
        ti = lax.broadcasted_iota(jnp.int32, (tm, tm), 0)
        tj = lax.broadcasted_iota(jnp.int32, (tm, tm), 1)
        tri = jnp.where(ti <= tj, 1.0, 0.0).astype(BF16)
        csum = _dot(sel.astype(BF16), tri)
        slot = start_s[:, 0:1] + run_s[:, 0:1] + csum - sel
        run_s[...] += csum[:, tm - 1:tm]
        ia = jnp.min(jnp.where(sel > 0.0, sub, float(ne)), axis=0, keepdims=True)
        ib = jnp.max(jnp.where(sel > 0.0, sub, -1.0), axis=0, keepdims=True)
        pick_a = sub == ia
        pick_b = sub == ib
        rows = [jnp.sum(jnp.where(pick_a, slot, 0.0), axis=0, keepdims=True),
                jnp.sum(jnp.where(pick_b, slot, 0.0), axis=0, keepdims=True),
                jnp.sum(jnp.where(pick_a, gw, 0.0), axis=0, keepdims=True),
                jnp.sum(jnp.where(pick_b, gw, 0.0), axis=0, keepdims=True)]
        r_ref[...] = jnp.concatenate(rows + [jnp.zeros((ne - 4, tm), F32)], axis=0)
        wpad = jnp.concatenate(rows[2:4] + [jnp.zeros((LANES - 2, tm), F32)], axis=0)
        wt_ref[...] = wpad.T


def moe_route(x, gain, w_router, tm, tile_rows):
    t, d = x.shape
    ne = w_router.shape[1]
    assert ne == N_EXPERTS
    w_hi = w_router.astype(BF16)
    w_lo = (w_router - w_hi.astype(F32)).astype(BF16)
    wr = jnp.concatenate([w_hi.T, w_lo.T], axis=0)
    tm = min(tm, t)
    nt = t // tm
    return pl.pallas_call(
        functools.partial(_route_kernel, tile_rows=tile_rows),
        out_shape=(jax.ShapeDtypeStruct((ne, t), F32), jax.ShapeDtypeStruct((t, LANES), F32),
                   jax.ShapeDtypeStruct((ne, LANES), F32)),
        grid=(2, nt),
        in_specs=[pl.BlockSpec((tm, d), lambda p, i: (i * (1 - p) + (nt - 1) * p, 0)),
                  pl.BlockSpec((1, d), lambda p, i: (0, 0)),
                  pl.BlockSpec((2 * ne, d), lambda p, i: (0, 0))],
        out_specs=(pl.BlockSpec((ne, tm), lambda p, i: (0, i * p)),
                   pl.BlockSpec((tm, LANES), lambda p, i: (i * p, 0)),
                   pl.BlockSpec((ne, LANES), lambda p, i: (0, 0))),
        scratch_shapes=[pltpu.VMEM((nt, ne, tm), F32), pltpu.VMEM((nt, ne, tm), F32),
                        pltpu.VMEM((ne, LANES), F32), pltpu.VMEM((ne, LANES), F32), pltpu.VMEM((ne, LANES), F32)],
        compiler_params=_cparams(("arbitrary", "arbitrary")),
        name="moe_route",
    )(x, gain.reshape(1, d), wr)


def _dispatch_kernel(zf_ref, slots_ref, x_ref, g_ref, xs_ref, xn_s, zero_s, sem, zsem, *, tile_rows):
    tm = x_ref.shape[0]

    @pl.when(pl.program_id(0) == 0)
    def _():
        zero_s[...] = jnp.zeros_like(zero_s)

        def zero_copy(e):
            row0 = pl.multiple_of(zf_ref[e], tile_rows)
            return pltpu.make_async_copy(zero_s, xs_ref.at[pl.ds(row0, tile_rows)], zsem)

        for e in range(zf_ref.shape[0]):
            @pl.when(zf_ref[e] >= 0)
            def _():
                zero_copy(e).start()
        for e in range(zf_ref.shape[0]):
            @pl.when(zf_ref[e] >= 0)
            def _():
                zero_copy(e).wait()

    x = x_ref[...]
    ms = jnp.mean(x * x, axis=-1, keepdims=True)
    xn_s[...] = (x * lax.rsqrt(ms + EPS)) * g_ref[...]

    def row_copy(r, k):
        return pltpu.make_async_copy(xn_s.at[pl.ds(r, 1)], xs_ref.at[pl.ds(slots_ref[0, k, r], 1)], sem)

    def start(r, c):
        row_copy(r, 0).start()
        row_copy(r, 1).start()
        return c

    def wait(r, c):
        row_copy(r, 0).wait()
        row_copy(r, 1).wait()
        return c

    lax.fori_loop(0, tm, start, 0, unroll=8)
    lax.fori_loop(0, tm, wait, 0, unroll=8)


def moe_dispatch(x, gain, slots, zf_rows, n_slots, tm, tile_rows):
    t, d = x.shape
    tm = min(tm, t)
    nt = t // tm
    slots3 = slots.reshape(2, nt, tm).transpose(1, 0, 2)
    grid_spec = pltpu.PrefetchScalarGridSpec(
        num_scalar_prefetch=1,
        grid=(nt,),
        in_specs=[pl.BlockSpec((1, 2, tm), lambda i, zf: (i, 0, 0), memory_space=pltpu.SMEM),
                  pl.BlockSpec((tm, d), lambda i, zf: (i, 0)),
                  pl.BlockSpec((1, d), lambda i, zf: (0, 0))],
        out_specs=pl.BlockSpec(memory_space=pl.ANY),
        scratch_shapes=[pltpu.VMEM((tm, d), F32), pltpu.VMEM((tile_rows, d), F32),
                        pltpu.SemaphoreType.DMA, pltpu.SemaphoreType.DMA],
    )
    return pl.pallas_call(
        functools.partial(_dispatch_kernel, tile_rows=tile_rows),
        out_shape=jax.ShapeDtypeStruct((n_slots, d), F32),
        grid_spec=grid_spec,
        compiler_params=_cparams(("arbitrary",)),
        name="moe_dispatch",
    )(zf_rows, slots3, x, gain.reshape(1, d))


def _combine_kernel(slots_ref, h_ref, wt_ref, ys_ref, o_ref, buf, sem):
    tm = h_ref.shape[0]

    def row_copy(r, k):
        return pltpu.make_async_copy(ys_ref.at[pl.ds(slots_ref[0, k, r], 1)], buf.at[k, pl.ds(r, 1)], sem)

    def start(r, c):
        row_copy(r, 0).start()
        row_copy(r, 1).start()
        return c

    def wait(r, c):
        row_copy(r, 0).wait()
        row_copy(r, 1).wait()
        return c

    lax.fori_loop(0, tm, start, 0, unroll=8)
    lax.fori_loop(0, tm, wait, 0, unroll=8)
    wt = wt_ref[...]
    o_ref[...] = h_ref[...] + wt[:, 0:1] * buf[0] + wt[:, 1:2] * buf[1]


def moe_combine(h, wt, slots, ys, tm):
    t, d = h.shape
    tm = min(tm, t)
    nt = t // tm
    slots3 = slots.reshape(2, nt, tm).transpose(1, 0, 2)
    return pl.pallas_call(
        _combine_kernel,
        out_shape=jax.ShapeDtypeStruct((t, d), F32),
        grid=(nt,),
        in_specs=[pl.BlockSpec((1, 2, tm), lambda i: (i, 0, 0), memory_space=pltpu.SMEM),
                  pl.BlockSpec((tm, d), lambda i: (i, 0)),
                  pl.BlockSpec((tm, LANES), lambda i: (i, 0)),
                  pl.BlockSpec(memory_space=pl.ANY)],
        out_specs=pl.BlockSpec((tm, d), lambda i: (i, 0)),
        scratch_shapes=[pltpu.VMEM((2, tm, d), F32), pltpu.SemaphoreType.DMA],
        compiler_params=_cparams(("arbitrary",)),
        name="moe_combine",
    )(slots3, h, wt, ys)


MOE_TILE_ROWS = 512
TOP_K = 2


def moe_layer(h, gain, w_router, wg, wu, wd):
    t, d = h.shape
    ne = w_router.shape[1]
    tr = MOE_TILE_ROWS
    n_tiles = -(-(TOP_K * t + ne * (tr - 1)) // tr)
    n_slots = n_tiles * tr

    r, wt, cnt = moe_route(h, gain, w_router, 512, tr)
    slots = r[0:2].astype(jnp.int32)

    counts = cnt[:, 0].astype(jnp.int32)
    padded = ((counts + (tr - 1)) // tr) * tr
    ends = jnp.cumsum(padded)
    n_valid = (ends[-1] // tr).astype(jnp.int32)
    tile_row0 = jnp.arange(n_tiles, dtype=jnp.int32) * tr
    tile_expert = jnp.sum((tile_row0[:, None] >= ends[None, :]).astype(jnp.int32), axis=1)
    tile_expert = jnp.minimum(tile_expert, ne - 1)
    tile_expert = jnp.where(jnp.arange(n_tiles) < n_valid, tile_expert, tile_expert[jnp.maximum(n_valid - 1, 0)])
    prev_expert = jnp.concatenate([jnp.full((1,), -1, jnp.int32), tile_expert[:-1]])
    tile_first = (tile_expert != prev_expert).astype(jnp.int32)
    tail = jnp.arange(TOP_K * t // tr, n_tiles, dtype=jnp.int32)
    zf_rows = jnp.concatenate([jnp.where(padded > 0, ends - tr, -1),
                               jnp.where(tail >= n_valid, tail * tr, -1)]).astype(jnp.int32)

    xs = moe_dispatch(h, gain, slots, zf_rows, n_slots, 512, tr)
    ys = expert_swiglu(xs, gain, tile_expert, tile_first, n_valid.reshape(1), wg, wu, wd, tr, 512, F32, False,
                       "moe_experts")
    return moe_combine(h, wt, slots, ys, 256)


def kernel(x, a_norm, a_w_in, a_conv, a_log_decay, a_dt_bias, a_out_norm, a_w_out, kv_norm, kv_w, k_norm,
           b_norm, b_w_q, q_norm, b_sinks, b_w_o, rel_bias, ffn_norm, dense_w_gate, dense_w_up, dense_w_down,
           moe_router, moe_w_gate, moe_w_up, moe_w_down):
    batch, seq, d = x.shape
    t = batch * seq
    nh, hd = LA_HEADS, LA_D
    main_w = 4 * nh * hd
    h0 = x.reshape(t, d)

    w_in = a_w_in[0]
    w_main = w_in[:, 0:main_w].astype(BF16)
    w_gate = jnp.zeros((d, LANES), BF16).at[:, 0:2 * nh].set(w_in[:, main_w:main_w + 2 * nh].astype(BF16))
    proj, gates = norm_matmul(h0, [(a_norm[0], w_main, BF16), (a_norm[0], w_gate, F32)], 512, "gdn_in_proj")
    o = gdn_core(proj, gates, a_conv[0], a_log_decay[0], a_dt_bias[0], a_out_norm[0], batch, seq)
    h1 = matmul_residual(o, a_w_out[0].astype(BF16), h0, 1024, 1024, "gdn_out_proj")

    h2 = ffn_dense(h1, ffn_norm[0], dense_w_gate[0], dense_w_up[0], dense_w_down[0], 512, 512)

    kv, q = norm_matmul(h2, [(kv_norm, kv_w.astype(BF16), BF16), (b_norm[0], b_w_q[0].astype(BF16), BF16)],
                        1024, "qkv_proj")
    bias = bias_table(rel_bias)
    attn = swa_attention(q, kv, bias, q_norm[0], k_norm, b_sinks[0], batch, seq)
    h3 = matmul_residual(attn, b_w_o[0].astype(BF16), h2, 1024, 1024, "attn_out_proj")

    h4 = moe_layer(h3, ffn_norm[1], moe_router[0], moe_w_gate[0], moe_w_up[0], moe_w_down[0])
    return h4.reshape(batch, seq, d)
```

```python
import functools

import numpy as np
import jax
import jax.numpy as jnp
from jax import lax
from jax.experimental import pallas as pl
from jax.experimental.pallas import tpu as pltpu

F32 = jnp.float32
BF16 = jnp.bfloat16

EPS = 1e-6
NEG_INF = -1e30

LA_HEADS = 8
LA_D = 128
CONV_W = 4
CHUNK = 64
SW_HEADS = 16
SW_KV_HEADS = 4
SW_GROUP = SW_HEADS // SW_KV_HEADS
SW_HD = 64
WINDOW = 128
N_BUCKETS = 32
MAX_DIST = 128
N_EXPERTS = 8

LANES = 128
GDN_BLOCK = 2 * CHUNK
HALO = 8

VMEM_LIMIT = 56 * 1024 * 1024


def _cparams(sem):
    return pltpu.CompilerParams(dimension_semantics=sem, vmem_limit_bytes=VMEM_LIMIT)


def _silu(x):
    return x * (1.0 / (1.0 + jnp.exp(-x)))


def _dot(a, b):
    return jnp.dot(a, b, preferred_element_type=F32)


def _dot_nt(a, b):
    return lax.dot_general(a, b, (((1,), (1,)), ((), ())), preferred_element_type=F32)


def _norm_matmul_kernel(*refs, n_groups):
    x_ref = refs[0]
    g_refs = refs[1:1 + n_groups]
    w_refs = refs[1 + n_groups:1 + 2 * n_groups]
    o_refs = refs[1 + 2 * n_groups:1 + 3 * n_groups]
    x = x_ref[...]
    xr = x * lax.rsqrt(jnp.mean(x * x, axis=-1, keepdims=True) + EPS)
    for g_ref, w_ref, o_ref in zip(g_refs, w_refs, o_refs):
        o_ref[...] = _dot((xr * g_ref[...]).astype(BF16), w_ref[...]).astype(o_ref.dtype)


def norm_matmul(x, groups, tm, name):
    t, d = x.shape
    tm = min(tm, t)
    gains = [g.reshape(1, d).astype(F32) for g, _, _ in groups]
    ws = [w for _, w, _ in groups]
    return pl.pallas_call(
        functools.partial(_norm_matmul_kernel, n_groups=len(groups)),
        out_shape=[jax.ShapeDtypeStruct((t, w.shape[1]), dt) for _, w, dt in groups],
        grid=(t // tm,),
        in_specs=([pl.BlockSpec((tm, d), lambda i: (i, 0))]
                  + [pl.BlockSpec((1, d), lambda i: (0, 0)) for _ in groups]
                  + [pl.BlockSpec(w.shape, lambda i: (0, 0)) for w in ws]),
        out_specs=[pl.BlockSpec((tm, w.shape[1]), lambda i: (i, 0)) for w in ws],
        compiler_params=_cparams(("parallel",)),
        name=name,
    )(x, *gains, *ws)


def _matmul_res_kernel(a_ref, w_ref, r_ref, o_ref):
    o_ref[...] = r_ref[...] + _dot(a_ref[...], w_ref[...])


def matmul_residual(a, w, res, tm, tn, name):
    t, k = a.shape
    n = w.shape[1]
    tm, tn = min(tm, t), min(tn, n)
    return pl.pallas_call(
        _matmul_res_kernel,
        out_shape=jax.ShapeDtypeStruct((t, n), F32),
        grid=(t // tm, n // tn),
        in_specs=[pl.BlockSpec((tm, k), lambda i, j: (i, 0)),
                  pl.BlockSpec((k, tn), lambda i, j: (0, j)),
                  pl.BlockSpec((tm, tn), lambda i, j: (i, j))],
        out_specs=pl.BlockSpec((tm, tn), lambda i, j: (i, j)),
        compiler_params=_cparams(("parallel", "parallel")),
        name=name,
    )(a, w, res)


def _gdn_kernel(proj_ref, gates_ref, convw_ref, hp_ref, onorm_ref, o_ref,
                xs_ref, state_ref, q_s, k_s, v_s, z_s, gc_s, gct_s, beta_s):
    n = pl.program_id(1)

    @pl.when(n == 0)
    def _():
        xs_ref[0:HALO, :] = jnp.zeros((HALO, xs_ref.shape[1]), F32)
        for ref in (q_s, k_s, v_s, z_s, gc_s, gct_s, beta_s):
            ref[1] = jnp.zeros(ref.shape[1:], ref.dtype)

    @pl.when(n <= 1)
    def _():
        state_ref[...] = jnp.zeros_like(state_ref)

    args = (proj_ref, gates_ref, convw_ref, hp_ref, onorm_ref, o_ref, xs_ref, state_ref,
            q_s, k_s, v_s, z_s, gc_s, gct_s, beta_s)

    @pl.when(lax.rem(n, 2) == 0)
    def _():
        _gdn_step(*args, slot_w=0, slot_r=1)

    @pl.when(lax.rem(n, 2) == 1)
    def _():
        _gdn_step(*args, slot_w=1, slot_r=0)


def _gdn_step(proj_ref, gates_ref, convw_ref, hp_ref, onorm_ref, o_ref, xs_ref, state_ref,
              q_s, k_s, v_s, z_s, gc_s, gct_s, beta_s, *, slot_w, slot_r):
    nh, d, c = LA_HEADS, LA_D, CHUNK
    blk = GDN_BLOCK
    qkv_w = 3 * nh * d

    gc = gc_s[slot_r]
    gc_t = gct_s[slot_r]
    beta = beta_s[slot_r]

    xs_ref[HALO:HALO + blk, :] = proj_ref[:, 0:qkv_w].astype(F32)

    def front_gates():
        _gdn_front_gates(gates_ref, hp_ref, gc_s, gct_s, beta_s, slot_w)

    ci = lax.broadcasted_iota(jnp.int32, (c, c), 0)
    cj = lax.broadcasted_iota(jnp.int32, (c, c), 1)
    lower_incl = ci >= cj
    strict = ci > cj
    eye_c = jnp.where(ci == cj, 1.0, 0.0).astype(F32)
    di = lax.broadcasted_iota(jnp.int32, (d, d), 0)
    dj = lax.broadcasted_iota(jnp.int32, (d, d), 1)
    eye_d = jnp.where(di == dj, 1.0, 0.0).astype(BF16)

    onorm = onorm_ref[...]

    sub_halo = lax.broadcasted_iota(jnp.int32, (HALO, d), 0)

    def conv_silu(col0):
        x_cur = xs_ref[HALO:HALO + blk, col0:col0 + d]
        x_tail = xs_ref[0:HALO, col0:col0 + d]
        acc = convw_ref[CONV_W - 1:CONV_W, col0:col0 + d] * x_cur
        for j in range(CONV_W - 1):
            s = CONV_W - 1 - j
            rolled = pltpu.roll(x_cur, s, 0)
            head = jnp.where(sub_halo < s, pltpu.roll(x_tail, s, 0), rolled[0:HALO])
            shifted = jnp.concatenate([head, rolled[HALO:blk]], axis=0)
            acc = acc + convw_ref[j:j + 1, col0:col0 + d] * shifted
        return _silu(acc)

    def front_head(h):
        qf = conv_silu(h * d)
        kf = conv_silu(nh * d + h * d)
        vf = conv_silu(2 * nh * d + h * d)
        q_s[slot_w, :, h * d:(h + 1) * d] = qf * (lax.rsqrt(jnp.sum(qf * qf, axis=-1, keepdims=True) + EPS)
                                                  * (d ** -0.5))
        k_s[slot_w, :, h * d:(h + 1) * d] = kf * lax.rsqrt(jnp.sum(kf * kf, axis=-1, keepdims=True) + EPS)
        v_s[slot_w, :, h * d:(h + 1) * d] = vf
        z_s[slot_w, :, h * d:(h + 1) * d] = proj_ref[:, qkv_w + h * d:qkv_w + (h + 1) * d]

    front_tasks = [front_gates] + [functools.partial(front_head, h) for h in range(nh)]

    def run_front_task():
        if front_tasks:
            front_tasks.pop(0)()

    n_ck = blk // c
    chains = [(h, ck) for h in range(nh) for ck in range(n_ck)]

    st = {}
    for (h, ck) in chains:
        r = ck * c
        q = q_s[slot_r, r:r + c, h * d:(h + 1) * d]
        k = k_s[slot_r, r:r + c, h * d:(h + 1) * d]
        v = v_s[slot_r, r:r + c, h * d:(h + 1) * d]
        g_col = gc[r:r + c, nh + h:nh + h + 1]
        g_row = gc_t[nh + h:nh + h + 1, r:r + c]
        g_last = gc[r + c - 1:r + c, nh + h:nh + h + 1]
        b_col = beta[r:r + c, h:h + 1]
        decay = jnp.where(lower_incl, jnp.exp2(jnp.where(lower_incl, g_col - g_row, 0.0)), 0.0)
        k_beta = k * b_col
        e_col = jnp.exp2(g_col)
        lhs = jnp.concatenate([k_beta.astype(BF16), q.astype(BF16), eye_d], axis=0)
        kk = _dot_nt(lhs, k.astype(BF16))
        a_mat = jnp.where(strict, kk[0:c] * decay, 0.0)
        st[(h, ck)] = dict(
            a=a_mat, attn=(kk[c:2 * c] * decay).astype(BF16),
            k_tail_t=(kk[2 * c:2 * c + d] * jnp.exp2(g_last - g_row)).astype(BF16),
            rhs=jnp.concatenate([(v * b_col).astype(BF16), (k_beta * e_col).astype(BF16)], axis=1),
            qe=(q * e_col).astype(BF16), e_last=jnp.exp2(g_last))
    run_front_task()

    for key in chains:
        x_b = (-st[key]["a"]).astype(BF16)
        st[key]["y"] = _dot(x_b, x_b)
        st[key]["p"] = eye_c - st[key]["a"]
    run_front_task()
    n_levels = int(np.log2(c))
    for lvl in range(1, n_levels):
        for key in chains:
            y_b = st[key]["y"].astype(BF16)
            p = st[key]["p"]
            if lvl + 1 < n_levels:
                zz = _dot(jnp.concatenate([y_b, p.astype(BF16)], axis=0), y_b)
                st[key]["y"] = zz[0:c]
                st[key]["p"] = p + zz[c:2 * c]
            else:
                st[key]["p"] = p + _dot(p.astype(BF16), y_b)
        run_front_task()
    for key in chains:
        st[key]["uw"] = _dot(st[key]["p"].astype(BF16), st[key]["rhs"])
    run_front_task()

    for ck in range(n_ck):
        r = ck * c
        s_old = [state_ref[h] for h in range(nh)]
        ws_qs = []
        for h in range(nh):
            cur = st[(h, ck)]
            lhs = jnp.concatenate([cur["uw"][:, d:2 * d].astype(BF16), cur["qe"]], axis=0)
            ws_qs.append(_dot(lhs, s_old[h].astype(BF16)))
        run_front_task()
        for h in range(nh):
            cur = st[(h, ck)]
            v_new = cur["uw"][:, 0:d] - ws_qs[h][0:c]
            av_kv = _dot(jnp.concatenate([cur["attn"], cur["k_tail_t"]], axis=0), v_new.astype(BF16))
            state_ref[h] = s_old[h] * cur["e_last"] + av_kv[c:c + d]
            o = ws_qs[h][c:2 * c] + av_kv[0:c]
            o = (o * lax.rsqrt(jnp.mean(o * o, axis=-1, keepdims=True) + EPS)) * onorm
            z = z_s[slot_r, r:r + c, h * d:(h + 1) * d].astype(F32)
            o_ref[r:r + c, h * d:(h + 1) * d] = (o * _silu(z)).astype(o_ref.dtype)
    while front_tasks:
        run_front_task()

    xs_ref[0:HALO, :] = xs_ref[blk:blk + HALO, :]


def _gdn_front_gates(gates_ref, hp_ref, gc_s, gct_s, beta_s, slot_w):
    blk, c = GDN_BLOCK, CHUNK
    gates = gates_ref[...]
    a_log = hp_ref[0:1, :]
    dt_bias = hp_ref[1:2, :]
    beta = 1.0 / (1.0 + jnp.exp(-gates))
    sp_in = gates + dt_bias
    softplus = jnp.maximum(sp_in, 0.0) + jnp.log(1.0 + jnp.exp(-jnp.abs(sp_in)))
    g = (-jnp.exp(a_log) * softplus) * float(np.log2(np.e))

    row = lax.broadcasted_iota(jnp.int32, (blk, blk), 0)
    col = lax.broadcasted_iota(jnp.int32, (blk, blk), 1)
    tri = jnp.where((row >= col) & ((row // c) == (col // c)), 1.0, 0.0).astype(BF16)
    g_hi = g.astype(BF16)
    g_r1 = g - g_hi.astype(F32)
    g_mid = g_r1.astype(BF16)
    g_lo = (g_r1 - g_mid.astype(F32)).astype(BF16)
    gc = _dot(tri, g_hi) + _dot(tri, g_mid) + _dot(tri, g_lo)
    gc_s[slot_w] = gc
    gct_s[slot_w] = gc.T
    beta_s[slot_w] = beta


def gdn_core(proj, gates, conv_w, a_log, dt_bias, out_norm, batch, seq):
    t = proj.shape[0]
    nh, d = LA_HEADS, LA_D
    blk = GDN_BLOCK
    nblk = seq // blk
    hp = jnp.zeros((8, LANES), F32)
    hp = hp.at[0, nh:2 * nh].set(a_log.astype(F32)).at[1, nh:2 * nh].set(dt_bias.astype(F32))

    def in_map(b, n):
        return (b * nblk + jnp.minimum(n, nblk - 1), 0)

    return pl.pallas_call(
        _gdn_kernel,
        out_shape=jax.ShapeDtypeStruct((t, nh * d), BF16),
        grid=(batch, nblk + 1),
        in_specs=[pl.BlockSpec((blk, 4 * nh * d), in_map),
                  pl.BlockSpec((blk, LANES), in_map),
                  pl.BlockSpec((CONV_W, 3 * nh * d), lambda b, n: (0, 0)),
                  pl.BlockSpec((8, LANES), lambda b, n: (0, 0)),
                  pl.BlockSpec((1, d), lambda b, n: (0, 0))],
        out_specs=pl.BlockSpec((blk, nh * d), lambda b, n: (b * nblk + jnp.maximum(n - 1, 0), 0)),
        scratch_shapes=[pltpu.VMEM((HALO + blk, 3 * nh * d), F32),
                        pltpu.VMEM((nh, d, d), F32),
                        pltpu.VMEM((2, blk, nh * d), F32), pltpu.VMEM((2, blk, nh * d), F32),
                        pltpu.VMEM((2, blk, nh * d), F32), pltpu.VMEM((2, blk, nh * d), BF16),
                        pltpu.VMEM((2, blk, LANES), F32), pltpu.VMEM((2, LANES, blk), F32),
                        pltpu.VMEM((2, blk, LANES), F32)],
        compiler_params=_cparams(("arbitrary", "arbitrary")),
        name="gdn_core",
    )(proj, gates, conv_w.astype(F32), hp, out_norm.reshape(1, d).astype(F32))


def _swiglu_kernel(te_ref, first_ref, nv_ref, x_ref, g_ref, wg_hbm, wu_hbm, wd_hbm, o_ref,
                   wg_c, wu_c, wd_c, stage_in, stage_out, sems, *, pre_norm, tf):
    i = pl.program_id(0)
    nf = wg_c.shape[0]
    e = te_ref[i]
    valid = i < nv_ref[0]

    def chunk_copies(j, slot):
        cols = pl.ds(j * tf, tf)
        return (pltpu.make_async_copy(wg_hbm.at[e, :, cols], stage_in.at[slot, 0], sems.at[slot, 0]),
                pltpu.make_async_copy(wu_hbm.at[e, :, cols], stage_in.at[slot, 1], sems.at[slot, 1]),
                pltpu.make_async_copy(wd_hbm.at[e, cols, :], stage_out.at[slot], sems.at[slot, 2]))

    def prepare_rows():
        x = x_ref[...].astype(F32)
        if pre_norm:
            ms = jnp.mean(x * x, axis=-1, keepdims=True)
            x = (x * lax.rsqrt(ms + EPS)) * g_ref[...]
        return x.astype(BF16)

    def chunk(xb, j):
        hid = _silu(_dot(xb, wg_c[j])) * _dot(xb, wu_c[j])
        return _dot(hid.astype(BF16), wd_c[j])

    def finish(acc):
        if pre_norm:
            o_ref[...] = (x_ref[...] + acc).astype(o_ref.dtype)
        else:
            o_ref[...] = acc.astype(o_ref.dtype)

    @pl.when(valid & (first_ref[i] == 1))
    def _():
        for c in chunk_copies(0, 0):
            c.start()
        xb = prepare_rows()
        acc = None
        for j in range(nf):
            slot = j % 2
            if j + 1 < nf:
                for c in chunk_copies(j + 1, 1 - slot):
                    c.start()
            for c in chunk_copies(j, slot):
                c.wait()
            wg_c[j] = stage_in[slot, 0].astype(BF16)
            wu_c[j] = stage_in[slot, 1].astype(BF16)
            wd_c[j] = stage_out[slot].astype(BF16)
            y = chunk(xb, j)
            acc = y if acc is None else acc + y
        finish(acc)

    @pl.when(valid & (first_ref[i] != 1))
    def _():
        xb = prepare_rows()
        acc = None
        for j in range(nf):
            y = chunk(xb, j)
            acc = y if acc is None else acc + y
        finish(acc)

    @pl.when(jnp.logical_not(valid))
    def _():
        o_ref[...] = jnp.zeros_like(o_ref)


def expert_swiglu(x, gain, tile_expert, tile_first, n_valid, wg, wu, wd, tile_rows, tf, out_dtype, pre_norm, name):
    n_rows, d = x.shape
    ne, _, f = wg.shape
    n_tiles = n_rows // tile_rows
    nf = f // tf
    grid_spec = pltpu.PrefetchScalarGridSpec(
        num_scalar_prefetch=3,
        grid=(n_tiles,),
        in_specs=[pl.BlockSpec((tile_rows, d), lambda i, te, fi, nv: (jnp.minimum(i, nv[0] - 1), 0)),
                  pl.BlockSpec((1, d), lambda i, te, fi, nv: (0, 0)),
                  pl.BlockSpec(memory_space=pl.ANY),
                  pl.BlockSpec(memory_space=pl.ANY),
                  pl.BlockSpec(memory_space=pl.ANY)],
        out_specs=pl.BlockSpec((tile_rows, d), lambda i, te, fi, nv: (i, 0)),
        scratch_shapes=[pltpu.VMEM((nf, d, tf), BF16), pltpu.VMEM((nf, d, tf), BF16), pltpu.VMEM((nf, tf, d), BF16),
                        pltpu.VMEM((2, 2, d, tf), F32), pltpu.VMEM((2, tf, d), F32),
                        pltpu.SemaphoreType.DMA((2, 3))],
    )
    return pl.pallas_call(
        functools.partial(_swiglu_kernel, pre_norm=pre_norm, tf=tf),
        out_shape=jax.ShapeDtypeStruct((n_rows, d), out_dtype),
        grid_spec=grid_spec,
        compiler_params=_cparams(("arbitrary",)),
        name=name,
    )(tile_expert, tile_first, n_valid, x, gain.reshape(1, d).astype(F32), wg, wu, wd)


def ffn_dense(x, gain, wg, wu, wd, tm, tf):
    t = x.shape[0]
    n_tiles = t // tm
    tile_first = jnp.zeros((n_tiles,), jnp.int32).at[0].set(1)
    return expert_swiglu(x, gain, jnp.zeros((n_tiles,), jnp.int32), tile_first, jnp.full((1,), n_tiles, jnp.int32),
                         wg[None], wu[None], wd[None], tm, tf, F32, True, "ffn_dense")


def _t5_bucket_np(dist):
    max_exact = N_BUCKETS // 2
    n = np.maximum(dist, 0)
    safe = np.maximum(n, 1).astype(np.float32)
    large = max_exact + (np.log(safe / max_exact) / np.log(MAX_DIST / max_exact)
                         * (N_BUCKETS - max_exact)).astype(np.int32)
    large = np.minimum(large, N_BUCKETS - 1)
    return np.where(n < max_exact, n, large).astype(np.int32)


def _bias_kernel(bucket_ref, rb_ref, o_ref):
    bucket = bucket_ref[...]
    for h in range(SW_HEADS):
        acc = jnp.zeros(bucket.shape, F32)
        for b in range(N_BUCKETS):
            acc = jnp.where(bucket == b, rb_ref[b, h], acc)
        o_ref[h] = acc


def bias_table(rel_bias):
    qi = np.arange(WINDOW)[:, None] + WINDOW
    kj = np.arange(2 * WINDOW)[None, :]
    bucket = jnp.asarray(_t5_bucket_np(qi - kj))
    return pl.pallas_call(
        _bias_kernel,
        out_shape=jax.ShapeDtypeStruct((SW_HEADS, WINDOW, 2 * WINDOW), F32),
        in_specs=[pl.BlockSpec(memory_space=pltpu.VMEM), pl.BlockSpec(memory_space=pltpu.SMEM)],
        out_specs=pl.BlockSpec(memory_space=pltpu.VMEM),
        name="t5_bias_table",
    )(bucket, rel_bias.astype(F32))


def _swa_kernel(q_ref, kvp_ref, kvc_ref, bias_ref, qn_ref, kn_ref, sink_ref, o_ref):
    n = pl.program_id(1)
    blk, hd = WINDOW, SW_HD
    kv_w = SW_KV_HEADS * hd
    qi = lax.broadcasted_iota(jnp.int32, (blk, 2 * blk), 0) + blk
    kj = lax.broadcasted_iota(jnp.int32, (blk, 2 * blk), 1)
    dist = qi - kj
    first_key = jnp.where(n > 0, 0, blk)
    mask = (dist >= 0) & (dist < WINDOW) & (kj >= first_key)
    gw = 2 * LANES
    gi = lax.broadcasted_iota(jnp.int32, (gw, gw), 0)
    gj = lax.broadcasted_iota(jnp.int32, (gw, gw), 1)
    group_ones = jnp.where((gi // hd) == (gj // hd), 1.0, 0.0).astype(BF16)
    lane = lax.broadcasted_iota(jnp.int32, (1, LANES), 1)
    low_half = lane < hd

    def head_norm(x, gain):
        cols = []
        for c0 in range(0, x.shape[1], gw):
            xc = x[:, c0:c0 + gw]
            ss = _dot((xc * xc).astype(BF16), group_ones)
            cols.append(xc * lax.rsqrt(ss * (1.0 / hd) + EPS))
        return jnp.concatenate(cols, axis=1) * gain

    def dup_half(x, half):
        swapped = pltpu.roll(x, hd, 1)
        return jnp.where(low_half == (half == 0), x, swapped)

    qn = head_norm(q_ref[...].astype(F32), qn_ref[...]) * (hd ** -0.5)
    half_sel = [jnp.where(low_half, 1.0, 0.0), jnp.where(low_half, 0.0, 1.0)]
    k_all = jnp.concatenate([kvp_ref[:, 0:kv_w], kvc_ref[:, 0:kv_w]], axis=0).astype(F32)
    kn = head_norm(k_all, kn_ref[...])
    v_all = jnp.concatenate([kvp_ref[:, kv_w:2 * kv_w], kvc_ref[:, kv_w:2 * kv_w]], axis=0).astype(F32)
    ks, vs = [], []
    for g in range(SW_KV_HEADS):
        c0 = (g // 2) * LANES
        ks.append(dup_half(kn[:, c0:c0 + LANES], g % 2).astype(BF16))
        vs.append(dup_half(v_all[:, c0:c0 + LANES], g % 2).astype(BF16))

    scores = []
    for hq in range(SW_HEADS):
        c0 = (hq // 2) * LANES
        q_h = (qn[:, c0:c0 + LANES] * half_sel[hq % 2]).astype(BF16)
        scores.append(_dot_nt(q_h, ks[hq // SW_GROUP]))
    probs = []
    for hq in range(SW_HEADS):
        s = jnp.where(mask, scores[hq] + bias_ref[hq], NEG_INF)
        sink = sink_ref[hq]
        mx = jnp.maximum(jnp.max(s, axis=-1, keepdims=True), sink)
        p = jnp.exp(s - mx)
        denom = jnp.sum(p, axis=-1, keepdims=True) + jnp.exp(sink - mx)
        probs.append((p / denom).astype(BF16))
    outs = [_dot(probs[hq], vs[hq // SW_GROUP]) for hq in range(SW_HEADS)]
    for c in range(SW_HEADS // 2):
        o_ref[:, c * LANES:(c + 1) * LANES] = jnp.where(low_half, outs[2 * c], outs[2 * c + 1]).astype(o_ref.dtype)


def swa_attention(q, kv, bias, q_norm, k_norm, sinks, batch, seq):
    t = q.shape[0]
    blk = WINDOW
    nb = seq // blk
    qw = SW_HEADS * SW_HD
    kvw = 2 * SW_KV_HEADS * SW_HD
    return pl.pallas_call(
        _swa_kernel,
        out_shape=jax.ShapeDtypeStruct((t, qw), BF16),
        grid=(batch, nb),
        in_specs=[pl.BlockSpec((blk, qw), lambda b, n: (b * nb + n, 0)),
                  pl.BlockSpec((blk, kvw), lambda b, n: (b * nb + jnp.maximum(n - 1, 0), 0)),
                  pl.BlockSpec((blk, kvw), lambda b, n: (b * nb + n, 0)),
                  pl.BlockSpec((SW_HEADS, blk, 2 * blk), lambda b, n: (0, 0, 0)),
                  pl.BlockSpec((1, qw), lambda b, n: (0, 0)),
                  pl.BlockSpec((1, kvw // 2), lambda b, n: (0, 0)),
                  pl.BlockSpec(memory_space=pltpu.SMEM)],
        out_specs=pl.BlockSpec((blk, qw), lambda b, n: (b * nb + n, 0)),
        compiler_params=_cparams(("parallel", "parallel")),
        name="swa_attention",
    )(q, kv, kv, bias, jnp.tile(q_norm.astype(F32), SW_HEADS).reshape(1, qw),
      jnp.tile(k_norm.astype(F32), SW_KV_HEADS).reshape(1, kvw // 2), sinks.astype(F32))


def _route_kernel(x_ref, g_ref, wr_ref, r_ref, wt_ref, tab_ref, cnt_ref, sel_s, gw_s, cnt_s, start_s, run_s,
                  *, tile_rows):
    ne = N_EXPERTS
    p = pl.program_id(0)
    i = pl.program_id(1)
    tm = x_ref.shape[0]
    sub = lax.broadcasted_iota(jnp.int32, (ne, tm), 0).astype(F32)

    def seg_rows(sel):
        n = jnp.sum(sel, axis=1, keepdims=True)
        return jnp.floor((n + (SEG_ALIGN - 1)) * (1.0 / SEG_ALIGN)) * SEG_ALIGN

    def excl_cumsum_experts(v):
        sub8 = lax.broadcasted_iota(jnp.int32, v.shape, 0)
        out = jnp.zeros_like(v)
        for e in range(ne - 1):
            out = out + jnp.where(sub8 > e, v[e:e + 1, :], 0.0)
        return out

    @pl.when(p == 0)
    def _():
        @pl.when(i == 0)
        def _():
            cnt_s[...] = jnp.zeros_like(cnt_s)

        x = x_ref[...]
        ms = jnp.mean(x * x, axis=-1, keepdims=True)
        xn32 = (x * lax.rsqrt(ms + EPS)) * g_ref[...]
        xn_hi = xn32.astype(BF16)
        xn_lo = (xn32 - xn_hi.astype(F32)).astype(BF16)
        p_hi = _dot_nt(wr_ref[...], xn_hi)
        p_lo = _dot_nt(wr_ref[...], xn_lo)
        logits = p_hi[0:ne] + p_hi[ne:2 * ne] + p_lo[0:ne]
        m1 = jnp.max(logits, axis=0, keepdims=True)
        i1 = jnp.min(jnp.where(logits == m1, sub, float(ne)), axis=0, keepdims=True)
        l2 = jnp.where(sub == i1, -jnp.inf, logits)
        m2 = jnp.max(l2, axis=0, keepdims=True)
        i2 = jnp.min(jnp.where(l2 == m2, sub, float(ne)), axis=0, keepdims=True)
        e2 = jnp.exp(m2 - m1)
        w1 = 1.0 / (1.0 + e2)
        w2 = e2 / (1.0 + e2)
        sel = jnp.where((sub == i1) | (sub == i2), 1.0, 0.0)
        sel_s[i] = sel
        gw_s[i] = jnp.where(sub == i1, w1, jnp.where(sub == i2, w2, 0.0))
        cnt_s[...] += seg_rows(sel)

    @pl.when(p == 1)
    def _():
        @pl.when(i == 0)
        def _():
            cnt = cnt_s[...]
            padded = jnp.floor((cnt + (tile_rows - 1)) * (1.0 / tile_rows)) * tile_rows
            start_s[...] = excl_cumsum_experts(padded)
            run_s[...] = jnp.zeros_like(run_s)
            cnt_ref[...] = cnt

        sel = sel_s[i]
        gw = gw_s[i]
        ti = lax.broadcasted_iota(jnp.int32, (tm, tm), 0)
        tj = lax.broadcasted_iota(jnp.int32, (tm, tm), 1)
        tri = jnp.where(ti <= tj, 1.0, 0.0).astype(BF16)
        csum = _dot(sel.astype(BF16), tri)
        seg = jnp.broadcast_to(seg_rows(sel), run_s.shape)
        local0 = excl_cumsum_experts(seg)
        tab_ref[0, 0] = start_s[...] + run_s[...]
        tab_ref[0, 1] = seg
        tab_ref[0, 2] = local0
        run_s[...] += seg
        local_row = local0[:, 0:1] + csum - sel
        ia = jnp.min(jnp.where(sel > 0.0, sub, float(ne)), axis=0, keepdims=True)
        ib = jnp.max(jnp.where(sel > 0.0, sub, -1.0), axis=0, keepdims=True)
        pick_a = sub == ia
        pick_b = sub == ib
        rows = [jnp.sum(jnp.where(pick_a, local_row, 0.0), axis=0, keepdims=True),
                jnp.sum(jnp.where(pick_b, local_row, 0.0), axis=0, keepdims=True),
                jnp.sum(jnp.where(pick_a, gw, 0.0), axis=0, keepdims=True),
                jnp.sum(jnp.where(pick_b, gw, 0.0), axis=0, keepdims=True)]
        r_ref[...] = jnp.concatenate(rows + [jnp.zeros((ne - 4, tm), F32)], axis=0)
        wpad = jnp.concatenate(rows[2:4] + rows[0:2] + [jnp.zeros((LANES - 4, tm), F32)], axis=0)
        wt_ref[...] = wpad.T


def moe_route(x, gain, w_router, tm, tile_rows):
    t, d = x.shape
    ne = w_router.shape[1]
    assert ne == N_EXPERTS
    w_hi = w_router.astype(BF16)
    w_lo = (w_router - w_hi.astype(F32)).astype(BF16)
    wr = jnp.concatenate([w_hi.T, w_lo.T], axis=0)
    tm = min(tm, t)
    nt = t // tm
    return pl.pallas_call(
        functools.partial(_route_kernel, tile_rows=tile_rows),
        out_shape=(jax.ShapeDtypeStruct((ne, t), F32), jax.ShapeDtypeStruct((t, LANES), F32),
                   jax.ShapeDtypeStruct((nt, 3, ne, LANES), F32), jax.ShapeDtypeStruct((ne, LANES), F32)),
        grid=(2, nt),
        in_specs=[pl.BlockSpec((tm, d), lambda p, i: (i * (1 - p) + (nt - 1) * p, 0)),
                  pl.BlockSpec((1, d), lambda p, i: (0, 0)),
                  pl.BlockSpec((2 * ne, d), lambda p, i: (0, 0))],
        out_specs=(pl.BlockSpec((ne, tm), lambda p, i: (0, i * p)),
                   pl.BlockSpec((tm, LANES), lambda p, i: (i * p, 0)),
                   pl.BlockSpec((1, 3, ne, LANES), lambda p, i: (i * p, 0, 0, 0)),
                   pl.BlockSpec((ne, LANES), lambda p, i: (0, 0))),
        scratch_shapes=[pltpu.VMEM((nt, ne, tm), F32), pltpu.VMEM((nt, ne, tm), F32),
                        pltpu.VMEM((ne, LANES), F32), pltpu.VMEM((ne, LANES), F32), pltpu.VMEM((ne, LANES), F32)],
        compiler_params=_cparams(("arbitrary", "arbitrary")),
        name="moe_route",
    )(x, gain.reshape(1, d), wr)


def _segment_copies(tab_ref, i, e, local_ref, slot_ref, sem, to_slots):
    base = (i * N_EXPERTS + e) * 3
    slot0, rows, local0 = tab_ref[base], tab_ref[base + 1], tab_ref[base + 2]
    out = []
    done = 0
    size = MOE_TOKEN_TILE
    while size >= SEG_ALIGN:
        take = rows & size
        loc = local_ref.at[pl.ds(pl.multiple_of(local0 + done, SEG_ALIGN), size)]
        slt = slot_ref.at[pl.ds(pl.multiple_of(slot0 + done, SEG_ALIGN), size)]
        desc = pltpu.make_async_copy(loc, slt, sem) if to_slots else pltpu.make_async_copy(slt, loc, sem)
        out.append((take != 0, desc))
        done = done + take
        size //= 2
    return out


def _dispatch_kernel(tab_ref, zf_ref, x_ref, g_ref, r_ref, xs_ref, rows_s, zero_s, sem, zsem, *, tile_rows):
    i = pl.program_id(0)
    tm = x_ref.shape[0]
    n_local = rows_s.shape[0]

    @pl.when(i == 0)
    def _():
        zero_s[...] = jnp.zeros_like(zero_s)

        def zero_copy(e):
            row0 = pl.multiple_of(zf_ref[e], tile_rows)
            return pltpu.make_async_copy(zero_s, xs_ref.at[pl.ds(row0, tile_rows)], zsem)

        for e in range(zf_ref.shape[0]):
            @pl.when(zf_ref[e] >= 0)
            def _():
                zero_copy(e).start()
        for e in range(zf_ref.shape[0]):
            @pl.when(zf_ref[e] >= 0)
            def _():
                zero_copy(e).wait()

    x = x_ref[...]
    ms = jnp.mean(x * x, axis=-1, keepdims=True)
    xn = ((x * lax.rsqrt(ms + EPS)) * g_ref[...]).astype(BF16)
    row_id = lax.broadcasted_iota(jnp.int32, (n_local, tm), 0).astype(F32)
    onehot = jnp.where((row_id == r_ref[0:1, :]) | (row_id == r_ref[1:2, :]), 1.0, 0.0).astype(BF16)
    rows_s[...] = _dot(onehot, xn)

    copies = [c for e in range(N_EXPERTS) for c in _segment_copies(tab_ref, i, e, rows_s, xs_ref, sem, True)]
    for cond, desc in copies:
        @pl.when(cond)
        def _():
            desc.start()
    for cond, desc in copies:
        @pl.when(cond)
        def _():
            desc.wait()


def moe_dispatch(x, gain, r, tab, zf_rows, n_slots, tm, tile_rows):
    t, d = x.shape
    nt = t // tm
    n_local = TOP_K * tm + N_EXPERTS * SEG_ALIGN
    grid_spec = pltpu.PrefetchScalarGridSpec(
        num_scalar_prefetch=2,
        grid=(nt,),
        in_specs=[pl.BlockSpec((tm, d), lambda i, tb, zf: (i, 0)),
                  pl.BlockSpec((1, d), lambda i, tb, zf: (0, 0)),
                  pl.BlockSpec((N_EXPERTS, tm), lambda i, tb, zf: (0, i))],
        out_specs=pl.BlockSpec(memory_space=pl.ANY),
        scratch_shapes=[pltpu.VMEM((n_local, d), F32), pltpu.VMEM((tile_rows, d), F32),
                        pltpu.SemaphoreType.DMA, pltpu.SemaphoreType.DMA],
    )
    return pl.pallas_call(
        functools.partial(_dispatch_kernel, tile_rows=tile_rows),
        out_shape=jax.ShapeDtypeStruct((n_slots, d), F32),
        grid_spec=grid_spec,
        compiler_params=_cparams(("arbitrary",)),
        name="moe_dispatch",
    )(tab, zf_rows, x, gain.reshape(1, d), r)


def _combine_kernel(tab_ref, h_ref, wt_ref, ys_ref, o_ref, rows_s, sem):
    i = pl.program_id(0)
    tm = h_ref.shape[0]
    n_local = rows_s.shape[0]

    rows_s[...] = jnp.zeros_like(rows_s)
    copies = [c for e in range(N_EXPERTS) for c in _segment_copies(tab_ref, i, e, rows_s, ys_ref, sem, False)]
    for cond, desc in copies:
        @pl.when(cond)
        def _():
            desc.start()
    for cond, desc in copies:
        @pl.when(cond)
        def _():
            desc.wait()

    wt = wt_ref[...]
    y = rows_s[...].astype(BF16)
    col_id = lax.broadcasted_iota(jnp.int32, (tm, n_local), 1).astype(F32)
    pick_a = jnp.where(col_id == wt[:, 2:3], 1.0, 0.0).astype(BF16)
    pick_b = jnp.where(col_id == wt[:, 3:4], 1.0, 0.0).astype(BF16)
    o_ref[...] = h_ref[...] + wt[:, 0:1] * _dot(pick_a, y) + wt[:, 1:2] * _dot(pick_b, y)


def moe_combine(h, wt, tab, ys, tm):
    t, d = h.shape
    nt = t // tm
    n_local = TOP_K * tm + N_EXPERTS * SEG_ALIGN
    grid_spec = pltpu.PrefetchScalarGridSpec(
        num_scalar_prefetch=1,
        grid=(nt,),
        in_specs=[pl.BlockSpec((tm, d), lambda i, tb: (i, 0)),
                  pl.BlockSpec((tm, LANES), lambda i, tb: (i, 0)),
                  pl.BlockSpec(memory_space=pl.ANY)],
        out_specs=pl.BlockSpec((tm, d), lambda i, tb: (i, 0)),
        scratch_shapes=[pltpu.VMEM((n_local, d), F32), pltpu.SemaphoreType.DMA],
    )
    return pl.pallas_call(
        _combine_kernel,
        out_shape=jax.ShapeDtypeStruct((t, d), F32),
        grid_spec=grid_spec,
        compiler_params=_cparams(("arbitrary",)),
        name="moe_combine",
    )(tab, h, wt, ys)


MOE_TILE_ROWS = 512
MOE_TOKEN_TILE = 512
SEG_ALIGN = 8
TOP_K = 2


def moe_layer(h, gain, w_router, wg, wu, wd):
    t, d = h.shape
    ne = w_router.shape[1]
    tr, tm = MOE_TILE_ROWS, MOE_TOKEN_TILE
    nt = t // tm
    n_tiles = -(-(TOP_K * t + nt * ne * (SEG_ALIGN - 1) + ne * (tr - 1)) // tr)
    n_slots = n_tiles * tr

    r, wt, tab, cnt = moe_route(h, gain, w_router, tm, tr)
    tab = jnp.transpose(tab[:, :, :, 0], (0, 2, 1)).astype(jnp.int32).reshape(-1)

    counts = cnt[:, 0].astype(jnp.int32)
    padded = ((counts + (tr - 1)) // tr) * tr
    ends = jnp.cumsum(padded)
    n_valid = (ends[-1] // tr).astype(jnp.int32)
    tile_row0 = jnp.arange(n_tiles, dtype=jnp.int32) * tr
    tile_expert = jnp.sum((tile_row0[:, None] >= ends[None, :]).astype(jnp.int32), axis=1)
    tile_expert = jnp.minimum(tile_expert, ne - 1)
    tile_expert = jnp.where(jnp.arange(n_tiles) < n_valid, tile_expert, tile_expert[jnp.maximum(n_valid - 1, 0)])
    prev_expert = jnp.concatenate([jnp.full((1,), -1, jnp.int32), tile_expert[:-1]])
    tile_first = (tile_expert != prev_expert).astype(jnp.int32)
    tail = jnp.arange(TOP_K * t // tr, n_tiles, dtype=jnp.int32)
    zf_rows = jnp.concatenate([jnp.where(padded > 0, ends - tr, -1),
                               jnp.where(tail >= n_valid, tail * tr, -1)]).astype(jnp.int32)

    xs = moe_dispatch(h, gain, r, tab, zf_rows, n_slots, tm, tr)
    ys = expert_swiglu(xs, gain, tile_expert, tile_first, n_valid.reshape(1), wg, wu, wd, tr, 512, F32, False,
                       "moe_experts")
    return moe_combine(h, wt, tab, ys, tm)


def kernel(x, a_norm, a_w_in, a_conv, a_log_decay, a_dt_bias, a_out_norm, a_w_out, kv_norm, kv_w, k_norm,
           b_norm, b_w_q, q_norm, b_sinks, b_w_o, rel_bias, ffn_norm, dense_w_gate, dense_w_up, dense_w_down,
           moe_router, moe_w_gate, moe_w_up, moe_w_down):
    batch, seq, d = x.shape
    t = batch * seq
    nh, hd = LA_HEADS, LA_D
    main_w = 4 * nh * hd
    h0 = x.reshape(t, d)

    w_in = a_w_in[0]
    w_main = w_in[:, 0:main_w].astype(BF16)
    w_gate = jnp.zeros((d, LANES), BF16).at[:, 0:2 * nh].set(w_in[:, main_w:main_w + 2 * nh].astype(BF16))
    proj, gates = norm_matmul(h0, [(a_norm[0], w_main, BF16), (a_norm[0], w_gate, F32)], 512, "gdn_in_proj")
    o = gdn_core(proj, gates, a_conv[0], a_log_decay[0], a_dt_bias[0], a_out_norm[0], batch, seq)
    h1 = matmul_residual(o, a_w_out[0].astype(BF16), h0, 1024, 1024, "gdn_out_proj")

    h2 = ffn_dense(h1, ffn_norm[0], dense_w_gate[0], dense_w_up[0], dense_w_down[0], 512, 512)

    kv, q = norm_matmul(h2, [(kv_norm, kv_w.astype(BF16), BF16), (b_norm[0], b_w_q[0].astype(BF16), BF16)],
                        1024, "qkv_proj")
    bias = bias_table(rel_bias)
    attn = swa_attention(q, kv, bias, q_norm[0], k_norm, b_sinks[0], batch, seq)
    h3 = matmul_residual(attn, b_w_o[0].astype(BF16), h2, 1024, 1024, "attn_out_proj")

    h4 = moe_layer(h3, ffn_norm[1], moe_router[0], moe_w_gate[0], moe_w_up[0], moe_w_down[0])
    return h4.reshape(batch, seq, d)
```

```python
import functools

import numpy as np
import jax
import jax.numpy as jnp
from jax import lax
from jax.experimental import pallas as pl
from jax.experimental.pallas import tpu as pltpu

F32 = jnp.float32
BF16 = jnp.bfloat16

EPS = 1e-6
NEG_INF = -1e30

LA_HEADS = 8
LA_D = 128
CONV_W = 4
CHUNK = 64
SW_HEADS = 16
SW_KV_HEADS = 4
SW_GROUP = SW_HEADS // SW_KV_HEADS
SW_HD = 64
WINDOW = 128
N_BUCKETS = 32
MAX_DIST = 128
N_EXPERTS = 8

LANES = 128
GDN_BLOCK = 2 * CHUNK
HALO = 8

VMEM_LIMIT = 56 * 1024 * 1024


def _cparams(sem):
    return pltpu.CompilerParams(dimension_semantics=sem, vmem_limit_bytes=VMEM_LIMIT)


def _silu(x):
    return x * (1.0 / (1.0 + jnp.exp(-x)))


def _dot(a, b):
    return jnp.dot(a, b, preferred_element_type=F32)


def _dot_nt(a, b):
    return lax.dot_general(a, b, (((1,), (1,)), ((), ())), preferred_element_type=F32)


def _norm_matmul_kernel(*refs, n_groups):
    x_ref = refs[0]
    g_refs = refs[1:1 + n_groups]
    w_refs = refs[1 + n_groups:1 + 2 * n_groups]
    o_refs = refs[1 + 2 * n_groups:1 + 3 * n_groups]
    x = x_ref[...]
    xr = x * lax.rsqrt(jnp.mean(x * x, axis=-1, keepdims=True) + EPS)
    for g_ref, w_ref, o_ref in zip(g_refs, w_refs, o_refs):
        o_ref[...] = _dot((xr * g_ref[...]).astype(BF16), w_ref[...]).astype(o_ref.dtype)


def norm_matmul(x, groups, tm, name):
    t, d = x.shape
    tm = min(tm, t)
    gains = [g.reshape(1, d).astype(F32) for g, _, _ in groups]
    ws = [w for _, w, _ in groups]
    return pl.pallas_call(
        functools.partial(_norm_matmul_kernel, n_groups=len(groups)),
        out_shape=[jax.ShapeDtypeStruct((t, w.shape[1]), dt) for _, w, dt in groups],
        grid=(t // tm,),
        in_specs=([pl.BlockSpec((tm, d), lambda i: (i, 0))]
                  + [pl.BlockSpec((1, d), lambda i: (0, 0)) for _ in groups]
                  + [pl.BlockSpec(w.shape, lambda i: (0, 0)) for w in ws]),
        out_specs=[pl.BlockSpec((tm, w.shape[1]), lambda i: (i, 0)) for w in ws],
        compiler_params=_cparams(("parallel",)),
        name=name,
    )(x, *gains, *ws)


def _matmul_res_kernel(a_ref, w_ref, r_ref, o_ref):
    o_ref[...] = r_ref[...] + _dot(a_ref[...], w_ref[...])


def matmul_residual(a, w, res, tm, tn, name):
    t, k = a.shape
    n = w.shape[1]
    tm, tn = min(tm, t), min(tn, n)
    return pl.pallas_call(
        _matmul_res_kernel,
        out_shape=jax.ShapeDtypeStruct((t, n), F32),
        grid=(t // tm, n // tn),
        in_specs=[pl.BlockSpec((tm, k), lambda i, j: (i, 0)),
                  pl.BlockSpec((k, tn), lambda i, j: (0, j)),
                  pl.BlockSpec((tm, tn), lambda i, j: (i, j))],
        out_specs=pl.BlockSpec((tm, tn), lambda i, j: (i, j)),
        compiler_params=_cparams(("parallel", "parallel")),
        name=name,
    )(a, w, res)


def _gdn_kernel(proj_ref, gates_ref, convw_ref, hp_ref, onorm_ref, o_ref,
                xs_ref, state_ref, q_s, k_s, v_s, z_s, gc_s, gct_s, beta_s):
    n = pl.program_id(1)

    @pl.when(n == 0)
    def _():
        xs_ref[0:HALO, :] = jnp.zeros((HALO, xs_ref.shape[1]), F32)
        for ref in (q_s, k_s, v_s, z_s, gc_s, gct_s, beta_s):
            ref[1] = jnp.zeros(ref.shape[1:], ref.dtype)

    @pl.when(n <= 1)
    def _():
        state_ref[...] = jnp.zeros_like(state_ref)

    args = (proj_ref, gates_ref, convw_ref, hp_ref, onorm_ref, o_ref, xs_ref, state_ref,
            q_s, k_s, v_s, z_s, gc_s, gct_s, beta_s)

    @pl.when(lax.rem(n, 2) == 0)
    def _():
        _gdn_step(*args, slot_w=0, slot_r=1)

    @pl.when(lax.rem(n, 2) == 1)
    def _():
        _gdn_step(*args, slot_w=1, slot_r=0)


def _gdn_step(proj_ref, gates_ref, convw_ref, hp_ref, onorm_ref, o_ref, xs_ref, state_ref,
              q_s, k_s, v_s, z_s, gc_s, gct_s, beta_s, *, slot_w, slot_r):
    nh, d, c = LA_HEADS, LA_D, CHUNK
    blk = GDN_BLOCK
    qkv_w = 3 * nh * d

    gc = gc_s[slot_r]
    gc_t = gct_s[slot_r]
    beta = beta_s[slot_r]

    xs_ref[HALO:HALO + blk, :] = proj_ref[:, 0:qkv_w].astype(F32)

    def front_gates():
        _gdn_front_gates(gates_ref, hp_ref, gc_s, gct_s, beta_s, slot_w)

    ci = lax.broadcasted_iota(jnp.int32, (c, c), 0)
    cj = lax.broadcasted_iota(jnp.int32, (c, c), 1)
    lower_incl = ci >= cj
    strict = ci > cj
    eye_c = jnp.where(ci == cj, 1.0, 0.0).astype(F32)
    di = lax.broadcasted_iota(jnp.int32, (d, d), 0)
    dj = lax.broadcasted_iota(jnp.int32, (d, d), 1)
    eye_d = jnp.where(di == dj, 1.0, 0.0).astype(BF16)

    onorm = onorm_ref[...]

    sub_halo = lax.broadcasted_iota(jnp.int32, (HALO, d), 0)

    def conv_silu(col0):
        x_cur = xs_ref[HALO:HALO + blk, col0:col0 + d]
        x_tail = xs_ref[0:HALO, col0:col0 + d]
        acc = convw_ref[CONV_W - 1:CONV_W, col0:col0 + d] * x_cur
        for j in range(CONV_W - 1):
            s = CONV_W - 1 - j
            rolled = pltpu.roll(x_cur, s, 0)
            head = jnp.where(sub_halo < s, pltpu.roll(x_tail, s, 0), rolled[0:HALO])
            shifted = jnp.concatenate([head, rolled[HALO:blk]], axis=0)
            acc = acc + convw_ref[j:j + 1, col0:col0 + d] * shifted
        return _silu(acc)

    def front_head(h):
        qf = conv_silu(h * d)
        kf = conv_silu(nh * d + h * d)
        vf = conv_silu(2 * nh * d + h * d)
        q_s[slot_w, :, h * d:(h + 1) * d] = qf * (lax.rsqrt(jnp.sum(qf * qf, axis=-1, keepdims=True) + EPS)
                                                  * (d ** -0.5))
        k_s[slot_w, :, h * d:(h + 1) * d] = kf * lax.rsqrt(jnp.sum(kf * kf, axis=-1, keepdims=True) + EPS)
        v_s[slot_w, :, h * d:(h + 1) * d] = vf
        z_s[slot_w, :, h * d:(h + 1) * d] = proj_ref[:, qkv_w + h * d:qkv_w + (h + 1) * d]

    front_tasks = [front_gates] + [functools.partial(front_head, h) for h in range(nh)]

    def run_front_task():
        if front_tasks:
            front_tasks.pop(0)()

    n_ck = blk // c
    chains = [(h, ck) for h in range(nh) for ck in range(n_ck)]

    st = {}
    for (h, ck) in chains:
        r = ck * c
        q = q_s[slot_r, r:r + c, h * d:(h + 1) * d]
        k = k_s[slot_r, r:r + c, h * d:(h + 1) * d]
        v = v_s[slot_r, r:r + c, h * d:(h + 1) * d]
        g_col = gc[r:r + c, nh + h:nh + h + 1]
        g_row = gc_t[nh + h:nh + h + 1, r:r + c]
        g_last = gc[r + c - 1:r + c, nh + h:nh + h + 1]
        b_col = beta[r:r + c, h:h + 1]
        decay = jnp.where(lower_incl, jnp.exp2(jnp.where(lower_incl, g_col - g_row, 0.0)), 0.0)
        k_beta = k * b_col
        e_col = jnp.exp2(g_col)
        lhs = jnp.concatenate([k_beta.astype(BF16), q.astype(BF16), eye_d], axis=0)
        kk = _dot_nt(lhs, k.astype(BF16))
        a_mat = jnp.where(strict, kk[0:c] * decay, 0.0)
        st[(h, ck)] = dict(
            a=a_mat, attn=(kk[c:2 * c] * decay).astype(BF16),
            k_tail_t=(kk[2 * c:2 * c + d] * jnp.exp2(g_last - g_row)).astype(BF16),
            rhs=jnp.concatenate([(v * b_col).astype(BF16), (k_beta * e_col).astype(BF16)], axis=1),
            qe=(q * e_col).astype(BF16), e_last=jnp.exp2(g_last))
    run_front_task()

    for key in chains:
        x_b = (-st[key]["a"]).astype(BF16)
        st[key]["y"] = _dot(x_b, x_b)
        st[key]["p"] = eye_c - st[key]["a"]
    run_front_task()
    n_levels = int(np.log2(c))
    for lvl in range(1, n_levels):
        for key in chains:
            y_b = st[key]["y"].astype(BF16)
            p = st[key]["p"]
            if lvl + 1 < n_levels:
                zz = _dot(jnp.concatenate([y_b, p.astype(BF16)], axis=0), y_b)
                st[key]["y"] = zz[0:c]
                st[key]["p"] = p + zz[c:2 * c]
            else:
                st[key]["p"] = p + _dot(p.astype(BF16), y_b)
        run_front_task()
    for key in chains:
        st[key]["uw"] = _dot(st[key]["p"].astype(BF16), st[key]["rhs"])
    run_front_task()

    for ck in range(n_ck):
        r = ck * c
        s_old = [state_ref[h] for h in range(nh)]
        ws_qs = []
        for h in range(nh):
            cur = st[(h, ck)]
            lhs = jnp.concatenate([cur["uw"][:, d:2 * d].astype(BF16), cur["qe"]], axis=0)
            ws_qs.append(_dot(lhs, s_old[h].astype(BF16)))
        run_front_task()
        for h in range(nh):
            cur = st[(h, ck)]
            v_new = cur["uw"][:, 0:d] - ws_qs[h][0:c]
            av_kv = _dot(jnp.concatenate([cur["attn"], cur["k_tail_t"]], axis=0), v_new.astype(BF16))
            state_ref[h] = s_old[h] * cur["e_last"] + av_kv[c:c + d]
            o = ws_qs[h][c:2 * c] + av_kv[0:c]
            o = (o * lax.rsqrt(jnp.mean(o * o, axis=-1, keepdims=True) + EPS)) * onorm
            z = z_s[slot_r, r:r + c, h * d:(h + 1) * d].astype(F32)
            o_ref[r:r + c, h * d:(h + 1) * d] = (o * _silu(z)).astype(o_ref.dtype)
    while front_tasks:
        run_front_task()

    xs_ref[0:HALO, :] = xs_ref[blk:blk + HALO, :]


def _gdn_front_gates(gates_ref, hp_ref, gc_s, gct_s, beta_s, slot_w):
    blk, c = GDN_BLOCK, CHUNK
    gates = gates_ref[...]
    a_log = hp_ref[0:1, :]
    dt_bias = hp_ref[1:2, :]
    beta = 1.0 / (1.0 + jnp.exp(-gates))
    sp_in = gates + dt_bias
    softplus = jnp.maximum(sp_in, 0.0) + jnp.log(1.0 + jnp.exp(-jnp.abs(sp_in)))
    g = (-jnp.exp(a_log) * softplus) * float(np.log2(np.e))

    row = lax.broadcasted_iota(jnp.int32, (blk, blk), 0)
    col = lax.broadcasted_iota(jnp.int32, (blk, blk), 1)
    tri = jnp.where((row >= col) & ((row // c) == (col // c)), 1.0, 0.0).astype(BF16)
    g_hi = g.astype(BF16)
    g_r1 = g - g_hi.astype(F32)
    g_mid = g_r1.astype(BF16)
    g_lo = (g_r1 - g_mid.astype(F32)).astype(BF16)
    gc = _dot(tri, g_hi) + _dot(tri, g_mid) + _dot(tri, g_lo)
    gc_s[slot_w] = gc
    gct_s[slot_w] = gc.T
    beta_s[slot_w] = beta


def gdn_core(proj, gates, conv_w, a_log, dt_bias, out_norm, batch, seq):
    t = proj.shape[0]
    nh, d = LA_HEADS, LA_D
    blk = GDN_BLOCK
    nblk = seq // blk
    hp = jnp.zeros((8, LANES), F32)
    hp = hp.at[0, nh:2 * nh].set(a_log.astype(F32)).at[1, nh:2 * nh].set(dt_bias.astype(F32))

    def in_map(b, n):
        return (b * nblk + jnp.minimum(n, nblk - 1), 0)

    return pl.pallas_call(
        _gdn_kernel,
        out_shape=jax.ShapeDtypeStruct((t, nh * d), BF16),
        grid=(batch, nblk + 1),
        in_specs=[pl.BlockSpec((blk, 4 * nh * d), in_map),
                  pl.BlockSpec((blk, LANES), in_map),
                  pl.BlockSpec((CONV_W, 3 * nh * d), lambda b, n: (0, 0)),
                  pl.BlockSpec((8, LANES), lambda b, n: (0, 0)),
                  pl.BlockSpec((1, d), lambda b, n: (0, 0))],
        out_specs=pl.BlockSpec((blk, nh * d), lambda b, n: (b * nblk + jnp.maximum(n - 1, 0), 0)),
        scratch_shapes=[pltpu.VMEM((HALO + blk, 3 * nh * d), F32),
                        pltpu.VMEM((nh, d, d), F32),
                        pltpu.VMEM((2, blk, nh * d), F32), pltpu.VMEM((2, blk, nh * d), F32),
                        pltpu.VMEM((2, blk, nh * d), F32), pltpu.VMEM((2, blk, nh * d), BF16),
                        pltpu.VMEM((2, blk, LANES), F32), pltpu.VMEM((2, LANES, blk), F32),
                        pltpu.VMEM((2, blk, LANES), F32)],
        compiler_params=_cparams(("arbitrary", "arbitrary")),
        name="gdn_core",
    )(proj, gates, conv_w.astype(F32), hp, out_norm.reshape(1, d).astype(F32))


def _swiglu_kernel(te_ref, first_ref, nv_ref, x_ref, g_ref, wg_hbm, wu_hbm, wd_hbm, o_ref,
                   wg_c, wu_c, wd_c, stage_in, stage_out, sems, *, pre_norm, tf):
    i = pl.program_id(0)
    nf = wg_c.shape[0]
    e = te_ref[i]
    valid = i < nv_ref[0]

    def chunk_copies(j, slot):
        cols = pl.ds(j * tf, tf)
        return (pltpu.make_async_copy(wg_hbm.at[e, :, cols], stage_in.at[slot, 0], sems.at[slot, 0]),
                pltpu.make_async_copy(wu_hbm.at[e, :, cols], stage_in.at[slot, 1], sems.at[slot, 1]),
                pltpu.make_async_copy(wd_hbm.at[e, cols, :], stage_out.at[slot], sems.at[slot, 2]))

    def prepare_rows():
        x = x_ref[...].astype(F32)
        if pre_norm:
            ms = jnp.mean(x * x, axis=-1, keepdims=True)
            x = (x * lax.rsqrt(ms + EPS)) * g_ref[...]
        return x.astype(BF16)

    def chunk(xb, j):
        hid = _silu(_dot(xb, wg_c[j])) * _dot(xb, wu_c[j])
        return _dot(hid.astype(BF16), wd_c[j])

    def finish(acc):
        if pre_norm:
            o_ref[...] = (x_ref[...] + acc).astype(o_ref.dtype)
        else:
            o_ref[...] = acc.astype(o_ref.dtype)

    @pl.when(valid & (first_ref[i] == 1))
    def _():
        for c in chunk_copies(0, 0):
            c.start()
        xb = prepare_rows()
        acc = None
        for j in range(nf):
            slot = j % 2
            if j + 1 < nf:
                for c in chunk_copies(j + 1, 1 - slot):
                    c.start()
            for c in chunk_copies(j, slot):
                c.wait()
            wg_c[j] = stage_in[slot, 0].astype(BF16)
            wu_c[j] = stage_in[slot, 1].astype(BF16)
            wd_c[j] = stage_out[slot].astype(BF16)
            y = chunk(xb, j)
            acc = y if acc is None else acc + y
        finish(acc)

    @pl.when(valid & (first_ref[i] != 1))
    def _():
        xb = prepare_rows()
        acc = None
        for j in range(nf):
            y = chunk(xb, j)
            acc = y if acc is None else acc + y
        finish(acc)

    @pl.when(jnp.logical_not(valid))
    def _():
        o_ref[...] = jnp.zeros_like(o_ref)


def expert_swiglu(x, gain, tile_expert, tile_first, n_valid, wg, wu, wd, tile_rows, tf, out_dtype, pre_norm, name):
    n_rows, d = x.shape
    ne, _, f = wg.shape
    n_tiles = n_rows // tile_rows
    nf = f // tf
    grid_spec = pltpu.PrefetchScalarGridSpec(
        num_scalar_prefetch=3,
        grid=(n_tiles,),
        in_specs=[pl.BlockSpec((tile_rows, d), lambda i, te, fi, nv: (jnp.minimum(i, nv[0] - 1), 0)),
                  pl.BlockSpec((1, d), lambda i, te, fi, nv: (0, 0)),
                  pl.BlockSpec(memory_space=pl.ANY),
                  pl.BlockSpec(memory_space=pl.ANY),
                  pl.BlockSpec(memory_space=pl.ANY)],
        out_specs=pl.BlockSpec((tile_rows, d), lambda i, te, fi, nv: (i, 0)),
        scratch_shapes=[pltpu.VMEM((nf, d, tf), BF16), pltpu.VMEM((nf, d, tf), BF16), pltpu.VMEM((nf, tf, d), BF16),
                        pltpu.VMEM((2, 2, d, tf), F32), pltpu.VMEM((2, tf, d), F32),
                        pltpu.SemaphoreType.DMA((2, 3))],
    )
    return pl.pallas_call(
        functools.partial(_swiglu_kernel, pre_norm=pre_norm, tf=tf),
        out_shape=jax.ShapeDtypeStruct((n_rows, d), out_dtype),
        grid_spec=grid_spec,
        compiler_params=_cparams(("arbitrary",)),
        name=name,
    )(tile_expert, tile_first, n_valid, x, gain.reshape(1, d).astype(F32), wg, wu, wd)


def ffn_dense(x, gain, wg, wu, wd, tm, tf):
    t = x.shape[0]
    n_tiles = t // tm
    tile_first = jnp.zeros((n_tiles,), jnp.int32).at[0].set(1)
    return expert_swiglu(x, gain, jnp.zeros((n_tiles,), jnp.int32), tile_first, jnp.full((1,), n_tiles, jnp.int32),
                         wg[None], wu[None], wd[None], tm, tf, F32, True, "ffn_dense")


def _t5_bucket_np(dist):
    max_exact = N_BUCKETS // 2
    n = np.maximum(dist, 0)
    safe = np.maximum(n, 1).astype(np.float32)
    large = max_exact + (np.log(safe / max_exact) / np.log(MAX_DIST / max_exact)
                         * (N_BUCKETS - max_exact)).astype(np.int32)
    large = np.minimum(large, N_BUCKETS - 1)
    return np.where(n < max_exact, n, large).astype(np.int32)


def _bias_kernel(bucket_ref, rb_ref, o_ref):
    bucket = bucket_ref[...]
    for h in range(SW_HEADS):
        acc = jnp.zeros(bucket.shape, F32)
        for b in range(N_BUCKETS):
            acc = jnp.where(bucket == b, rb_ref[b, h], acc)
        o_ref[h] = acc


def bias_table(rel_bias):
    qi = np.arange(WINDOW)[:, None] + WINDOW
    kj = np.arange(2 * WINDOW)[None, :]
    bucket = jnp.asarray(_t5_bucket_np(qi - kj))
    return pl.pallas_call(
        _bias_kernel,
        out_shape=jax.ShapeDtypeStruct((SW_HEADS, WINDOW, 2 * WINDOW), F32),
        in_specs=[pl.BlockSpec(memory_space=pltpu.VMEM), pl.BlockSpec(memory_space=pltpu.SMEM)],
        out_specs=pl.BlockSpec(memory_space=pltpu.VMEM),
        name="t5_bias_table",
    )(bucket, rel_bias.astype(F32))


def _swa_kernel(q_ref, kvp_ref, kvc_ref, bias_ref, qn_ref, kn_ref, sink_ref, o_ref):
    n = pl.program_id(1)
    blk, hd = WINDOW, SW_HD
    kv_w = SW_KV_HEADS * hd
    qi = lax.broadcasted_iota(jnp.int32, (blk, 2 * blk), 0) + blk
    kj = lax.broadcasted_iota(jnp.int32, (blk, 2 * blk), 1)
    dist = qi - kj
    first_key = jnp.where(n > 0, 0, blk)
    mask = (dist >= 0) & (dist < WINDOW) & (kj >= first_key)
    gw = 2 * LANES
    gi = lax.broadcasted_iota(jnp.int32, (gw, gw), 0)
    gj = lax.broadcasted_iota(jnp.int32, (gw, gw), 1)
    group_ones = jnp.where((gi // hd) == (gj // hd), 1.0, 0.0).astype(BF16)
    lane = lax.broadcasted_iota(jnp.int32, (1, LANES), 1)
    low_half = lane < hd

    def head_norm(x, gain):
        cols = []
        for c0 in range(0, x.shape[1], gw):
            xc = x[:, c0:c0 + gw]
            ss = _dot((xc * xc).astype(BF16), group_ones)
            cols.append(xc * lax.rsqrt(ss * (1.0 / hd) + EPS))
        return jnp.concatenate(cols, axis=1) * gain

    def dup_half(x, half):
        swapped = pltpu.roll(x, hd, 1)
        return jnp.where(low_half == (half == 0), x, swapped)

    qn = head_norm(q_ref[...].astype(F32), qn_ref[...]) * (hd ** -0.5)
    half_sel = [jnp.where(low_half, 1.0, 0.0), jnp.where(low_half, 0.0, 1.0)]
    k_all = jnp.concatenate([kvp_ref[:, 0:kv_w], kvc_ref[:, 0:kv_w]], axis=0).astype(F32)
    kn = head_norm(k_all, kn_ref[...])
    v_all = jnp.concatenate([kvp_ref[:, kv_w:2 * kv_w], kvc_ref[:, kv_w:2 * kv_w]], axis=0).astype(F32)
    ks, vs = [], []
    for g in range(SW_KV_HEADS):
        c0 = (g // 2) * LANES
        ks.append(dup_half(kn[:, c0:c0 + LANES], g % 2).astype(BF16))
        vs.append(dup_half(v_all[:, c0:c0 + LANES], g % 2).astype(BF16))

    scores = []
    for hq in range(SW_HEADS):
        c0 = (hq // 2) * LANES
        q_h = (qn[:, c0:c0 + LANES] * half_sel[hq % 2]).astype(BF16)
        scores.append(_dot_nt(q_h, ks[hq // SW_GROUP]))
    probs = []
    for hq in range(SW_HEADS):
        s = jnp.where(mask, scores[hq] + bias_ref[hq], NEG_INF)
        sink = sink_ref[hq]
        mx = jnp.maximum(jnp.max(s, axis=-1, keepdims=True), sink)
        p = jnp.exp(s - mx)
        denom = jnp.sum(p, axis=-1, keepdims=True) + jnp.exp(sink - mx)
        probs.append((p / denom).astype(BF16))
    outs = [_dot(probs[hq], vs[hq // SW_GROUP]) for hq in range(SW_HEADS)]
    for c in range(SW_HEADS // 2):
        o_ref[:, c * LANES:(c + 1) * LANES] = jnp.where(low_half, outs[2 * c], outs[2 * c + 1]).astype(o_ref.dtype)


def swa_attention(q, kv, bias, q_norm, k_norm, sinks, batch, seq):
    t = q.shape[0]
    blk = WINDOW
    nb = seq // blk
    qw = SW_HEADS * SW_HD
    kvw = 2 * SW_KV_HEADS * SW_HD
    return pl.pallas_call(
        _swa_kernel,
        out_shape=jax.ShapeDtypeStruct((t, qw), BF16),
        grid=(batch, nb),
        in_specs=[pl.BlockSpec((blk, qw), lambda b, n: (b * nb + n, 0)),
                  pl.BlockSpec((blk, kvw), lambda b, n: (b * nb + jnp.maximum(n - 1, 0), 0)),
                  pl.BlockSpec((blk, kvw), lambda b, n: (b * nb + n, 0)),
                  pl.BlockSpec((SW_HEADS, blk, 2 * blk), lambda b, n: (0, 0, 0)),
                  pl.BlockSpec((1, qw), lambda b, n: (0, 0)),
                  pl.BlockSpec((1, kvw // 2), lambda b, n: (0, 0)),
                  pl.BlockSpec(memory_space=pltpu.SMEM)],
        out_specs=pl.BlockSpec((blk, qw), lambda b, n: (b * nb + n, 0)),
        compiler_params=_cparams(("parallel", "parallel")),
        name="swa_attention",
    )(q, kv, kv, bias, jnp.tile(q_norm.astype(F32), SW_HEADS).reshape(1, qw),
      jnp.tile(k_norm.astype(F32), SW_KV_HEADS).reshape(1, kvw // 2), sinks.astype(F32))


def _route_kernel(x_ref, g_ref, wr_ref, r_ref, wt_ref, tab_ref, cnt_ref, sel_s, gw_s, cnt_s, start_s, run_s,
                  *, tile_rows):
    ne = N_EXPERTS
    p = pl.program_id(0)
    i = pl.program_id(1)
    tm = x_ref.shape[0]
    sub = lax.broadcasted_iota(jnp.int32, (ne, tm), 0).astype(F32)

    def seg_rows(sel):
        n = jnp.sum(sel, axis=1, keepdims=True)
        return jnp.floor((n + (SEG_ALIGN - 1)) * (1.0 / SEG_ALIGN)) * SEG_ALIGN

    def excl_cumsum_experts(v):
        sub8 = lax.broadcasted_iota(jnp.int32, v.shape, 0)
        out = jnp.zeros_like(v)
        for e in range(ne - 1):
            out = out + jnp.where(sub8 > e, v[e:e + 1, :], 0.0)
        return out

    @pl.when(p == 0)
    def _():
        @pl.when(i == 0)
        def _():
            cnt_s[...] = jnp.zeros_like(cnt_s)

        x = x_ref[...]
        ms = jnp.mean(x * x, axis=-1, keepdims=True)
        xn32 = (x * lax.rsqrt(ms + EPS)) * g_ref[...]
        xn_hi = xn32.astype(BF16)
        xn_lo = (xn32 - xn_hi.astype(F32)).astype(BF16)
        p_hi = _dot_nt(wr_ref[...], xn_hi)
        p_lo = _dot_nt(wr_ref[...], xn_lo)
        logits = p_hi[0:ne] + p_hi[ne:2 * ne] + p_lo[0:ne]
        m1 = jnp.max(logits, axis=0, keepdims=True)
        i1 = jnp.min(jnp.where(logits == m1, sub, float(ne)), axis=0, keepdims=True)
        l2 = jnp.where(sub == i1, -jnp.inf, logits)
        m2 = jnp.max(l2, axis=0, keepdims=True)
        i2 = jnp.min(jnp.where(l2 == m2, sub, float(ne)), axis=0, keepdims=True)
        e2 = jnp.exp(m2 - m1)
        w1 = 1.0 / (1.0 + e2)
        w2 = e2 / (1.0 + e2)
        sel = jnp.where((sub == i1) | (sub == i2), 1.0, 0.0)
        sel_s[i] = sel
        gw_s[i] = jnp.where(sub == i1, w1, jnp.where(sub == i2, w2, 0.0))
        cnt_s[...] += seg_rows(sel)

    @pl.when(p == 1)
    def _():
        @pl.when(i == 0)
        def _():
            cnt = cnt_s[...]
            padded = jnp.floor((cnt + (tile_rows - 1)) * (1.0 / tile_rows)) * tile_rows
            start_s[...] = excl_cumsum_experts(padded)
            run_s[...] = jnp.zeros_like(run_s)
            cnt_ref[...] = cnt

        sel = sel_s[i]
        gw = gw_s[i]
        ti = lax.broadcasted_iota(jnp.int32, (tm, tm), 0)
        tj = lax.broadcasted_iota(jnp.int32, (tm, tm), 1)
        tri = jnp.where(ti <= tj, 1.0, 0.0).astype(BF16)
        csum = _dot(sel.astype(BF16), tri)
        seg = jnp.broadcast_to(seg_rows(sel), run_s.shape)
        local0 = excl_cumsum_experts(seg)
        tab_ref[0, 0] = start_s[...] + run_s[...]
        tab_ref[0, 1] = seg
        tab_ref[0, 2] = local0
        run_s[...] += seg
        local_row = local0[:, 0:1] + csum - sel
        ia = jnp.min(jnp.where(sel > 0.0, sub, float(ne)), axis=0, keepdims=True)
        ib = jnp.max(jnp.where(sel > 0.0, sub, -1.0), axis=0, keepdims=True)
        pick_a = sub == ia
        pick_b = sub == ib
        rows = [jnp.sum(jnp.where(pick_a, local_row, 0.0), axis=0, keepdims=True),
                jnp.sum(jnp.where(pick_b, local_row, 0.0), axis=0, keepdims=True),
                jnp.sum(jnp.where(pick_a, gw, 0.0), axis=0, keepdims=True),
                jnp.sum(jnp.where(pick_b, gw, 0.0), axis=0, keepdims=True)]
        r_ref[...] = jnp.concatenate(rows + [jnp.zeros((ne - 4, tm), F32)], axis=0)
        wpad = jnp.concatenate(rows[2:4] + rows[0:2] + [jnp.zeros((LANES - 4, tm), F32)], axis=0)
        wt_ref[...] = wpad.T


def moe_route(x, gain, w_router, tm, tile_rows):
    t, d = x.shape
    ne = w_router.shape[1]
    assert ne == N_EXPERTS
    w_hi = w_router.astype(BF16)
    w_lo = (w_router - w_hi.astype(F32)).astype(BF16)
    wr = jnp.concatenate([w_hi.T, w_lo.T], axis=0)
    tm = min(tm, t)
    nt = t // tm
    return pl.pallas_call(
        functools.partial(_route_kernel, tile_rows=tile_rows),
        out_shape=(jax.ShapeDtypeStruct((ne, t), F32), jax.ShapeDtypeStruct((t, LANES), F32),
                   jax.ShapeDtypeStruct((nt, 3, ne, LANES), F32), jax.ShapeDtypeStruct((ne, LANES), F32)),
        grid=(2, nt),
        in_specs=[pl.BlockSpec((tm, d), lambda p, i: (i * (1 - p) + (nt - 1) * p, 0)),
                  pl.BlockSpec((1, d), lambda p, i: (0, 0)),
                  pl.BlockSpec((2 * ne, d), lambda p, i: (0, 0))],
        out_specs=(pl.BlockSpec((ne, tm), lambda p, i: (0, i * p)),
                   pl.BlockSpec((tm, LANES), lambda p, i: (i * p, 0)),
                   pl.BlockSpec((1, 3, ne, LANES), lambda p, i: (i * p, 0, 0, 0)),
                   pl.BlockSpec((ne, LANES), lambda p, i: (0, 0))),
        scratch_shapes=[pltpu.VMEM((nt, ne, tm), F32), pltpu.VMEM((nt, ne, tm), F32),
                        pltpu.VMEM((ne, LANES), F32), pltpu.VMEM((ne, LANES), F32), pltpu.VMEM((ne, LANES), F32)],
        compiler_params=_cparams(("arbitrary", "arbitrary")),
        name="moe_route",
    )(x, gain.reshape(1, d), wr)


def _segment_copies(tab_ref, i, e, local_ref, slot_ref, sem, to_slots):
    base = (i * N_EXPERTS + e) * 3
    slot0, rows, local0 = tab_ref[base], tab_ref[base + 1], tab_ref[base + 2]
    out = []
    done = 0
    size = MOE_TOKEN_TILE
    while size >= SEG_ALIGN:
        take = rows & size
        loc = local_ref.at[pl.ds(pl.multiple_of(local0 + done, SEG_ALIGN), size)]
        slt = slot_ref.at[pl.ds(pl.multiple_of(slot0 + done, SEG_ALIGN), size)]
        desc = pltpu.make_async_copy(loc, slt, sem) if to_slots else pltpu.make_async_copy(slt, loc, sem)
        out.append((take != 0, desc))
        done = done + take
        size //= 2
    return out


def _run_segment_copies(tab_ref, tile, slot, rows_s, slot_ref, sems, to_slots, action):
    for e in range(N_EXPERTS):
        for cond, desc in _segment_copies(tab_ref, tile, e, rows_s.at[slot], slot_ref, sems.at[slot], to_slots):
            @pl.when(cond)
            def _():
                getattr(desc, action)()


def _dispatch_kernel(tab_ref, zf_ref, x_ref, g_ref, r_ref, xs_ref, rows_s, zero_s, sem, zsem, *, tile_rows):
    i = pl.program_id(0)
    tm = x_ref.shape[0]
    n_local = rows_s.shape[1]

    @pl.when(i == 0)
    def _():
        zero_s[...] = jnp.zeros_like(zero_s)

        def zero_copy(e):
            row0 = pl.multiple_of(zf_ref[e], tile_rows)
            return pltpu.make_async_copy(zero_s, xs_ref.at[pl.ds(row0, tile_rows)], zsem)

        for e in range(zf_ref.shape[0]):
            @pl.when(zf_ref[e] >= 0)
            def _():
                zero_copy(e).start()
        for e in range(zf_ref.shape[0]):
            @pl.when(zf_ref[e] >= 0)
            def _():
                zero_copy(e).wait()

    x = x_ref[...]
    ms = jnp.mean(x * x, axis=-1, keepdims=True)
    xn = ((x * lax.rsqrt(ms + EPS)) * g_ref[...]).astype(BF16)
    row_id = lax.broadcasted_iota(jnp.int32, (n_local, tm), 0).astype(F32)
    onehot = jnp.where((row_id == r_ref[0:1, :]) | (row_id == r_ref[1:2, :]), 1.0, 0.0).astype(BF16)
    slot = lax.rem(i, 2)
    rows_s[slot] = _dot(onehot, xn)

    _run_segment_copies(tab_ref, i, slot, rows_s, xs_ref, sem, True, "start")

    @pl.when(i > 0)
    def _():
        _run_segment_copies(tab_ref, i - 1, 1 - slot, rows_s, xs_ref, sem, True, "wait")

    @pl.when(i == pl.num_programs(0) - 1)
    def _():
        _run_segment_copies(tab_ref, i, slot, rows_s, xs_ref, sem, True, "wait")


def moe_dispatch(x, gain, r, tab, zf_rows, n_slots, tm, tile_rows):
    t, d = x.shape
    nt = t // tm
    n_local = TOP_K * tm + N_EXPERTS * SEG_ALIGN
    grid_spec = pltpu.PrefetchScalarGridSpec(
        num_scalar_prefetch=2,
        grid=(nt,),
        in_specs=[pl.BlockSpec((tm, d), lambda i, tb, zf: (i, 0)),
                  pl.BlockSpec((1, d), lambda i, tb, zf: (0, 0)),
                  pl.BlockSpec((N_EXPERTS, tm), lambda i, tb, zf: (0, i))],
        out_specs=pl.BlockSpec(memory_space=pl.ANY),
        scratch_shapes=[pltpu.VMEM((2, n_local, d), F32), pltpu.VMEM((tile_rows, d), F32),
                        pltpu.SemaphoreType.DMA((2,)), pltpu.SemaphoreType.DMA],
    )
    return pl.pallas_call(
        functools.partial(_dispatch_kernel, tile_rows=tile_rows),
        out_shape=jax.ShapeDtypeStruct((n_slots, d), F32),
        grid_spec=grid_spec,
        compiler_params=_cparams(("arbitrary",)),
        name="moe_dispatch",
    )(tab, zf_rows, x, gain.reshape(1, d), r)


def _combine_kernel(tab_ref, h_ref, wt_ref, ys_ref, o_ref, rows_s, sems):
    i = pl.program_id(0)
    tm = h_ref.shape[0]
    n_local = rows_s.shape[1]
    slot = lax.rem(i, 2)

    def fetch(tile, into):
        rows_s[into] = jnp.zeros(rows_s.shape[1:], rows_s.dtype)
        _run_segment_copies(tab_ref, tile, into, rows_s, ys_ref, sems, False, "start")

    @pl.when(i == 0)
    def _():
        fetch(i, slot)

    @pl.when(i + 1 < pl.num_programs(0))
    def _():
        fetch(i + 1, 1 - slot)

    _run_segment_copies(tab_ref, i, slot, rows_s, ys_ref, sems, False, "wait")

    wt = wt_ref[...]
    y = rows_s[slot].astype(BF16)
    col_id = lax.broadcasted_iota(jnp.int32, (tm, n_local), 1).astype(F32)
    pick_a = jnp.where(col_id == wt[:, 2:3], 1.0, 0.0).astype(BF16)
    pick_b = jnp.where(col_id == wt[:, 3:4], 1.0, 0.0).astype(BF16)
    o_ref[...] = h_ref[...] + wt[:, 0:1] * _dot(pick_a, y) + wt[:, 1:2] * _dot(pick_b, y)


def moe_combine(h, wt, tab, ys, tm):
    t, d = h.shape
    nt = t // tm
    n_local = TOP_K * tm + N_EXPERTS * SEG_ALIGN
    grid_spec = pltpu.PrefetchScalarGridSpec(
        num_scalar_prefetch=1,
        grid=(nt,),
        in_specs=[pl.BlockSpec((tm, d), lambda i, tb: (i, 0)),
                  pl.BlockSpec((tm, LANES), lambda i, tb: (i, 0)),
                  pl.BlockSpec(memory_space=pl.ANY)],
        out_specs=pl.BlockSpec((tm, d), lambda i, tb: (i, 0)),
        scratch_shapes=[pltpu.VMEM((2, n_local, d), F32), pltpu.SemaphoreType.DMA((2,))],
    )
    return pl.pallas_call(
        _combine_kernel,
        out_shape=jax.ShapeDtypeStruct((t, d), F32),
        grid_spec=grid_spec,
        compiler_params=_cparams(("arbitrary",)),
        name="moe_combine",
    )(tab, h, wt, ys)


MOE_TILE_ROWS = 512
MOE_TOKEN_TILE = 512
SEG_ALIGN = 8
TOP_K = 2


def moe_layer(h, gain, w_router, wg, wu, wd):
    t, d = h.shape
    ne = w_router.shape[1]
    tr, tm = MOE_TILE_ROWS, MOE_TOKEN_TILE
    nt = t // tm
    n_tiles = -(-(TOP_K * t + nt * ne * (SEG_ALIGN - 1) + ne * (tr - 1)) // tr)
    n_slots = n_tiles * tr

    r, wt, tab, cnt = moe_route(h, gain, w_router, tm, tr)
    tab = jnp.transpose(tab[:, :, :, 0], (0, 2, 1)).astype(jnp.int32).reshape(-1)

    counts = cnt[:, 0].astype(jnp.int32)
    padded = ((counts + (tr - 1)) // tr) * tr
    ends = jnp.cumsum(padded)
    n_valid = (ends[-1] // tr).astype(jnp.int32)
    tile_row0 = jnp.arange(n_tiles, dtype=jnp.int32) * tr
    tile_expert = jnp.sum((tile_row0[:, None] >= ends[None, :]).astype(jnp.int32), axis=1)
    tile_expert = jnp.minimum(tile_expert, ne - 1)
    tile_expert = jnp.where(jnp.arange(n_tiles) < n_valid, tile_expert, tile_expert[jnp.maximum(n_valid - 1, 0)])
    prev_expert = jnp.concatenate([jnp.full((1,), -1, jnp.int32), tile_expert[:-1]])
    tile_first = (tile_expert != prev_expert).astype(jnp.int32)
    tail = jnp.arange(TOP_K * t // tr, n_tiles, dtype=jnp.int32)
    zf_rows = jnp.concatenate([jnp.where(padded > 0, ends - tr, -1),
                               jnp.where(tail >= n_valid, tail * tr, -1)]).astype(jnp.int32)

    xs = moe_dispatch(h, gain, r, tab, zf_rows, n_slots, tm, tr)
    ys = expert_swiglu(xs, gain, tile_expert, tile_first, n_valid.reshape(1), wg, wu, wd, tr, 512, F32, False,
                       "moe_experts")
    return moe_combine(h, wt, tab, ys, tm)


def kernel(x, a_norm, a_w_in, a_conv, a_log_decay, a_dt_bias, a_out_norm, a_w_out, kv_norm, kv_w, k_norm,
           b_norm, b_w_q, q_norm, b_sinks, b_w_o, rel_bias, ffn_norm, dense_w_gate, dense_w_up, dense_w_down,
           moe_router, moe_w_gate, moe_w_up, moe_w_down):
    batch, seq, d = x.shape
    t = batch * seq
    nh, hd = LA_HEADS, LA_D
    main_w = 4 * nh * hd
    h0 = x.reshape(t, d)

    w_in = a_w_in[0]
    w_main = w_in[:, 0:main_w].astype(BF16)
    w_gate = jnp.zeros((d, LANES), BF16).at[:, 0:2 * nh].set(w_in[:, main_w:main_w + 2 * nh].astype(BF16))
    proj, gates = norm_matmul(h0, [(a_norm[0], w_main, BF16), (a_norm[0], w_gate, F32)], 512, "gdn_in_proj")
    o = gdn_core(proj, gates, a_conv[0], a_log_decay[0], a_dt_bias[0], a_out_norm[0], batch, seq)
    h1 = matmul_residual(o, a_w_out[0].astype(BF16), h0, 1024, 1024, "gdn_out_proj")

    h2 = ffn_dense(h1, ffn_norm[0], dense_w_gate[0], dense_w_up[0], dense_w_down[0], 512, 512)

    kv, q = norm_matmul(h2, [(kv_norm, kv_w.astype(BF16), BF16), (b_norm[0], b_w_q[0].astype(BF16), BF16)],
                        1024, "qkv_proj")
    bias = bias_table(rel_bias)
    attn = swa_attention(q, kv, bias, q_norm[0], k_norm, b_sinks[0], batch, seq)
    h3 = matmul_residual(attn, b_w_o[0].astype(BF16), h2, 1024, 1024, "attn_out_proj")

    h4 = moe_layer(h3, ffn_norm[1], moe_router[0], moe_w_gate[0], moe_w_up[0], moe_w_down[0])
    return h4.reshape(batch, seq, d)
```

```python
import functools

import numpy as np
import jax
import jax.numpy as jnp
from jax import lax
from jax.experimental import pallas as pl
from jax.experimental.pallas import tpu as pltpu

F32 = jnp.float32
BF16 = jnp.bfloat16

EPS = 1e-6
NEG_INF = -1e30

LA_HEADS = 8
LA_D = 128
CONV_W = 4
CHUNK = 64
SW_HEADS = 16
SW_KV_HEADS = 4
SW_GROUP = SW_HEADS // SW_KV_HEADS
SW_HD = 64
WINDOW = 128
N_BUCKETS = 32
MAX_DIST = 128
N_EXPERTS = 8

LANES = 128
GDN_BLOCK = 2 * CHUNK
HALO = 16

VMEM_LIMIT = 56 * 1024 * 1024


def _cparams(sem):
    return pltpu.CompilerParams(dimension_semantics=sem, vmem_limit_bytes=VMEM_LIMIT)


def _silu(x):
    return x * (1.0 / (1.0 + jnp.exp(-x)))


def _dot(a, b):
    return jnp.dot(a, b, preferred_element_type=F32)


def _dot_nt(a, b):
    return lax.dot_general(a, b, (((1,), (1,)), ((), ())), preferred_element_type=F32)


def _norm_matmul_kernel(*refs, n_groups):
    x_ref = refs[0]
    g_refs = refs[1:1 + n_groups]
    w_refs = refs[1 + n_groups:1 + 2 * n_groups]
    o_refs = refs[1 + 2 * n_groups:1 + 3 * n_groups]
    x = x_ref[...]
    xr = x * lax.rsqrt(jnp.mean(x * x, axis=-1, keepdims=True) + EPS)
    for g_ref, w_ref, o_ref in zip(g_refs, w_refs, o_refs):
        o_ref[...] = _dot((xr * g_ref[...]).astype(BF16), w_ref[...]).astype(o_ref.dtype)


def norm_matmul(x, groups, tm, name):
    t, d = x.shape
    tm = min(tm, t)
    gains = [g.reshape(1, d).astype(F32) for g, _, _ in groups]
    ws = [w for _, w, _ in groups]
    return pl.pallas_call(
        functools.partial(_norm_matmul_kernel, n_groups=len(groups)),
        out_shape=[jax.ShapeDtypeStruct((t, w.shape[1]), dt) for _, w, dt in groups],
        grid=(t // tm,),
        in_specs=([pl.BlockSpec((tm, d), lambda i: (i, 0))]
                  + [pl.BlockSpec((1, d), lambda i: (0, 0)) for _ in groups]
                  + [pl.BlockSpec(w.shape, lambda i: (0, 0)) for w in ws]),
        out_specs=[pl.BlockSpec((tm, w.shape[1]), lambda i: (i, 0)) for w in ws],
        compiler_params=_cparams(("parallel",)),
        name=name,
    )(x, *gains, *ws)


def _matmul_res_kernel(a_ref, w_ref, r_ref, o_ref):
    o_ref[...] = r_ref[...] + _dot(a_ref[...], w_ref[...])


def matmul_residual(a, w, res, tm, tn, name):
    t, k = a.shape
    n = w.shape[1]
    tm, tn = min(tm, t), min(tn, n)
    return pl.pallas_call(
        _matmul_res_kernel,
        out_shape=jax.ShapeDtypeStruct((t, n), F32),
        grid=(t // tm, n // tn),
        in_specs=[pl.BlockSpec((tm, k), lambda i, j: (i, 0)),
                  pl.BlockSpec((k, tn), lambda i, j: (0, j)),
                  pl.BlockSpec((tm, tn), lambda i, j: (i, j))],
        out_specs=pl.BlockSpec((tm, tn), lambda i, j: (i, j)),
        compiler_params=_cparams(("parallel", "parallel")),
        name=name,
    )(a, w, res)


def _gdn_kernel(proj_ref, gates_ref, convw_ref, hp_ref, onorm_ref, o_ref,
                xs_ref, state_ref, q_s, k_s, v_s, z_s, gc_s, gct_s, beta_s):
    n = pl.program_id(1)

    @pl.when(n == 0)
    def _():
        xs_ref[0:HALO, :] = jnp.zeros((HALO, xs_ref.shape[1]), xs_ref.dtype)
        for ref in (q_s, k_s, v_s, z_s, gc_s, gct_s, beta_s):
            ref[1] = jnp.zeros(ref.shape[1:], ref.dtype)

    @pl.when(n <= 1)
    def _():
        state_ref[...] = jnp.zeros_like(state_ref)

    args = (proj_ref, gates_ref, convw_ref, hp_ref, onorm_ref, o_ref, xs_ref, state_ref,
            q_s, k_s, v_s, z_s, gc_s, gct_s, beta_s)

    @pl.when(lax.rem(n, 2) == 0)
    def _():
        _gdn_step(*args, slot_w=0, slot_r=1)

    @pl.when(lax.rem(n, 2) == 1)
    def _():
        _gdn_step(*args, slot_w=1, slot_r=0)


def _gdn_step(proj_ref, gates_ref, convw_ref, hp_ref, onorm_ref, o_ref, xs_ref, state_ref,
              q_s, k_s, v_s, z_s, gc_s, gct_s, beta_s, *, slot_w, slot_r):
    nh, d, c = LA_HEADS, LA_D, CHUNK
    blk = GDN_BLOCK
    qkv_w = 3 * nh * d

    gc = gc_s[slot_r]
    gc_t = gct_s[slot_r]
    beta = beta_s[slot_r]

    xs_ref[HALO:HALO + blk, :] = proj_ref[:, 0:qkv_w]

    def front_gates():
        _gdn_front_gates(gates_ref, hp_ref, gc_s, gct_s, beta_s, slot_w)

    ci = lax.broadcasted_iota(jnp.int32, (c, c), 0)
    cj = lax.broadcasted_iota(jnp.int32, (c, c), 1)
    lower_incl = ci >= cj
    strict = ci > cj
    eye_c = jnp.where(ci == cj, 1.0, 0.0).astype(F32)
    di = lax.broadcasted_iota(jnp.int32, (d, d), 0)
    dj = lax.broadcasted_iota(jnp.int32, (d, d), 1)
    eye_d = jnp.where(di == dj, 1.0, 0.0).astype(BF16)

    onorm = onorm_ref[...]

    n_shift = CONV_W - 1
    sr = lax.broadcasted_iota(jnp.int32, (n_shift * blk, HALO + blk), 0)
    sc = lax.broadcasted_iota(jnp.int32, (n_shift * blk, HALO + blk), 1)
    shift_mat = jnp.where(sc == HALO + (sr % blk) - (sr // blk + 1), 1.0, 0.0).astype(BF16)
    pair_w = 2 * d

    def conv_silu(col0):
        cols = slice(col0, col0 + pair_w)
        shifted = _dot(shift_mat, xs_ref[:, cols])
        acc = convw_ref[CONV_W - 1:CONV_W, cols] * xs_ref[HALO:HALO + blk, cols].astype(F32)
        for s in range(1, CONV_W):
            acc = acc + convw_ref[CONV_W - 1 - s:CONV_W - s, cols] * shifted[(s - 1) * blk:s * blk]
        return _silu(acc)

    def front_pair(hp):
        c0 = hp * pair_w
        qf = conv_silu(c0)
        kf = conv_silu(nh * d + c0)
        v_s[slot_w, :, c0:c0 + pair_w] = conv_silu(2 * nh * d + c0)
        for half in range(2):
            lo, hi = half * d, (half + 1) * d
            qh, kh = qf[:, lo:hi], kf[:, lo:hi]
            q_s[slot_w, :, c0 + lo:c0 + hi] = qh * (lax.rsqrt(jnp.sum(qh * qh, axis=-1, keepdims=True) + EPS)
                                                    * (d ** -0.5))
            k_s[slot_w, :, c0 + lo:c0 + hi] = kh * lax.rsqrt(jnp.sum(kh * kh, axis=-1, keepdims=True) + EPS)
        z_s[slot_w, :, c0:c0 + pair_w] = proj_ref[:, qkv_w + c0:qkv_w + c0 + pair_w]

    front_tasks = [front_gates] + [functools.partial(front_pair, hp) for hp in range(nh // 2)]

    def run_front_task():
        if front_tasks:
            front_tasks.pop(0)()

    n_ck = blk // c
    chains = [(h, ck) for h in range(nh) for ck in range(n_ck)]

    st = {}
    for (h, ck) in chains:
        r = ck * c
        q = q_s[slot_r, r:r + c, h * d:(h + 1) * d]
        k = k_s[slot_r, r:r + c, h * d:(h + 1) * d]
        v = v_s[slot_r, r:r + c, h * d:(h + 1) * d]
        g_col = gc[r:r + c, nh + h:nh + h + 1]
        g_row = gc_t[nh + h:nh + h + 1, r:r + c]
        g_last = gc[r + c - 1:r + c, nh + h:nh + h + 1]
        b_col = beta[r:r + c, h:h + 1]
        decay = jnp.where(lower_incl, jnp.exp2(jnp.where(lower_incl, g_col - g_row, 0.0)), 0.0)
        k_beta = k * b_col
        e_col = jnp.exp2(g_col)
        lhs = jnp.concatenate([k_beta.astype(BF16), q.astype(BF16), eye_d], axis=0)
        kk = _dot_nt(lhs, k.astype(BF16))
        a_mat = jnp.where(strict, kk[0:c] * decay, 0.0)
        st[(h, ck)] = dict(
            a=a_mat, attn=(kk[c:2 * c] * decay).astype(BF16),
            k_tail_t=(kk[2 * c:2 * c + d] * jnp.exp2(g_last - g_row)).astype(BF16),
            rhs=jnp.concatenate([(v * b_col).astype(BF16), (k_beta * e_col).astype(BF16)], axis=1),
            qe=(q * e_col).astype(BF16), e_last=jnp.exp2(g_last))
    run_front_task()

    for key in chains:
        x_b = (-st[key]["a"]).astype(BF16)
        st[key]["y"] = _dot(x_b, x_b)
        st[key]["p"] = eye_c - st[key]["a"]
    run_front_task()
    n_levels = int(np.log2(c))
    for lvl in range(1, n_levels):
        for key in chains:
            y_b = st[key]["y"].astype(BF16)
            p = st[key]["p"]
            if lvl + 1 < n_levels:
                zz = _dot(jnp.concatenate([y_b, p.astype(BF16)], axis=0), y_b)
                st[key]["y"] = zz[0:c]
                st[key]["p"] = p + zz[c:2 * c]
            else:
                st[key]["p"] = p + _dot(p.astype(BF16), y_b)
        run_front_task()
    for key in chains:
        st[key]["uw"] = _dot(st[key]["p"].astype(BF16), st[key]["rhs"])
    run_front_task()

    for ck in range(n_ck):
        r = ck * c
        s_old = [state_ref[h] for h in range(nh)]
        ws_qs = []
        for h in range(nh):
            cur = st[(h, ck)]
            lhs = jnp.concatenate([cur["uw"][:, d:2 * d].astype(BF16), cur["qe"]], axis=0)
            ws_qs.append(_dot(lhs, s_old[h].astype(BF16)))
        run_front_task()
        for h in range(nh):
            cur = st[(h, ck)]
            v_new = cur["uw"][:, 0:d] - ws_qs[h][0:c]
            av_kv = _dot(jnp.concatenate([cur["attn"], cur["k_tail_t"]], axis=0), v_new.astype(BF16))
            state_ref[h] = s_old[h] * cur["e_last"] + av_kv[c:c + d]
            o = ws_qs[h][c:2 * c] + av_kv[0:c]
            o = (o * lax.rsqrt(jnp.mean(o * o, axis=-1, keepdims=True) + EPS)) * onorm
            z = z_s[slot_r, r:r + c, h * d:(h + 1) * d].astype(F32)
            o_ref[r:r + c, h * d:(h + 1) * d] = (o * _silu(z)).astype(o_ref.dtype)
    while front_tasks:
        run_front_task()

    xs_ref[0:HALO, :] = xs_ref[blk:blk + HALO, :]


def _gdn_front_gates(gates_ref, hp_ref, gc_s, gct_s, beta_s, slot_w):
    blk, c = GDN_BLOCK, CHUNK
    gates = gates_ref[...]
    a_log = hp_ref[0:1, :]
    dt_bias = hp_ref[1:2, :]
    beta = 1.0 / (1.0 + jnp.exp(-gates))
    sp_in = gates + dt_bias
    softplus = jnp.maximum(sp_in, 0.0) + jnp.log(1.0 + jnp.exp(-jnp.abs(sp_in)))
    g = (-jnp.exp(a_log) * softplus) * float(np.log2(np.e))

    row = lax.broadcasted_iota(jnp.int32, (blk, blk), 0)
    col = lax.broadcasted_iota(jnp.int32, (blk, blk), 1)
    tri = jnp.where((row >= col) & ((row // c) == (col // c)), 1.0, 0.0).astype(BF16)
    g_hi = g.astype(BF16)
    g_r1 = g - g_hi.astype(F32)
    g_mid = g_r1.astype(BF16)
    g_lo = (g_r1 - g_mid.astype(F32)).astype(BF16)
    gc = _dot(tri, g_hi) + _dot(tri, g_mid) + _dot(tri, g_lo)
    gc_s[slot_w] = gc
    gct_s[slot_w] = gc.T
    beta_s[slot_w] = beta


def gdn_core(proj, gates, conv_w, a_log, dt_bias, out_norm, batch, seq):
    t = proj.shape[0]
    nh, d = LA_HEADS, LA_D
    blk = GDN_BLOCK
    nblk = seq // blk
    hp = jnp.zeros((8, LANES), F32)
    hp = hp.at[0, nh:2 * nh].set(a_log.astype(F32)).at[1, nh:2 * nh].set(dt_bias.astype(F32))

    def in_map(b, n):
        return (b * nblk + jnp.minimum(n, nblk - 1), 0)

    return pl.pallas_call(
        _gdn_kernel,
        out_shape=jax.ShapeDtypeStruct((t, nh * d), BF16),
        grid=(batch, nblk + 1),
        in_specs=[pl.BlockSpec((blk, 4 * nh * d), in_map),
                  pl.BlockSpec((blk, LANES), in_map),
                  pl.BlockSpec((CONV_W, 3 * nh * d), lambda b, n: (0, 0)),
                  pl.BlockSpec((8, LANES), lambda b, n: (0, 0)),
                  pl.BlockSpec((1, d), lambda b, n: (0, 0))],
        out_specs=pl.BlockSpec((blk, nh * d), lambda b, n: (b * nblk + jnp.maximum(n - 1, 0), 0)),
        scratch_shapes=[pltpu.VMEM((HALO + blk, 3 * nh * d), BF16),
                        pltpu.VMEM((nh, d, d), F32),
                        pltpu.VMEM((2, blk, nh * d), F32), pltpu.VMEM((2, blk, nh * d), F32),
                        pltpu.VMEM((2, blk, nh * d), F32), pltpu.VMEM((2, blk, nh * d), BF16),
                        pltpu.VMEM((2, blk, LANES), F32), pltpu.VMEM((2, LANES, blk), F32),
                        pltpu.VMEM((2, blk, LANES), F32)],
        compiler_params=_cparams(("arbitrary", "arbitrary")),
        name="gdn_core",
    )(proj, gates, conv_w.astype(F32), hp, out_norm.reshape(1, d).astype(F32))


def _swiglu_kernel(te_ref, first_ref, nv_ref, x_ref, g_ref, wg_hbm, wu_hbm, wd_hbm, o_ref,
                   wg_c, wu_c, wd_c, stage_in, stage_out, sems, *, pre_norm, tf):
    i = pl.program_id(0)
    nf = wg_c.shape[0]
    e = te_ref[i]
    valid = i < nv_ref[0]

    def chunk_copies(j, slot):
        cols = pl.ds(j * tf, tf)
        return (pltpu.make_async_copy(wg_hbm.at[e, :, cols], stage_in.at[slot, 0], sems.at[slot, 0]),
                pltpu.make_async_copy(wu_hbm.at[e, :, cols], stage_in.at[slot, 1], sems.at[slot, 1]),
                pltpu.make_async_copy(wd_hbm.at[e, cols, :], stage_out.at[slot], sems.at[slot, 2]))

    def prepare_rows():
        x = x_ref[...].astype(F32)
        if pre_norm:
            ms = jnp.mean(x * x, axis=-1, keepdims=True)
            x = (x * lax.rsqrt(ms + EPS)) * g_ref[...]
        return x.astype(BF16)

    def chunk(xb, j):
        hid = _silu(_dot(xb, wg_c[j])) * _dot(xb, wu_c[j])
        return _dot(hid.astype(BF16), wd_c[j])

    def finish(acc):
        if pre_norm:
            o_ref[...] = (x_ref[...] + acc).astype(o_ref.dtype)
        else:
            o_ref[...] = acc.astype(o_ref.dtype)

    @pl.when(valid & (first_ref[i] == 1))
    def _():
        for c in chunk_copies(0, 0):
            c.start()
        xb = prepare_rows()
        acc = None
        for j in range(nf):
            slot = j % 2
            if j + 1 < nf:
                for c in chunk_copies(j + 1, 1 - slot):
                    c.start()
            for c in chunk_copies(j, slot):
                c.wait()
            wg_c[j] = stage_in[slot, 0].astype(BF16)
            wu_c[j] = stage_in[slot, 1].astype(BF16)
            wd_c[j] = stage_out[slot].astype(BF16)
            y = chunk(xb, j)
            acc = y if acc is None else acc + y
        finish(acc)

    @pl.when(valid & (first_ref[i] != 1))
    def _():
        xb = prepare_rows()
        acc = None
        for j in range(nf):
            y = chunk(xb, j)
            acc = y if acc is None else acc + y
        finish(acc)

    @pl.when(jnp.logical_not(valid))
    def _():
        o_ref[...] = jnp.zeros_like(o_ref)


def expert_swiglu(x, gain, tile_expert, tile_first, n_valid, wg, wu, wd, tile_rows, tf, out_dtype, pre_norm, name):
    n_rows, d = x.shape
    ne, _, f = wg.shape
    n_tiles = n_rows // tile_rows
    nf = f // tf
    grid_spec = pltpu.PrefetchScalarGridSpec(
        num_scalar_prefetch=3,
        grid=(n_tiles,),
        in_specs=[pl.BlockSpec((tile_rows, d), lambda i, te, fi, nv: (jnp.minimum(i, nv[0] - 1), 0)),
                  pl.BlockSpec((1, d), lambda i, te, fi, nv: (0, 0)),
                  pl.BlockSpec(memory_space=pl.ANY),
                  pl.BlockSpec(memory_space=pl.ANY),
                  pl.BlockSpec(memory_space=pl.ANY)],
        out_specs=pl.BlockSpec((tile_rows, d), lambda i, te, fi, nv: (i, 0)),
        scratch_shapes=[pltpu.VMEM((nf, d, tf), BF16), pltpu.VMEM((nf, d, tf), BF16), pltpu.VMEM((nf, tf, d), BF16),
                        pltpu.VMEM((2, 2, d, tf), F32), pltpu.VMEM((2, tf, d), F32),
                        pltpu.SemaphoreType.DMA((2, 3))],
    )
    return pl.pallas_call(
        functools.partial(_swiglu_kernel, pre_norm=pre_norm, tf=tf),
        out_shape=jax.ShapeDtypeStruct((n_rows, d), out_dtype),
        grid_spec=grid_spec,
        compiler_params=_cparams(("arbitrary",)),
        name=name,
    )(tile_expert, tile_first, n_valid, x, gain.reshape(1, d).astype(F32), wg, wu, wd)


def ffn_dense(x, gain, wg, wu, wd, tm, tf):
    t = x.shape[0]
    n_tiles = t // tm
    tile_first = jnp.zeros((n_tiles,), jnp.int32).at[0].set(1)
    return expert_swiglu(x, gain, jnp.zeros((n_tiles,), jnp.int32), tile_first, jnp.full((1,), n_tiles, jnp.int32),
                         wg[None], wu[None], wd[None], tm, tf, F32, True, "ffn_dense")


def _t5_bucket_np(dist):
    max_exact = N_BUCKETS // 2
    n = np.maximum(dist, 0)
    safe = np.maximum(n, 1).astype(np.float32)
    large = max_exact + (np.log(safe / max_exact) / np.log(MAX_DIST / max_exact)
                         * (N_BUCKETS - max_exact)).astype(np.int32)
    large = np.minimum(large, N_BUCKETS - 1)
    return np.where(n < max_exact, n, large).astype(np.int32)


LOG2E = float(np.log2(np.e))


def _bias_kernel(bucket_ref, valid_ref, rb_ref, o_ref):
    bucket = bucket_ref[...]
    for h in range(SW_HEADS):
        acc = jnp.zeros(bucket.shape, F32)
        for b in range(N_BUCKETS):
            acc = jnp.where(bucket == b, rb_ref[b, h], acc)
        for v in range(valid_ref.shape[0]):
            o_ref[v, h] = jnp.where(valid_ref[v] > 0, acc * LOG2E, NEG_INF)


def bias_table(rel_bias):
    qi = np.arange(WINDOW)[:, None] + WINDOW
    kj = np.arange(2 * WINDOW)[None, :]
    dist = qi - kj
    band = (dist >= 0) & (dist < WINDOW)
    valid = np.stack([band, band & (kj >= WINDOW)]).astype(np.int32)
    return pl.pallas_call(
        _bias_kernel,
        out_shape=jax.ShapeDtypeStruct((2, SW_HEADS, WINDOW, 2 * WINDOW), F32),
        in_specs=[pl.BlockSpec(memory_space=pltpu.VMEM), pl.BlockSpec(memory_space=pltpu.VMEM),
                  pl.BlockSpec(memory_space=pltpu.SMEM)],
        out_specs=pl.BlockSpec(memory_space=pltpu.VMEM),
        name="t5_bias_table",
    )(jnp.asarray(_t5_bucket_np(dist)), jnp.asarray(valid), rel_bias.astype(F32))


def _swa_kernel(q_ref, kvp_ref, kvc_ref, bias_ref, qn_ref, kn_ref, sink_ref, o_ref):
    blk, hd = WINDOW, SW_HD
    kv_w = SW_KV_HEADS * hd
    variant = jnp.where(pl.program_id(1) == 0, 1, 0)
    gw = 2 * LANES
    gi = lax.broadcasted_iota(jnp.int32, (gw, gw), 0)
    gj = lax.broadcasted_iota(jnp.int32, (gw, gw), 1)
    group_ones = jnp.where((gi // hd) == (gj // hd), 1.0, 0.0).astype(BF16)
    lane = lax.broadcasted_iota(jnp.int32, (1, LANES), 1)
    low_half = lane < hd

    def head_norm(x, gain):
        cols = []
        for c0 in range(0, x.shape[1], gw):
            xc = x[:, c0:c0 + gw]
            ss = _dot((xc * xc).astype(BF16), group_ones)
            cols.append(xc * lax.rsqrt(ss * (1.0 / hd) + EPS))
        return jnp.concatenate(cols, axis=1) * gain

    def dup_half(x, half):
        swapped = pltpu.roll(x, hd, 1)
        return jnp.where(low_half == (half == 0), x, swapped)

    qn = head_norm(q_ref[...].astype(F32), qn_ref[...]) * ((hd ** -0.5) * LOG2E)
    half_sel = [jnp.where(low_half, 1.0, 0.0), jnp.where(low_half, 0.0, 1.0)]
    k_all = jnp.concatenate([kvp_ref[:, 0:kv_w], kvc_ref[:, 0:kv_w]], axis=0).astype(F32)
    kn = head_norm(k_all, kn_ref[...])
    v_all = jnp.concatenate([kvp_ref[:, kv_w:2 * kv_w], kvc_ref[:, kv_w:2 * kv_w]], axis=0).astype(F32)
    ks, vs = [], []
    for g in range(SW_KV_HEADS):
        c0 = (g // 2) * LANES
        ks.append(dup_half(kn[:, c0:c0 + LANES], g % 2).astype(BF16))
        vs.append(dup_half(v_all[:, c0:c0 + LANES], g % 2).astype(BF16))

    scores = []
    for hq in range(SW_HEADS):
        c0 = (hq // 2) * LANES
        q_h = (qn[:, c0:c0 + LANES] * half_sel[hq % 2]).astype(BF16)
        scores.append(_dot_nt(q_h, ks[hq // SW_GROUP]))
    probs = []
    for hq in range(SW_HEADS):
        s = scores[hq] + bias_ref[variant, hq]
        sink = sink_ref[hq] * LOG2E
        mx = jnp.maximum(jnp.max(s, axis=-1, keepdims=True), sink)
        p = jnp.exp2(s - mx)
        denom = jnp.sum(p, axis=-1, keepdims=True) + jnp.exp2(sink - mx)
        probs.append((p / denom).astype(BF16))
    outs = [_dot(probs[hq], vs[hq // SW_GROUP]) for hq in range(SW_HEADS)]
    for c in range(SW_HEADS // 2):
        o_ref[:, c * LANES:(c + 1) * LANES] = jnp.where(low_half, outs[2 * c], outs[2 * c + 1]).astype(o_ref.dtype)


def swa_attention(q, kv, bias, q_norm, k_norm, sinks, batch, seq):
    t = q.shape[0]
    blk = WINDOW
    nb = seq // blk
    qw = SW_HEADS * SW_HD
    kvw = 2 * SW_KV_HEADS * SW_HD
    return pl.pallas_call(
        _swa_kernel,
        out_shape=jax.ShapeDtypeStruct((t, qw), BF16),
        grid=(batch, nb),
        in_specs=[pl.BlockSpec((blk, qw), lambda b, n: (b * nb + n, 0)),
                  pl.BlockSpec((blk, kvw), lambda b, n: (b * nb + jnp.maximum(n - 1, 0), 0)),
                  pl.BlockSpec((blk, kvw), lambda b, n: (b * nb + n, 0)),
                  pl.BlockSpec((2, SW_HEADS, blk, 2 * blk), lambda b, n: (0, 0, 0, 0)),
                  pl.BlockSpec((1, qw), lambda b, n: (0, 0)),
                  pl.BlockSpec((1, kvw // 2), lambda b, n: (0, 0)),
                  pl.BlockSpec(memory_space=pltpu.SMEM)],
        out_specs=pl.BlockSpec((blk, qw), lambda b, n: (b * nb + n, 0)),
        compiler_params=_cparams(("parallel", "parallel")),
        name="swa_attention",
    )(q, kv, kv, bias, jnp.tile(q_norm.astype(F32), SW_HEADS).reshape(1, qw),
      jnp.tile(k_norm.astype(F32), SW_KV_HEADS).reshape(1, kvw // 2), sinks.astype(F32))


def _route_kernel(x_ref, g_ref, wr_ref, r_ref, wt_ref, tab_ref, cnt_ref, sel_s, gw_s, cnt_s, start_s, run_s,
                  *, tile_rows):
    ne = N_EXPERTS
    p = pl.program_id(0)
    i = pl.program_id(1)
    tm = x_ref.shape[0]
    sub = lax.broadcasted_iota(jnp.int32, (ne, tm), 0).astype(F32)

    def seg_rows(sel):
        n = jnp.sum(sel, axis=1, keepdims=True)
        return jnp.floor((n + (SEG_ALIGN - 1)) * (1.0 / SEG_ALIGN)) * SEG_ALIGN

    def excl_cumsum_experts(v):
        sub8 = lax.broadcasted_iota(jnp.int32, v.shape, 0)
        out = jnp.zeros_like(v)
        for e in range(ne - 1):
            out = out + jnp.where(sub8 > e, v[e:e + 1, :], 0.0)
        return out

    @pl.when(p == 0)
    def _():
        @pl.when(i == 0)
        def _():
            cnt_s[...] = jnp.zeros_like(cnt_s)

        x = x_ref[...]
        ms = jnp.mean(x * x, axis=-1, keepdims=True)
        xn32 = (x * lax.rsqrt(ms + EPS)) * g_ref[...]
        xn_hi = xn32.astype(BF16)
        xn_lo = (xn32 - xn_hi.astype(F32)).astype(BF16)
        p_hi = _dot_nt(wr_ref[...], xn_hi)
        p_lo = _dot_nt(wr_ref[...], xn_lo)
        logits = p_hi[0:ne] + p_hi[ne:2 * ne] + p_lo[0:ne]
        m1 = jnp.max(logits, axis=0, keepdims=True)
        i1 = jnp.min(jnp.where(logits == m1, sub, float(ne)), axis=0, keepdims=True)
        l2 = jnp.where(sub == i1, -jnp.inf, logits)
        m2 = jnp.max(l2, axis=0, keepdims=True)
        i2 = jnp.min(jnp.where(l2 == m2, sub, float(ne)), axis=0, keepdims=True)
        e2 = jnp.exp(m2 - m1)
        w1 = 1.0 / (1.0 + e2)
        w2 = e2 / (1.0 + e2)
        sel = jnp.where((sub == i1) | (sub == i2), 1.0, 0.0)
        sel_s[i] = sel
        gw_s[i] = jnp.where(sub == i1, w1, jnp.where(sub == i2, w2, 0.0))
        cnt_s[...] += seg_rows(sel)

    @pl.when(p == 1)
    def _():
        @pl.when(i == 0)
        def _():
            cnt = cnt_s[...]
            padded = jnp.floor((cnt + (tile_rows - 1)) * (1.0 / tile_rows)) * tile_rows
            start_s[...] = excl_cumsum_experts(padded)
            run_s[...] = jnp.zeros_like(run_s)
            cnt_ref[...] = cnt

        sel = sel_s[i]
        gw = gw_s[i]
        ti = lax.broadcasted_iota(jnp.int32, (tm, tm), 0)
        tj = lax.broadcasted_iota(jnp.int32, (tm, tm), 1)
        tri = jnp.where(ti <= tj, 1.0, 0.0).astype(BF16)
        csum = _dot(sel.astype(BF16), tri)
        seg = jnp.broadcast_to(seg_rows(sel), run_s.shape)
        local0 = excl_cumsum_experts(seg)
        tab_ref[0, 0] = start_s[...] + run_s[...]
        tab_ref[0, 1] = seg
        tab_ref[0, 2] = local0
        run_s[...] += seg
        local_row = local0[:, 0:1] + csum - sel
        ia = jnp.min(jnp.where(sel > 0.0, sub, float(ne)), axis=0, keepdims=True)
        ib = jnp.max(jnp.where(sel > 0.0, sub, -1.0), axis=0, keepdims=True)
        pick_a = sub == ia
        pick_b = sub == ib
        rows = [jnp.sum(jnp.where(pick_a, local_row, 0.0), axis=0, keepdims=True),
                jnp.sum(jnp.where(pick_b, local_row, 0.0), axis=0, keepdims=True),
                jnp.sum(jnp.where(pick_a, gw, 0.0), axis=0, keepdims=True),
                jnp.sum(jnp.where(pick_b, gw, 0.0), axis=0, keepdims=True)]
        r_ref[...] = jnp.concatenate(rows + [jnp.zeros((ne - 4, tm), F32)], axis=0)
        wpad = jnp.concatenate(rows[2:4] + rows[0:2] + [jnp.zeros((LANES - 4, tm), F32)], axis=0)
        wt_ref[...] = wpad.T


def moe_route(x, gain, w_router, tm, tile_rows):
    t, d = x.shape
    ne = w_router.shape[1]
    assert ne == N_EXPERTS
    w_hi = w_router.astype(BF16)
    w_lo = (w_router - w_hi.astype(F32)).astype(BF16)
    wr = jnp.concatenate([w_hi.T, w_lo.T], axis=0)
    tm = min(tm, t)
    nt = t // tm
    return pl.pallas_call(
        functools.partial(_route_kernel, tile_rows=tile_rows),
        out_shape=(jax.ShapeDtypeStruct((ne, t), F32), jax.ShapeDtypeStruct((t, LANES), F32),
                   jax.ShapeDtypeStruct((nt, 3, ne, LANES), F32), jax.ShapeDtypeStruct((ne, LANES), F32)),
        grid=(2, nt),
        in_specs=[pl.BlockSpec((tm, d), lambda p, i: (i * (1 - p) + (nt - 1) * p, 0)),
                  pl.BlockSpec((1, d), lambda p, i: (0, 0)),
                  pl.BlockSpec((2 * ne, d), lambda p, i: (0, 0))],
        out_specs=(pl.BlockSpec((ne, tm), lambda p, i: (0, i * p)),
                   pl.BlockSpec((tm, LANES), lambda p, i: (i * p, 0)),
                   pl.BlockSpec((1, 3, ne, LANES), lambda p, i: (i * p, 0, 0, 0)),
                   pl.BlockSpec((ne, LANES), lambda p, i: (0, 0))),
        scratch_shapes=[pltpu.VMEM((nt, ne, tm), F32), pltpu.VMEM((nt, ne, tm), F32),
                        pltpu.VMEM((ne, LANES), F32), pltpu.VMEM((ne, LANES), F32), pltpu.VMEM((ne, LANES), F32)],
        compiler_params=_cparams(("arbitrary", "arbitrary")),
        name="moe_route",
    )(x, gain.reshape(1, d), wr)


def _segment_copies(tab_ref, i, e, local_ref, slot_ref, sem, to_slots):
    base = (i * N_EXPERTS + e) * 3
    slot0, rows, local0 = tab_ref[base], tab_ref[base + 1], tab_ref[base + 2]
    out = []
    done = 0
    size = MOE_TOKEN_TILE
    while size >= SEG_ALIGN:
        take = rows & size
        loc = local_ref.at[pl.ds(pl.multiple_of(local0 + done, SEG_ALIGN), size)]
        slt = slot_ref.at[pl.ds(pl.multiple_of(slot0 + done, SEG_ALIGN), size)]
        desc = pltpu.make_async_copy(loc, slt, sem) if to_slots else pltpu.make_async_copy(slt, loc, sem)
        out.append((take != 0, desc))
        done = done + take
        size //= 2
    return out


def _run_segment_copies(tab_ref, tile, slot, rows_s, slot_ref, sems, to_slots, action):
    for e in range(N_EXPERTS):
        for cond, desc in _segment_copies(tab_ref, tile, e, rows_s.at[slot], slot_ref, sems.at[slot], to_slots):
            @pl.when(cond)
            def _():
                getattr(desc, action)()


def _dispatch_kernel(tab_ref, zf_ref, x_ref, g_ref, r_ref, xs_ref, rows_s, zero_s, sem, zsem, *, tile_rows):
    i = pl.program_id(0)
    tm = x_ref.shape[0]
    n_local = rows_s.shape[1]

    @pl.when(i == 0)
    def _():
        zero_s[...] = jnp.zeros_like(zero_s)

        def zero_copy(e):
            row0 = pl.multiple_of(zf_ref[e], tile_rows)
            return pltpu.make_async_copy(zero_s, xs_ref.at[pl.ds(row0, tile_rows)], zsem)

        for e in range(zf_ref.shape[0]):
            @pl.when(zf_ref[e] >= 0)
            def _():
                zero_copy(e).start()
        for e in range(zf_ref.shape[0]):
            @pl.when(zf_ref[e] >= 0)
            def _():
                zero_copy(e).wait()

    x = x_ref[...]
    ms = jnp.mean(x * x, axis=-1, keepdims=True)
    xn = ((x * lax.rsqrt(ms + EPS)) * g_ref[...]).astype(BF16)
    row_id = lax.broadcasted_iota(jnp.int32, (n_local, tm), 0).astype(F32)
    onehot = jnp.where((row_id == r_ref[0:1, :]) | (row_id == r_ref[1:2, :]), 1.0, 0.0).astype(BF16)
    slot = lax.rem(i, 2)
    rows_s[slot] = _dot(onehot, xn)

    _run_segment_copies(tab_ref, i, slot, rows_s, xs_ref, sem, True, "start")

    @pl.when(i > 0)
    def _():
        _run_segment_copies(tab_ref, i - 1, 1 - slot, rows_s, xs_ref, sem, True, "wait")

    @pl.when(i == pl.num_programs(0) - 1)
    def _():
        _run_segment_copies(tab_ref, i, slot, rows_s, xs_ref, sem, True, "wait")


def moe_dispatch(x, gain, r, tab, zf_rows, n_slots, tm, tile_rows):
    t, d = x.shape
    nt = t // tm
    n_local = TOP_K * tm + N_EXPERTS * SEG_ALIGN
    grid_spec = pltpu.PrefetchScalarGridSpec(
        num_scalar_prefetch=2,
        grid=(nt,),
        in_specs=[pl.BlockSpec((tm, d), lambda i, tb, zf: (i, 0)),
                  pl.BlockSpec((1, d), lambda i, tb, zf: (0, 0)),
                  pl.BlockSpec((N_EXPERTS, tm), lambda i, tb, zf: (0, i))],
        out_specs=pl.BlockSpec(memory_space=pl.ANY),
        scratch_shapes=[pltpu.VMEM((2, n_local, d), F32), pltpu.VMEM((tile_rows, d), F32),
                        pltpu.SemaphoreType.DMA((2,)), pltpu.SemaphoreType.DMA],
    )
    return pl.pallas_call(
        functools.partial(_dispatch_kernel, tile_rows=tile_rows),
        out_shape=jax.ShapeDtypeStruct((n_slots, d), F32),
        grid_spec=grid_spec,
        compiler_params=_cparams(("arbitrary",)),
        name="moe_dispatch",
    )(tab, zf_rows, x, gain.reshape(1, d), r)


def _combine_kernel(tab_ref, h_ref, wt_ref, ys_ref, o_ref, rows_s, sems):
    i = pl.program_id(0)
    tm = h_ref.shape[0]
    n_local = rows_s.shape[1]
    slot = lax.rem(i, 2)

    def fetch(tile, into):
        rows_s[into] = jnp.zeros(rows_s.shape[1:], rows_s.dtype)
        _run_segment_copies(tab_ref, tile, into, rows_s, ys_ref, sems, False, "start")

    @pl.when(i == 0)
    def _():
        fetch(i, slot)

    @pl.when(i + 1 < pl.num_programs(0))
    def _():
        fetch(i + 1, 1 - slot)

    _run_segment_copies(tab_ref, i, slot, rows_s, ys_ref, sems, False, "wait")

    wt = wt_ref[...]
    y = rows_s[slot].astype(BF16)
    col_id = lax.broadcasted_iota(jnp.int32, (tm, n_local), 1).astype(F32)
    pick_a = jnp.where(col_id == wt[:, 2:3], 1.0, 0.0).astype(BF16)
    pick_b = jnp.where(col_id == wt[:, 3:4], 1.0, 0.0).astype(BF16)
    o_ref[...] = h_ref[...] + wt[:, 0:1] * _dot(pick_a, y) + wt[:, 1:2] * _dot(pick_b, y)


def moe_combine(h, wt, tab, ys, tm):
    t, d = h.shape
    nt = t // tm
    n_local = TOP_K * tm + N_EXPERTS * SEG_ALIGN
    grid_spec = pltpu.PrefetchScalarGridSpec(
        num_scalar_prefetch=1,
        grid=(nt,),
        in_specs=[pl.BlockSpec((tm, d), lambda i, tb: (i, 0)),
                  pl.BlockSpec((tm, LANES), lambda i, tb: (i, 0)),
                  pl.BlockSpec(memory_space=pl.ANY)],
        out_specs=pl.BlockSpec((tm, d), lambda i, tb: (i, 0)),
        scratch_shapes=[pltpu.VMEM((2, n_local, d), F32), pltpu.SemaphoreType.DMA((2,))],
    )
    return pl.pallas_call(
        _combine_kernel,
        out_shape=jax.ShapeDtypeStruct((t, d), F32),
        grid_spec=grid_spec,
        compiler_params=_cparams(("arbitrary",)),
        name="moe_combine",
    )(tab, h, wt, ys)


MOE_TILE_ROWS = 512
MOE_TOKEN_TILE = 512
SEG_ALIGN = 8
TOP_K = 2


def moe_layer(h, gain, w_router, wg, wu, wd):
    t, d = h.shape
    ne = w_router.shape[1]
    tr, tm = MOE_TILE_ROWS, MOE_TOKEN_TILE
    nt = t // tm
    n_tiles = -(-(TOP_K * t + nt * ne * (SEG_ALIGN - 1) + ne * (tr - 1)) // tr)
    n_slots = n_tiles * tr

    r, wt, tab, cnt = moe_route(h, gain, w_router, tm, tr)
    tab = jnp.transpose(tab[:, :, :, 0], (0, 2, 1)).astype(jnp.int32).reshape(-1)

    counts = cnt[:, 0].astype(jnp.int32)
    padded = ((counts + (tr - 1)) // tr) * tr
    ends = jnp.cumsum(padded)
    n_valid = (ends[-1] // tr).astype(jnp.int32)
    tile_row0 = jnp.arange(n_tiles, dtype=jnp.int32) * tr
    tile_expert = jnp.sum((tile_row0[:, None] >= ends[None, :]).astype(jnp.int32), axis=1)
    tile_expert = jnp.minimum(tile_expert, ne - 1)
    tile_expert = jnp.where(jnp.arange(n_tiles) < n_valid, tile_expert, tile_expert[jnp.maximum(n_valid - 1, 0)])
    prev_expert = jnp.concatenate([jnp.full((1,), -1, jnp.int32), tile_expert[:-1]])
    tile_first = (tile_expert != prev_expert).astype(jnp.int32)
    tail = jnp.arange(TOP_K * t // tr, n_tiles, dtype=jnp.int32)
    zf_rows = jnp.concatenate([jnp.where(padded > 0, ends - tr, -1),
                               jnp.where(tail >= n_valid, tail * tr, -1)]).astype(jnp.int32)

    xs = moe_dispatch(h, gain, r, tab, zf_rows, n_slots, tm, tr)
    ys = expert_swiglu(xs, gain, tile_expert, tile_first, n_valid.reshape(1), wg, wu, wd, tr, 512, F32, False,
                       "moe_experts")
    return moe_combine(h, wt, tab, ys, tm)


def kernel(x, a_norm, a_w_in, a_conv, a_log_decay, a_dt_bias, a_out_norm, a_w_out, kv_norm, kv_w, k_norm,
           b_norm, b_w_q, q_norm, b_sinks, b_w_o, rel_bias, ffn_norm, dense_w_gate, dense_w_up, dense_w_down,
           moe_router, moe_w_gate, moe_w_up, moe_w_down):
    batch, seq, d = x.shape
    t = batch * seq
    nh, hd = LA_HEADS, LA_D
    main_w = 4 * nh * hd
    h0 = x.reshape(t, d)

    w_in = a_w_in[0]
    w_main = w_in[:, 0:main_w].astype(BF16)
    w_gate = jnp.zeros((d, LANES), BF16).at[:, 0:2 * nh].set(w_in[:, main_w:main_w + 2 * nh].astype(BF16))
    proj, gates = norm_matmul(h0, [(a_norm[0], w_main, BF16), (a_norm[0], w_gate, F32)], 512, "gdn_in_proj")
    o = gdn_core(proj, gates, a_conv[0], a_log_decay[0], a_dt_bias[0], a_out_norm[0], batch, seq)
    h1 = matmul_residual(o, a_w_out[0].astype(BF16), h0, 1024, 1024, "gdn_out_proj")

    h2 = ffn_dense(h1, ffn_norm[0], dense_w_gate[0], dense_w_up[0], dense_w_down[0], 512, 512)

    kv, q = norm_matmul(h2, [(kv_norm, kv_w.astype(BF16), BF16), (b_norm[0], b_w_q[0].astype(BF16), BF16)],
                        1024, "qkv_proj")
    bias = bias_table(rel_bias)
    attn = swa_attention(q, kv, bias, q_norm[0], k_norm, b_sinks[0], batch, seq)
    h3 = matmul_residual(attn, b_w_o[0].astype(BF16), h2, 1024, 1024, "attn_out_proj")

    h4 = moe_layer(h3, ffn_norm[1], moe_router[0], moe_w_gate[0], moe_w_up[0], moe_w_down[0])
    return h4.reshape(batch, seq, d)
```

```python
import functools

import numpy as np
import jax
import jax.numpy as jnp
from jax import lax
from jax.experimental import pallas as pl
from jax.experimental.pallas import tpu as pltpu

F32 = jnp.float32
BF16 = jnp.bfloat16

EPS = 1e-6
NEG_INF = -1e30

LA_HEADS = 8
LA_D = 128
CONV_W = 4
CHUNK = 64
SW_HEADS = 16
SW_KV_HEADS = 4
SW_GROUP = SW_HEADS // SW_KV_HEADS
SW_HD = 64
WINDOW = 128
N_BUCKETS = 32
MAX_DIST = 128
N_EXPERTS = 8

LANES = 128
GDN_BLOCK = 2 * CHUNK
HALO = 16

VMEM_LIMIT = 56 * 1024 * 1024
EXPERT_VMEM_LIMIT = 60 * 1024 * 1024


def _cparams(sem):
    return pltpu.CompilerParams(dimension_semantics=sem, vmem_limit_bytes=VMEM_LIMIT)


def _silu(x):
    return x * (1.0 / (1.0 + jnp.exp(-x)))


def _dot(a, b):
    return jnp.dot(a, b, preferred_element_type=F32)


def _dot_nt(a, b):
    return lax.dot_general(a, b, (((1,), (1,)), ((), ())), preferred_element_type=F32)


def _norm_matmul_kernel(*refs, n_groups):
    x_ref = refs[0]
    g_refs = refs[1:1 + n_groups]
    w_refs = refs[1 + n_groups:1 + 2 * n_groups]
    o_refs = refs[1 + 2 * n_groups:1 + 3 * n_groups]
    x = x_ref[...]
    xr = x * lax.rsqrt(jnp.mean(x * x, axis=-1, keepdims=True) + EPS)
    for g_ref, w_ref, o_ref in zip(g_refs, w_refs, o_refs):
        o_ref[...] = _dot((xr * g_ref[...]).astype(BF16), w_ref[...]).astype(o_ref.dtype)


def norm_matmul(x, groups, tm, name):
    t, d = x.shape
    tm = min(tm, t)
    gains = [g.reshape(1, d).astype(F32) for g, _, _ in groups]
    ws = [w for _, w, _ in groups]
    return pl.pallas_call(
        functools.partial(_norm_matmul_kernel, n_groups=len(groups)),
        out_shape=[jax.ShapeDtypeStruct((t, w.shape[1]), dt) for _, w, dt in groups],
        grid=(t // tm,),
        in_specs=([pl.BlockSpec((tm, d), lambda i: (i, 0))]
                  + [pl.BlockSpec((1, d), lambda i: (0, 0)) for _ in groups]
                  + [pl.BlockSpec(w.shape, lambda i: (0, 0)) for w in ws]),
        out_specs=[pl.BlockSpec((tm, w.shape[1]), lambda i: (i, 0)) for w in ws],
        compiler_params=_cparams(("parallel",)),
        name=name,
    )(x, *gains, *ws)


def _gdn_kernel(proj_ref, gates_ref, convw_ref, hp_ref, onorm_ref, o_ref,
                xs_ref, state_ref, q_s, k_s, v_s, z_s, gc_s, gct_s, beta_s):
    n = pl.program_id(1)

    @pl.when(n == 0)
    def _():
        xs_ref[0:HALO, :] = jnp.zeros((HALO, xs_ref.shape[1]), xs_ref.dtype)
        for ref in (q_s, k_s, v_s, z_s, gc_s, gct_s, beta_s):
            ref[1] = jnp.zeros(ref.shape[1:], ref.dtype)

    @pl.when(n <= 1)
    def _():
        state_ref[...] = jnp.zeros_like(state_ref)

    args = (proj_ref, gates_ref, convw_ref, hp_ref, onorm_ref, o_ref, xs_ref, state_ref,
            q_s, k_s, v_s, z_s, gc_s, gct_s, beta_s)

    @pl.when(lax.rem(n, 2) == 0)
    def _():
        _gdn_step(*args, slot_w=0, slot_r=1)

    @pl.when(lax.rem(n, 2) == 1)
    def _():
        _gdn_step(*args, slot_w=1, slot_r=0)


def _gdn_step(proj_ref, gates_ref, convw_ref, hp_ref, onorm_ref, o_ref, xs_ref, state_ref,
              q_s, k_s, v_s, z_s, gc_s, gct_s, beta_s, *, slot_w, slot_r):
    nh, d, c = LA_HEADS, LA_D, CHUNK
    blk = GDN_BLOCK
    qkv_w = 3 * nh * d

    gc = gc_s[slot_r]
    gc_t = gct_s[slot_r]
    beta = beta_s[slot_r]

    xs_ref[HALO:HALO + blk, :] = proj_ref[:, 0:qkv_w]

    def front_gates():
        _gdn_front_gates(gates_ref, hp_ref, gc_s, gct_s, beta_s, slot_w)

    ci = lax.broadcasted_iota(jnp.int32, (c, c), 0)
    cj = lax.broadcasted_iota(jnp.int32, (c, c), 1)
    lower_incl = ci >= cj
    strict = ci > cj
    eye_c = jnp.where(ci == cj, 1.0, 0.0).astype(F32)
    di = lax.broadcasted_iota(jnp.int32, (d, d), 0)
    dj = lax.broadcasted_iota(jnp.int32, (d, d), 1)
    eye_d = jnp.where(di == dj, 1.0, 0.0).astype(BF16)

    onorm = onorm_ref[...]

    n_shift = CONV_W - 1
    sr = lax.broadcasted_iota(jnp.int32, (n_shift * blk, HALO + blk), 0)
    sc = lax.broadcasted_iota(jnp.int32, (n_shift * blk, HALO + blk), 1)
    shift_mat = jnp.where(sc == HALO + (sr % blk) - (sr // blk + 1), 1.0, 0.0).astype(BF16)
    pair_w = 2 * d

    def conv_silu(col0):
        cols = slice(col0, col0 + pair_w)
        shifted = _dot(shift_mat, xs_ref[:, cols])
        acc = convw_ref[CONV_W - 1:CONV_W, cols] * xs_ref[HALO:HALO + blk, cols].astype(F32)
        for s in range(1, CONV_W):
            acc = acc + convw_ref[CONV_W - 1 - s:CONV_W - s, cols] * shifted[(s - 1) * blk:s * blk]
        return _silu(acc)

    def front_pair(hp):
        c0 = hp * pair_w
        qf = conv_silu(c0)
        kf = conv_silu(nh * d + c0)
        v_s[slot_w, :, c0:c0 + pair_w] = conv_silu(2 * nh * d + c0)
        for half in range(2):
            lo, hi = half * d, (half + 1) * d
            qh, kh = qf[:, lo:hi], kf[:, lo:hi]
            q_s[slot_w, :, c0 + lo:c0 + hi] = qh * (lax.rsqrt(jnp.sum(qh * qh, axis=-1, keepdims=True) + EPS)
                                                    * (d ** -0.5))
            k_s[slot_w, :, c0 + lo:c0 + hi] = kh * lax.rsqrt(jnp.sum(kh * kh, axis=-1, keepdims=True) + EPS)
        z_s[slot_w, :, c0:c0 + pair_w] = proj_ref[:, qkv_w + c0:qkv_w + c0 + pair_w]

    front_tasks = [front_gates] + [functools.partial(front_pair, hp) for hp in range(nh // 2)]

    def run_front_task():
        if front_tasks:
            front_tasks.pop(0)()

    n_ck = blk // c
    chains = [(h, ck) for h in range(nh) for ck in range(n_ck)]

    st = {}
    for (h, ck) in chains:
        r = ck * c
        q = q_s[slot_r, r:r + c, h * d:(h + 1) * d]
        k = k_s[slot_r, r:r + c, h * d:(h + 1) * d]
        v = v_s[slot_r, r:r + c, h * d:(h + 1) * d]
        g_col = gc[r:r + c, nh + h:nh + h + 1]
        g_row = gc_t[nh + h:nh + h + 1, r:r + c]
        g_last = gc[r + c - 1:r + c, nh + h:nh + h + 1]
        b_col = beta[r:r + c, h:h + 1]
        decay = jnp.where(lower_incl, jnp.exp2(jnp.where(lower_incl, g_col - g_row, 0.0)), 0.0)
        k_beta = k * b_col
        e_col = jnp.exp2(g_col)
        lhs = jnp.concatenate([k_beta.astype(BF16), q.astype(BF16), eye_d], axis=0)
        kk = _dot_nt(lhs, k.astype(BF16))
        a_mat = jnp.where(strict, kk[0:c] * decay, 0.0)
        st[(h, ck)] = dict(
            a=a_mat, attn=(kk[c:2 * c] * decay).astype(BF16),
            k_tail_t=(kk[2 * c:2 * c + d] * jnp.exp2(g_last - g_row)).astype(BF16),
            rhs=jnp.concatenate([(v * b_col).astype(BF16), (k_beta * e_col).astype(BF16)], axis=1),
            qe=(q * e_col).astype(BF16), e_last=jnp.exp2(g_last))
    run_front_task()

    for key in chains:
        x_b = (-st[key]["a"]).astype(BF16)
        st[key]["y"] = _dot(x_b, x_b)
        st[key]["p"] = eye_c - st[key]["a"]
    run_front_task()
    n_levels = int(np.log2(c))
    for lvl in range(1, n_levels):
        for key in chains:
            y_b = st[key]["y"].astype(BF16)
            p = st[key]["p"]
            if lvl + 1 < n_levels:
                zz = _dot(jnp.concatenate([y_b, p.astype(BF16)], axis=0), y_b)
                st[key]["y"] = zz[0:c]
                st[key]["p"] = p + zz[c:2 * c]
            else:
                st[key]["p"] = p + _dot(p.astype(BF16), y_b)
        run_front_task()
    for key in chains:
        st[key]["uw"] = _dot(st[key]["p"].astype(BF16), st[key]["rhs"])
    run_front_task()

    for ck in range(n_ck):
        r = ck * c
        s_old = [state_ref[h] for h in range(nh)]
        ws_qs = []
        for h in range(nh):
            cur = st[(h, ck)]
            lhs = jnp.concatenate([cur["uw"][:, d:2 * d].astype(BF16), cur["qe"]], axis=0)
            ws_qs.append(_dot(lhs, s_old[h].astype(BF16)))
        run_front_task()
        for h in range(nh):
            cur = st[(h, ck)]
            v_new = cur["uw"][:, 0:d] - ws_qs[h][0:c]
            av_kv = _dot(jnp.concatenate([cur["attn"], cur["k_tail_t"]], axis=0), v_new.astype(BF16))
            state_ref[h] = s_old[h] * cur["e_last"] + av_kv[c:c + d]
            o = ws_qs[h][c:2 * c] + av_kv[0:c]
            o = (o * lax.rsqrt(jnp.mean(o * o, axis=-1, keepdims=True) + EPS)) * onorm
            z = z_s[slot_r, r:r + c, h * d:(h + 1) * d].astype(F32)
            o_ref[r:r + c, h * d:(h + 1) * d] = (o * _silu(z)).astype(o_ref.dtype)
    while front_tasks:
        run_front_task()

    xs_ref[0:HALO, :] = xs_ref[blk:blk + HALO, :]


def _gdn_front_gates(gates_ref, hp_ref, gc_s, gct_s, beta_s, slot_w):
    blk, c = GDN_BLOCK, CHUNK
    gates = gates_ref[...]
    a_log = hp_ref[0:1, :]
    dt_bias = hp_ref[1:2, :]
    beta = 1.0 / (1.0 + jnp.exp(-gates))
    sp_in = gates + dt_bias
    softplus = jnp.maximum(sp_in, 0.0) + jnp.log(1.0 + jnp.exp(-jnp.abs(sp_in)))
    g = (-jnp.exp(a_log) * softplus) * float(np.log2(np.e))

    row = lax.broadcasted_iota(jnp.int32, (blk, blk), 0)
    col = lax.broadcasted_iota(jnp.int32, (blk, blk), 1)
    tri = jnp.where((row >= col) & ((row // c) == (col // c)), 1.0, 0.0).astype(BF16)
    g_hi = g.astype(BF16)
    g_r1 = g - g_hi.astype(F32)
    g_mid = g_r1.astype(BF16)
    g_lo = (g_r1 - g_mid.astype(F32)).astype(BF16)
    gc = _dot(tri, g_hi) + _dot(tri, g_mid) + _dot(tri, g_lo)
    gc_s[slot_w] = gc
    gct_s[slot_w] = gc.T
    beta_s[slot_w] = beta


def gdn_core(proj, gates, conv_w, a_log, dt_bias, out_norm, batch, seq):
    t = proj.shape[0]
    nh, d = LA_HEADS, LA_D
    blk = GDN_BLOCK
    nblk = seq // blk
    hp = jnp.zeros((8, LANES), F32)
    hp = hp.at[0, nh:2 * nh].set(a_log.astype(F32)).at[1, nh:2 * nh].set(dt_bias.astype(F32))

    def in_map(b, n):
        return (b * nblk + jnp.minimum(n, nblk - 1), 0)

    return pl.pallas_call(
        _gdn_kernel,
        out_shape=jax.ShapeDtypeStruct((t, nh * d), BF16),
        grid=(batch, nblk + 1),
        in_specs=[pl.BlockSpec((blk, 4 * nh * d), in_map),
                  pl.BlockSpec((blk, LANES), in_map),
                  pl.BlockSpec((CONV_W, 3 * nh * d), lambda b, n: (0, 0)),
                  pl.BlockSpec((8, LANES), lambda b, n: (0, 0)),
                  pl.BlockSpec((1, d), lambda b, n: (0, 0))],
        out_specs=pl.BlockSpec((blk, nh * d), lambda b, n: (b * nblk + jnp.maximum(n - 1, 0), 0)),
        scratch_shapes=[pltpu.VMEM((HALO + blk, 3 * nh * d), BF16),
                        pltpu.VMEM((nh, d, d), F32),
                        pltpu.VMEM((2, blk, nh * d), F32), pltpu.VMEM((2, blk, nh * d), F32),
                        pltpu.VMEM((2, blk, nh * d), F32), pltpu.VMEM((2, blk, nh * d), BF16),
                        pltpu.VMEM((2, blk, LANES), F32), pltpu.VMEM((2, LANES, blk), F32),
                        pltpu.VMEM((2, blk, LANES), F32)],
        compiler_params=_cparams(("arbitrary", "arbitrary")),
        name="gdn_core",
    )(proj, gates, conv_w.astype(F32), hp, out_norm.reshape(1, d).astype(F32))


def _swiglu_kernel(te_ref, first_ref, nv_ref, x_ref, g_ref, a_ref, wp_ref, wg_hbm, wu_hbm, wd_hbm, o_ref,
                   wg_c, wu_c, wd_c, stage_in, stage_out, sems, xres_s, *, pre_norm, pre_proj, tf):
    i = pl.program_id(0)
    nf = wg_c.shape[0]
    e = te_ref[i]
    valid = i < nv_ref[0]

    def chunk_copies(j, slot):
        cols = pl.ds(j * tf, tf)
        return (pltpu.make_async_copy(wg_hbm.at[e, :, cols], stage_in.at[slot, 0], sems.at[slot, 0]),
                pltpu.make_async_copy(wu_hbm.at[e, :, cols], stage_in.at[slot, 1], sems.at[slot, 1]),
                pltpu.make_async_copy(wd_hbm.at[e, cols, :], stage_out.at[slot], sems.at[slot, 2]))

    def prepare_rows():
        x = x_ref[...].astype(F32)
        if pre_proj:
            x = x + _dot(a_ref[...], wp_ref[...])
            xres_s[...] = x
        if pre_norm:
            ms = jnp.mean(x * x, axis=-1, keepdims=True)
            x = (x * lax.rsqrt(ms + EPS)) * g_ref[...]
        return x.astype(BF16)

    def chunk(xb, j):
        hid = _silu(_dot(xb, wg_c[j])) * _dot(xb, wu_c[j])
        return _dot(hid.astype(BF16), wd_c[j])

    def finish(acc):
        if pre_norm:
            res = xres_s[...] if pre_proj else x_ref[...]
            o_ref[...] = (res + acc).astype(o_ref.dtype)
        else:
            o_ref[...] = acc.astype(o_ref.dtype)

    @pl.when(valid & (first_ref[i] == 1))
    def _():
        for c in chunk_copies(0, 0):
            c.start()
        xb = prepare_rows()
        acc = None
        for j in range(nf):
            slot = j % 2
            if j + 1 < nf:
                for c in chunk_copies(j + 1, 1 - slot):
                    c.start()
            for c in chunk_copies(j, slot):
                c.wait()
            wg_c[j] = stage_in[slot, 0].astype(BF16)
            wu_c[j] = stage_in[slot, 1].astype(BF16)
            wd_c[j] = stage_out[slot].astype(BF16)
            y = chunk(xb, j)
            acc = y if acc is None else acc + y
        finish(acc)

    @pl.when(valid & (first_ref[i] != 1))
    def _():
        xb = prepare_rows()
        acc = None
        for j in range(nf):
            y = chunk(xb, j)
            acc = y if acc is None else acc + y
        finish(acc)

    @pl.when(jnp.logical_not(valid))
    def _():
        o_ref[...] = jnp.zeros_like(o_ref)


def expert_swiglu(x, gain, tile_expert, tile_first, n_valid, wg, wu, wd, tile_rows, tf, out_dtype, pre_norm, name,
                  proj=None):
    n_rows, d = x.shape
    ne, _, f = wg.shape
    n_tiles = n_rows // tile_rows
    nf = f // tf
    pre_proj = proj is not None
    if pre_proj:
        a, wp = proj
        a_spec = pl.BlockSpec((tile_rows, a.shape[1]), lambda i, te, fi, nv: (jnp.minimum(i, nv[0] - 1), 0))
    else:
        a, wp = jnp.zeros((8, LANES), BF16), jnp.zeros((LANES, d), BF16)
        a_spec = pl.BlockSpec(a.shape, lambda i, te, fi, nv: (0, 0))
    grid_spec = pltpu.PrefetchScalarGridSpec(
        num_scalar_prefetch=3,
        grid=(n_tiles,),
        in_specs=[pl.BlockSpec((tile_rows, d), lambda i, te, fi, nv: (jnp.minimum(i, nv[0] - 1), 0)),
                  pl.BlockSpec((1, d), lambda i, te, fi, nv: (0, 0)),
                  a_spec,
                  pl.BlockSpec(wp.shape, lambda i, te, fi, nv: (0, 0)),
                  pl.BlockSpec(memory_space=pl.ANY),
                  pl.BlockSpec(memory_space=pl.ANY),
                  pl.BlockSpec(memory_space=pl.ANY)],
        out_specs=pl.BlockSpec((tile_rows, d), lambda i, te, fi, nv: (i, 0)),
        scratch_shapes=[pltpu.VMEM((nf, d, tf), BF16), pltpu.VMEM((nf, d, tf), BF16), pltpu.VMEM((nf, tf, d), BF16),
                        pltpu.VMEM((2, 2, d, tf), F32), pltpu.VMEM((2, tf, d), F32),
                        pltpu.SemaphoreType.DMA((2, 3)),
                        pltpu.VMEM((tile_rows, d) if pre_proj else (8, LANES), F32)],
    )
    return pl.pallas_call(
        functools.partial(_swiglu_kernel, pre_norm=pre_norm, pre_proj=pre_proj, tf=tf),
        out_shape=jax.ShapeDtypeStruct((n_rows, d), out_dtype),
        grid_spec=grid_spec,
        compiler_params=pltpu.CompilerParams(dimension_semantics=("arbitrary",), vmem_limit_bytes=EXPERT_VMEM_LIMIT),
        name=name,
    )(tile_expert, tile_first, n_valid, x, gain.reshape(1, d).astype(F32), a, wp, wg, wu, wd)


def ffn_dense(x, gain, wg, wu, wd, tm, tf, proj=None):
    t = x.shape[0]
    n_tiles = t // tm
    tile_first = jnp.zeros((n_tiles,), jnp.int32).at[0].set(1)
    return expert_swiglu(x, gain, jnp.zeros((n_tiles,), jnp.int32), tile_first, jnp.full((1,), n_tiles, jnp.int32),
                         wg[None], wu[None], wd[None], tm, tf, F32, True, "ffn_dense", proj=proj)


def _t5_bucket_np(dist):
    max_exact = N_BUCKETS // 2
    n = np.maximum(dist, 0)
    safe = np.maximum(n, 1).astype(np.float32)
    large = max_exact + (np.log(safe / max_exact) / np.log(MAX_DIST / max_exact)
                         * (N_BUCKETS - max_exact)).astype(np.int32)
    large = np.minimum(large, N_BUCKETS - 1)
    return np.where(n < max_exact, n, large).astype(np.int32)


LOG2E = float(np.log2(np.e))


def _bias_kernel(bucket_ref, valid_ref, rb_ref, o_ref):
    bucket = bucket_ref[...]
    for h in range(SW_HEADS):
        acc = jnp.zeros(bucket.shape, F32)
        for b in range(N_BUCKETS):
            acc = jnp.where(bucket == b, rb_ref[b, h], acc)
        for v in range(valid_ref.shape[0]):
            o_ref[v, h] = jnp.where(valid_ref[v] > 0, acc * LOG2E, NEG_INF)


def bias_table(rel_bias):
    qi = np.arange(WINDOW)[:, None] + WINDOW
    kj = np.arange(2 * WINDOW)[None, :]
    dist = qi - kj
    band = (dist >= 0) & (dist < WINDOW)
    valid = np.stack([band, band & (kj >= WINDOW)]).astype(np.int32)
    return pl.pallas_call(
        _bias_kernel,
        out_shape=jax.ShapeDtypeStruct((2, SW_HEADS, WINDOW, 2 * WINDOW), F32),
        in_specs=[pl.BlockSpec(memory_space=pltpu.VMEM), pl.BlockSpec(memory_space=pltpu.VMEM),
                  pl.BlockSpec(memory_space=pltpu.SMEM)],
        out_specs=pl.BlockSpec(memory_space=pltpu.VMEM),
        name="t5_bias_table",
    )(jnp.asarray(_t5_bucket_np(dist)), jnp.asarray(valid), rel_bias.astype(F32))


def _swa_kernel(q_ref, kvp_ref, kvc_ref, bias_ref, qn_ref, kn_ref, sink_ref, o_ref):
    blk, hd = WINDOW, SW_HD
    kv_w = SW_KV_HEADS * hd
    variant = jnp.where(pl.program_id(1) == 0, 1, 0)
    gw = 2 * LANES
    gi = lax.broadcasted_iota(jnp.int32, (gw, gw), 0)
    gj = lax.broadcasted_iota(jnp.int32, (gw, gw), 1)
    group_ones = jnp.where((gi // hd) == (gj // hd), 1.0, 0.0).astype(BF16)
    lane = lax.broadcasted_iota(jnp.int32, (1, LANES), 1)
    low_half = lane < hd

    def head_norm(x, gain):
        cols = []
        for c0 in range(0, x.shape[1], gw):
            xc = x[:, c0:c0 + gw]
            ss = _dot((xc * xc).astype(BF16), group_ones)
            cols.append(xc * lax.rsqrt(ss * (1.0 / hd) + EPS))
        return jnp.concatenate(cols, axis=1) * gain

    def dup_half(x, half):
        swapped = pltpu.roll(x, hd, 1)
        return jnp.where(low_half == (half == 0), x, swapped)

    qn = head_norm(q_ref[...].astype(F32), qn_ref[...]) * ((hd ** -0.5) * LOG2E)
    half_sel = [jnp.where(low_half, 1.0, 0.0), jnp.where(low_half, 0.0, 1.0)]
    k_all = jnp.concatenate([kvp_ref[:, 0:kv_w], kvc_ref[:, 0:kv_w]], axis=0).astype(F32)
    kn = head_norm(k_all, kn_ref[...])
    v_all = jnp.concatenate([kvp_ref[:, kv_w:2 * kv_w], kvc_ref[:, kv_w:2 * kv_w]], axis=0).astype(F32)
    ks, vs = [], []
    for g in range(SW_KV_HEADS):
        c0 = (g // 2) * LANES
        ks.append(dup_half(kn[:, c0:c0 + LANES], g % 2).astype(BF16))
        vs.append(dup_half(v_all[:, c0:c0 + LANES], g % 2).astype(BF16))

    scores = []
    for hq in range(SW_HEADS):
        c0 = (hq // 2) * LANES
        q_h = (qn[:, c0:c0 + LANES] * half_sel[hq % 2]).astype(BF16)
        scores.append(_dot_nt(q_h, ks[hq // SW_GROUP]))
    probs = []
    for hq in range(SW_HEADS):
        s = scores[hq] + bias_ref[variant, hq]
        sink = sink_ref[hq] * LOG2E
        mx = jnp.maximum(jnp.max(s, axis=-1, keepdims=True), sink)
        p = jnp.exp2(s - mx)
        denom = jnp.sum(p, axis=-1, keepdims=True) + jnp.exp2(sink - mx)
        probs.append((p / denom).astype(BF16))
    outs = [_dot(probs[hq], vs[hq // SW_GROUP]) for hq in range(SW_HEADS)]
    for c in range(SW_HEADS // 2):
        o_ref[:, c * LANES:(c + 1) * LANES] = jnp.where(low_half, outs[2 * c], outs[2 * c + 1]).astype(o_ref.dtype)


def swa_attention(q, kv, bias, q_norm, k_norm, sinks, batch, seq):
    t = q.shape[0]
    blk = WINDOW
    nb = seq // blk
    qw = SW_HEADS * SW_HD
    kvw = 2 * SW_KV_HEADS * SW_HD
    return pl.pallas_call(
        _swa_kernel,
        out_shape=jax.ShapeDtypeStruct((t, qw), BF16),
        grid=(batch, nb),
        in_specs=[pl.BlockSpec((blk, qw), lambda b, n: (b * nb + n, 0)),
                  pl.BlockSpec((blk, kvw), lambda b, n: (b * nb + jnp.maximum(n - 1, 0), 0)),
                  pl.BlockSpec((blk, kvw), lambda b, n: (b * nb + n, 0)),
                  pl.BlockSpec((2, SW_HEADS, blk, 2 * blk), lambda b, n: (0, 0, 0, 0)),
                  pl.BlockSpec((1, qw), lambda b, n: (0, 0)),
                  pl.BlockSpec((1, kvw // 2), lambda b, n: (0, 0)),
                  pl.BlockSpec(memory_space=pltpu.SMEM)],
        out_specs=pl.BlockSpec((blk, qw), lambda b, n: (b * nb + n, 0)),
        compiler_params=_cparams(("parallel", "parallel")),
        name="swa_attention",
    )(q, kv, kv, bias, jnp.tile(q_norm.astype(F32), SW_HEADS).reshape(1, qw),
      jnp.tile(k_norm.astype(F32), SW_KV_HEADS).reshape(1, kvw // 2), sinks.astype(F32))


def _route_kernel(x_ref, a_ref, wp_ref, g_ref, wr_ref, h_ref, r_ref, wt_ref, tab_ref, cnt_ref,
                  sel_s, gw_s, cnt_s, start_s, run_s, *, tile_rows):
    ne = N_EXPERTS
    p = pl.program_id(0)
    i = pl.program_id(1)
    tm = x_ref.shape[0]
    sub = lax.broadcasted_iota(jnp.int32, (ne, tm), 0).astype(F32)

    def seg_rows(sel):
        n = jnp.sum(sel, axis=1, keepdims=True)
        return jnp.floor((n + (SEG_ALIGN - 1)) * (1.0 / SEG_ALIGN)) * SEG_ALIGN

    def excl_cumsum_experts(v):
        sub8 = lax.broadcasted_iota(jnp.int32, v.shape, 0)
        out = jnp.zeros_like(v)
        for e in range(ne - 1):
            out = out + jnp.where(sub8 > e, v[e:e + 1, :], 0.0)
        return out

    @pl.when(p == 0)
    def _():
        @pl.when(i == 0)
        def _():
            cnt_s[...] = jnp.zeros_like(cnt_s)

        x = x_ref[...] + _dot(a_ref[...], wp_ref[...])
        h_ref[...] = x
        ms = jnp.mean(x * x, axis=-1, keepdims=True)
        xn32 = (x * lax.rsqrt(ms + EPS)) * g_ref[...]
        xn_hi = xn32.astype(BF16)
        xn_lo = (xn32 - xn_hi.astype(F32)).astype(BF16)
        p_hi = _dot_nt(wr_ref[...], xn_hi)
        p_lo = _dot_nt(wr_ref[...], xn_lo)
        logits = p_hi[0:ne] + p_hi[ne:2 * ne] + p_lo[0:ne]
        m1 = jnp.max(logits, axis=0, keepdims=True)
        i1 = jnp.min(jnp.where(logits == m1, sub, float(ne)), axis=0, keepdims=True)
        l2 = jnp.where(sub == i1, -jnp.inf, logits)
        m2 = jnp.max(l2, axis=0, keepdims=True)
        i2 = jnp.min(jnp.where(l2 == m2, sub, float(ne)), axis=0, keepdims=True)
        e2 = jnp.exp(m2 - m1)
        w1 = 1.0 / (1.0 + e2)
        w2 = e2 / (1.0 + e2)
        sel = jnp.where((sub == i1) | (sub == i2), 1.0, 0.0)
        sel_s[i] = sel
        gw_s[i] = jnp.where(sub == i1, w1, jnp.where(sub == i2, w2, 0.0))
        cnt_s[...] += seg_rows(sel)

    @pl.when(p == 1)
    def _():
        @pl.when(i == 0)
        def _():
            cnt = cnt_s[...]
            padded = jnp.floor((cnt + (tile_rows - 1)) * (1.0 / tile_rows)) * tile_rows
            start_s[...] = excl_cumsum_experts(padded)
            run_s[...] = jnp.zeros_like(run_s)
            cnt_ref[...] = cnt

        sel = sel_s[i]
        gw = gw_s[i]
        ti = lax.broadcasted_iota(jnp.int32, (tm, tm), 0)
        tj = lax.broadcasted_iota(jnp.int32, (tm, tm), 1)
        tri = jnp.where(ti <= tj, 1.0, 0.0).astype(BF16)
        csum = _dot(sel.astype(BF16), tri)
        seg = jnp.broadcast_to(seg_rows(sel), run_s.shape)
        local0 = excl_cumsum_experts(seg)
        tab_ref[0, 0] = start_s[...] + run_s[...]
        tab_ref[0, 1] = seg
        tab_ref[0, 2] = local0
        run_s[...] += seg
        local_row = local0[:, 0:1] + csum - sel
        ia = jnp.min(jnp.where(sel > 0.0, sub, float(ne)), axis=0, keepdims=True)
        ib = jnp.max(jnp.where(sel > 0.0, sub, -1.0), axis=0, keepdims=True)
        pick_a = sub == ia
        pick_b = sub == ib
        rows = [jnp.sum(jnp.where(pick_a, local_row, 0.0), axis=0, keepdims=True),
                jnp.sum(jnp.where(pick_b, local_row, 0.0), axis=0, keepdims=True),
                jnp.sum(jnp.where(pick_a, gw, 0.0), axis=0, keepdims=True),
                jnp.sum(jnp.where(pick_b, gw, 0.0), axis=0, keepdims=True)]
        r_ref[...] = jnp.concatenate(rows + [jnp.zeros((ne - 4, tm), F32)], axis=0)
        wpad = jnp.concatenate(rows[2:4] + rows[0:2] + [jnp.zeros((LANES - 4, tm), F32)], axis=0)
        wt_ref[...] = wpad.T


def moe_route(x, a, wp, gain, w_router, tm, tile_rows):
    t, d = x.shape
    ne = w_router.shape[1]
    assert ne == N_EXPERTS
    w_hi = w_router.astype(BF16)
    w_lo = (w_router - w_hi.astype(F32)).astype(BF16)
    wr = jnp.concatenate([w_hi.T, w_lo.T], axis=0)
    tm = min(tm, t)
    nt = t // tm

    def row_map(p, i):
        return (i * (1 - p) + (nt - 1) * p, 0)

    return pl.pallas_call(
        functools.partial(_route_kernel, tile_rows=tile_rows),
        out_shape=(jax.ShapeDtypeStruct((t, d), F32),
                   jax.ShapeDtypeStruct((ne, t), F32), jax.ShapeDtypeStruct((t, LANES), F32),
                   jax.ShapeDtypeStruct((nt, 3, ne, LANES), F32), jax.ShapeDtypeStruct((ne, LANES), F32)),
        grid=(2, nt),
        in_specs=[pl.BlockSpec((tm, d), row_map),
                  pl.BlockSpec((tm, a.shape[1]), row_map),
                  pl.BlockSpec(wp.shape, lambda p, i: (0, 0)),
                  pl.BlockSpec((1, d), lambda p, i: (0, 0)),
                  pl.BlockSpec((2 * ne, d), lambda p, i: (0, 0))],
        out_specs=(pl.BlockSpec((tm, d), row_map),
                   pl.BlockSpec((ne, tm), lambda p, i: (0, i * p)),
                   pl.BlockSpec((tm, LANES), lambda p, i: (i * p, 0)),
                   pl.BlockSpec((1, 3, ne, LANES), lambda p, i: (i * p, 0, 0, 0)),
                   pl.BlockSpec((ne, LANES), lambda p, i: (0, 0))),
        scratch_shapes=[pltpu.VMEM((nt, ne, tm), F32), pltpu.VMEM((nt, ne, tm), F32),
                        pltpu.VMEM((ne, LANES), F32), pltpu.VMEM((ne, LANES), F32), pltpu.VMEM((ne, LANES), F32)],
        compiler_params=_cparams(("arbitrary", "arbitrary")),
        name="moe_route",
    )(x, a, wp, gain.reshape(1, d), wr)


def _segment_copies(tab_ref, i, e, local_ref, slot_ref, sem, to_slots):
    base = (i * N_EXPERTS + e) * 3
    slot0, rows, local0 = tab_ref[base], tab_ref[base + 1], tab_ref[base + 2]
    out = []
    done = 0
    size = MOE_TOKEN_TILE
    while size >= SEG_ALIGN:
        take = rows & size
        loc = local_ref.at[pl.ds(pl.multiple_of(local0 + done, SEG_ALIGN), size)]
        slt = slot_ref.at[pl.ds(pl.multiple_of(slot0 + done, SEG_ALIGN), size)]
        desc = pltpu.make_async_copy(loc, slt, sem) if to_slots else pltpu.make_async_copy(slt, loc, sem)
        out.append((take != 0, desc))
        done = done + take
        size //= 2
    return out


def _run_segment_copies(tab_ref, tile, slot, rows_s, slot_ref, sems, to_slots, action):
    for e in range(N_EXPERTS):
        for cond, desc in _segment_copies(tab_ref, tile, e, rows_s.at[slot], slot_ref, sems.at[slot], to_slots):
            @pl.when(cond)
            def _():
                getattr(desc, action)()


def _dispatch_kernel(tab_ref, zf_ref, x_ref, g_ref, r_ref, xs_ref, rows_s, zero_s, sem, zsem, *, tile_rows):
    i = pl.program_id(0)
    tm = x_ref.shape[0]
    n_local = rows_s.shape[1]

    @pl.when(i == 0)
    def _():
        zero_s[...] = jnp.zeros_like(zero_s)

        def zero_copy(e):
            row0 = pl.multiple_of(zf_ref[e], tile_rows)
            return pltpu.make_async_copy(zero_s, xs_ref.at[pl.ds(row0, tile_rows)], zsem)

        for e in range(zf_ref.shape[0]):
            @pl.when(zf_ref[e] >= 0)
            def _():
                zero_copy(e).start()
        for e in range(zf_ref.shape[0]):
            @pl.when(zf_ref[e] >= 0)
            def _():
                zero_copy(e).wait()

    x = x_ref[...]
    ms = jnp.mean(x * x, axis=-1, keepdims=True)
    xn = ((x * lax.rsqrt(ms + EPS)) * g_ref[...]).astype(BF16)
    row_id = lax.broadcasted_iota(jnp.int32, (n_local, tm), 0).astype(F32)
    onehot = jnp.where((row_id == r_ref[0:1, :]) | (row_id == r_ref[1:2, :]), 1.0, 0.0).astype(BF16)
    slot = lax.rem(i, 2)
    rows_s[slot] = _dot(onehot, xn)

    _run_segment_copies(tab_ref, i, slot, rows_s, xs_ref, sem, True, "start")

    @pl.when(i > 0)
    def _():
        _run_segment_copies(tab_ref, i - 1, 1 - slot, rows_s, xs_ref, sem, True, "wait")

    @pl.when(i == pl.num_programs(0) - 1)
    def _():
        _run_segment_copies(tab_ref, i, slot, rows_s, xs_ref, sem, True, "wait")


def moe_dispatch(x, gain, r, tab, zf_rows, n_slots, tm, tile_rows):
    t, d = x.shape
    nt = t // tm
    n_local = TOP_K * tm + N_EXPERTS * SEG_ALIGN
    grid_spec = pltpu.PrefetchScalarGridSpec(
        num_scalar_prefetch=2,
        grid=(nt,),
        in_specs=[pl.BlockSpec((tm, d), lambda i, tb, zf: (i, 0)),
                  pl.BlockSpec((1, d), lambda i, tb, zf: (0, 0)),
                  pl.BlockSpec((N_EXPERTS, tm), lambda i, tb, zf: (0, i))],
        out_specs=pl.BlockSpec(memory_space=pl.ANY),
        scratch_shapes=[pltpu.VMEM((2, n_local, d), F32), pltpu.VMEM((tile_rows, d), F32),
                        pltpu.SemaphoreType.DMA((2,)), pltpu.SemaphoreType.DMA],
    )
    return pl.pallas_call(
        functools.partial(_dispatch_kernel, tile_rows=tile_rows),
        out_shape=jax.ShapeDtypeStruct((n_slots, d), F32),
        grid_spec=grid_spec,
        compiler_params=_cparams(("arbitrary",)),
        name="moe_dispatch",
    )(tab, zf_rows, x, gain.reshape(1, d), r)


def _combine_kernel(tab_ref, h_ref, wt_ref, ys_ref, o_ref, rows_s, sems):
    i = pl.program_id(0)
    tm = h_ref.shape[0]
    n_local = rows_s.shape[1]
    slot = lax.rem(i, 2)

    def fetch(tile, into):
        rows_s[into] = jnp.zeros(rows_s.shape[1:], rows_s.dtype)
        _run_segment_copies(tab_ref, tile, into, rows_s, ys_ref, sems, False, "start")

    @pl.when(i == 0)
    def _():
        fetch(i, slot)

    @pl.when(i + 1 < pl.num_programs(0))
    def _():
        fetch(i + 1, 1 - slot)

    _run_segment_copies(tab_ref, i, slot, rows_s, ys_ref, sems, False, "wait")

    wt = wt_ref[...]
    y = rows_s[slot].astype(BF16)
    col_id = lax.broadcasted_iota(jnp.int32, (tm, n_local), 1).astype(F32)
    pick_a = jnp.where(col_id == wt[:, 2:3], 1.0, 0.0).astype(BF16)
    pick_b = jnp.where(col_id == wt[:, 3:4], 1.0, 0.0).astype(BF16)
    o_ref[...] = h_ref[...] + wt[:, 0:1] * _dot(pick_a, y) + wt[:, 1:2] * _dot(pick_b, y)


def moe_combine(h, wt, tab, ys, tm):
    t, d = h.shape
    nt = t // tm
    n_local = TOP_K * tm + N_EXPERTS * SEG_ALIGN
    grid_spec = pltpu.PrefetchScalarGridSpec(
        num_scalar_prefetch=1,
        grid=(nt,),
        in_specs=[pl.BlockSpec((tm, d), lambda i, tb: (i, 0)),
                  pl.BlockSpec((tm, LANES), lambda i, tb: (i, 0)),
                  pl.BlockSpec(memory_space=pl.ANY)],
        out_specs=pl.BlockSpec((tm, d), lambda i, tb: (i, 0)),
        scratch_shapes=[pltpu.VMEM((2, n_local, d), F32), pltpu.SemaphoreType.DMA((2,))],
    )
    return pl.pallas_call(
        _combine_kernel,
        out_shape=jax.ShapeDtypeStruct((t, d), F32),
        grid_spec=grid_spec,
        compiler_params=_cparams(("arbitrary",)),
        name="moe_combine",
    )(tab, h, wt, ys)


MOE_TILE_ROWS = 512
MOE_TOKEN_TILE = 512
SEG_ALIGN = 8
TOP_K = 2


def moe_layer(x, a, wp, gain, w_router, wg, wu, wd):
    t, d = x.shape
    ne = w_router.shape[1]
    tr, tm = MOE_TILE_ROWS, MOE_TOKEN_TILE
    nt = t // tm
    n_tiles = -(-(TOP_K * t + nt * ne * (SEG_ALIGN - 1) + ne * (tr - 1)) // tr)
    n_slots = n_tiles * tr

    h, r, wt, tab, cnt = moe_route(x, a, wp, gain, w_router, tm, tr)
    tab = jnp.transpose(tab[:, :, :, 0], (0, 2, 1)).astype(jnp.int32).reshape(-1)

    counts = cnt[:, 0].astype(jnp.int32)
    padded = ((counts + (tr - 1)) // tr) * tr
    ends = jnp.cumsum(padded)
    n_valid = (ends[-1] // tr).astype(jnp.int32)
    tile_row0 = jnp.arange(n_tiles, dtype=jnp.int32) * tr
    tile_expert = jnp.sum((tile_row0[:, None] >= ends[None, :]).astype(jnp.int32), axis=1)
    tile_expert = jnp.minimum(tile_expert, ne - 1)
    tile_expert = jnp.where(jnp.arange(n_tiles) < n_valid, tile_expert, tile_expert[jnp.maximum(n_valid - 1, 0)])
    prev_expert = jnp.concatenate([jnp.full((1,), -1, jnp.int32), tile_expert[:-1]])
    tile_first = (tile_expert != prev_expert).astype(jnp.int32)
    tail = jnp.arange(TOP_K * t // tr, n_tiles, dtype=jnp.int32)
    zf_rows = jnp.concatenate([jnp.where(padded > 0, ends - tr, -1),
                               jnp.where(tail >= n_valid, tail * tr, -1)]).astype(jnp.int32)

    xs = moe_dispatch(h, gain, r, tab, zf_rows, n_slots, tm, tr)
    ys = expert_swiglu(xs, gain, tile_expert, tile_first, n_valid.reshape(1), wg, wu, wd, tr, 512, F32, False,
                       "moe_experts")
    return moe_combine(h, wt, tab, ys, tm)


def kernel(x, a_norm, a_w_in, a_conv, a_log_decay, a_dt_bias, a_out_norm, a_w_out, kv_norm, kv_w, k_norm,
           b_norm, b_w_q, q_norm, b_sinks, b_w_o, rel_bias, ffn_norm, dense_w_gate, dense_w_up, dense_w_down,
           moe_router, moe_w_gate, moe_w_up, moe_w_down):
    batch, seq, d = x.shape
    t = batch * seq
    nh, hd = LA_HEADS, LA_D
    main_w = 4 * nh * hd
    h0 = x.reshape(t, d)

    w_in = a_w_in[0]
    w_main = w_in[:, 0:main_w].astype(BF16)
    w_gate = jnp.zeros((d, LANES), BF16).at[:, 0:2 * nh].set(w_in[:, main_w:main_w + 2 * nh].astype(BF16))
    proj, gates = norm_matmul(h0, [(a_norm[0], w_main, BF16), (a_norm[0], w_gate, F32)], 512, "gdn_in_proj")
    o = gdn_core(proj, gates, a_conv[0], a_log_decay[0], a_dt_bias[0], a_out_norm[0], batch, seq)

    h2 = ffn_dense(h0, ffn_norm[0], dense_w_gate[0], dense_w_up[0], dense_w_down[0], 512, 512,
                   proj=(o, a_w_out[0].astype(BF16)))

    kv, q = norm_matmul(h2, [(kv_norm, kv_w.astype(BF16), BF16), (b_norm[0], b_w_q[0].astype(BF16), BF16)],
                        1024, "qkv_proj")
    bias = bias_table(rel_bias)
    attn = swa_attention(q, kv, bias, q_norm[0], k_norm, b_sinks[0], batch, seq)

    h4 = moe_layer(h2, attn, b_w_o[0].astype(BF16), ffn_norm[1], moe_router[0], moe_w_gate[0], moe_w_up[0],
                   moe_w_down[0])
    return h4.reshape(batch, seq, d)
```

```python
import functools

import numpy as np
import jax
import jax.numpy as jnp
from jax import lax
from jax.experimental import pallas as pl
from jax.experimental.pallas import tpu as pltpu

F32 = jnp.float32
BF16 = jnp.bfloat16

EPS = 1e-6
NEG_INF = -1e30

LA_HEADS = 8
LA_D = 128
CONV_W = 4
CHUNK = 64
SW_HEADS = 16
SW_KV_HEADS = 4
SW_GROUP = SW_HEADS // SW_KV_HEADS
SW_HD = 64
WINDOW = 128
N_BUCKETS = 32
MAX_DIST = 128
N_EXPERTS = 8

LANES = 128
GDN_BLOCK = 2 * CHUNK
HALO = 16

VMEM_LIMIT = 56 * 1024 * 1024
EXPERT_VMEM_LIMIT = 60 * 1024 * 1024


def _cparams(sem):
    return pltpu.CompilerParams(dimension_semantics=sem, vmem_limit_bytes=VMEM_LIMIT)


def _silu(x):
    return x * (1.0 / (1.0 + jnp.exp(-x)))


def _dot(a, b):
    return jnp.dot(a, b, preferred_element_type=F32)


def _dot_nt(a, b):
    return lax.dot_general(a, b, (((1,), (1,)), ((), ())), preferred_element_type=F32)


def _norm_matmul_kernel(*refs, n_groups):
    x_ref = refs[0]
    g_refs = refs[1:1 + n_groups]
    w_refs = refs[1 + n_groups:1 + 2 * n_groups]
    o_refs = refs[1 + 2 * n_groups:1 + 3 * n_groups]
    x = x_ref[...]
    xr = x * lax.rsqrt(jnp.mean(x * x, axis=-1, keepdims=True) + EPS)
    for g_ref, w_ref, o_ref in zip(g_refs, w_refs, o_refs):
        o_ref[...] = _dot((xr * g_ref[...]).astype(BF16), w_ref[...]).astype(o_ref.dtype)


def norm_matmul(x, groups, tm, name):
    t, d = x.shape
    tm = min(tm, t)
    gains = [g.reshape(1, d).astype(F32) for g, _, _ in groups]
    ws = [w for _, w, _ in groups]
    return pl.pallas_call(
        functools.partial(_norm_matmul_kernel, n_groups=len(groups)),
        out_shape=[jax.ShapeDtypeStruct((t, w.shape[1]), dt) for _, w, dt in groups],
        grid=(t // tm,),
        in_specs=([pl.BlockSpec((tm, d), lambda i: (i, 0))]
                  + [pl.BlockSpec((1, d), lambda i: (0, 0)) for _ in groups]
                  + [pl.BlockSpec(w.shape, lambda i: (0, 0)) for w in ws]),
        out_specs=[pl.BlockSpec((tm, w.shape[1]), lambda i: (i, 0)) for w in ws],
        compiler_params=_cparams(("parallel",)),
        name=name,
    )(x, *gains, *ws)


def _gdn_kernel(proj_ref, gates_ref, convw_ref, hp_ref, onorm_ref, o_ref,
                xs_ref, state_ref, q_s, k_s, v_s, z_s, gc_s, gct_s, beta_s):
    n = pl.program_id(1)

    @pl.when(n == 0)
    def _():
        xs_ref[0:HALO, :] = jnp.zeros((HALO, xs_ref.shape[1]), xs_ref.dtype)
        for ref in (q_s, k_s, v_s, z_s, gc_s, gct_s, beta_s):
            ref[1] = jnp.zeros(ref.shape[1:], ref.dtype)

    @pl.when(n <= 1)
    def _():
        state_ref[...] = jnp.zeros_like(state_ref)

    args = (proj_ref, gates_ref, convw_ref, hp_ref, onorm_ref, o_ref, xs_ref, state_ref,
            q_s, k_s, v_s, z_s, gc_s, gct_s, beta_s)

    @pl.when(lax.rem(n, 2) == 0)
    def _():
        _gdn_step(*args, slot_w=0, slot_r=1)

    @pl.when(lax.rem(n, 2) == 1)
    def _():
        _gdn_step(*args, slot_w=1, slot_r=0)


def _gdn_step(proj_ref, gates_ref, convw_ref, hp_ref, onorm_ref, o_ref, xs_ref, state_ref,
              q_s, k_s, v_s, z_s, gc_s, gct_s, beta_s, *, slot_w, slot_r):
    nh, d, c = LA_HEADS, LA_D, CHUNK
    blk = GDN_BLOCK
    qkv_w = 3 * nh * d

    gc = gc_s[slot_r]
    gc_t = gct_s[slot_r]
    beta = beta_s[slot_r]

    xs_ref[HALO:HALO + blk, :] = proj_ref[:, 0:qkv_w]

    def front_gates():
        _gdn_front_gates(gates_ref, hp_ref, gc_s, gct_s, beta_s, slot_w)

    ci = lax.broadcasted_iota(jnp.int32, (c, c), 0)
    cj = lax.broadcasted_iota(jnp.int32, (c, c), 1)
    lower_incl = ci >= cj
    strict = ci > cj
    eye_c = jnp.where(ci == cj, 1.0, 0.0).astype(F32)
    di = lax.broadcasted_iota(jnp.int32, (d, d), 0)
    dj = lax.broadcasted_iota(jnp.int32, (d, d), 1)
    eye_d = jnp.where(di == dj, 1.0, 0.0).astype(BF16)

    onorm = onorm_ref[...]

    n_shift = CONV_W - 1
    sr = lax.broadcasted_iota(jnp.int32, (n_shift * blk, HALO + blk), 0)
    sc = lax.broadcasted_iota(jnp.int32, (n_shift * blk, HALO + blk), 1)
    shift_mat = jnp.where(sc == HALO + (sr % blk) - (sr // blk + 1), 1.0, 0.0).astype(BF16)
    pair_w = 2 * d

    def conv_silu(col0):
        cols = slice(col0, col0 + pair_w)
        shifted = _dot(shift_mat, xs_ref[:, cols])
        acc = convw_ref[CONV_W - 1:CONV_W, cols] * xs_ref[HALO:HALO + blk, cols].astype(F32)
        for s in range(1, CONV_W):
            acc = acc + convw_ref[CONV_W - 1 - s:CONV_W - s, cols] * shifted[(s - 1) * blk:s * blk]
        return _silu(acc)

    def front_pair(hp):
        c0 = hp * pair_w
        qf = conv_silu(c0)
        kf = conv_silu(nh * d + c0)
        v_s[slot_w, :, c0:c0 + pair_w] = conv_silu(2 * nh * d + c0)
        for half in range(2):
            lo, hi = half * d, (half + 1) * d
            qh, kh = qf[:, lo:hi], kf[:, lo:hi]
            q_s[slot_w, :, c0 + lo:c0 + hi] = qh * (lax.rsqrt(jnp.sum(qh * qh, axis=-1, keepdims=True) + EPS)
                                                    * (d ** -0.5))
            k_s[slot_w, :, c0 + lo:c0 + hi] = kh * lax.rsqrt(jnp.sum(kh * kh, axis=-1, keepdims=True) + EPS)
        z_s[slot_w, :, c0:c0 + pair_w] = proj_ref[:, qkv_w + c0:qkv_w + c0 + pair_w]

    front_tasks = [front_gates] + [functools.partial(front_pair, hp) for hp in range(nh // 2)]

    def run_front_task():
        if front_tasks:
            front_tasks.pop(0)()

    n_ck = blk // c
    chains = [(h, ck) for h in range(nh) for ck in range(n_ck)]

    st = {}
    for (h, ck) in chains:
        r = ck * c
        q = q_s[slot_r, r:r + c, h * d:(h + 1) * d]
        k = k_s[slot_r, r:r + c, h * d:(h + 1) * d]
        v = v_s[slot_r, r:r + c, h * d:(h + 1) * d]
        g_col = gc[r:r + c, nh + h:nh + h + 1]
        g_row = gc_t[nh + h:nh + h + 1, r:r + c]
        g_last = gc[r + c - 1:r + c, nh + h:nh + h + 1]
        b_col = beta[r:r + c, h:h + 1]
        decay = jnp.where(lower_incl, jnp.exp2(jnp.where(lower_incl, g_col - g_row, 0.0)), 0.0)
        k_beta = k * b_col
        e_col = jnp.exp2(g_col)
        lhs = jnp.concatenate([k_beta.astype(BF16), q.astype(BF16), eye_d], axis=0)
        kk = _dot_nt(lhs, k.astype(BF16))
        a_mat = jnp.where(strict, kk[0:c] * decay, 0.0)
        st[(h, ck)] = dict(
            a=a_mat, attn=(kk[c:2 * c] * decay).astype(BF16),
            k_tail_t=(kk[2 * c:2 * c + d] * jnp.exp2(g_last - g_row)).astype(BF16),
            rhs=jnp.concatenate([(v * b_col).astype(BF16), (k_beta * e_col).astype(BF16)], axis=1),
            qe=(q * e_col).astype(BF16), e_last=jnp.exp2(g_last))
    run_front_task()

    for key in chains:
        x_b = (-st[key]["a"]).astype(BF16)
        st[key]["y"] = _dot(x_b, x_b)
        st[key]["p"] = eye_c - st[key]["a"]
    run_front_task()
    n_levels = int(np.log2(c))
    for lvl in range(1, n_levels):
        for key in chains:
            y_b = st[key]["y"].astype(BF16)
            p = st[key]["p"]
            if lvl + 1 < n_levels:
                zz = _dot(jnp.concatenate([y_b, p.astype(BF16)], axis=0), y_b)
                st[key]["y"] = zz[0:c]
                st[key]["p"] = p + zz[c:2 * c]
            else:
                st[key]["p"] = p + _dot(p.astype(BF16), y_b)
        run_front_task()
    for key in chains:
        st[key]["uw"] = _dot(st[key]["p"].astype(BF16), st[key]["rhs"])
    run_front_task()

    for ck in range(n_ck):
        r = ck * c
        s_old = [state_ref[h] for h in range(nh)]
        ws_qs = []
        for h in range(nh):
            cur = st[(h, ck)]
            lhs = jnp.concatenate([cur["uw"][:, d:2 * d].astype(BF16), cur["qe"]], axis=0)
            ws_qs.append(_dot(lhs, s_old[h].astype(BF16)))
        run_front_task()
        for h in range(nh):
            cur = st[(h, ck)]
            v_new = cur["uw"][:, 0:d] - ws_qs[h][0:c]
            av_kv = _dot(jnp.concatenate([cur["attn"], cur["k_tail_t"]], axis=0), v_new.astype(BF16))
            state_ref[h] = s_old[h] * cur["e_last"] + av_kv[c:c + d]
            o = ws_qs[h][c:2 * c] + av_kv[0:c]
            o = (o * lax.rsqrt(jnp.mean(o * o, axis=-1, keepdims=True) + EPS)) * onorm
            z = z_s[slot_r, r:r + c, h * d:(h + 1) * d].astype(F32)
            o_ref[r:r + c, h * d:(h + 1) * d] = (o * _silu(z)).astype(o_ref.dtype)
    while front_tasks:
        run_front_task()

    xs_ref[0:HALO, :] = xs_ref[blk:blk + HALO, :]


def _gdn_front_gates(gates_ref, hp_ref, gc_s, gct_s, beta_s, slot_w):
    blk, c = GDN_BLOCK, CHUNK
    gates = gates_ref[...]
    a_log = hp_ref[0:1, :]
    dt_bias = hp_ref[1:2, :]
    beta = 1.0 / (1.0 + jnp.exp(-gates))
    sp_in = gates + dt_bias
    softplus = jnp.maximum(sp_in, 0.0) + jnp.log(1.0 + jnp.exp(-jnp.abs(sp_in)))
    g = (-jnp.exp(a_log) * softplus) * float(np.log2(np.e))

    row = lax.broadcasted_iota(jnp.int32, (blk, blk), 0)
    col = lax.broadcasted_iota(jnp.int32, (blk, blk), 1)
    tri = jnp.where((row >= col) & ((row // c) == (col // c)), 1.0, 0.0).astype(BF16)
    g_hi = g.astype(BF16)
    g_r1 = g - g_hi.astype(F32)
    g_mid = g_r1.astype(BF16)
    g_lo = (g_r1 - g_mid.astype(F32)).astype(BF16)
    gc = _dot(tri, g_hi) + _dot(tri, g_mid) + _dot(tri, g_lo)
    gc_s[slot_w] = gc
    gct_s[slot_w] = gc.T
    beta_s[slot_w] = beta


def gdn_core(proj, gates, conv_w, a_log, dt_bias, out_norm, batch, seq):
    t = proj.shape[0]
    nh, d = LA_HEADS, LA_D
    blk = GDN_BLOCK
    nblk = seq // blk
    hp = jnp.zeros((8, LANES), F32)
    hp = hp.at[0, nh:2 * nh].set(a_log.astype(F32)).at[1, nh:2 * nh].set(dt_bias.astype(F32))

    def in_map(b, n):
        return (b * nblk + jnp.minimum(n, nblk - 1), 0)

    return pl.pallas_call(
        _gdn_kernel,
        out_shape=jax.ShapeDtypeStruct((t, nh * d), BF16),
        grid=(batch, nblk + 1),
        in_specs=[pl.BlockSpec((blk, 4 * nh * d), in_map),
                  pl.BlockSpec((blk, LANES), in_map),
                  pl.BlockSpec((CONV_W, 3 * nh * d), lambda b, n: (0, 0)),
                  pl.BlockSpec((8, LANES), lambda b, n: (0, 0)),
                  pl.BlockSpec((1, d), lambda b, n: (0, 0))],
        out_specs=pl.BlockSpec((blk, nh * d), lambda b, n: (b * nblk + jnp.maximum(n - 1, 0), 0)),
        scratch_shapes=[pltpu.VMEM((HALO + blk, 3 * nh * d), BF16),
                        pltpu.VMEM((nh, d, d), F32),
                        pltpu.VMEM((2, blk, nh * d), F32), pltpu.VMEM((2, blk, nh * d), F32),
                        pltpu.VMEM((2, blk, nh * d), F32), pltpu.VMEM((2, blk, nh * d), BF16),
                        pltpu.VMEM((2, blk, LANES), F32), pltpu.VMEM((2, LANES, blk), F32),
                        pltpu.VMEM((2, blk, LANES), F32)],
        compiler_params=_cparams(("arbitrary", "arbitrary")),
        name="gdn_core",
    )(proj, gates, conv_w.astype(F32), hp, out_norm.reshape(1, d).astype(F32))


def _swiglu_kernel(te_ref, first_ref, nv_ref, x_ref, g_ref, a_ref, wp_ref, wg_hbm, wu_hbm, wd_hbm, o_ref,
                   wg_c, wu_c, wd_c, stage_in, stage_out, sems, xres_s, *, pre_norm, pre_proj, tf):
    i = pl.program_id(0)
    nf = wg_c.shape[0]
    e = te_ref[i]
    valid = i < nv_ref[0]

    def chunk_copies(j, slot):
        cols = pl.ds(j * tf, tf)
        return (pltpu.make_async_copy(wg_hbm.at[e, :, cols], stage_in.at[slot, 0], sems.at[slot, 0]),
                pltpu.make_async_copy(wu_hbm.at[e, :, cols], stage_in.at[slot, 1], sems.at[slot, 1]),
                pltpu.make_async_copy(wd_hbm.at[e, cols, :], stage_out.at[slot], sems.at[slot, 2]))

    def prepare_rows():
        x = x_ref[...].astype(F32)
        if pre_proj:
            x = x + _dot(a_ref[...], wp_ref[...])
            xres_s[...] = x
        if pre_norm:
            ms = jnp.mean(x * x, axis=-1, keepdims=True)
            x = (x * lax.rsqrt(ms + EPS)) * g_ref[...]
        return x.astype(BF16)

    def chunk(xb, j):
        hid = _silu(_dot(xb, wg_c[j])) * _dot(xb, wu_c[j])
        return _dot(hid.astype(BF16), wd_c[j])

    def finish(acc):
        if pre_norm:
            res = xres_s[...] if pre_proj else x_ref[...]
            o_ref[...] = (res + acc).astype(o_ref.dtype)
        else:
            o_ref[...] = acc.astype(o_ref.dtype)

    @pl.when(valid & (first_ref[i] == 1))
    def _():
        for c in chunk_copies(0, 0):
            c.start()
        xb = prepare_rows()
        acc = None
        for j in range(nf):
            slot = j % 2
            if j + 1 < nf:
                for c in chunk_copies(j + 1, 1 - slot):
                    c.start()
            for c in chunk_copies(j, slot):
                c.wait()
            wg_c[j] = stage_in[slot, 0].astype(BF16)
            wu_c[j] = stage_in[slot, 1].astype(BF16)
            wd_c[j] = stage_out[slot].astype(BF16)
            y = chunk(xb, j)
            acc = y if acc is None else acc + y
        finish(acc)

    @pl.when(valid & (first_ref[i] != 1))
    def _():
        xb = prepare_rows()
        acc = None
        for j in range(nf):
            y = chunk(xb, j)
            acc = y if acc is None else acc + y
        finish(acc)

    @pl.when(jnp.logical_not(valid))
    def _():
        o_ref[...] = jnp.zeros_like(o_ref)


def expert_swiglu(x, gain, tile_expert, tile_first, n_valid, wg, wu, wd, tile_rows, tf, out_dtype, pre_norm, name,
                  proj=None):
    n_rows, d = x.shape
    ne, _, f = wg.shape
    n_tiles = n_rows // tile_rows
    nf = f // tf
    pre_proj = proj is not None
    if pre_proj:
        a, wp = proj
        a_spec = pl.BlockSpec((tile_rows, a.shape[1]), lambda i, te, fi, nv: (jnp.minimum(i, nv[0] - 1), 0))
    else:
        a, wp = jnp.zeros((8, LANES), BF16), jnp.zeros((LANES, d), BF16)
        a_spec = pl.BlockSpec(a.shape, lambda i, te, fi, nv: (0, 0))
    grid_spec = pltpu.PrefetchScalarGridSpec(
        num_scalar_prefetch=3,
        grid=(n_tiles,),
        in_specs=[pl.BlockSpec((tile_rows, d), lambda i, te, fi, nv: (jnp.minimum(i, nv[0] - 1), 0)),
                  pl.BlockSpec((1, d), lambda i, te, fi, nv: (0, 0)),
                  a_spec,
                  pl.BlockSpec(wp.shape, lambda i, te, fi, nv: (0, 0)),
                  pl.BlockSpec(memory_space=pl.ANY),
                  pl.BlockSpec(memory_space=pl.ANY),
                  pl.BlockSpec(memory_space=pl.ANY)],
        out_specs=pl.BlockSpec((tile_rows, d), lambda i, te, fi, nv: (i, 0)),
        scratch_shapes=[pltpu.VMEM((nf, d, tf), BF16), pltpu.VMEM((nf, d, tf), BF16), pltpu.VMEM((nf, tf, d), BF16),
                        pltpu.VMEM((2, 2, d, tf), F32), pltpu.VMEM((2, tf, d), F32),
                        pltpu.SemaphoreType.DMA((2, 3)),
                        pltpu.VMEM((tile_rows, d) if pre_proj else (8, LANES), F32)],
    )
    return pl.pallas_call(
        functools.partial(_swiglu_kernel, pre_norm=pre_norm, pre_proj=pre_proj, tf=tf),
        out_shape=jax.ShapeDtypeStruct((n_rows, d), out_dtype),
        grid_spec=grid_spec,
        compiler_params=pltpu.CompilerParams(dimension_semantics=("arbitrary",), vmem_limit_bytes=EXPERT_VMEM_LIMIT),
        name=name,
    )(tile_expert, tile_first, n_valid, x, gain.reshape(1, d).astype(F32), a, wp, wg, wu, wd)


def ffn_dense(x, gain, wg, wu, wd, tm, tf, proj=None):
    t = x.shape[0]
    n_tiles = t // tm
    tile_first = jnp.zeros((n_tiles,), jnp.int32).at[0].set(1)
    return expert_swiglu(x, gain, jnp.zeros((n_tiles,), jnp.int32), tile_first, jnp.full((1,), n_tiles, jnp.int32),
                         wg[None], wu[None], wd[None], tm, tf, F32, True, "ffn_dense", proj=proj)


def _t5_bucket_np(dist):
    max_exact = N_BUCKETS // 2
    n = np.maximum(dist, 0)
    safe = np.maximum(n, 1).astype(np.float32)
    large = max_exact + (np.log(safe / max_exact) / np.log(MAX_DIST / max_exact)
                         * (N_BUCKETS - max_exact)).astype(np.int32)
    large = np.minimum(large, N_BUCKETS - 1)
    return np.where(n < max_exact, n, large).astype(np.int32)


LOG2E = float(np.log2(np.e))


def _bias_kernel(bucket_ref, valid_ref, rb_ref, o_ref):
    bucket = bucket_ref[...]
    for h in range(SW_HEADS):
        acc = jnp.zeros(bucket.shape, F32)
        for b in range(N_BUCKETS):
            acc = jnp.where(bucket == b, rb_ref[b, h], acc)
        for v in range(valid_ref.shape[0]):
            o_ref[v, h] = jnp.where(valid_ref[v] > 0, acc * LOG2E, NEG_INF)


def bias_table(rel_bias):
    qi = np.arange(WINDOW)[:, None] + WINDOW
    kj = np.arange(2 * WINDOW)[None, :]
    dist = qi - kj
    band = (dist >= 0) & (dist < WINDOW)
    valid = np.stack([band, band & (kj >= WINDOW)]).astype(np.int32)
    return pl.pallas_call(
        _bias_kernel,
        out_shape=jax.ShapeDtypeStruct((2, SW_HEADS, WINDOW, 2 * WINDOW), F32),
        in_specs=[pl.BlockSpec(memory_space=pltpu.VMEM), pl.BlockSpec(memory_space=pltpu.VMEM),
                  pl.BlockSpec(memory_space=pltpu.SMEM)],
        out_specs=pl.BlockSpec(memory_space=pltpu.VMEM),
        name="t5_bias_table",
    )(jnp.asarray(_t5_bucket_np(dist)), jnp.asarray(valid), rel_bias.astype(F32))


def _swa_kernel(q_ref, kvp_ref, kvc_ref, bias_ref, qn_ref, kn_ref, sink_ref, o_ref):
    blk, hd = WINDOW, SW_HD
    kv_w = SW_KV_HEADS * hd
    variant = jnp.where(pl.program_id(1) == 0, 1, 0)
    gw = 2 * LANES
    gi = lax.broadcasted_iota(jnp.int32, (gw, gw), 0)
    gj = lax.broadcasted_iota(jnp.int32, (gw, gw), 1)
    group_ones = jnp.where((gi // hd) == (gj // hd), 1.0, 0.0).astype(BF16)
    lane = lax.broadcasted_iota(jnp.int32, (1, LANES), 1)
    low_half = lane < hd

    def head_norm(x, gain):
        cols = []
        for c0 in range(0, x.shape[1], gw):
            xc = x[:, c0:c0 + gw]
            ss = _dot((xc * xc).astype(BF16), group_ones)
            cols.append(xc * lax.rsqrt(ss * (1.0 / hd) + EPS))
        return jnp.concatenate(cols, axis=1) * gain

    def dup_half(x, half):
        swapped = pltpu.roll(x, hd, 1)
        return jnp.where(low_half == (half == 0), x, swapped)

    qn = head_norm(q_ref[...].astype(F32), qn_ref[...]) * ((hd ** -0.5) * LOG2E)
    half_sel = [jnp.where(low_half, 1.0, 0.0), jnp.where(low_half, 0.0, 1.0)]
    k_all = jnp.concatenate([kvp_ref[:, 0:kv_w], kvc_ref[:, 0:kv_w]], axis=0).astype(F32)
    kn = head_norm(k_all, kn_ref[...])
    v_all = jnp.concatenate([kvp_ref[:, kv_w:2 * kv_w], kvc_ref[:, kv_w:2 * kv_w]], axis=0).astype(F32)
    ks, vs = [], []
    for g in range(SW_KV_HEADS):
        c0 = (g // 2) * LANES
        ks.append(dup_half(kn[:, c0:c0 + LANES], g % 2).astype(BF16))
        vs.append(dup_half(v_all[:, c0:c0 + LANES], g % 2).astype(BF16))

    scores = []
    for hq in range(SW_HEADS):
        c0 = (hq // 2) * LANES
        q_h = (qn[:, c0:c0 + LANES] * half_sel[hq % 2]).astype(BF16)
        scores.append(_dot_nt(q_h, ks[hq // SW_GROUP]))
    probs = []
    for hq in range(SW_HEADS):
        s = scores[hq] + bias_ref[variant, hq]
        sink = sink_ref[hq] * LOG2E
        mx = jnp.maximum(jnp.max(s, axis=-1, keepdims=True), sink)
        p = jnp.exp2(s - mx)
        denom = jnp.sum(p, axis=-1, keepdims=True) + jnp.exp2(sink - mx)
        probs.append((p / denom).astype(BF16))
    outs = [_dot(probs[hq], vs[hq // SW_GROUP]) for hq in range(SW_HEADS)]
    for c in range(SW_HEADS // 2):
        o_ref[:, c * LANES:(c + 1) * LANES] = jnp.where(low_half, outs[2 * c], outs[2 * c + 1]).astype(o_ref.dtype)


def swa_attention(q, kv, bias, q_norm, k_norm, sinks, batch, seq):
    t = q.shape[0]
    blk = WINDOW
    nb = seq // blk
    qw = SW_HEADS * SW_HD
    kvw = 2 * SW_KV_HEADS * SW_HD
    return pl.pallas_call(
        _swa_kernel,
        out_shape=jax.ShapeDtypeStruct((t, qw), BF16),
        grid=(batch, nb),
        in_specs=[pl.BlockSpec((blk, qw), lambda b, n: (b * nb + n, 0)),
                  pl.BlockSpec((blk, kvw), lambda b, n: (b * nb + jnp.maximum(n - 1, 0), 0)),
                  pl.BlockSpec((blk, kvw), lambda b, n: (b * nb + n, 0)),
                  pl.BlockSpec((2, SW_HEADS, blk, 2 * blk), lambda b, n: (0, 0, 0, 0)),
                  pl.BlockSpec((1, qw), lambda b, n: (0, 0)),
                  pl.BlockSpec((1, kvw // 2), lambda b, n: (0, 0)),
                  pl.BlockSpec(memory_space=pltpu.SMEM)],
        out_specs=pl.BlockSpec((blk, qw), lambda b, n: (b * nb + n, 0)),
        compiler_params=_cparams(("parallel", "parallel")),
        name="swa_attention",
    )(q, kv, kv, bias, jnp.tile(q_norm.astype(F32), SW_HEADS).reshape(1, qw),
      jnp.tile(k_norm.astype(F32), SW_KV_HEADS).reshape(1, kvw // 2), sinks.astype(F32))


def _route_kernel(x_ref, a_ref, wp_ref, g_ref, wr_ref, h_ref, r_ref, wt_ref, tab_ref, cnt_ref,
                  sel_s, gw_s, cnt_s, start_s, run_s, *, tile_rows):
    ne = N_EXPERTS
    p = pl.program_id(0)
    i = pl.program_id(1)
    tm = x_ref.shape[0]
    sub = lax.broadcasted_iota(jnp.int32, (ne, tm), 0).astype(F32)

    def seg_rows(sel):
        n = jnp.sum(sel, axis=1, keepdims=True)
        return jnp.floor((n + (SEG_ALIGN - 1)) * (1.0 / SEG_ALIGN)) * SEG_ALIGN

    def excl_cumsum_experts(v):
        sub8 = lax.broadcasted_iota(jnp.int32, v.shape, 0)
        out = jnp.zeros_like(v)
        for e in range(ne - 1):
            out = out + jnp.where(sub8 > e, v[e:e + 1, :], 0.0)
        return out

    @pl.when(p == 0)
    def _():
        @pl.when(i == 0)
        def _():
            cnt_s[...] = jnp.zeros_like(cnt_s)

        x = x_ref[...] + _dot(a_ref[...], wp_ref[...])
        h_ref[...] = x
        ms = jnp.mean(x * x, axis=-1, keepdims=True)
        xn32 = (x * lax.rsqrt(ms + EPS)) * g_ref[...]
        xn_hi = xn32.astype(BF16)
        xn_lo = (xn32 - xn_hi.astype(F32)).astype(BF16)
        p_hi = _dot_nt(wr_ref[...], xn_hi)
        p_lo = _dot_nt(wr_ref[...], xn_lo)
        logits = p_hi[0:ne] + p_hi[ne:2 * ne] + p_lo[0:ne]
        m1 = jnp.max(logits, axis=0, keepdims=True)
        i1 = jnp.min(jnp.where(logits == m1, sub, float(ne)), axis=0, keepdims=True)
        l2 = jnp.where(sub == i1, -jnp.inf, logits)
        m2 = jnp.max(l2, axis=0, keepdims=True)
        i2 = jnp.min(jnp.where(l2 == m2, sub, float(ne)), axis=0, keepdims=True)
        e2 = jnp.exp(m2 - m1)
        w1 = 1.0 / (1.0 + e2)
        w2 = e2 / (1.0 + e2)
        sel = jnp.where((sub == i1) | (sub == i2), 1.0, 0.0)
        sel_s[i] = sel
        gw_s[i] = jnp.where(sub == i1, w1, jnp.where(sub == i2, w2, 0.0))
        cnt_s[...] += seg_rows(sel)

    @pl.when(p == 1)
    def _():
        @pl.when(i == 0)
        def _():
            cnt = cnt_s[...]
            padded = jnp.floor((cnt + (tile_rows - 1)) * (1.0 / tile_rows)) * tile_rows
            start_s[...] = excl_cumsum_experts(padded)
            run_s[...] = jnp.zeros_like(run_s)
            cnt_ref[...] = cnt

        sel = sel_s[i]
        gw = gw_s[i]
        ti = lax.broadcasted_iota(jnp.int32, (tm, tm), 0)
        tj = lax.broadcasted_iota(jnp.int32, (tm, tm), 1)
        tri = jnp.where(ti <= tj, 1.0, 0.0).astype(BF16)
        csum = _dot(sel.astype(BF16), tri)
        seg = jnp.broadcast_to(seg_rows(sel), run_s.shape)
        local0 = excl_cumsum_experts(seg)
        tab_ref[0, 0] = start_s[...] + run_s[...]
        tab_ref[0, 1] = seg
        tab_ref[0, 2] = local0
        run_s[...] += seg
        local_row = local0[:, 0:1] + csum - sel
        ia = jnp.min(jnp.where(sel > 0.0, sub, float(ne)), axis=0, keepdims=True)
        ib = jnp.max(jnp.where(sel > 0.0, sub, -1.0), axis=0, keepdims=True)
        pick_a = sub == ia
        pick_b = sub == ib
        rows = [jnp.sum(jnp.where(pick_a, local_row, 0.0), axis=0, keepdims=True),
                jnp.sum(jnp.where(pick_b, local_row, 0.0), axis=0, keepdims=True),
                jnp.sum(jnp.where(pick_a, gw, 0.0), axis=0, keepdims=True),
                jnp.sum(jnp.where(pick_b, gw, 0.0), axis=0, keepdims=True)]
        r_ref[...] = jnp.concatenate(rows + [jnp.zeros((ne - 4, tm), F32)], axis=0)
        wpad = jnp.concatenate(rows[2:4] + rows[0:2] + [jnp.zeros((LANES - 4, tm), F32)], axis=0)
        wt_ref[...] = wpad.T


def moe_route(x, a, wp, gain, w_router, tm, tile_rows):
    t, d = x.shape
    ne = w_router.shape[1]
    assert ne == N_EXPERTS
    w_hi = w_router.astype(BF16)
    w_lo = (w_router - w_hi.astype(F32)).astype(BF16)
    wr = jnp.concatenate([w_hi.T, w_lo.T], axis=0)
    tm = min(tm, t)
    nt = t // tm

    def row_map(p, i):
        return (i * (1 - p) + (nt - 1) * p, 0)

    return pl.pallas_call(
        functools.partial(_route_kernel, tile_rows=tile_rows),
        out_shape=(jax.ShapeDtypeStruct((t, d), F32),
                   jax.ShapeDtypeStruct((ne, t), F32), jax.ShapeDtypeStruct((t, LANES), F32),
                   jax.ShapeDtypeStruct((nt, 3, ne, LANES), F32), jax.ShapeDtypeStruct((ne, LANES), F32)),
        grid=(2, nt),
        in_specs=[pl.BlockSpec((tm, d), row_map),
                  pl.BlockSpec((tm, a.shape[1]), row_map),
                  pl.BlockSpec(wp.shape, lambda p, i: (0, 0)),
                  pl.BlockSpec((1, d), lambda p, i: (0, 0)),
                  pl.BlockSpec((2 * ne, d), lambda p, i: (0, 0))],
        out_specs=(pl.BlockSpec((tm, d), row_map),
                   pl.BlockSpec((ne, tm), lambda p, i: (0, i * p)),
                   pl.BlockSpec((tm, LANES), lambda p, i: (i * p, 0)),
                   pl.BlockSpec((1, 3, ne, LANES), lambda p, i: (i * p, 0, 0, 0)),
                   pl.BlockSpec((ne, LANES), lambda p, i: (0, 0))),
        scratch_shapes=[pltpu.VMEM((nt, ne, tm), F32), pltpu.VMEM((nt, ne, tm), F32),
                        pltpu.VMEM((ne, LANES), F32), pltpu.VMEM((ne, LANES), F32), pltpu.VMEM((ne, LANES), F32)],
        compiler_params=_cparams(("arbitrary", "arbitrary")),
        name="moe_route",
    )(x, a, wp, gain.reshape(1, d), wr)


def _segment_copies(tab_ref, i, e, local_ref, slot_ref, sem, to_slots):
    base = (i * N_EXPERTS + e) * 3
    slot0, rows, local0 = tab_ref[base], tab_ref[base + 1], tab_ref[base + 2]
    out = []
    done = 0
    size = MOE_TOKEN_TILE
    while size >= SEG_ALIGN:
        take = rows & size
        loc = local_ref.at[pl.ds(pl.multiple_of(local0 + done, SEG_ALIGN), size)]
        slt = slot_ref.at[pl.ds(pl.multiple_of(slot0 + done, SEG_ALIGN), size)]
        desc = pltpu.make_async_copy(loc, slt, sem) if to_slots else pltpu.make_async_copy(slt, loc, sem)
        out.append((take != 0, desc))
        done = done + take
        size //= 2
    return out


def _run_segment_copies(tab_ref, tile, slot, rows_s, slot_ref, sems, to_slots, action):
    for e in range(N_EXPERTS):
        for cond, desc in _segment_copies(tab_ref, tile, e, rows_s.at[slot], slot_ref, sems.at[slot], to_slots):
            @pl.when(cond)
            def _():
                getattr(desc, action)()


def _dispatch_kernel(tab_ref, zf_ref, x_ref, g_ref, r_ref, xs_ref, rows_s, zero_s, sem, zsem, *, tile_rows):
    i = pl.program_id(0)
    tm = x_ref.shape[0]
    n_local = rows_s.shape[1]

    @pl.when(i == 0)
    def _():
        zero_s[...] = jnp.zeros_like(zero_s)

        def zero_copy(e):
            row0 = pl.multiple_of(zf_ref[e], tile_rows)
            return pltpu.make_async_copy(zero_s, xs_ref.at[pl.ds(row0, tile_rows)], zsem)

        for e in range(zf_ref.shape[0]):
            @pl.when(zf_ref[e] >= 0)
            def _():
                zero_copy(e).start()
        for e in range(zf_ref.shape[0]):
            @pl.when(zf_ref[e] >= 0)
            def _():
                zero_copy(e).wait()

    x = x_ref[...]
    ms = jnp.mean(x * x, axis=-1, keepdims=True)
    xn = ((x * lax.rsqrt(ms + EPS)) * g_ref[...]).astype(BF16)
    row_id = lax.broadcasted_iota(jnp.int32, (n_local, tm), 0).astype(F32)
    onehot = jnp.where((row_id == r_ref[0:1, :]) | (row_id == r_ref[1:2, :]), 1.0, 0.0).astype(BF16)
    slot = lax.rem(i, 2)
    rows_s[slot] = _dot(onehot, xn)

    _run_segment_copies(tab_ref, i, slot, rows_s, xs_ref, sem, True, "start")

    @pl.when(i > 0)
    def _():
        _run_segment_copies(tab_ref, i - 1, 1 - slot, rows_s, xs_ref, sem, True, "wait")

    @pl.when(i == pl.num_programs(0) - 1)
    def _():
        _run_segment_copies(tab_ref, i, slot, rows_s, xs_ref, sem, True, "wait")


def moe_dispatch(x, gain, r, tab, zf_rows, n_slots, tm, tile_rows):
    t, d = x.shape
    nt = t // tm
    n_local = TOP_K * tm + N_EXPERTS * SEG_ALIGN
    grid_spec = pltpu.PrefetchScalarGridSpec(
        num_scalar_prefetch=2,
        grid=(nt,),
        in_specs=[pl.BlockSpec((tm, d), lambda i, tb, zf: (i, 0)),
                  pl.BlockSpec((1, d), lambda i, tb, zf: (0, 0)),
                  pl.BlockSpec((N_EXPERTS, tm), lambda i, tb, zf: (0, i))],
        out_specs=pl.BlockSpec(memory_space=pl.ANY),
        scratch_shapes=[pltpu.VMEM((2, n_local, d), F32), pltpu.VMEM((tile_rows, d), F32),
                        pltpu.SemaphoreType.DMA((2,)), pltpu.SemaphoreType.DMA],
    )
    return pl.pallas_call(
        functools.partial(_dispatch_kernel, tile_rows=tile_rows),
        out_shape=jax.ShapeDtypeStruct((n_slots, d), F32),
        grid_spec=grid_spec,
        compiler_params=_cparams(("arbitrary",)),
        name="moe_dispatch",
    )(tab, zf_rows, x, gain.reshape(1, d), r)


def _combine_kernel(tab_ref, h_ref, wt_ref, ys_ref, o_ref, rows_s, sems):
    i = pl.program_id(0)
    tm = h_ref.shape[0]
    n_local = rows_s.shape[1]
    slot = lax.rem(i, 2)

    def fetch(tile, into):
        rows_s[into] = jnp.zeros(rows_s.shape[1:], rows_s.dtype)
        _run_segment_copies(tab_ref, tile, into, rows_s, ys_ref, sems, False, "start")

    @pl.when(i == 0)
    def _():
        fetch(i, slot)

    @pl.when(i + 1 < pl.num_programs(0))
    def _():
        fetch(i + 1, 1 - slot)

    _run_segment_copies(tab_ref, i, slot, rows_s, ys_ref, sems, False, "wait")

    wt = wt_ref[...]
    y = rows_s[slot].astype(BF16)
    col_id = lax.broadcasted_iota(jnp.int32, (tm, n_local), 1).astype(F32)
    pick_a = jnp.where(col_id == wt[:, 2:3], 1.0, 0.0).astype(BF16)
    pick_b = jnp.where(col_id == wt[:, 3:4], 1.0, 0.0).astype(BF16)
    o_ref[...] = h_ref[...] + wt[:, 0:1] * _dot(pick_a, y) + wt[:, 1:2] * _dot(pick_b, y)


def moe_combine(h, wt, tab, ys, tm):
    t, d = h.shape
    nt = t // tm
    n_local = TOP_K * tm + N_EXPERTS * SEG_ALIGN
    grid_spec = pltpu.PrefetchScalarGridSpec(
        num_scalar_prefetch=1,
        grid=(nt,),
        in_specs=[pl.BlockSpec((tm, d), lambda i, tb: (i, 0)),
                  pl.BlockSpec((tm, LANES), lambda i, tb: (i, 0)),
                  pl.BlockSpec(memory_space=pl.ANY)],
        out_specs=pl.BlockSpec((tm, d), lambda i, tb: (i, 0)),
        scratch_shapes=[pltpu.VMEM((2, n_local, d), F32), pltpu.SemaphoreType.DMA((2,))],
    )
    return pl.pallas_call(
        _combine_kernel,
        out_shape=jax.ShapeDtypeStruct((t, d), F32),
        grid_spec=grid_spec,
        compiler_params=_cparams(("arbitrary",)),
        name="moe_combine",
    )(tab, h, wt, ys)


MOE_TILE_ROWS = 512
FFN_CHUNK = 256
MOE_TOKEN_TILE = 512
SEG_ALIGN = 8
TOP_K = 2


def moe_layer(x, a, wp, gain, w_router, wg, wu, wd):
    t, d = x.shape
    ne = w_router.shape[1]
    tr, tm = MOE_TILE_ROWS, MOE_TOKEN_TILE
    nt = t // tm
    n_tiles = -(-(TOP_K * t + nt * ne * (SEG_ALIGN - 1) + ne * (tr - 1)) // tr)
    n_slots = n_tiles * tr

    h, r, wt, tab, cnt = moe_route(x, a, wp, gain, w_router, tm, tr)
    tab = jnp.transpose(tab[:, :, :, 0], (0, 2, 1)).astype(jnp.int32).reshape(-1)

    counts = cnt[:, 0].astype(jnp.int32)
    padded = ((counts + (tr - 1)) // tr) * tr
    ends = jnp.cumsum(padded)
    n_valid = (ends[-1] // tr).astype(jnp.int32)
    tile_row0 = jnp.arange(n_tiles, dtype=jnp.int32) * tr
    tile_expert = jnp.sum((tile_row0[:, None] >= ends[None, :]).astype(jnp.int32), axis=1)
    tile_expert = jnp.minimum(tile_expert, ne - 1)
    tile_expert = jnp.where(jnp.arange(n_tiles) < n_valid, tile_expert, tile_expert[jnp.maximum(n_valid - 1, 0)])
    prev_expert = jnp.concatenate([jnp.full((1,), -1, jnp.int32), tile_expert[:-1]])
    tile_first = (tile_expert != prev_expert).astype(jnp.int32)
    tail = jnp.arange(TOP_K * t // tr, n_tiles, dtype=jnp.int32)
    zf_rows = jnp.concatenate([jnp.where(padded > 0, ends - tr, -1),
                               jnp.where(tail >= n_valid, tail * tr, -1)]).astype(jnp.int32)

    xs = moe_dispatch(h, gain, r, tab, zf_rows, n_slots, tm, tr)
    ys = expert_swiglu(xs, gain, tile_expert, tile_first, n_valid.reshape(1), wg, wu, wd, tr, FFN_CHUNK, F32, False,
                       "moe_experts")
    return moe_combine(h, wt, tab, ys, tm)


def kernel(x, a_norm, a_w_in, a_conv, a_log_decay, a_dt_bias, a_out_norm, a_w_out, kv_norm, kv_w, k_norm,
           b_norm, b_w_q, q_norm, b_sinks, b_w_o, rel_bias, ffn_norm, dense_w_gate, dense_w_up, dense_w_down,
           moe_router, moe_w_gate, moe_w_up, moe_w_down):
    batch, seq, d = x.shape
    t = batch * seq
    nh, hd = LA_HEADS, LA_D
    main_w = 4 * nh * hd
    h0 = x.reshape(t, d)

    w_in = a_w_in[0]
    w_main = w_in[:, 0:main_w].astype(BF16)
    w_gate = jnp.zeros((d, LANES), BF16).at[:, 0:2 * nh].set(w_in[:, main_w:main_w + 2 * nh].astype(BF16))
    proj, gates = norm_matmul(h0, [(a_norm[0], w_main, BF16), (a_norm[0], w_gate, F32)], 512, "gdn_in_proj")
    o = gdn_core(proj, gates, a_conv[0], a_log_decay[0], a_dt_bias[0], a_out_norm[0], batch, seq)

    h2 = ffn_dense(h0, ffn_norm[0], dense_w_gate[0], dense_w_up[0], dense_w_down[0], MOE_TILE_ROWS, FFN_CHUNK,
                   proj=(o, a_w_out[0].astype(BF16)))

    kv, q = norm_matmul(h2, [(kv_norm, kv_w.astype(BF16), BF16), (b_norm[0], b_w_q[0].astype(BF16), BF16)],
                        1024, "qkv_proj")
    bias = bias_table(rel_bias)
    attn = swa_attention(q, kv, bias, q_norm[0], k_norm, b_sinks[0], batch, seq)

    h4 = moe_layer(h2, attn, b_w_o[0].astype(BF16), ffn_norm[1], moe_router[0], moe_w_gate[0], moe_w_up[0],
                   moe_w_down[0])
    return h4.reshape(batch, seq, d)
```

```python
import functools

import numpy as np
import jax
import jax.numpy as jnp
from jax import lax
from jax.experimental import pallas as pl
from jax.experimental.pallas import tpu as pltpu

F32 = jnp.float32
BF16 = jnp.bfloat16

EPS = 1e-6
NEG_INF = -1e30

LA_HEADS = 8
LA_D = 128
CONV_W = 4
CHUNK = 64
SW_HEADS = 16
SW_KV_HEADS = 4
SW_GROUP = SW_HEADS // SW_KV_HEADS
SW_HD = 64
WINDOW = 128
N_BUCKETS = 32
MAX_DIST = 128
N_EXPERTS = 8

LANES = 128
GDN_BLOCK = 2 * CHUNK
HALO = 16

VMEM_LIMIT = 56 * 1024 * 1024
EXPERT_VMEM_LIMIT = 60 * 1024 * 1024


def _cparams(sem):
    return pltpu.CompilerParams(dimension_semantics=sem, vmem_limit_bytes=VMEM_LIMIT)


def _silu(x):
    return x * (1.0 / (1.0 + jnp.exp(-x)))


def _dot(a, b):
    return jnp.dot(a, b, preferred_element_type=F32)


def _dot_nt(a, b):
    return lax.dot_general(a, b, (((1,), (1,)), ((), ())), preferred_element_type=F32)


def _norm_matmul_kernel(*refs, n_groups):
    x_ref = refs[0]
    g_refs = refs[1:1 + n_groups]
    w_refs = refs[1 + n_groups:1 + 2 * n_groups]
    o_refs = refs[1 + 2 * n_groups:1 + 3 * n_groups]
    x = x_ref[...]
    xr = x * lax.rsqrt(jnp.mean(x * x, axis=-1, keepdims=True) + EPS)
    for g_ref, w_ref, o_ref in zip(g_refs, w_refs, o_refs):
        o_ref[...] = _dot((xr * g_ref[...]).astype(BF16), w_ref[...]).astype(o_ref.dtype)


def norm_matmul(x, groups, tm, name):
    t, d = x.shape
    tm = min(tm, t)
    gains = [g.reshape(1, d).astype(F32) for g, _, _ in groups]
    ws = [w for _, w, _ in groups]
    return pl.pallas_call(
        functools.partial(_norm_matmul_kernel, n_groups=len(groups)),
        out_shape=[jax.ShapeDtypeStruct((t, w.shape[1]), dt) for _, w, dt in groups],
        grid=(t // tm,),
        in_specs=([pl.BlockSpec((tm, d), lambda i: (i, 0))]
                  + [pl.BlockSpec((1, d), lambda i: (0, 0)) for _ in groups]
                  + [pl.BlockSpec(w.shape, lambda i: (0, 0)) for w in ws]),
        out_specs=[pl.BlockSpec((tm, w.shape[1]), lambda i: (i, 0)) for w in ws],
        compiler_params=_cparams(("parallel",)),
        name=name,
    )(x, *gains, *ws)


def _gdn_kernel(proj_ref, gates_ref, convw_ref, hp_ref, onorm_ref, o_ref,
                xs_ref, state_ref, q_s, k_s, v_s, z_s, gc_s, gct_s, beta_s):
    n = pl.program_id(1)

    @pl.when(n == 0)
    def _():
        xs_ref[0:HALO, :] = jnp.zeros((HALO, xs_ref.shape[1]), xs_ref.dtype)
        for ref in (q_s, k_s, v_s, z_s, gc_s, gct_s, beta_s):
            ref[1] = jnp.zeros(ref.shape[1:], ref.dtype)

    @pl.when(n <= 1)
    def _():
        state_ref[...] = jnp.zeros_like(state_ref)

    args = (proj_ref, gates_ref, convw_ref, hp_ref, onorm_ref, o_ref, xs_ref, state_ref,
            q_s, k_s, v_s, z_s, gc_s, gct_s, beta_s)

    @pl.when(lax.rem(n, 2) == 0)
    def _():
        _gdn_step(*args, slot_w=0, slot_r=1)

    @pl.when(lax.rem(n, 2) == 1)
    def _():
        _gdn_step(*args, slot_w=1, slot_r=0)


def _gdn_step(proj_ref, gates_ref, convw_ref, hp_ref, onorm_ref, o_ref, xs_ref, state_ref,
              q_s, k_s, v_s, z_s, gc_s, gct_s, beta_s, *, slot_w, slot_r):
    nh, d, c = LA_HEADS, LA_D, CHUNK
    blk = GDN_BLOCK
    qkv_w = 3 * nh * d

    gc = gc_s[slot_r]
    gc_t = gct_s[slot_r]
    beta = beta_s[slot_r]

    xs_ref[HALO:HALO + blk, :] = proj_ref[:, 0:qkv_w]

    def front_gates():
        _gdn_front_gates(gates_ref, hp_ref, gc_s, gct_s, beta_s, slot_w)

    ci = lax.broadcasted_iota(jnp.int32, (c, c), 0)
    cj = lax.broadcasted_iota(jnp.int32, (c, c), 1)
    lower_incl = ci >= cj
    strict = ci > cj
    eye_c = jnp.where(ci == cj, 1.0, 0.0).astype(F32)
    di = lax.broadcasted_iota(jnp.int32, (d, d), 0)
    dj = lax.broadcasted_iota(jnp.int32, (d, d), 1)
    eye_d = jnp.where(di == dj, 1.0, 0.0).astype(BF16)

    onorm = onorm_ref[...]

    n_shift = CONV_W - 1
    sr = lax.broadcasted_iota(jnp.int32, (n_shift * blk, HALO + blk), 0)
    sc = lax.broadcasted_iota(jnp.int32, (n_shift * blk, HALO + blk), 1)
    shift_mat = jnp.where(sc == HALO + (sr % blk) - (sr // blk + 1), 1.0, 0.0).astype(BF16)
    pair_w = 2 * d

    def conv_silu(col0):
        cols = slice(col0, col0 + pair_w)
        shifted = _dot(shift_mat, xs_ref[:, cols])
        acc = convw_ref[CONV_W - 1:CONV_W, cols] * xs_ref[HALO:HALO + blk, cols].astype(F32)
        for s in range(1, CONV_W):
            acc = acc + convw_ref[CONV_W - 1 - s:CONV_W - s, cols] * shifted[(s - 1) * blk:s * blk]
        return _silu(acc)

    def front_pair(hp):
        c0 = hp * pair_w
        qf = conv_silu(c0)
        kf = conv_silu(nh * d + c0)
        v_s[slot_w, :, c0:c0 + pair_w] = conv_silu(2 * nh * d + c0)
        for half in range(2):
            lo, hi = half * d, (half + 1) * d
            qh, kh = qf[:, lo:hi], kf[:, lo:hi]
            q_s[slot_w, :, c0 + lo:c0 + hi] = qh * (lax.rsqrt(jnp.sum(qh * qh, axis=-1, keepdims=True) + EPS)
                                                    * (d ** -0.5))
            k_s[slot_w, :, c0 + lo:c0 + hi] = kh * lax.rsqrt(jnp.sum(kh * kh, axis=-1, keepdims=True) + EPS)
        z_s[slot_w, :, c0:c0 + pair_w] = proj_ref[:, qkv_w + c0:qkv_w + c0 + pair_w]

    front_tasks = [front_gates] + [functools.partial(front_pair, hp) for hp in range(nh // 2)]

    def run_front_task():
        if front_tasks:
            front_tasks.pop(0)()

    n_ck = blk // c
    chains = [(h, ck) for h in range(nh) for ck in range(n_ck)]

    st = {}
    for (h, ck) in chains:
        r = ck * c
        q = q_s[slot_r, r:r + c, h * d:(h + 1) * d]
        k = k_s[slot_r, r:r + c, h * d:(h + 1) * d]
        v = v_s[slot_r, r:r + c, h * d:(h + 1) * d]
        g_col = gc[r:r + c, nh + h:nh + h + 1]
        g_row = gc_t[nh + h:nh + h + 1, r:r + c]
        g_last = gc[r + c - 1:r + c, nh + h:nh + h + 1]
        b_col = beta[r:r + c, h:h + 1]
        decay = jnp.where(lower_incl, jnp.exp2(jnp.where(lower_incl, g_col - g_row, 0.0)), 0.0)
        k_beta = k * b_col
        e_col = jnp.exp2(g_col)
        lhs = jnp.concatenate([k_beta.astype(BF16), q.astype(BF16), eye_d], axis=0)
        kk = _dot_nt(lhs, k.astype(BF16))
        a_mat = jnp.where(strict, kk[0:c] * decay, 0.0)
        st[(h, ck)] = dict(
            a=a_mat, attn=(kk[c:2 * c] * decay).astype(BF16),
            k_tail_t=(kk[2 * c:2 * c + d] * jnp.exp2(g_last - g_row)).astype(BF16),
            rhs=jnp.concatenate([(v * b_col).astype(BF16), (k_beta * e_col).astype(BF16)], axis=1),
            qe=(q * e_col).astype(BF16), e_last=jnp.exp2(g_last))
    run_front_task()

    for key in chains:
        x_b = (-st[key]["a"]).astype(BF16)
        st[key]["y"] = _dot(x_b, x_b)
        st[key]["p"] = eye_c - st[key]["a"]
    run_front_task()
    n_levels = int(np.log2(c))
    for lvl in range(1, n_levels):
        for key in chains:
            y_b = st[key]["y"].astype(BF16)
            p = st[key]["p"]
            if lvl + 1 < n_levels:
                zz = _dot(jnp.concatenate([y_b, p.astype(BF16)], axis=0), y_b)
                st[key]["y"] = zz[0:c]
                st[key]["p"] = p + zz[c:2 * c]
            else:
                st[key]["p"] = p + _dot(p.astype(BF16), y_b)
        run_front_task()
    for key in chains:
        st[key]["uw"] = _dot(st[key]["p"].astype(BF16), st[key]["rhs"])
    run_front_task()

    for ck in range(n_ck):
        r = ck * c
        s_old = [state_ref[h] for h in range(nh)]
        ws_qs = []
        for h in range(nh):
            cur = st[(h, ck)]
            lhs = jnp.concatenate([cur["uw"][:, d:2 * d].astype(BF16), cur["qe"]], axis=0)
            ws_qs.append(_dot(lhs, s_old[h].astype(BF16)))
        run_front_task()
        for h in range(nh):
            cur = st[(h, ck)]
            v_new = cur["uw"][:, 0:d] - ws_qs[h][0:c]
            av_kv = _dot(jnp.concatenate([cur["attn"], cur["k_tail_t"]], axis=0), v_new.astype(BF16))
            state_ref[h] = s_old[h] * cur["e_last"] + av_kv[c:c + d]
            o = ws_qs[h][c:2 * c] + av_kv[0:c]
            o = (o * lax.rsqrt(jnp.mean(o * o, axis=-1, keepdims=True) + EPS)) * onorm
            z = z_s[slot_r, r:r + c, h * d:(h + 1) * d].astype(F32)
            o_ref[r:r + c, h * d:(h + 1) * d] = (o * _silu(z)).astype(o_ref.dtype)
    while front_tasks:
        run_front_task()

    xs_ref[0:HALO, :] = xs_ref[blk:blk + HALO, :]


def _gdn_front_gates(gates_ref, hp_ref, gc_s, gct_s, beta_s, slot_w):
    blk, c = GDN_BLOCK, CHUNK
    gates = gates_ref[...]
    a_log = hp_ref[0:1, :]
    dt_bias = hp_ref[1:2, :]
    beta = 1.0 / (1.0 + jnp.exp(-gates))
    sp_in = gates + dt_bias
    softplus = jnp.maximum(sp_in, 0.0) + jnp.log(1.0 + jnp.exp(-jnp.abs(sp_in)))
    g = (-jnp.exp(a_log) * softplus) * float(np.log2(np.e))

    row = lax.broadcasted_iota(jnp.int32, (blk, blk), 0)
    col = lax.broadcasted_iota(jnp.int32, (blk, blk), 1)
    tri = jnp.where((row >= col) & ((row // c) == (col // c)), 1.0, 0.0).astype(BF16)
    g_hi = g.astype(BF16)
    g_r1 = g - g_hi.astype(F32)
    g_mid = g_r1.astype(BF16)
    g_lo = (g_r1 - g_mid.astype(F32)).astype(BF16)
    gc = _dot(tri, g_hi) + _dot(tri, g_mid) + _dot(tri, g_lo)
    gc_s[slot_w] = gc
    gct_s[slot_w] = gc.T
    beta_s[slot_w] = beta


def gdn_core(proj, gates, conv_w, a_log, dt_bias, out_norm, batch, seq):
    t = proj.shape[0]
    nh, d = LA_HEADS, LA_D
    blk = GDN_BLOCK
    nblk = seq // blk
    hp = jnp.zeros((8, LANES), F32)
    hp = hp.at[0, nh:2 * nh].set(a_log.astype(F32)).at[1, nh:2 * nh].set(dt_bias.astype(F32))

    def in_map(b, n):
        return (b * nblk + jnp.minimum(n, nblk - 1), 0)

    return pl.pallas_call(
        _gdn_kernel,
        out_shape=jax.ShapeDtypeStruct((t, nh * d), BF16),
        grid=(batch, nblk + 1),
        in_specs=[pl.BlockSpec((blk, 4 * nh * d), in_map),
                  pl.BlockSpec((blk, LANES), in_map),
                  pl.BlockSpec((CONV_W, 3 * nh * d), lambda b, n: (0, 0)),
                  pl.BlockSpec((8, LANES), lambda b, n: (0, 0)),
                  pl.BlockSpec((1, d), lambda b, n: (0, 0))],
        out_specs=pl.BlockSpec((blk, nh * d), lambda b, n: (b * nblk + jnp.maximum(n - 1, 0), 0)),
        scratch_shapes=[pltpu.VMEM((HALO + blk, 3 * nh * d), BF16),
                        pltpu.VMEM((nh, d, d), F32),
                        pltpu.VMEM((2, blk, nh * d), F32), pltpu.VMEM((2, blk, nh * d), F32),
                        pltpu.VMEM((2, blk, nh * d), F32), pltpu.VMEM((2, blk, nh * d), BF16),
                        pltpu.VMEM((2, blk, LANES), F32), pltpu.VMEM((2, LANES, blk), F32),
                        pltpu.VMEM((2, blk, LANES), F32)],
        compiler_params=_cparams(("arbitrary", "arbitrary")),
        name="gdn_core",
    )(proj, gates, conv_w.astype(F32), hp, out_norm.reshape(1, d).astype(F32))


TILE_FULL, TILE_FIRST, TILE_HALF = 0, 1, 2


def _swiglu_kernel(te_ref, mode_ref, nv_ref, x_ref, g_ref, a_ref, wp_ref, wg_hbm, wu_hbm, wd_hbm, o_ref,
                   wg_c, wu_c, wd_c, stage_in, stage_out, sems, xres_s, *, pre_norm, pre_proj, tf):
    i = pl.program_id(0)
    nf = wg_c.shape[0]
    e = te_ref[i]
    valid = i < nv_ref[0]

    def chunk_copies(j, slot):
        cols = pl.ds(j * tf, tf)
        return (pltpu.make_async_copy(wg_hbm.at[e, :, cols], stage_in.at[slot, 0], sems.at[slot, 0]),
                pltpu.make_async_copy(wu_hbm.at[e, :, cols], stage_in.at[slot, 1], sems.at[slot, 1]),
                pltpu.make_async_copy(wd_hbm.at[e, cols, :], stage_out.at[slot], sems.at[slot, 2]))

    tile_rows = x_ref.shape[0]

    def prepare_rows(rows):
        x = x_ref[0:rows, :].astype(F32)
        if pre_proj:
            x = x + _dot(a_ref[0:rows, :], wp_ref[...])
            xres_s[0:rows, :] = x
        if pre_norm:
            ms = jnp.mean(x * x, axis=-1, keepdims=True)
            x = (x * lax.rsqrt(ms + EPS)) * g_ref[...]
        return x.astype(BF16)

    def chunk(xb, j):
        hid = _silu(_dot(xb, wg_c[j])) * _dot(xb, wu_c[j])
        return _dot(hid.astype(BF16), wd_c[j])

    def finish(acc, rows):
        if pre_norm:
            res = xres_s[0:rows, :] if pre_proj else x_ref[0:rows, :]
            acc = res + acc
        o_ref[0:rows, :] = acc.astype(o_ref.dtype)
        if rows < tile_rows:
            o_ref[rows:tile_rows, :] = jnp.zeros((tile_rows - rows, o_ref.shape[1]), o_ref.dtype)

    mode = mode_ref[i]

    @pl.when(valid & (mode == TILE_FIRST))
    def _():
        for c in chunk_copies(0, 0):
            c.start()
        xb = prepare_rows(tile_rows)
        acc = None
        for j in range(nf):
            slot = j % 2
            if j + 1 < nf:
                for c in chunk_copies(j + 1, 1 - slot):
                    c.start()
            for c in chunk_copies(j, slot):
                c.wait()
            wg_c[j] = stage_in[slot, 0].astype(BF16)
            wu_c[j] = stage_in[slot, 1].astype(BF16)
            wd_c[j] = stage_out[slot].astype(BF16)
            y = chunk(xb, j)
            acc = y if acc is None else acc + y
        finish(acc, tile_rows)

    def steady(rows):
        xb = prepare_rows(rows)
        acc = None
        for j in range(nf):
            y = chunk(xb, j)
            acc = y if acc is None else acc + y
        finish(acc, rows)

    @pl.when(valid & (mode == TILE_FULL))
    def _():
        steady(tile_rows)

    @pl.when(valid & (mode == TILE_HALF))
    def _():
        steady(tile_rows // 2)

    @pl.when(jnp.logical_not(valid))
    def _():
        o_ref[...] = jnp.zeros_like(o_ref)


def expert_swiglu(x, gain, tile_expert, tile_mode, n_valid, wg, wu, wd, tile_rows, tf, out_dtype, pre_norm, name,
                  proj=None):
    n_rows, d = x.shape
    ne, _, f = wg.shape
    n_tiles = n_rows // tile_rows
    nf = f // tf
    pre_proj = proj is not None
    if pre_proj:
        a, wp = proj
        a_spec = pl.BlockSpec((tile_rows, a.shape[1]), lambda i, te, fi, nv: (jnp.minimum(i, nv[0] - 1), 0))
    else:
        a, wp = jnp.zeros((8, LANES), BF16), jnp.zeros((LANES, d), BF16)
        a_spec = pl.BlockSpec(a.shape, lambda i, te, fi, nv: (0, 0))
    grid_spec = pltpu.PrefetchScalarGridSpec(
        num_scalar_prefetch=3,
        grid=(n_tiles,),
        in_specs=[pl.BlockSpec((tile_rows, d), lambda i, te, fi, nv: (jnp.minimum(i, nv[0] - 1), 0)),
                  pl.BlockSpec((1, d), lambda i, te, fi, nv: (0, 0)),
                  a_spec,
                  pl.BlockSpec(wp.shape, lambda i, te, fi, nv: (0, 0)),
                  pl.BlockSpec(memory_space=pl.ANY),
                  pl.BlockSpec(memory_space=pl.ANY),
                  pl.BlockSpec(memory_space=pl.ANY)],
        out_specs=pl.BlockSpec((tile_rows, d), lambda i, te, fi, nv: (i, 0)),
        scratch_shapes=[pltpu.VMEM((nf, d, tf), BF16), pltpu.VMEM((nf, d, tf), BF16), pltpu.VMEM((nf, tf, d), BF16),
                        pltpu.VMEM((2, 2, d, tf), F32), pltpu.VMEM((2, tf, d), F32),
                        pltpu.SemaphoreType.DMA((2, 3)),
                        pltpu.VMEM((tile_rows, d) if pre_proj else (8, LANES), F32)],
    )
    return pl.pallas_call(
        functools.partial(_swiglu_kernel, pre_norm=pre_norm, pre_proj=pre_proj, tf=tf),
        out_shape=jax.ShapeDtypeStruct((n_rows, d), out_dtype),
        grid_spec=grid_spec,
        compiler_params=pltpu.CompilerParams(dimension_semantics=("arbitrary",), vmem_limit_bytes=EXPERT_VMEM_LIMIT),
        name=name,
    )(tile_expert, tile_mode, n_valid, x, gain.reshape(1, d).astype(F32), a, wp, wg, wu, wd)


def ffn_dense(x, gain, wg, wu, wd, tm, tf, proj=None):
    t = x.shape[0]
    n_tiles = t // tm
    tile_mode = jnp.full((n_tiles,), TILE_FULL, jnp.int32).at[0].set(TILE_FIRST)
    return expert_swiglu(x, gain, jnp.zeros((n_tiles,), jnp.int32), tile_mode, jnp.full((1,), n_tiles, jnp.int32),
                         wg[None], wu[None], wd[None], tm, tf, F32, True, "ffn_dense", proj=proj)


def _t5_bucket_np(dist):
    max_exact = N_BUCKETS // 2
    n = np.maximum(dist, 0)
    safe = np.maximum(n, 1).astype(np.float32)
    large = max_exact + (np.log(safe / max_exact) / np.log(MAX_DIST / max_exact)
                         * (N_BUCKETS - max_exact)).astype(np.int32)
    large = np.minimum(large, N_BUCKETS - 1)
    return np.where(n < max_exact, n, large).astype(np.int32)


LOG2E = float(np.log2(np.e))


def _bias_kernel(bucket_ref, valid_ref, rb_ref, o_ref):
    bucket = bucket_ref[...]
    for h in range(SW_HEADS):
        acc = jnp.zeros(bucket.shape, F32)
        for b in range(N_BUCKETS):
            acc = jnp.where(bucket == b, rb_ref[b, h], acc)
        for v in range(valid_ref.shape[0]):
            o_ref[v, h] = jnp.where(valid_ref[v] > 0, acc * LOG2E, NEG_INF)


def bias_table(rel_bias):
    qi = np.arange(WINDOW)[:, None] + WINDOW
    kj = np.arange(2 * WINDOW)[None, :]
    dist = qi - kj
    band = (dist >= 0) & (dist < WINDOW)
    valid = np.stack([band, band & (kj >= WINDOW)]).astype(np.int32)
    return pl.pallas_call(
        _bias_kernel,
        out_shape=jax.ShapeDtypeStruct((2, SW_HEADS, WINDOW, 2 * WINDOW), F32),
        in_specs=[pl.BlockSpec(memory_space=pltpu.VMEM), pl.BlockSpec(memory_space=pltpu.VMEM),
                  pl.BlockSpec(memory_space=pltpu.SMEM)],
        out_specs=pl.BlockSpec(memory_space=pltpu.VMEM),
        name="t5_bias_table",
    )(jnp.asarray(_t5_bucket_np(dist)), jnp.asarray(valid), rel_bias.astype(F32))


def _swa_kernel(q_ref, kvp_ref, kvc_ref, bias_ref, qn_ref, kn_ref, sink_ref, o_ref):
    blk, hd = WINDOW, SW_HD
    kv_w = SW_KV_HEADS * hd
    variant = jnp.where(pl.program_id(1) == 0, 1, 0)
    gw = 2 * LANES
    gi = lax.broadcasted_iota(jnp.int32, (gw, gw), 0)
    gj = lax.broadcasted_iota(jnp.int32, (gw, gw), 1)
    group_ones = jnp.where((gi // hd) == (gj // hd), 1.0, 0.0).astype(BF16)
    lane = lax.broadcasted_iota(jnp.int32, (1, LANES), 1)
    low_half = lane < hd

    def head_norm(x, gain):
        cols = []
        for c0 in range(0, x.shape[1], gw):
            xc = x[:, c0:c0 + gw]
            ss = _dot((xc * xc).astype(BF16), group_ones)
            cols.append(xc * lax.rsqrt(ss * (1.0 / hd) + EPS))
        return jnp.concatenate(cols, axis=1) * gain

    def dup_half(x, half):
        swapped = pltpu.roll(x, hd, 1)
        return jnp.where(low_half == (half == 0), x, swapped)

    qn = head_norm(q_ref[...].astype(F32), qn_ref[...]) * ((hd ** -0.5) * LOG2E)
    half_sel = [jnp.where(low_half, 1.0, 0.0), jnp.where(low_half, 0.0, 1.0)]
    k_all = jnp.concatenate([kvp_ref[:, 0:kv_w], kvc_ref[:, 0:kv_w]], axis=0).astype(F32)
    kn = head_norm(k_all, kn_ref[...])
    v_all = jnp.concatenate([kvp_ref[:, kv_w:2 * kv_w], kvc_ref[:, kv_w:2 * kv_w]], axis=0).astype(F32)
    ks, vs = [], []
    for g in range(SW_KV_HEADS):
        c0 = (g // 2) * LANES
        ks.append(dup_half(kn[:, c0:c0 + LANES], g % 2).astype(BF16))
        vs.append(dup_half(v_all[:, c0:c0 + LANES], g % 2).astype(BF16))

    scores = []
    for hq in range(SW_HEADS):
        c0 = (hq // 2) * LANES
        q_h = (qn[:, c0:c0 + LANES] * half_sel[hq % 2]).astype(BF16)
        scores.append(_dot_nt(q_h, ks[hq // SW_GROUP]))
    probs = []
    for hq in range(SW_HEADS):
        s = scores[hq] + bias_ref[variant, hq]
        sink = sink_ref[hq] * LOG2E
        mx = jnp.maximum(jnp.max(s, axis=-1, keepdims=True), sink)
        p = jnp.exp2(s - mx)
        denom = jnp.sum(p, axis=-1, keepdims=True) + jnp.exp2(sink - mx)
        probs.append((p / denom).astype(BF16))
    outs = [_dot(probs[hq], vs[hq // SW_GROUP]) for hq in range(SW_HEADS)]
    for c in range(SW_HEADS // 2):
        o_ref[:, c * LANES:(c + 1) * LANES] = jnp.where(low_half, outs[2 * c], outs[2 * c + 1]).astype(o_ref.dtype)


def swa_attention(q, kv, bias, q_norm, k_norm, sinks, batch, seq):
    t = q.shape[0]
    blk = WINDOW
    nb = seq // blk
    qw = SW_HEADS * SW_HD
    kvw = 2 * SW_KV_HEADS * SW_HD
    return pl.pallas_call(
        _swa_kernel,
        out_shape=jax.ShapeDtypeStruct((t, qw), BF16),
        grid=(batch, nb),
        in_specs=[pl.BlockSpec((blk, qw), lambda b, n: (b * nb + n, 0)),
                  pl.BlockSpec((blk, kvw), lambda b, n: (b * nb + jnp.maximum(n - 1, 0), 0)),
                  pl.BlockSpec((blk, kvw), lambda b, n: (b * nb + n, 0)),
                  pl.BlockSpec((2, SW_HEADS, blk, 2 * blk), lambda b, n: (0, 0, 0, 0)),
                  pl.BlockSpec((1, qw), lambda b, n: (0, 0)),
                  pl.BlockSpec((1, kvw // 2), lambda b, n: (0, 0)),
                  pl.BlockSpec(memory_space=pltpu.SMEM)],
        out_specs=pl.BlockSpec((blk, qw), lambda b, n: (b * nb + n, 0)),
        compiler_params=_cparams(("parallel", "parallel")),
        name="swa_attention",
    )(q, kv, kv, bias, jnp.tile(q_norm.astype(F32), SW_HEADS).reshape(1, qw),
      jnp.tile(k_norm.astype(F32), SW_KV_HEADS).reshape(1, kvw // 2), sinks.astype(F32))


def _route_kernel(x_ref, a_ref, wp_ref, g_ref, wr_ref, h_ref, r_ref, wt_ref, tab_ref, cnt_ref,
                  sel_s, gw_s, cnt_s, start_s, run_s, *, tile_rows):
    ne = N_EXPERTS
    p = pl.program_id(0)
    i = pl.program_id(1)
    tm = x_ref.shape[0]
    sub = lax.broadcasted_iota(jnp.int32, (ne, tm), 0).astype(F32)

    def seg_rows(sel):
        n = jnp.sum(sel, axis=1, keepdims=True)
        return jnp.floor((n + (SEG_ALIGN - 1)) * (1.0 / SEG_ALIGN)) * SEG_ALIGN

    def excl_cumsum_experts(v):
        sub8 = lax.broadcasted_iota(jnp.int32, v.shape, 0)
        out = jnp.zeros_like(v)
        for e in range(ne - 1):
            out = out + jnp.where(sub8 > e, v[e:e + 1, :], 0.0)
        return out

    @pl.when(p == 0)
    def _():
        @pl.when(i == 0)
        def _():
            cnt_s[...] = jnp.zeros_like(cnt_s)

        x = x_ref[...] + _dot(a_ref[...], wp_ref[...])
        h_ref[...] = x
        ms = jnp.mean(x * x, axis=-1, keepdims=True)
        xn32 = (x * lax.rsqrt(ms + EPS)) * g_ref[...]
        xn_hi = xn32.astype(BF16)
        xn_lo = (xn32 - xn_hi.astype(F32)).astype(BF16)
        p_hi = _dot_nt(wr_ref[...], xn_hi)
        p_lo = _dot_nt(wr_ref[...], xn_lo)
        logits = p_hi[0:ne] + p_hi[ne:2 * ne] + p_lo[0:ne]
        m1 = jnp.max(logits, axis=0, keepdims=True)
        i1 = jnp.min(jnp.where(logits == m1, sub, float(ne)), axis=0, keepdims=True)
        l2 = jnp.where(sub == i1, -jnp.inf, logits)
        m2 = jnp.max(l2, axis=0, keepdims=True)
        i2 = jnp.min(jnp.where(l2 == m2, sub, float(ne)), axis=0, keepdims=True)
        e2 = jnp.exp(m2 - m1)
        w1 = 1.0 / (1.0 + e2)
        w2 = e2 / (1.0 + e2)
        sel = jnp.where((sub == i1) | (sub == i2), 1.0, 0.0)
        sel_s[i] = sel
        gw_s[i] = jnp.where(sub == i1, w1, jnp.where(sub == i2, w2, 0.0))
        cnt_s[...] += seg_rows(sel)

    @pl.when(p == 1)
    def _():
        @pl.when(i == 0)
        def _():
            cnt = cnt_s[...]
            padded = jnp.floor((cnt + (tile_rows - 1)) * (1.0 / tile_rows)) * tile_rows
            start_s[...] = excl_cumsum_experts(padded)
            run_s[...] = jnp.zeros_like(run_s)
            cnt_ref[...] = cnt

        sel = sel_s[i]
        gw = gw_s[i]
        ti = lax.broadcasted_iota(jnp.int32, (tm, tm), 0)
        tj = lax.broadcasted_iota(jnp.int32, (tm, tm), 1)
        tri = jnp.where(ti <= tj, 1.0, 0.0).astype(BF16)
        csum = _dot(sel.astype(BF16), tri)
        seg = jnp.broadcast_to(seg_rows(sel), run_s.shape)
        local0 = excl_cumsum_experts(seg)
        tab_ref[0, 0] = start_s[...] + run_s[...]
        tab_ref[0, 1] = seg
        tab_ref[0, 2] = local0
        run_s[...] += seg
        local_row = local0[:, 0:1] + csum - sel
        ia = jnp.min(jnp.where(sel > 0.0, sub, float(ne)), axis=0, keepdims=True)
        ib = jnp.max(jnp.where(sel > 0.0, sub, -1.0), axis=0, keepdims=True)
        pick_a = sub == ia
        pick_b = sub == ib
        rows = [jnp.sum(jnp.where(pick_a, local_row, 0.0), axis=0, keepdims=True),
                jnp.sum(jnp.where(pick_b, local_row, 0.0), axis=0, keepdims=True),
                jnp.sum(jnp.where(pick_a, gw, 0.0), axis=0, keepdims=True),
                jnp.sum(jnp.where(pick_b, gw, 0.0), axis=0, keepdims=True)]
        r_ref[...] = jnp.concatenate(rows + [jnp.zeros((ne - 4, tm), F32)], axis=0)
        wpad = jnp.concatenate(rows[2:4] + rows[0:2] + [jnp.zeros((LANES - 4, tm), F32)], axis=0)
        wt_ref[...] = wpad.T


def moe_route(x, a, wp, gain, w_router, tm, tile_rows):
    t, d = x.shape
    ne = w_router.shape[1]
    assert ne == N_EXPERTS
    w_hi = w_router.astype(BF16)
    w_lo = (w_router - w_hi.astype(F32)).astype(BF16)
    wr = jnp.concatenate([w_hi.T, w_lo.T], axis=0)
    tm = min(tm, t)
    nt = t // tm

    def row_map(p, i):
        return (i * (1 - p) + (nt - 1) * p, 0)

    return pl.pallas_call(
        functools.partial(_route_kernel, tile_rows=tile_rows),
        out_shape=(jax.ShapeDtypeStruct((t, d), F32),
                   jax.ShapeDtypeStruct((ne, t), F32), jax.ShapeDtypeStruct((t, LANES), F32),
                   jax.ShapeDtypeStruct((nt, 3, ne, LANES), F32), jax.ShapeDtypeStruct((ne, LANES), F32)),
        grid=(2, nt),
        in_specs=[pl.BlockSpec((tm, d), row_map),
                  pl.BlockSpec((tm, a.shape[1]), row_map),
                  pl.BlockSpec(wp.shape, lambda p, i: (0, 0)),
                  pl.BlockSpec((1, d), lambda p, i: (0, 0)),
                  pl.BlockSpec((2 * ne, d), lambda p, i: (0, 0))],
        out_specs=(pl.BlockSpec((tm, d), row_map),
                   pl.BlockSpec((ne, tm), lambda p, i: (0, i * p)),
                   pl.BlockSpec((tm, LANES), lambda p, i: (i * p, 0)),
                   pl.BlockSpec((1, 3, ne, LANES), lambda p, i: (i * p, 0, 0, 0)),
                   pl.BlockSpec((ne, LANES), lambda p, i: (0, 0))),
        scratch_shapes=[pltpu.VMEM((nt, ne, tm), F32), pltpu.VMEM((nt, ne, tm), F32),
                        pltpu.VMEM((ne, LANES), F32), pltpu.VMEM((ne, LANES), F32), pltpu.VMEM((ne, LANES), F32)],
        compiler_params=_cparams(("arbitrary", "arbitrary")),
        name="moe_route",
    )(x, a, wp, gain.reshape(1, d), wr)


def _segment_copies(tab_ref, i, e, local_ref, slot_ref, sem, to_slots):
    base = (i * N_EXPERTS + e) * 3
    slot0, rows, local0 = tab_ref[base], tab_ref[base + 1], tab_ref[base + 2]
    out = []
    done = 0
    size = MOE_TOKEN_TILE
    while size >= SEG_ALIGN:
        take = rows & size
        loc = local_ref.at[pl.ds(pl.multiple_of(local0 + done, SEG_ALIGN), size)]
        slt = slot_ref.at[pl.ds(pl.multiple_of(slot0 + done, SEG_ALIGN), size)]
        desc = pltpu.make_async_copy(loc, slt, sem) if to_slots else pltpu.make_async_copy(slt, loc, sem)
        out.append((take != 0, desc))
        done = done + take
        size //= 2
    return out


def _run_segment_copies(tab_ref, tile, slot, rows_s, slot_ref, sems, to_slots, action):
    for e in range(N_EXPERTS):
        for cond, desc in _segment_copies(tab_ref, tile, e, rows_s.at[slot], slot_ref, sems.at[slot], to_slots):
            @pl.when(cond)
            def _():
                getattr(desc, action)()


def _dispatch_kernel(tab_ref, zf_ref, x_ref, g_ref, r_ref, xs_ref, rows_s, zero_s, sem, zsem, *, tile_rows):
    i = pl.program_id(0)
    tm = x_ref.shape[0]
    n_local = rows_s.shape[1]

    @pl.when(i == 0)
    def _():
        zero_s[...] = jnp.zeros_like(zero_s)

        def zero_copy(e):
            row0 = pl.multiple_of(zf_ref[e], tile_rows)
            return pltpu.make_async_copy(zero_s, xs_ref.at[pl.ds(row0, tile_rows)], zsem)

        for e in range(zf_ref.shape[0]):
            @pl.when(zf_ref[e] >= 0)
            def _():
                zero_copy(e).start()
        for e in range(zf_ref.shape[0]):
            @pl.when(zf_ref[e] >= 0)
            def _():
                zero_copy(e).wait()

    x = x_ref[...]
    ms = jnp.mean(x * x, axis=-1, keepdims=True)
    xn = ((x * lax.rsqrt(ms + EPS)) * g_ref[...]).astype(BF16)
    row_id = lax.broadcasted_iota(jnp.int32, (n_local, tm), 0).astype(F32)
    onehot = jnp.where((row_id == r_ref[0:1, :]) | (row_id == r_ref[1:2, :]), 1.0, 0.0).astype(BF16)
    slot = lax.rem(i, 2)
    rows_s[slot] = _dot(onehot, xn)

    _run_segment_copies(tab_ref, i, slot, rows_s, xs_ref, sem, True, "start")

    @pl.when(i > 0)
    def _():
        _run_segment_copies(tab_ref, i - 1, 1 - slot, rows_s, xs_ref, sem, True, "wait")

    @pl.when(i == pl.num_programs(0) - 1)
    def _():
        _run_segment_copies(tab_ref, i, slot, rows_s, xs_ref, sem, True, "wait")


def moe_dispatch(x, gain, r, tab, zf_rows, n_slots, tm, tile_rows):
    t, d = x.shape
    nt = t // tm
    n_local = TOP_K * tm + N_EXPERTS * SEG_ALIGN
    grid_spec = pltpu.PrefetchScalarGridSpec(
        num_scalar_prefetch=2,
        grid=(nt,),
        in_specs=[pl.BlockSpec((tm, d), lambda i, tb, zf: (i, 0)),
                  pl.BlockSpec((1, d), lambda i, tb, zf: (0, 0)),
                  pl.BlockSpec((N_EXPERTS, tm), lambda i, tb, zf: (0, i))],
        out_specs=pl.BlockSpec(memory_space=pl.ANY),
        scratch_shapes=[pltpu.VMEM((2, n_local, d), F32), pltpu.VMEM((tile_rows, d), F32),
                        pltpu.SemaphoreType.DMA((2,)), pltpu.SemaphoreType.DMA],
    )
    return pl.pallas_call(
        functools.partial(_dispatch_kernel, tile_rows=tile_rows),
        out_shape=jax.ShapeDtypeStruct((n_slots, d), F32),
        grid_spec=grid_spec,
        compiler_params=_cparams(("arbitrary",)),
        name="moe_dispatch",
    )(tab, zf_rows, x, gain.reshape(1, d), r)


def _combine_kernel(tab_ref, h_ref, wt_ref, ys_ref, o_ref, rows_s, sems):
    i = pl.program_id(0)
    tm = h_ref.shape[0]
    n_local = rows_s.shape[1]
    slot = lax.rem(i, 2)

    def fetch(tile, into):
        rows_s[into] = jnp.zeros(rows_s.shape[1:], rows_s.dtype)
        _run_segment_copies(tab_ref, tile, into, rows_s, ys_ref, sems, False, "start")

    @pl.when(i == 0)
    def _():
        fetch(i, slot)

    @pl.when(i + 1 < pl.num_programs(0))
    def _():
        fetch(i + 1, 1 - slot)

    _run_segment_copies(tab_ref, i, slot, rows_s, ys_ref, sems, False, "wait")

    wt = wt_ref[...]
    y = rows_s[slot].astype(BF16)
    col_id = lax.broadcasted_iota(jnp.int32, (tm, n_local), 1).astype(F32)
    pick_a = jnp.where(col_id == wt[:, 2:3], 1.0, 0.0).astype(BF16)
    pick_b = jnp.where(col_id == wt[:, 3:4], 1.0, 0.0).astype(BF16)
    o_ref[...] = h_ref[...] + wt[:, 0:1] * _dot(pick_a, y) + wt[:, 1:2] * _dot(pick_b, y)


def moe_combine(h, wt, tab, ys, tm):
    t, d = h.shape
    nt = t // tm
    n_local = TOP_K * tm + N_EXPERTS * SEG_ALIGN
    grid_spec = pltpu.PrefetchScalarGridSpec(
        num_scalar_prefetch=1,
        grid=(nt,),
        in_specs=[pl.BlockSpec((tm, d), lambda i, tb: (i, 0)),
                  pl.BlockSpec((tm, LANES), lambda i, tb: (i, 0)),
                  pl.BlockSpec(memory_space=pl.ANY)],
        out_specs=pl.BlockSpec((tm, d), lambda i, tb: (i, 0)),
        scratch_shapes=[pltpu.VMEM((2, n_local, d), F32), pltpu.SemaphoreType.DMA((2,))],
    )
    return pl.pallas_call(
        _combine_kernel,
        out_shape=jax.ShapeDtypeStruct((t, d), F32),
        grid_spec=grid_spec,
        compiler_params=_cparams(("arbitrary",)),
        name="moe_combine",
    )(tab, h, wt, ys)


MOE_TILE_ROWS = 512
FFN_CHUNK = 512
MOE_TOKEN_TILE = 512
SEG_ALIGN = 8
TOP_K = 2


def moe_layer(x, a, wp, gain, w_router, wg, wu, wd):
    t, d = x.shape
    ne = w_router.shape[1]
    tr, tm = MOE_TILE_ROWS, MOE_TOKEN_TILE
    nt = t // tm
    n_tiles = -(-(TOP_K * t + nt * ne * (SEG_ALIGN - 1) + ne * (tr - 1)) // tr)
    n_slots = n_tiles * tr

    h, r, wt, tab, cnt = moe_route(x, a, wp, gain, w_router, tm, tr)
    tab = jnp.transpose(tab[:, :, :, 0], (0, 2, 1)).astype(jnp.int32).reshape(-1)

    counts = cnt[:, 0].astype(jnp.int32)
    padded = ((counts + (tr - 1)) // tr) * tr
    ends = jnp.cumsum(padded)
    n_valid = (ends[-1] // tr).astype(jnp.int32)
    tile_row0 = jnp.arange(n_tiles, dtype=jnp.int32) * tr
    tile_expert = jnp.sum((tile_row0[:, None] >= ends[None, :]).astype(jnp.int32), axis=1)
    tile_expert = jnp.minimum(tile_expert, ne - 1)
    tile_expert = jnp.where(jnp.arange(n_tiles) < n_valid, tile_expert, tile_expert[jnp.maximum(n_valid - 1, 0)])
    prev_expert = jnp.concatenate([jnp.full((1,), -1, jnp.int32), tile_expert[:-1]])
    rows_used = (ends - padded + counts)[tile_expert] - tile_row0
    tile_mode = jnp.where(tile_expert != prev_expert, TILE_FIRST,
                          jnp.where(rows_used <= tr // 2, TILE_HALF, TILE_FULL)).astype(jnp.int32)
    tail = jnp.arange(TOP_K * t // tr, n_tiles, dtype=jnp.int32)
    zf_rows = jnp.concatenate([jnp.where(padded > 0, ends - tr, -1),
                               jnp.where(tail >= n_valid, tail * tr, -1)]).astype(jnp.int32)

    xs = moe_dispatch(h, gain, r, tab, zf_rows, n_slots, tm, tr)
    ys = expert_swiglu(xs, gain, tile_expert, tile_mode, n_valid.reshape(1), wg, wu, wd, tr, FFN_CHUNK, F32, False,
                       "moe_experts")
    return moe_combine(h, wt, tab, ys, tm)


def kernel(x, a_norm, a_w_in, a_conv, a_log_decay, a_dt_bias, a_out_norm, a_w_out, kv_norm, kv_w, k_norm,
           b_norm, b_w_q, q_norm, b_sinks, b_w_o, rel_bias, ffn_norm, dense_w_gate, dense_w_up, dense_w_down,
           moe_router, moe_w_gate, moe_w_up, moe_w_down):
    batch, seq, d = x.shape
    t = batch * seq
    nh, hd = LA_HEADS, LA_D
    main_w = 4 * nh * hd
    h0 = x.reshape(t, d)

    w_in = a_w_in[0]
    w_main = w_in[:, 0:main_w].astype(BF16)
    w_gate = jnp.zeros((d, LANES), BF16).at[:, 0:2 * nh].set(w_in[:, main_w:main_w + 2 * nh].astype(BF16))
    proj, gates = norm_matmul(h0, [(a_norm[0], w_main, BF16), (a_norm[0], w_gate, F32)], 512, "gdn_in_proj")
    o = gdn_core(proj, gates, a_conv[0], a_log_decay[0], a_dt_bias[0], a_out_norm[0], batch, seq)

    h2 = ffn_dense(h0, ffn_norm[0], dense_w_gate[0], dense_w_up[0], dense_w_down[0], MOE_TILE_ROWS, FFN_CHUNK,
                   proj=(o, a_w_out[0].astype(BF16)))

    kv, q = norm_matmul(h2, [(kv_norm, kv_w.astype(BF16), BF16), (b_norm[0], b_w_q[0].astype(BF16), BF16)],
                        1024, "qkv_proj")
    bias = bias_table(rel_bias)
    attn = swa_attention(q, kv, bias, q_norm[0], k_norm, b_sinks[0], batch, seq)

    h4 = moe_layer(h2, attn, b_w_o[0].astype(BF16), ffn_norm[1], moe_router[0], moe_w_gate[0], moe_w_up[0],
                   moe_w_down[0])
    return h4.reshape(batch, seq, d)
```

```python
import functools

import numpy as np
import jax
import jax.numpy as jnp
from jax import lax
from jax.experimental import pallas as pl
from jax.experimental.pallas import tpu as pltpu

F32 = jnp.float32
BF16 = jnp.bfloat16

EPS = 1e-6
NEG_INF = -1e30

LA_HEADS = 8
LA_D = 128
CONV_W = 4
CHUNK = 64
SW_HEADS = 16
SW_KV_HEADS = 4
SW_GROUP = SW_HEADS // SW_KV_HEADS
SW_HD = 64
WINDOW = 128
N_BUCKETS = 32
MAX_DIST = 128
N_EXPERTS = 8

LANES = 128
GDN_BLOCK = 2 * CHUNK
HALO = 16

VMEM_LIMIT = 56 * 1024 * 1024
EXPERT_VMEM_LIMIT = 60 * 1024 * 1024


def _cparams(sem):
    return pltpu.CompilerParams(dimension_semantics=sem, vmem_limit_bytes=VMEM_LIMIT)


def _silu(x):
    return x * (1.0 / (1.0 + jnp.exp(-x)))


def _dot(a, b):
    return jnp.dot(a, b, preferred_element_type=F32)


def _dot_nt(a, b):
    return lax.dot_general(a, b, (((1,), (1,)), ((), ())), preferred_element_type=F32)


def _norm_matmul_kernel(*refs, n_groups):
    x_ref = refs[0]
    g_refs = refs[1:1 + n_groups]
    w_refs = refs[1 + n_groups:1 + 2 * n_groups]
    o_refs = refs[1 + 2 * n_groups:1 + 3 * n_groups]
    x = x_ref[...]
    xr = x * lax.rsqrt(jnp.mean(x * x, axis=-1, keepdims=True) + EPS)
    for g_ref, w_ref, o_ref in zip(g_refs, w_refs, o_refs):
        o_ref[...] = _dot((xr * g_ref[...]).astype(BF16), w_ref[...]).astype(o_ref.dtype)


def norm_matmul(x, groups, tm, name):
    t, d = x.shape
    tm = min(tm, t)
    gains = [g.reshape(1, d).astype(F32) for g, _, _ in groups]
    ws = [w for _, w, _ in groups]
    return pl.pallas_call(
        functools.partial(_norm_matmul_kernel, n_groups=len(groups)),
        out_shape=[jax.ShapeDtypeStruct((t, w.shape[1]), dt) for _, w, dt in groups],
        grid=(t // tm,),
        in_specs=([pl.BlockSpec((tm, d), lambda i: (i, 0))]
                  + [pl.BlockSpec((1, d), lambda i: (0, 0)) for _ in groups]
                  + [pl.BlockSpec(w.shape, lambda i: (0, 0)) for w in ws]),
        out_specs=[pl.BlockSpec((tm, w.shape[1]), lambda i: (i, 0)) for w in ws],
        compiler_params=_cparams(("parallel",)),
        name=name,
    )(x, *gains, *ws)


def _gdn_kernel(proj_ref, gates_ref, convw_ref, hp_ref, onorm_ref, o_ref,
                xs_ref, state_ref, q_s, k_s, v_s, z_s, gc_s, gct_s, beta_s):
    n = pl.program_id(1)

    @pl.when(n == 0)
    def _():
        xs_ref[0:HALO, :] = jnp.zeros((HALO, xs_ref.shape[1]), xs_ref.dtype)
        for ref in (q_s, k_s, v_s, z_s, gc_s, gct_s, beta_s):
            ref[1] = jnp.zeros(ref.shape[1:], ref.dtype)

    @pl.when(n <= 1)
    def _():
        state_ref[...] = jnp.zeros_like(state_ref)

    args = (proj_ref, gates_ref, convw_ref, hp_ref, onorm_ref, o_ref, xs_ref, state_ref,
            q_s, k_s, v_s, z_s, gc_s, gct_s, beta_s)

    @pl.when(lax.rem(n, 2) == 0)
    def _():
        _gdn_step(*args, slot_w=0, slot_r=1)

    @pl.when(lax.rem(n, 2) == 1)
    def _():
        _gdn_step(*args, slot_w=1, slot_r=0)


def _gdn_step(proj_ref, gates_ref, convw_ref, hp_ref, onorm_ref, o_ref, xs_ref, state_ref,
              q_s, k_s, v_s, z_s, gc_s, gct_s, beta_s, *, slot_w, slot_r):
    nh, d, c = LA_HEADS, LA_D, CHUNK
    blk = GDN_BLOCK
    qkv_w = 3 * nh * d

    gc = gc_s[slot_r]
    gc_t = gct_s[slot_r]
    beta = beta_s[slot_r]

    xs_ref[HALO:HALO + blk, :] = proj_ref[:, 0:qkv_w]

    def front_gates():
        _gdn_front_gates(gates_ref, hp_ref, gc_s, gct_s, beta_s, slot_w)

    ci = lax.broadcasted_iota(jnp.int32, (c, c), 0)
    cj = lax.broadcasted_iota(jnp.int32, (c, c), 1)
    lower_incl = ci >= cj
    strict = ci > cj
    eye_c = jnp.where(ci == cj, 1.0, 0.0).astype(F32)
    di = lax.broadcasted_iota(jnp.int32, (d, d), 0)
    dj = lax.broadcasted_iota(jnp.int32, (d, d), 1)
    eye_d = jnp.where(di == dj, 1.0, 0.0).astype(BF16)

    onorm = onorm_ref[...]

    n_shift = CONV_W - 1
    sr = lax.broadcasted_iota(jnp.int32, (n_shift * blk, HALO + blk), 0)
    sc = lax.broadcasted_iota(jnp.int32, (n_shift * blk, HALO + blk), 1)
    shift_mat = jnp.where(sc == HALO + (sr % blk) - (sr // blk + 1), 1.0, 0.0).astype(BF16)
    pair_w = 2 * d

    def conv_silu(col0):
        cols = slice(col0, col0 + pair_w)
        shifted = _dot(shift_mat, xs_ref[:, cols])
        acc = convw_ref[CONV_W - 1:CONV_W, cols] * xs_ref[HALO:HALO + blk, cols].astype(F32)
        for s in range(1, CONV_W):
            acc = acc + convw_ref[CONV_W - 1 - s:CONV_W - s, cols] * shifted[(s - 1) * blk:s * blk]
        return _silu(acc)

    def front_pair(hp):
        c0 = hp * pair_w
        qf = conv_silu(c0)
        kf = conv_silu(nh * d + c0)
        v_s[slot_w, :, c0:c0 + pair_w] = conv_silu(2 * nh * d + c0)
        for half in range(2):
            lo, hi = half * d, (half + 1) * d
            qh, kh = qf[:, lo:hi], kf[:, lo:hi]
            q_s[slot_w, :, c0 + lo:c0 + hi] = qh * (lax.rsqrt(jnp.sum(qh * qh, axis=-1, keepdims=True) + EPS)
                                                    * (d ** -0.5))
            k_s[slot_w, :, c0 + lo:c0 + hi] = kh * lax.rsqrt(jnp.sum(kh * kh, axis=-1, keepdims=True) + EPS)
        z_s[slot_w, :, c0:c0 + pair_w] = proj_ref[:, qkv_w + c0:qkv_w + c0 + pair_w]

    front_tasks = [front_gates] + [functools.partial(front_pair, hp) for hp in range(nh // 2)]

    def run_front_task():
        if front_tasks:
            front_tasks.pop(0)()

    n_ck = blk // c
    chains = [(h, ck) for h in range(nh) for ck in range(n_ck)]

    st = {}
    for (h, ck) in chains:
        r = ck * c
        q = q_s[slot_r, r:r + c, h * d:(h + 1) * d]
        k = k_s[slot_r, r:r + c, h * d:(h + 1) * d]
        v = v_s[slot_r, r:r + c, h * d:(h + 1) * d]
        g_col = gc[r:r + c, nh + h:nh + h + 1]
        g_row = gc_t[nh + h:nh + h + 1, r:r + c]
        g_last = gc[r + c - 1:r + c, nh + h:nh + h + 1]
        b_col = beta[r:r + c, h:h + 1]
        decay = jnp.where(lower_incl, jnp.exp2(jnp.where(lower_incl, g_col - g_row, 0.0)), 0.0)
        k_beta = k * b_col
        e_col = jnp.exp2(g_col)
        lhs = jnp.concatenate([k_beta.astype(BF16), q.astype(BF16), eye_d], axis=0)
        kk = _dot_nt(lhs, k.astype(BF16))
        a_mat = jnp.where(strict, kk[0:c] * decay, 0.0)
        st[(h, ck)] = dict(
            a=a_mat, attn=(kk[c:2 * c] * decay).astype(BF16),
            k_tail_t=(kk[2 * c:2 * c + d] * jnp.exp2(g_last - g_row)).astype(BF16),
            rhs=jnp.concatenate([(v * b_col).astype(BF16), (k_beta * e_col).astype(BF16)], axis=1),
            qe=(q * e_col).astype(BF16), e_last=jnp.exp2(g_last))
    run_front_task()

    for key in chains:
        x_b = (-st[key]["a"]).astype(BF16)
        st[key]["y"] = _dot(x_b, x_b)
        st[key]["p"] = eye_c - st[key]["a"]
    run_front_task()
    n_levels = int(np.log2(c))
    for lvl in range(1, n_levels):
        for key in chains:
            y_b = st[key]["y"].astype(BF16)
            p = st[key]["p"]
            if lvl + 1 < n_levels:
                zz = _dot(jnp.concatenate([y_b, p.astype(BF16)], axis=0), y_b)
                st[key]["y"] = zz[0:c]
                st[key]["p"] = p + zz[c:2 * c]
            else:
                st[key]["p"] = p + _dot(p.astype(BF16), y_b)
        run_front_task()
    for key in chains:
        st[key]["uw"] = _dot(st[key]["p"].astype(BF16), st[key]["rhs"])
    run_front_task()

    for ck in range(n_ck):
        r = ck * c
        s_old = [state_ref[h] for h in range(nh)]
        ws_qs = []
        for h in range(nh):
            cur = st[(h, ck)]
            lhs = jnp.concatenate([cur["uw"][:, d:2 * d].astype(BF16), cur["qe"]], axis=0)
            ws_qs.append(_dot(lhs, s_old[h].astype(BF16)))
        run_front_task()
        for h in range(nh):
            cur = st[(h, ck)]
            v_new = cur["uw"][:, 0:d] - ws_qs[h][0:c]
            av_kv = _dot(jnp.concatenate([cur["attn"], cur["k_tail_t"]], axis=0), v_new.astype(BF16))
            state_ref[h] = s_old[h] * cur["e_last"] + av_kv[c:c + d]
            o = ws_qs[h][c:2 * c] + av_kv[0:c]
            o = (o * lax.rsqrt(jnp.mean(o * o, axis=-1, keepdims=True) + EPS)) * onorm
            z = z_s[slot_r, r:r + c, h * d:(h + 1) * d].astype(F32)
            o_ref[r:r + c, h * d:(h + 1) * d] = (o * _silu(z)).astype(o_ref.dtype)
    while front_tasks:
        run_front_task()

    xs_ref[0:HALO, :] = xs_ref[blk:blk + HALO, :]


def _gdn_front_gates(gates_ref, hp_ref, gc_s, gct_s, beta_s, slot_w):
    blk, c = GDN_BLOCK, CHUNK
    gates = gates_ref[...]
    a_log = hp_ref[0:1, :]
    dt_bias = hp_ref[1:2, :]
    beta = 1.0 / (1.0 + jnp.exp(-gates))
    sp_in = gates + dt_bias
    softplus = jnp.maximum(sp_in, 0.0) + jnp.log(1.0 + jnp.exp(-jnp.abs(sp_in)))
    g = (-jnp.exp(a_log) * softplus) * float(np.log2(np.e))

    row = lax.broadcasted_iota(jnp.int32, (blk, blk), 0)
    col = lax.broadcasted_iota(jnp.int32, (blk, blk), 1)
    tri = jnp.where((row >= col) & ((row // c) == (col // c)), 1.0, 0.0).astype(BF16)
    g_hi = g.astype(BF16)
    g_r1 = g - g_hi.astype(F32)
    g_mid = g_r1.astype(BF16)
    g_lo = (g_r1 - g_mid.astype(F32)).astype(BF16)
    gc = _dot(tri, g_hi) + _dot(tri, g_mid) + _dot(tri, g_lo)
    gc_s[slot_w] = gc
    gct_s[slot_w] = gc.T
    beta_s[slot_w] = beta


def gdn_core(proj, gates, conv_w, a_log, dt_bias, out_norm, batch, seq):
    t = proj.shape[0]
    nh, d = LA_HEADS, LA_D
    blk = GDN_BLOCK
    nblk = seq // blk
    hp = jnp.zeros((8, LANES), F32)
    hp = hp.at[0, nh:2 * nh].set(a_log.astype(F32)).at[1, nh:2 * nh].set(dt_bias.astype(F32))

    def in_map(b, n):
        return (b * nblk + jnp.minimum(n, nblk - 1), 0)

    return pl.pallas_call(
        _gdn_kernel,
        out_shape=jax.ShapeDtypeStruct((t, nh * d), BF16),
        grid=(batch, nblk + 1),
        in_specs=[pl.BlockSpec((blk, 4 * nh * d), in_map),
                  pl.BlockSpec((blk, LANES), in_map),
                  pl.BlockSpec((CONV_W, 3 * nh * d), lambda b, n: (0, 0)),
                  pl.BlockSpec((8, LANES), lambda b, n: (0, 0)),
                  pl.BlockSpec((1, d), lambda b, n: (0, 0))],
        out_specs=pl.BlockSpec((blk, nh * d), lambda b, n: (b * nblk + jnp.maximum(n - 1, 0), 0)),
        scratch_shapes=[pltpu.VMEM((HALO + blk, 3 * nh * d), BF16),
                        pltpu.VMEM((nh, d, d), F32),
                        pltpu.VMEM((2, blk, nh * d), F32), pltpu.VMEM((2, blk, nh * d), F32),
                        pltpu.VMEM((2, blk, nh * d), F32), pltpu.VMEM((2, blk, nh * d), BF16),
                        pltpu.VMEM((2, blk, LANES), F32), pltpu.VMEM((2, LANES, blk), F32),
                        pltpu.VMEM((2, blk, LANES), F32)],
        compiler_params=_cparams(("arbitrary", "arbitrary")),
        name="gdn_core",
    )(proj, gates, conv_w.astype(F32), hp, out_norm.reshape(1, d).astype(F32))


TILE_FULL, TILE_FIRST, TILE_HALF = 0, 1, 2


def _swiglu_kernel(te_ref, mode_ref, nv_ref, x_ref, g_ref, a_ref, wp_ref, wg_hbm, wu_hbm, wd_hbm, o_ref,
                   wg_c, wu_c, wd_c, stage_in, stage_out, sems, xres_s, *, pre_norm, pre_proj, routed, tf):
    i = pl.program_id(0)
    nf = wg_c.shape[0]
    e = te_ref[i]
    valid = i < nv_ref[0]

    def chunk_copies(j, slot):
        cols = pl.ds(j * tf, tf)
        return (pltpu.make_async_copy(wg_hbm.at[e, :, cols], stage_in.at[slot, 0], sems.at[slot, 0]),
                pltpu.make_async_copy(wu_hbm.at[e, :, cols], stage_in.at[slot, 1], sems.at[slot, 1]),
                pltpu.make_async_copy(wd_hbm.at[e, cols, :], stage_out.at[slot], sems.at[slot, 2]))

    tile_rows = x_ref.shape[0]

    d_model = o_ref.shape[1]

    def prepare_rows(rows):
        x = x_ref[0:rows, 0:d_model].astype(F32)
        if pre_proj:
            x = x + _dot(a_ref[0:rows, :], wp_ref[...])
            xres_s[0:rows, :] = x
        if pre_norm:
            ms = jnp.mean(x * x, axis=-1, keepdims=True)
            x = (x * lax.rsqrt(ms + EPS)) * g_ref[...]
        return x.astype(BF16)

    def chunk(xb, j):
        hid = _silu(_dot(xb, wg_c[j])) * _dot(xb, wu_c[j])
        return _dot(hid.astype(BF16), wd_c[j])

    def finish(acc, rows):
        if routed:
            acc = acc * (x_ref[0:rows, d_model:d_model + 1] + x_ref[0:rows, d_model + 1:d_model + 2])
        if pre_norm:
            res = xres_s[0:rows, :] if pre_proj else x_ref[0:rows, 0:d_model]
            acc = res + acc
        o_ref[0:rows, :] = acc.astype(o_ref.dtype)
        if rows < tile_rows:
            o_ref[rows:tile_rows, :] = jnp.zeros((tile_rows - rows, o_ref.shape[1]), o_ref.dtype)

    mode = mode_ref[i]

    @pl.when(valid & (mode == TILE_FIRST))
    def _():
        for c in chunk_copies(0, 0):
            c.start()
        xb = prepare_rows(tile_rows)
        acc = None
        for j in range(nf):
            slot = j % 2
            if j + 1 < nf:
                for c in chunk_copies(j + 1, 1 - slot):
                    c.start()
            for c in chunk_copies(j, slot):
                c.wait()
            wg_c[j] = stage_in[slot, 0].astype(BF16)
            wu_c[j] = stage_in[slot, 1].astype(BF16)
            wd_c[j] = stage_out[slot].astype(BF16)
            y = chunk(xb, j)
            acc = y if acc is None else acc + y
        finish(acc, tile_rows)

    def steady(rows):
        xb = prepare_rows(rows)
        acc = None
        for j in range(nf):
            y = chunk(xb, j)
            acc = y if acc is None else acc + y
        finish(acc, rows)

    @pl.when(valid & (mode == TILE_FULL))
    def _():
        steady(tile_rows)

    if routed:
        @pl.when(valid & (mode == TILE_HALF))
        def _():
            steady(tile_rows // 2)

    @pl.when(jnp.logical_not(valid))
    def _():
        o_ref[...] = jnp.zeros_like(o_ref)


def expert_swiglu(x, gain, tile_expert, tile_mode, n_valid, wg, wu, wd, tile_rows, tf, out_dtype, pre_norm, name,
                  proj=None, routed=False):
    n_rows = x.shape[0]
    ne, d, f = wg.shape
    n_tiles = n_rows // tile_rows
    nf = f // tf
    pre_proj = proj is not None
    if pre_proj:
        a, wp = proj
        a_spec = pl.BlockSpec((tile_rows, a.shape[1]), lambda i, te, fi, nv: (jnp.minimum(i, nv[0] - 1), 0))
    else:
        a, wp = jnp.zeros((8, LANES), BF16), jnp.zeros((LANES, d), BF16)
        a_spec = pl.BlockSpec(a.shape, lambda i, te, fi, nv: (0, 0))
    grid_spec = pltpu.PrefetchScalarGridSpec(
        num_scalar_prefetch=3,
        grid=(n_tiles,),
        in_specs=[pl.BlockSpec((tile_rows, x.shape[1]), lambda i, te, fi, nv: (jnp.minimum(i, nv[0] - 1), 0)),
                  pl.BlockSpec((1, d), lambda i, te, fi, nv: (0, 0)),
                  a_spec,
                  pl.BlockSpec(wp.shape, lambda i, te, fi, nv: (0, 0)),
                  pl.BlockSpec(memory_space=pl.ANY),
                  pl.BlockSpec(memory_space=pl.ANY),
                  pl.BlockSpec(memory_space=pl.ANY)],
        out_specs=pl.BlockSpec((tile_rows, d), lambda i, te, fi, nv: (i, 0)),
        scratch_shapes=[pltpu.VMEM((nf, d, tf), BF16), pltpu.VMEM((nf, d, tf), BF16), pltpu.VMEM((nf, tf, d), BF16),
                        pltpu.VMEM((2, 2, d, tf), F32), pltpu.VMEM((2, tf, d), F32),
                        pltpu.SemaphoreType.DMA((2, 3)),
                        pltpu.VMEM((tile_rows, d) if pre_proj else (8, LANES), F32)],
    )
    return pl.pallas_call(
        functools.partial(_swiglu_kernel, pre_norm=pre_norm, pre_proj=pre_proj, routed=routed, tf=tf),
        out_shape=jax.ShapeDtypeStruct((n_rows, d), out_dtype),
        grid_spec=grid_spec,
        compiler_params=pltpu.CompilerParams(dimension_semantics=("arbitrary",), vmem_limit_bytes=EXPERT_VMEM_LIMIT),
        name=name,
    )(tile_expert, tile_mode, n_valid, x, gain.reshape(1, d).astype(F32), a, wp, wg, wu, wd)


def ffn_dense(x, gain, wg, wu, wd, tm, tf, proj=None):
    t = x.shape[0]
    n_tiles = t // tm
    tile_mode = jnp.full((n_tiles,), TILE_FULL, jnp.int32).at[0].set(TILE_FIRST)
    return expert_swiglu(x, gain, jnp.zeros((n_tiles,), jnp.int32), tile_mode, jnp.full((1,), n_tiles, jnp.int32),
                         wg[None], wu[None], wd[None], tm, tf, F32, True, "ffn_dense", proj=proj)


def _t5_bucket_np(dist):
    max_exact = N_BUCKETS // 2
    n = np.maximum(dist, 0)
    safe = np.maximum(n, 1).astype(np.float32)
    large = max_exact + (np.log(safe / max_exact) / np.log(MAX_DIST / max_exact)
                         * (N_BUCKETS - max_exact)).astype(np.int32)
    large = np.minimum(large, N_BUCKETS - 1)
    return np.where(n < max_exact, n, large).astype(np.int32)


LOG2E = float(np.log2(np.e))


def _bias_kernel(bucket_ref, valid_ref, rb_ref, o_ref):
    bucket = bucket_ref[...]
    for h in range(SW_HEADS):
        acc = jnp.zeros(bucket.shape, F32)
        for b in range(N_BUCKETS):
            acc = jnp.where(bucket == b, rb_ref[b, h], acc)
        for v in range(valid_ref.shape[0]):
            o_ref[v, h] = jnp.where(valid_ref[v] > 0, acc * LOG2E, NEG_INF)


def bias_table(rel_bias):
    qi = np.arange(WINDOW)[:, None] + WINDOW
    kj = np.arange(2 * WINDOW)[None, :]
    dist = qi - kj
    band = (dist >= 0) & (dist < WINDOW)
    valid = np.stack([band, band & (kj >= WINDOW)]).astype(np.int32)
    return pl.pallas_call(
        _bias_kernel,
        out_shape=jax.ShapeDtypeStruct((2, SW_HEADS, WINDOW, 2 * WINDOW), F32),
        in_specs=[pl.BlockSpec(memory_space=pltpu.VMEM), pl.BlockSpec(memory_space=pltpu.VMEM),
                  pl.BlockSpec(memory_space=pltpu.SMEM)],
        out_specs=pl.BlockSpec(memory_space=pltpu.VMEM),
        name="t5_bias_table",
    )(jnp.asarray(_t5_bucket_np(dist)), jnp.asarray(valid), rel_bias.astype(F32))


def _swa_kernel(q_ref, kvp_ref, kvc_ref, bias_ref, qn_ref, kn_ref, sink_ref, o_ref):
    blk, hd = WINDOW, SW_HD
    kv_w = SW_KV_HEADS * hd
    variant = jnp.where(pl.program_id(1) == 0, 1, 0)
    gw = 2 * LANES
    gi = lax.broadcasted_iota(jnp.int32, (gw, gw), 0)
    gj = lax.broadcasted_iota(jnp.int32, (gw, gw), 1)
    group_ones = jnp.where((gi // hd) == (gj // hd), 1.0, 0.0).astype(BF16)
    lane = lax.broadcasted_iota(jnp.int32, (1, LANES), 1)
    low_half = lane < hd

    def head_norm(x, gain):
        cols = []
        for c0 in range(0, x.shape[1], gw):
            xc = x[:, c0:c0 + gw]
            ss = _dot((xc * xc).astype(BF16), group_ones)
            cols.append(xc * lax.rsqrt(ss * (1.0 / hd) + EPS))
        return jnp.concatenate(cols, axis=1) * gain

    def dup_half(x, half):
        swapped = pltpu.roll(x, hd, 1)
        return jnp.where(low_half == (half == 0), x, swapped)

    qn = head_norm(q_ref[...].astype(F32), qn_ref[...]) * ((hd ** -0.5) * LOG2E)
    half_sel = [jnp.where(low_half, 1.0, 0.0), jnp.where(low_half, 0.0, 1.0)]
    k_all = jnp.concatenate([kvp_ref[:, 0:kv_w], kvc_ref[:, 0:kv_w]], axis=0).astype(F32)
    kn = head_norm(k_all, kn_ref[...])
    v_all = jnp.concatenate([kvp_ref[:, kv_w:2 * kv_w], kvc_ref[:, kv_w:2 * kv_w]], axis=0).astype(F32)
    ks, vs = [], []
    for g in range(SW_KV_HEADS):
        c0 = (g // 2) * LANES
        ks.append(dup_half(kn[:, c0:c0 + LANES], g % 2).astype(BF16))
        vs.append(dup_half(v_all[:, c0:c0 + LANES], g % 2).astype(BF16))

    scores = []
    for hq in range(SW_HEADS):
        c0 = (hq // 2) * LANES
        q_h = (qn[:, c0:c0 + LANES] * half_sel[hq % 2]).astype(BF16)
        scores.append(_dot_nt(q_h, ks[hq // SW_GROUP]))
    probs = []
    for hq in range(SW_HEADS):
        s = scores[hq] + bias_ref[variant, hq]
        sink = sink_ref[hq] * LOG2E
        mx = jnp.maximum(jnp.max(s, axis=-1, keepdims=True), sink)
        p = jnp.exp2(s - mx)
        denom = jnp.sum(p, axis=-1, keepdims=True) + jnp.exp2(sink - mx)
        probs.append((p / denom).astype(BF16))
    outs = [_dot(probs[hq], vs[hq // SW_GROUP]) for hq in range(SW_HEADS)]
    for c in range(SW_HEADS // 2):
        o_ref[:, c * LANES:(c + 1) * LANES] = jnp.where(low_half, outs[2 * c], outs[2 * c + 1]).astype(o_ref.dtype)


def swa_attention(q, kv, bias, q_norm, k_norm, sinks, batch, seq):
    t = q.shape[0]
    blk = WINDOW
    nb = seq // blk
    qw = SW_HEADS * SW_HD
    kvw = 2 * SW_KV_HEADS * SW_HD
    return pl.pallas_call(
        _swa_kernel,
        out_shape=jax.ShapeDtypeStruct((t, qw), BF16),
        grid=(batch, nb),
        in_specs=[pl.BlockSpec((blk, qw), lambda b, n: (b * nb + n, 0)),
                  pl.BlockSpec((blk, kvw), lambda b, n: (b * nb + jnp.maximum(n - 1, 0), 0)),
                  pl.BlockSpec((blk, kvw), lambda b, n: (b * nb + n, 0)),
                  pl.BlockSpec((2, SW_HEADS, blk, 2 * blk), lambda b, n: (0, 0, 0, 0)),
                  pl.BlockSpec((1, qw), lambda b, n: (0, 0)),
                  pl.BlockSpec((1, kvw // 2), lambda b, n: (0, 0)),
                  pl.BlockSpec(memory_space=pltpu.SMEM)],
        out_specs=pl.BlockSpec((blk, qw), lambda b, n: (b * nb + n, 0)),
        compiler_params=_cparams(("parallel", "parallel")),
        name="swa_attention",
    )(q, kv, kv, bias, jnp.tile(q_norm.astype(F32), SW_HEADS).reshape(1, qw),
      jnp.tile(k_norm.astype(F32), SW_KV_HEADS).reshape(1, kvw // 2), sinks.astype(F32))


def _route_kernel(x_ref, a_ref, wp_ref, g_ref, wr_ref, h_ref, r_ref, wt_ref, tab_ref, cnt_ref,
                  sel_s, gw_s, cnt_s, start_s, run_s, *, tile_rows):
    ne = N_EXPERTS
    p = pl.program_id(0)
    i = pl.program_id(1)
    tm = x_ref.shape[0]
    sub = lax.broadcasted_iota(jnp.int32, (ne, tm), 0).astype(F32)

    def seg_rows(sel):
        n = jnp.sum(sel, axis=1, keepdims=True)
        return jnp.floor((n + (SEG_ALIGN - 1)) * (1.0 / SEG_ALIGN)) * SEG_ALIGN

    def excl_cumsum_experts(v):
        sub8 = lax.broadcasted_iota(jnp.int32, v.shape, 0)
        out = jnp.zeros_like(v)
        for e in range(ne - 1):
            out = out + jnp.where(sub8 > e, v[e:e + 1, :], 0.0)
        return out

    @pl.when(p == 0)
    def _():
        @pl.when(i == 0)
        def _():
            cnt_s[...] = jnp.zeros_like(cnt_s)

        x = x_ref[...] + _dot(a_ref[...], wp_ref[...])
        h_ref[...] = x
        ms = jnp.mean(x * x, axis=-1, keepdims=True)
        xn32 = (x * lax.rsqrt(ms + EPS)) * g_ref[...]
        xn_hi = xn32.astype(BF16)
        xn_lo = (xn32 - xn_hi.astype(F32)).astype(BF16)
        p_hi = _dot_nt(wr_ref[...], xn_hi)
        p_lo = _dot_nt(wr_ref[...], xn_lo)
        logits = p_hi[0:ne] + p_hi[ne:2 * ne] + p_lo[0:ne]
        m1 = jnp.max(logits, axis=0, keepdims=True)
        i1 = jnp.min(jnp.where(logits == m1, sub, float(ne)), axis=0, keepdims=True)
        l2 = jnp.where(sub == i1, -jnp.inf, logits)
        m2 = jnp.max(l2, axis=0, keepdims=True)
        i2 = jnp.min(jnp.where(l2 == m2, sub, float(ne)), axis=0, keepdims=True)
        e2 = jnp.exp(m2 - m1)
        w1 = 1.0 / (1.0 + e2)
        w2 = e2 / (1.0 + e2)
        sel = jnp.where((sub == i1) | (sub == i2), 1.0, 0.0)
        sel_s[i] = sel
        gw_s[i] = jnp.where(sub == i1, w1, jnp.where(sub == i2, w2, 0.0))
        cnt_s[...] += seg_rows(sel)

    @pl.when(p == 1)
    def _():
        @pl.when(i == 0)
        def _():
            cnt = cnt_s[...]
            padded = jnp.floor((cnt + (tile_rows - 1)) * (1.0 / tile_rows)) * tile_rows
            start_s[...] = excl_cumsum_experts(padded)
            run_s[...] = jnp.zeros_like(run_s)
            cnt_ref[...] = cnt

        sel = sel_s[i]
        gw = gw_s[i]
        ti = lax.broadcasted_iota(jnp.int32, (tm, tm), 0)
        tj = lax.broadcasted_iota(jnp.int32, (tm, tm), 1)
        tri = jnp.where(ti <= tj, 1.0, 0.0).astype(BF16)
        csum = _dot(sel.astype(BF16), tri)
        seg = jnp.broadcast_to(seg_rows(sel), run_s.shape)
        local0 = excl_cumsum_experts(seg)
        tab_ref[0, 0] = start_s[...] + run_s[...]
        tab_ref[0, 1] = seg
        tab_ref[0, 2] = local0
        run_s[...] += seg
        local_row = local0[:, 0:1] + csum - sel
        ia = jnp.min(jnp.where(sel > 0.0, sub, float(ne)), axis=0, keepdims=True)
        ib = jnp.max(jnp.where(sel > 0.0, sub, -1.0), axis=0, keepdims=True)
        pick_a = sub == ia
        pick_b = sub == ib
        rows = [jnp.sum(jnp.where(pick_a, local_row, 0.0), axis=0, keepdims=True),
                jnp.sum(jnp.where(pick_b, local_row, 0.0), axis=0, keepdims=True),
                jnp.sum(jnp.where(pick_a, gw, 0.0), axis=0, keepdims=True),
                jnp.sum(jnp.where(pick_b, gw, 0.0), axis=0, keepdims=True)]
        r_ref[...] = jnp.concatenate(rows + [jnp.zeros((ne - 4, tm), F32)], axis=0)
        wpad = jnp.concatenate(rows[2:4] + rows[0:2] + [jnp.zeros((LANES - 4, tm), F32)], axis=0)
        wt_ref[...] = wpad.T


def moe_route(x, a, wp, gain, w_router, tm, tile_rows):
    t, d = x.shape
    ne = w_router.shape[1]
    assert ne == N_EXPERTS
    w_hi = w_router.astype(BF16)
    w_lo = (w_router - w_hi.astype(F32)).astype(BF16)
    wr = jnp.concatenate([w_hi.T, w_lo.T], axis=0)
    tm = min(tm, t)
    nt = t // tm

    def row_map(p, i):
        return (i * (1 - p) + (nt - 1) * p, 0)

    return pl.pallas_call(
        functools.partial(_route_kernel, tile_rows=tile_rows),
        out_shape=(jax.ShapeDtypeStruct((t, d), F32),
                   jax.ShapeDtypeStruct((ne, t), F32), jax.ShapeDtypeStruct((t, LANES), F32),
                   jax.ShapeDtypeStruct((nt, 3, ne, LANES), F32), jax.ShapeDtypeStruct((ne, LANES), F32)),
        grid=(2, nt),
        in_specs=[pl.BlockSpec((tm, d), row_map),
                  pl.BlockSpec((tm, a.shape[1]), row_map),
                  pl.BlockSpec(wp.shape, lambda p, i: (0, 0)),
                  pl.BlockSpec((1, d), lambda p, i: (0, 0)),
                  pl.BlockSpec((2 * ne, d), lambda p, i: (0, 0))],
        out_specs=(pl.BlockSpec((tm, d), row_map),
                   pl.BlockSpec((ne, tm), lambda p, i: (0, i * p)),
                   pl.BlockSpec((tm, LANES), lambda p, i: (i * p, 0)),
                   pl.BlockSpec((1, 3, ne, LANES), lambda p, i: (i * p, 0, 0, 0)),
                   pl.BlockSpec((ne, LANES), lambda p, i: (0, 0))),
        scratch_shapes=[pltpu.VMEM((nt, ne, tm), F32), pltpu.VMEM((nt, ne, tm), F32),
                        pltpu.VMEM((ne, LANES), F32), pltpu.VMEM((ne, LANES), F32), pltpu.VMEM((ne, LANES), F32)],
        compiler_params=_cparams(("arbitrary", "arbitrary")),
        name="moe_route",
    )(x, a, wp, gain.reshape(1, d), wr)


def _segment_copies(tab_ref, i, e, local_ref, slot_ref, sem, to_slots):
    base = (i * N_EXPERTS + e) * 3
    slot0, rows, local0 = tab_ref[base], tab_ref[base + 1], tab_ref[base + 2]
    out = []
    done = 0
    size = MOE_TOKEN_TILE
    while size >= SEG_ALIGN:
        take = rows & size
        loc = local_ref.at[pl.ds(pl.multiple_of(local0 + done, SEG_ALIGN), size)]
        slt = slot_ref.at[pl.ds(pl.multiple_of(slot0 + done, SEG_ALIGN), size)]
        desc = pltpu.make_async_copy(loc, slt, sem) if to_slots else pltpu.make_async_copy(slt, loc, sem)
        out.append((take != 0, desc))
        done = done + take
        size //= 2
    return out


def _run_segment_copies(tab_ref, tile, slot, rows_s, slot_ref, sems, to_slots, action):
    for e in range(N_EXPERTS):
        for cond, desc in _segment_copies(tab_ref, tile, e, rows_s.at[slot], slot_ref, sems.at[slot], to_slots):
            @pl.when(cond)
            def _():
                getattr(desc, action)()


def _dispatch_kernel(tab_ref, zf_ref, x_ref, g_ref, r_ref, wt_ref, xs_ref, rows_s, zero_s, sem, zsem, *, tile_rows):
    i = pl.program_id(0)
    tm, d = x_ref.shape
    n_local = rows_s.shape[1]

    @pl.when(i == 0)
    def _():
        zero_s[...] = jnp.zeros_like(zero_s)

        def zero_copy(e):
            row0 = pl.multiple_of(zf_ref[e], tile_rows)
            return pltpu.make_async_copy(zero_s, xs_ref.at[pl.ds(row0, tile_rows)], zsem)

        for e in range(zf_ref.shape[0]):
            @pl.when(zf_ref[e] >= 0)
            def _():
                zero_copy(e).start()
        for e in range(zf_ref.shape[0]):
            @pl.when(zf_ref[e] >= 0)
            def _():
                zero_copy(e).wait()

    x = x_ref[...]
    ms = jnp.mean(x * x, axis=-1, keepdims=True)
    xn = ((x * lax.rsqrt(ms + EPS)) * g_ref[...]).astype(BF16)
    row_id = lax.broadcasted_iota(jnp.int32, (n_local, tm), 0).astype(F32)
    pick_a = jnp.where(row_id == r_ref[0:1, :], 1.0, 0.0)
    pick_b = jnp.where(row_id == r_ref[1:2, :], 1.0, 0.0)
    slot = lax.rem(i, 2)
    rows_s[slot, :, 0:d] = _dot((pick_a + pick_b).astype(BF16), xn)

    wt = wt_ref[...]
    lane = lax.broadcasted_iota(jnp.int32, wt.shape, 1)

    def gate_lanes(w):
        hi = w.astype(BF16).astype(F32)
        return jnp.where(lane == 0, hi, jnp.where(lane == 1, w - hi, 0.0)).astype(BF16)

    rows_s[slot, :, d:d + LANES] = (_dot(pick_a.astype(BF16), gate_lanes(wt[:, 0:1]))
                                    + _dot(pick_b.astype(BF16), gate_lanes(wt[:, 1:2])))

    _run_segment_copies(tab_ref, i, slot, rows_s, xs_ref, sem, True, "start")

    @pl.when(i > 0)
    def _():
        _run_segment_copies(tab_ref, i - 1, 1 - slot, rows_s, xs_ref, sem, True, "wait")

    @pl.when(i == pl.num_programs(0) - 1)
    def _():
        _run_segment_copies(tab_ref, i, slot, rows_s, xs_ref, sem, True, "wait")


def moe_dispatch(x, gain, r, wt, tab, zf_rows, n_slots, tm, tile_rows):
    t, d = x.shape
    nt = t // tm
    n_local = TOP_K * tm + N_EXPERTS * SEG_ALIGN
    dw = d + LANES
    grid_spec = pltpu.PrefetchScalarGridSpec(
        num_scalar_prefetch=2,
        grid=(nt,),
        in_specs=[pl.BlockSpec((tm, d), lambda i, tb, zf: (i, 0)),
                  pl.BlockSpec((1, d), lambda i, tb, zf: (0, 0)),
                  pl.BlockSpec((N_EXPERTS, tm), lambda i, tb, zf: (0, i)),
                  pl.BlockSpec((tm, LANES), lambda i, tb, zf: (i, 0))],
        out_specs=pl.BlockSpec(memory_space=pl.ANY),
        scratch_shapes=[pltpu.VMEM((2, n_local, dw), F32), pltpu.VMEM((tile_rows, dw), F32),
                        pltpu.SemaphoreType.DMA((2,)), pltpu.SemaphoreType.DMA],
    )
    return pl.pallas_call(
        functools.partial(_dispatch_kernel, tile_rows=tile_rows),
        out_shape=jax.ShapeDtypeStruct((n_slots, dw), F32),
        grid_spec=grid_spec,
        compiler_params=_cparams(("arbitrary",)),
        name="moe_dispatch",
    )(tab, zf_rows, x, gain.reshape(1, d), r, wt)


def _combine_kernel(tab_ref, h_ref, wt_ref, ys_ref, o_ref, rows_s, sems):
    i = pl.program_id(0)
    tm = h_ref.shape[0]
    n_local = rows_s.shape[1]
    slot = lax.rem(i, 2)

    def fetch(tile, into):
        rows_s[into] = jnp.zeros(rows_s.shape[1:], rows_s.dtype)
        _run_segment_copies(tab_ref, tile, into, rows_s, ys_ref, sems, False, "start")

    @pl.when(i == 0)
    def _():
        fetch(i, slot)

    @pl.when(i + 1 < pl.num_programs(0))
    def _():
        fetch(i + 1, 1 - slot)

    _run_segment_copies(tab_ref, i, slot, rows_s, ys_ref, sems, False, "wait")

    wt = wt_ref[...]
    y = rows_s[slot].astype(BF16)
    col_id = lax.broadcasted_iota(jnp.int32, (tm, n_local), 1).astype(F32)
    pick = jnp.where((col_id == wt[:, 2:3]) | (col_id == wt[:, 3:4]), 1.0, 0.0).astype(BF16)
    o_ref[...] = h_ref[...] + _dot(pick, y)


def moe_combine(h, wt, tab, ys, tm):
    t, d = h.shape
    nt = t // tm
    n_local = TOP_K * tm + N_EXPERTS * SEG_ALIGN
    grid_spec = pltpu.PrefetchScalarGridSpec(
        num_scalar_prefetch=1,
        grid=(nt,),
        in_specs=[pl.BlockSpec((tm, d), lambda i, tb: (i, 0)),
                  pl.BlockSpec((tm, LANES), lambda i, tb: (i, 0)),
                  pl.BlockSpec(memory_space=pl.ANY)],
        out_specs=pl.BlockSpec((tm, d), lambda i, tb: (i, 0)),
        scratch_shapes=[pltpu.VMEM((2, n_local, d), F32), pltpu.SemaphoreType.DMA((2,))],
    )
    return pl.pallas_call(
        _combine_kernel,
        out_shape=jax.ShapeDtypeStruct((t, d), F32),
        grid_spec=grid_spec,
        compiler_params=_cparams(("arbitrary",)),
        name="moe_combine",
    )(tab, h, wt, ys)


MOE_TILE_ROWS = 512
FFN_CHUNK = 512
MOE_TOKEN_TILE = 512
SEG_ALIGN = 8
TOP_K = 2


def moe_layer(x, a, wp, gain, w_router, wg, wu, wd):
    t, d = x.shape
    ne = w_router.shape[1]
    tr, tm = MOE_TILE_ROWS, MOE_TOKEN_TILE
    nt = t // tm
    n_tiles = -(-(TOP_K * t + nt * ne * (SEG_ALIGN - 1) + ne * (tr - 1)) // tr)
    n_slots = n_tiles * tr

    h, r, wt, tab, cnt = moe_route(x, a, wp, gain, w_router, tm, tr)
    tab = jnp.transpose(tab[:, :, :, 0], (0, 2, 1)).astype(jnp.int32).reshape(-1)

    counts = cnt[:, 0].astype(jnp.int32)
    padded = ((counts + (tr - 1)) // tr) * tr
    ends = jnp.cumsum(padded)
    n_valid = (ends[-1] // tr).astype(jnp.int32)
    tile_row0 = jnp.arange(n_tiles, dtype=jnp.int32) * tr
    tile_expert = jnp.sum((tile_row0[:, None] >= ends[None, :]).astype(jnp.int32), axis=1)
    tile_expert = jnp.minimum(tile_expert, ne - 1)
    tile_expert = jnp.where(jnp.arange(n_tiles) < n_valid, tile_expert, tile_expert[jnp.maximum(n_valid - 1, 0)])
    prev_expert = jnp.concatenate([jnp.full((1,), -1, jnp.int32), tile_expert[:-1]])
    rows_used = (ends - padded + counts)[tile_expert] - tile_row0
    tile_mode = jnp.where(tile_expert != prev_expert, TILE_FIRST,
                          jnp.where(rows_used <= tr // 2, TILE_HALF, TILE_FULL)).astype(jnp.int32)
    tail = jnp.arange(TOP_K * t // tr, n_tiles, dtype=jnp.int32)
    zf_rows = jnp.concatenate([jnp.where(padded > 0, ends - tr, -1),
                               jnp.where(tail >= n_valid, tail * tr, -1)]).astype(jnp.int32)

    xs = moe_dispatch(h, gain, r, wt, tab, zf_rows, n_slots, tm, tr)
    ys = expert_swiglu(xs, gain, tile_expert, tile_mode, n_valid.reshape(1), wg, wu, wd, tr, FFN_CHUNK, F32, False,
                       "moe_experts", routed=True)
    return moe_combine(h, wt, tab, ys, tm)


def kernel(x, a_norm, a_w_in, a_conv, a_log_decay, a_dt_bias, a_out_norm, a_w_out, kv_norm, kv_w, k_norm,
           b_norm, b_w_q, q_norm, b_sinks, b_w_o, rel_bias, ffn_norm, dense_w_gate, dense_w_up, dense_w_down,
           moe_router, moe_w_gate, moe_w_up, moe_w_down):
    batch, seq, d = x.shape
    t = batch * seq
    nh, hd = LA_HEADS, LA_D
    main_w = 4 * nh * hd
    h0 = x.reshape(t, d)

    w_in = a_w_in[0]
    w_main = w_in[:, 0:main_w].astype(BF16)
    w_gate = jnp.zeros((d, LANES), BF16).at[:, 0:2 * nh].set(w_in[:, main_w:main_w + 2 * nh].astype(BF16))
    proj, gates = norm_matmul(h0, [(a_norm[0], w_main, BF16), (a_norm[0], w_gate, F32)], 512, "gdn_in_proj")
    o = gdn_core(proj, gates, a_conv[0], a_log_decay[0], a_dt_bias[0], a_out_norm[0], batch, seq)

    h2 = ffn_dense(h0, ffn_norm[0], dense_w_gate[0], dense_w_up[0], dense_w_down[0], MOE_TILE_ROWS, FFN_CHUNK,
                   proj=(o, a_w_out[0].astype(BF16)))

    kv, q = norm_matmul(h2, [(kv_norm, kv_w.astype(BF16), BF16), (b_norm[0], b_w_q[0].astype(BF16), BF16)],
                        1024, "qkv_proj")
    bias = bias_table(rel_bias)
    attn = swa_attention(q, kv, bias, q_norm[0], k_norm, b_sinks[0], batch, seq)

    h4 = moe_layer(h2, attn, b_w_o[0].astype(BF16), ffn_norm[1], moe_router[0], moe_w_gate[0], moe_w_up[0],
                   moe_w_down[0])
    return h4.reshape(batch, seq, d)
```

```python
import functools

import numpy as np
import jax
import jax.numpy as jnp
from jax import lax
from jax.experimental import pallas as pl
from jax.experimental.pallas import tpu as pltpu

F32 = jnp.float32
BF16 = jnp.bfloat16

EPS = 1e-6
NEG_INF = -1e30

LA_HEADS = 8
LA_D = 128
CONV_W = 4
CHUNK = 64
SW_HEADS = 16
SW_KV_HEADS = 4
SW_GROUP = SW_HEADS // SW_KV_HEADS
SW_HD = 64
WINDOW = 128
SWA_QBLOCKS = 2
N_BUCKETS = 32
MAX_DIST = 128
N_EXPERTS = 8

LANES = 128
GDN_BLOCK = 2 * CHUNK
HALO = 16

VMEM_LIMIT = 56 * 1024 * 1024
EXPERT_VMEM_LIMIT = 60 * 1024 * 1024


def _cparams(sem):
    return pltpu.CompilerParams(dimension_semantics=sem, vmem_limit_bytes=VMEM_LIMIT)


def _silu(x):
    return x * (1.0 / (1.0 + jnp.exp(-x)))


def _dot(a, b):
    return jnp.dot(a, b, preferred_element_type=F32)


def _dot_nt(a, b):
    return lax.dot_general(a, b, (((1,), (1,)), ((), ())), preferred_element_type=F32)


def _norm_matmul_kernel(*refs, n_groups):
    x_ref = refs[0]
    g_refs = refs[1:1 + n_groups]
    w_refs = refs[1 + n_groups:1 + 2 * n_groups]
    o_refs = refs[1 + 2 * n_groups:1 + 3 * n_groups]
    x = x_ref[...]
    xr = x * lax.rsqrt(jnp.mean(x * x, axis=-1, keepdims=True) + EPS)
    for g_ref, w_ref, o_ref in zip(g_refs, w_refs, o_refs):
        o_ref[...] = _dot((xr * g_ref[...]).astype(BF16), w_ref[...]).astype(o_ref.dtype)


def norm_matmul(x, groups, tm, name):
    t, d = x.shape
    tm = min(tm, t)
    assert t % tm == 0
    gains = [g.reshape(1, d).astype(F32) for g, _, _ in groups]
    ws = [w for _, w, _ in groups]
    return pl.pallas_call(
        functools.partial(_norm_matmul_kernel, n_groups=len(groups)),
        out_shape=[jax.ShapeDtypeStruct((t, w.shape[1]), dt) for _, w, dt in groups],
        grid=(t // tm,),
        in_specs=([pl.BlockSpec((tm, d), lambda i: (i, 0))]
                  + [pl.BlockSpec((1, d), lambda i: (0, 0)) for _ in groups]
                  + [pl.BlockSpec(w.shape, lambda i: (0, 0)) for w in ws]),
        out_specs=[pl.BlockSpec((tm, w.shape[1]), lambda i: (i, 0)) for w in ws],
        compiler_params=_cparams(("parallel",)),
        name=name,
    )(x, *gains, *ws)


def _gdn_kernel(proj_ref, gates_ref, convw_ref, hp_ref, onorm_ref, o_ref,
                xs_ref, state_ref, q_s, k_s, v_s, z_s, gc_s, gct_s, beta_s):
    n = pl.program_id(1)

    @pl.when(n == 0)
    def _():
        xs_ref[0:HALO, :] = jnp.zeros((HALO, xs_ref.shape[1]), xs_ref.dtype)
        for ref in (q_s, k_s, v_s, z_s, gc_s, gct_s, beta_s):
            ref[1] = jnp.zeros(ref.shape[1:], ref.dtype)

    @pl.when(n <= 1)
    def _():
        state_ref[...] = jnp.zeros_like(state_ref)

    args = (proj_ref, gates_ref, convw_ref, hp_ref, onorm_ref, o_ref, xs_ref, state_ref,
            q_s, k_s, v_s, z_s, gc_s, gct_s, beta_s)

    @pl.when(lax.rem(n, 2) == 0)
    def _():
        _gdn_step(*args, slot_w=0, slot_r=1)

    @pl.when(lax.rem(n, 2) == 1)
    def _():
        _gdn_step(*args, slot_w=1, slot_r=0)


def _gdn_step(proj_ref, gates_ref, convw_ref, hp_ref, onorm_ref, o_ref, xs_ref, state_ref,
              q_s, k_s, v_s, z_s, gc_s, gct_s, beta_s, *, slot_w, slot_r):
    nh, d, c = LA_HEADS, LA_D, CHUNK
    blk = GDN_BLOCK
    qkv_w = 3 * nh * d

    gc = gc_s[slot_r]
    gc_t = gct_s[slot_r]
    beta = beta_s[slot_r]

    xs_ref[HALO:HALO + blk, :] = proj_ref[:, 0:qkv_w]

    def front_gates():
        _gdn_front_gates(gates_ref, hp_ref, gc_s, gct_s, beta_s, slot_w)

    ci = lax.broadcasted_iota(jnp.int32, (c, c), 0)
    cj = lax.broadcasted_iota(jnp.int32, (c, c), 1)
    lower_incl = ci >= cj
    strict = ci > cj
    eye_c = jnp.where(ci == cj, 1.0, 0.0).astype(F32)
    di = lax.broadcasted_iota(jnp.int32, (d, d), 0)
    dj = lax.broadcasted_iota(jnp.int32, (d, d), 1)
    eye_d = jnp.where(di == dj, 1.0, 0.0).astype(BF16)

    onorm = onorm_ref[...]

    n_shift = CONV_W - 1
    sr = lax.broadcasted_iota(jnp.int32, (n_shift * blk, HALO + blk), 0)
    sc = lax.broadcasted_iota(jnp.int32, (n_shift * blk, HALO + blk), 1)
    shift_mat = jnp.where(sc == HALO + (sr % blk) - (sr // blk + 1), 1.0, 0.0).astype(BF16)
    pair_w = 2 * d

    def conv_silu(col0):
        cols = slice(col0, col0 + pair_w)
        shifted = _dot(shift_mat, xs_ref[:, cols])
        acc = convw_ref[CONV_W - 1:CONV_W, cols] * xs_ref[HALO:HALO + blk, cols].astype(F32)
        for s in range(1, CONV_W):
            acc = acc + convw_ref[CONV_W - 1 - s:CONV_W - s, cols] * shifted[(s - 1) * blk:s * blk]
        return _silu(acc)

    def front_pair(hp):
        c0 = hp * pair_w
        qf = conv_silu(c0)
        kf = conv_silu(nh * d + c0)
        v_s[slot_w, :, c0:c0 + pair_w] = conv_silu(2 * nh * d + c0)
        for half in range(2):
            lo, hi = half * d, (half + 1) * d
            qh, kh = qf[:, lo:hi], kf[:, lo:hi]
            q_s[slot_w, :, c0 + lo:c0 + hi] = qh * (lax.rsqrt(jnp.sum(qh * qh, axis=-1, keepdims=True) + EPS)
                                                    * (d ** -0.5))
            k_s[slot_w, :, c0 + lo:c0 + hi] = kh * lax.rsqrt(jnp.sum(kh * kh, axis=-1, keepdims=True) + EPS)
        z_s[slot_w, :, c0:c0 + pair_w] = proj_ref[:, qkv_w + c0:qkv_w + c0 + pair_w]

    front_tasks = [front_gates] + [functools.partial(front_pair, hp) for hp in range(nh // 2)]

    def run_front_task():
        if front_tasks:
            front_tasks.pop(0)()

    n_ck = blk // c
    chains = [(h, ck) for h in range(nh) for ck in range(n_ck)]

    st = {}
    for (h, ck) in chains:
        r = ck * c
        q = q_s[slot_r, r:r + c, h * d:(h + 1) * d]
        k = k_s[slot_r, r:r + c, h * d:(h + 1) * d]
        v = v_s[slot_r, r:r + c, h * d:(h + 1) * d]
        g_col = gc[r:r + c, nh + h:nh + h + 1]
        g_row = gc_t[nh + h:nh + h + 1, r:r + c]
        g_last = gc[r + c - 1:r + c, nh + h:nh + h + 1]
        b_col = beta[r:r + c, h:h + 1]
        decay = jnp.where(lower_incl, jnp.exp2(jnp.where(lower_incl, g_col - g_row, 0.0)), 0.0)
        k_beta = k * b_col
        e_col = jnp.exp2(g_col)
        lhs = jnp.concatenate([k_beta.astype(BF16), q.astype(BF16), eye_d], axis=0)
        kk = _dot_nt(lhs, k.astype(BF16))
        a_mat = jnp.where(strict, kk[0:c] * decay, 0.0)
        st[(h, ck)] = dict(
            a=a_mat, attn=(kk[c:2 * c] * decay).astype(BF16),
            k_tail_t=(kk[2 * c:2 * c + d] * jnp.exp2(g_last - g_row)).astype(BF16),
            rhs=jnp.concatenate([(v * b_col).astype(BF16), (k_beta * e_col).astype(BF16)], axis=1),
            qe=(q * e_col).astype(BF16), e_last=jnp.exp2(g_last))
    run_front_task()

    for key in chains:
        x_b = (-st[key]["a"]).astype(BF16)
        st[key]["y"] = _dot(x_b, x_b)
        st[key]["p"] = eye_c - st[key]["a"]
    run_front_task()
    n_levels = int(np.log2(c))
    for lvl in range(1, n_levels):
        for key in chains:
            y_b = st[key]["y"].astype(BF16)
            p = st[key]["p"]
            if lvl + 1 < n_levels:
                zz = _dot(jnp.concatenate([y_b, p.astype(BF16)], axis=0), y_b)
                st[key]["y"] = zz[0:c]
                st[key]["p"] = p + zz[c:2 * c]
            else:
                st[key]["p"] = p + _dot(p.astype(BF16), y_b)
        run_front_task()
    for key in chains:
        st[key]["uw"] = _dot(st[key]["p"].astype(BF16), st[key]["rhs"])
    run_front_task()

    for ck in range(n_ck):
        r = ck * c
        s_old = [state_ref[h] for h in range(nh)]
        ws_qs = []
        for h in range(nh):
            cur = st[(h, ck)]
            lhs = jnp.concatenate([cur["uw"][:, d:2 * d].astype(BF16), cur["qe"]], axis=0)
            ws_qs.append(_dot(lhs, s_old[h].astype(BF16)))
        run_front_task()
        for h in range(nh):
            cur = st[(h, ck)]
            v_new = cur["uw"][:, 0:d] - ws_qs[h][0:c]
            av_kv = _dot(jnp.concatenate([cur["attn"], cur["k_tail_t"]], axis=0), v_new.astype(BF16))
            state_ref[h] = s_old[h] * cur["e_last"] + av_kv[c:c + d]
            o = ws_qs[h][c:2 * c] + av_kv[0:c]
            o = (o * lax.rsqrt(jnp.mean(o * o, axis=-1, keepdims=True) + EPS)) * onorm
            z = z_s[slot_r, r:r + c, h * d:(h + 1) * d].astype(F32)
            o_ref[r:r + c, h * d:(h + 1) * d] = (o * _silu(z)).astype(o_ref.dtype)
    while front_tasks:
        run_front_task()

    xs_ref[0:HALO, :] = xs_ref[blk:blk + HALO, :]


def _gdn_front_gates(gates_ref, hp_ref, gc_s, gct_s, beta_s, slot_w):
    blk, c = GDN_BLOCK, CHUNK
    gates = gates_ref[...]
    a_log = hp_ref[0:1, :]
    dt_bias = hp_ref[1:2, :]
    beta = 1.0 / (1.0 + jnp.exp(-gates))
    sp_in = gates + dt_bias
    softplus = jnp.maximum(sp_in, 0.0) + jnp.log(1.0 + jnp.exp(-jnp.abs(sp_in)))
    g = (-jnp.exp(a_log) * softplus) * float(np.log2(np.e))

    row = lax.broadcasted_iota(jnp.int32, (blk, blk), 0)
    col = lax.broadcasted_iota(jnp.int32, (blk, blk), 1)
    tri = jnp.where((row >= col) & ((row // c) == (col // c)), 1.0, 0.0).astype(BF16)
    g_hi = g.astype(BF16)
    g_r1 = g - g_hi.astype(F32)
    g_mid = g_r1.astype(BF16)
    g_lo = (g_r1 - g_mid.astype(F32)).astype(BF16)
    gc = _dot(tri, g_hi) + _dot(tri, g_mid) + _dot(tri, g_lo)
    gc_s[slot_w] = gc
    gct_s[slot_w] = gc.T
    beta_s[slot_w] = beta


def gdn_core(proj, gates, conv_w, a_log, dt_bias, out_norm, batch, seq):
    t = proj.shape[0]
    nh, d = LA_HEADS, LA_D
    blk = GDN_BLOCK
    assert seq % blk == 0
    nblk = seq // blk
    hp = jnp.zeros((8, LANES), F32)
    hp = hp.at[0, nh:2 * nh].set(a_log.astype(F32)).at[1, nh:2 * nh].set(dt_bias.astype(F32))

    def in_map(b, n):
        return (b * nblk + jnp.minimum(n, nblk - 1), 0)

    return pl.pallas_call(
        _gdn_kernel,
        out_shape=jax.ShapeDtypeStruct((t, nh * d), BF16),
        grid=(batch, nblk + 1),
        in_specs=[pl.BlockSpec((blk, 4 * nh * d), in_map),
                  pl.BlockSpec((blk, LANES), in_map),
                  pl.BlockSpec((CONV_W, 3 * nh * d), lambda b, n: (0, 0)),
                  pl.BlockSpec((8, LANES), lambda b, n: (0, 0)),
                  pl.BlockSpec((1, d), lambda b, n: (0, 0))],
        out_specs=pl.BlockSpec((blk, nh * d), lambda b, n: (b * nblk + jnp.maximum(n - 1, 0), 0)),
        scratch_shapes=[pltpu.VMEM((HALO + blk, 3 * nh * d), BF16),
                        pltpu.VMEM((nh, d, d), F32),
                        pltpu.VMEM((2, blk, nh * d), F32), pltpu.VMEM((2, blk, nh * d), F32),
                        pltpu.VMEM((2, blk, nh * d), F32), pltpu.VMEM((2, blk, nh * d), BF16),
                        pltpu.VMEM((2, blk, LANES), F32), pltpu.VMEM((2, LANES, blk), F32),
                        pltpu.VMEM((2, blk, LANES), F32)],
        compiler_params=_cparams(("arbitrary", "arbitrary")),
        name="gdn_core",
    )(proj, gates, conv_w.astype(F32), hp, out_norm.reshape(1, d).astype(F32))


TILE_FULL, TILE_FIRST, TILE_HALF = 0, 1, 2


def _swiglu_kernel(te_ref, mode_ref, nv_ref, x_ref, g_ref, a_ref, wp_ref, wg_hbm, wu_hbm, wd_hbm, o_ref,
                   wg_c, wu_c, wd_c, stage_in, stage_out, sems, xres_s, *, pre_norm, pre_proj, routed, tf):
    i = pl.program_id(0)
    nf = wg_c.shape[0]
    e = te_ref[i]
    valid = i < nv_ref[0]

    def chunk_copies(j, slot):
        cols = pl.ds(j * tf, tf)
        return (pltpu.make_async_copy(wg_hbm.at[e, :, cols], stage_in.at[slot, 0], sems.at[slot, 0]),
                pltpu.make_async_copy(wu_hbm.at[e, :, cols], stage_in.at[slot, 1], sems.at[slot, 1]),
                pltpu.make_async_copy(wd_hbm.at[e, cols, :], stage_out.at[slot], sems.at[slot, 2]))

    tile_rows = x_ref.shape[0]

    d_model = o_ref.shape[1]

    def prepare_rows(rows):
        x = x_ref[0:rows, 0:d_model].astype(F32)
        if pre_proj:
            x = x + _dot(a_ref[0:rows, :], wp_ref[...])
            xres_s[0:rows, :] = x
        if pre_norm:
            ms = jnp.mean(x * x, axis=-1, keepdims=True)
            x = (x * lax.rsqrt(ms + EPS)) * g_ref[...]
        return x.astype(BF16)

    def chunk(xb, j):
        hid = _silu(_dot(xb, wg_c[j])) * _dot(xb, wu_c[j])
        return _dot(hid.astype(BF16), wd_c[j])

    def finish(acc, rows):
        if pre_norm:
            res = xres_s[0:rows, :] if pre_proj else x_ref[0:rows, 0:d_model]
            acc = res + acc
        o_ref[0:rows, :] = acc.astype(o_ref.dtype)
        if rows < tile_rows:
            o_ref[rows:tile_rows, :] = jnp.zeros((tile_rows - rows, o_ref.shape[1]), o_ref.dtype)

    mode = mode_ref[i]

    @pl.when(valid & (mode == TILE_FIRST))
    def _():
        for c in chunk_copies(0, 0):
            c.start()
        xb = prepare_rows(tile_rows)
        acc = None
        for j in range(nf):
            slot = j % 2
            if j + 1 < nf:
                for c in chunk_copies(j + 1, 1 - slot):
                    c.start()
            for c in chunk_copies(j, slot):
                c.wait()
            wg_c[j] = stage_in[slot, 0].astype(BF16)
            wu_c[j] = stage_in[slot, 1].astype(BF16)
            wd_c[j] = stage_out[slot].astype(BF16)
            y = chunk(xb, j)
            acc = y if acc is None else acc + y
        finish(acc, tile_rows)

    def steady(rows):
        xb = prepare_rows(rows)
        acc = None
        for j in range(nf):
            y = chunk(xb, j)
            acc = y if acc is None else acc + y
        finish(acc, rows)

    @pl.when(valid & (mode == TILE_FULL))
    def _():
        steady(tile_rows)

    if routed:
        @pl.when(valid & (mode == TILE_HALF))
        def _():
            steady(tile_rows // 2)

    @pl.when(jnp.logical_not(valid))
    def _():
        o_ref[...] = jnp.zeros_like(o_ref)


def expert_swiglu(x, gain, tile_expert, tile_mode, n_valid, wg, wu, wd, tile_rows, tf, out_dtype, pre_norm, name,
                  proj=None, routed=False):
    n_rows = x.shape[0]
    ne, d, f = wg.shape
    assert n_rows % tile_rows == 0 and f % tf == 0
    n_tiles = n_rows // tile_rows
    nf = f // tf
    pre_proj = proj is not None
    if pre_proj:
        a, wp = proj
        a_spec = pl.BlockSpec((tile_rows, a.shape[1]), lambda i, te, fi, nv: (jnp.minimum(i, nv[0] - 1), 0))
    else:
        a, wp = jnp.zeros((8, LANES), BF16), jnp.zeros((LANES, d), BF16)
        a_spec = pl.BlockSpec(a.shape, lambda i, te, fi, nv: (0, 0))
    grid_spec = pltpu.PrefetchScalarGridSpec(
        num_scalar_prefetch=3,
        grid=(n_tiles,),
        in_specs=[pl.BlockSpec((tile_rows, x.shape[1]), lambda i, te, fi, nv: (jnp.minimum(i, nv[0] - 1), 0)),
                  pl.BlockSpec((1, d), lambda i, te, fi, nv: (0, 0)),
                  a_spec,
                  pl.BlockSpec(wp.shape, lambda i, te, fi, nv: (0, 0)),
                  pl.BlockSpec(memory_space=pl.ANY),
                  pl.BlockSpec(memory_space=pl.ANY),
                  pl.BlockSpec(memory_space=pl.ANY)],
        out_specs=pl.BlockSpec((tile_rows, d), lambda i, te, fi, nv: (i, 0)),
        scratch_shapes=[pltpu.VMEM((nf, d, tf), BF16), pltpu.VMEM((nf, d, tf), BF16), pltpu.VMEM((nf, tf, d), BF16),
                        pltpu.VMEM((2, 2, d, tf), F32), pltpu.VMEM((2, tf, d), F32),
                        pltpu.SemaphoreType.DMA((2, 3)),
                        pltpu.VMEM((tile_rows, d) if pre_proj else (8, LANES), F32)],
    )
    return pl.pallas_call(
        functools.partial(_swiglu_kernel, pre_norm=pre_norm, pre_proj=pre_proj, routed=routed, tf=tf),
        out_shape=jax.ShapeDtypeStruct((n_rows, d), out_dtype),
        grid_spec=grid_spec,
        compiler_params=pltpu.CompilerParams(dimension_semantics=("arbitrary",), vmem_limit_bytes=EXPERT_VMEM_LIMIT),
        name=name,
    )(tile_expert, tile_mode, n_valid, x, gain.reshape(1, d).astype(F32), a, wp, wg, wu, wd)


def ffn_dense(x, gain, wg, wu, wd, tm, tf, proj=None):
    t = x.shape[0]
    n_tiles = t // tm
    tile_mode = jnp.full((n_tiles,), TILE_FULL, jnp.int32).at[0].set(TILE_FIRST)
    return expert_swiglu(x, gain, jnp.zeros((n_tiles,), jnp.int32), tile_mode, jnp.full((1,), n_tiles, jnp.int32),
                         wg[None], wu[None], wd[None], tm, tf, F32, True, "ffn_dense", proj=proj)


def _t5_bucket_np(dist):
    max_exact = N_BUCKETS // 2
    n = np.maximum(dist, 0)
    safe = np.maximum(n, 1).astype(np.float32)
    large = max_exact + (np.log(safe / max_exact) / np.log(MAX_DIST / max_exact)
                         * (N_BUCKETS - max_exact)).astype(np.int32)
    large = np.minimum(large, N_BUCKETS - 1)
    return np.where(n < max_exact, n, large).astype(np.int32)


LOG2E = float(np.log2(np.e))


def _bias_kernel(bucket_ref, valid_ref, rb_ref, o_ref):
    bucket = bucket_ref[...]
    for h in range(SW_HEADS):
        acc = jnp.zeros(bucket.shape, F32)
        for b in range(N_BUCKETS):
            acc = jnp.where(bucket == b, rb_ref[b, h], acc)
        for v in range(valid_ref.shape[0]):
            o_ref[v, h] = jnp.where(valid_ref[v] > 0, acc * LOG2E, NEG_INF)


def bias_table(rel_bias):
    qi = np.arange(WINDOW)[:, None] + WINDOW
    kj = np.arange(2 * WINDOW)[None, :]
    dist = qi - kj
    band = (dist >= 0) & (dist < WINDOW)
    valid = np.stack([band, band & (kj >= WINDOW)]).astype(np.int32)
    return pl.pallas_call(
        _bias_kernel,
        out_shape=jax.ShapeDtypeStruct((2, SW_HEADS, WINDOW, 2 * WINDOW), F32),
        in_specs=[pl.BlockSpec(memory_space=pltpu.VMEM), pl.BlockSpec(memory_space=pltpu.VMEM),
                  pl.BlockSpec(memory_space=pltpu.SMEM)],
        out_specs=pl.BlockSpec(memory_space=pltpu.VMEM),
        name="t5_bias_table",
    )(jnp.asarray(_t5_bucket_np(dist)), jnp.asarray(valid), rel_bias.astype(F32))


def _swa_kernel(q_ref, kvp_ref, kvc_ref, bias_ref, qn_ref, kn_ref, sink_ref, o_ref):
    blk, hd = WINDOW, SW_HD
    kv_w = SW_KV_HEADS * hd
    first = jnp.where(pl.program_id(1) == 0, 1, 0)
    gw = 2 * LANES
    gi = lax.broadcasted_iota(jnp.int32, (gw, gw), 0)
    gj = lax.broadcasted_iota(jnp.int32, (gw, gw), 1)
    group_ones = jnp.where((gi // hd) == (gj // hd), 1.0, 0.0).astype(BF16)
    lane = lax.broadcasted_iota(jnp.int32, (1, LANES), 1)
    low_half = lane < hd

    def head_norm(x, gain):
        cols = []
        for c0 in range(0, x.shape[1], gw):
            xc = x[:, c0:c0 + gw]
            ss = _dot((xc * xc).astype(BF16), group_ones)
            cols.append(xc * lax.rsqrt(ss * (1.0 / hd) + EPS))
        return jnp.concatenate(cols, axis=1) * gain

    def dup_half(x, half):
        swapped = pltpu.roll(x, hd, 1)
        return jnp.where(low_half == (half == 0), x, swapped)

    n_qb = q_ref.shape[0] // blk
    qn = head_norm(q_ref[...].astype(F32), qn_ref[...]) * ((hd ** -0.5) * LOG2E)
    half_sel = [jnp.where(low_half, 1.0, 0.0), jnp.where(low_half, 0.0, 1.0)]
    k_all = jnp.concatenate([kvp_ref[:, 0:kv_w], kvc_ref[:, 0:kv_w]], axis=0).astype(F32)
    kn = head_norm(k_all, kn_ref[...])
    v_all = jnp.concatenate([kvp_ref[:, kv_w:2 * kv_w], kvc_ref[:, kv_w:2 * kv_w]], axis=0).astype(F32)
    ks, vs = [], []
    for g in range(SW_KV_HEADS):
        c0 = (g // 2) * LANES
        ks.append(dup_half(kn[:, c0:c0 + LANES], g % 2).astype(BF16))
        vs.append(dup_half(v_all[:, c0:c0 + LANES], g % 2).astype(BF16))

    pairs = [(j, hq) for j in range(n_qb) for hq in range(SW_HEADS)]
    scores = {}
    for (j, hq) in pairs:
        c0 = (hq // 2) * LANES
        q_h = (qn[j * blk:(j + 1) * blk, c0:c0 + LANES] * half_sel[hq % 2]).astype(BF16)
        scores[(j, hq)] = _dot_nt(q_h, ks[hq // SW_GROUP][j * blk:(j + 2) * blk])
    probs = {}
    for (j, hq) in pairs:
        variant = first if j == 0 else 0
        s = scores[(j, hq)] + bias_ref[variant, hq]
        sink = sink_ref[hq] * LOG2E
        mx = jnp.maximum(jnp.max(s, axis=-1, keepdims=True), sink)
        p = jnp.exp2(s - mx)
        denom = jnp.sum(p, axis=-1, keepdims=True) + jnp.exp2(sink - mx)
        probs[(j, hq)] = (p / denom).astype(BF16)
    outs = {key: _dot(probs[key], vs[key[1] // SW_GROUP][key[0] * blk:(key[0] + 2) * blk]) for key in pairs}
    for j in range(n_qb):
        for c in range(SW_HEADS // 2):
            o_ref[j * blk:(j + 1) * blk, c * LANES:(c + 1) * LANES] = jnp.where(
                low_half, outs[(j, 2 * c)], outs[(j, 2 * c + 1)]).astype(o_ref.dtype)


def swa_attention(q, kv, bias, q_norm, k_norm, sinks, batch, seq):
    t = q.shape[0]
    blk = WINDOW
    step = SWA_QBLOCKS * blk
    assert seq % step == 0
    nb = seq // step
    qw = SW_HEADS * SW_HD
    kvw = 2 * SW_KV_HEADS * SW_HD
    return pl.pallas_call(
        _swa_kernel,
        out_shape=jax.ShapeDtypeStruct((t, qw), BF16),
        grid=(batch, nb),
        in_specs=[pl.BlockSpec((step, qw), lambda b, n: (b * nb + n, 0)),
                  pl.BlockSpec((blk, kvw), lambda b, n: (jnp.maximum((b * nb + n) * SWA_QBLOCKS - 1, b * nb * SWA_QBLOCKS), 0)),
                  pl.BlockSpec((step, kvw), lambda b, n: (b * nb + n, 0)),
                  pl.BlockSpec((2, SW_HEADS, blk, 2 * blk), lambda b, n: (0, 0, 0, 0)),
                  pl.BlockSpec((1, qw), lambda b, n: (0, 0)),
                  pl.BlockSpec((1, kvw // 2), lambda b, n: (0, 0)),
                  pl.BlockSpec(memory_space=pltpu.SMEM)],
        out_specs=pl.BlockSpec((step, qw), lambda b, n: (b * nb + n, 0)),
        compiler_params=_cparams(("parallel", "parallel")),
        name="swa_attention",
    )(q, kv, kv, bias, jnp.tile(q_norm.astype(F32), SW_HEADS).reshape(1, qw),
      jnp.tile(k_norm.astype(F32), SW_KV_HEADS).reshape(1, kvw // 2), sinks.astype(F32))


def _route_kernel(x_ref, a_ref, wp_ref, g_ref, wr_ref, h_ref, r_ref, wt_ref, tab_ref, cnt_ref,
                  sel_s, gw_s, cnt_s, start_s, run_s, *, tile_rows):
    ne = N_EXPERTS
    p = pl.program_id(0)
    i = pl.program_id(1)
    tm = x_ref.shape[0]
    sub = lax.broadcasted_iota(jnp.int32, (ne, tm), 0).astype(F32)

    def seg_rows(sel):
        n = jnp.sum(sel, axis=1, keepdims=True)
        return jnp.floor((n + (SEG_ALIGN - 1)) * (1.0 / SEG_ALIGN)) * SEG_ALIGN

    def excl_cumsum_experts(v):
        sub8 = lax.broadcasted_iota(jnp.int32, v.shape, 0)
        out = jnp.zeros_like(v)
        for e in range(ne - 1):
            out = out + jnp.where(sub8 > e, v[e:e + 1, :], 0.0)
        return out

    @pl.when(p == 0)
    def _():
        @pl.when(i == 0)
        def _():
            cnt_s[...] = jnp.zeros_like(cnt_s)

        x = x_ref[...] + _dot(a_ref[...], wp_ref[...])
        h_ref[...] = x
        ms = jnp.mean(x * x, axis=-1, keepdims=True)
        xn32 = (x * lax.rsqrt(ms + EPS)) * g_ref[...]
        xn_hi = xn32.astype(BF16)
        xn_lo = (xn32 - xn_hi.astype(F32)).astype(BF16)
        p_hi = _dot_nt(wr_ref[...], xn_hi)
        p_lo = _dot_nt(wr_ref[...], xn_lo)
        logits = p_hi[0:ne] + p_hi[ne:2 * ne] + p_lo[0:ne]
        m1 = jnp.max(logits, axis=0, keepdims=True)
        i1 = jnp.min(jnp.where(logits == m1, sub, float(ne)), axis=0, keepdims=True)
        l2 = jnp.where(sub == i1, -jnp.inf, logits)
        m2 = jnp.max(l2, axis=0, keepdims=True)
        i2 = jnp.min(jnp.where(l2 == m2, sub, float(ne)), axis=0, keepdims=True)
        e2 = jnp.exp(m2 - m1)
        w1 = 1.0 / (1.0 + e2)
        w2 = e2 / (1.0 + e2)
        sel = jnp.where((sub == i1) | (sub == i2), 1.0, 0.0)
        sel_s[i] = sel
        gw_s[i] = jnp.where(sub == i1, w1, jnp.where(sub == i2, w2, 0.0))
        cnt_s[...] += seg_rows(sel)

    @pl.when(p == 1)
    def _():
        @pl.when(i == 0)
        def _():
            cnt = cnt_s[...]
            padded = jnp.floor((cnt + (tile_rows - 1)) * (1.0 / tile_rows)) * tile_rows
            start_s[...] = excl_cumsum_experts(padded)
            run_s[...] = jnp.zeros_like(run_s)
            cnt_ref[...] = cnt

        sel = sel_s[i]
        gw = gw_s[i]
        ti = lax.broadcasted_iota(jnp.int32, (tm, tm), 0)
        tj = lax.broadcasted_iota(jnp.int32, (tm, tm), 1)
        tri = jnp.where(ti <= tj, 1.0, 0.0).astype(BF16)
        csum = _dot(sel.astype(BF16), tri)
        seg = jnp.broadcast_to(seg_rows(sel), run_s.shape)
        local0 = excl_cumsum_experts(seg)
        tab_ref[0, 0] = start_s[...] + run_s[...]
        tab_ref[0, 1] = seg
        tab_ref[0, 2] = local0
        run_s[...] += seg
        local_row = local0[:, 0:1] + csum - sel
        ia = jnp.min(jnp.where(sel > 0.0, sub, float(ne)), axis=0, keepdims=True)
        ib = jnp.max(jnp.where(sel > 0.0, sub, -1.0), axis=0, keepdims=True)
        pick_a = sub == ia
        pick_b = sub == ib
        rows = [jnp.sum(jnp.where(pick_a, local_row, 0.0), axis=0, keepdims=True),
                jnp.sum(jnp.where(pick_b, local_row, 0.0), axis=0, keepdims=True),
                jnp.sum(jnp.where(pick_a, gw, 0.0), axis=0, keepdims=True),
                jnp.sum(jnp.where(pick_b, gw, 0.0), axis=0, keepdims=True)]
        r_ref[...] = jnp.concatenate(rows + [jnp.zeros((ne - 4, tm), F32)], axis=0)
        wpad = jnp.concatenate(rows[2:4] + rows[0:2] + [jnp.zeros((LANES - 4, tm), F32)], axis=0)
        wt_ref[...] = wpad.T


def moe_route(x, a, wp, gain, w_router, tm, tile_rows):
    t, d = x.shape
    ne = w_router.shape[1]
    assert ne == N_EXPERTS
    w_hi = w_router.astype(BF16)
    w_lo = (w_router - w_hi.astype(F32)).astype(BF16)
    wr = jnp.concatenate([w_hi.T, w_lo.T], axis=0)
    tm = min(tm, t)
    nt = t // tm

    def row_map(p, i):
        return (i * (1 - p) + (nt - 1) * p, 0)

    return pl.pallas_call(
        functools.partial(_route_kernel, tile_rows=tile_rows),
        out_shape=(jax.ShapeDtypeStruct((t, d), F32),
                   jax.ShapeDtypeStruct((ne, t), F32), jax.ShapeDtypeStruct((t, LANES), F32),
                   jax.ShapeDtypeStruct((nt, 3, ne, LANES), F32), jax.ShapeDtypeStruct((ne, LANES), F32)),
        grid=(2, nt),
        in_specs=[pl.BlockSpec((tm, d), row_map),
                  pl.BlockSpec((tm, a.shape[1]), row_map),
                  pl.BlockSpec(wp.shape, lambda p, i: (0, 0)),
                  pl.BlockSpec((1, d), lambda p, i: (0, 0)),
                  pl.BlockSpec((2 * ne, d), lambda p, i: (0, 0))],
        out_specs=(pl.BlockSpec((tm, d), row_map),
                   pl.BlockSpec((ne, tm), lambda p, i: (0, i * p)),
                   pl.BlockSpec((tm, LANES), lambda p, i: (i * p, 0)),
                   pl.BlockSpec((1, 3, ne, LANES), lambda p, i: (i * p, 0, 0, 0)),
                   pl.BlockSpec((ne, LANES), lambda p, i: (0, 0))),
        scratch_shapes=[pltpu.VMEM((nt, ne, tm), F32), pltpu.VMEM((nt, ne, tm), F32),
                        pltpu.VMEM((ne, LANES), F32), pltpu.VMEM((ne, LANES), F32), pltpu.VMEM((ne, LANES), F32)],
        compiler_params=_cparams(("arbitrary", "arbitrary")),
        name="moe_route",
    )(x, a, wp, gain.reshape(1, d), wr)


def _segment_copies(tab_ref, i, e, local_ref, slot_ref, sem, to_slots):
    base = (i * N_EXPERTS + e) * 3
    slot0, rows, local0 = tab_ref[base], tab_ref[base + 1], tab_ref[base + 2]
    out = []
    done = 0
    size = MOE_TOKEN_TILE
    while size >= SEG_ALIGN:
        take = rows & size
        loc = local_ref.at[pl.ds(pl.multiple_of(local0 + done, SEG_ALIGN), size)]
        slt = slot_ref.at[pl.ds(pl.multiple_of(slot0 + done, SEG_ALIGN), size)]
        desc = pltpu.make_async_copy(loc, slt, sem) if to_slots else pltpu.make_async_copy(slt, loc, sem)
        out.append((take != 0, desc))
        done = done + take
        size //= 2
    return out


def _run_segment_copies(tab_ref, tile, slot, rows_s, slot_ref, sems, to_slots, action):
    for e in range(N_EXPERTS):
        for cond, desc in _segment_copies(tab_ref, tile, e, rows_s.at[slot], slot_ref, sems.at[slot], to_slots):
            @pl.when(cond)
            def _():
                getattr(desc, action)()


def _dispatch_kernel(tab_ref, zf_ref, x_ref, g_ref, r_ref, xs_ref, rows_s, zero_s, sem, zsem, *, tile_rows):
    i = pl.program_id(0)
    tm = x_ref.shape[0]
    n_local = rows_s.shape[1]

    @pl.when(i == 0)
    def _():
        zero_s[...] = jnp.zeros_like(zero_s)

        def zero_copy(e):
            row0 = pl.multiple_of(zf_ref[e], tile_rows)
            return pltpu.make_async_copy(zero_s, xs_ref.at[pl.ds(row0, tile_rows)], zsem)

        for e in range(zf_ref.shape[0]):
            @pl.when(zf_ref[e] >= 0)
            def _():
                zero_copy(e).start()
        for e in range(zf_ref.shape[0]):
            @pl.when(zf_ref[e] >= 0)
            def _():
                zero_copy(e).wait()

    x = x_ref[...]
    ms = jnp.mean(x * x, axis=-1, keepdims=True)
    xn = ((x * lax.rsqrt(ms + EPS)) * g_ref[...]).astype(BF16)
    row_id = lax.broadcasted_iota(jnp.int32, (n_local, tm), 0).astype(F32)
    onehot = jnp.where((row_id == r_ref[0:1, :]) | (row_id == r_ref[1:2, :]), 1.0, 0.0).astype(BF16)
    slot = lax.rem(i, 2)
    rows_s[slot] = _dot(onehot, xn)

    _run_segment_copies(tab_ref, i, slot, rows_s, xs_ref, sem, True, "start")

    @pl.when(i > 0)
    def _():
        _run_segment_copies(tab_ref, i - 1, 1 - slot, rows_s, xs_ref, sem, True, "wait")

    @pl.when(i == pl.num_programs(0) - 1)
    def _():
        _run_segment_copies(tab_ref, i, slot, rows_s, xs_ref, sem, True, "wait")


def moe_dispatch(x, gain, r, tab, zf_rows, n_slots, tm, tile_rows):
    t, d = x.shape
    nt = t // tm
    n_local = TOP_K * tm + N_EXPERTS * SEG_ALIGN
    grid_spec = pltpu.PrefetchScalarGridSpec(
        num_scalar_prefetch=2,
        grid=(nt,),
        in_specs=[pl.BlockSpec((tm, d), lambda i, tb, zf: (i, 0)),
                  pl.BlockSpec((1, d), lambda i, tb, zf: (0, 0)),
                  pl.BlockSpec((N_EXPERTS, tm), lambda i, tb, zf: (0, i))],
        out_specs=pl.BlockSpec(memory_space=pl.ANY),
        scratch_shapes=[pltpu.VMEM((2, n_local, d), F32), pltpu.VMEM((tile_rows, d), F32),
                        pltpu.SemaphoreType.DMA((2,)), pltpu.SemaphoreType.DMA],
    )
    return pl.pallas_call(
        functools.partial(_dispatch_kernel, tile_rows=tile_rows),
        out_shape=jax.ShapeDtypeStruct((n_slots, d), F32),
        grid_spec=grid_spec,
        compiler_params=_cparams(("arbitrary",)),
        name="moe_dispatch",
    )(tab, zf_rows, x, gain.reshape(1, d), r)


def _combine_kernel(tab_ref, h_ref, wt_ref, ys_ref, o_ref, rows_s, sems):
    i = pl.program_id(0)
    tm = h_ref.shape[0]
    n_local = rows_s.shape[1]
    slot = lax.rem(i, 2)

    def fetch(tile, into):
        rows_s[into] = jnp.zeros(rows_s.shape[1:], rows_s.dtype)
        _run_segment_copies(tab_ref, tile, into, rows_s, ys_ref, sems, False, "start")

    @pl.when(i == 0)
    def _():
        fetch(i, slot)

    @pl.when(i + 1 < pl.num_programs(0))
    def _():
        fetch(i + 1, 1 - slot)

    _run_segment_copies(tab_ref, i, slot, rows_s, ys_ref, sems, False, "wait")

    wt = wt_ref[...]
    y = rows_s[slot].astype(BF16)
    col_id = lax.broadcasted_iota(jnp.int32, (tm, n_local), 1).astype(F32)
    pick_a = jnp.where(col_id == wt[:, 2:3], 1.0, 0.0).astype(BF16)
    pick_b = jnp.where(col_id == wt[:, 3:4], 1.0, 0.0).astype(BF16)
    o_ref[...] = h_ref[...] + wt[:, 0:1] * _dot(pick_a, y) + wt[:, 1:2] * _dot(pick_b, y)


def moe_combine(h, wt, tab, ys, tm):
    t, d = h.shape
    nt = t // tm
    n_local = TOP_K * tm + N_EXPERTS * SEG_ALIGN
    grid_spec = pltpu.PrefetchScalarGridSpec(
        num_scalar_prefetch=1,
        grid=(nt,),
        in_specs=[pl.BlockSpec((tm, d), lambda i, tb: (i, 0)),
                  pl.BlockSpec((tm, LANES), lambda i, tb: (i, 0)),
                  pl.BlockSpec(memory_space=pl.ANY)],
        out_specs=pl.BlockSpec((tm, d), lambda i, tb: (i, 0)),
        scratch_shapes=[pltpu.VMEM((2, n_local, d), F32), pltpu.SemaphoreType.DMA((2,))],
    )
    return pl.pallas_call(
        _combine_kernel,
        out_shape=jax.ShapeDtypeStruct((t, d), F32),
        grid_spec=grid_spec,
        compiler_params=_cparams(("arbitrary",)),
        name="moe_combine",
    )(tab, h, wt, ys)


MOE_TILE_ROWS = 512
FFN_CHUNK = 512
MOE_TOKEN_TILE = 512
SEG_ALIGN = 8
TOP_K = 2


def moe_layer(x, a, wp, gain, w_router, wg, wu, wd):
    t, d = x.shape
    ne = w_router.shape[1]
    tr, tm = MOE_TILE_ROWS, MOE_TOKEN_TILE
    assert t % tm == 0 and ne == N_EXPERTS
    nt = t // tm
    n_tiles = -(-(TOP_K * t + nt * ne * (SEG_ALIGN - 1) + ne * (tr - 1)) // tr)
    n_slots = n_tiles * tr

    h, r, wt, tab, cnt = moe_route(x, a, wp, gain, w_router, tm, tr)
    tab = jnp.transpose(tab[:, :, :, 0], (0, 2, 1)).astype(jnp.int32).reshape(-1)

    counts = cnt[:, 0].astype(jnp.int32)
    padded = ((counts + (tr - 1)) // tr) * tr
    ends = jnp.cumsum(padded)
    n_valid = (ends[-1] // tr).astype(jnp.int32)
    tile_row0 = jnp.arange(n_tiles, dtype=jnp.int32) * tr
    tile_expert = jnp.sum((tile_row0[:, None] >= ends[None, :]).astype(jnp.int32), axis=1)
    tile_expert = jnp.minimum(tile_expert, ne - 1)
    tile_expert = jnp.where(jnp.arange(n_tiles) < n_valid, tile_expert, tile_expert[jnp.maximum(n_valid - 1, 0)])
    prev_expert = jnp.concatenate([jnp.full((1,), -1, jnp.int32), tile_expert[:-1]])
    rows_used = (ends - padded + counts)[tile_expert] - tile_row0
    tile_mode = jnp.where(tile_expert != prev_expert, TILE_FIRST,
                          jnp.where(rows_used <= tr // 2, TILE_HALF, TILE_FULL)).astype(jnp.int32)
    tail = jnp.arange(TOP_K * t // tr, n_tiles, dtype=jnp.int32)
    zf_rows = jnp.concatenate([jnp.where(padded > 0, ends - tr, -1),
                               jnp.where(tail >= n_valid, tail * tr, -1)]).astype(jnp.int32)

    xs = moe_dispatch(h, gain, r, tab, zf_rows, n_slots, tm, tr)
    ys = expert_swiglu(xs, gain, tile_expert, tile_mode, n_valid.reshape(1), wg, wu, wd, tr, FFN_CHUNK, F32, False,
                       "moe_experts", routed=True)
    return moe_combine(h, wt, tab, ys, tm)


def kernel(x, a_norm, a_w_in, a_conv, a_log_decay, a_dt_bias, a_out_norm, a_w_out, kv_norm, kv_w, k_norm,
           b_norm, b_w_q, q_norm, b_sinks, b_w_o, rel_bias, ffn_norm, dense_w_gate, dense_w_up, dense_w_down,
           moe_router, moe_w_gate, moe_w_up, moe_w_down):
    batch, seq, d = x.shape
    t = batch * seq
    nh, hd = LA_HEADS, LA_D
    main_w = 4 * nh * hd
    h0 = x.reshape(t, d)

    w_in = a_w_in[0]
    w_main = w_in[:, 0:main_w].astype(BF16)
    w_gate = jnp.zeros((d, LANES), BF16).at[:, 0:2 * nh].set(w_in[:, main_w:main_w + 2 * nh].astype(BF16))
    proj, gates = norm_matmul(h0, [(a_norm[0], w_main, BF16), (a_norm[0], w_gate, F32)], 512, "gdn_in_proj")
    o = gdn_core(proj, gates, a_conv[0], a_log_decay[0], a_dt_bias[0], a_out_norm[0], batch, seq)

    h2 = ffn_dense(h0, ffn_norm[0], dense_w_gate[0], dense_w_up[0], dense_w_down[0], MOE_TILE_ROWS, FFN_CHUNK,
                   proj=(o, a_w_out[0].astype(BF16)))

    kv, q = norm_matmul(h2, [(kv_norm, kv_w.astype(BF16), BF16), (b_norm[0], b_w_q[0].astype(BF16), BF16)],
                        1024, "qkv_proj")
    bias = bias_table(rel_bias)
    attn = swa_attention(q, kv, bias, q_norm[0], k_norm, b_sinks[0], batch, seq)

    h4 = moe_layer(h2, attn, b_w_o[0].astype(BF16), ffn_norm[1], moe_router[0], moe_w_gate[0], moe_w_up[0],
                   moe_w_down[0])
    return h4.reshape(batch, seq, d)
```

```python
import functools

import numpy as np
import jax
import jax.numpy as jnp
from jax import lax
from jax.experimental import pallas as pl
from jax.experimental.pallas import tpu as pltpu

F32 = jnp.float32
BF16 = jnp.bfloat16

EPS = 1e-6
NEG_INF = -1e30

LA_HEADS = 8
LA_D = 128
CONV_W = 4
CHUNK = 64
SW_HEADS = 16
SW_KV_HEADS = 4
SW_GROUP = SW_HEADS // SW_KV_HEADS
SW_HD = 64
WINDOW = 128
SWA_QBLOCKS = 2
N_BUCKETS = 32
MAX_DIST = 128
N_EXPERTS = 8
TOP_K = 2
LOG2E = float(np.log2(np.e))

LANES = 128
SEG_ALIGN = 8
GDN_BLOCK = 2 * CHUNK
HALO = 16
MOE_TILE_ROWS = 512
MOE_TOKEN_TILE = 512
FFN_CHUNK = 512
IN_PROJ_TILE = 1024
QKV_PROJ_TILE = 1024

VMEM_LIMIT = 56 * 1024 * 1024
EXPERT_VMEM_LIMIT = 60 * 1024 * 1024


def _cparams(sem):
    return pltpu.CompilerParams(dimension_semantics=sem, vmem_limit_bytes=VMEM_LIMIT)


def _silu(x):
    return x * (1.0 / (1.0 + jnp.exp(-x)))


def _dot(a, b):
    return jnp.dot(a, b, preferred_element_type=F32)


def _dot_nt(a, b):
    return lax.dot_general(a, b, (((1,), (1,)), ((), ())), preferred_element_type=F32)


def _norm_matmul_kernel(*refs, n_groups):
    x_ref = refs[0]
    g_refs = refs[1:1 + n_groups]
    w_refs = refs[1 + n_groups:1 + 2 * n_groups]
    o_refs = refs[1 + 2 * n_groups:1 + 3 * n_groups]
    x = x_ref[...]
    xr = x * lax.rsqrt(jnp.mean(x * x, axis=-1, keepdims=True) + EPS)
    for g_ref, w_ref, o_ref in zip(g_refs, w_refs, o_refs):
        o_ref[...] = _dot((xr * g_ref[...]).astype(BF16), w_ref[...]).astype(o_ref.dtype)


def norm_matmul(x, groups, tm, name):
    t, d = x.shape
    tm = min(tm, t)
    assert t % tm == 0
    gains = [g.reshape(1, d).astype(F32) for g, _, _ in groups]
    ws = [w for _, w, _ in groups]
    return pl.pallas_call(
        functools.partial(_norm_matmul_kernel, n_groups=len(groups)),
        out_shape=[jax.ShapeDtypeStruct((t, w.shape[1]), dt) for _, w, dt in groups],
        grid=(t // tm,),
        in_specs=([pl.BlockSpec((tm, d), lambda i: (i, 0))]
                  + [pl.BlockSpec((1, d), lambda i: (0, 0)) for _ in groups]
                  + [pl.BlockSpec(w.shape, lambda i: (0, 0)) for w in ws]),
        out_specs=[pl.BlockSpec((tm, w.shape[1]), lambda i: (i, 0)) for w in ws],
        compiler_params=_cparams(("parallel",)),
        name=name,
    )(x, *gains, *ws)


def _gdn_kernel(proj_ref, gates_ref, convw_ref, hp_ref, onorm_ref, o_ref,
                xs_ref, state_ref, q_s, k_s, v_s, z_s, gc_s, gct_s, beta_s):
    n = pl.program_id(1)

    @pl.when(n == 0)
    def _():
        xs_ref[0:HALO, :] = jnp.zeros((HALO, xs_ref.shape[1]), xs_ref.dtype)
        for ref in (q_s, k_s, v_s, z_s, gc_s, gct_s, beta_s):
            ref[1] = jnp.zeros(ref.shape[1:], ref.dtype)

    @pl.when(n <= 1)
    def _():
        state_ref[...] = jnp.zeros_like(state_ref)

    args = (proj_ref, gates_ref, convw_ref, hp_ref, onorm_ref, o_ref, xs_ref, state_ref,
            q_s, k_s, v_s, z_s, gc_s, gct_s, beta_s)

    @pl.when(lax.rem(n, 2) == 0)
    def _():
        _gdn_step(*args, slot_w=0, slot_r=1)

    @pl.when(lax.rem(n, 2) == 1)
    def _():
        _gdn_step(*args, slot_w=1, slot_r=0)


def _gdn_step(proj_ref, gates_ref, convw_ref, hp_ref, onorm_ref, o_ref, xs_ref, state_ref,
              q_s, k_s, v_s, z_s, gc_s, gct_s, beta_s, *, slot_w, slot_r):
    nh, d, c = LA_HEADS, LA_D, CHUNK
    blk = GDN_BLOCK
    qkv_w = 3 * nh * d

    gc = gc_s[slot_r]
    gc_t = gct_s[slot_r]
    beta = beta_s[slot_r]

    xs_ref[HALO:HALO + blk, :] = proj_ref[:, 0:qkv_w]

    def front_gates():
        _gdn_front_gates(gates_ref, hp_ref, gc_s, gct_s, beta_s, slot_w)

    ci = lax.broadcasted_iota(jnp.int32, (c, c), 0)
    cj = lax.broadcasted_iota(jnp.int32, (c, c), 1)
    lower_incl = ci >= cj
    strict = ci > cj
    eye_c = jnp.where(ci == cj, 1.0, 0.0).astype(F32)
    di = lax.broadcasted_iota(jnp.int32, (d, d), 0)
    dj = lax.broadcasted_iota(jnp.int32, (d, d), 1)
    eye_d = jnp.where(di == dj, 1.0, 0.0).astype(BF16)

    onorm = onorm_ref[...]

    n_shift = CONV_W - 1
    sr = lax.broadcasted_iota(jnp.int32, (n_shift * blk, HALO + blk), 0)
    sc = lax.broadcasted_iota(jnp.int32, (n_shift * blk, HALO + blk), 1)
    shift_mat = jnp.where(sc == HALO + (sr % blk) - (sr // blk + 1), 1.0, 0.0).astype(BF16)
    pair_w = 2 * d

    def conv_silu(col0):
        cols = slice(col0, col0 + pair_w)
        shifted = _dot(shift_mat, xs_ref[:, cols])
        acc = convw_ref[CONV_W - 1:CONV_W, cols] * xs_ref[HALO:HALO + blk, cols].astype(F32)
        for s in range(1, CONV_W):
            acc = acc + convw_ref[CONV_W - 1 - s:CONV_W - s, cols] * shifted[(s - 1) * blk:s * blk]
        return _silu(acc)

    def front_pair(hp):
        c0 = hp * pair_w
        qf = conv_silu(c0)
        kf = conv_silu(nh * d + c0)
        v_s[slot_w, :, c0:c0 + pair_w] = conv_silu(2 * nh * d + c0)
        for half in range(2):
            lo, hi = half * d, (half + 1) * d
            qh, kh = qf[:, lo:hi], kf[:, lo:hi]
            q_s[slot_w, :, c0 + lo:c0 + hi] = qh * (lax.rsqrt(jnp.sum(qh * qh, axis=-1, keepdims=True) + EPS)
                                                    * (d ** -0.5))
            k_s[slot_w, :, c0 + lo:c0 + hi] = kh * lax.rsqrt(jnp.sum(kh * kh, axis=-1, keepdims=True) + EPS)
        z_s[slot_w, :, c0:c0 + pair_w] = proj_ref[:, qkv_w + c0:qkv_w + c0 + pair_w]

    front_tasks = [front_gates] + [functools.partial(front_pair, hp) for hp in range(nh // 2)]

    def run_front_task():
        if front_tasks:
            front_tasks.pop(0)()

    n_ck = blk // c
    chains = [(h, ck) for h in range(nh) for ck in range(n_ck)]

    st = {}
    for (h, ck) in chains:
        r = ck * c
        q = q_s[slot_r, r:r + c, h * d:(h + 1) * d]
        k = k_s[slot_r, r:r + c, h * d:(h + 1) * d]
        v = v_s[slot_r, r:r + c, h * d:(h + 1) * d]
        g_col = gc[r:r + c, nh + h:nh + h + 1]
        g_row = gc_t[nh + h:nh + h + 1, r:r + c]
        g_last = gc[r + c - 1:r + c, nh + h:nh + h + 1]
        b_col = beta[r:r + c, h:h + 1]
        decay = jnp.where(lower_incl, jnp.exp2(jnp.where(lower_incl, g_col - g_row, 0.0)), 0.0)
        k_beta = k * b_col
        e_col = jnp.exp2(g_col)
        lhs = jnp.concatenate([k_beta.astype(BF16), q.astype(BF16), eye_d], axis=0)
        kk = _dot_nt(lhs, k.astype(BF16))
        a_mat = jnp.where(strict, kk[0:c] * decay, 0.0)
        st[(h, ck)] = dict(
            a=a_mat, attn=(kk[c:2 * c] * decay).astype(BF16),
            k_tail_t=(kk[2 * c:2 * c + d] * jnp.exp2(g_last - g_row)).astype(BF16),
            rhs=jnp.concatenate([(v * b_col).astype(BF16), (k_beta * e_col).astype(BF16)], axis=1),
            qe=(q * e_col).astype(BF16), e_last=jnp.exp2(g_last))
    run_front_task()

    for key in chains:
        x_b = (-st[key]["a"]).astype(BF16)
        st[key]["y"] = _dot(x_b, x_b)
        st[key]["p"] = eye_c - st[key]["a"]
    run_front_task()
    n_levels = int(np.log2(c))
    for lvl in range(1, n_levels):
        for key in chains:
            y_b = st[key]["y"].astype(BF16)
            p = st[key]["p"]
            if lvl + 1 < n_levels:
                zz = _dot(jnp.concatenate([y_b, p.astype(BF16)], axis=0), y_b)
                st[key]["y"] = zz[0:c]
                st[key]["p"] = p + zz[c:2 * c]
            else:
                st[key]["p"] = p + _dot(p.astype(BF16), y_b)
        run_front_task()
    for key in chains:
        st[key]["uw"] = _dot(st[key]["p"].astype(BF16), st[key]["rhs"])
    run_front_task()

    for ck in range(n_ck):
        r = ck * c
        s_old = [state_ref[h] for h in range(nh)]
        ws_qs = []
        for h in range(nh):
            cur = st[(h, ck)]
            lhs = jnp.concatenate([cur["uw"][:, d:2 * d].astype(BF16), cur["qe"]], axis=0)
            ws_qs.append(_dot(lhs, s_old[h].astype(BF16)))
        run_front_task()
        for h in range(nh):
            cur = st[(h, ck)]
            v_new = cur["uw"][:, 0:d] - ws_qs[h][0:c]
            av_kv = _dot(jnp.concatenate([cur["attn"], cur["k_tail_t"]], axis=0), v_new.astype(BF16))
            state_ref[h] = s_old[h] * cur["e_last"] + av_kv[c:c + d]
            o = ws_qs[h][c:2 * c] + av_kv[0:c]
            o = (o * lax.rsqrt(jnp.mean(o * o, axis=-1, keepdims=True) + EPS)) * onorm
            z = z_s[slot_r, r:r + c, h * d:(h + 1) * d].astype(F32)
            o_ref[r:r + c, h * d:(h + 1) * d] = (o * _silu(z)).astype(o_ref.dtype)
    while front_tasks:
        run_front_task()

    xs_ref[0:HALO, :] = xs_ref[blk:blk + HALO, :]


def _gdn_front_gates(gates_ref, hp_ref, gc_s, gct_s, beta_s, slot_w):
    blk, c = GDN_BLOCK, CHUNK
    gates = gates_ref[...]
    a_log = hp_ref[0:1, :]
    dt_bias = hp_ref[1:2, :]
    beta = 1.0 / (1.0 + jnp.exp(-gates))
    sp_in = gates + dt_bias
    softplus = jnp.maximum(sp_in, 0.0) + jnp.log(1.0 + jnp.exp(-jnp.abs(sp_in)))
    g = (-jnp.exp(a_log) * softplus) * float(np.log2(np.e))

    row = lax.broadcasted_iota(jnp.int32, (blk, blk), 0)
    col = lax.broadcasted_iota(jnp.int32, (blk, blk), 1)
    tri = jnp.where((row >= col) & ((row // c) == (col // c)), 1.0, 0.0).astype(BF16)
    g_hi = g.astype(BF16)
    g_r1 = g - g_hi.astype(F32)
    g_mid = g_r1.astype(BF16)
    g_lo = (g_r1 - g_mid.astype(F32)).astype(BF16)
    gc = _dot(tri, g_hi) + _dot(tri, g_mid) + _dot(tri, g_lo)
    gc_s[slot_w] = gc
    gct_s[slot_w] = gc.T
    beta_s[slot_w] = beta


def gdn_core(proj, gates, conv_w, a_log, dt_bias, out_norm, batch, seq):
    t = proj.shape[0]
    nh, d = LA_HEADS, LA_D
    blk = GDN_BLOCK
    assert seq % blk == 0
    nblk = seq // blk
    hp = jnp.zeros((8, LANES), F32)
    hp = hp.at[0, nh:2 * nh].set(a_log.astype(F32)).at[1, nh:2 * nh].set(dt_bias.astype(F32))

    def in_map(b, n):
        return (b * nblk + jnp.minimum(n, nblk - 1), 0)

    return pl.pallas_call(
        _gdn_kernel,
        out_shape=jax.ShapeDtypeStruct((t, nh * d), BF16),
        grid=(batch, nblk + 1),
        in_specs=[pl.BlockSpec((blk, 4 * nh * d), in_map),
                  pl.BlockSpec((blk, LANES), in_map),
                  pl.BlockSpec((CONV_W, 3 * nh * d), lambda b, n: (0, 0)),
                  pl.BlockSpec((8, LANES), lambda b, n: (0, 0)),
                  pl.BlockSpec((1, d), lambda b, n: (0, 0))],
        out_specs=pl.BlockSpec((blk, nh * d), lambda b, n: (b * nblk + jnp.maximum(n - 1, 0), 0)),
        scratch_shapes=[pltpu.VMEM((HALO + blk, 3 * nh * d), BF16),
                        pltpu.VMEM((nh, d, d), F32),
                        pltpu.VMEM((2, blk, nh * d), F32), pltpu.VMEM((2, blk, nh * d), F32),
                        pltpu.VMEM((2, blk, nh * d), F32), pltpu.VMEM((2, blk, nh * d), BF16),
                        pltpu.VMEM((2, blk, LANES), F32), pltpu.VMEM((2, LANES, blk), F32),
                        pltpu.VMEM((2, blk, LANES), F32)],
        compiler_params=_cparams(("arbitrary", "arbitrary")),
        name="gdn_core",
    )(proj, gates, conv_w.astype(F32), hp, out_norm.reshape(1, d).astype(F32))


TILE_FULL, TILE_FIRST, TILE_HALF = 0, 1, 2


def _swiglu_kernel(te_ref, mode_ref, nv_ref, x_ref, g_ref, a_ref, wp_ref, wg_hbm, wu_hbm, wd_hbm, o_ref,
                   wg_c, wu_c, wd_c, stage_in, stage_out, sems, xres_s, *, pre_norm, pre_proj, routed, tf):
    i = pl.program_id(0)
    nf = wg_c.shape[0]
    e = te_ref[i]
    valid = i < nv_ref[0]

    def chunk_copies(j, slot):
        cols = pl.ds(j * tf, tf)
        return (pltpu.make_async_copy(wg_hbm.at[e, :, cols], stage_in.at[slot, 0], sems.at[slot, 0]),
                pltpu.make_async_copy(wu_hbm.at[e, :, cols], stage_in.at[slot, 1], sems.at[slot, 1]),
                pltpu.make_async_copy(wd_hbm.at[e, cols, :], stage_out.at[slot], sems.at[slot, 2]))

    tile_rows = x_ref.shape[0]

    d_model = o_ref.shape[1]

    def prepare_rows(rows):
        x = x_ref[0:rows, 0:d_model].astype(F32)
        if pre_proj:
            x = x + _dot(a_ref[0:rows, :], wp_ref[...])
            xres_s[0:rows, :] = x
        if pre_norm:
            ms = jnp.mean(x * x, axis=-1, keepdims=True)
            x = (x * lax.rsqrt(ms + EPS)) * g_ref[...]
        return x.astype(BF16)

    def chunk(xb, j):
        hid = _silu(_dot(xb, wg_c[j])) * _dot(xb, wu_c[j])
        return _dot(hid.astype(BF16), wd_c[j])

    def finish(acc, rows):
        if pre_norm:
            res = xres_s[0:rows, :] if pre_proj else x_ref[0:rows, 0:d_model]
            acc = res + acc
        o_ref[0:rows, :] = acc.astype(o_ref.dtype)
        if rows < tile_rows:
            o_ref[rows:tile_rows, :] = jnp.zeros((tile_rows - rows, o_ref.shape[1]), o_ref.dtype)

    mode = mode_ref[i]

    @pl.when(valid & (mode == TILE_FIRST))
    def _():
        for c in chunk_copies(0, 0):
            c.start()
        xb = prepare_rows(tile_rows)
        acc = None
        for j in range(nf):
            slot = j % 2
            if j + 1 < nf:
                for c in chunk_copies(j + 1, 1 - slot):
                    c.start()
            for c in chunk_copies(j, slot):
                c.wait()
            wg_c[j] = stage_in[slot, 0].astype(BF16)
            wu_c[j] = stage_in[slot, 1].astype(BF16)
            wd_c[j] = stage_out[slot].astype(BF16)
            y = chunk(xb, j)
            acc = y if acc is None else acc + y
        finish(acc, tile_rows)

    def steady(rows):
        xb = prepare_rows(rows)
        acc = None
        for j in range(nf):
            y = chunk(xb, j)
            acc = y if acc is None else acc + y
        finish(acc, rows)

    @pl.when(valid & (mode == TILE_FULL))
    def _():
        steady(tile_rows)

    if routed:
        @pl.when(valid & (mode == TILE_HALF))
        def _():
            steady(tile_rows // 2)

    @pl.when(jnp.logical_not(valid))
    def _():
        o_ref[...] = jnp.zeros_like(o_ref)


def expert_swiglu(x, gain, tile_expert, tile_mode, n_valid, wg, wu, wd, tile_rows, tf, out_dtype, pre_norm, name,
                  proj=None, routed=False):
    n_rows = x.shape[0]
    ne, d, f = wg.shape
    assert n_rows % tile_rows == 0 and f % tf == 0
    n_tiles = n_rows // tile_rows
    nf = f // tf
    pre_proj = proj is not None
    if pre_proj:
        a, wp = proj
        a_spec = pl.BlockSpec((tile_rows, a.shape[1]), lambda i, te, fi, nv: (jnp.minimum(i, nv[0] - 1), 0))
    else:
        a, wp = jnp.zeros((8, LANES), BF16), jnp.zeros((LANES, d), BF16)
        a_spec = pl.BlockSpec(a.shape, lambda i, te, fi, nv: (0, 0))
    grid_spec = pltpu.PrefetchScalarGridSpec(
        num_scalar_prefetch=3,
        grid=(n_tiles,),
        in_specs=[pl.BlockSpec((tile_rows, x.shape[1]), lambda i, te, fi, nv: (jnp.minimum(i, nv[0] - 1), 0)),
                  pl.BlockSpec((1, d), lambda i, te, fi, nv: (0, 0)),
                  a_spec,
                  pl.BlockSpec(wp.shape, lambda i, te, fi, nv: (0, 0)),
                  pl.BlockSpec(memory_space=pl.ANY),
                  pl.BlockSpec(memory_space=pl.ANY),
                  pl.BlockSpec(memory_space=pl.ANY)],
        out_specs=pl.BlockSpec((tile_rows, d), lambda i, te, fi, nv: (i, 0)),
        scratch_shapes=[pltpu.VMEM((nf, d, tf), BF16), pltpu.VMEM((nf, d, tf), BF16), pltpu.VMEM((nf, tf, d), BF16),
                        pltpu.VMEM((2, 2, d, tf), F32), pltpu.VMEM((2, tf, d), F32),
                        pltpu.SemaphoreType.DMA((2, 3)),
                        pltpu.VMEM((tile_rows, d) if pre_proj else (8, LANES), F32)],
    )
    return pl.pallas_call(
        functools.partial(_swiglu_kernel, pre_norm=pre_norm, pre_proj=pre_proj, routed=routed, tf=tf),
        out_shape=jax.ShapeDtypeStruct((n_rows, d), out_dtype),
        grid_spec=grid_spec,
        compiler_params=pltpu.CompilerParams(dimension_semantics=("arbitrary",), vmem_limit_bytes=EXPERT_VMEM_LIMIT),
        name=name,
    )(tile_expert, tile_mode, n_valid, x, gain.reshape(1, d).astype(F32), a, wp, wg, wu, wd)


def ffn_dense(x, gain, wg, wu, wd, tm, tf, proj=None):
    t = x.shape[0]
    n_tiles = t // tm
    tile_mode = jnp.full((n_tiles,), TILE_FULL, jnp.int32).at[0].set(TILE_FIRST)
    return expert_swiglu(x, gain, jnp.zeros((n_tiles,), jnp.int32), tile_mode, jnp.full((1,), n_tiles, jnp.int32),
                         wg[None], wu[None], wd[None], tm, tf, F32, True, "ffn_dense", proj=proj)


def _t5_bucket_np(dist):
    max_exact = N_BUCKETS // 2
    n = np.maximum(dist, 0)
    safe = np.maximum(n, 1).astype(np.float32)
    large = max_exact + (np.log(safe / max_exact) / np.log(MAX_DIST / max_exact)
                         * (N_BUCKETS - max_exact)).astype(np.int32)
    large = np.minimum(large, N_BUCKETS - 1)
    return np.where(n < max_exact, n, large).astype(np.int32)


def _bias_kernel(bucket_ref, valid_ref, rb_ref, o_ref):
    bucket = bucket_ref[...]
    for h in range(SW_HEADS):
        acc = jnp.zeros(bucket.shape, F32)
        for b in range(N_BUCKETS):
            acc = jnp.where(bucket == b, rb_ref[b, h], acc)
        for v in range(valid_ref.shape[0]):
            o_ref[v, h] = jnp.where(valid_ref[v] > 0, acc * LOG2E, NEG_INF)


def bias_table(rel_bias):
    qi = np.arange(WINDOW)[:, None] + WINDOW
    kj = np.arange(2 * WINDOW)[None, :]
    dist = qi - kj
    band = (dist >= 0) & (dist < WINDOW)
    valid = np.stack([band, band & (kj >= WINDOW)]).astype(np.int32)
    return pl.pallas_call(
        _bias_kernel,
        out_shape=jax.ShapeDtypeStruct((2, SW_HEADS, WINDOW, 2 * WINDOW), F32),
        in_specs=[pl.BlockSpec(memory_space=pltpu.VMEM), pl.BlockSpec(memory_space=pltpu.VMEM),
                  pl.BlockSpec(memory_space=pltpu.SMEM)],
        out_specs=pl.BlockSpec(memory_space=pltpu.VMEM),
        name="t5_bias_table",
    )(jnp.asarray(_t5_bucket_np(dist)), jnp.asarray(valid), rel_bias.astype(F32))


def _swa_kernel(q_ref, kvp_ref, kvc_ref, bias_ref, qn_ref, kn_ref, sink_ref, o_ref):
    blk, hd = WINDOW, SW_HD
    kv_w = SW_KV_HEADS * hd
    first = jnp.where(pl.program_id(1) == 0, 1, 0)
    gw = 2 * LANES
    gi = lax.broadcasted_iota(jnp.int32, (gw, gw), 0)
    gj = lax.broadcasted_iota(jnp.int32, (gw, gw), 1)
    group_ones = jnp.where((gi // hd) == (gj // hd), 1.0, 0.0).astype(BF16)
    lane = lax.broadcasted_iota(jnp.int32, (1, LANES), 1)
    low_half = lane < hd

    def head_norm(x, gain):
        cols = []
        for c0 in range(0, x.shape[1], gw):
            xc = x[:, c0:c0 + gw]
            ss = _dot((xc * xc).astype(BF16), group_ones)
            cols.append(xc * lax.rsqrt(ss * (1.0 / hd) + EPS))
        return jnp.concatenate(cols, axis=1) * gain

    def dup_half(x, half):
        swapped = pltpu.roll(x, hd, 1)
        return jnp.where(low_half == (half == 0), x, swapped)

    n_qb = q_ref.shape[0] // blk
    qn = head_norm(q_ref[...].astype(F32), qn_ref[...]) * ((hd ** -0.5) * LOG2E)
    half_sel = [jnp.where(low_half, 1.0, 0.0), jnp.where(low_half, 0.0, 1.0)]
    k_all = jnp.concatenate([kvp_ref[:, 0:kv_w], kvc_ref[:, 0:kv_w]], axis=0).astype(F32)
    kn = head_norm(k_all, kn_ref[...])
    v_all = jnp.concatenate([kvp_ref[:, kv_w:2 * kv_w], kvc_ref[:, kv_w:2 * kv_w]], axis=0).astype(F32)
    ks, vs = [], []
    for g in range(SW_KV_HEADS):
        c0 = (g // 2) * LANES
        ks.append(dup_half(kn[:, c0:c0 + LANES], g % 2).astype(BF16))
        vs.append(dup_half(v_all[:, c0:c0 + LANES], g % 2).astype(BF16))

    pairs = [(j, hq) for j in range(n_qb) for hq in range(SW_HEADS)]
    scores = {}
    for (j, hq) in pairs:
        c0 = (hq // 2) * LANES
        q_h = (qn[j * blk:(j + 1) * blk, c0:c0 + LANES] * half_sel[hq % 2]).astype(BF16)
        scores[(j, hq)] = _dot_nt(q_h, ks[hq // SW_GROUP][j * blk:(j + 2) * blk])
    probs = {}
    for (j, hq) in pairs:
        variant = first if j == 0 else 0
        s = scores[(j, hq)] + bias_ref[variant, hq]
        sink = sink_ref[hq] * LOG2E
        mx = jnp.maximum(jnp.max(s, axis=-1, keepdims=True), sink)
        p = jnp.exp2(s - mx)
        denom = jnp.sum(p, axis=-1, keepdims=True) + jnp.exp2(sink - mx)
        probs[(j, hq)] = (p / denom).astype(BF16)
    outs = {key: _dot(probs[key], vs[key[1] // SW_GROUP][key[0] * blk:(key[0] + 2) * blk]) for key in pairs}
    for j in range(n_qb):
        for c in range(SW_HEADS // 2):
            o_ref[j * blk:(j + 1) * blk, c * LANES:(c + 1) * LANES] = jnp.where(
                low_half, outs[(j, 2 * c)], outs[(j, 2 * c + 1)]).astype(o_ref.dtype)


def swa_attention(q, kv, bias, q_norm, k_norm, sinks, batch, seq):
    t = q.shape[0]
    blk = WINDOW
    step = SWA_QBLOCKS * blk
    assert seq % step == 0
    nb = seq // step
    qw = SW_HEADS * SW_HD
    kvw = 2 * SW_KV_HEADS * SW_HD
    return pl.pallas_call(
        _swa_kernel,
        out_shape=jax.ShapeDtypeStruct((t, qw), BF16),
        grid=(batch, nb),
        in_specs=[pl.BlockSpec((step, qw), lambda b, n: (b * nb + n, 0)),
                  pl.BlockSpec((blk, kvw), lambda b, n: (jnp.maximum((b * nb + n) * SWA_QBLOCKS - 1, b * nb * SWA_QBLOCKS), 0)),
                  pl.BlockSpec((step, kvw), lambda b, n: (b * nb + n, 0)),
                  pl.BlockSpec((2, SW_HEADS, blk, 2 * blk), lambda b, n: (0, 0, 0, 0)),
                  pl.BlockSpec((1, qw), lambda b, n: (0, 0)),
                  pl.BlockSpec((1, kvw // 2), lambda b, n: (0, 0)),
                  pl.BlockSpec(memory_space=pltpu.SMEM)],
        out_specs=pl.BlockSpec((step, qw), lambda b, n: (b * nb + n, 0)),
        compiler_params=_cparams(("parallel", "parallel")),
        name="swa_attention",
    )(q, kv, kv, bias, jnp.tile(q_norm.astype(F32), SW_HEADS).reshape(1, qw),
      jnp.tile(k_norm.astype(F32), SW_KV_HEADS).reshape(1, kvw // 2), sinks.astype(F32))


def _route_kernel(x_ref, a_ref, wp_ref, g_ref, wr_ref, h_ref, r_ref, wt_ref, tab_ref, cnt_ref,
                  sel_s, gw_s, cnt_s, start_s, run_s, *, tile_rows):
    ne = N_EXPERTS
    p = pl.program_id(0)
    i = pl.program_id(1)
    tm = x_ref.shape[0]
    sub = lax.broadcasted_iota(jnp.int32, (ne, tm), 0).astype(F32)

    def seg_rows(sel):
        n = jnp.sum(sel, axis=1, keepdims=True)
        return jnp.floor((n + (SEG_ALIGN - 1)) * (1.0 / SEG_ALIGN)) * SEG_ALIGN

    def excl_cumsum_experts(v):
        sub8 = lax.broadcasted_iota(jnp.int32, v.shape, 0)
        out = jnp.zeros_like(v)
        for e in range(ne - 1):
            out = out + jnp.where(sub8 > e, v[e:e + 1, :], 0.0)
        return out

    @pl.when(p == 0)
    def _():
        @pl.when(i == 0)
        def _():
            cnt_s[...] = jnp.zeros_like(cnt_s)

        x = x_ref[...] + _dot(a_ref[...], wp_ref[...])
        h_ref[...] = x
        ms = jnp.mean(x * x, axis=-1, keepdims=True)
        xn32 = (x * lax.rsqrt(ms + EPS)) * g_ref[...]
        xn_hi = xn32.astype(BF16)
        xn_lo = (xn32 - xn_hi.astype(F32)).astype(BF16)
        p_hi = _dot_nt(wr_ref[...], xn_hi)
        p_lo = _dot_nt(wr_ref[...], xn_lo)
        logits = p_hi[0:ne] + p_hi[ne:2 * ne] + p_lo[0:ne]
        m1 = jnp.max(logits, axis=0, keepdims=True)
        i1 = jnp.min(jnp.where(logits == m1, sub, float(ne)), axis=0, keepdims=True)
        l2 = jnp.where(sub == i1, -jnp.inf, logits)
        m2 = jnp.max(l2, axis=0, keepdims=True)
        i2 = jnp.min(jnp.where(l2 == m2, sub, float(ne)), axis=0, keepdims=True)
        e2 = jnp.exp(m2 - m1)
        w1 = 1.0 / (1.0 + e2)
        w2 = e2 / (1.0 + e2)
        sel = jnp.where((sub == i1) | (sub == i2), 1.0, 0.0)
        sel_s[i] = sel
        gw_s[i] = jnp.where(sub == i1, w1, jnp.where(sub == i2, w2, 0.0))
        cnt_s[...] += seg_rows(sel)

    @pl.when(p == 1)
    def _():
        @pl.when(i == 0)
        def _():
            cnt = cnt_s[...]
            padded = jnp.floor((cnt + (tile_rows - 1)) * (1.0 / tile_rows)) * tile_rows
            start_s[...] = excl_cumsum_experts(padded)
            run_s[...] = jnp.zeros_like(run_s)
            cnt_ref[...] = cnt

        sel = sel_s[i]
        gw = gw_s[i]
        ti = lax.broadcasted_iota(jnp.int32, (tm, tm), 0)
        tj = lax.broadcasted_iota(jnp.int32, (tm, tm), 1)
        tri = jnp.where(ti <= tj, 1.0, 0.0).astype(BF16)
        csum = _dot(sel.astype(BF16), tri)
        seg = jnp.broadcast_to(seg_rows(sel), run_s.shape)
        local0 = excl_cumsum_experts(seg)
        tab_ref[0, 0] = start_s[...] + run_s[...]
        tab_ref[0, 1] = seg
        tab_ref[0, 2] = local0
        run_s[...] += seg
        local_row = local0[:, 0:1] + csum - sel
        ia = jnp.min(jnp.where(sel > 0.0, sub, float(ne)), axis=0, keepdims=True)
        ib = jnp.max(jnp.where(sel > 0.0, sub, -1.0), axis=0, keepdims=True)
        pick_a = sub == ia
        pick_b = sub == ib
        rows = [jnp.sum(jnp.where(pick_a, local_row, 0.0), axis=0, keepdims=True),
                jnp.sum(jnp.where(pick_b, local_row, 0.0), axis=0, keepdims=True),
                jnp.sum(jnp.where(pick_a, gw, 0.0), axis=0, keepdims=True),
                jnp.sum(jnp.where(pick_b, gw, 0.0), axis=0, keepdims=True)]
        r_ref[...] = jnp.concatenate(rows + [jnp.zeros((ne - 4, tm), F32)], axis=0)
        wpad = jnp.concatenate(rows[2:4] + rows[0:2] + [jnp.zeros((LANES - 4, tm), F32)], axis=0)
        wt_ref[...] = wpad.T


def moe_route(x, a, wp, gain, w_router, tm, tile_rows):
    t, d = x.shape
    ne = w_router.shape[1]
    assert ne == N_EXPERTS
    w_hi = w_router.astype(BF16)
    w_lo = (w_router - w_hi.astype(F32)).astype(BF16)
    wr = jnp.concatenate([w_hi.T, w_lo.T], axis=0)
    tm = min(tm, t)
    nt = t // tm

    def row_map(p, i):
        return (i * (1 - p) + (nt - 1) * p, 0)

    return pl.pallas_call(
        functools.partial(_route_kernel, tile_rows=tile_rows),
        out_shape=(jax.ShapeDtypeStruct((t, d), F32),
                   jax.ShapeDtypeStruct((ne, t), F32), jax.ShapeDtypeStruct((t, LANES), F32),
                   jax.ShapeDtypeStruct((nt, 3, ne, LANES), F32), jax.ShapeDtypeStruct((ne, LANES), F32)),
        grid=(2, nt),
        in_specs=[pl.BlockSpec((tm, d), row_map),
                  pl.BlockSpec((tm, a.shape[1]), row_map),
                  pl.BlockSpec(wp.shape, lambda p, i: (0, 0)),
                  pl.BlockSpec((1, d), lambda p, i: (0, 0)),
                  pl.BlockSpec((2 * ne, d), lambda p, i: (0, 0))],
        out_specs=(pl.BlockSpec((tm, d), row_map),
                   pl.BlockSpec((ne, tm), lambda p, i: (0, i * p)),
                   pl.BlockSpec((tm, LANES), lambda p, i: (i * p, 0)),
                   pl.BlockSpec((1, 3, ne, LANES), lambda p, i: (i * p, 0, 0, 0)),
                   pl.BlockSpec((ne, LANES), lambda p, i: (0, 0))),
        scratch_shapes=[pltpu.VMEM((nt, ne, tm), F32), pltpu.VMEM((nt, ne, tm), F32),
                        pltpu.VMEM((ne, LANES), F32), pltpu.VMEM((ne, LANES), F32), pltpu.VMEM((ne, LANES), F32)],
        compiler_params=_cparams(("arbitrary", "arbitrary")),
        name="moe_route",
    )(x, a, wp, gain.reshape(1, d), wr)


def _segment_copies(tab_ref, i, e, local_ref, slot_ref, sem, to_slots):
    base = (i * N_EXPERTS + e) * 3
    slot0, rows, local0 = tab_ref[base], tab_ref[base + 1], tab_ref[base + 2]
    out = []
    done = 0
    size = MOE_TOKEN_TILE
    while size >= SEG_ALIGN:
        take = rows & size
        loc = local_ref.at[pl.ds(pl.multiple_of(local0 + done, SEG_ALIGN), size)]
        slt = slot_ref.at[pl.ds(pl.multiple_of(slot0 + done, SEG_ALIGN), size)]
        desc = pltpu.make_async_copy(loc, slt, sem) if to_slots else pltpu.make_async_copy(slt, loc, sem)
        out.append((take != 0, desc))
        done = done + take
        size //= 2
    return out


def _run_segment_copies(tab_ref, tile, slot, rows_s, slot_ref, sems, to_slots, action):
    for e in range(N_EXPERTS):
        for cond, desc in _segment_copies(tab_ref, tile, e, rows_s.at[slot], slot_ref, sems.at[slot], to_slots):
            @pl.when(cond)
            def _():
                getattr(desc, action)()


def _dispatch_kernel(tab_ref, zf_ref, x_ref, g_ref, r_ref, xs_ref, rows_s, zero_s, sem, zsem, *, tile_rows):
    i = pl.program_id(0)
    tm = x_ref.shape[0]
    n_local = rows_s.shape[1]

    @pl.when(i == 0)
    def _():
        zero_s[...] = jnp.zeros_like(zero_s)

        def zero_copy(e):
            row0 = pl.multiple_of(zf_ref[e], tile_rows)
            return pltpu.make_async_copy(zero_s, xs_ref.at[pl.ds(row0, tile_rows)], zsem)

        for e in range(zf_ref.shape[0]):
            @pl.when(zf_ref[e] >= 0)
            def _():
                zero_copy(e).start()
        for e in range(zf_ref.shape[0]):
            @pl.when(zf_ref[e] >= 0)
            def _():
                zero_copy(e).wait()

    x = x_ref[...]
    ms = jnp.mean(x * x, axis=-1, keepdims=True)
    xn = ((x * lax.rsqrt(ms + EPS)) * g_ref[...]).astype(BF16)
    row_id = lax.broadcasted_iota(jnp.int32, (n_local, tm), 0).astype(F32)
    onehot = jnp.where((row_id == r_ref[0:1, :]) | (row_id == r_ref[1:2, :]), 1.0, 0.0).astype(BF16)
    slot = lax.rem(i, 2)
    rows_s[slot] = _dot(onehot, xn)

    _run_segment_copies(tab_ref, i, slot, rows_s, xs_ref, sem, True, "start")

    @pl.when(i > 0)
    def _():
        _run_segment_copies(tab_ref, i - 1, 1 - slot, rows_s, xs_ref, sem, True, "wait")

    @pl.when(i == pl.num_programs(0) - 1)
    def _():
        _run_segment_copies(tab_ref, i, slot, rows_s, xs_ref, sem, True, "wait")


def moe_dispatch(x, gain, r, tab, zf_rows, n_slots, tm, tile_rows):
    t, d = x.shape
    nt = t // tm
    n_local = TOP_K * tm + N_EXPERTS * SEG_ALIGN
    grid_spec = pltpu.PrefetchScalarGridSpec(
        num_scalar_prefetch=2,
        grid=(nt,),
        in_specs=[pl.BlockSpec((tm, d), lambda i, tb, zf: (i, 0)),
                  pl.BlockSpec((1, d), lambda i, tb, zf: (0, 0)),
                  pl.BlockSpec((N_EXPERTS, tm), lambda i, tb, zf: (0, i))],
        out_specs=pl.BlockSpec(memory_space=pl.ANY),
        scratch_shapes=[pltpu.VMEM((2, n_local, d), F32), pltpu.VMEM((tile_rows, d), F32),
                        pltpu.SemaphoreType.DMA((2,)), pltpu.SemaphoreType.DMA],
    )
    return pl.pallas_call(
        functools.partial(_dispatch_kernel, tile_rows=tile_rows),
        out_shape=jax.ShapeDtypeStruct((n_slots, d), F32),
        grid_spec=grid_spec,
        compiler_params=_cparams(("arbitrary",)),
        name="moe_dispatch",
    )(tab, zf_rows, x, gain.reshape(1, d), r)


def _combine_kernel(tab_ref, h_ref, wt_ref, ys_ref, o_ref, rows_s, sems):
    i = pl.program_id(0)
    tm = h_ref.shape[0]
    n_local = rows_s.shape[1]
    slot = lax.rem(i, 2)

    def fetch(tile, into):
        rows_s[into] = jnp.zeros(rows_s.shape[1:], rows_s.dtype)
        _run_segment_copies(tab_ref, tile, into, rows_s, ys_ref, sems, False, "start")

    @pl.when(i == 0)
    def _():
        fetch(i, slot)

    @pl.when(i + 1 < pl.num_programs(0))
    def _():
        fetch(i + 1, 1 - slot)

    _run_segment_copies(tab_ref, i, slot, rows_s, ys_ref, sems, False, "wait")

    wt = wt_ref[...]
    y = rows_s[slot].astype(BF16)
    col_id = lax.broadcasted_iota(jnp.int32, (tm, n_local), 1).astype(F32)
    pick_a = jnp.where(col_id == wt[:, 2:3], 1.0, 0.0).astype(BF16)
    pick_b = jnp.where(col_id == wt[:, 3:4], 1.0, 0.0).astype(BF16)
    o_ref[...] = h_ref[...] + wt[:, 0:1] * _dot(pick_a, y) + wt[:, 1:2] * _dot(pick_b, y)


def moe_combine(h, wt, tab, ys, tm):
    t, d = h.shape
    nt = t // tm
    n_local = TOP_K * tm + N_EXPERTS * SEG_ALIGN
    grid_spec = pltpu.PrefetchScalarGridSpec(
        num_scalar_prefetch=1,
        grid=(nt,),
        in_specs=[pl.BlockSpec((tm, d), lambda i, tb: (i, 0)),
                  pl.BlockSpec((tm, LANES), lambda i, tb: (i, 0)),
                  pl.BlockSpec(memory_space=pl.ANY)],
        out_specs=pl.BlockSpec((tm, d), lambda i, tb: (i, 0)),
        scratch_shapes=[pltpu.VMEM((2, n_local, d), F32), pltpu.SemaphoreType.DMA((2,))],
    )
    return pl.pallas_call(
        _combine_kernel,
        out_shape=jax.ShapeDtypeStruct((t, d), F32),
        grid_spec=grid_spec,
        compiler_params=_cparams(("arbitrary",)),
        name="moe_combine",
    )(tab, h, wt, ys)


def moe_layer(x, a, wp, gain, w_router, wg, wu, wd):
    t, d = x.shape
    ne = w_router.shape[1]
    tr, tm = MOE_TILE_ROWS, MOE_TOKEN_TILE
    assert t % tm == 0 and ne == N_EXPERTS
    nt = t // tm
    n_tiles = -(-(TOP_K * t + nt * ne * (SEG_ALIGN - 1) + ne * (tr - 1)) // tr)
    n_slots = n_tiles * tr

    h, r, wt, tab, cnt = moe_route(x, a, wp, gain, w_router, tm, tr)
    tab = jnp.transpose(tab[:, :, :, 0], (0, 2, 1)).astype(jnp.int32).reshape(-1)

    counts = cnt[:, 0].astype(jnp.int32)
    padded = ((counts + (tr - 1)) // tr) * tr
    ends = jnp.cumsum(padded)
    n_valid = (ends[-1] // tr).astype(jnp.int32)
    tile_row0 = jnp.arange(n_tiles, dtype=jnp.int32) * tr
    tile_expert = jnp.sum((tile_row0[:, None] >= ends[None, :]).astype(jnp.int32), axis=1)
    tile_expert = jnp.minimum(tile_expert, ne - 1)
    tile_expert = jnp.where(jnp.arange(n_tiles) < n_valid, tile_expert, tile_expert[jnp.maximum(n_valid - 1, 0)])
    prev_expert = jnp.concatenate([jnp.full((1,), -1, jnp.int32), tile_expert[:-1]])
    rows_used = (ends - padded + counts)[tile_expert] - tile_row0
    tile_mode = jnp.where(tile_expert != prev_expert, TILE_FIRST,
                          jnp.where(rows_used <= tr // 2, TILE_HALF, TILE_FULL)).astype(jnp.int32)
    tail = jnp.arange(TOP_K * t // tr, n_tiles, dtype=jnp.int32)
    zf_rows = jnp.concatenate([jnp.where(padded > 0, ends - tr, -1),
                               jnp.where(tail >= n_valid, tail * tr, -1)]).astype(jnp.int32)

    xs = moe_dispatch(h, gain, r, tab, zf_rows, n_slots, tm, tr)
    ys = expert_swiglu(xs, gain, tile_expert, tile_mode, n_valid.reshape(1), wg, wu, wd, tr, FFN_CHUNK, F32, False,
                       "moe_experts", routed=True)
    return moe_combine(h, wt, tab, ys, tm)


def kernel(x, a_norm, a_w_in, a_conv, a_log_decay, a_dt_bias, a_out_norm, a_w_out, kv_norm, kv_w, k_norm,
           b_norm, b_w_q, q_norm, b_sinks, b_w_o, rel_bias, ffn_norm, dense_w_gate, dense_w_up, dense_w_down,
           moe_router, moe_w_gate, moe_w_up, moe_w_down):
    batch, seq, d = x.shape
    t = batch * seq
    nh, hd = LA_HEADS, LA_D
    main_w = 4 * nh * hd
    h0 = x.reshape(t, d)

    w_in = a_w_in[0]
    w_main = w_in[:, 0:main_w].astype(BF16)
    w_gate = jnp.zeros((d, LANES), BF16).at[:, 0:2 * nh].set(w_in[:, main_w:main_w + 2 * nh].astype(BF16))
    proj, gates = norm_matmul(h0, [(a_norm[0], w_main, BF16), (a_norm[0], w_gate, F32)], IN_PROJ_TILE,
                              "gdn_in_proj")
    o = gdn_core(proj, gates, a_conv[0], a_log_decay[0], a_dt_bias[0], a_out_norm[0], batch, seq)

    h2 = ffn_dense(h0, ffn_norm[0], dense_w_gate[0], dense_w_up[0], dense_w_down[0], MOE_TILE_ROWS, FFN_CHUNK,
                   proj=(o, a_w_out[0].astype(BF16)))

    kv, q = norm_matmul(h2, [(kv_norm, kv_w.astype(BF16), BF16), (b_norm[0], b_w_q[0].astype(BF16), BF16)],
                        QKV_PROJ_TILE, "qkv_proj")
    bias = bias_table(rel_bias)
    attn = swa_attention(q, kv, bias, q_norm[0], k_norm, b_sinks[0], batch, seq)

    h4 = moe_layer(h2, attn, b_w_o[0].astype(BF16), ffn_norm[1], moe_router[0], moe_w_gate[0], moe_w_up[0],
                   moe_w_down[0])
    return h4.reshape(batch, seq, d)
```

```python
import functools

import numpy as np
import jax
import jax.numpy as jnp
from jax import lax
from jax.experimental import pallas as pl
from jax.experimental.pallas import tpu as pltpu

F32 = jnp.float32
BF16 = jnp.bfloat16

EPS = 1e-6
NEG_INF = -1e30

LA_HEADS = 8
LA_D = 128
CONV_W = 4
CHUNK = 64
SW_HEADS = 16
SW_KV_HEADS = 4
SW_GROUP = SW_HEADS // SW_KV_HEADS
SW_HD = 64
WINDOW = 128
SWA_QBLOCKS = 2
N_BUCKETS = 32
MAX_DIST = 128
N_EXPERTS = 8
TOP_K = 2
LOG2E = float(np.log2(np.e))

LANES = 128
SEG_ALIGN = 8
GDN_BLOCK = 2 * CHUNK
HALO = 16
MOE_TILE_ROWS = 512
MOE_TOKEN_TILE = 512
FFN_CHUNK = 512
IN_PROJ_TILE = 1024
QKV_PROJ_TILE = 1024

VMEM_LIMIT = 56 * 1024 * 1024
EXPERT_VMEM_LIMIT = 60 * 1024 * 1024


def _cparams(sem):
    return pltpu.CompilerParams(dimension_semantics=sem, vmem_limit_bytes=VMEM_LIMIT)


def _silu(x):
    return x * (1.0 / (1.0 + jnp.exp(-x)))


def _dot(a, b):
    return jnp.dot(a, b, preferred_element_type=F32)


def _dot_nt(a, b):
    return lax.dot_general(a, b, (((1,), (1,)), ((), ())), preferred_element_type=F32)


def _norm_matmul_kernel(*refs, n_groups):
    x_ref = refs[0]
    g_refs = refs[1:1 + n_groups]
    w_refs = refs[1 + n_groups:1 + 2 * n_groups]
    o_refs = refs[1 + 2 * n_groups:1 + 3 * n_groups]
    x = x_ref[...]
    xr = x * lax.rsqrt(jnp.mean(x * x, axis=-1, keepdims=True) + EPS)
    for g_ref, w_ref, o_ref in zip(g_refs, w_refs, o_refs):
        o_ref[...] = _dot((xr * g_ref[...]).astype(BF16), w_ref[...]).astype(o_ref.dtype)


def norm_matmul(x, groups, tm, name):
    t, d = x.shape
    tm = min(tm, t)
    assert t % tm == 0
    gains = [g.reshape(1, d).astype(F32) for g, _, _ in groups]
    ws = [w for _, w, _ in groups]
    return pl.pallas_call(
        functools.partial(_norm_matmul_kernel, n_groups=len(groups)),
        out_shape=[jax.ShapeDtypeStruct((t, w.shape[1]), dt) for _, w, dt in groups],
        grid=(t // tm,),
        in_specs=([pl.BlockSpec((tm, d), lambda i: (i, 0))]
                  + [pl.BlockSpec((1, d), lambda i: (0, 0)) for _ in groups]
                  + [pl.BlockSpec(w.shape, lambda i: (0, 0)) for w in ws]),
        out_specs=[pl.BlockSpec((tm, w.shape[1]), lambda i: (i, 0)) for w in ws],
        compiler_params=_cparams(("parallel",)),
        name=name,
    )(x, *gains, *ws)


def _gdn_kernel(proj_ref, gates_ref, convw_ref, hp_ref, onorm_ref, o_ref,
                xs_ref, state_ref, q_s, k_s, v_s, z_s, gc_s, gct_s, beta_s):
    n = pl.program_id(1)

    @pl.when(n == 0)
    def _():
        xs_ref[0:HALO, :] = jnp.zeros((HALO, xs_ref.shape[1]), xs_ref.dtype)
        for ref in (q_s, k_s, v_s, z_s, gc_s, gct_s, beta_s):
            ref[1] = jnp.zeros(ref.shape[1:], ref.dtype)

    @pl.when(n <= 1)
    def _():
        state_ref[...] = jnp.zeros_like(state_ref)

    args = (proj_ref, gates_ref, convw_ref, hp_ref, onorm_ref, o_ref, xs_ref, state_ref,
            q_s, k_s, v_s, z_s, gc_s, gct_s, beta_s)

    @pl.when(lax.rem(n, 2) == 0)
    def _():
        _gdn_step(*args, slot_w=0, slot_r=1)

    @pl.when(lax.rem(n, 2) == 1)
    def _():
        _gdn_step(*args, slot_w=1, slot_r=0)


def _gdn_step(proj_ref, gates_ref, convw_ref, hp_ref, onorm_ref, o_ref, xs_ref, state_ref,
              q_s, k_s, v_s, z_s, gc_s, gct_s, beta_s, *, slot_w, slot_r):
    nh, d, c = LA_HEADS, LA_D, CHUNK
    blk = GDN_BLOCK
    qkv_w = 3 * nh * d

    gc = gc_s[slot_r]
    gc_t = gct_s[slot_r]
    beta = beta_s[slot_r]

    xs_ref[HALO:HALO + blk, :] = proj_ref[:, 0:qkv_w]

    def front_gates():
        _gdn_front_gates(gates_ref, hp_ref, gc_s, gct_s, beta_s, slot_w)

    di = lax.broadcasted_iota(jnp.int32, (d, d), 0)
    dj = lax.broadcasted_iota(jnp.int32, (d, d), 1)
    eye_d = jnp.where(di == dj, 1.0, 0.0).astype(BF16)

    onorm = onorm_ref[...]

    n_shift = CONV_W - 1
    sr = lax.broadcasted_iota(jnp.int32, (n_shift * blk, HALO + blk), 0)
    sc = lax.broadcasted_iota(jnp.int32, (n_shift * blk, HALO + blk), 1)
    shift_mat = jnp.where(sc == HALO + (sr % blk) - (sr // blk + 1), 1.0, 0.0).astype(BF16)
    pair_w = 2 * d

    def conv_silu(col0):
        cols = slice(col0, col0 + pair_w)
        shifted = _dot(shift_mat, xs_ref[:, cols])
        acc = convw_ref[CONV_W - 1:CONV_W, cols] * xs_ref[HALO:HALO + blk, cols].astype(F32)
        for s in range(1, CONV_W):
            acc = acc + convw_ref[CONV_W - 1 - s:CONV_W - s, cols] * shifted[(s - 1) * blk:s * blk]
        return _silu(acc)

    def front_pair(hp):
        c0 = hp * pair_w
        qf = conv_silu(c0)
        kf = conv_silu(nh * d + c0)
        v_s[slot_w, :, c0:c0 + pair_w] = conv_silu(2 * nh * d + c0)
        for half in range(2):
            lo, hi = half * d, (half + 1) * d
            qh, kh = qf[:, lo:hi], kf[:, lo:hi]
            q_s[slot_w, :, c0 + lo:c0 + hi] = qh * (lax.rsqrt(jnp.sum(qh * qh, axis=-1, keepdims=True) + EPS)
                                                    * (d ** -0.5))
            k_s[slot_w, :, c0 + lo:c0 + hi] = kh * lax.rsqrt(jnp.sum(kh * kh, axis=-1, keepdims=True) + EPS)
        z_s[slot_w, :, c0:c0 + pair_w] = proj_ref[:, qkv_w + c0:qkv_w + c0 + pair_w]

    front_tasks = [front_gates] + [functools.partial(front_pair, hp) for hp in range(nh // 2)]

    def run_front_task():
        if front_tasks:
            front_tasks.pop(0)()

    assert blk == 2 * c and 2 * c == LANES and d == LANES
    si = lax.broadcasted_iota(jnp.int32, (c, 2 * c), 0)
    sl = lax.broadcasted_iota(jnp.int32, (c, 2 * c), 1)
    first_chunk = sl < c
    sj = jnp.where(first_chunk, sl, sl - c)
    lower_incl = si >= sj
    strict = si > sj
    eye_pair = jnp.where(si == sj, 1.0, 0.0).astype(F32)
    lane_row = lax.broadcasted_iota(jnp.int32, (1, 2 * c), 1) < c
    zeros_cd = jnp.zeros((c, d), BF16)

    def block_diag(m):
        return jnp.concatenate([jnp.where(first_chunk, m, 0.0), jnp.where(first_chunk, 0.0, m)], axis=0).astype(BF16)

    st = []
    for h in range(nh):
        hs = slice(h * d, (h + 1) * d)
        q = q_s[slot_r, :, hs]
        k = k_s[slot_r, :, hs]
        v = v_s[slot_r, :, hs]
        g_col = gc[:, nh + h:nh + h + 1]
        g_row = gc_t[nh + h:nh + h + 1, :]
        b_col = beta[:, h:h + 1]
        g_col_pair = jnp.where(first_chunk, g_col[0:c], g_col[c:2 * c])
        g_last = jnp.where(lane_row, g_col[c - 1:c], g_col[2 * c - 1:2 * c])
        decay = jnp.where(lower_incl, jnp.exp2(jnp.where(lower_incl, g_col_pair - g_row, 0.0)), 0.0)
        k_beta = k * b_col
        e_col = jnp.exp2(g_col)
        kb, qb, kbf = k_beta.astype(BF16), q.astype(BF16), k.astype(BF16)
        lhs = jnp.concatenate([jnp.concatenate([kb[0:c], kb[c:2 * c]], axis=1),
                               jnp.concatenate([qb[0:c], qb[c:2 * c]], axis=1),
                               jnp.concatenate([eye_d, eye_d], axis=1)], axis=0)
        k_diag = jnp.concatenate([jnp.concatenate([kbf[0:c], zeros_cd], axis=1),
                                  jnp.concatenate([zeros_cd, kbf[c:2 * c]], axis=1)], axis=0)
        kk = _dot_nt(lhs, k_diag)
        vb, kbe = (v * b_col).astype(BF16), (k_beta * e_col).astype(BF16)
        zeros_2 = jnp.zeros((c, 2 * d), BF16)
        st.append(dict(
            a=jnp.where(strict, kk[0:c] * decay, 0.0),
            attn=kk[c:2 * c] * decay,
            k_tail_t=kk[2 * c:2 * c + d] * jnp.exp2(g_last - g_row),
            rhs=jnp.concatenate([jnp.concatenate([vb[0:c], kbe[0:c], zeros_2], axis=1),
                                 jnp.concatenate([zeros_2, vb[c:2 * c], kbe[c:2 * c]], axis=1)], axis=0),
            qe=(q * e_col).astype(BF16),
            e_last=[jnp.exp2(g_col[c - 1:c]), jnp.exp2(g_col[2 * c - 1:2 * c])]))
    run_front_task()

    for cur in st:
        x = -cur["a"]
        cur["y"] = _dot(x.astype(BF16), block_diag(x))
        cur["p"] = eye_pair + x
    run_front_task()
    n_levels = int(np.log2(c))
    for lvl in range(1, n_levels):
        for cur in st:
            y_bd = block_diag(cur["y"])
            p = cur["p"]
            if lvl + 1 < n_levels:
                zz = _dot(jnp.concatenate([cur["y"].astype(BF16), p.astype(BF16)], axis=0), y_bd)
                cur["y"] = zz[0:c]
                cur["p"] = p + zz[c:2 * c]
            else:
                cur["p"] = p + _dot(p.astype(BF16), y_bd)
        run_front_task()
    for cur in st:
        cur["uw"] = _dot(cur["p"].astype(BF16), cur["rhs"])
    run_front_task()

    for ck in range(2):
        r = ck * c
        in_chunk = first_chunk if ck == 0 else jnp.logical_not(first_chunk)
        in_chunk_d = lane_row if ck == 0 else jnp.logical_not(lane_row)
        s_old = [state_ref[h] for h in range(nh)]
        ws_qs = []
        for h in range(nh):
            cur = st[h]
            w = cur["uw"][:, (2 * ck + 1) * d:(2 * ck + 2) * d]
            lhs = jnp.concatenate([w.astype(BF16), cur["qe"][r:r + c]], axis=0)
            ws_qs.append(_dot(lhs, s_old[h].astype(BF16)))
        run_front_task()
        for h in range(nh):
            cur = st[h]
            v_new = (cur["uw"][:, 2 * ck * d:(2 * ck + 1) * d] - ws_qs[h][0:c]).astype(BF16)
            lhs = jnp.concatenate([jnp.where(in_chunk, cur["attn"], 0.0).astype(BF16),
                                   jnp.where(in_chunk_d, cur["k_tail_t"], 0.0).astype(BF16)], axis=0)
            rhs = jnp.concatenate([v_new, zeros_cd] if ck == 0 else [zeros_cd, v_new], axis=0)
            av_kv = _dot(lhs, rhs)
            state_ref[h] = s_old[h] * cur["e_last"][ck] + av_kv[c:c + d]
            o = ws_qs[h][c:2 * c] + av_kv[0:c]
            o = (o * lax.rsqrt(jnp.mean(o * o, axis=-1, keepdims=True) + EPS)) * onorm
            z = z_s[slot_r, r:r + c, h * d:(h + 1) * d].astype(F32)
            o_ref[r:r + c, h * d:(h + 1) * d] = (o * _silu(z)).astype(o_ref.dtype)
    while front_tasks:
        run_front_task()

    xs_ref[0:HALO, :] = xs_ref[blk:blk + HALO, :]


def _gdn_front_gates(gates_ref, hp_ref, gc_s, gct_s, beta_s, slot_w):
    blk, c = GDN_BLOCK, CHUNK
    gates = gates_ref[...]
    a_log = hp_ref[0:1, :]
    dt_bias = hp_ref[1:2, :]
    beta = 1.0 / (1.0 + jnp.exp(-gates))
    sp_in = gates + dt_bias
    softplus = jnp.maximum(sp_in, 0.0) + jnp.log(1.0 + jnp.exp(-jnp.abs(sp_in)))
    g = (-jnp.exp(a_log) * softplus) * float(np.log2(np.e))

    row = lax.broadcasted_iota(jnp.int32, (blk, blk), 0)
    col = lax.broadcasted_iota(jnp.int32, (blk, blk), 1)
    tri = jnp.where((row >= col) & ((row // c) == (col // c)), 1.0, 0.0).astype(BF16)
    g_hi = g.astype(BF16)
    g_r1 = g - g_hi.astype(F32)
    g_mid = g_r1.astype(BF16)
    g_lo = (g_r1 - g_mid.astype(F32)).astype(BF16)
    gc = _dot(tri, g_hi) + _dot(tri, g_mid) + _dot(tri, g_lo)
    gc_s[slot_w] = gc
    gct_s[slot_w] = gc.T
    beta_s[slot_w] = beta


def gdn_core(proj, gates, conv_w, a_log, dt_bias, out_norm, batch, seq):
    t = proj.shape[0]
    nh, d = LA_HEADS, LA_D
    blk = GDN_BLOCK
    assert seq % blk == 0
    nblk = seq // blk
    hp = jnp.zeros((8, LANES), F32)
    hp = hp.at[0, nh:2 * nh].set(a_log.astype(F32)).at[1, nh:2 * nh].set(dt_bias.astype(F32))

    def in_map(b, n):
        return (b * nblk + jnp.minimum(n, nblk - 1), 0)

    return pl.pallas_call(
        _gdn_kernel,
        out_shape=jax.ShapeDtypeStruct((t, nh * d), BF16),
        grid=(batch, nblk + 1),
        in_specs=[pl.BlockSpec((blk, 4 * nh * d), in_map),
                  pl.BlockSpec((blk, LANES), in_map),
                  pl.BlockSpec((CONV_W, 3 * nh * d), lambda b, n: (0, 0)),
                  pl.BlockSpec((8, LANES), lambda b, n: (0, 0)),
                  pl.BlockSpec((1, d), lambda b, n: (0, 0))],
        out_specs=pl.BlockSpec((blk, nh * d), lambda b, n: (b * nblk + jnp.maximum(n - 1, 0), 0)),
        scratch_shapes=[pltpu.VMEM((HALO + blk, 3 * nh * d), BF16),
                        pltpu.VMEM((nh, d, d), F32),
                        pltpu.VMEM((2, blk, nh * d), F32), pltpu.VMEM((2, blk, nh * d), F32),
                        pltpu.VMEM((2, blk, nh * d), F32), pltpu.VMEM((2, blk, nh * d), BF16),
                        pltpu.VMEM((2, blk, LANES), F32), pltpu.VMEM((2, LANES, blk), F32),
                        pltpu.VMEM((2, blk, LANES), F32)],
        compiler_params=_cparams(("arbitrary", "arbitrary")),
        name="gdn_core",
    )(proj, gates, conv_w.astype(F32), hp, out_norm.reshape(1, d).astype(F32))


TILE_FULL, TILE_FIRST, TILE_HALF = 0, 1, 2


def _swiglu_kernel(te_ref, mode_ref, nv_ref, x_ref, g_ref, a_ref, wp_ref, wg_hbm, wu_hbm, wd_hbm, o_ref,
                   wg_c, wu_c, wd_c, stage_in, stage_out, sems, xres_s, *, pre_norm, pre_proj, routed, tf):
    i = pl.program_id(0)
    nf = wg_c.shape[0]
    e = te_ref[i]
    valid = i < nv_ref[0]

    def chunk_copies(j, slot):
        cols = pl.ds(j * tf, tf)
        return (pltpu.make_async_copy(wg_hbm.at[e, :, cols], stage_in.at[slot, 0], sems.at[slot, 0]),
                pltpu.make_async_copy(wu_hbm.at[e, :, cols], stage_in.at[slot, 1], sems.at[slot, 1]),
                pltpu.make_async_copy(wd_hbm.at[e, cols, :], stage_out.at[slot], sems.at[slot, 2]))

    tile_rows = x_ref.shape[0]

    d_model = o_ref.shape[1]

    def prepare_rows(rows):
        x = x_ref[0:rows, 0:d_model].astype(F32)
        if pre_proj:
            x = x + _dot(a_ref[0:rows, :], wp_ref[...])
            xres_s[0:rows, :] = x
        if pre_norm:
            ms = jnp.mean(x * x, axis=-1, keepdims=True)
            x = (x * lax.rsqrt(ms + EPS)) * g_ref[...]
        return x.astype(BF16)

    def chunk(xb, j):
        hid = _silu(_dot(xb, wg_c[j])) * _dot(xb, wu_c[j])
        return _dot(hid.astype(BF16), wd_c[j])

    def finish(acc, rows):
        if pre_norm:
            res = xres_s[0:rows, :] if pre_proj else x_ref[0:rows, 0:d_model]
            acc = res + acc
        o_ref[0:rows, :] = acc.astype(o_ref.dtype)
        if rows < tile_rows:
            o_ref[rows:tile_rows, :] = jnp.zeros((tile_rows - rows, o_ref.shape[1]), o_ref.dtype)

    mode = mode_ref[i]

    @pl.when(valid & (mode == TILE_FIRST))
    def _():
        for c in chunk_copies(0, 0):
            c.start()
        xb = prepare_rows(tile_rows)
        acc = None
        for j in range(nf):
            slot = j % 2
            if j + 1 < nf:
                for c in chunk_copies(j + 1, 1 - slot):
                    c.start()
            for c in chunk_copies(j, slot):
                c.wait()
            wg_c[j] = stage_in[slot, 0].astype(BF16)
            wu_c[j] = stage_in[slot, 1].astype(BF16)
            wd_c[j] = stage_out[slot].astype(BF16)
            y = chunk(xb, j)
            acc = y if acc is None else acc + y
        finish(acc, tile_rows)

    def steady(rows):
        xb = prepare_rows(rows)
        acc = None
        for j in range(nf):
            y = chunk(xb, j)
            acc = y if acc is None else acc + y
        finish(acc, rows)

    @pl.when(valid & (mode == TILE_FULL))
    def _():
        steady(tile_rows)

    if routed:
        @pl.when(valid & (mode == TILE_HALF))
        def _():
            steady(tile_rows // 2)

    @pl.when(jnp.logical_not(valid))
    def _():
        o_ref[...] = jnp.zeros_like(o_ref)


def expert_swiglu(x, gain, tile_expert, tile_mode, n_valid, wg, wu, wd, tile_rows, tf, out_dtype, pre_norm, name,
                  proj=None, routed=False):
    n_rows = x.shape[0]
    ne, d, f = wg.shape
    assert n_rows % tile_rows == 0 and f % tf == 0
    n_tiles = n_rows // tile_rows
    nf = f // tf
    pre_proj = proj is not None
    if pre_proj:
        a, wp = proj
        a_spec = pl.BlockSpec((tile_rows, a.shape[1]), lambda i, te, fi, nv: (jnp.minimum(i, nv[0] - 1), 0))
    else:
        a, wp = jnp.zeros((8, LANES), BF16), jnp.zeros((LANES, d), BF16)
        a_spec = pl.BlockSpec(a.shape, lambda i, te, fi, nv: (0, 0))
    grid_spec = pltpu.PrefetchScalarGridSpec(
        num_scalar_prefetch=3,
        grid=(n_tiles,),
        in_specs=[pl.BlockSpec((tile_rows, x.shape[1]), lambda i, te, fi, nv: (jnp.minimum(i, nv[0] - 1), 0)),
                  pl.BlockSpec((1, d), lambda i, te, fi, nv: (0, 0)),
                  a_spec,
                  pl.BlockSpec(wp.shape, lambda i, te, fi, nv: (0, 0)),
                  pl.BlockSpec(memory_space=pl.ANY),
                  pl.BlockSpec(memory_space=pl.ANY),
                  pl.BlockSpec(memory_space=pl.ANY)],
        out_specs=pl.BlockSpec((tile_rows, d), lambda i, te, fi, nv: (i, 0)),
        scratch_shapes=[pltpu.VMEM((nf, d, tf), BF16), pltpu.VMEM((nf, d, tf), BF16), pltpu.VMEM((nf, tf, d), BF16),
                        pltpu.VMEM((2, 2, d, tf), F32), pltpu.VMEM((2, tf, d), F32),
                        pltpu.SemaphoreType.DMA((2, 3)),
                        pltpu.VMEM((tile_rows, d) if pre_proj else (8, LANES), F32)],
    )
    return pl.pallas_call(
        functools.partial(_swiglu_kernel, pre_norm=pre_norm, pre_proj=pre_proj, routed=routed, tf=tf),
        out_shape=jax.ShapeDtypeStruct((n_rows, d), out_dtype),
        grid_spec=grid_spec,
        compiler_params=pltpu.CompilerParams(dimension_semantics=("arbitrary",), vmem_limit_bytes=EXPERT_VMEM_LIMIT),
        name=name,
    )(tile_expert, tile_mode, n_valid, x, gain.reshape(1, d).astype(F32), a, wp, wg, wu, wd)


def ffn_dense(x, gain, wg, wu, wd, tm, tf, proj=None):
    t = x.shape[0]
    n_tiles = t // tm
    tile_mode = jnp.full((n_tiles,), TILE_FULL, jnp.int32).at[0].set(TILE_FIRST)
    return expert_swiglu(x, gain, jnp.zeros((n_tiles,), jnp.int32), tile_mode, jnp.full((1,), n_tiles, jnp.int32),
                         wg[None], wu[None], wd[None], tm, tf, F32, True, "ffn_dense", proj=proj)


def _t5_bucket_np(dist):
    max_exact = N_BUCKETS // 2
    n = np.maximum(dist, 0)
    safe = np.maximum(n, 1).astype(np.float32)
    large = max_exact + (np.log(safe / max_exact) / np.log(MAX_DIST / max_exact)
                         * (N_BUCKETS - max_exact)).astype(np.int32)
    large = np.minimum(large, N_BUCKETS - 1)
    return np.where(n < max_exact, n, large).astype(np.int32)


def _bias_kernel(bucket_ref, valid_ref, rb_ref, o_ref):
    bucket = bucket_ref[...]
    for h in range(SW_HEADS):
        acc = jnp.zeros(bucket.shape, F32)
        for b in range(N_BUCKETS):
            acc = jnp.where(bucket == b, rb_ref[b, h], acc)
        for v in range(valid_ref.shape[0]):
            o_ref[v, h] = jnp.where(valid_ref[v] > 0, acc * LOG2E, NEG_INF)


def bias_table(rel_bias):
    qi = np.arange(WINDOW)[:, None] + WINDOW
    kj = np.arange(2 * WINDOW)[None, :]
    dist = qi - kj
    band = (dist >= 0) & (dist < WINDOW)
    valid = np.stack([band, band & (kj >= WINDOW)]).astype(np.int32)
    return pl.pallas_call(
        _bias_kernel,
        out_shape=jax.ShapeDtypeStruct((2, SW_HEADS, WINDOW, 2 * WINDOW), F32),
        in_specs=[pl.BlockSpec(memory_space=pltpu.VMEM), pl.BlockSpec(memory_space=pltpu.VMEM),
                  pl.BlockSpec(memory_space=pltpu.SMEM)],
        out_specs=pl.BlockSpec(memory_space=pltpu.VMEM),
        name="t5_bias_table",
    )(jnp.asarray(_t5_bucket_np(dist)), jnp.asarray(valid), rel_bias.astype(F32))


def _swa_kernel(q_ref, kvp_ref, kvc_ref, bias_ref, qn_ref, kn_ref, sink_ref, o_ref):
    blk, hd = WINDOW, SW_HD
    kv_w = SW_KV_HEADS * hd
    first = jnp.where(pl.program_id(1) == 0, 1, 0)
    gw = 2 * LANES
    gi = lax.broadcasted_iota(jnp.int32, (gw, gw), 0)
    gj = lax.broadcasted_iota(jnp.int32, (gw, gw), 1)
    group_ones = jnp.where((gi // hd) == (gj // hd), 1.0, 0.0).astype(BF16)
    lane = lax.broadcasted_iota(jnp.int32, (1, LANES), 1)
    low_half = lane < hd

    def head_norm(x, gain):
        cols = []
        for c0 in range(0, x.shape[1], gw):
            xc = x[:, c0:c0 + gw]
            ss = _dot((xc * xc).astype(BF16), group_ones)
            cols.append(xc * lax.rsqrt(ss * (1.0 / hd) + EPS))
        return jnp.concatenate(cols, axis=1) * gain

    def dup_half(x, half):
        swapped = pltpu.roll(x, hd, 1)
        return jnp.where(low_half == (half == 0), x, swapped)

    n_qb = q_ref.shape[0] // blk
    qn = head_norm(q_ref[...].astype(F32), qn_ref[...]) * ((hd ** -0.5) * LOG2E)
    half_sel = [jnp.where(low_half, 1.0, 0.0), jnp.where(low_half, 0.0, 1.0)]
    k_all = jnp.concatenate([kvp_ref[:, 0:kv_w], kvc_ref[:, 0:kv_w]], axis=0).astype(F32)
    kn = head_norm(k_all, kn_ref[...])
    v_all = jnp.concatenate([kvp_ref[:, kv_w:2 * kv_w], kvc_ref[:, kv_w:2 * kv_w]], axis=0).astype(F32)
    ks, vs = [], []
    for g in range(SW_KV_HEADS):
        c0 = (g // 2) * LANES
        ks.append(dup_half(kn[:, c0:c0 + LANES], g % 2).astype(BF16))
        vs.append(dup_half(v_all[:, c0:c0 + LANES], g % 2).astype(BF16))

    pairs = [(j, hq) for j in range(n_qb) for hq in range(SW_HEADS)]
    scores = {}
    for (j, hq) in pairs:
        c0 = (hq // 2) * LANES
        q_h = (qn[j * blk:(j + 1) * blk, c0:c0 + LANES] * half_sel[hq % 2]).astype(BF16)
        scores[(j, hq)] = _dot_nt(q_h, ks[hq // SW_GROUP][j * blk:(j + 2) * blk])
    probs = {}
    for (j, hq) in pairs:
        variant = first if j == 0 else 0
        s = scores[(j, hq)] + bias_ref[variant, hq]
        sink = sink_ref[hq] * LOG2E
        mx = jnp.maximum(jnp.max(s, axis=-1, keepdims=True), sink)
        p = jnp.exp2(s - mx)
        denom = jnp.sum(p, axis=-1, keepdims=True) + jnp.exp2(sink - mx)
        probs[(j, hq)] = (p / denom).astype(BF16)
    outs = {key: _dot(probs[key], vs[key[1] // SW_GROUP][key[0] * blk:(key[0] + 2) * blk]) for key in pairs}
    for j in range(n_qb):
        for c in range(SW_HEADS // 2):
            o_ref[j * blk:(j + 1) * blk, c * LANES:(c + 1) * LANES] = jnp.where(
                low_half, outs[(j, 2 * c)], outs[(j, 2 * c + 1)]).astype(o_ref.dtype)


def swa_attention(q, kv, bias, q_norm, k_norm, sinks, batch, seq):
    t = q.shape[0]
    blk = WINDOW
    step = SWA_QBLOCKS * blk
    assert seq % step == 0
    nb = seq // step
    qw = SW_HEADS * SW_HD
    kvw = 2 * SW_KV_HEADS * SW_HD
    return pl.pallas_call(
        _swa_kernel,
        out_shape=jax.ShapeDtypeStruct((t, qw), BF16),
        grid=(batch, nb),
        in_specs=[pl.BlockSpec((step, qw), lambda b, n: (b * nb + n, 0)),
                  pl.BlockSpec((blk, kvw), lambda b, n: (jnp.maximum((b * nb + n) * SWA_QBLOCKS - 1, b * nb * SWA_QBLOCKS), 0)),
                  pl.BlockSpec((step, kvw), lambda b, n: (b * nb + n, 0)),
                  pl.BlockSpec((2, SW_HEADS, blk, 2 * blk), lambda b, n: (0, 0, 0, 0)),
                  pl.BlockSpec((1, qw), lambda b, n: (0, 0)),
                  pl.BlockSpec((1, kvw // 2), lambda b, n: (0, 0)),
                  pl.BlockSpec(memory_space=pltpu.SMEM)],
        out_specs=pl.BlockSpec((step, qw), lambda b, n: (b * nb + n, 0)),
        compiler_params=_cparams(("parallel", "parallel")),
        name="swa_attention",
    )(q, kv, kv, bias, jnp.tile(q_norm.astype(F32), SW_HEADS).reshape(1, qw),
      jnp.tile(k_norm.astype(F32), SW_KV_HEADS).reshape(1, kvw // 2), sinks.astype(F32))


def _route_kernel(x_ref, a_ref, wp_ref, g_ref, wr_ref, h_ref, r_ref, wt_ref, tab_ref, cnt_ref,
                  sel_s, gw_s, cnt_s, start_s, run_s, *, tile_rows):
    ne = N_EXPERTS
    p = pl.program_id(0)
    i = pl.program_id(1)
    tm = x_ref.shape[0]
    sub = lax.broadcasted_iota(jnp.int32, (ne, tm), 0).astype(F32)

    def seg_rows(sel):
        n = jnp.sum(sel, axis=1, keepdims=True)
        return jnp.floor((n + (SEG_ALIGN - 1)) * (1.0 / SEG_ALIGN)) * SEG_ALIGN

    def excl_cumsum_experts(v):
        sub8 = lax.broadcasted_iota(jnp.int32, v.shape, 0)
        out = jnp.zeros_like(v)
        for e in range(ne - 1):
            out = out + jnp.where(sub8 > e, v[e:e + 1, :], 0.0)
        return out

    @pl.when(p == 0)
    def _():
        @pl.when(i == 0)
        def _():
            cnt_s[...] = jnp.zeros_like(cnt_s)

        x = x_ref[...] + _dot(a_ref[...], wp_ref[...])
        h_ref[...] = x
        ms = jnp.mean(x * x, axis=-1, keepdims=True)
        xn32 = (x * lax.rsqrt(ms + EPS)) * g_ref[...]
        xn_hi = xn32.astype(BF16)
        xn_lo = (xn32 - xn_hi.astype(F32)).astype(BF16)
        p_hi = _dot_nt(wr_ref[...], xn_hi)
        p_lo = _dot_nt(wr_ref[...], xn_lo)
        logits = p_hi[0:ne] + p_hi[ne:2 * ne] + p_lo[0:ne]
        m1 = jnp.max(logits, axis=0, keepdims=True)
        i1 = jnp.min(jnp.where(logits == m1, sub, float(ne)), axis=0, keepdims=True)
        l2 = jnp.where(sub == i1, -jnp.inf, logits)
        m2 = jnp.max(l2, axis=0, keepdims=True)
        i2 = jnp.min(jnp.where(l2 == m2, sub, float(ne)), axis=0, keepdims=True)
        e2 = jnp.exp(m2 - m1)
        w1 = 1.0 / (1.0 + e2)
        w2 = e2 / (1.0 + e2)
        sel = jnp.where((sub == i1) | (sub == i2), 1.0, 0.0)
        sel_s[i] = sel
        gw_s[i] = jnp.where(sub == i1, w1, jnp.where(sub == i2, w2, 0.0))
        cnt_s[...] += seg_rows(sel)

    @pl.when(p == 1)
    def _():
        @pl.when(i == 0)
        def _():
            cnt = cnt_s[...]
            padded = jnp.floor((cnt + (tile_rows - 1)) * (1.0 / tile_rows)) * tile_rows
            start_s[...] = excl_cumsum_experts(padded)
            run_s[...] = jnp.zeros_like(run_s)
            cnt_ref[...] = cnt

        sel = sel_s[i]
        gw = gw_s[i]
        ti = lax.broadcasted_iota(jnp.int32, (tm, tm), 0)
        tj = lax.broadcasted_iota(jnp.int32, (tm, tm), 1)
        tri = jnp.where(ti <= tj, 1.0, 0.0).astype(BF16)
        csum = _dot(sel.astype(BF16), tri)
        seg = jnp.broadcast_to(seg_rows(sel), run_s.shape)
        local0 = excl_cumsum_experts(seg)
        tab_ref[0, 0] = start_s[...] + run_s[...]
        tab_ref[0, 1] = seg
        tab_ref[0, 2] = local0
        run_s[...] += seg
        local_row = local0[:, 0:1] + csum - sel
        ia = jnp.min(jnp.where(sel > 0.0, sub, float(ne)), axis=0, keepdims=True)
        ib = jnp.max(jnp.where(sel > 0.0, sub, -1.0), axis=0, keepdims=True)
        pick_a = sub == ia
        pick_b = sub == ib
        rows = [jnp.sum(jnp.where(pick_a, local_row, 0.0), axis=0, keepdims=True),
                jnp.sum(jnp.where(pick_b, local_row, 0.0), axis=0, keepdims=True),
                jnp.sum(jnp.where(pick_a, gw, 0.0), axis=0, keepdims=True),
                jnp.sum(jnp.where(pick_b, gw, 0.0), axis=0, keepdims=True)]
        r_ref[...] = jnp.concatenate(rows + [jnp.zeros((ne - 4, tm), F32)], axis=0)
        wpad = jnp.concatenate(rows[2:4] + rows[0:2] + [jnp.zeros((LANES - 4, tm), F32)], axis=0)
        wt_ref[...] = wpad.T


def moe_route(x, a, wp, gain, w_router, tm, tile_rows):
    t, d = x.shape
    ne = w_router.shape[1]
    assert ne == N_EXPERTS
    w_hi = w_router.astype(BF16)
    w_lo = (w_router - w_hi.astype(F32)).astype(BF16)
    wr = jnp.concatenate([w_hi.T, w_lo.T], axis=0)
    tm = min(tm, t)
    nt = t // tm

    def row_map(p, i):
        return (i * (1 - p) + (nt - 1) * p, 0)

    return pl.pallas_call(
        functools.partial(_route_kernel, tile_rows=tile_rows),
        out_shape=(jax.ShapeDtypeStruct((t, d), F32),
                   jax.ShapeDtypeStruct((ne, t), F32), jax.ShapeDtypeStruct((t, LANES), F32),
                   jax.ShapeDtypeStruct((nt, 3, ne, LANES), F32), jax.ShapeDtypeStruct((ne, LANES), F32)),
        grid=(2, nt),
        in_specs=[pl.BlockSpec((tm, d), row_map),
                  pl.BlockSpec((tm, a.shape[1]), row_map),
                  pl.BlockSpec(wp.shape, lambda p, i: (0, 0)),
                  pl.BlockSpec((1, d), lambda p, i: (0, 0)),
                  pl.BlockSpec((2 * ne, d), lambda p, i: (0, 0))],
        out_specs=(pl.BlockSpec((tm, d), row_map),
                   pl.BlockSpec((ne, tm), lambda p, i: (0, i * p)),
                   pl.BlockSpec((tm, LANES), lambda p, i: (i * p, 0)),
                   pl.BlockSpec((1, 3, ne, LANES), lambda p, i: (i * p, 0, 0, 0)),
                   pl.BlockSpec((ne, LANES), lambda p, i: (0, 0))),
        scratch_shapes=[pltpu.VMEM((nt, ne, tm), F32), pltpu.VMEM((nt, ne, tm), F32),
                        pltpu.VMEM((ne, LANES), F32), pltpu.VMEM((ne, LANES), F32), pltpu.VMEM((ne, LANES), F32)],
        compiler_params=_cparams(("arbitrary", "arbitrary")),
        name="moe_route",
    )(x, a, wp, gain.reshape(1, d), wr)


def _segment_copies(tab_ref, i, e, local_ref, slot_ref, sem, to_slots):
    base = (i * N_EXPERTS + e) * 3
    slot0, rows, local0 = tab_ref[base], tab_ref[base + 1], tab_ref[base + 2]
    out = []
    done = 0
    size = MOE_TOKEN_TILE
    while size >= SEG_ALIGN:
        take = rows & size
        loc = local_ref.at[pl.ds(pl.multiple_of(local0 + done, SEG_ALIGN), size)]
        slt = slot_ref.at[pl.ds(pl.multiple_of(slot0 + done, SEG_ALIGN), size)]
        desc = pltpu.make_async_copy(loc, slt, sem) if to_slots else pltpu.make_async_copy(slt, loc, sem)
        out.append((take != 0, desc))
        done = done + take
        size //= 2
    return out


def _run_segment_copies(tab_ref, tile, slot, rows_s, slot_ref, sems, to_slots, action):
    for e in range(N_EXPERTS):
        for cond, desc in _segment_copies(tab_ref, tile, e, rows_s.at[slot], slot_ref, sems.at[slot], to_slots):
            @pl.when(cond)
            def _():
                getattr(desc, action)()


def _dispatch_kernel(tab_ref, zf_ref, x_ref, g_ref, r_ref, xs_ref, rows_s, zero_s, sem, zsem, *, tile_rows):
    i = pl.program_id(0)
    tm = x_ref.shape[0]
    n_local = rows_s.shape[1]

    @pl.when(i == 0)
    def _():
        zero_s[...] = jnp.zeros_like(zero_s)

        def zero_copy(e):
            row0 = pl.multiple_of(zf_ref[e], tile_rows)
            return pltpu.make_async_copy(zero_s, xs_ref.at[pl.ds(row0, tile_rows)], zsem)

        for e in range(zf_ref.shape[0]):
            @pl.when(zf_ref[e] >= 0)
            def _():
                zero_copy(e).start()
        for e in range(zf_ref.shape[0]):
            @pl.when(zf_ref[e] >= 0)
            def _():
                zero_copy(e).wait()

    x = x_ref[...]
    ms = jnp.mean(x * x, axis=-1, keepdims=True)
    xn = ((x * lax.rsqrt(ms + EPS)) * g_ref[...]).astype(BF16)
    row_id = lax.broadcasted_iota(jnp.int32, (n_local, tm), 0).astype(F32)
    onehot = jnp.where((row_id == r_ref[0:1, :]) | (row_id == r_ref[1:2, :]), 1.0, 0.0).astype(BF16)
    slot = lax.rem(i, 2)
    rows_s[slot] = _dot(onehot, xn)

    _run_segment_copies(tab_ref, i, slot, rows_s, xs_ref, sem, True, "start")

    @pl.when(i > 0)
    def _():
        _run_segment_copies(tab_ref, i - 1, 1 - slot, rows_s, xs_ref, sem, True, "wait")

    @pl.when(i == pl.num_programs(0) - 1)
    def _():
        _run_segment_copies(tab_ref, i, slot, rows_s, xs_ref, sem, True, "wait")


def moe_dispatch(x, gain, r, tab, zf_rows, n_slots, tm, tile_rows):
    t, d = x.shape
    nt = t // tm
    n_local = TOP_K * tm + N_EXPERTS * SEG_ALIGN
    grid_spec = pltpu.PrefetchScalarGridSpec(
        num_scalar_prefetch=2,
        grid=(nt,),
        in_specs=[pl.BlockSpec((tm, d), lambda i, tb, zf: (i, 0)),
                  pl.BlockSpec((1, d), lambda i, tb, zf: (0, 0)),
                  pl.BlockSpec((N_EXPERTS, tm), lambda i, tb, zf: (0, i))],
        out_specs=pl.BlockSpec(memory_space=pl.ANY),
        scratch_shapes=[pltpu.VMEM((2, n_local, d), F32), pltpu.VMEM((tile_rows, d), F32),
                        pltpu.SemaphoreType.DMA((2,)), pltpu.SemaphoreType.DMA],
    )
    return pl.pallas_call(
        functools.partial(_dispatch_kernel, tile_rows=tile_rows),
        out_shape=jax.ShapeDtypeStruct((n_slots, d), F32),
        grid_spec=grid_spec,
        compiler_params=_cparams(("arbitrary",)),
        name="moe_dispatch",
    )(tab, zf_rows, x, gain.reshape(1, d), r)


def _combine_kernel(tab_ref, h_ref, wt_ref, ys_ref, o_ref, rows_s, sems):
    i = pl.program_id(0)
    tm = h_ref.shape[0]
    n_local = rows_s.shape[1]
    slot = lax.rem(i, 2)

    def fetch(tile, into):
        rows_s[into] = jnp.zeros(rows_s.shape[1:], rows_s.dtype)
        _run_segment_copies(tab_ref, tile, into, rows_s, ys_ref, sems, False, "start")

    @pl.when(i == 0)
    def _():
        fetch(i, slot)

    @pl.when(i + 1 < pl.num_programs(0))
    def _():
        fetch(i + 1, 1 - slot)

    _run_segment_copies(tab_ref, i, slot, rows_s, ys_ref, sems, False, "wait")

    wt = wt_ref[...]
    y = rows_s[slot].astype(BF16)
    col_id = lax.broadcasted_iota(jnp.int32, (tm, n_local), 1).astype(F32)
    pick_a = jnp.where(col_id == wt[:, 2:3], 1.0, 0.0).astype(BF16)
    pick_b = jnp.where(col_id == wt[:, 3:4], 1.0, 0.0).astype(BF16)
    o_ref[...] = h_ref[...] + wt[:, 0:1] * _dot(pick_a, y) + wt[:, 1:2] * _dot(pick_b, y)


def moe_combine(h, wt, tab, ys, tm):
    t, d = h.shape
    nt = t // tm
    n_local = TOP_K * tm + N_EXPERTS * SEG_ALIGN
    grid_spec = pltpu.PrefetchScalarGridSpec(
        num_scalar_prefetch=1,
        grid=(nt,),
        in_specs=[pl.BlockSpec((tm, d), lambda i, tb: (i, 0)),
                  pl.BlockSpec((tm, LANES), lambda i, tb: (i, 0)),
                  pl.BlockSpec(memory_space=pl.ANY)],
        out_specs=pl.BlockSpec((tm, d), lambda i, tb: (i, 0)),
        scratch_shapes=[pltpu.VMEM((2, n_local, d), F32), pltpu.SemaphoreType.DMA((2,))],
    )
    return pl.pallas_call(
        _combine_kernel,
        out_shape=jax.ShapeDtypeStruct((t, d), F32),
        grid_spec=grid_spec,
        compiler_params=_cparams(("arbitrary",)),
        name="moe_combine",
    )(tab, h, wt, ys)


def moe_layer(x, a, wp, gain, w_router, wg, wu, wd):
    t, d = x.shape
    ne = w_router.shape[1]
    tr, tm = MOE_TILE_ROWS, MOE_TOKEN_TILE
    assert t % tm == 0 and ne == N_EXPERTS
    nt = t // tm
    n_tiles = -(-(TOP_K * t + nt * ne * (SEG_ALIGN - 1) + ne * (tr - 1)) // tr)
    n_slots = n_tiles * tr

    h, r, wt, tab, cnt = moe_route(x, a, wp, gain, w_router, tm, tr)
    tab = jnp.transpose(tab[:, :, :, 0], (0, 2, 1)).astype(jnp.int32).reshape(-1)

    counts = cnt[:, 0].astype(jnp.int32)
    padded = ((counts + (tr - 1)) // tr) * tr
    ends = jnp.cumsum(padded)
    n_valid = (ends[-1] // tr).astype(jnp.int32)
    tile_row0 = jnp.arange(n_tiles, dtype=jnp.int32) * tr
    tile_expert = jnp.sum((tile_row0[:, None] >= ends[None, :]).astype(jnp.int32), axis=1)
    tile_expert = jnp.minimum(tile_expert, ne - 1)
    tile_expert = jnp.where(jnp.arange(n_tiles) < n_valid, tile_expert, tile_expert[jnp.maximum(n_valid - 1, 0)])
    prev_expert = jnp.concatenate([jnp.full((1,), -1, jnp.int32), tile_expert[:-1]])
    rows_used = (ends - padded + counts)[tile_expert] - tile_row0
    tile_mode = jnp.where(tile_expert != prev_expert, TILE_FIRST,
                          jnp.where(rows_used <= tr // 2, TILE_HALF, TILE_FULL)).astype(jnp.int32)
    tail = jnp.arange(TOP_K * t // tr, n_tiles, dtype=jnp.int32)
    zf_rows = jnp.concatenate([jnp.where(padded > 0, ends - tr, -1),
                               jnp.where(tail >= n_valid, tail * tr, -1)]).astype(jnp.int32)

    xs = moe_dispatch(h, gain, r, tab, zf_rows, n_slots, tm, tr)
    ys = expert_swiglu(xs, gain, tile_expert, tile_mode, n_valid.reshape(1), wg, wu, wd, tr, FFN_CHUNK, F32, False,
                       "moe_experts", routed=True)
    return moe_combine(h, wt, tab, ys, tm)


def kernel(x, a_norm, a_w_in, a_conv, a_log_decay, a_dt_bias, a_out_norm, a_w_out, kv_norm, kv_w, k_norm,
           b_norm, b_w_q, q_norm, b_sinks, b_w_o, rel_bias, ffn_norm, dense_w_gate, dense_w_up, dense_w_down,
           moe_router, moe_w_gate, moe_w_up, moe_w_down):
    batch, seq, d = x.shape
    t = batch * seq
    nh, hd = LA_HEADS, LA_D
    main_w = 4 * nh * hd
    h0 = x.reshape(t, d)

    w_in = a_w_in[0]
    w_main = w_in[:, 0:main_w].astype(BF16)
    w_gate = jnp.zeros((d, LANES), BF16).at[:, 0:2 * nh].set(w_in[:, main_w:main_w + 2 * nh].astype(BF16))
    proj, gates = norm_matmul(h0, [(a_norm[0], w_main, BF16), (a_norm[0], w_gate, F32)], IN_PROJ_TILE,
                              "gdn_in_proj")
    o = gdn_core(proj, gates, a_conv[0], a_log_decay[0], a_dt_bias[0], a_out_norm[0], batch, seq)

    h2 = ffn_dense(h0, ffn_norm[0], dense_w_gate[0], dense_w_up[0], dense_w_down[0], MOE_TILE_ROWS, FFN_CHUNK,
                   proj=(o, a_w_out[0].astype(BF16)))

    kv, q = norm_matmul(h2, [(kv_norm, kv_w.astype(BF16), BF16), (b_norm[0], b_w_q[0].astype(BF16), BF16)],
                        QKV_PROJ_TILE, "qkv_proj")
    bias = bias_table(rel_bias)
    attn = swa_attention(q, kv, bias, q_norm[0], k_norm, b_sinks[0], batch, seq)

    h4 = moe_layer(h2, attn, b_w_o[0].astype(BF16), ffn_norm[1], moe_router[0], moe_w_gate[0], moe_w_up[0],
                   moe_w_down[0])
    return h4.reshape(batch, seq, d)
```

```python
import functools

import numpy as np
import jax
import jax.numpy as jnp
from jax import lax
from jax.experimental import pallas as pl
from jax.experimental.pallas import tpu as pltpu

F32 = jnp.float32
BF16 = jnp.bfloat16

EPS = 1e-6
NEG_INF = -1e30

LA_HEADS = 8
LA_D = 128
CONV_W = 4
CHUNK = 64
SW_HEADS = 16
SW_KV_HEADS = 4
SW_GROUP = SW_HEADS // SW_KV_HEADS
SW_HD = 64
WINDOW = 128
SWA_QBLOCKS = 2
N_BUCKETS = 32
MAX_DIST = 128
N_EXPERTS = 8
TOP_K = 2
LOG2E = float(np.log2(np.e))

LANES = 128
SEG_ALIGN = 8
GDN_BLOCK = 2 * CHUNK
HALO = 16
MOE_TILE_ROWS = 512
MOE_TOKEN_TILE = 512
FFN_CHUNK = 512
IN_PROJ_TILE = 1024
QKV_PROJ_TILE = 1024

VMEM_LIMIT = 56 * 1024 * 1024
EXPERT_VMEM_LIMIT = 60 * 1024 * 1024


def _cparams(sem):
    return pltpu.CompilerParams(dimension_semantics=sem, vmem_limit_bytes=VMEM_LIMIT)


def _silu(x):
    return x * (1.0 / (1.0 + jnp.exp2(x * (-LOG2E))))


def _dot(a, b):
    return jnp.dot(a, b, preferred_element_type=F32)


def _dot_nt(a, b):
    return lax.dot_general(a, b, (((1,), (1,)), ((), ())), preferred_element_type=F32)


def _norm_matmul_kernel(*refs, n_groups):
    x_ref = refs[0]
    g_refs = refs[1:1 + n_groups]
    w_refs = refs[1 + n_groups:1 + 2 * n_groups]
    o_refs = refs[1 + 2 * n_groups:1 + 3 * n_groups]
    x = x_ref[...]
    xr = x * lax.rsqrt(jnp.mean(x * x, axis=-1, keepdims=True) + EPS)
    for g_ref, w_ref, o_ref in zip(g_refs, w_refs, o_refs):
        o_ref[...] = _dot((xr * g_ref[...]).astype(BF16), w_ref[...]).astype(o_ref.dtype)


def norm_matmul(x, groups, tm, name):
    t, d = x.shape
    tm = min(tm, t)
    assert t % tm == 0
    gains = [g.reshape(1, d).astype(F32) for g, _, _ in groups]
    ws = [w for _, w, _ in groups]
    return pl.pallas_call(
        functools.partial(_norm_matmul_kernel, n_groups=len(groups)),
        out_shape=[jax.ShapeDtypeStruct((t, w.shape[1]), dt) for _, w, dt in groups],
        grid=(t // tm,),
        in_specs=([pl.BlockSpec((tm, d), lambda i: (i, 0))]
                  + [pl.BlockSpec((1, d), lambda i: (0, 0)) for _ in groups]
                  + [pl.BlockSpec(w.shape, lambda i: (0, 0)) for w in ws]),
        out_specs=[pl.BlockSpec((tm, w.shape[1]), lambda i: (i, 0)) for w in ws],
        compiler_params=_cparams(("parallel",)),
        name=name,
    )(x, *gains, *ws)


def _gdn_kernel(proj_ref, gates_ref, convw_ref, hp_ref, onorm_ref, o_ref,
                xs_ref, state_ref, q_s, k_s, v_s, z_s, gc_s, gct_s, beta_s):
    n = pl.program_id(1)

    @pl.when(n == 0)
    def _():
        xs_ref[0:HALO, :] = jnp.zeros((HALO, xs_ref.shape[1]), xs_ref.dtype)
        for ref in (q_s, k_s, v_s, z_s, gc_s, gct_s, beta_s):
            ref[1] = jnp.zeros(ref.shape[1:], ref.dtype)

    @pl.when(n <= 1)
    def _():
        state_ref[...] = jnp.zeros_like(state_ref)

    args = (proj_ref, gates_ref, convw_ref, hp_ref, onorm_ref, o_ref, xs_ref, state_ref,
            q_s, k_s, v_s, z_s, gc_s, gct_s, beta_s)

    @pl.when(lax.rem(n, 2) == 0)
    def _():
        _gdn_step(*args, slot_w=0, slot_r=1)

    @pl.when(lax.rem(n, 2) == 1)
    def _():
        _gdn_step(*args, slot_w=1, slot_r=0)


def _gdn_step(proj_ref, gates_ref, convw_ref, hp_ref, onorm_ref, o_ref, xs_ref, state_ref,
              q_s, k_s, v_s, z_s, gc_s, gct_s, beta_s, *, slot_w, slot_r):
    nh, d, c = LA_HEADS, LA_D, CHUNK
    blk = GDN_BLOCK
    qkv_w = 3 * nh * d

    gc = gc_s[slot_r]
    gc_t = gct_s[slot_r]
    beta = beta_s[slot_r]

    xs_ref[HALO:HALO + blk, :] = proj_ref[:, 0:qkv_w]

    def front_gates():
        _gdn_front_gates(gates_ref, hp_ref, gc_s, gct_s, beta_s, slot_w)

    di = lax.broadcasted_iota(jnp.int32, (d, d), 0)
    dj = lax.broadcasted_iota(jnp.int32, (d, d), 1)
    eye_d = jnp.where(di == dj, 1.0, 0.0).astype(BF16)

    onorm = onorm_ref[...]

    n_shift = CONV_W - 1
    sr = lax.broadcasted_iota(jnp.int32, (n_shift * blk, HALO + blk), 0)
    sc = lax.broadcasted_iota(jnp.int32, (n_shift * blk, HALO + blk), 1)
    shift_mat = jnp.where(sc == HALO + (sr % blk) - (sr // blk + 1), 1.0, 0.0).astype(BF16)
    pair_w = 2 * d

    def conv_silu(col0):
        cols = slice(col0, col0 + pair_w)
        shifted = _dot(shift_mat, xs_ref[:, cols])
        acc = convw_ref[CONV_W - 1:CONV_W, cols] * xs_ref[HALO:HALO + blk, cols].astype(F32)
        for s in range(1, CONV_W):
            acc = acc + convw_ref[CONV_W - 1 - s:CONV_W - s, cols] * shifted[(s - 1) * blk:s * blk]
        return _silu(acc)

    def front_pair(hp):
        c0 = hp * pair_w
        qf = conv_silu(c0)
        kf = conv_silu(nh * d + c0)
        v_s[slot_w, :, c0:c0 + pair_w] = conv_silu(2 * nh * d + c0)
        for half in range(2):
            lo, hi = half * d, (half + 1) * d
            qh, kh = qf[:, lo:hi], kf[:, lo:hi]
            q_s[slot_w, :, c0 + lo:c0 + hi] = qh * (lax.rsqrt(jnp.sum(qh * qh, axis=-1, keepdims=True) + EPS)
                                                    * (d ** -0.5))
            k_s[slot_w, :, c0 + lo:c0 + hi] = kh * lax.rsqrt(jnp.sum(kh * kh, axis=-1, keepdims=True) + EPS)
        z_s[slot_w, :, c0:c0 + pair_w] = proj_ref[:, qkv_w + c0:qkv_w + c0 + pair_w]

    front_tasks = [front_gates] + [functools.partial(front_pair, hp) for hp in range(nh // 2)]

    def run_front_task():
        if front_tasks:
            front_tasks.pop(0)()

    assert blk == 2 * c and 2 * c == LANES and d == LANES
    si = lax.broadcasted_iota(jnp.int32, (c, 2 * c), 0)
    sl = lax.broadcasted_iota(jnp.int32, (c, 2 * c), 1)
    first_chunk = sl < c
    sj = jnp.where(first_chunk, sl, sl - c)
    lower_incl = si >= sj
    strict = si > sj
    eye_pair = jnp.where(si == sj, 1.0, 0.0).astype(F32)
    lane_row = lax.broadcasted_iota(jnp.int32, (1, 2 * c), 1) < c
    zeros_cd = jnp.zeros((c, d), BF16)

    def block_diag(m):
        return jnp.concatenate([jnp.where(first_chunk, m, 0.0), jnp.where(first_chunk, 0.0, m)], axis=0).astype(BF16)

    st = []
    for h in range(nh):
        hs = slice(h * d, (h + 1) * d)
        q = q_s[slot_r, :, hs]
        k = k_s[slot_r, :, hs]
        v = v_s[slot_r, :, hs]
        g_col = gc[:, nh + h:nh + h + 1]
        g_row = gc_t[nh + h:nh + h + 1, :]
        b_col = beta[:, h:h + 1]
        g_col_pair = jnp.where(first_chunk, g_col[0:c], g_col[c:2 * c])
        g_last = jnp.where(lane_row, g_col[c - 1:c], g_col[2 * c - 1:2 * c])
        decay = jnp.where(lower_incl, jnp.exp2(jnp.where(lower_incl, g_col_pair - g_row, 0.0)), 0.0)
        k_beta = k * b_col
        e_col = jnp.exp2(g_col)
        kb, qb, kbf = k_beta.astype(BF16), q.astype(BF16), k.astype(BF16)
        lhs = jnp.concatenate([jnp.concatenate([kb[0:c], kb[c:2 * c]], axis=1),
                               jnp.concatenate([qb[0:c], qb[c:2 * c]], axis=1),
                               jnp.concatenate([eye_d, eye_d], axis=1)], axis=0)
        k_diag = jnp.concatenate([jnp.concatenate([kbf[0:c], zeros_cd], axis=1),
                                  jnp.concatenate([zeros_cd, kbf[c:2 * c]], axis=1)], axis=0)
        kk = _dot_nt(lhs, k_diag)
        vb, kbe = (v * b_col).astype(BF16), (k_beta * e_col).astype(BF16)
        zeros_2 = jnp.zeros((c, 2 * d), BF16)
        st.append(dict(
            a=jnp.where(strict, kk[0:c] * decay, 0.0),
            attn=kk[c:2 * c] * decay,
            k_tail_t=kk[2 * c:2 * c + d] * jnp.exp2(g_last - g_row),
            rhs=jnp.concatenate([jnp.concatenate([vb[0:c], kbe[0:c], zeros_2], axis=1),
                                 jnp.concatenate([zeros_2, vb[c:2 * c], kbe[c:2 * c]], axis=1)], axis=0),
            qe=(q * e_col).astype(BF16),
            e_last=[jnp.exp2(g_col[c - 1:c]), jnp.exp2(g_col[2 * c - 1:2 * c])]))
    run_front_task()

    for cur in st:
        x = -cur["a"]
        cur["y"] = _dot(x.astype(BF16), block_diag(x))
        cur["p"] = eye_pair + x
    run_front_task()
    n_levels = int(np.log2(c))
    for lvl in range(1, n_levels):
        for cur in st:
            y_bd = block_diag(cur["y"])
            p = cur["p"]
            if lvl + 1 < n_levels:
                zz = _dot(jnp.concatenate([cur["y"].astype(BF16), p.astype(BF16)], axis=0), y_bd)
                cur["y"] = zz[0:c]
                cur["p"] = p + zz[c:2 * c]
            else:
                cur["p"] = p + _dot(p.astype(BF16), y_bd)
        run_front_task()
    for cur in st:
        cur["uw"] = _dot(cur["p"].astype(BF16), cur["rhs"])
    run_front_task()

    for ck in range(2):
        r = ck * c
        in_chunk = first_chunk if ck == 0 else jnp.logical_not(first_chunk)
        in_chunk_d = lane_row if ck == 0 else jnp.logical_not(lane_row)
        s_old = [state_ref[h] for h in range(nh)]
        ws_qs = []
        for h in range(nh):
            cur = st[h]
            w = cur["uw"][:, (2 * ck + 1) * d:(2 * ck + 2) * d]
            lhs = jnp.concatenate([w.astype(BF16), cur["qe"][r:r + c]], axis=0)
            ws_qs.append(_dot(lhs, s_old[h].astype(BF16)))
        run_front_task()
        for h in range(nh):
            cur = st[h]
            v_new = (cur["uw"][:, 2 * ck * d:(2 * ck + 1) * d] - ws_qs[h][0:c]).astype(BF16)
            lhs = jnp.concatenate([jnp.where(in_chunk, cur["attn"], 0.0).astype(BF16),
                                   jnp.where(in_chunk_d, cur["k_tail_t"], 0.0).astype(BF16)], axis=0)
            rhs = jnp.concatenate([v_new, zeros_cd] if ck == 0 else [zeros_cd, v_new], axis=0)
            av_kv = _dot(lhs, rhs)
            state_ref[h] = s_old[h] * cur["e_last"][ck] + av_kv[c:c + d]
            o = ws_qs[h][c:2 * c] + av_kv[0:c]
            o = (o * lax.rsqrt(jnp.mean(o * o, axis=-1, keepdims=True) + EPS)) * onorm
            z = z_s[slot_r, r:r + c, h * d:(h + 1) * d].astype(F32)
            o_ref[r:r + c, h * d:(h + 1) * d] = (o * _silu(z)).astype(o_ref.dtype)
    while front_tasks:
        run_front_task()

    xs_ref[0:HALO, :] = xs_ref[blk:blk + HALO, :]


def _gdn_front_gates(gates_ref, hp_ref, gc_s, gct_s, beta_s, slot_w):
    blk, c = GDN_BLOCK, CHUNK
    gates = gates_ref[...]
    a_log = hp_ref[0:1, :]
    dt_bias = hp_ref[1:2, :]
    beta = 1.0 / (1.0 + jnp.exp(-gates))
    sp_in = gates + dt_bias
    softplus = jnp.maximum(sp_in, 0.0) + jnp.log(1.0 + jnp.exp(-jnp.abs(sp_in)))
    g = (-jnp.exp(a_log) * softplus) * float(np.log2(np.e))

    row = lax.broadcasted_iota(jnp.int32, (blk, blk), 0)
    col = lax.broadcasted_iota(jnp.int32, (blk, blk), 1)
    tri = jnp.where((row >= col) & ((row // c) == (col // c)), 1.0, 0.0).astype(BF16)
    g_hi = g.astype(BF16)
    g_r1 = g - g_hi.astype(F32)
    g_mid = g_r1.astype(BF16)
    g_lo = (g_r1 - g_mid.astype(F32)).astype(BF16)
    gc = _dot(tri, g_hi) + _dot(tri, g_mid) + _dot(tri, g_lo)
    gc_s[slot_w] = gc
    gct_s[slot_w] = gc.T
    beta_s[slot_w] = beta


def gdn_core(proj, gates, conv_w, a_log, dt_bias, out_norm, batch, seq):
    t = proj.shape[0]
    nh, d = LA_HEADS, LA_D
    blk = GDN_BLOCK
    assert seq % blk == 0
    nblk = seq // blk
    hp = jnp.zeros((8, LANES), F32)
    hp = hp.at[0, nh:2 * nh].set(a_log.astype(F32)).at[1, nh:2 * nh].set(dt_bias.astype(F32))

    def in_map(b, n):
        return (b * nblk + jnp.minimum(n, nblk - 1), 0)

    return pl.pallas_call(
        _gdn_kernel,
        out_shape=jax.ShapeDtypeStruct((t, nh * d), BF16),
        grid=(batch, nblk + 1),
        in_specs=[pl.BlockSpec((blk, 4 * nh * d), in_map),
                  pl.BlockSpec((blk, LANES), in_map),
                  pl.BlockSpec((CONV_W, 3 * nh * d), lambda b, n: (0, 0)),
                  pl.BlockSpec((8, LANES), lambda b, n: (0, 0)),
                  pl.BlockSpec((1, d), lambda b, n: (0, 0))],
        out_specs=pl.BlockSpec((blk, nh * d), lambda b, n: (b * nblk + jnp.maximum(n - 1, 0), 0)),
        scratch_shapes=[pltpu.VMEM((HALO + blk, 3 * nh * d), BF16),
                        pltpu.VMEM((nh, d, d), F32),
                        pltpu.VMEM((2, blk, nh * d), F32), pltpu.VMEM((2, blk, nh * d), F32),
                        pltpu.VMEM((2, blk, nh * d), F32), pltpu.VMEM((2, blk, nh * d), BF16),
                        pltpu.VMEM((2, blk, LANES), F32), pltpu.VMEM((2, LANES, blk), F32),
                        pltpu.VMEM((2, blk, LANES), F32)],
        compiler_params=_cparams(("arbitrary", "arbitrary")),
        name="gdn_core",
    )(proj, gates, conv_w.astype(F32), hp, out_norm.reshape(1, d).astype(F32))


TILE_FULL, TILE_FIRST, TILE_HALF = 0, 1, 2


def _swiglu_kernel(te_ref, mode_ref, nv_ref, x_ref, g_ref, a_ref, wp_ref, wg_hbm, wu_hbm, wd_hbm, o_ref,
                   wg_c, wu_c, wd_c, stage_in, stage_out, sems, xres_s, *, pre_norm, pre_proj, routed, tf):
    i = pl.program_id(0)
    nf = wg_c.shape[0]
    e = te_ref[i]
    valid = i < nv_ref[0]

    def chunk_copies(j, slot):
        cols = pl.ds(j * tf, tf)
        return (pltpu.make_async_copy(wg_hbm.at[e, :, cols], stage_in.at[slot, 0], sems.at[slot, 0]),
                pltpu.make_async_copy(wu_hbm.at[e, :, cols], stage_in.at[slot, 1], sems.at[slot, 1]),
                pltpu.make_async_copy(wd_hbm.at[e, cols, :], stage_out.at[slot], sems.at[slot, 2]))

    tile_rows = x_ref.shape[0]

    d_model = o_ref.shape[1]

    def prepare_rows(rows):
        x = x_ref[0:rows, 0:d_model].astype(F32)
        if pre_proj:
            x = x + _dot(a_ref[0:rows, :], wp_ref[...])
            xres_s[0:rows, :] = x
        if pre_norm:
            ms = jnp.mean(x * x, axis=-1, keepdims=True)
            x = (x * lax.rsqrt(ms + EPS)) * g_ref[...]
        return x.astype(BF16)

    def chunk(xb, j):
        hid = _silu(_dot(xb, wg_c[j])) * _dot(xb, wu_c[j])
        return _dot(hid.astype(BF16), wd_c[j])

    def finish(acc, rows):
        if pre_norm:
            res = xres_s[0:rows, :] if pre_proj else x_ref[0:rows, 0:d_model]
            acc = res + acc
        o_ref[0:rows, :] = acc.astype(o_ref.dtype)
        if rows < tile_rows:
            o_ref[rows:tile_rows, :] = jnp.zeros((tile_rows - rows, o_ref.shape[1]), o_ref.dtype)

    mode = mode_ref[i]

    @pl.when(valid & (mode == TILE_FIRST))
    def _():
        for c in chunk_copies(0, 0):
            c.start()
        xb = prepare_rows(tile_rows)
        acc = None
        for j in range(nf):
            slot = j % 2
            if j + 1 < nf:
                for c in chunk_copies(j + 1, 1 - slot):
                    c.start()
            for c in chunk_copies(j, slot):
                c.wait()
            wg_c[j] = stage_in[slot, 0].astype(BF16)
            wu_c[j] = stage_in[slot, 1].astype(BF16)
            wd_c[j] = stage_out[slot].astype(BF16)
            y = chunk(xb, j)
            acc = y if acc is None else acc + y
        finish(acc, tile_rows)

    def steady(rows):
        xb = prepare_rows(rows)
        acc = None
        for j in range(nf):
            y = chunk(xb, j)
            acc = y if acc is None else acc + y
        finish(acc, rows)

    @pl.when(valid & (mode == TILE_FULL))
    def _():
        steady(tile_rows)

    if routed:
        @pl.when(valid & (mode == TILE_HALF))
        def _():
            steady(tile_rows // 2)

    @pl.when(jnp.logical_not(valid))
    def _():
        o_ref[...] = jnp.zeros_like(o_ref)


def expert_swiglu(x, gain, tile_expert, tile_mode, n_valid, wg, wu, wd, tile_rows, tf, out_dtype, pre_norm, name,
                  proj=None, routed=False):
    n_rows = x.shape[0]
    ne, d, f = wg.shape
    assert n_rows % tile_rows == 0 and f % tf == 0
    n_tiles = n_rows // tile_rows
    nf = f // tf
    pre_proj = proj is not None
    if pre_proj:
        a, wp = proj
        a_spec = pl.BlockSpec((tile_rows, a.shape[1]), lambda i, te, fi, nv: (jnp.minimum(i, nv[0] - 1), 0))
    else:
        a, wp = jnp.zeros((8, LANES), BF16), jnp.zeros((LANES, d), BF16)
        a_spec = pl.BlockSpec(a.shape, lambda i, te, fi, nv: (0, 0))
    grid_spec = pltpu.PrefetchScalarGridSpec(
        num_scalar_prefetch=3,
        grid=(n_tiles,),
        in_specs=[pl.BlockSpec((tile_rows, x.shape[1]), lambda i, te, fi, nv: (jnp.minimum(i, nv[0] - 1), 0)),
                  pl.BlockSpec((1, d), lambda i, te, fi, nv: (0, 0)),
                  a_spec,
                  pl.BlockSpec(wp.shape, lambda i, te, fi, nv: (0, 0)),
                  pl.BlockSpec(memory_space=pl.ANY),
                  pl.BlockSpec(memory_space=pl.ANY),
                  pl.BlockSpec(memory_space=pl.ANY)],
        out_specs=pl.BlockSpec((tile_rows, d), lambda i, te, fi, nv: (i, 0)),
        scratch_shapes=[pltpu.VMEM((nf, d, tf), BF16), pltpu.VMEM((nf, d, tf), BF16), pltpu.VMEM((nf, tf, d), BF16),
                        pltpu.VMEM((2, 2, d, tf), F32), pltpu.VMEM((2, tf, d), F32),
                        pltpu.SemaphoreType.DMA((2, 3)),
                        pltpu.VMEM((tile_rows, d) if pre_proj else (8, LANES), F32)],
    )
    return pl.pallas_call(
        functools.partial(_swiglu_kernel, pre_norm=pre_norm, pre_proj=pre_proj, routed=routed, tf=tf),
        out_shape=jax.ShapeDtypeStruct((n_rows, d), out_dtype),
        grid_spec=grid_spec,
        compiler_params=pltpu.CompilerParams(dimension_semantics=("arbitrary",), vmem_limit_bytes=EXPERT_VMEM_LIMIT),
        name=name,
    )(tile_expert, tile_mode, n_valid, x, gain.reshape(1, d).astype(F32), a, wp, wg, wu, wd)


def ffn_dense(x, gain, wg, wu, wd, tm, tf, proj=None):
    t = x.shape[0]
    n_tiles = t // tm
    tile_mode = jnp.full((n_tiles,), TILE_FULL, jnp.int32).at[0].set(TILE_FIRST)
    return expert_swiglu(x, gain, jnp.zeros((n_tiles,), jnp.int32), tile_mode, jnp.full((1,), n_tiles, jnp.int32),
                         wg[None], wu[None], wd[None], tm, tf, F32, True, "ffn_dense", proj=proj)


def _t5_bucket_np(dist):
    max_exact = N_BUCKETS // 2
    n = np.maximum(dist, 0)
    safe = np.maximum(n, 1).astype(np.float32)
    large = max_exact + (np.log(safe / max_exact) / np.log(MAX_DIST / max_exact)
                         * (N_BUCKETS - max_exact)).astype(np.int32)
    large = np.minimum(large, N_BUCKETS - 1)
    return np.where(n < max_exact, n, large).astype(np.int32)


def _bias_kernel(bucket_ref, valid_ref, rb_ref, o_ref):
    bucket = bucket_ref[...]
    for h in range(SW_HEADS):
        acc = jnp.zeros(bucket.shape, F32)
        for b in range(N_BUCKETS):
            acc = jnp.where(bucket == b, rb_ref[b, h], acc)
        for v in range(valid_ref.shape[0]):
            o_ref[v, h] = jnp.where(valid_ref[v] > 0, acc * LOG2E, NEG_INF)


def bias_table(rel_bias):
    qi = np.arange(WINDOW)[:, None] + WINDOW
    kj = np.arange(2 * WINDOW)[None, :]
    dist = qi - kj
    band = (dist >= 0) & (dist < WINDOW)
    valid = np.stack([band, band & (kj >= WINDOW)]).astype(np.int32)
    return pl.pallas_call(
        _bias_kernel,
        out_shape=jax.ShapeDtypeStruct((2, SW_HEADS, WINDOW, 2 * WINDOW), F32),
        in_specs=[pl.BlockSpec(memory_space=pltpu.VMEM), pl.BlockSpec(memory_space=pltpu.VMEM),
                  pl.BlockSpec(memory_space=pltpu.SMEM)],
        out_specs=pl.BlockSpec(memory_space=pltpu.VMEM),
        name="t5_bias_table",
    )(jnp.asarray(_t5_bucket_np(dist)), jnp.asarray(valid), rel_bias.astype(F32))


def _swa_kernel(q_ref, kvp_ref, kvc_ref, bias_ref, qn_ref, kn_ref, sink_ref, o_ref):
    blk, hd = WINDOW, SW_HD
    kv_w = SW_KV_HEADS * hd
    first = jnp.where(pl.program_id(1) == 0, 1, 0)
    gw = 2 * LANES
    gi = lax.broadcasted_iota(jnp.int32, (gw, gw), 0)
    gj = lax.broadcasted_iota(jnp.int32, (gw, gw), 1)
    group_ones = jnp.where((gi // hd) == (gj // hd), 1.0, 0.0).astype(BF16)
    lane = lax.broadcasted_iota(jnp.int32, (1, LANES), 1)
    low_half = lane < hd

    def head_norm(x, gain):
        cols = []
        for c0 in range(0, x.shape[1], gw):
            xc = x[:, c0:c0 + gw]
            ss = _dot((xc * xc).astype(BF16), group_ones)
            cols.append(xc * lax.rsqrt(ss * (1.0 / hd) + EPS))
        return jnp.concatenate(cols, axis=1) * gain

    def dup_half(x, half):
        swapped = pltpu.roll(x, hd, 1)
        return jnp.where(low_half == (half == 0), x, swapped)

    n_qb = q_ref.shape[0] // blk
    qn = head_norm(q_ref[...].astype(F32), qn_ref[...]) * ((hd ** -0.5) * LOG2E)
    half_sel = [jnp.where(low_half, 1.0, 0.0), jnp.where(low_half, 0.0, 1.0)]
    k_all = jnp.concatenate([kvp_ref[:, 0:kv_w], kvc_ref[:, 0:kv_w]], axis=0).astype(F32)
    kn = head_norm(k_all, kn_ref[...])
    v_all = jnp.concatenate([kvp_ref[:, kv_w:2 * kv_w], kvc_ref[:, kv_w:2 * kv_w]], axis=0).astype(F32)
    ks, vs = [], []
    for g in range(SW_KV_HEADS):
        c0 = (g // 2) * LANES
        ks.append(dup_half(kn[:, c0:c0 + LANES], g % 2).astype(BF16))
        vs.append(dup_half(v_all[:, c0:c0 + LANES], g % 2).astype(BF16))

    pairs = [(j, hq) for j in range(n_qb) for hq in range(SW_HEADS)]
    scores = {}
    for (j, hq) in pairs:
        c0 = (hq // 2) * LANES
        q_h = (qn[j * blk:(j + 1) * blk, c0:c0 + LANES] * half_sel[hq % 2]).astype(BF16)
        scores[(j, hq)] = _dot_nt(q_h, ks[hq // SW_GROUP][j * blk:(j + 2) * blk])
    probs = {}
    for (j, hq) in pairs:
        variant = first if j == 0 else 0
        s = scores[(j, hq)] + bias_ref[variant, hq]
        sink = sink_ref[hq] * LOG2E
        mx = jnp.maximum(jnp.max(s, axis=-1, keepdims=True), sink)
        p = jnp.exp2(s - mx)
        denom = jnp.sum(p, axis=-1, keepdims=True) + jnp.exp2(sink - mx)
        probs[(j, hq)] = (p * (1.0 / denom)).astype(BF16)
    outs = {key: _dot(probs[key], vs[key[1] // SW_GROUP][key[0] * blk:(key[0] + 2) * blk]) for key in pairs}
    for j in range(n_qb):
        for c in range(SW_HEADS // 2):
            o_ref[j * blk:(j + 1) * blk, c * LANES:(c + 1) * LANES] = jnp.where(
                low_half, outs[(j, 2 * c)], outs[(j, 2 * c + 1)]).astype(o_ref.dtype)


def swa_attention(q, kv, bias, q_norm, k_norm, sinks, batch, seq):
    t = q.shape[0]
    blk = WINDOW
    step = SWA_QBLOCKS * blk
    assert seq % step == 0
    nb = seq // step
    qw = SW_HEADS * SW_HD
    kvw = 2 * SW_KV_HEADS * SW_HD
    return pl.pallas_call(
        _swa_kernel,
        out_shape=jax.ShapeDtypeStruct((t, qw), BF16),
        grid=(batch, nb),
        in_specs=[pl.BlockSpec((step, qw), lambda b, n: (b * nb + n, 0)),
                  pl.BlockSpec((blk, kvw), lambda b, n: (jnp.maximum((b * nb + n) * SWA_QBLOCKS - 1, b * nb * SWA_QBLOCKS), 0)),
                  pl.BlockSpec((step, kvw), lambda b, n: (b * nb + n, 0)),
                  pl.BlockSpec((2, SW_HEADS, blk, 2 * blk), lambda b, n: (0, 0, 0, 0)),
                  pl.BlockSpec((1, qw), lambda b, n: (0, 0)),
                  pl.BlockSpec((1, kvw // 2), lambda b, n: (0, 0)),
                  pl.BlockSpec(memory_space=pltpu.SMEM)],
        out_specs=pl.BlockSpec((step, qw), lambda b, n: (b * nb + n, 0)),
        compiler_params=_cparams(("parallel", "parallel")),
        name="swa_attention",
    )(q, kv, kv, bias, jnp.tile(q_norm.astype(F32), SW_HEADS).reshape(1, qw),
      jnp.tile(k_norm.astype(F32), SW_KV_HEADS).reshape(1, kvw // 2), sinks.astype(F32))


def _route_kernel(x_ref, a_ref, wp_ref, g_ref, wr_ref, h_ref, r_ref, wt_ref, tab_ref, cnt_ref,
                  sel_s, gw_s, cnt_s, start_s, run_s, *, tile_rows):
    ne = N_EXPERTS
    p = pl.program_id(0)
    i = pl.program_id(1)
    tm = x_ref.shape[0]
    sub = lax.broadcasted_iota(jnp.int32, (ne, tm), 0).astype(F32)

    def seg_rows(sel):
        n = jnp.sum(sel, axis=1, keepdims=True)
        return jnp.floor((n + (SEG_ALIGN - 1)) * (1.0 / SEG_ALIGN)) * SEG_ALIGN

    def excl_cumsum_experts(v):
        sub8 = lax.broadcasted_iota(jnp.int32, v.shape, 0)
        out = jnp.zeros_like(v)
        for e in range(ne - 1):
            out = out + jnp.where(sub8 > e, v[e:e + 1, :], 0.0)
        return out

    @pl.when(p == 0)
    def _():
        @pl.when(i == 0)
        def _():
            cnt_s[...] = jnp.zeros_like(cnt_s)

        x = x_ref[...] + _dot(a_ref[...], wp_ref[...])
        h_ref[...] = x
        ms = jnp.mean(x * x, axis=-1, keepdims=True)
        xn32 = (x * lax.rsqrt(ms + EPS)) * g_ref[...]
        xn_hi = xn32.astype(BF16)
        xn_lo = (xn32 - xn_hi.astype(F32)).astype(BF16)
        p_hi = _dot_nt(wr_ref[...], xn_hi)
        p_lo = _dot_nt(wr_ref[...], xn_lo)
        logits = p_hi[0:ne] + p_hi[ne:2 * ne] + p_lo[0:ne]
        m1 = jnp.max(logits, axis=0, keepdims=True)
        i1 = jnp.min(jnp.where(logits == m1, sub, float(ne)), axis=0, keepdims=True)
        l2 = jnp.where(sub == i1, -jnp.inf, logits)
        m2 = jnp.max(l2, axis=0, keepdims=True)
        i2 = jnp.min(jnp.where(l2 == m2, sub, float(ne)), axis=0, keepdims=True)
        e2 = jnp.exp(m2 - m1)
        w1 = 1.0 / (1.0 + e2)
        w2 = e2 / (1.0 + e2)
        sel = jnp.where((sub == i1) | (sub == i2), 1.0, 0.0)
        sel_s[i] = sel
        gw_s[i] = jnp.where(sub == i1, w1, jnp.where(sub == i2, w2, 0.0))
        cnt_s[...] += seg_rows(sel)

    @pl.when(p == 1)
    def _():
        @pl.when(i == 0)
        def _():
            cnt = cnt_s[...]
            padded = jnp.floor((cnt + (tile_rows - 1)) * (1.0 / tile_rows)) * tile_rows
            start_s[...] = excl_cumsum_experts(padded)
            run_s[...] = jnp.zeros_like(run_s)
            cnt_ref[...] = cnt

        sel = sel_s[i]
        gw = gw_s[i]
        ti = lax.broadcasted_iota(jnp.int32, (tm, tm), 0)
        tj = lax.broadcasted_iota(jnp.int32, (tm, tm), 1)
        tri = jnp.where(ti <= tj, 1.0, 0.0).astype(BF16)
        csum = _dot(sel.astype(BF16), tri)
        seg = jnp.broadcast_to(seg_rows(sel), run_s.shape)
        local0 = excl_cumsum_experts(seg)
        tab_ref[0, 0] = start_s[...] + run_s[...]
        tab_ref[0, 1] = seg
        tab_ref[0, 2] = local0
        run_s[...] += seg
        local_row = local0[:, 0:1] + csum - sel
        ia = jnp.min(jnp.where(sel > 0.0, sub, float(ne)), axis=0, keepdims=True)
        ib = jnp.max(jnp.where(sel > 0.0, sub, -1.0), axis=0, keepdims=True)
        pick_a = sub == ia
        pick_b = sub == ib
        rows = [jnp.sum(jnp.where(pick_a, local_row, 0.0), axis=0, keepdims=True),
                jnp.sum(jnp.where(pick_b, local_row, 0.0), axis=0, keepdims=True),
                jnp.sum(jnp.where(pick_a, gw, 0.0), axis=0, keepdims=True),
                jnp.sum(jnp.where(pick_b, gw, 0.0), axis=0, keepdims=True)]
        r_ref[...] = jnp.concatenate(rows + [jnp.zeros((ne - 4, tm), F32)], axis=0)
        wpad = jnp.concatenate(rows[2:4] + rows[0:2] + [jnp.zeros((LANES - 4, tm), F32)], axis=0)
        wt_ref[...] = wpad.T


def moe_route(x, a, wp, gain, w_router, tm, tile_rows):
    t, d = x.shape
    ne = w_router.shape[1]
    assert ne == N_EXPERTS
    w_hi = w_router.astype(BF16)
    w_lo = (w_router - w_hi.astype(F32)).astype(BF16)
    wr = jnp.concatenate([w_hi.T, w_lo.T], axis=0)
    tm = min(tm, t)
    nt = t // tm

    def row_map(p, i):
        return (i * (1 - p) + (nt - 1) * p, 0)

    return pl.pallas_call(
        functools.partial(_route_kernel, tile_rows=tile_rows),
        out_shape=(jax.ShapeDtypeStruct((t, d), F32),
                   jax.ShapeDtypeStruct((ne, t), F32), jax.ShapeDtypeStruct((t, LANES), F32),
                   jax.ShapeDtypeStruct((nt, 3, ne, LANES), F32), jax.ShapeDtypeStruct((ne, LANES), F32)),
        grid=(2, nt),
        in_specs=[pl.BlockSpec((tm, d), row_map),
                  pl.BlockSpec((tm, a.shape[1]), row_map),
                  pl.BlockSpec(wp.shape, lambda p, i: (0, 0)),
                  pl.BlockSpec((1, d), lambda p, i: (0, 0)),
                  pl.BlockSpec((2 * ne, d), lambda p, i: (0, 0))],
        out_specs=(pl.BlockSpec((tm, d), row_map),
                   pl.BlockSpec((ne, tm), lambda p, i: (0, i * p)),
                   pl.BlockSpec((tm, LANES), lambda p, i: (i * p, 0)),
                   pl.BlockSpec((1, 3, ne, LANES), lambda p, i: (i * p, 0, 0, 0)),
                   pl.BlockSpec((ne, LANES), lambda p, i: (0, 0))),
        scratch_shapes=[pltpu.VMEM((nt, ne, tm), F32), pltpu.VMEM((nt, ne, tm), F32),
                        pltpu.VMEM((ne, LANES), F32), pltpu.VMEM((ne, LANES), F32), pltpu.VMEM((ne, LANES), F32)],
        compiler_params=_cparams(("arbitrary", "arbitrary")),
        name="moe_route",
    )(x, a, wp, gain.reshape(1, d), wr)


def _segment_copies(tab_ref, i, e, local_ref, slot_ref, sem, to_slots):
    base = (i * N_EXPERTS + e) * 3
    slot0, rows, local0 = tab_ref[base], tab_ref[base + 1], tab_ref[base + 2]
    out = []
    done = 0
    size = MOE_TOKEN_TILE
    while size >= SEG_ALIGN:
        take = rows & size
        loc = local_ref.at[pl.ds(pl.multiple_of(local0 + done, SEG_ALIGN), size)]
        slt = slot_ref.at[pl.ds(pl.multiple_of(slot0 + done, SEG_ALIGN), size)]
        desc = pltpu.make_async_copy(loc, slt, sem) if to_slots else pltpu.make_async_copy(slt, loc, sem)
        out.append((take != 0, desc))
        done = done + take
        size //= 2
    return out


def _run_segment_copies(tab_ref, tile, slot, rows_s, slot_ref, sems, to_slots, action):
    for e in range(N_EXPERTS):
        for cond, desc in _segment_copies(tab_ref, tile, e, rows_s.at[slot], slot_ref, sems.at[slot], to_slots):
            @pl.when(cond)
            def _():
                getattr(desc, action)()


def _dispatch_kernel(tab_ref, zf_ref, x_ref, g_ref, r_ref, xs_ref, rows_s, zero_s, sem, zsem, *, tile_rows):
    i = pl.program_id(0)
    tm = x_ref.shape[0]
    n_local = rows_s.shape[1]

    @pl.when(i == 0)
    def _():
        zero_s[...] = jnp.zeros_like(zero_s)

        def zero_copy(e):
            row0 = pl.multiple_of(zf_ref[e], tile_rows)
            return pltpu.make_async_copy(zero_s, xs_ref.at[pl.ds(row0, tile_rows)], zsem)

        for e in range(zf_ref.shape[0]):
            @pl.when(zf_ref[e] >= 0)
            def _():
                zero_copy(e).start()
        for e in range(zf_ref.shape[0]):
            @pl.when(zf_ref[e] >= 0)
            def _():
                zero_copy(e).wait()

    x = x_ref[...]
    ms = jnp.mean(x * x, axis=-1, keepdims=True)
    xn = ((x * lax.rsqrt(ms + EPS)) * g_ref[...]).astype(BF16)
    row_id = lax.broadcasted_iota(jnp.int32, (n_local, tm), 0).astype(F32)
    onehot = jnp.where((row_id == r_ref[0:1, :]) | (row_id == r_ref[1:2, :]), 1.0, 0.0).astype(BF16)
    slot = lax.rem(i, 2)
    rows_s[slot] = _dot(onehot, xn)

    _run_segment_copies(tab_ref, i, slot, rows_s, xs_ref, sem, True, "start")

    @pl.when(i > 0)
    def _():
        _run_segment_copies(tab_ref, i - 1, 1 - slot, rows_s, xs_ref, sem, True, "wait")

    @pl.when(i == pl.num_programs(0) - 1)
    def _():
        _run_segment_copies(tab_ref, i, slot, rows_s, xs_ref, sem, True, "wait")


def moe_dispatch(x, gain, r, tab, zf_rows, n_slots, tm, tile_rows):
    t, d = x.shape
    nt = t // tm
    n_local = TOP_K * tm + N_EXPERTS * SEG_ALIGN
    grid_spec = pltpu.PrefetchScalarGridSpec(
        num_scalar_prefetch=2,
        grid=(nt,),
        in_specs=[pl.BlockSpec((tm, d), lambda i, tb, zf: (i, 0)),
                  pl.BlockSpec((1, d), lambda i, tb, zf: (0, 0)),
                  pl.BlockSpec((N_EXPERTS, tm), lambda i, tb, zf: (0, i))],
        out_specs=pl.BlockSpec(memory_space=pl.ANY),
        scratch_shapes=[pltpu.VMEM((2, n_local, d), F32), pltpu.VMEM((tile_rows, d), F32),
                        pltpu.SemaphoreType.DMA((2,)), pltpu.SemaphoreType.DMA],
    )
    return pl.pallas_call(
        functools.partial(_dispatch_kernel, tile_rows=tile_rows),
        out_shape=jax.ShapeDtypeStruct((n_slots, d), F32),
        grid_spec=grid_spec,
        compiler_params=_cparams(("arbitrary",)),
        name="moe_dispatch",
    )(tab, zf_rows, x, gain.reshape(1, d), r)


def _combine_kernel(tab_ref, h_ref, wt_ref, ys_ref, o_ref, rows_s, sems):
    i = pl.program_id(0)
    tm = h_ref.shape[0]
    n_local = rows_s.shape[1]
    slot = lax.rem(i, 2)

    def fetch(tile, into):
        rows_s[into] = jnp.zeros(rows_s.shape[1:], rows_s.dtype)
        _run_segment_copies(tab_ref, tile, into, rows_s, ys_ref, sems, False, "start")

    @pl.when(i == 0)
    def _():
        fetch(i, slot)

    @pl.when(i + 1 < pl.num_programs(0))
    def _():
        fetch(i + 1, 1 - slot)

    _run_segment_copies(tab_ref, i, slot, rows_s, ys_ref, sems, False, "wait")

    wt = wt_ref[...]
    y = rows_s[slot].astype(BF16)
    col_id = lax.broadcasted_iota(jnp.int32, (tm, n_local), 1).astype(F32)
    pick_a = jnp.where(col_id == wt[:, 2:3], 1.0, 0.0).astype(BF16)
    pick_b = jnp.where(col_id == wt[:, 3:4], 1.0, 0.0).astype(BF16)
    o_ref[...] = h_ref[...] + wt[:, 0:1] * _dot(pick_a, y) + wt[:, 1:2] * _dot(pick_b, y)


def moe_combine(h, wt, tab, ys, tm):
    t, d = h.shape
    nt = t // tm
    n_local = TOP_K * tm + N_EXPERTS * SEG_ALIGN
    grid_spec = pltpu.PrefetchScalarGridSpec(
        num_scalar_prefetch=1,
        grid=(nt,),
        in_specs=[pl.BlockSpec((tm, d), lambda i, tb: (i, 0)),
                  pl.BlockSpec((tm, LANES), lambda i, tb: (i, 0)),
                  pl.BlockSpec(memory_space=pl.ANY)],
        out_specs=pl.BlockSpec((tm, d), lambda i, tb: (i, 0)),
        scratch_shapes=[pltpu.VMEM((2, n_local, d), F32), pltpu.SemaphoreType.DMA((2,))],
    )
    return pl.pallas_call(
        _combine_kernel,
        out_shape=jax.ShapeDtypeStruct((t, d), F32),
        grid_spec=grid_spec,
        compiler_params=_cparams(("arbitrary",)),
        name="moe_combine",
    )(tab, h, wt, ys)


def moe_layer(x, a, wp, gain, w_router, wg, wu, wd):
    t, d = x.shape
    ne = w_router.shape[1]
    tr, tm = MOE_TILE_ROWS, MOE_TOKEN_TILE
    assert t % tm == 0 and ne == N_EXPERTS
    nt = t // tm
    n_tiles = -(-(TOP_K * t + nt * ne * (SEG_ALIGN - 1) + ne * (tr - 1)) // tr)
    n_slots = n_tiles * tr

    h, r, wt, tab, cnt = moe_route(x, a, wp, gain, w_router, tm, tr)
    tab = jnp.transpose(tab[:, :, :, 0], (0, 2, 1)).astype(jnp.int32).reshape(-1)

    counts = cnt[:, 0].astype(jnp.int32)
    padded = ((counts + (tr - 1)) // tr) * tr
    ends = jnp.cumsum(padded)
    n_valid = (ends[-1] // tr).astype(jnp.int32)
    tile_row0 = jnp.arange(n_tiles, dtype=jnp.int32) * tr
    tile_expert = jnp.sum((tile_row0[:, None] >= ends[None, :]).astype(jnp.int32), axis=1)
    tile_expert = jnp.minimum(tile_expert, ne - 1)
    tile_expert = jnp.where(jnp.arange(n_tiles) < n_valid, tile_expert, tile_expert[jnp.maximum(n_valid - 1, 0)])
    prev_expert = jnp.concatenate([jnp.full((1,), -1, jnp.int32), tile_expert[:-1]])
    rows_used = (ends - padded + counts)[tile_expert] - tile_row0
    tile_mode = jnp.where(tile_expert != prev_expert, TILE_FIRST,
                          jnp.where(rows_used <= tr // 2, TILE_HALF, TILE_FULL)).astype(jnp.int32)
    tail = jnp.arange(TOP_K * t // tr, n_tiles, dtype=jnp.int32)
    zf_rows = jnp.concatenate([jnp.where(padded > 0, ends - tr, -1),
                               jnp.where(tail >= n_valid, tail * tr, -1)]).astype(jnp.int32)

    xs = moe_dispatch(h, gain, r, tab, zf_rows, n_slots, tm, tr)
    ys = expert_swiglu(xs, gain, tile_expert, tile_mode, n_valid.reshape(1), wg, wu, wd, tr, FFN_CHUNK, F32, False,
                       "moe_experts", routed=True)
    return moe_combine(h, wt, tab, ys, tm)


def kernel(x, a_norm, a_w_in, a_conv, a_log_decay, a_dt_bias, a_out_norm, a_w_out, kv_norm, kv_w, k_norm,
           b_norm, b_w_q, q_norm, b_sinks, b_w_o, rel_bias, ffn_norm, dense_w_gate, dense_w_up, dense_w_down,
           moe_router, moe_w_gate, moe_w_up, moe_w_down):
    batch, seq, d = x.shape
    t = batch * seq
    nh, hd = LA_HEADS, LA_D
    main_w = 4 * nh * hd
    h0 = x.reshape(t, d)

    w_in = a_w_in[0]
    w_main = w_in[:, 0:main_w].astype(BF16)
    w_gate = jnp.zeros((d, LANES), BF16).at[:, 0:2 * nh].set(w_in[:, main_w:main_w + 2 * nh].astype(BF16))
    proj, gates = norm_matmul(h0, [(a_norm[0], w_main, BF16), (a_norm[0], w_gate, F32)], IN_PROJ_TILE,
                              "gdn_in_proj")
    o = gdn_core(proj, gates, a_conv[0], a_log_decay[0], a_dt_bias[0], a_out_norm[0], batch, seq)

    h2 = ffn_dense(h0, ffn_norm[0], dense_w_gate[0], dense_w_up[0], dense_w_down[0], MOE_TILE_ROWS, FFN_CHUNK,
                   proj=(o, a_w_out[0].astype(BF16)))

    kv, q = norm_matmul(h2, [(kv_norm, kv_w.astype(BF16), BF16), (b_norm[0], b_w_q[0].astype(BF16), BF16)],
                        QKV_PROJ_TILE, "qkv_proj")
    bias = bias_table(rel_bias)
    attn = swa_attention(q, kv, bias, q_norm[0], k_norm, b_sinks[0], batch, seq)

    h4 = moe_layer(h2, attn, b_w_o[0].astype(BF16), ffn_norm[1], moe_router[0], moe_w_gate[0], moe_w_up[0],
                   moe_w_down[0])
    return h4.reshape(batch, seq, d)
```

```python
import functools

import numpy as np
import jax
import jax.numpy as jnp
from jax import lax
from jax.experimental import pallas as pl
from jax.experimental.pallas import tpu as pltpu

F32 = jnp.float32
BF16 = jnp.bfloat16

EPS = 1e-6
NEG_INF = -1e30

LA_HEADS = 8
LA_D = 128
CONV_W = 4
CHUNK = 64
SW_HEADS = 16
SW_KV_HEADS = 4
SW_GROUP = SW_HEADS // SW_KV_HEADS
SW_HD = 64
WINDOW = 128
SWA_QBLOCKS = 2
N_BUCKETS = 32
MAX_DIST = 128
N_EXPERTS = 8
TOP_K = 2
LOG2E = float(np.log2(np.e))

LANES = 128
SEG_ALIGN = 8
GDN_BLOCK = 2 * CHUNK
HALO = 16
MOE_TILE_ROWS = 512
MOE_TOKEN_TILE = 512
FFN_CHUNK = 512
IN_PROJ_TILE = 1024
QKV_PROJ_TILE = 1024

VMEM_LIMIT = 56 * 1024 * 1024
EXPERT_VMEM_LIMIT = 60 * 1024 * 1024


def _cparams(sem):
    return pltpu.CompilerParams(dimension_semantics=sem, vmem_limit_bytes=VMEM_LIMIT)


def _silu(x, base2=True):
    e = jnp.exp2(x * (-LOG2E)) if base2 else jnp.exp(-x)
    return x * (1.0 / (1.0 + e))


def _dot(a, b):
    return jnp.dot(a, b, preferred_element_type=F32)


def _dot_nt(a, b):
    return lax.dot_general(a, b, (((1,), (1,)), ((), ())), preferred_element_type=F32)


def _norm_matmul_kernel(*refs, n_groups):
    x_ref = refs[0]
    g_refs = refs[1:1 + n_groups]
    w_refs = refs[1 + n_groups:1 + 2 * n_groups]
    o_refs = refs[1 + 2 * n_groups:1 + 3 * n_groups]
    x = x_ref[...]
    xr = x * lax.rsqrt(jnp.mean(x * x, axis=-1, keepdims=True) + EPS)
    for g_ref, w_ref, o_ref in zip(g_refs, w_refs, o_refs):
        o_ref[...] = _dot((xr * g_ref[...]).astype(BF16), w_ref[...]).astype(o_ref.dtype)


def norm_matmul(x, groups, tm, name):
    t, d = x.shape
    tm = min(tm, t)
    assert t % tm == 0
    gains = [g.reshape(1, d).astype(F32) for g, _, _ in groups]
    ws = [w for _, w, _ in groups]
    return pl.pallas_call(
        functools.partial(_norm_matmul_kernel, n_groups=len(groups)),
        out_shape=[jax.ShapeDtypeStruct((t, w.shape[1]), dt) for _, w, dt in groups],
        grid=(t // tm,),
        in_specs=([pl.BlockSpec((tm, d), lambda i: (i, 0))]
                  + [pl.BlockSpec((1, d), lambda i: (0, 0)) for _ in groups]
                  + [pl.BlockSpec(w.shape, lambda i: (0, 0)) for w in ws]),
        out_specs=[pl.BlockSpec((tm, w.shape[1]), lambda i: (i, 0)) for w in ws],
        compiler_params=_cparams(("parallel",)),
        name=name,
    )(x, *gains, *ws)


def _gdn_kernel(proj_ref, gates_ref, convw_ref, hp_ref, onorm_ref, o_ref,
                xs_ref, state_ref, q_s, k_s, v_s, z_s, gc_s, gct_s, beta_s):
    n = pl.program_id(1)

    @pl.when(n == 0)
    def _():
        xs_ref[0:HALO, :] = jnp.zeros((HALO, xs_ref.shape[1]), xs_ref.dtype)
        for ref in (q_s, k_s, v_s, z_s, gc_s, gct_s, beta_s):
            ref[1] = jnp.zeros(ref.shape[1:], ref.dtype)

    @pl.when(n <= 1)
    def _():
        state_ref[...] = jnp.zeros_like(state_ref)

    args = (proj_ref, gates_ref, convw_ref, hp_ref, onorm_ref, o_ref, xs_ref, state_ref,
            q_s, k_s, v_s, z_s, gc_s, gct_s, beta_s)

    @pl.when(lax.rem(n, 2) == 0)
    def _():
        _gdn_step(*args, slot_w=0, slot_r=1)

    @pl.when(lax.rem(n, 2) == 1)
    def _():
        _gdn_step(*args, slot_w=1, slot_r=0)


def _gdn_step(proj_ref, gates_ref, convw_ref, hp_ref, onorm_ref, o_ref, xs_ref, state_ref,
              q_s, k_s, v_s, z_s, gc_s, gct_s, beta_s, *, slot_w, slot_r):
    nh, d, c = LA_HEADS, LA_D, CHUNK
    blk = GDN_BLOCK
    qkv_w = 3 * nh * d

    gc = gc_s[slot_r]
    gc_t = gct_s[slot_r]
    beta = beta_s[slot_r]

    xs_ref[HALO:HALO + blk, :] = proj_ref[:, 0:qkv_w]

    def front_gates():
        _gdn_front_gates(gates_ref, hp_ref, gc_s, gct_s, beta_s, slot_w)

    di = lax.broadcasted_iota(jnp.int32, (d, d), 0)
    dj = lax.broadcasted_iota(jnp.int32, (d, d), 1)
    eye_d = jnp.where(di == dj, 1.0, 0.0).astype(BF16)

    onorm = onorm_ref[...]

    n_shift = CONV_W - 1
    sr = lax.broadcasted_iota(jnp.int32, (n_shift * blk, HALO + blk), 0)
    sc = lax.broadcasted_iota(jnp.int32, (n_shift * blk, HALO + blk), 1)
    shift_mat = jnp.where(sc == HALO + (sr % blk) - (sr // blk + 1), 1.0, 0.0).astype(BF16)
    pair_w = 2 * d

    def conv_silu(col0):
        cols = slice(col0, col0 + pair_w)
        shifted = _dot(shift_mat, xs_ref[:, cols])
        acc = convw_ref[CONV_W - 1:CONV_W, cols] * xs_ref[HALO:HALO + blk, cols].astype(F32)
        for s in range(1, CONV_W):
            acc = acc + convw_ref[CONV_W - 1 - s:CONV_W - s, cols] * shifted[(s - 1) * blk:s * blk]
        return _silu(acc)

    def front_pair(hp):
        c0 = hp * pair_w
        qf = conv_silu(c0)
        kf = conv_silu(nh * d + c0)
        v_s[slot_w, :, c0:c0 + pair_w] = conv_silu(2 * nh * d + c0)
        for half in range(2):
            lo, hi = half * d, (half + 1) * d
            qh, kh = qf[:, lo:hi], kf[:, lo:hi]
            q_s[slot_w, :, c0 + lo:c0 + hi] = qh * (lax.rsqrt(jnp.sum(qh * qh, axis=-1, keepdims=True) + EPS)
                                                    * (d ** -0.5))
            k_s[slot_w, :, c0 + lo:c0 + hi] = kh * lax.rsqrt(jnp.sum(kh * kh, axis=-1, keepdims=True) + EPS)
        z_s[slot_w, :, c0:c0 + pair_w] = proj_ref[:, qkv_w + c0:qkv_w + c0 + pair_w]

    front_tasks = [front_gates] + [functools.partial(front_pair, hp) for hp in range(nh // 2)]

    def run_front_task():
        if front_tasks:
            front_tasks.pop(0)()

    assert blk == 2 * c and 2 * c == LANES and d == LANES
    si = lax.broadcasted_iota(jnp.int32, (c, 2 * c), 0)
    sl = lax.broadcasted_iota(jnp.int32, (c, 2 * c), 1)
    first_chunk = sl < c
    sj = jnp.where(first_chunk, sl, sl - c)
    lower_incl = si >= sj
    strict = si > sj
    eye_pair = jnp.where(si == sj, 1.0, 0.0).astype(F32)
    lane_row = lax.broadcasted_iota(jnp.int32, (1, 2 * c), 1) < c
    zeros_cd = jnp.zeros((c, d), BF16)

    def block_diag(m):
        return jnp.concatenate([jnp.where(first_chunk, m, 0.0), jnp.where(first_chunk, 0.0, m)], axis=0).astype(BF16)

    st = []
    for h in range(nh):
        hs = slice(h * d, (h + 1) * d)
        q = q_s[slot_r, :, hs]
        k = k_s[slot_r, :, hs]
        v = v_s[slot_r, :, hs]
        g_col = gc[:, nh + h:nh + h + 1]
        g_row = gc_t[nh + h:nh + h + 1, :]
        b_col = beta[:, h:h + 1]
        g_col_pair = jnp.where(first_chunk, g_col[0:c], g_col[c:2 * c])
        g_last = jnp.where(lane_row, g_col[c - 1:c], g_col[2 * c - 1:2 * c])
        decay = jnp.where(lower_incl, jnp.exp2(jnp.where(lower_incl, g_col_pair - g_row, 0.0)), 0.0)
        k_beta = k * b_col
        e_col = jnp.exp2(g_col)
        kb, qb, kbf = k_beta.astype(BF16), q.astype(BF16), k.astype(BF16)
        lhs = jnp.concatenate([jnp.concatenate([kb[0:c], kb[c:2 * c]], axis=1),
                               jnp.concatenate([qb[0:c], qb[c:2 * c]], axis=1),
                               jnp.concatenate([eye_d, eye_d], axis=1)], axis=0)
        k_diag = jnp.concatenate([jnp.concatenate([kbf[0:c], zeros_cd], axis=1),
                                  jnp.concatenate([zeros_cd, kbf[c:2 * c]], axis=1)], axis=0)
        kk = _dot_nt(lhs, k_diag)
        vb, kbe = (v * b_col).astype(BF16), (k_beta * e_col).astype(BF16)
        zeros_2 = jnp.zeros((c, 2 * d), BF16)
        st.append(dict(
            a=jnp.where(strict, kk[0:c] * decay, 0.0),
            attn=kk[c:2 * c] * decay,
            k_tail_t=kk[2 * c:2 * c + d] * jnp.exp2(g_last - g_row),
            rhs=jnp.concatenate([jnp.concatenate([vb[0:c], kbe[0:c], zeros_2], axis=1),
                                 jnp.concatenate([zeros_2, vb[c:2 * c], kbe[c:2 * c]], axis=1)], axis=0),
            qe=(q * e_col).astype(BF16),
            e_last=[jnp.exp2(g_col[c - 1:c]), jnp.exp2(g_col[2 * c - 1:2 * c])]))
    run_front_task()

    for cur in st:
        x = -cur["a"]
        cur["y"] = _dot(x.astype(BF16), block_diag(x))
        cur["p"] = eye_pair + x
    run_front_task()
    n_levels = int(np.log2(c))
    for lvl in range(1, n_levels):
        for cur in st:
            y_bd = block_diag(cur["y"])
            p = cur["p"]
            if lvl + 1 < n_levels:
                zz = _dot(jnp.concatenate([cur["y"].astype(BF16), p.astype(BF16)], axis=0), y_bd)
                cur["y"] = zz[0:c]
                cur["p"] = p + zz[c:2 * c]
            else:
                cur["p"] = p + _dot(p.astype(BF16), y_bd)
        run_front_task()
    for cur in st:
        cur["uw"] = _dot(cur["p"].astype(BF16), cur["rhs"])
    run_front_task()

    for ck in range(2):
        r = ck * c
        in_chunk = first_chunk if ck == 0 else jnp.logical_not(first_chunk)
        in_chunk_d = lane_row if ck == 0 else jnp.logical_not(lane_row)
        s_old = [state_ref[h] for h in range(nh)]
        ws_qs = []
        for h in range(nh):
            cur = st[h]
            w = cur["uw"][:, (2 * ck + 1) * d:(2 * ck + 2) * d]
            lhs = jnp.concatenate([w.astype(BF16), cur["qe"][r:r + c]], axis=0)
            ws_qs.append(_dot(lhs, s_old[h].astype(BF16)))
        run_front_task()
        for h in range(nh):
            cur = st[h]
            v_new = (cur["uw"][:, 2 * ck * d:(2 * ck + 1) * d] - ws_qs[h][0:c]).astype(BF16)
            lhs = jnp.concatenate([jnp.where(in_chunk, cur["attn"], 0.0).astype(BF16),
                                   jnp.where(in_chunk_d, cur["k_tail_t"], 0.0).astype(BF16)], axis=0)
            rhs = jnp.concatenate([v_new, zeros_cd] if ck == 0 else [zeros_cd, v_new], axis=0)
            av_kv = _dot(lhs, rhs)
            state_ref[h] = s_old[h] * cur["e_last"][ck] + av_kv[c:c + d]
            o = ws_qs[h][c:2 * c] + av_kv[0:c]
            o = (o * lax.rsqrt(jnp.mean(o * o, axis=-1, keepdims=True) + EPS)) * onorm
            z = z_s[slot_r, r:r + c, h * d:(h + 1) * d].astype(F32)
            o_ref[r:r + c, h * d:(h + 1) * d] = (o * _silu(z)).astype(o_ref.dtype)
    while front_tasks:
        run_front_task()

    xs_ref[0:HALO, :] = xs_ref[blk:blk + HALO, :]


def _gdn_front_gates(gates_ref, hp_ref, gc_s, gct_s, beta_s, slot_w):
    blk, c = GDN_BLOCK, CHUNK
    gates = gates_ref[...]
    a_log = hp_ref[0:1, :]
    dt_bias = hp_ref[1:2, :]
    beta = 1.0 / (1.0 + jnp.exp(-gates))
    sp_in = gates + dt_bias
    softplus = jnp.maximum(sp_in, 0.0) + jnp.log(1.0 + jnp.exp(-jnp.abs(sp_in)))
    g = (-jnp.exp(a_log) * softplus) * float(np.log2(np.e))

    row = lax.broadcasted_iota(jnp.int32, (blk, blk), 0)
    col = lax.broadcasted_iota(jnp.int32, (blk, blk), 1)
    tri = jnp.where((row >= col) & ((row // c) == (col // c)), 1.0, 0.0).astype(BF16)
    g_hi = g.astype(BF16)
    g_r1 = g - g_hi.astype(F32)
    g_mid = g_r1.astype(BF16)
    g_lo = (g_r1 - g_mid.astype(F32)).astype(BF16)
    gc = _dot(tri, g_hi) + _dot(tri, g_mid) + _dot(tri, g_lo)
    gc_s[slot_w] = gc
    gct_s[slot_w] = gc.T
    beta_s[slot_w] = beta


def gdn_core(proj, gates, conv_w, a_log, dt_bias, out_norm, batch, seq):
    t = proj.shape[0]
    nh, d = LA_HEADS, LA_D
    blk = GDN_BLOCK
    assert seq % blk == 0
    nblk = seq // blk
    hp = jnp.zeros((8, LANES), F32)
    hp = hp.at[0, nh:2 * nh].set(a_log.astype(F32)).at[1, nh:2 * nh].set(dt_bias.astype(F32))

    def in_map(b, n):
        return (b * nblk + jnp.minimum(n, nblk - 1), 0)

    return pl.pallas_call(
        _gdn_kernel,
        out_shape=jax.ShapeDtypeStruct((t, nh * d), BF16),
        grid=(batch, nblk + 1),
        in_specs=[pl.BlockSpec((blk, 4 * nh * d), in_map),
                  pl.BlockSpec((blk, LANES), in_map),
                  pl.BlockSpec((CONV_W, 3 * nh * d), lambda b, n: (0, 0)),
                  pl.BlockSpec((8, LANES), lambda b, n: (0, 0)),
                  pl.BlockSpec((1, d), lambda b, n: (0, 0))],
        out_specs=pl.BlockSpec((blk, nh * d), lambda b, n: (b * nblk + jnp.maximum(n - 1, 0), 0)),
        scratch_shapes=[pltpu.VMEM((HALO + blk, 3 * nh * d), BF16),
                        pltpu.VMEM((nh, d, d), F32),
                        pltpu.VMEM((2, blk, nh * d), F32), pltpu.VMEM((2, blk, nh * d), F32),
                        pltpu.VMEM((2, blk, nh * d), F32), pltpu.VMEM((2, blk, nh * d), BF16),
                        pltpu.VMEM((2, blk, LANES), F32), pltpu.VMEM((2, LANES, blk), F32),
                        pltpu.VMEM((2, blk, LANES), F32)],
        compiler_params=_cparams(("arbitrary", "arbitrary")),
        name="gdn_core",
    )(proj, gates, conv_w.astype(F32), hp, out_norm.reshape(1, d).astype(F32))


TILE_FULL, TILE_FIRST, TILE_HALF = 0, 1, 2


def _swiglu_kernel(te_ref, mode_ref, nv_ref, x_ref, g_ref, a_ref, wp_ref, wg_hbm, wu_hbm, wd_hbm, o_ref,
                   wg_c, wu_c, wd_c, stage_in, stage_out, sems, xres_s, *, pre_norm, pre_proj, routed, tf):
    i = pl.program_id(0)
    nf = wg_c.shape[0]
    e = te_ref[i]
    valid = i < nv_ref[0]

    def chunk_copies(j, slot):
        cols = pl.ds(j * tf, tf)
        return (pltpu.make_async_copy(wg_hbm.at[e, :, cols], stage_in.at[slot, 0], sems.at[slot, 0]),
                pltpu.make_async_copy(wu_hbm.at[e, :, cols], stage_in.at[slot, 1], sems.at[slot, 1]),
                pltpu.make_async_copy(wd_hbm.at[e, cols, :], stage_out.at[slot], sems.at[slot, 2]))

    tile_rows = x_ref.shape[0]

    d_model = o_ref.shape[1]

    def prepare_rows(rows):
        x = x_ref[0:rows, 0:d_model].astype(F32)
        if pre_proj:
            x = x + _dot(a_ref[0:rows, :], wp_ref[...])
            xres_s[0:rows, :] = x
        if pre_norm:
            ms = jnp.mean(x * x, axis=-1, keepdims=True)
            x = (x * lax.rsqrt(ms + EPS)) * g_ref[...]
        return x.astype(BF16)

    def chunk(xb, j):
        hid = _silu(_dot(xb, wg_c[j]), base2=not routed) * _dot(xb, wu_c[j])
        return _dot(hid.astype(BF16), wd_c[j])

    def finish(acc, rows):
        if pre_norm:
            res = xres_s[0:rows, :] if pre_proj else x_ref[0:rows, 0:d_model]
            acc = res + acc
        o_ref[0:rows, :] = acc.astype(o_ref.dtype)
        if rows < tile_rows:
            o_ref[rows:tile_rows, :] = jnp.zeros((tile_rows - rows, o_ref.shape[1]), o_ref.dtype)

    mode = mode_ref[i]

    @pl.when(valid & (mode == TILE_FIRST))
    def _():
        for c in chunk_copies(0, 0):
            c.start()
        xb = prepare_rows(tile_rows)
        acc = None
        for j in range(nf):
            slot = j % 2
            if j + 1 < nf:
                for c in chunk_copies(j + 1, 1 - slot):
                    c.start()
            for c in chunk_copies(j, slot):
                c.wait()
            wg_c[j] = stage_in[slot, 0].astype(BF16)
            wu_c[j] = stage_in[slot, 1].astype(BF16)
            wd_c[j] = stage_out[slot].astype(BF16)
            y = chunk(xb, j)
            acc = y if acc is None else acc + y
        finish(acc, tile_rows)

    def steady(rows):
        xb = prepare_rows(rows)
        acc = None
        for j in range(nf):
            y = chunk(xb, j)
            acc = y if acc is None else acc + y
        finish(acc, rows)

    @pl.when(valid & (mode == TILE_FULL))
    def _():
        steady(tile_rows)

    if routed:
        @pl.when(valid & (mode == TILE_HALF))
        def _():
            steady(tile_rows // 2)

    @pl.when(jnp.logical_not(valid))
    def _():
        o_ref[...] = jnp.zeros_like(o_ref)


def expert_swiglu(x, gain, tile_expert, tile_mode, n_valid, wg, wu, wd, tile_rows, tf, out_dtype, pre_norm, name,
                  proj=None, routed=False):
    n_rows = x.shape[0]
    ne, d, f = wg.shape
    assert n_rows % tile_rows == 0 and f % tf == 0
    n_tiles = n_rows // tile_rows
    nf = f // tf
    pre_proj = proj is not None
    if pre_proj:
        a, wp = proj
        a_spec = pl.BlockSpec((tile_rows, a.shape[1]), lambda i, te, fi, nv: (jnp.minimum(i, nv[0] - 1), 0))
    else:
        a, wp = jnp.zeros((8, LANES), BF16), jnp.zeros((LANES, d), BF16)
        a_spec = pl.BlockSpec(a.shape, lambda i, te, fi, nv: (0, 0))
    grid_spec = pltpu.PrefetchScalarGridSpec(
        num_scalar_prefetch=3,
        grid=(n_tiles,),
        in_specs=[pl.BlockSpec((tile_rows, x.shape[1]), lambda i, te, fi, nv: (jnp.minimum(i, nv[0] - 1), 0)),
                  pl.BlockSpec((1, d), lambda i, te, fi, nv: (0, 0)),
                  a_spec,
                  pl.BlockSpec(wp.shape, lambda i, te, fi, nv: (0, 0)),
                  pl.BlockSpec(memory_space=pl.ANY),
                  pl.BlockSpec(memory_space=pl.ANY),
                  pl.BlockSpec(memory_space=pl.ANY)],
        out_specs=pl.BlockSpec((tile_rows, d), lambda i, te, fi, nv: (i, 0)),
        scratch_shapes=[pltpu.VMEM((nf, d, tf), BF16), pltpu.VMEM((nf, d, tf), BF16), pltpu.VMEM((nf, tf, d), BF16),
                        pltpu.VMEM((2, 2, d, tf), F32), pltpu.VMEM((2, tf, d), F32),
                        pltpu.SemaphoreType.DMA((2, 3)),
                        pltpu.VMEM((tile_rows, d) if pre_proj else (8, LANES), F32)],
    )
    return pl.pallas_call(
        functools.partial(_swiglu_kernel, pre_norm=pre_norm, pre_proj=pre_proj, routed=routed, tf=tf),
        out_shape=jax.ShapeDtypeStruct((n_rows, d), out_dtype),
        grid_spec=grid_spec,
        compiler_params=pltpu.CompilerParams(dimension_semantics=("arbitrary",), vmem_limit_bytes=EXPERT_VMEM_LIMIT),
        name=name,
    )(tile_expert, tile_mode, n_valid, x, gain.reshape(1, d).astype(F32), a, wp, wg, wu, wd)


def ffn_dense(x, gain, wg, wu, wd, tm, tf, proj=None):
    t = x.shape[0]
    n_tiles = t // tm
    tile_mode = jnp.full((n_tiles,), TILE_FULL, jnp.int32).at[0].set(TILE_FIRST)
    return expert_swiglu(x, gain, jnp.zeros((n_tiles,), jnp.int32), tile_mode, jnp.full((1,), n_tiles, jnp.int32),
                         wg[None], wu[None], wd[None], tm, tf, F32, True, "ffn_dense", proj=proj)


def _t5_bucket_np(dist):
    max_exact = N_BUCKETS // 2
    n = np.maximum(dist, 0)
    safe = np.maximum(n, 1).astype(np.float32)
    large = max_exact + (np.log(safe / max_exact) / np.log(MAX_DIST / max_exact)
                         * (N_BUCKETS - max_exact)).astype(np.int32)
    large = np.minimum(large, N_BUCKETS - 1)
    return np.where(n < max_exact, n, large).astype(np.int32)


def _bias_kernel(bucket_ref, valid_ref, rb_ref, o_ref):
    bucket = bucket_ref[...]
    for h in range(SW_HEADS):
        acc = jnp.zeros(bucket.shape, F32)
        for b in range(N_BUCKETS):
            acc = jnp.where(bucket == b, rb_ref[b, h], acc)
        for v in range(valid_ref.shape[0]):
            o_ref[v, h] = jnp.where(valid_ref[v] > 0, acc * LOG2E, NEG_INF)


def bias_table(rel_bias):
    qi = np.arange(WINDOW)[:, None] + WINDOW
    kj = np.arange(2 * WINDOW)[None, :]
    dist = qi - kj
    band = (dist >= 0) & (dist < WINDOW)
    valid = np.stack([band, band & (kj >= WINDOW)]).astype(np.int32)
    return pl.pallas_call(
        _bias_kernel,
        out_shape=jax.ShapeDtypeStruct((2, SW_HEADS, WINDOW, 2 * WINDOW), F32),
        in_specs=[pl.BlockSpec(memory_space=pltpu.VMEM), pl.BlockSpec(memory_space=pltpu.VMEM),
                  pl.BlockSpec(memory_space=pltpu.SMEM)],
        out_specs=pl.BlockSpec(memory_space=pltpu.VMEM),
        name="t5_bias_table",
    )(jnp.asarray(_t5_bucket_np(dist)), jnp.asarray(valid), rel_bias.astype(F32))


def _swa_kernel(q_ref, kvp_ref, kvc_ref, bias_ref, qn_ref, kn_ref, sink_ref, o_ref):
    blk, hd = WINDOW, SW_HD
    kv_w = SW_KV_HEADS * hd
    first = jnp.where(pl.program_id(1) == 0, 1, 0)
    gw = 2 * LANES
    gi = lax.broadcasted_iota(jnp.int32, (gw, gw), 0)
    gj = lax.broadcasted_iota(jnp.int32, (gw, gw), 1)
    group_ones = jnp.where((gi // hd) == (gj // hd), 1.0, 0.0).astype(BF16)
    lane = lax.broadcasted_iota(jnp.int32, (1, LANES), 1)
    low_half = lane < hd

    def head_norm(x, gain):
        cols = []
        for c0 in range(0, x.shape[1], gw):
            xc = x[:, c0:c0 + gw]
            ss = _dot((xc * xc).astype(BF16), group_ones)
            cols.append(xc * lax.rsqrt(ss * (1.0 / hd) + EPS))
        return jnp.concatenate(cols, axis=1) * gain

    def dup_half(x, half):
        swapped = pltpu.roll(x, hd, 1)
        return jnp.where(low_half == (half == 0), x, swapped)

    n_qb = q_ref.shape[0] // blk
    qn = head_norm(q_ref[...].astype(F32), qn_ref[...]) * ((hd ** -0.5) * LOG2E)
    half_sel = [jnp.where(low_half, 1.0, 0.0), jnp.where(low_half, 0.0, 1.0)]
    k_all = jnp.concatenate([kvp_ref[:, 0:kv_w], kvc_ref[:, 0:kv_w]], axis=0).astype(F32)
    kn = head_norm(k_all, kn_ref[...])
    v_all = jnp.concatenate([kvp_ref[:, kv_w:2 * kv_w], kvc_ref[:, kv_w:2 * kv_w]], axis=0).astype(F32)
    ks, vs = [], []
    for g in range(SW_KV_HEADS):
        c0 = (g // 2) * LANES
        ks.append(dup_half(kn[:, c0:c0 + LANES], g % 2).astype(BF16))
        vs.append(dup_half(v_all[:, c0:c0 + LANES], g % 2).astype(BF16))

    pairs = [(j, hq) for j in range(n_qb) for hq in range(SW_HEADS)]
    scores = {}
    for (j, hq) in pairs:
        c0 = (hq // 2) * LANES
        q_h = (qn[j * blk:(j + 1) * blk, c0:c0 + LANES] * half_sel[hq % 2]).astype(BF16)
        scores[(j, hq)] = _dot_nt(q_h, ks[hq // SW_GROUP][j * blk:(j + 2) * blk])
    probs = {}
    for (j, hq) in pairs:
        variant = first if j == 0 else 0
        s = scores[(j, hq)] + bias_ref[variant, hq]
        sink = sink_ref[hq] * LOG2E
        mx = jnp.maximum(jnp.max(s, axis=-1, keepdims=True), sink)
        p = jnp.exp2(s - mx)
        denom = jnp.sum(p, axis=-1, keepdims=True) + jnp.exp2(sink - mx)
        probs[(j, hq)] = (p * (1.0 / denom)).astype(BF16)
    outs = {key: _dot(probs[key], vs[key[1] // SW_GROUP][key[0] * blk:(key[0] + 2) * blk]) for key in pairs}
    for j in range(n_qb):
        for c in range(SW_HEADS // 2):
            o_ref[j * blk:(j + 1) * blk, c * LANES:(c + 1) * LANES] = jnp.where(
                low_half, outs[(j, 2 * c)], outs[(j, 2 * c + 1)]).astype(o_ref.dtype)


def swa_attention(q, kv, bias, q_norm, k_norm, sinks, batch, seq):
    t = q.shape[0]
    blk = WINDOW
    step = SWA_QBLOCKS * blk
    assert seq % step == 0
    nb = seq // step
    qw = SW_HEADS * SW_HD
    kvw = 2 * SW_KV_HEADS * SW_HD
    return pl.pallas_call(
        _swa_kernel,
        out_shape=jax.ShapeDtypeStruct((t, qw), BF16),
        grid=(batch, nb),
        in_specs=[pl.BlockSpec((step, qw), lambda b, n: (b * nb + n, 0)),
                  pl.BlockSpec((blk, kvw), lambda b, n: (jnp.maximum((b * nb + n) * SWA_QBLOCKS - 1, b * nb * SWA_QBLOCKS), 0)),
                  pl.BlockSpec((step, kvw), lambda b, n: (b * nb + n, 0)),
                  pl.BlockSpec((2, SW_HEADS, blk, 2 * blk), lambda b, n: (0, 0, 0, 0)),
                  pl.BlockSpec((1, qw), lambda b, n: (0, 0)),
                  pl.BlockSpec((1, kvw // 2), lambda b, n: (0, 0)),
                  pl.BlockSpec(memory_space=pltpu.SMEM)],
        out_specs=pl.BlockSpec((step, qw), lambda b, n: (b * nb + n, 0)),
        compiler_params=_cparams(("parallel", "parallel")),
        name="swa_attention",
    )(q, kv, kv, bias, jnp.tile(q_norm.astype(F32), SW_HEADS).reshape(1, qw),
      jnp.tile(k_norm.astype(F32), SW_KV_HEADS).reshape(1, kvw // 2), sinks.astype(F32))


def _route_kernel(x_ref, a_ref, wp_ref, g_ref, wr_ref, h_ref, r_ref, wt_ref, tab_ref, cnt_ref,
                  sel_s, gw_s, cnt_s, start_s, run_s, *, tile_rows):
    ne = N_EXPERTS
    p = pl.program_id(0)
    i = pl.program_id(1)
    tm = x_ref.shape[0]
    sub = lax.broadcasted_iota(jnp.int32, (ne, tm), 0).astype(F32)

    def seg_rows(sel):
        n = jnp.sum(sel, axis=1, keepdims=True)
        return jnp.floor((n + (SEG_ALIGN - 1)) * (1.0 / SEG_ALIGN)) * SEG_ALIGN

    def excl_cumsum_experts(v):
        sub8 = lax.broadcasted_iota(jnp.int32, v.shape, 0)
        out = jnp.zeros_like(v)
        for e in range(ne - 1):
            out = out + jnp.where(sub8 > e, v[e:e + 1, :], 0.0)
        return out

    @pl.when(p == 0)
    def _():
        @pl.when(i == 0)
        def _():
            cnt_s[...] = jnp.zeros_like(cnt_s)

        x = x_ref[...] + _dot(a_ref[...], wp_ref[...])
        h_ref[...] = x
        ms = jnp.mean(x * x, axis=-1, keepdims=True)
        xn32 = (x * lax.rsqrt(ms + EPS)) * g_ref[...]
        xn_hi = xn32.astype(BF16)
        xn_lo = (xn32 - xn_hi.astype(F32)).astype(BF16)
        p_hi = _dot_nt(wr_ref[...], xn_hi)
        p_lo = _dot_nt(wr_ref[...], xn_lo)
        logits = p_hi[0:ne] + p_hi[ne:2 * ne] + p_lo[0:ne]
        m1 = jnp.max(logits, axis=0, keepdims=True)
        i1 = jnp.min(jnp.where(logits == m1, sub, float(ne)), axis=0, keepdims=True)
        l2 = jnp.where(sub == i1, -jnp.inf, logits)
        m2 = jnp.max(l2, axis=0, keepdims=True)
        i2 = jnp.min(jnp.where(l2 == m2, sub, float(ne)), axis=0, keepdims=True)
        e2 = jnp.exp(m2 - m1)
        w1 = 1.0 / (1.0 + e2)
        w2 = e2 / (1.0 + e2)
        sel = jnp.where((sub == i1) | (sub == i2), 1.0, 0.0)
        sel_s[i] = sel
        gw_s[i] = jnp.where(sub == i1, w1, jnp.where(sub == i2, w2, 0.0))
        cnt_s[...] += seg_rows(sel)

    @pl.when(p == 1)
    def _():
        @pl.when(i == 0)
        def _():
            cnt = cnt_s[...]
            padded = jnp.floor((cnt + (tile_rows - 1)) * (1.0 / tile_rows)) * tile_rows
            start_s[...] = excl_cumsum_experts(padded)
            run_s[...] = jnp.zeros_like(run_s)
            cnt_ref[...] = cnt

        sel = sel_s[i]
        gw = gw_s[i]
        ti = lax.broadcasted_iota(jnp.int32, (tm, tm), 0)
        tj = lax.broadcasted_iota(jnp.int32, (tm, tm), 1)
        tri = jnp.where(ti <= tj, 1.0, 0.0).astype(BF16)
        csum = _dot(sel.astype(BF16), tri)
        seg = jnp.broadcast_to(seg_rows(sel), run_s.shape)
        local0 = excl_cumsum_experts(seg)
        tab_ref[0, 0] = start_s[...] + run_s[...]
        tab_ref[0, 1] = seg
        tab_ref[0, 2] = local0
        run_s[...] += seg
        local_row = local0[:, 0:1] + csum - sel
        ia = jnp.min(jnp.where(sel > 0.0, sub, float(ne)), axis=0, keepdims=True)
        ib = jnp.max(jnp.where(sel > 0.0, sub, -1.0), axis=0, keepdims=True)
        pick_a = sub == ia
        pick_b = sub == ib
        rows = [jnp.sum(jnp.where(pick_a, local_row, 0.0), axis=0, keepdims=True),
                jnp.sum(jnp.where(pick_b, local_row, 0.0), axis=0, keepdims=True),
                jnp.sum(jnp.where(pick_a, gw, 0.0), axis=0, keepdims=True),
                jnp.sum(jnp.where(pick_b, gw, 0.0), axis=0, keepdims=True)]
        r_ref[...] = jnp.concatenate(rows + [jnp.zeros((ne - 4, tm), F32)], axis=0)
        wpad = jnp.concatenate(rows[2:4] + rows[0:2] + [jnp.zeros((LANES - 4, tm), F32)], axis=0)
        wt_ref[...] = wpad.T


def moe_route(x, a, wp, gain, w_router, tm, tile_rows):
    t, d = x.shape
    ne = w_router.shape[1]
    assert ne == N_EXPERTS
    w_hi = w_router.astype(BF16)
    w_lo = (w_router - w_hi.astype(F32)).astype(BF16)
    wr = jnp.concatenate([w_hi.T, w_lo.T], axis=0)
    tm = min(tm, t)
    nt = t // tm

    def row_map(p, i):
        return (i * (1 - p) + (nt - 1) * p, 0)

    return pl.pallas_call(
        functools.partial(_route_kernel, tile_rows=tile_rows),
        out_shape=(jax.ShapeDtypeStruct((t, d), F32),
                   jax.ShapeDtypeStruct((ne, t), F32), jax.ShapeDtypeStruct((t, LANES), F32),
                   jax.ShapeDtypeStruct((nt, 3, ne, LANES), F32), jax.ShapeDtypeStruct((ne, LANES), F32)),
        grid=(2, nt),
        in_specs=[pl.BlockSpec((tm, d), row_map),
                  pl.BlockSpec((tm, a.shape[1]), row_map),
                  pl.BlockSpec(wp.shape, lambda p, i: (0, 0)),
                  pl.BlockSpec((1, d), lambda p, i: (0, 0)),
                  pl.BlockSpec((2 * ne, d), lambda p, i: (0, 0))],
        out_specs=(pl.BlockSpec((tm, d), row_map),
                   pl.BlockSpec((ne, tm), lambda p, i: (0, i * p)),
                   pl.BlockSpec((tm, LANES), lambda p, i: (i * p, 0)),
                   pl.BlockSpec((1, 3, ne, LANES), lambda p, i: (i * p, 0, 0, 0)),
                   pl.BlockSpec((ne, LANES), lambda p, i: (0, 0))),
        scratch_shapes=[pltpu.VMEM((nt, ne, tm), F32), pltpu.VMEM((nt, ne, tm), F32),
                        pltpu.VMEM((ne, LANES), F32), pltpu.VMEM((ne, LANES), F32), pltpu.VMEM((ne, LANES), F32)],
        compiler_params=_cparams(("arbitrary", "arbitrary")),
        name="moe_route",
    )(x, a, wp, gain.reshape(1, d), wr)


def _segment_copies(tab_ref, i, e, local_ref, slot_ref, sem, to_slots):
    base = (i * N_EXPERTS + e) * 3
    slot0, rows, local0 = tab_ref[base], tab_ref[base + 1], tab_ref[base + 2]
    out = []
    done = 0
    size = MOE_TOKEN_TILE
    while size >= SEG_ALIGN:
        take = rows & size
        loc = local_ref.at[pl.ds(pl.multiple_of(local0 + done, SEG_ALIGN), size)]
        slt = slot_ref.at[pl.ds(pl.multiple_of(slot0 + done, SEG_ALIGN), size)]
        desc = pltpu.make_async_copy(loc, slt, sem) if to_slots else pltpu.make_async_copy(slt, loc, sem)
        out.append((take != 0, desc))
        done = done + take
        size //= 2
    return out


def _run_segment_copies(tab_ref, tile, slot, rows_s, slot_ref, sems, to_slots, action):
    for e in range(N_EXPERTS):
        for cond, desc in _segment_copies(tab_ref, tile, e, rows_s.at[slot], slot_ref, sems.at[slot], to_slots):
            @pl.when(cond)
            def _():
                getattr(desc, action)()


def _dispatch_kernel(tab_ref, zf_ref, x_ref, g_ref, r_ref, xs_ref, rows_s, zero_s, sem, zsem, *, tile_rows):
    i = pl.program_id(0)
    tm = x_ref.shape[0]
    n_local = rows_s.shape[1]

    @pl.when(i == 0)
    def _():
        zero_s[...] = jnp.zeros_like(zero_s)

        def zero_copy(e):
            row0 = pl.multiple_of(zf_ref[e], tile_rows)
            return pltpu.make_async_copy(zero_s, xs_ref.at[pl.ds(row0, tile_rows)], zsem)

        for e in range(zf_ref.shape[0]):
            @pl.when(zf_ref[e] >= 0)
            def _():
                zero_copy(e).start()
        for e in range(zf_ref.shape[0]):
            @pl.when(zf_ref[e] >= 0)
            def _():
                zero_copy(e).wait()

    x = x_ref[...]
    ms = jnp.mean(x * x, axis=-1, keepdims=True)
    xn = ((x * lax.rsqrt(ms + EPS)) * g_ref[...]).astype(BF16)
    row_id = lax.broadcasted_iota(jnp.int32, (n_local, tm), 0).astype(F32)
    onehot = jnp.where((row_id == r_ref[0:1, :]) | (row_id == r_ref[1:2, :]), 1.0, 0.0).astype(BF16)
    slot = lax.rem(i, 2)
    rows_s[slot] = _dot(onehot, xn)

    _run_segment_copies(tab_ref, i, slot, rows_s, xs_ref, sem, True, "start")

    @pl.when(i > 0)
    def _():
        _run_segment_copies(tab_ref, i - 1, 1 - slot, rows_s, xs_ref, sem, True, "wait")

    @pl.when(i == pl.num_programs(0) - 1)
    def _():
        _run_segment_copies(tab_ref, i, slot, rows_s, xs_ref, sem, True, "wait")


def moe_dispatch(x, gain, r, tab, zf_rows, n_slots, tm, tile_rows):
    t, d = x.shape
    nt = t // tm
    n_local = TOP_K * tm + N_EXPERTS * SEG_ALIGN
    grid_spec = pltpu.PrefetchScalarGridSpec(
        num_scalar_prefetch=2,
        grid=(nt,),
        in_specs=[pl.BlockSpec((tm, d), lambda i, tb, zf: (i, 0)),
                  pl.BlockSpec((1, d), lambda i, tb, zf: (0, 0)),
                  pl.BlockSpec((N_EXPERTS, tm), lambda i, tb, zf: (0, i))],
        out_specs=pl.BlockSpec(memory_space=pl.ANY),
        scratch_shapes=[pltpu.VMEM((2, n_local, d), F32), pltpu.VMEM((tile_rows, d), F32),
                        pltpu.SemaphoreType.DMA((2,)), pltpu.SemaphoreType.DMA],
    )
    return pl.pallas_call(
        functools.partial(_dispatch_kernel, tile_rows=tile_rows),
        out_shape=jax.ShapeDtypeStruct((n_slots, d), F32),
        grid_spec=grid_spec,
        compiler_params=_cparams(("arbitrary",)),
        name="moe_dispatch",
    )(tab, zf_rows, x, gain.reshape(1, d), r)


def _combine_kernel(tab_ref, h_ref, wt_ref, ys_ref, o_ref, rows_s, sems):
    i = pl.program_id(0)
    tm = h_ref.shape[0]
    n_local = rows_s.shape[1]
    slot = lax.rem(i, 2)

    def fetch(tile, into):
        rows_s[into] = jnp.zeros(rows_s.shape[1:], rows_s.dtype)
        _run_segment_copies(tab_ref, tile, into, rows_s, ys_ref, sems, False, "start")

    @pl.when(i == 0)
    def _():
        fetch(i, slot)

    @pl.when(i + 1 < pl.num_programs(0))
    def _():
        fetch(i + 1, 1 - slot)

    _run_segment_copies(tab_ref, i, slot, rows_s, ys_ref, sems, False, "wait")

    wt = wt_ref[...]
    y = rows_s[slot].astype(BF16)
    col_id = lax.broadcasted_iota(jnp.int32, (tm, n_local), 1).astype(F32)
    pick_a = jnp.where(col_id == wt[:, 2:3], 1.0, 0.0).astype(BF16)
    pick_b = jnp.where(col_id == wt[:, 3:4], 1.0, 0.0).astype(BF16)
    o_ref[...] = h_ref[...] + wt[:, 0:1] * _dot(pick_a, y) + wt[:, 1:2] * _dot(pick_b, y)


def moe_combine(h, wt, tab, ys, tm):
    t, d = h.shape
    nt = t // tm
    n_local = TOP_K * tm + N_EXPERTS * SEG_ALIGN
    grid_spec = pltpu.PrefetchScalarGridSpec(
        num_scalar_prefetch=1,
        grid=(nt,),
        in_specs=[pl.BlockSpec((tm, d), lambda i, tb: (i, 0)),
                  pl.BlockSpec((tm, LANES), lambda i, tb: (i, 0)),
                  pl.BlockSpec(memory_space=pl.ANY)],
        out_specs=pl.BlockSpec((tm, d), lambda i, tb: (i, 0)),
        scratch_shapes=[pltpu.VMEM((2, n_local, d), F32), pltpu.SemaphoreType.DMA((2,))],
    )
    return pl.pallas_call(
        _combine_kernel,
        out_shape=jax.ShapeDtypeStruct((t, d), F32),
        grid_spec=grid_spec,
        compiler_params=_cparams(("arbitrary",)),
        name="moe_combine",
    )(tab, h, wt, ys)


def moe_layer(x, a, wp, gain, w_router, wg, wu, wd):
    t, d = x.shape
    ne = w_router.shape[1]
    tr, tm = MOE_TILE_ROWS, MOE_TOKEN_TILE
    assert t % tm == 0 and ne == N_EXPERTS
    nt = t // tm
    n_tiles = -(-(TOP_K * t + nt * ne * (SEG_ALIGN - 1) + ne * (tr - 1)) // tr)
    n_slots = n_tiles * tr

    h, r, wt, tab, cnt = moe_route(x, a, wp, gain, w_router, tm, tr)
    tab = jnp.transpose(tab[:, :, :, 0], (0, 2, 1)).astype(jnp.int32).reshape(-1)

    counts = cnt[:, 0].astype(jnp.int32)
    padded = ((counts + (tr - 1)) // tr) * tr
    ends = jnp.cumsum(padded)
    n_valid = (ends[-1] // tr).astype(jnp.int32)
    tile_row0 = jnp.arange(n_tiles, dtype=jnp.int32) * tr
    tile_expert = jnp.sum((tile_row0[:, None] >= ends[None, :]).astype(jnp.int32), axis=1)
    tile_expert = jnp.minimum(tile_expert, ne - 1)
    tile_expert = jnp.where(jnp.arange(n_tiles) < n_valid, tile_expert, tile_expert[jnp.maximum(n_valid - 1, 0)])
    prev_expert = jnp.concatenate([jnp.full((1,), -1, jnp.int32), tile_expert[:-1]])
    rows_used = (ends - padded + counts)[tile_expert] - tile_row0
    tile_mode = jnp.where(tile_expert != prev_expert, TILE_FIRST,
                          jnp.where(rows_used <= tr // 2, TILE_HALF, TILE_FULL)).astype(jnp.int32)
    tail = jnp.arange(TOP_K * t // tr, n_tiles, dtype=jnp.int32)
    zf_rows = jnp.concatenate([jnp.where(padded > 0, ends - tr, -1),
                               jnp.where(tail >= n_valid, tail * tr, -1)]).astype(jnp.int32)

    xs = moe_dispatch(h, gain, r, tab, zf_rows, n_slots, tm, tr)
    ys = expert_swiglu(xs, gain, tile_expert, tile_mode, n_valid.reshape(1), wg, wu, wd, tr, FFN_CHUNK, F32, False,
                       "moe_experts", routed=True)
    return moe_combine(h, wt, tab, ys, tm)


def kernel(x, a_norm, a_w_in, a_conv, a_log_decay, a_dt_bias, a_out_norm, a_w_out, kv_norm, kv_w, k_norm,
           b_norm, b_w_q, q_norm, b_sinks, b_w_o, rel_bias, ffn_norm, dense_w_gate, dense_w_up, dense_w_down,
           moe_router, moe_w_gate, moe_w_up, moe_w_down):
    batch, seq, d = x.shape
    t = batch * seq
    nh, hd = LA_HEADS, LA_D
    main_w = 4 * nh * hd
    h0 = x.reshape(t, d)

    w_in = a_w_in[0]
    w_main = w_in[:, 0:main_w].astype(BF16)
    w_gate = jnp.zeros((d, LANES), BF16).at[:, 0:2 * nh].set(w_in[:, main_w:main_w + 2 * nh].astype(BF16))
    proj, gates = norm_matmul(h0, [(a_norm[0], w_main, BF16), (a_norm[0], w_gate, F32)], IN_PROJ_TILE,
                              "gdn_in_proj")
    o = gdn_core(proj, gates, a_conv[0], a_log_decay[0], a_dt_bias[0], a_out_norm[0], batch, seq)

    h2 = ffn_dense(h0, ffn_norm[0], dense_w_gate[0], dense_w_up[0], dense_w_down[0], MOE_TILE_ROWS, FFN_CHUNK,
                   proj=(o, a_w_out[0].astype(BF16)))

    kv, q = norm_matmul(h2, [(kv_norm, kv_w.astype(BF16), BF16), (b_norm[0], b_w_q[0].astype(BF16), BF16)],
                        QKV_PROJ_TILE, "qkv_proj")
    bias = bias_table(rel_bias)
    attn = swa_attention(q, kv, bias, q_norm[0], k_norm, b_sinks[0], batch, seq)

    h4 = moe_layer(h2, attn, b_w_o[0].astype(BF16), ffn_norm[1], moe_router[0], moe_w_gate[0], moe_w_up[0],
                   moe_w_down[0])
    return h4.reshape(batch, seq, d)
```

```python
import functools

import numpy as np
import jax
import jax.numpy as jnp
from jax import lax
from jax.experimental import pallas as pl
from jax.experimental.pallas import tpu as pltpu

F32 = jnp.float32
BF16 = jnp.bfloat16

EPS = 1e-6
NEG_INF = -1e30

LA_HEADS = 8
LA_D = 128
CONV_W = 4
CHUNK = 64
SW_HEADS = 16
SW_KV_HEADS = 4
SW_GROUP = SW_HEADS // SW_KV_HEADS
SW_HD = 64
WINDOW = 128
SWA_QBLOCKS = 2
N_BUCKETS = 32
MAX_DIST = 128
N_EXPERTS = 8
TOP_K = 2
LOG2E = float(np.log2(np.e))

LANES = 128
SEG_ALIGN = 8
GDN_BLOCK = 2 * CHUNK
HALO = 16
MOE_TILE_ROWS = 512
MOE_TOKEN_TILE = 512
FFN_CHUNK = 512
IN_PROJ_TILE = 1024
QKV_PROJ_TILE = 2048

VMEM_LIMIT = 56 * 1024 * 1024
EXPERT_VMEM_LIMIT = 60 * 1024 * 1024


def _cparams(sem):
    return pltpu.CompilerParams(dimension_semantics=sem, vmem_limit_bytes=VMEM_LIMIT)


def _silu(x, base2=True):
    e = jnp.exp2(x * (-LOG2E)) if base2 else jnp.exp(-x)
    return x * (1.0 / (1.0 + e))


def _dot(a, b):
    return jnp.dot(a, b, preferred_element_type=F32)


def _dot_nt(a, b):
    return lax.dot_general(a, b, (((1,), (1,)), ((), ())), preferred_element_type=F32)


def _norm_matmul_kernel(*refs, n_groups):
    x_ref = refs[0]
    g_refs = refs[1:1 + n_groups]
    w_refs = refs[1 + n_groups:1 + 2 * n_groups]
    o_refs = refs[1 + 2 * n_groups:1 + 3 * n_groups]
    x = x_ref[...]
    xr = x * lax.rsqrt(jnp.mean(x * x, axis=-1, keepdims=True) + EPS)
    for g_ref, w_ref, o_ref in zip(g_refs, w_refs, o_refs):
        o_ref[...] = _dot((xr * g_ref[...]).astype(BF16), w_ref[...]).astype(o_ref.dtype)


def norm_matmul(x, groups, tm, name):
    t, d = x.shape
    tm = min(tm, t)
    assert t % tm == 0
    gains = [g.reshape(1, d).astype(F32) for g, _, _ in groups]
    ws = [w for _, w, _ in groups]
    return pl.pallas_call(
        functools.partial(_norm_matmul_kernel, n_groups=len(groups)),
        out_shape=[jax.ShapeDtypeStruct((t, w.shape[1]), dt) for _, w, dt in groups],
        grid=(t // tm,),
        in_specs=([pl.BlockSpec((tm, d), lambda i: (i, 0))]
                  + [pl.BlockSpec((1, d), lambda i: (0, 0)) for _ in groups]
                  + [pl.BlockSpec(w.shape, lambda i: (0, 0)) for w in ws]),
        out_specs=[pl.BlockSpec((tm, w.shape[1]), lambda i: (i, 0)) for w in ws],
        compiler_params=_cparams(("parallel",)),
        name=name,
    )(x, *gains, *ws)


def _gdn_kernel(proj_ref, gates_ref, convw_ref, hp_ref, onorm_ref, o_ref,
                xs_ref, state_ref, q_s, k_s, v_s, z_s, gc_s, gct_s, beta_s):
    n = pl.program_id(1)

    @pl.when(n == 0)
    def _():
        xs_ref[0:HALO, :] = jnp.zeros((HALO, xs_ref.shape[1]), xs_ref.dtype)
        for ref in (q_s, k_s, v_s, z_s, gc_s, gct_s, beta_s):
            ref[1] = jnp.zeros(ref.shape[1:], ref.dtype)

    @pl.when(n <= 1)
    def _():
        state_ref[...] = jnp.zeros_like(state_ref)

    args = (proj_ref, gates_ref, convw_ref, hp_ref, onorm_ref, o_ref, xs_ref, state_ref,
            q_s, k_s, v_s, z_s, gc_s, gct_s, beta_s)

    @pl.when(lax.rem(n, 2) == 0)
    def _():
        _gdn_step(*args, slot_w=0, slot_r=1)

    @pl.when(lax.rem(n, 2) == 1)
    def _():
        _gdn_step(*args, slot_w=1, slot_r=0)


def _gdn_step(proj_ref, gates_ref, convw_ref, hp_ref, onorm_ref, o_ref, xs_ref, state_ref,
              q_s, k_s, v_s, z_s, gc_s, gct_s, beta_s, *, slot_w, slot_r):
    nh, d, c = LA_HEADS, LA_D, CHUNK
    blk = GDN_BLOCK
    qkv_w = 3 * nh * d

    gc = gc_s[slot_r]
    gc_t = gct_s[slot_r]
    beta = beta_s[slot_r]

    xs_ref[HALO:HALO + blk, :] = proj_ref[:, 0:qkv_w]

    def front_gates():
        _gdn_front_gates(gates_ref, hp_ref, gc_s, gct_s, beta_s, slot_w)

    di = lax.broadcasted_iota(jnp.int32, (d, d), 0)
    dj = lax.broadcasted_iota(jnp.int32, (d, d), 1)
    eye_d = jnp.where(di == dj, 1.0, 0.0).astype(BF16)

    onorm = onorm_ref[...]

    n_shift = CONV_W - 1
    sr = lax.broadcasted_iota(jnp.int32, (n_shift * blk, HALO + blk), 0)
    sc = lax.broadcasted_iota(jnp.int32, (n_shift * blk, HALO + blk), 1)
    shift_mat = jnp.where(sc == HALO + (sr % blk) - (sr // blk + 1), 1.0, 0.0).astype(BF16)
    pair_w = 2 * d

    def conv_silu(col0):
        cols = slice(col0, col0 + pair_w)
        shifted = _dot(shift_mat, xs_ref[:, cols])
        acc = convw_ref[CONV_W - 1:CONV_W, cols] * xs_ref[HALO:HALO + blk, cols].astype(F32)
        for s in range(1, CONV_W):
            acc = acc + convw_ref[CONV_W - 1 - s:CONV_W - s, cols] * shifted[(s - 1) * blk:s * blk]
        return _silu(acc)

    def front_pair(hp):
        c0 = hp * pair_w
        qf = conv_silu(c0)
        kf = conv_silu(nh * d + c0)
        v_s[slot_w, :, c0:c0 + pair_w] = conv_silu(2 * nh * d + c0)
        for half in range(2):
            lo, hi = half * d, (half + 1) * d
            qh, kh = qf[:, lo:hi], kf[:, lo:hi]
            q_s[slot_w, :, c0 + lo:c0 + hi] = qh * (lax.rsqrt(jnp.sum(qh * qh, axis=-1, keepdims=True) + EPS)
                                                    * (d ** -0.5))
            k_s[slot_w, :, c0 + lo:c0 + hi] = kh * lax.rsqrt(jnp.sum(kh * kh, axis=-1, keepdims=True) + EPS)
        z_s[slot_w, :, c0:c0 + pair_w] = proj_ref[:, qkv_w + c0:qkv_w + c0 + pair_w]

    front_tasks = [front_gates] + [functools.partial(front_pair, hp) for hp in range(nh // 2)]

    def run_front_task():
        if front_tasks:
            front_tasks.pop(0)()

    assert blk == 2 * c and 2 * c == LANES and d == LANES
    si = lax.broadcasted_iota(jnp.int32, (c, 2 * c), 0)
    sl = lax.broadcasted_iota(jnp.int32, (c, 2 * c), 1)
    first_chunk = sl < c
    sj = jnp.where(first_chunk, sl, sl - c)
    lower_incl = si >= sj
    strict = si > sj
    eye_pair = jnp.where(si == sj, 1.0, 0.0).astype(F32)
    lane_row = lax.broadcasted_iota(jnp.int32, (1, 2 * c), 1) < c
    zeros_cd = jnp.zeros((c, d), BF16)

    def block_diag(m):
        return jnp.concatenate([jnp.where(first_chunk, m, 0.0), jnp.where(first_chunk, 0.0, m)], axis=0).astype(BF16)

    st = []
    for h in range(nh):
        hs = slice(h * d, (h + 1) * d)
        q = q_s[slot_r, :, hs]
        k = k_s[slot_r, :, hs]
        v = v_s[slot_r, :, hs]
        g_col = gc[:, nh + h:nh + h + 1]
        g_row = gc_t[nh + h:nh + h + 1, :]
        b_col = beta[:, h:h + 1]
        g_col_pair = jnp.where(first_chunk, g_col[0:c], g_col[c:2 * c])
        g_last = jnp.where(lane_row, g_col[c - 1:c], g_col[2 * c - 1:2 * c])
        decay = jnp.where(lower_incl, jnp.exp2(jnp.where(lower_incl, g_col_pair - g_row, 0.0)), 0.0)
        k_beta = k * b_col
        e_col = jnp.exp2(g_col)
        kb, qb, kbf = k_beta.astype(BF16), q.astype(BF16), k.astype(BF16)
        lhs = jnp.concatenate([jnp.concatenate([kb[0:c], kb[c:2 * c]], axis=1),
                               jnp.concatenate([qb[0:c], qb[c:2 * c]], axis=1),
                               jnp.concatenate([eye_d, eye_d], axis=1)], axis=0)
        k_diag = jnp.concatenate([jnp.concatenate([kbf[0:c], zeros_cd], axis=1),
                                  jnp.concatenate([zeros_cd, kbf[c:2 * c]], axis=1)], axis=0)
        kk = _dot_nt(lhs, k_diag)
        vb, kbe = (v * b_col).astype(BF16), (k_beta * e_col).astype(BF16)
        zeros_2 = jnp.zeros((c, 2 * d), BF16)
        st.append(dict(
            a=jnp.where(strict, kk[0:c] * decay, 0.0),
            attn=kk[c:2 * c] * decay,
            k_tail_t=kk[2 * c:2 * c + d] * jnp.exp2(g_last - g_row),
            rhs=jnp.concatenate([jnp.concatenate([vb[0:c], kbe[0:c], zeros_2], axis=1),
                                 jnp.concatenate([zeros_2, vb[c:2 * c], kbe[c:2 * c]], axis=1)], axis=0),
            qe=(q * e_col).astype(BF16),
            e_last=[jnp.exp2(g_col[c - 1:c]), jnp.exp2(g_col[2 * c - 1:2 * c])]))
    run_front_task()

    for cur in st:
        x = -cur["a"]
        cur["y"] = _dot(x.astype(BF16), block_diag(x))
        cur["p"] = eye_pair + x
    run_front_task()
    n_levels = int(np.log2(c))
    for lvl in range(1, n_levels):
        for cur in st:
            y_bd = block_diag(cur["y"])
            p = cur["p"]
            if lvl + 1 < n_levels:
                zz = _dot(jnp.concatenate([cur["y"].astype(BF16), p.astype(BF16)], axis=0), y_bd)
                cur["y"] = zz[0:c]
                cur["p"] = p + zz[c:2 * c]
            else:
                cur["p"] = p + _dot(p.astype(BF16), y_bd)
        run_front_task()
    for cur in st:
        cur["uw"] = _dot(cur["p"].astype(BF16), cur["rhs"])
    run_front_task()

    for ck in range(2):
        r = ck * c
        in_chunk = first_chunk if ck == 0 else jnp.logical_not(first_chunk)
        in_chunk_d = lane_row if ck == 0 else jnp.logical_not(lane_row)
        s_old = [state_ref[h] for h in range(nh)]
        ws_qs = []
        for h in range(nh):
            cur = st[h]
            w = cur["uw"][:, (2 * ck + 1) * d:(2 * ck + 2) * d]
            lhs = jnp.concatenate([w.astype(BF16), cur["qe"][r:r + c]], axis=0)
            ws_qs.append(_dot(lhs, s_old[h].astype(BF16)))
        run_front_task()
        for h in range(nh):
            cur = st[h]
            v_new = (cur["uw"][:, 2 * ck * d:(2 * ck + 1) * d] - ws_qs[h][0:c]).astype(BF16)
            lhs = jnp.concatenate([jnp.where(in_chunk, cur["attn"], 0.0).astype(BF16),
                                   jnp.where(in_chunk_d, cur["k_tail_t"], 0.0).astype(BF16)], axis=0)
            rhs = jnp.concatenate([v_new, zeros_cd] if ck == 0 else [zeros_cd, v_new], axis=0)
            av_kv = _dot(lhs, rhs)
            state_ref[h] = s_old[h] * cur["e_last"][ck] + av_kv[c:c + d]
            o = ws_qs[h][c:2 * c] + av_kv[0:c]
            o = (o * lax.rsqrt(jnp.mean(o * o, axis=-1, keepdims=True) + EPS)) * onorm
            z = z_s[slot_r, r:r + c, h * d:(h + 1) * d].astype(F32)
            o_ref[r:r + c, h * d:(h + 1) * d] = (o * _silu(z)).astype(o_ref.dtype)
    while front_tasks:
        run_front_task()

    xs_ref[0:HALO, :] = xs_ref[blk:blk + HALO, :]


def _gdn_front_gates(gates_ref, hp_ref, gc_s, gct_s, beta_s, slot_w):
    blk, c = GDN_BLOCK, CHUNK
    gates = gates_ref[...]
    a_log = hp_ref[0:1, :]
    dt_bias = hp_ref[1:2, :]
    beta = 1.0 / (1.0 + jnp.exp(-gates))
    sp_in = gates + dt_bias
    softplus = jnp.maximum(sp_in, 0.0) + jnp.log(1.0 + jnp.exp(-jnp.abs(sp_in)))
    g = (-jnp.exp(a_log) * softplus) * float(np.log2(np.e))

    row = lax.broadcasted_iota(jnp.int32, (blk, blk), 0)
    col = lax.broadcasted_iota(jnp.int32, (blk, blk), 1)
    tri = jnp.where((row >= col) & ((row // c) == (col // c)), 1.0, 0.0).astype(BF16)
    g_hi = g.astype(BF16)
    g_r1 = g - g_hi.astype(F32)
    g_mid = g_r1.astype(BF16)
    g_lo = (g_r1 - g_mid.astype(F32)).astype(BF16)
    gc = _dot(tri, g_hi) + _dot(tri, g_mid) + _dot(tri, g_lo)
    gc_s[slot_w] = gc
    gct_s[slot_w] = gc.T
    beta_s[slot_w] = beta


def gdn_core(proj, gates, conv_w, a_log, dt_bias, out_norm, batch, seq):
    t = proj.shape[0]
    nh, d = LA_HEADS, LA_D
    blk = GDN_BLOCK
    assert seq % blk == 0
    nblk = seq // blk
    hp = jnp.zeros((8, LANES), F32)
    hp = hp.at[0, nh:2 * nh].set(a_log.astype(F32)).at[1, nh:2 * nh].set(dt_bias.astype(F32))

    def in_map(b, n):
        return (b * nblk + jnp.minimum(n, nblk - 1), 0)

    return pl.pallas_call(
        _gdn_kernel,
        out_shape=jax.ShapeDtypeStruct((t, nh * d), BF16),
        grid=(batch, nblk + 1),
        in_specs=[pl.BlockSpec((blk, 4 * nh * d), in_map),
                  pl.BlockSpec((blk, LANES), in_map),
                  pl.BlockSpec((CONV_W, 3 * nh * d), lambda b, n: (0, 0)),
                  pl.BlockSpec((8, LANES), lambda b, n: (0, 0)),
                  pl.BlockSpec((1, d), lambda b, n: (0, 0))],
        out_specs=pl.BlockSpec((blk, nh * d), lambda b, n: (b * nblk + jnp.maximum(n - 1, 0), 0)),
        scratch_shapes=[pltpu.VMEM((HALO + blk, 3 * nh * d), BF16),
                        pltpu.VMEM((nh, d, d), F32),
                        pltpu.VMEM((2, blk, nh * d), F32), pltpu.VMEM((2, blk, nh * d), F32),
                        pltpu.VMEM((2, blk, nh * d), F32), pltpu.VMEM((2, blk, nh * d), BF16),
                        pltpu.VMEM((2, blk, LANES), F32), pltpu.VMEM((2, LANES, blk), F32),
                        pltpu.VMEM((2, blk, LANES), F32)],
        compiler_params=_cparams(("arbitrary", "arbitrary")),
        name="gdn_core",
    )(proj, gates, conv_w.astype(F32), hp, out_norm.reshape(1, d).astype(F32))


TILE_FULL, TILE_FIRST, TILE_HALF = 0, 1, 2


def _swiglu_kernel(te_ref, mode_ref, nv_ref, x_ref, g_ref, a_ref, wp_ref, wg_hbm, wu_hbm, wd_hbm, o_ref,
                   wg_c, wu_c, wd_c, stage_in, stage_out, sems, xres_s, *, pre_norm, pre_proj, routed, tf):
    i = pl.program_id(0)
    nf = wg_c.shape[0]
    valid = i < nv_ref[0]

    def chunk_copies(j, slot, tile=None):
        e = te_ref[i if tile is None else tile]
        cols = pl.ds(j * tf, tf)
        return (pltpu.make_async_copy(wg_hbm.at[e, :, cols], stage_in.at[slot, 0], sems.at[slot, 0]),
                pltpu.make_async_copy(wu_hbm.at[e, :, cols], stage_in.at[slot, 1], sems.at[slot, 1]),
                pltpu.make_async_copy(wd_hbm.at[e, cols, :], stage_out.at[slot], sems.at[slot, 2]))

    nxt = jnp.minimum(i + 1, pl.num_programs(0) - 1)
    prefetch_next = (valid & (mode_ref[i] != TILE_FIRST) & (i + 1 < nv_ref[0]) & (mode_ref[nxt] == TILE_FIRST))
    prv = jnp.maximum(i - 1, 0)
    chunk0_requested = (i > 0) & (mode_ref[prv] != TILE_FIRST)

    @pl.when(prefetch_next)
    def _():
        for c in chunk_copies(0, 0, tile=nxt):
            c.start()

    tile_rows = x_ref.shape[0]

    d_model = o_ref.shape[1]

    def prepare_rows(rows):
        x = x_ref[0:rows, 0:d_model].astype(F32)
        if pre_proj:
            x = x + _dot(a_ref[0:rows, :], wp_ref[...])
            xres_s[0:rows, :] = x
        if pre_norm:
            ms = jnp.mean(x * x, axis=-1, keepdims=True)
            x = (x * lax.rsqrt(ms + EPS)) * g_ref[...]
        return x.astype(BF16)

    def chunk(xb, j):
        hid = _silu(_dot(xb, wg_c[j]), base2=not routed) * _dot(xb, wu_c[j])
        return _dot(hid.astype(BF16), wd_c[j])

    def finish(acc, rows):
        if pre_norm:
            res = xres_s[0:rows, :] if pre_proj else x_ref[0:rows, 0:d_model]
            acc = res + acc
        o_ref[0:rows, :] = acc.astype(o_ref.dtype)
        if rows < tile_rows:
            o_ref[rows:tile_rows, :] = jnp.zeros((tile_rows - rows, o_ref.shape[1]), o_ref.dtype)

    mode = mode_ref[i]

    @pl.when(valid & (mode == TILE_FIRST) & jnp.logical_not(chunk0_requested))
    def _():
        for c in chunk_copies(0, 0):
            c.start()

    @pl.when(valid & (mode == TILE_FIRST))
    def _():
        xb = prepare_rows(tile_rows)
        acc = None
        for j in range(nf):
            slot = j % 2
            if j + 1 < nf:
                for c in chunk_copies(j + 1, 1 - slot):
                    c.start()
            for c in chunk_copies(j, slot):
                c.wait()
            wg_c[j] = stage_in[slot, 0].astype(BF16)
            wu_c[j] = stage_in[slot, 1].astype(BF16)
            wd_c[j] = stage_out[slot].astype(BF16)
            y = chunk(xb, j)
            acc = y if acc is None else acc + y
        finish(acc, tile_rows)

    def steady(rows):
        xb = prepare_rows(rows)
        acc = None
        for j in range(nf):
            y = chunk(xb, j)
            acc = y if acc is None else acc + y
        finish(acc, rows)

    @pl.when(valid & (mode == TILE_FULL))
    def _():
        steady(tile_rows)

    if routed:
        @pl.when(valid & (mode == TILE_HALF))
        def _():
            steady(tile_rows // 2)

    @pl.when(jnp.logical_not(valid))
    def _():
        o_ref[...] = jnp.zeros_like(o_ref)


def expert_swiglu(x, gain, tile_expert, tile_mode, n_valid, wg, wu, wd, tile_rows, tf, out_dtype, pre_norm, name,
                  proj=None, routed=False):
    n_rows = x.shape[0]
    ne, d, f = wg.shape
    assert n_rows % tile_rows == 0 and f % tf == 0
    n_tiles = n_rows // tile_rows
    nf = f // tf
    pre_proj = proj is not None
    if pre_proj:
        a, wp = proj
        a_spec = pl.BlockSpec((tile_rows, a.shape[1]), lambda i, te, fi, nv: (jnp.minimum(i, nv[0] - 1), 0))
    else:
        a, wp = jnp.zeros((8, LANES), BF16), jnp.zeros((LANES, d), BF16)
        a_spec = pl.BlockSpec(a.shape, lambda i, te, fi, nv: (0, 0))
    grid_spec = pltpu.PrefetchScalarGridSpec(
        num_scalar_prefetch=3,
        grid=(n_tiles,),
        in_specs=[pl.BlockSpec((tile_rows, x.shape[1]), lambda i, te, fi, nv: (jnp.minimum(i, nv[0] - 1), 0)),
                  pl.BlockSpec((1, d), lambda i, te, fi, nv: (0, 0)),
                  a_spec,
                  pl.BlockSpec(wp.shape, lambda i, te, fi, nv: (0, 0)),
                  pl.BlockSpec(memory_space=pl.ANY),
                  pl.BlockSpec(memory_space=pl.ANY),
                  pl.BlockSpec(memory_space=pl.ANY)],
        out_specs=pl.BlockSpec((tile_rows, d), lambda i, te, fi, nv: (i, 0)),
        scratch_shapes=[pltpu.VMEM((nf, d, tf), BF16), pltpu.VMEM((nf, d, tf), BF16), pltpu.VMEM((nf, tf, d), BF16),
                        pltpu.VMEM((2, 2, d, tf), F32), pltpu.VMEM((2, tf, d), F32),
                        pltpu.SemaphoreType.DMA((2, 3)),
                        pltpu.VMEM((tile_rows, d) if pre_proj else (8, LANES), F32)],
    )
    return pl.pallas_call(
        functools.partial(_swiglu_kernel, pre_norm=pre_norm, pre_proj=pre_proj, routed=routed, tf=tf),
        out_shape=jax.ShapeDtypeStruct((n_rows, d), out_dtype),
        grid_spec=grid_spec,
        compiler_params=pltpu.CompilerParams(dimension_semantics=("arbitrary",), vmem_limit_bytes=EXPERT_VMEM_LIMIT),
        name=name,
    )(tile_expert, tile_mode, n_valid, x, gain.reshape(1, d).astype(F32), a, wp, wg, wu, wd)


def ffn_dense(x, gain, wg, wu, wd, tm, tf, proj=None):
    t = x.shape[0]
    n_tiles = t // tm
    tile_mode = jnp.full((n_tiles,), TILE_FULL, jnp.int32).at[0].set(TILE_FIRST)
    return expert_swiglu(x, gain, jnp.zeros((n_tiles,), jnp.int32), tile_mode, jnp.full((1,), n_tiles, jnp.int32),
                         wg[None], wu[None], wd[None], tm, tf, F32, True, "ffn_dense", proj=proj)


def _t5_bucket_np(dist):
    max_exact = N_BUCKETS // 2
    n = np.maximum(dist, 0)
    safe = np.maximum(n, 1).astype(np.float32)
    large = max_exact + (np.log(safe / max_exact) / np.log(MAX_DIST / max_exact)
                         * (N_BUCKETS - max_exact)).astype(np.int32)
    large = np.minimum(large, N_BUCKETS - 1)
    return np.where(n < max_exact, n, large).astype(np.int32)


def _bias_kernel(bucket_ref, valid_ref, rb_ref, o_ref):
    bucket = bucket_ref[...]
    for h in range(SW_HEADS):
        acc = jnp.zeros(bucket.shape, F32)
        for b in range(N_BUCKETS):
            acc = jnp.where(bucket == b, rb_ref[b, h], acc)
        for v in range(valid_ref.shape[0]):
            o_ref[v, h] = jnp.where(valid_ref[v] > 0, acc * LOG2E, NEG_INF)


def bias_table(rel_bias):
    qi = np.arange(WINDOW)[:, None] + WINDOW
    kj = np.arange(2 * WINDOW)[None, :]
    dist = qi - kj
    band = (dist >= 0) & (dist < WINDOW)
    valid = np.stack([band, band & (kj >= WINDOW)]).astype(np.int32)
    return pl.pallas_call(
        _bias_kernel,
        out_shape=jax.ShapeDtypeStruct((2, SW_HEADS, WINDOW, 2 * WINDOW), F32),
        in_specs=[pl.BlockSpec(memory_space=pltpu.VMEM), pl.BlockSpec(memory_space=pltpu.VMEM),
                  pl.BlockSpec(memory_space=pltpu.SMEM)],
        out_specs=pl.BlockSpec(memory_space=pltpu.VMEM),
        name="t5_bias_table",
    )(jnp.asarray(_t5_bucket_np(dist)), jnp.asarray(valid), rel_bias.astype(F32))


def _swa_kernel(q_ref, kvp_ref, kvc_ref, bias_ref, qn_ref, kn_ref, sink_ref, o_ref):
    blk, hd = WINDOW, SW_HD
    kv_w = SW_KV_HEADS * hd
    first = jnp.where(pl.program_id(1) == 0, 1, 0)
    gw = 2 * LANES
    gi = lax.broadcasted_iota(jnp.int32, (gw, gw), 0)
    gj = lax.broadcasted_iota(jnp.int32, (gw, gw), 1)
    group_ones = jnp.where((gi // hd) == (gj // hd), 1.0, 0.0).astype(BF16)
    lane = lax.broadcasted_iota(jnp.int32, (1, LANES), 1)
    low_half = lane < hd

    def head_norm(x, gain):
        cols = []
        for c0 in range(0, x.shape[1], gw):
            xc = x[:, c0:c0 + gw]
            ss = _dot((xc * xc).astype(BF16), group_ones)
            cols.append(xc * lax.rsqrt(ss * (1.0 / hd) + EPS))
        return jnp.concatenate(cols, axis=1) * gain

    def dup_half(x, half):
        swapped = pltpu.roll(x, hd, 1)
        return jnp.where(low_half == (half == 0), x, swapped)

    n_qb = q_ref.shape[0] // blk
    qn = head_norm(q_ref[...].astype(F32), qn_ref[...]) * ((hd ** -0.5) * LOG2E)
    half_sel = [jnp.where(low_half, 1.0, 0.0), jnp.where(low_half, 0.0, 1.0)]
    k_all = jnp.concatenate([kvp_ref[:, 0:kv_w], kvc_ref[:, 0:kv_w]], axis=0).astype(F32)
    kn = head_norm(k_all, kn_ref[...])
    v_all = jnp.concatenate([kvp_ref[:, kv_w:2 * kv_w], kvc_ref[:, kv_w:2 * kv_w]], axis=0).astype(F32)
    ks, vs = [], []
    for g in range(SW_KV_HEADS):
        c0 = (g // 2) * LANES
        ks.append(dup_half(kn[:, c0:c0 + LANES], g % 2).astype(BF16))
        vs.append(dup_half(v_all[:, c0:c0 + LANES], g % 2).astype(BF16))

    pairs = [(j, hq) for j in range(n_qb) for hq in range(SW_HEADS)]
    scores = {}
    for (j, hq) in pairs:
        c0 = (hq // 2) * LANES
        q_h = (qn[j * blk:(j + 1) * blk, c0:c0 + LANES] * half_sel[hq % 2]).astype(BF16)
        scores[(j, hq)] = _dot_nt(q_h, ks[hq // SW_GROUP][j * blk:(j + 2) * blk])
    probs = {}
    for (j, hq) in pairs:
        variant = first if j == 0 else 0
        s = scores[(j, hq)] + bias_ref[variant, hq]
        sink = sink_ref[hq] * LOG2E
        mx = jnp.maximum(jnp.max(s, axis=-1, keepdims=True), sink)
        p = jnp.exp2(s - mx)
        denom = jnp.sum(p, axis=-1, keepdims=True) + jnp.exp2(sink - mx)
        probs[(j, hq)] = (p * (1.0 / denom)).astype(BF16)
    outs = {key: _dot(probs[key], vs[key[1] // SW_GROUP][key[0] * blk:(key[0] + 2) * blk]) for key in pairs}
    for j in range(n_qb):
        for c in range(SW_HEADS // 2):
            o_ref[j * blk:(j + 1) * blk, c * LANES:(c + 1) * LANES] = jnp.where(
                low_half, outs[(j, 2 * c)], outs[(j, 2 * c + 1)]).astype(o_ref.dtype)


def swa_attention(q, kv, bias, q_norm, k_norm, sinks, batch, seq):
    t = q.shape[0]
    blk = WINDOW
    step = SWA_QBLOCKS * blk
    assert seq % step == 0
    nb = seq // step
    qw = SW_HEADS * SW_HD
    kvw = 2 * SW_KV_HEADS * SW_HD
    return pl.pallas_call(
        _swa_kernel,
        out_shape=jax.ShapeDtypeStruct((t, qw), BF16),
        grid=(batch, nb),
        in_specs=[pl.BlockSpec((step, qw), lambda b, n: (b * nb + n, 0)),
                  pl.BlockSpec((blk, kvw), lambda b, n: (jnp.maximum((b * nb + n) * SWA_QBLOCKS - 1, b * nb * SWA_QBLOCKS), 0)),
                  pl.BlockSpec((step, kvw), lambda b, n: (b * nb + n, 0)),
                  pl.BlockSpec((2, SW_HEADS, blk, 2 * blk), lambda b, n: (0, 0, 0, 0)),
                  pl.BlockSpec((1, qw), lambda b, n: (0, 0)),
                  pl.BlockSpec((1, kvw // 2), lambda b, n: (0, 0)),
                  pl.BlockSpec(memory_space=pltpu.SMEM)],
        out_specs=pl.BlockSpec((step, qw), lambda b, n: (b * nb + n, 0)),
        compiler_params=_cparams(("parallel", "parallel")),
        name="swa_attention",
    )(q, kv, kv, bias, jnp.tile(q_norm.astype(F32), SW_HEADS).reshape(1, qw),
      jnp.tile(k_norm.astype(F32), SW_KV_HEADS).reshape(1, kvw // 2), sinks.astype(F32))


def _route_kernel(x_ref, a_ref, wp_ref, g_ref, wr_ref, h_ref, r_ref, wt_ref, tab_ref, cnt_ref,
                  sel_s, gw_s, cnt_s, start_s, run_s, *, tile_rows):
    ne = N_EXPERTS
    p = pl.program_id(0)
    i = pl.program_id(1)
    tm = x_ref.shape[0]
    sub = lax.broadcasted_iota(jnp.int32, (ne, tm), 0).astype(F32)

    def seg_rows(sel):
        n = jnp.sum(sel, axis=1, keepdims=True)
        return jnp.floor((n + (SEG_ALIGN - 1)) * (1.0 / SEG_ALIGN)) * SEG_ALIGN

    def excl_cumsum_experts(v):
        sub8 = lax.broadcasted_iota(jnp.int32, v.shape, 0)
        out = jnp.zeros_like(v)
        for e in range(ne - 1):
            out = out + jnp.where(sub8 > e, v[e:e + 1, :], 0.0)
        return out

    @pl.when(p == 0)
    def _():
        @pl.when(i == 0)
        def _():
            cnt_s[...] = jnp.zeros_like(cnt_s)

        x = x_ref[...] + _dot(a_ref[...], wp_ref[...])
        h_ref[...] = x
        ms = jnp.mean(x * x, axis=-1, keepdims=True)
        xn32 = (x * lax.rsqrt(ms + EPS)) * g_ref[...]
        xn_hi = xn32.astype(BF16)
        xn_lo = (xn32 - xn_hi.astype(F32)).astype(BF16)
        p_hi = _dot_nt(wr_ref[...], xn_hi)
        p_lo = _dot_nt(wr_ref[...], xn_lo)
        logits = p_hi[0:ne] + p_hi[ne:2 * ne] + p_lo[0:ne]
        m1 = jnp.max(logits, axis=0, keepdims=True)
        i1 = jnp.min(jnp.where(logits == m1, sub, float(ne)), axis=0, keepdims=True)
        l2 = jnp.where(sub == i1, -jnp.inf, logits)
        m2 = jnp.max(l2, axis=0, keepdims=True)
        i2 = jnp.min(jnp.where(l2 == m2, sub, float(ne)), axis=0, keepdims=True)
        e2 = jnp.exp(m2 - m1)
        w1 = 1.0 / (1.0 + e2)
        w2 = e2 / (1.0 + e2)
        sel = jnp.where((sub == i1) | (sub == i2), 1.0, 0.0)
        sel_s[i] = sel
        gw_s[i] = jnp.where(sub == i1, w1, jnp.where(sub == i2, w2, 0.0))
        cnt_s[...] += seg_rows(sel)

    @pl.when(p == 1)
    def _():
        @pl.when(i == 0)
        def _():
            cnt = cnt_s[...]
            padded = jnp.floor((cnt + (tile_rows - 1)) * (1.0 / tile_rows)) * tile_rows
            start_s[...] = excl_cumsum_experts(padded)
            run_s[...] = jnp.zeros_like(run_s)
            cnt_ref[...] = cnt

        sel = sel_s[i]
        gw = gw_s[i]
        ti = lax.broadcasted_iota(jnp.int32, (tm, tm), 0)
        tj = lax.broadcasted_iota(jnp.int32, (tm, tm), 1)
        tri = jnp.where(ti <= tj, 1.0, 0.0).astype(BF16)
        csum = _dot(sel.astype(BF16), tri)
        seg = jnp.broadcast_to(seg_rows(sel), run_s.shape)
        local0 = excl_cumsum_experts(seg)
        tab_ref[0, 0] = start_s[...] + run_s[...]
        tab_ref[0, 1] = seg
        tab_ref[0, 2] = local0
        run_s[...] += seg
        local_row = local0[:, 0:1] + csum - sel
        ia = jnp.min(jnp.where(sel > 0.0, sub, float(ne)), axis=0, keepdims=True)
        ib = jnp.max(jnp.where(sel > 0.0, sub, -1.0), axis=0, keepdims=True)
        pick_a = sub == ia
        pick_b = sub == ib
        rows = [jnp.sum(jnp.where(pick_a, local_row, 0.0), axis=0, keepdims=True),
                jnp.sum(jnp.where(pick_b, local_row, 0.0), axis=0, keepdims=True),
                jnp.sum(jnp.where(pick_a, gw, 0.0), axis=0, keepdims=True),
                jnp.sum(jnp.where(pick_b, gw, 0.0), axis=0, keepdims=True)]
        r_ref[...] = jnp.concatenate(rows + [jnp.zeros((ne - 4, tm), F32)], axis=0)
        wpad = jnp.concatenate(rows[2:4] + rows[0:2] + [jnp.zeros((LANES - 4, tm), F32)], axis=0)
        wt_ref[...] = wpad.T


def moe_route(x, a, wp, gain, w_router, tm, tile_rows):
    t, d = x.shape
    ne = w_router.shape[1]
    assert ne == N_EXPERTS
    w_hi = w_router.astype(BF16)
    w_lo = (w_router - w_hi.astype(F32)).astype(BF16)
    wr = jnp.concatenate([w_hi.T, w_lo.T], axis=0)
    tm = min(tm, t)
    nt = t // tm

    def row_map(p, i):
        return (i * (1 - p) + (nt - 1) * p, 0)

    return pl.pallas_call(
        functools.partial(_route_kernel, tile_rows=tile_rows),
        out_shape=(jax.ShapeDtypeStruct((t, d), F32),
                   jax.ShapeDtypeStruct((ne, t), F32), jax.ShapeDtypeStruct((t, LANES), F32),
                   jax.ShapeDtypeStruct((nt, 3, ne, LANES), F32), jax.ShapeDtypeStruct((ne, LANES), F32)),
        grid=(2, nt),
        in_specs=[pl.BlockSpec((tm, d), row_map),
                  pl.BlockSpec((tm, a.shape[1]), row_map),
                  pl.BlockSpec(wp.shape, lambda p, i: (0, 0)),
                  pl.BlockSpec((1, d), lambda p, i: (0, 0)),
                  pl.BlockSpec((2 * ne, d), lambda p, i: (0, 0))],
        out_specs=(pl.BlockSpec((tm, d), row_map),
                   pl.BlockSpec((ne, tm), lambda p, i: (0, i * p)),
                   pl.BlockSpec((tm, LANES), lambda p, i: (i * p, 0)),
                   pl.BlockSpec((1, 3, ne, LANES), lambda p, i: (i * p, 0, 0, 0)),
                   pl.BlockSpec((ne, LANES), lambda p, i: (0, 0))),
        scratch_shapes=[pltpu.VMEM((nt, ne, tm), F32), pltpu.VMEM((nt, ne, tm), F32),
                        pltpu.VMEM((ne, LANES), F32), pltpu.VMEM((ne, LANES), F32), pltpu.VMEM((ne, LANES), F32)],
        compiler_params=_cparams(("arbitrary", "arbitrary")),
        name="moe_route",
    )(x, a, wp, gain.reshape(1, d), wr)


def _segment_copies(tab_ref, i, e, local_ref, slot_ref, sem, to_slots):
    base = (i * N_EXPERTS + e) * 3
    slot0, rows, local0 = tab_ref[base], tab_ref[base + 1], tab_ref[base + 2]
    out = []
    done = 0
    size = MOE_TOKEN_TILE
    while size >= SEG_ALIGN:
        take = rows & size
        loc = local_ref.at[pl.ds(pl.multiple_of(local0 + done, SEG_ALIGN), size)]
        slt = slot_ref.at[pl.ds(pl.multiple_of(slot0 + done, SEG_ALIGN), size)]
        desc = pltpu.make_async_copy(loc, slt, sem) if to_slots else pltpu.make_async_copy(slt, loc, sem)
        out.append((take != 0, desc))
        done = done + take
        size //= 2
    return out


def _run_segment_copies(tab_ref, tile, slot, rows_s, slot_ref, sems, to_slots, action):
    for e in range(N_EXPERTS):
        for cond, desc in _segment_copies(tab_ref, tile, e, rows_s.at[slot], slot_ref, sems.at[slot], to_slots):
            @pl.when(cond)
            def _():
                getattr(desc, action)()


def _dispatch_kernel(tab_ref, zf_ref, x_ref, g_ref, r_ref, xs_ref, rows_s, zero_s, sem, zsem, *, tile_rows):
    i = pl.program_id(0)
    tm = x_ref.shape[0]
    n_local = rows_s.shape[1]

    @pl.when(i == 0)
    def _():
        zero_s[...] = jnp.zeros_like(zero_s)

        def zero_copy(e):
            row0 = pl.multiple_of(zf_ref[e], tile_rows)
            return pltpu.make_async_copy(zero_s, xs_ref.at[pl.ds(row0, tile_rows)], zsem)

        for e in range(zf_ref.shape[0]):
            @pl.when(zf_ref[e] >= 0)
            def _():
                zero_copy(e).start()
        for e in range(zf_ref.shape[0]):
            @pl.when(zf_ref[e] >= 0)
            def _():
                zero_copy(e).wait()

    x = x_ref[...]
    ms = jnp.mean(x * x, axis=-1, keepdims=True)
    xn = ((x * lax.rsqrt(ms + EPS)) * g_ref[...]).astype(BF16)
    row_id = lax.broadcasted_iota(jnp.int32, (n_local, tm), 0).astype(F32)
    onehot = jnp.where((row_id == r_ref[0:1, :]) | (row_id == r_ref[1:2, :]), 1.0, 0.0).astype(BF16)
    slot = lax.rem(i, 2)
    rows_s[slot] = _dot(onehot, xn)

    _run_segment_copies(tab_ref, i, slot, rows_s, xs_ref, sem, True, "start")

    @pl.when(i > 0)
    def _():
        _run_segment_copies(tab_ref, i - 1, 1 - slot, rows_s, xs_ref, sem, True, "wait")

    @pl.when(i == pl.num_programs(0) - 1)
    def _():
        _run_segment_copies(tab_ref, i, slot, rows_s, xs_ref, sem, True, "wait")


def moe_dispatch(x, gain, r, tab, zf_rows, n_slots, tm, tile_rows):
    t, d = x.shape
    nt = t // tm
    n_local = TOP_K * tm + N_EXPERTS * SEG_ALIGN
    grid_spec = pltpu.PrefetchScalarGridSpec(
        num_scalar_prefetch=2,
        grid=(nt,),
        in_specs=[pl.BlockSpec((tm, d), lambda i, tb, zf: (i, 0)),
                  pl.BlockSpec((1, d), lambda i, tb, zf: (0, 0)),
                  pl.BlockSpec((N_EXPERTS, tm), lambda i, tb, zf: (0, i))],
        out_specs=pl.BlockSpec(memory_space=pl.ANY),
        scratch_shapes=[pltpu.VMEM((2, n_local, d), F32), pltpu.VMEM((tile_rows, d), F32),
                        pltpu.SemaphoreType.DMA((2,)), pltpu.SemaphoreType.DMA],
    )
    return pl.pallas_call(
        functools.partial(_dispatch_kernel, tile_rows=tile_rows),
        out_shape=jax.ShapeDtypeStruct((n_slots, d), F32),
        grid_spec=grid_spec,
        compiler_params=_cparams(("arbitrary",)),
        name="moe_dispatch",
    )(tab, zf_rows, x, gain.reshape(1, d), r)


def _combine_kernel(tab_ref, h_ref, wt_ref, ys_ref, o_ref, rows_s, sems):
    i = pl.program_id(0)
    tm = h_ref.shape[0]
    n_local = rows_s.shape[1]
    slot = lax.rem(i, 2)

    def fetch(tile, into):
        rows_s[into] = jnp.zeros(rows_s.shape[1:], rows_s.dtype)
        _run_segment_copies(tab_ref, tile, into, rows_s, ys_ref, sems, False, "start")

    @pl.when(i == 0)
    def _():
        fetch(i, slot)

    @pl.when(i + 1 < pl.num_programs(0))
    def _():
        fetch(i + 1, 1 - slot)

    _run_segment_copies(tab_ref, i, slot, rows_s, ys_ref, sems, False, "wait")

    wt = wt_ref[...]
    y = rows_s[slot].astype(BF16)
    col_id = lax.broadcasted_iota(jnp.int32, (tm, n_local), 1).astype(F32)
    pick_a = jnp.where(col_id == wt[:, 2:3], 1.0, 0.0).astype(BF16)
    pick_b = jnp.where(col_id == wt[:, 3:4], 1.0, 0.0).astype(BF16)
    o_ref[...] = h_ref[...] + wt[:, 0:1] * _dot(pick_a, y) + wt[:, 1:2] * _dot(pick_b, y)


def moe_combine(h, wt, tab, ys, tm):
    t, d = h.shape
    nt = t // tm
    n_local = TOP_K * tm + N_EXPERTS * SEG_ALIGN
    grid_spec = pltpu.PrefetchScalarGridSpec(
        num_scalar_prefetch=1,
        grid=(nt,),
        in_specs=[pl.BlockSpec((tm, d), lambda i, tb: (i, 0)),
                  pl.BlockSpec((tm, LANES), lambda i, tb: (i, 0)),
                  pl.BlockSpec(memory_space=pl.ANY)],
        out_specs=pl.BlockSpec((tm, d), lambda i, tb: (i, 0)),
        scratch_shapes=[pltpu.VMEM((2, n_local, d), F32), pltpu.SemaphoreType.DMA((2,))],
    )
    return pl.pallas_call(
        _combine_kernel,
        out_shape=jax.ShapeDtypeStruct((t, d), F32),
        grid_spec=grid_spec,
        compiler_params=_cparams(("arbitrary",)),
        name="moe_combine",
    )(tab, h, wt, ys)


def moe_layer(x, a, wp, gain, w_router, wg, wu, wd):
    t, d = x.shape
    ne = w_router.shape[1]
    tr, tm = MOE_TILE_ROWS, MOE_TOKEN_TILE
    assert t % tm == 0 and ne == N_EXPERTS
    nt = t // tm
    n_tiles = -(-(TOP_K * t + nt * ne * (SEG_ALIGN - 1) + ne * (tr - 1)) // tr)
    n_slots = n_tiles * tr

    h, r, wt, tab, cnt = moe_route(x, a, wp, gain, w_router, tm, tr)
    tab = jnp.transpose(tab[:, :, :, 0], (0, 2, 1)).astype(jnp.int32).reshape(-1)

    counts = cnt[:, 0].astype(jnp.int32)
    padded = ((counts + (tr - 1)) // tr) * tr
    ends = jnp.cumsum(padded)
    n_valid = (ends[-1] // tr).astype(jnp.int32)
    tile_row0 = jnp.arange(n_tiles, dtype=jnp.int32) * tr
    tile_expert = jnp.sum((tile_row0[:, None] >= ends[None, :]).astype(jnp.int32), axis=1)
    tile_expert = jnp.minimum(tile_expert, ne - 1)
    tile_expert = jnp.where(jnp.arange(n_tiles) < n_valid, tile_expert, tile_expert[jnp.maximum(n_valid - 1, 0)])
    prev_expert = jnp.concatenate([jnp.full((1,), -1, jnp.int32), tile_expert[:-1]])
    rows_used = (ends - padded + counts)[tile_expert] - tile_row0
    tile_mode = jnp.where(tile_expert != prev_expert, TILE_FIRST,
                          jnp.where(rows_used <= tr // 2, TILE_HALF, TILE_FULL)).astype(jnp.int32)
    tail = jnp.arange(TOP_K * t // tr, n_tiles, dtype=jnp.int32)
    zf_rows = jnp.concatenate([jnp.where(padded > 0, ends - tr, -1),
                               jnp.where(tail >= n_valid, tail * tr, -1)]).astype(jnp.int32)

    xs = moe_dispatch(h, gain, r, tab, zf_rows, n_slots, tm, tr)
    ys = expert_swiglu(xs, gain, tile_expert, tile_mode, n_valid.reshape(1), wg, wu, wd, tr, FFN_CHUNK, F32, False,
                       "moe_experts", routed=True)
    return moe_combine(h, wt, tab, ys, tm)


def kernel(x, a_norm, a_w_in, a_conv, a_log_decay, a_dt_bias, a_out_norm, a_w_out, kv_norm, kv_w, k_norm,
           b_norm, b_w_q, q_norm, b_sinks, b_w_o, rel_bias, ffn_norm, dense_w_gate, dense_w_up, dense_w_down,
           moe_router, moe_w_gate, moe_w_up, moe_w_down):
    batch, seq, d = x.shape
    t = batch * seq
    nh, hd = LA_HEADS, LA_D
    main_w = 4 * nh * hd
    h0 = x.reshape(t, d)

    w_in = a_w_in[0]
    w_main = w_in[:, 0:main_w].astype(BF16)
    w_gate = jnp.zeros((d, LANES), BF16).at[:, 0:2 * nh].set(w_in[:, main_w:main_w + 2 * nh].astype(BF16))
    proj, gates = norm_matmul(h0, [(a_norm[0], w_main, BF16), (a_norm[0], w_gate, F32)], IN_PROJ_TILE,
                              "gdn_in_proj")
    o = gdn_core(proj, gates, a_conv[0], a_log_decay[0], a_dt_bias[0], a_out_norm[0], batch, seq)

    h2 = ffn_dense(h0, ffn_norm[0], dense_w_gate[0], dense_w_up[0], dense_w_down[0], MOE_TILE_ROWS, FFN_CHUNK,
                   proj=(o, a_w_out[0].astype(BF16)))

    kv, q = norm_matmul(h2, [(kv_norm, kv_w.astype(BF16), BF16), (b_norm[0], b_w_q[0].astype(BF16), BF16)],
                        QKV_PROJ_TILE, "qkv_proj")
    bias = bias_table(rel_bias)
    attn = swa_attention(q, kv, bias, q_norm[0], k_norm, b_sinks[0], batch, seq)

    h4 = moe_layer(h2, attn, b_w_o[0].astype(BF16), ffn_norm[1], moe_router[0], moe_w_gate[0], moe_w_up[0],
                   moe_w_down[0])
    return h4.reshape(batch, seq, d)
```

```python
import functools

import numpy as np
import jax
import jax.numpy as jnp
from jax import lax
from jax.experimental import pallas as pl
from jax.experimental.pallas import tpu as pltpu

F32 = jnp.float32
BF16 = jnp.bfloat16

EPS = 1e-6
NEG_INF = -1e30

LA_HEADS = 8
LA_D = 128
CONV_W = 4
CHUNK = 64
SW_HEADS = 16
SW_KV_HEADS = 4
SW_GROUP = SW_HEADS // SW_KV_HEADS
SW_HD = 64
WINDOW = 128
SWA_QBLOCKS = 2
N_BUCKETS = 32
MAX_DIST = 128
N_EXPERTS = 8
TOP_K = 2
LOG2E = float(np.log2(np.e))

LANES = 128
SEG_ALIGN = 8
GDN_BLOCK = 2 * CHUNK
HALO = 16
MOE_TILE_ROWS = 512
MOE_TOKEN_TILE = 512
FFN_CHUNK = 512
IN_PROJ_TILE = 1024
QKV_PROJ_TILE = 1024

VMEM_LIMIT = 56 * 1024 * 1024
EXPERT_VMEM_LIMIT = 60 * 1024 * 1024


def _cparams(sem):
    return pltpu.CompilerParams(dimension_semantics=sem, vmem_limit_bytes=VMEM_LIMIT)


def _silu(x, base2=True):
    e = jnp.exp2(x * (-LOG2E)) if base2 else jnp.exp(-x)
    return x * (1.0 / (1.0 + e))


def _dot(a, b):
    return jnp.dot(a, b, preferred_element_type=F32)


def _dot_nt(a, b):
    return lax.dot_general(a, b, (((1,), (1,)), ((), ())), preferred_element_type=F32)


def _norm_matmul_kernel(*refs, n_groups):
    x_ref = refs[0]
    g_refs = refs[1:1 + n_groups]
    w_refs = refs[1 + n_groups:1 + 2 * n_groups]
    o_refs = refs[1 + 2 * n_groups:1 + 3 * n_groups]
    x = x_ref[...]
    xr = x * lax.rsqrt(jnp.mean(x * x, axis=-1, keepdims=True) + EPS)
    for g_ref, w_ref, o_ref in zip(g_refs, w_refs, o_refs):
        o_ref[...] = _dot((xr * g_ref[...]).astype(BF16), w_ref[...]).astype(o_ref.dtype)


def norm_matmul(x, groups, tm, name):
    t, d = x.shape
    tm = min(tm, t)
    assert t % tm == 0
    gains = [g.reshape(1, d).astype(F32) for g, _, _ in groups]
    ws = [w for _, w, _ in groups]
    return pl.pallas_call(
        functools.partial(_norm_matmul_kernel, n_groups=len(groups)),
        out_shape=[jax.ShapeDtypeStruct((t, w.shape[1]), dt) for _, w, dt in groups],
        grid=(t // tm,),
        in_specs=([pl.BlockSpec((tm, d), lambda i: (i, 0))]
                  + [pl.BlockSpec((1, d), lambda i: (0, 0)) for _ in groups]
                  + [pl.BlockSpec(w.shape, lambda i: (0, 0)) for w in ws]),
        out_specs=[pl.BlockSpec((tm, w.shape[1]), lambda i: (i, 0)) for w in ws],
        compiler_params=_cparams(("parallel",)),
        name=name,
    )(x, *gains, *ws)


def _gdn_kernel(proj_ref, gates_ref, convw_ref, hp_ref, onorm_ref, o_ref,
                xs_ref, state_ref, q_s, k_s, v_s, z_s, gc_s, gct_s, beta_s):
    n = pl.program_id(1)

    @pl.when(n == 0)
    def _():
        xs_ref[0:HALO, :] = jnp.zeros((HALO, xs_ref.shape[1]), xs_ref.dtype)
        for ref in (q_s, k_s, v_s, z_s, gc_s, gct_s, beta_s):
            ref[1] = jnp.zeros(ref.shape[1:], ref.dtype)

    @pl.when(n <= 1)
    def _():
        state_ref[...] = jnp.zeros_like(state_ref)

    args = (proj_ref, gates_ref, convw_ref, hp_ref, onorm_ref, o_ref, xs_ref, state_ref,
            q_s, k_s, v_s, z_s, gc_s, gct_s, beta_s)

    @pl.when(lax.rem(n, 2) == 0)
    def _():
        _gdn_step(*args, slot_w=0, slot_r=1)

    @pl.when(lax.rem(n, 2) == 1)
    def _():
        _gdn_step(*args, slot_w=1, slot_r=0)


def _gdn_step(proj_ref, gates_ref, convw_ref, hp_ref, onorm_ref, o_ref, xs_ref, state_ref,
              q_s, k_s, v_s, z_s, gc_s, gct_s, beta_s, *, slot_w, slot_r):
    nh, d, c = LA_HEADS, LA_D, CHUNK
    blk = GDN_BLOCK
    qkv_w = 3 * nh * d

    gc = gc_s[slot_r]
    gc_t = gct_s[slot_r]
    beta = beta_s[slot_r]

    xs_ref[HALO:HALO + blk, :] = proj_ref[:, 0:qkv_w]

    def front_gates():
        _gdn_front_gates(gates_ref, hp_ref, gc_s, gct_s, beta_s, slot_w)

    di = lax.broadcasted_iota(jnp.int32, (d, d), 0)
    dj = lax.broadcasted_iota(jnp.int32, (d, d), 1)
    eye_d = jnp.where(di == dj, 1.0, 0.0).astype(BF16)

    onorm = onorm_ref[...]

    n_shift = CONV_W - 1
    sr = lax.broadcasted_iota(jnp.int32, (n_shift * blk, HALO + blk), 0)
    sc = lax.broadcasted_iota(jnp.int32, (n_shift * blk, HALO + blk), 1)
    shift_mat = jnp.where(sc == HALO + (sr % blk) - (sr // blk + 1), 1.0, 0.0).astype(BF16)
    pair_w = 2 * d

    def conv_silu(col0):
        cols = slice(col0, col0 + pair_w)
        shifted = _dot(shift_mat, xs_ref[:, cols])
        acc = convw_ref[CONV_W - 1:CONV_W, cols] * xs_ref[HALO:HALO + blk, cols].astype(F32)
        for s in range(1, CONV_W):
            acc = acc + convw_ref[CONV_W - 1 - s:CONV_W - s, cols] * shifted[(s - 1) * blk:s * blk]
        return _silu(acc)

    def front_pair(hp):
        c0 = hp * pair_w
        qf = conv_silu(c0)
        kf = conv_silu(nh * d + c0)
        v_s[slot_w, :, c0:c0 + pair_w] = conv_silu(2 * nh * d + c0)
        for half in range(2):
            lo, hi = half * d, (half + 1) * d
            qh, kh = qf[:, lo:hi], kf[:, lo:hi]
            q_s[slot_w, :, c0 + lo:c0 + hi] = qh * (lax.rsqrt(jnp.sum(qh * qh, axis=-1, keepdims=True) + EPS)
                                                    * (d ** -0.5))
            k_s[slot_w, :, c0 + lo:c0 + hi] = kh * lax.rsqrt(jnp.sum(kh * kh, axis=-1, keepdims=True) + EPS)
        z_s[slot_w, :, c0:c0 + pair_w] = proj_ref[:, qkv_w + c0:qkv_w + c0 + pair_w]

    front_tasks = [front_gates] + [functools.partial(front_pair, hp) for hp in range(nh // 2)]

    def run_front_task():
        if front_tasks:
            front_tasks.pop(0)()

    assert blk == 2 * c and 2 * c == LANES and d == LANES
    si = lax.broadcasted_iota(jnp.int32, (c, 2 * c), 0)
    sl = lax.broadcasted_iota(jnp.int32, (c, 2 * c), 1)
    first_chunk = sl < c
    sj = jnp.where(first_chunk, sl, sl - c)
    lower_incl = si >= sj
    strict = si > sj
    eye_pair = jnp.where(si == sj, 1.0, 0.0).astype(F32)
    lane_row = lax.broadcasted_iota(jnp.int32, (1, 2 * c), 1) < c
    zeros_cd = jnp.zeros((c, d), BF16)

    def block_diag(m):
        return jnp.concatenate([jnp.where(first_chunk, m, 0.0), jnp.where(first_chunk, 0.0, m)], axis=0).astype(BF16)

    st = []
    for h in range(nh):
        hs = slice(h * d, (h + 1) * d)
        q = q_s[slot_r, :, hs]
        k = k_s[slot_r, :, hs]
        v = v_s[slot_r, :, hs]
        g_col = gc[:, nh + h:nh + h + 1]
        g_row = gc_t[nh + h:nh + h + 1, :]
        b_col = beta[:, h:h + 1]
        g_col_pair = jnp.where(first_chunk, g_col[0:c], g_col[c:2 * c])
        g_last = jnp.where(lane_row, g_col[c - 1:c], g_col[2 * c - 1:2 * c])
        decay = jnp.where(lower_incl, jnp.exp2(jnp.where(lower_incl, g_col_pair - g_row, 0.0)), 0.0)
        k_beta = k * b_col
        e_col = jnp.exp2(g_col)
        kb, qb, kbf = k_beta.astype(BF16), q.astype(BF16), k.astype(BF16)
        lhs = jnp.concatenate([jnp.concatenate([kb[0:c], kb[c:2 * c]], axis=1),
                               jnp.concatenate([qb[0:c], qb[c:2 * c]], axis=1),
                               jnp.concatenate([eye_d, eye_d], axis=1)], axis=0)
        k_diag = jnp.concatenate([jnp.concatenate([kbf[0:c], zeros_cd], axis=1),
                                  jnp.concatenate([zeros_cd, kbf[c:2 * c]], axis=1)], axis=0)
        kk = _dot_nt(lhs, k_diag)
        vb, kbe = (v * b_col).astype(BF16), (k_beta * e_col).astype(BF16)
        zeros_2 = jnp.zeros((c, 2 * d), BF16)
        st.append(dict(
            a=jnp.where(strict, kk[0:c] * decay, 0.0),
            attn=kk[c:2 * c] * decay,
            k_tail_t=kk[2 * c:2 * c + d] * jnp.exp2(g_last - g_row),
            rhs=jnp.concatenate([jnp.concatenate([vb[0:c], kbe[0:c], zeros_2], axis=1),
                                 jnp.concatenate([zeros_2, vb[c:2 * c], kbe[c:2 * c]], axis=1)], axis=0),
            qe=(q * e_col).astype(BF16),
            e_last=[jnp.exp2(g_col[c - 1:c]), jnp.exp2(g_col[2 * c - 1:2 * c])]))
    run_front_task()

    for cur in st:
        x = -cur["a"]
        cur["y"] = _dot(x.astype(BF16), block_diag(x))
        cur["p"] = eye_pair + x
    run_front_task()
    n_levels = int(np.log2(c))
    for lvl in range(1, n_levels):
        for cur in st:
            y_bd = block_diag(cur["y"])
            p = cur["p"]
            if lvl + 1 < n_levels:
                zz = _dot(jnp.concatenate([cur["y"].astype(BF16), p.astype(BF16)], axis=0), y_bd)
                cur["y"] = zz[0:c]
                cur["p"] = p + zz[c:2 * c]
            else:
                cur["p"] = p + _dot(p.astype(BF16), y_bd)
        run_front_task()
    for cur in st:
        cur["uw"] = _dot(cur["p"].astype(BF16), cur["rhs"])
    run_front_task()

    for ck in range(2):
        r = ck * c
        in_chunk = first_chunk if ck == 0 else jnp.logical_not(first_chunk)
        in_chunk_d = lane_row if ck == 0 else jnp.logical_not(lane_row)
        s_old = [state_ref[h] for h in range(nh)]
        ws_qs = []
        for h in range(nh):
            cur = st[h]
            w = cur["uw"][:, (2 * ck + 1) * d:(2 * ck + 2) * d]
            lhs = jnp.concatenate([w.astype(BF16), cur["qe"][r:r + c]], axis=0)
            ws_qs.append(_dot(lhs, s_old[h].astype(BF16)))
        run_front_task()
        for h in range(nh):
            cur = st[h]
            v_new = (cur["uw"][:, 2 * ck * d:(2 * ck + 1) * d] - ws_qs[h][0:c]).astype(BF16)
            lhs = jnp.concatenate([jnp.where(in_chunk, cur["attn"], 0.0).astype(BF16),
                                   jnp.where(in_chunk_d, cur["k_tail_t"], 0.0).astype(BF16)], axis=0)
            rhs = jnp.concatenate([v_new, zeros_cd] if ck == 0 else [zeros_cd, v_new], axis=0)
            av_kv = _dot(lhs, rhs)
            state_ref[h] = s_old[h] * cur["e_last"][ck] + av_kv[c:c + d]
            o = ws_qs[h][c:2 * c] + av_kv[0:c]
            o = (o * lax.rsqrt(jnp.mean(o * o, axis=-1, keepdims=True) + EPS)) * onorm
            z = z_s[slot_r, r:r + c, h * d:(h + 1) * d].astype(F32)
            o_ref[r:r + c, h * d:(h + 1) * d] = (o * _silu(z)).astype(o_ref.dtype)
    while front_tasks:
        run_front_task()

    xs_ref[0:HALO, :] = xs_ref[blk:blk + HALO, :]


def _gdn_front_gates(gates_ref, hp_ref, gc_s, gct_s, beta_s, slot_w):
    blk, c = GDN_BLOCK, CHUNK
    gates = gates_ref[...]
    a_log = hp_ref[0:1, :]
    dt_bias = hp_ref[1:2, :]
    beta = 1.0 / (1.0 + jnp.exp(-gates))
    sp_in = gates + dt_bias
    softplus = jnp.maximum(sp_in, 0.0) + jnp.log(1.0 + jnp.exp(-jnp.abs(sp_in)))
    g = (-jnp.exp(a_log) * softplus) * float(np.log2(np.e))

    row = lax.broadcasted_iota(jnp.int32, (blk, blk), 0)
    col = lax.broadcasted_iota(jnp.int32, (blk, blk), 1)
    tri = jnp.where((row >= col) & ((row // c) == (col // c)), 1.0, 0.0).astype(BF16)
    g_hi = g.astype(BF16)
    g_r1 = g - g_hi.astype(F32)
    g_mid = g_r1.astype(BF16)
    g_lo = (g_r1 - g_mid.astype(F32)).astype(BF16)
    gc = _dot(tri, g_hi) + _dot(tri, g_mid) + _dot(tri, g_lo)
    gc_s[slot_w] = gc
    gct_s[slot_w] = gc.T
    beta_s[slot_w] = beta


def gdn_core(proj, gates, conv_w, a_log, dt_bias, out_norm, batch, seq):
    t = proj.shape[0]
    nh, d = LA_HEADS, LA_D
    blk = GDN_BLOCK
    assert seq % blk == 0
    nblk = seq // blk
    hp = jnp.zeros((8, LANES), F32)
    hp = hp.at[0, nh:2 * nh].set(a_log.astype(F32)).at[1, nh:2 * nh].set(dt_bias.astype(F32))

    def in_map(b, n):
        return (b * nblk + jnp.minimum(n, nblk - 1), 0)

    return pl.pallas_call(
        _gdn_kernel,
        out_shape=jax.ShapeDtypeStruct((t, nh * d), BF16),
        grid=(batch, nblk + 1),
        in_specs=[pl.BlockSpec((blk, 4 * nh * d), in_map),
                  pl.BlockSpec((blk, LANES), in_map),
                  pl.BlockSpec((CONV_W, 3 * nh * d), lambda b, n: (0, 0)),
                  pl.BlockSpec((8, LANES), lambda b, n: (0, 0)),
                  pl.BlockSpec((1, d), lambda b, n: (0, 0))],
        out_specs=pl.BlockSpec((blk, nh * d), lambda b, n: (b * nblk + jnp.maximum(n - 1, 0), 0)),
        scratch_shapes=[pltpu.VMEM((HALO + blk, 3 * nh * d), BF16),
                        pltpu.VMEM((nh, d, d), F32),
                        pltpu.VMEM((2, blk, nh * d), F32), pltpu.VMEM((2, blk, nh * d), F32),
                        pltpu.VMEM((2, blk, nh * d), F32), pltpu.VMEM((2, blk, nh * d), BF16),
                        pltpu.VMEM((2, blk, LANES), F32), pltpu.VMEM((2, LANES, blk), F32),
                        pltpu.VMEM((2, blk, LANES), F32)],
        compiler_params=_cparams(("arbitrary", "arbitrary")),
        name="gdn_core",
    )(proj, gates, conv_w.astype(F32), hp, out_norm.reshape(1, d).astype(F32))


TILE_FULL, TILE_FIRST, TILE_HALF = 0, 1, 2


def _swiglu_kernel(te_ref, mode_ref, nv_ref, x_ref, g_ref, a_ref, wp_ref, wg_hbm, wu_hbm, wd_hbm, o_ref,
                   wg_c, wu_c, wd_c, stage_in, stage_out, sems, xres_s, *, pre_norm, pre_proj, routed, tf):
    i = pl.program_id(0)
    nf = wg_c.shape[0]
    valid = i < nv_ref[0]

    def chunk_copies(j, slot, tile=None):
        e = te_ref[i if tile is None else tile]
        cols = pl.ds(j * tf, tf)
        return (pltpu.make_async_copy(wg_hbm.at[e, :, cols], stage_in.at[slot, 0], sems.at[slot, 0]),
                pltpu.make_async_copy(wu_hbm.at[e, :, cols], stage_in.at[slot, 1], sems.at[slot, 1]),
                pltpu.make_async_copy(wd_hbm.at[e, cols, :], stage_out.at[slot], sems.at[slot, 2]))

    if routed:
        nxt = jnp.minimum(i + 1, pl.num_programs(0) - 1)
        prefetch_next = (valid & (mode_ref[i] != TILE_FIRST) & (i + 1 < nv_ref[0]) & (mode_ref[nxt] == TILE_FIRST))
        prv = jnp.maximum(i - 1, 0)
        chunk0_requested = (i > 0) & (mode_ref[prv] != TILE_FIRST)

        @pl.when(prefetch_next)
        def _():
            for c in chunk_copies(0, 0, tile=nxt):
                c.start()
    else:
        chunk0_requested = False

    tile_rows = x_ref.shape[0]

    d_model = o_ref.shape[1]

    def prepare_rows(rows):
        x = x_ref[0:rows, 0:d_model].astype(F32)
        if pre_proj:
            x = x + _dot(a_ref[0:rows, :], wp_ref[...])
            xres_s[0:rows, :] = x
        if pre_norm:
            ms = jnp.mean(x * x, axis=-1, keepdims=True)
            x = (x * lax.rsqrt(ms + EPS)) * g_ref[...]
        return x.astype(BF16)

    def chunk(xb, j):
        hid = _silu(_dot(xb, wg_c[j]), base2=not routed) * _dot(xb, wu_c[j])
        return _dot(hid.astype(BF16), wd_c[j])

    def finish(acc, rows):
        if pre_norm:
            res = xres_s[0:rows, :] if pre_proj else x_ref[0:rows, 0:d_model]
            acc = res + acc
        o_ref[0:rows, :] = acc.astype(o_ref.dtype)
        if rows < tile_rows:
            o_ref[rows:tile_rows, :] = jnp.zeros((tile_rows - rows, o_ref.shape[1]), o_ref.dtype)

    mode = mode_ref[i]

    @pl.when(valid & (mode == TILE_FIRST) & jnp.logical_not(chunk0_requested))
    def _():
        for c in chunk_copies(0, 0):
            c.start()


    @pl.when(valid & (mode == TILE_FIRST))
    def _():
        xb = prepare_rows(tile_rows)
        acc = None
        for j in range(nf):
            slot = j % 2
            if j + 1 < nf:
                for c in chunk_copies(j + 1, 1 - slot):
                    c.start()
            for c in chunk_copies(j, slot):
                c.wait()
            wg_c[j] = stage_in[slot, 0].astype(BF16)
            wu_c[j] = stage_in[slot, 1].astype(BF16)
            wd_c[j] = stage_out[slot].astype(BF16)
            y = chunk(xb, j)
            acc = y if acc is None else acc + y
        finish(acc, tile_rows)

    def steady(rows):
        xb = prepare_rows(rows)
        acc = None
        for j in range(nf):
            y = chunk(xb, j)
            acc = y if acc is None else acc + y
        finish(acc, rows)

    @pl.when(valid & (mode == TILE_FULL))
    def _():
        steady(tile_rows)

    if routed:
        @pl.when(valid & (mode == TILE_HALF))
        def _():
            steady(tile_rows // 2)

    @pl.when(jnp.logical_not(valid))
    def _():
        o_ref[...] = jnp.zeros_like(o_ref)


def expert_swiglu(x, gain, tile_expert, tile_mode, n_valid, wg, wu, wd, tile_rows, tf, out_dtype, pre_norm, name,
                  proj=None, routed=False):
    n_rows = x.shape[0]
    ne, d, f = wg.shape
    assert n_rows % tile_rows == 0 and f % tf == 0
    n_tiles = n_rows // tile_rows
    nf = f // tf
    pre_proj = proj is not None
    if pre_proj:
        a, wp = proj
        a_spec = pl.BlockSpec((tile_rows, a.shape[1]), lambda i, te, fi, nv: (jnp.minimum(i, nv[0] - 1), 0))
    else:
        a, wp = jnp.zeros((8, LANES), BF16), jnp.zeros((LANES, d), BF16)
        a_spec = pl.BlockSpec(a.shape, lambda i, te, fi, nv: (0, 0))
    grid_spec = pltpu.PrefetchScalarGridSpec(
        num_scalar_prefetch=3,
        grid=(n_tiles,),
        in_specs=[pl.BlockSpec((tile_rows, x.shape[1]), lambda i, te, fi, nv: (jnp.minimum(i, nv[0] - 1), 0)),
                  pl.BlockSpec((1, d), lambda i, te, fi, nv: (0, 0)),
                  a_spec,
                  pl.BlockSpec(wp.shape, lambda i, te, fi, nv: (0, 0)),
                  pl.BlockSpec(memory_space=pl.ANY),
                  pl.BlockSpec(memory_space=pl.ANY),
                  pl.BlockSpec(memory_space=pl.ANY)],
        out_specs=pl.BlockSpec((tile_rows, d), lambda i, te, fi, nv: (i, 0)),
        scratch_shapes=[pltpu.VMEM((nf, d, tf), BF16), pltpu.VMEM((nf, d, tf), BF16), pltpu.VMEM((nf, tf, d), BF16),
                        pltpu.VMEM((2, 2, d, tf), F32), pltpu.VMEM((2, tf, d), F32),
                        pltpu.SemaphoreType.DMA((2, 3)),
                        pltpu.VMEM((tile_rows, d) if pre_proj else (8, LANES), F32)],
    )
    return pl.pallas_call(
        functools.partial(_swiglu_kernel, pre_norm=pre_norm, pre_proj=pre_proj, routed=routed, tf=tf),
        out_shape=jax.ShapeDtypeStruct((n_rows, d), out_dtype),
        grid_spec=grid_spec,
        compiler_params=pltpu.CompilerParams(dimension_semantics=("arbitrary",), vmem_limit_bytes=EXPERT_VMEM_LIMIT),
        name=name,
    )(tile_expert, tile_mode, n_valid, x, gain.reshape(1, d).astype(F32), a, wp, wg, wu, wd)


def ffn_dense(x, gain, wg, wu, wd, tm, tf, proj=None):
    t = x.shape[0]
    n_tiles = t // tm
    tile_mode = jnp.full((n_tiles,), TILE_FULL, jnp.int32).at[0].set(TILE_FIRST)
    return expert_swiglu(x, gain, jnp.zeros((n_tiles,), jnp.int32), tile_mode, jnp.full((1,), n_tiles, jnp.int32),
                         wg[None], wu[None], wd[None], tm, tf, F32, True, "ffn_dense", proj=proj)


def _t5_bucket_np(dist):
    max_exact = N_BUCKETS // 2
    n = np.maximum(dist, 0)
    safe = np.maximum(n, 1).astype(np.float32)
    large = max_exact + (np.log(safe / max_exact) / np.log(MAX_DIST / max_exact)
                         * (N_BUCKETS - max_exact)).astype(np.int32)
    large = np.minimum(large, N_BUCKETS - 1)
    return np.where(n < max_exact, n, large).astype(np.int32)


def _bias_kernel(bucket_ref, valid_ref, rb_ref, o_ref):
    bucket = bucket_ref[...]
    for h in range(SW_HEADS):
        acc = jnp.zeros(bucket.shape, F32)
        for b in range(N_BUCKETS):
            acc = jnp.where(bucket == b, rb_ref[b, h], acc)
        for v in range(valid_ref.shape[0]):
            o_ref[v, h] = jnp.where(valid_ref[v] > 0, acc * LOG2E, NEG_INF)


def bias_table(rel_bias):
    qi = np.arange(WINDOW)[:, None] + WINDOW
    kj = np.arange(2 * WINDOW)[None, :]
    dist = qi - kj
    band = (dist >= 0) & (dist < WINDOW)
    valid = np.stack([band, band & (kj >= WINDOW)]).astype(np.int32)
    return pl.pallas_call(
        _bias_kernel,
        out_shape=jax.ShapeDtypeStruct((2, SW_HEADS, WINDOW, 2 * WINDOW), F32),
        in_specs=[pl.BlockSpec(memory_space=pltpu.VMEM), pl.BlockSpec(memory_space=pltpu.VMEM),
                  pl.BlockSpec(memory_space=pltpu.SMEM)],
        out_specs=pl.BlockSpec(memory_space=pltpu.VMEM),
        name="t5_bias_table",
    )(jnp.asarray(_t5_bucket_np(dist)), jnp.asarray(valid), rel_bias.astype(F32))


def _swa_kernel(q_ref, kvp_ref, kvc_ref, bias_ref, qn_ref, kn_ref, sink_ref, o_ref):
    blk, hd = WINDOW, SW_HD
    kv_w = SW_KV_HEADS * hd
    first = jnp.where(pl.program_id(1) == 0, 1, 0)
    gw = 2 * LANES
    gi = lax.broadcasted_iota(jnp.int32, (gw, gw), 0)
    gj = lax.broadcasted_iota(jnp.int32, (gw, gw), 1)
    group_ones = jnp.where((gi // hd) == (gj // hd), 1.0, 0.0).astype(BF16)
    lane = lax.broadcasted_iota(jnp.int32, (1, LANES), 1)
    low_half = lane < hd

    def head_norm(x, gain):
        cols = []
        for c0 in range(0, x.shape[1], gw):
            xc = x[:, c0:c0 + gw]
            ss = _dot((xc * xc).astype(BF16), group_ones)
            cols.append(xc * lax.rsqrt(ss * (1.0 / hd) + EPS))
        return jnp.concatenate(cols, axis=1) * gain

    def dup_half(x, half):
        swapped = pltpu.roll(x, hd, 1)
        return jnp.where(low_half == (half == 0), x, swapped)

    n_qb = q_ref.shape[0] // blk
    qn = head_norm(q_ref[...].astype(F32), qn_ref[...]) * ((hd ** -0.5) * LOG2E)
    half_sel = [jnp.where(low_half, 1.0, 0.0), jnp.where(low_half, 0.0, 1.0)]
    k_all = jnp.concatenate([kvp_ref[:, 0:kv_w], kvc_ref[:, 0:kv_w]], axis=0).astype(F32)
    kn = head_norm(k_all, kn_ref[...])
    v_all = jnp.concatenate([kvp_ref[:, kv_w:2 * kv_w], kvc_ref[:, kv_w:2 * kv_w]], axis=0).astype(F32)
    ks, vs = [], []
    for g in range(SW_KV_HEADS):
        c0 = (g // 2) * LANES
        ks.append(dup_half(kn[:, c0:c0 + LANES], g % 2).astype(BF16))
        vs.append(dup_half(v_all[:, c0:c0 + LANES], g % 2).astype(BF16))

    pairs = [(j, hq) for j in range(n_qb) for hq in range(SW_HEADS)]
    scores = {}
    for (j, hq) in pairs:
        c0 = (hq // 2) * LANES
        q_h = (qn[j * blk:(j + 1) * blk, c0:c0 + LANES] * half_sel[hq % 2]).astype(BF16)
        scores[(j, hq)] = _dot_nt(q_h, ks[hq // SW_GROUP][j * blk:(j + 2) * blk])
    probs = {}
    for (j, hq) in pairs:
        variant = first if j == 0 else 0
        s = scores[(j, hq)] + bias_ref[variant, hq]
        sink = sink_ref[hq] * LOG2E
        mx = jnp.maximum(jnp.max(s, axis=-1, keepdims=True), sink)
        p = jnp.exp2(s - mx)
        denom = jnp.sum(p, axis=-1, keepdims=True) + jnp.exp2(sink - mx)
        probs[(j, hq)] = (p * (1.0 / denom)).astype(BF16)
    outs = {key: _dot(probs[key], vs[key[1] // SW_GROUP][key[0] * blk:(key[0] + 2) * blk]) for key in pairs}
    for j in range(n_qb):
        for c in range(SW_HEADS // 2):
            o_ref[j * blk:(j + 1) * blk, c * LANES:(c + 1) * LANES] = jnp.where(
                low_half, outs[(j, 2 * c)], outs[(j, 2 * c + 1)]).astype(o_ref.dtype)


def swa_attention(q, kv, bias, q_norm, k_norm, sinks, batch, seq):
    t = q.shape[0]
    blk = WINDOW
    step = SWA_QBLOCKS * blk
    assert seq % step == 0
    nb = seq // step
    qw = SW_HEADS * SW_HD
    kvw = 2 * SW_KV_HEADS * SW_HD
    return pl.pallas_call(
        _swa_kernel,
        out_shape=jax.ShapeDtypeStruct((t, qw), BF16),
        grid=(batch, nb),
        in_specs=[pl.BlockSpec((step, qw), lambda b, n: (b * nb + n, 0)),
                  pl.BlockSpec((blk, kvw), lambda b, n: (jnp.maximum((b * nb + n) * SWA_QBLOCKS - 1, b * nb * SWA_QBLOCKS), 0)),
                  pl.BlockSpec((step, kvw), lambda b, n: (b * nb + n, 0)),
                  pl.BlockSpec((2, SW_HEADS, blk, 2 * blk), lambda b, n: (0, 0, 0, 0)),
                  pl.BlockSpec((1, qw), lambda b, n: (0, 0)),
                  pl.BlockSpec((1, kvw // 2), lambda b, n: (0, 0)),
                  pl.BlockSpec(memory_space=pltpu.SMEM)],
        out_specs=pl.BlockSpec((step, qw), lambda b, n: (b * nb + n, 0)),
        compiler_params=_cparams(("parallel", "parallel")),
        name="swa_attention",
    )(q, kv, kv, bias, jnp.tile(q_norm.astype(F32), SW_HEADS).reshape(1, qw),
      jnp.tile(k_norm.astype(F32), SW_KV_HEADS).reshape(1, kvw // 2), sinks.astype(F32))


def _route_kernel(x_ref, a_ref, wp_ref, g_ref, wr_ref, h_ref, r_ref, wt_ref, tab_ref, cnt_ref,
                  sel_s, gw_s, cnt_s, start_s, run_s, *, tile_rows):
    ne = N_EXPERTS
    p = pl.program_id(0)
    i = pl.program_id(1)
    tm = x_ref.shape[0]
    sub = lax.broadcasted_iota(jnp.int32, (ne, tm), 0).astype(F32)

    def seg_rows(sel):
        n = jnp.sum(sel, axis=1, keepdims=True)
        return jnp.floor((n + (SEG_ALIGN - 1)) * (1.0 / SEG_ALIGN)) * SEG_ALIGN

    def excl_cumsum_experts(v):
        sub8 = lax.broadcasted_iota(jnp.int32, v.shape, 0)
        out = jnp.zeros_like(v)
        for e in range(ne - 1):
            out = out + jnp.where(sub8 > e, v[e:e + 1, :], 0.0)
        return out

    @pl.when(p == 0)
    def _():
        @pl.when(i == 0)
        def _():
            cnt_s[...] = jnp.zeros_like(cnt_s)

        x = x_ref[...] + _dot(a_ref[...], wp_ref[...])
        h_ref[...] = x
        ms = jnp.mean(x * x, axis=-1, keepdims=True)
        xn32 = (x * lax.rsqrt(ms + EPS)) * g_ref[...]
        xn_hi = xn32.astype(BF16)
        xn_lo = (xn32 - xn_hi.astype(F32)).astype(BF16)
        p_hi = _dot_nt(wr_ref[...], xn_hi)
        p_lo = _dot_nt(wr_ref[...], xn_lo)
        logits = p_hi[0:ne] + p_hi[ne:2 * ne] + p_lo[0:ne]
        m1 = jnp.max(logits, axis=0, keepdims=True)
        i1 = jnp.min(jnp.where(logits == m1, sub, float(ne)), axis=0, keepdims=True)
        l2 = jnp.where(sub == i1, -jnp.inf, logits)
        m2 = jnp.max(l2, axis=0, keepdims=True)
        i2 = jnp.min(jnp.where(l2 == m2, sub, float(ne)), axis=0, keepdims=True)
        e2 = jnp.exp(m2 - m1)
        w1 = 1.0 / (1.0 + e2)
        w2 = e2 / (1.0 + e2)
        sel = jnp.where((sub == i1) | (sub == i2), 1.0, 0.0)
        sel_s[i] = sel
        gw_s[i] = jnp.where(sub == i1, w1, jnp.where(sub == i2, w2, 0.0))
        cnt_s[...] += seg_rows(sel)

    @pl.when(p == 1)
    def _():
        @pl.when(i == 0)
        def _():
            cnt = cnt_s[...]
            padded = jnp.floor((cnt + (tile_rows - 1)) * (1.0 / tile_rows)) * tile_rows
            start_s[...] = excl_cumsum_experts(padded)
            run_s[...] = jnp.zeros_like(run_s)
            cnt_ref[...] = cnt

        sel = sel_s[i]
        gw = gw_s[i]
        ti = lax.broadcasted_iota(jnp.int32, (tm, tm), 0)
        tj = lax.broadcasted_iota(jnp.int32, (tm, tm), 1)
        tri = jnp.where(ti <= tj, 1.0, 0.0).astype(BF16)
        csum = _dot(sel.astype(BF16), tri)
        seg = jnp.broadcast_to(seg_rows(sel), run_s.shape)
        local0 = excl_cumsum_experts(seg)
        tab_ref[0, 0] = start_s[...] + run_s[...]
        tab_ref[0, 1] = seg
        tab_ref[0, 2] = local0
        run_s[...] += seg
        local_row = local0[:, 0:1] + csum - sel
        ia = jnp.min(jnp.where(sel > 0.0, sub, float(ne)), axis=0, keepdims=True)
        ib = jnp.max(jnp.where(sel > 0.0, sub, -1.0), axis=0, keepdims=True)
        pick_a = sub == ia
        pick_b = sub == ib
        rows = [jnp.sum(jnp.where(pick_a, local_row, 0.0), axis=0, keepdims=True),
                jnp.sum(jnp.where(pick_b, local_row, 0.0), axis=0, keepdims=True),
                jnp.sum(jnp.where(pick_a, gw, 0.0), axis=0, keepdims=True),
                jnp.sum(jnp.where(pick_b, gw, 0.0), axis=0, keepdims=True)]
        r_ref[...] = jnp.concatenate(rows + [jnp.zeros((ne - 4, tm), F32)], axis=0)
        wpad = jnp.concatenate(rows[2:4] + rows[0:2] + [jnp.zeros((LANES - 4, tm), F32)], axis=0)
        wt_ref[...] = wpad.T


def moe_route(x, a, wp, gain, w_router, tm, tile_rows):
    t, d = x.shape
    ne = w_router.shape[1]
    assert ne == N_EXPERTS
    w_hi = w_router.astype(BF16)
    w_lo = (w_router - w_hi.astype(F32)).astype(BF16)
    wr = jnp.concatenate([w_hi.T, w_lo.T], axis=0)
    tm = min(tm, t)
    nt = t // tm

    def row_map(p, i):
        return (i * (1 - p) + (nt - 1) * p, 0)

    return pl.pallas_call(
        functools.partial(_route_kernel, tile_rows=tile_rows),
        out_shape=(jax.ShapeDtypeStruct((t, d), F32),
                   jax.ShapeDtypeStruct((ne, t), F32), jax.ShapeDtypeStruct((t, LANES), F32),
                   jax.ShapeDtypeStruct((nt, 3, ne, LANES), F32), jax.ShapeDtypeStruct((ne, LANES), F32)),
        grid=(2, nt),
        in_specs=[pl.BlockSpec((tm, d), row_map),
                  pl.BlockSpec((tm, a.shape[1]), row_map),
                  pl.BlockSpec(wp.shape, lambda p, i: (0, 0)),
                  pl.BlockSpec((1, d), lambda p, i: (0, 0)),
                  pl.BlockSpec((2 * ne, d), lambda p, i: (0, 0))],
        out_specs=(pl.BlockSpec((tm, d), row_map),
                   pl.BlockSpec((ne, tm), lambda p, i: (0, i * p)),
                   pl.BlockSpec((tm, LANES), lambda p, i: (i * p, 0)),
                   pl.BlockSpec((1, 3, ne, LANES), lambda p, i: (i * p, 0, 0, 0)),
                   pl.BlockSpec((ne, LANES), lambda p, i: (0, 0))),
        scratch_shapes=[pltpu.VMEM((nt, ne, tm), F32), pltpu.VMEM((nt, ne, tm), F32),
                        pltpu.VMEM((ne, LANES), F32), pltpu.VMEM((ne, LANES), F32), pltpu.VMEM((ne, LANES), F32)],
        compiler_params=_cparams(("arbitrary", "arbitrary")),
        name="moe_route",
    )(x, a, wp, gain.reshape(1, d), wr)


def _segment_copies(tab_ref, i, e, local_ref, slot_ref, sem, to_slots):
    base = (i * N_EXPERTS + e) * 3
    slot0, rows, local0 = tab_ref[base], tab_ref[base + 1], tab_ref[base + 2]
    out = []
    done = 0
    size = MOE_TOKEN_TILE
    while size >= SEG_ALIGN:
        take = rows & size
        loc = local_ref.at[pl.ds(pl.multiple_of(local0 + done, SEG_ALIGN), size)]
        slt = slot_ref.at[pl.ds(pl.multiple_of(slot0 + done, SEG_ALIGN), size)]
        desc = pltpu.make_async_copy(loc, slt, sem) if to_slots else pltpu.make_async_copy(slt, loc, sem)
        out.append((take != 0, desc))
        done = done + take
        size //= 2
    return out


def _run_segment_copies(tab_ref, tile, slot, rows_s, slot_ref, sems, to_slots, action):
    for e in range(N_EXPERTS):
        for cond, desc in _segment_copies(tab_ref, tile, e, rows_s.at[slot], slot_ref, sems.at[slot], to_slots):
            @pl.when(cond)
            def _():
                getattr(desc, action)()


def _dispatch_kernel(tab_ref, zf_ref, x_ref, g_ref, r_ref, xs_ref, rows_s, zero_s, sem, zsem, *, tile_rows):
    i = pl.program_id(0)
    tm = x_ref.shape[0]
    n_local = rows_s.shape[1]

    @pl.when(i == 0)
    def _():
        zero_s[...] = jnp.zeros_like(zero_s)

        def zero_copy(e):
            row0 = pl.multiple_of(zf_ref[e], tile_rows)
            return pltpu.make_async_copy(zero_s, xs_ref.at[pl.ds(row0, tile_rows)], zsem)

        for e in range(zf_ref.shape[0]):
            @pl.when(zf_ref[e] >= 0)
            def _():
                zero_copy(e).start()
        for e in range(zf_ref.shape[0]):
            @pl.when(zf_ref[e] >= 0)
            def _():
                zero_copy(e).wait()

    x = x_ref[...]
    ms = jnp.mean(x * x, axis=-1, keepdims=True)
    xn = ((x * lax.rsqrt(ms + EPS)) * g_ref[...]).astype(BF16)
    row_id = lax.broadcasted_iota(jnp.int32, (n_local, tm), 0).astype(F32)
    onehot = jnp.where((row_id == r_ref[0:1, :]) | (row_id == r_ref[1:2, :]), 1.0, 0.0).astype(BF16)
    slot = lax.rem(i, 2)
    rows_s[slot] = _dot(onehot, xn)

    _run_segment_copies(tab_ref, i, slot, rows_s, xs_ref, sem, True, "start")

    @pl.when(i > 0)
    def _():
        _run_segment_copies(tab_ref, i - 1, 1 - slot, rows_s, xs_ref, sem, True, "wait")

    @pl.when(i == pl.num_programs(0) - 1)
    def _():
        _run_segment_copies(tab_ref, i, slot, rows_s, xs_ref, sem, True, "wait")


def moe_dispatch(x, gain, r, tab, zf_rows, n_slots, tm, tile_rows):
    t, d = x.shape
    nt = t // tm
    n_local = TOP_K * tm + N_EXPERTS * SEG_ALIGN
    grid_spec = pltpu.PrefetchScalarGridSpec(
        num_scalar_prefetch=2,
        grid=(nt,),
        in_specs=[pl.BlockSpec((tm, d), lambda i, tb, zf: (i, 0)),
                  pl.BlockSpec((1, d), lambda i, tb, zf: (0, 0)),
                  pl.BlockSpec((N_EXPERTS, tm), lambda i, tb, zf: (0, i))],
        out_specs=pl.BlockSpec(memory_space=pl.ANY),
        scratch_shapes=[pltpu.VMEM((2, n_local, d), F32), pltpu.VMEM((tile_rows, d), F32),
                        pltpu.SemaphoreType.DMA((2,)), pltpu.SemaphoreType.DMA],
    )
    return pl.pallas_call(
        functools.partial(_dispatch_kernel, tile_rows=tile_rows),
        out_shape=jax.ShapeDtypeStruct((n_slots, d), F32),
        grid_spec=grid_spec,
        compiler_params=_cparams(("arbitrary",)),
        name="moe_dispatch",
    )(tab, zf_rows, x, gain.reshape(1, d), r)


def _combine_kernel(tab_ref, h_ref, wt_ref, ys_ref, o_ref, rows_s, sems):
    i = pl.program_id(0)
    tm = h_ref.shape[0]
    n_local = rows_s.shape[1]
    slot = lax.rem(i, 2)

    def fetch(tile, into):
        rows_s[into] = jnp.zeros(rows_s.shape[1:], rows_s.dtype)
        _run_segment_copies(tab_ref, tile, into, rows_s, ys_ref, sems, False, "start")

    @pl.when(i == 0)
    def _():
        fetch(i, slot)

    @pl.when(i + 1 < pl.num_programs(0))
    def _():
        fetch(i + 1, 1 - slot)

    _run_segment_copies(tab_ref, i, slot, rows_s, ys_ref, sems, False, "wait")

    wt = wt_ref[...]
    y = rows_s[slot].astype(BF16)
    col_id = lax.broadcasted_iota(jnp.int32, (tm, n_local), 1).astype(F32)
    pick_a = jnp.where(col_id == wt[:, 2:3], 1.0, 0.0).astype(BF16)
    pick_b = jnp.where(col_id == wt[:, 3:4], 1.0, 0.0).astype(BF16)
    o_ref[...] = h_ref[...] + wt[:, 0:1] * _dot(pick_a, y) + wt[:, 1:2] * _dot(pick_b, y)


def moe_combine(h, wt, tab, ys, tm):
    t, d = h.shape
    nt = t // tm
    n_local = TOP_K * tm + N_EXPERTS * SEG_ALIGN
    grid_spec = pltpu.PrefetchScalarGridSpec(
        num_scalar_prefetch=1,
        grid=(nt,),
        in_specs=[pl.BlockSpec((tm, d), lambda i, tb: (i, 0)),
                  pl.BlockSpec((tm, LANES), lambda i, tb: (i, 0)),
                  pl.BlockSpec(memory_space=pl.ANY)],
        out_specs=pl.BlockSpec((tm, d), lambda i, tb: (i, 0)),
        scratch_shapes=[pltpu.VMEM((2, n_local, d), F32), pltpu.SemaphoreType.DMA((2,))],
    )
    return pl.pallas_call(
        _combine_kernel,
        out_shape=jax.ShapeDtypeStruct((t, d), F32),
        grid_spec=grid_spec,
        compiler_params=_cparams(("arbitrary",)),
        name="moe_combine",
    )(tab, h, wt, ys)


def moe_layer(x, a, wp, gain, w_router, wg, wu, wd):
    t, d = x.shape
    ne = w_router.shape[1]
    tr, tm = MOE_TILE_ROWS, MOE_TOKEN_TILE
    assert t % tm == 0 and ne == N_EXPERTS
    nt = t // tm
    n_tiles = -(-(TOP_K * t + nt * ne * (SEG_ALIGN - 1) + ne * (tr - 1)) // tr)
    n_slots = n_tiles * tr

    h, r, wt, tab, cnt = moe_route(x, a, wp, gain, w_router, tm, tr)
    tab = jnp.transpose(tab[:, :, :, 0], (0, 2, 1)).astype(jnp.int32).reshape(-1)

    counts = cnt[:, 0].astype(jnp.int32)
    padded = ((counts + (tr - 1)) // tr) * tr
    ends = jnp.cumsum(padded)
    n_valid = (ends[-1] // tr).astype(jnp.int32)
    tile_row0 = jnp.arange(n_tiles, dtype=jnp.int32) * tr
    tile_expert = jnp.sum((tile_row0[:, None] >= ends[None, :]).astype(jnp.int32), axis=1)
    tile_expert = jnp.minimum(tile_expert, ne - 1)
    tile_expert = jnp.where(jnp.arange(n_tiles) < n_valid, tile_expert, tile_expert[jnp.maximum(n_valid - 1, 0)])
    prev_expert = jnp.concatenate([jnp.full((1,), -1, jnp.int32), tile_expert[:-1]])
    rows_used = (ends - padded + counts)[tile_expert] - tile_row0
    tile_mode = jnp.where(tile_expert != prev_expert, TILE_FIRST,
                          jnp.where(rows_used <= tr // 2, TILE_HALF, TILE_FULL)).astype(jnp.int32)
    tail = jnp.arange(TOP_K * t // tr, n_tiles, dtype=jnp.int32)
    zf_rows = jnp.concatenate([jnp.where(padded > 0, ends - tr, -1),
                               jnp.where(tail >= n_valid, tail * tr, -1)]).astype(jnp.int32)

    xs = moe_dispatch(h, gain, r, tab, zf_rows, n_slots, tm, tr)
    ys = expert_swiglu(xs, gain, tile_expert, tile_mode, n_valid.reshape(1), wg, wu, wd, tr, FFN_CHUNK, F32, False,
                       "moe_experts", routed=True)
    return moe_combine(h, wt, tab, ys, tm)


def kernel(x, a_norm, a_w_in, a_conv, a_log_decay, a_dt_bias, a_out_norm, a_w_out, kv_norm, kv_w, k_norm,
           b_norm, b_w_q, q_norm, b_sinks, b_w_o, rel_bias, ffn_norm, dense_w_gate, dense_w_up, dense_w_down,
           moe_router, moe_w_gate, moe_w_up, moe_w_down):
    batch, seq, d = x.shape
    t = batch * seq
    nh, hd = LA_HEADS, LA_D
    main_w = 4 * nh * hd
    h0 = x.reshape(t, d)

    w_in = a_w_in[0]
    w_main = w_in[:, 0:main_w].astype(BF16)
    w_gate = jnp.zeros((d, LANES), BF16).at[:, 0:2 * nh].set(w_in[:, main_w:main_w + 2 * nh].astype(BF16))
    proj, gates = norm_matmul(h0, [(a_norm[0], w_main, BF16), (a_norm[0], w_gate, F32)], IN_PROJ_TILE,
                              "gdn_in_proj")
    o = gdn_core(proj, gates, a_conv[0], a_log_decay[0], a_dt_bias[0], a_out_norm[0], batch, seq)

    h2 = ffn_dense(h0, ffn_norm[0], dense_w_gate[0], dense_w_up[0], dense_w_down[0], MOE_TILE_ROWS, FFN_CHUNK,
                   proj=(o, a_w_out[0].astype(BF16)))

    kv, q = norm_matmul(h2, [(kv_norm, kv_w.astype(BF16), BF16), (b_norm[0], b_w_q[0].astype(BF16), BF16)],
                        QKV_PROJ_TILE, "qkv_proj")
    bias = bias_table(rel_bias)
    attn = swa_attention(q, kv, bias, q_norm[0], k_norm, b_sinks[0], batch, seq)

    h4 = moe_layer(h2, attn, b_w_o[0].astype(BF16), ffn_norm[1], moe_router[0], moe_w_gate[0], moe_w_up[0],
                   moe_w_down[0])
    return h4.reshape(batch, seq, d)
```

```python
import functools

import numpy as np
import jax
import jax.numpy as jnp
from jax import lax
from jax.experimental import pallas as pl
from jax.experimental.pallas import tpu as pltpu

F32 = jnp.float32
BF16 = jnp.bfloat16

EPS = 1e-6
NEG_INF = -1e30

LA_HEADS = 8
LA_D = 128
CONV_W = 4
CHUNK = 64
SW_HEADS = 16
SW_KV_HEADS = 4
SW_GROUP = SW_HEADS // SW_KV_HEADS
SW_HD = 64
WINDOW = 128
SWA_QBLOCKS = 2
N_BUCKETS = 32
MAX_DIST = 128
N_EXPERTS = 8
TOP_K = 2
LOG2E = float(np.log2(np.e))

LANES = 128
SEG_ALIGN = 8
GDN_BLOCK = 2 * CHUNK
HALO = 16
MOE_TILE_ROWS = 512
MOE_TOKEN_TILE = 512
FFN_CHUNK = 512
IN_PROJ_TILE = 1024
QKV_PROJ_TILE = 1024

VMEM_LIMIT = 56 * 1024 * 1024
EXPERT_VMEM_LIMIT = 60 * 1024 * 1024


def _cparams(sem):
    return pltpu.CompilerParams(dimension_semantics=sem, vmem_limit_bytes=VMEM_LIMIT)


def _silu(x, base2=True):
    e = jnp.exp2(x * (-LOG2E)) if base2 else jnp.exp(-x)
    return x * (1.0 / (1.0 + e))


def _dot(a, b):
    return jnp.dot(a, b, preferred_element_type=F32)


def _dot_nt(a, b):
    return lax.dot_general(a, b, (((1,), (1,)), ((), ())), preferred_element_type=F32)


def _norm_matmul_kernel(*refs, n_groups):
    x_ref = refs[0]
    g_refs = refs[1:1 + n_groups]
    w_refs = refs[1 + n_groups:1 + 2 * n_groups]
    o_refs = refs[1 + 2 * n_groups:1 + 3 * n_groups]
    x = x_ref[...]
    xr = x * lax.rsqrt(jnp.mean(x * x, axis=-1, keepdims=True) + EPS)
    for g_ref, w_ref, o_ref in zip(g_refs, w_refs, o_refs):
        o_ref[...] = _dot((xr * g_ref[...]).astype(BF16), w_ref[...]).astype(o_ref.dtype)


def norm_matmul(x, groups, tm, name):
    t, d = x.shape
    tm = min(tm, t)
    assert t % tm == 0
    gains = [g.reshape(1, d).astype(F32) for g, _, _ in groups]
    ws = [w for _, w, _ in groups]
    return pl.pallas_call(
        functools.partial(_norm_matmul_kernel, n_groups=len(groups)),
        out_shape=[jax.ShapeDtypeStruct((t, w.shape[1]), dt) for _, w, dt in groups],
        grid=(t // tm,),
        in_specs=([pl.BlockSpec((tm, d), lambda i: (i, 0))]
                  + [pl.BlockSpec((1, d), lambda i: (0, 0)) for _ in groups]
                  + [pl.BlockSpec(w.shape, lambda i: (0, 0)) for w in ws]),
        out_specs=[pl.BlockSpec((tm, w.shape[1]), lambda i: (i, 0)) for w in ws],
        compiler_params=_cparams(("parallel",)),
        name=name,
    )(x, *gains, *ws)


def _gdn_kernel(proj_ref, gates_ref, convw_ref, hp_ref, onorm_ref, o_ref,
                xs_ref, state_ref, q_s, k_s, v_s, z_s, gc_s, gct_s, beta_s):
    n = pl.program_id(1)

    @pl.when(n == 0)
    def _():
        xs_ref[0:HALO, :] = jnp.zeros((HALO, xs_ref.shape[1]), xs_ref.dtype)
        for ref in (q_s, k_s, v_s, z_s, gc_s, gct_s, beta_s):
            ref[1] = jnp.zeros(ref.shape[1:], ref.dtype)

    @pl.when(n <= 1)
    def _():
        state_ref[...] = jnp.zeros_like(state_ref)

    args = (proj_ref, gates_ref, convw_ref, hp_ref, onorm_ref, o_ref, xs_ref, state_ref,
            q_s, k_s, v_s, z_s, gc_s, gct_s, beta_s)

    @pl.when(lax.rem(n, 2) == 0)
    def _():
        _gdn_step(*args, slot_w=0, slot_r=1)

    @pl.when(lax.rem(n, 2) == 1)
    def _():
        _gdn_step(*args, slot_w=1, slot_r=0)


def _gdn_step(proj_ref, gates_ref, convw_ref, hp_ref, onorm_ref, o_ref, xs_ref, state_ref,
              q_s, k_s, v_s, z_s, gc_s, gct_s, beta_s, *, slot_w, slot_r):
    nh, d, c = LA_HEADS, LA_D, CHUNK
    blk = GDN_BLOCK
    qkv_w = 3 * nh * d

    gc = gc_s[slot_r]
    gc_t = gct_s[slot_r]
    beta = beta_s[slot_r]

    xs_ref[HALO:HALO + blk, :] = proj_ref[:, 0:qkv_w]

    def front_gates():
        _gdn_front_gates(gates_ref, hp_ref, gc_s, gct_s, beta_s, slot_w)

    di = lax.broadcasted_iota(jnp.int32, (d, d), 0)
    dj = lax.broadcasted_iota(jnp.int32, (d, d), 1)
    eye_d = jnp.where(di == dj, 1.0, 0.0).astype(BF16)

    onorm = onorm_ref[...]

    n_shift = CONV_W - 1
    sr = lax.broadcasted_iota(jnp.int32, (n_shift * blk, HALO + blk), 0)
    sc = lax.broadcasted_iota(jnp.int32, (n_shift * blk, HALO + blk), 1)
    shift_mat = jnp.where(sc == HALO + (sr % blk) - (sr // blk + 1), 1.0, 0.0).astype(BF16)
    pair_w = 2 * d

    def conv_silu(col0):
        cols = slice(col0, col0 + pair_w)
        shifted = _dot(shift_mat, xs_ref[:, cols])
        acc = convw_ref[CONV_W - 1:CONV_W, cols] * xs_ref[HALO:HALO + blk, cols].astype(F32)
        for s in range(1, CONV_W):
            acc = acc + convw_ref[CONV_W - 1 - s:CONV_W - s, cols] * shifted[(s - 1) * blk:s * blk]
        return _silu(acc)

    def front_pair(hp):
        c0 = hp * pair_w
        qf = conv_silu(c0)
        kf = conv_silu(nh * d + c0)
        v_s[slot_w, :, c0:c0 + pair_w] = conv_silu(2 * nh * d + c0)
        for half in range(2):
            lo, hi = half * d, (half + 1) * d
            qh, kh = qf[:, lo:hi], kf[:, lo:hi]
            q_s[slot_w, :, c0 + lo:c0 + hi] = qh * (lax.rsqrt(jnp.sum(qh * qh, axis=-1, keepdims=True) + EPS)
                                                    * (d ** -0.5))
            k_s[slot_w, :, c0 + lo:c0 + hi] = kh * lax.rsqrt(jnp.sum(kh * kh, axis=-1, keepdims=True) + EPS)
        z_s[slot_w, :, c0:c0 + pair_w] = proj_ref[:, qkv_w + c0:qkv_w + c0 + pair_w]

    front_tasks = [front_gates] + [functools.partial(front_pair, hp) for hp in range(nh // 2)]

    def run_front_task():
        if front_tasks:
            front_tasks.pop(0)()

    assert blk == 2 * c and 2 * c == LANES and d == LANES
    si = lax.broadcasted_iota(jnp.int32, (c, 2 * c), 0)
    sl = lax.broadcasted_iota(jnp.int32, (c, 2 * c), 1)
    first_chunk = sl < c
    sj = jnp.where(first_chunk, sl, sl - c)
    lower_incl = si >= sj
    strict = si > sj
    eye_pair = jnp.where(si == sj, 1.0, 0.0).astype(F32)
    lane_row = lax.broadcasted_iota(jnp.int32, (1, 2 * c), 1) < c
    zeros_cd = jnp.zeros((c, d), BF16)

    def block_diag(m):
        return jnp.concatenate([jnp.where(first_chunk, m, 0.0), jnp.where(first_chunk, 0.0, m)], axis=0).astype(BF16)

    st = []
    for h in range(nh):
        hs = slice(h * d, (h + 1) * d)
        q = q_s[slot_r, :, hs]
        k = k_s[slot_r, :, hs]
        v = v_s[slot_r, :, hs]
        g_col = gc[:, nh + h:nh + h + 1]
        g_row = gc_t[nh + h:nh + h + 1, :]
        b_col = beta[:, h:h + 1]
        g_col_pair = jnp.where(first_chunk, g_col[0:c], g_col[c:2 * c])
        g_last = jnp.where(lane_row, g_col[c - 1:c], g_col[2 * c - 1:2 * c])
        decay = jnp.where(lower_incl, jnp.exp2(jnp.where(lower_incl, g_col_pair - g_row, 0.0)), 0.0)
        k_beta = k * b_col
        e_col = jnp.exp2(g_col)
        kb, qb, kbf = k_beta.astype(BF16), q.astype(BF16), k.astype(BF16)
        lhs = jnp.concatenate([jnp.concatenate([kb[0:c], kb[c:2 * c]], axis=1),
                               jnp.concatenate([qb[0:c], qb[c:2 * c]], axis=1),
                               jnp.concatenate([eye_d, eye_d], axis=1)], axis=0)
        k_diag = jnp.concatenate([jnp.concatenate([kbf[0:c], zeros_cd], axis=1),
                                  jnp.concatenate([zeros_cd, kbf[c:2 * c]], axis=1)], axis=0)
        kk = _dot_nt(lhs, k_diag)
        vb, kbe = (v * b_col).astype(BF16), (k_beta * e_col).astype(BF16)
        zeros_2 = jnp.zeros((c, 2 * d), BF16)
        st.append(dict(
            a=jnp.where(strict, kk[0:c] * decay, 0.0),
            attn=kk[c:2 * c] * decay,
            k_tail_t=kk[2 * c:2 * c + d] * jnp.exp2(g_last - g_row),
            rhs=jnp.concatenate([jnp.concatenate([vb[0:c], kbe[0:c], zeros_2], axis=1),
                                 jnp.concatenate([zeros_2, vb[c:2 * c], kbe[c:2 * c]], axis=1)], axis=0),
            qe=(q * e_col).astype(BF16),
            e_last=[jnp.exp2(g_col[c - 1:c]), jnp.exp2(g_col[2 * c - 1:2 * c])]))
    run_front_task()

    for cur in st:
        x = -cur["a"]
        cur["y"] = _dot(x.astype(BF16), block_diag(x))
        cur["p"] = eye_pair + x
    run_front_task()
    n_levels = int(np.log2(c))
    for lvl in range(1, n_levels):
        for cur in st:
            y_bd = block_diag(cur["y"])
            p = cur["p"]
            if lvl + 1 < n_levels:
                zz = _dot(jnp.concatenate([cur["y"].astype(BF16), p.astype(BF16)], axis=0), y_bd)
                cur["y"] = zz[0:c]
                cur["p"] = p + zz[c:2 * c]
            else:
                cur["p"] = p + _dot(p.astype(BF16), y_bd)
        run_front_task()
    for cur in st:
        cur["uw"] = _dot(cur["p"].astype(BF16), cur["rhs"])
    run_front_task()

    for ck in range(2):
        r = ck * c
        in_chunk = first_chunk if ck == 0 else jnp.logical_not(first_chunk)
        in_chunk_d = lane_row if ck == 0 else jnp.logical_not(lane_row)
        s_old = [state_ref[h] for h in range(nh)]
        ws_qs = []
        for h in range(nh):
            cur = st[h]
            w = cur["uw"][:, (2 * ck + 1) * d:(2 * ck + 2) * d]
            lhs = jnp.concatenate([w.astype(BF16), cur["qe"][r:r + c]], axis=0)
            ws_qs.append(_dot(lhs, s_old[h].astype(BF16)))
        run_front_task()
        for h in range(nh):
            cur = st[h]
            v_new = (cur["uw"][:, 2 * ck * d:(2 * ck + 1) * d] - ws_qs[h][0:c]).astype(BF16)
            lhs = jnp.concatenate([jnp.where(in_chunk, cur["attn"], 0.0).astype(BF16),
                                   jnp.where(in_chunk_d, cur["k_tail_t"], 0.0).astype(BF16)], axis=0)
            rhs = jnp.concatenate([v_new, zeros_cd] if ck == 0 else [zeros_cd, v_new], axis=0)
            av_kv = _dot(lhs, rhs)
            state_ref[h] = s_old[h] * cur["e_last"][ck] + av_kv[c:c + d]
            o = ws_qs[h][c:2 * c] + av_kv[0:c]
            o = (o * lax.rsqrt(jnp.mean(o * o, axis=-1, keepdims=True) + EPS)) * onorm
            z = z_s[slot_r, r:r + c, h * d:(h + 1) * d].astype(F32)
            o_ref[r:r + c, h * d:(h + 1) * d] = (o * _silu(z)).astype(o_ref.dtype)
    while front_tasks:
        run_front_task()

    xs_ref[0:HALO, :] = xs_ref[blk:blk + HALO, :]


def _gdn_front_gates(gates_ref, hp_ref, gc_s, gct_s, beta_s, slot_w):
    blk, c = GDN_BLOCK, CHUNK
    gates = gates_ref[...]
    a_log = hp_ref[0:1, :]
    dt_bias = hp_ref[1:2, :]
    beta = 1.0 / (1.0 + jnp.exp(-gates))
    sp_in = gates + dt_bias
    softplus = jnp.maximum(sp_in, 0.0) + jnp.log(1.0 + jnp.exp(-jnp.abs(sp_in)))
    g = (-jnp.exp(a_log) * softplus) * float(np.log2(np.e))

    row = lax.broadcasted_iota(jnp.int32, (blk, blk), 0)
    col = lax.broadcasted_iota(jnp.int32, (blk, blk), 1)
    tri = jnp.where((row >= col) & ((row // c) == (col // c)), 1.0, 0.0).astype(BF16)
    g_hi = g.astype(BF16)
    g_r1 = g - g_hi.astype(F32)
    g_mid = g_r1.astype(BF16)
    g_lo = (g_r1 - g_mid.astype(F32)).astype(BF16)
    gc = _dot(tri, g_hi) + _dot(tri, g_mid) + _dot(tri, g_lo)
    gc_s[slot_w] = gc
    gct_s[slot_w] = gc.T
    beta_s[slot_w] = beta


def gdn_core(proj, gates, conv_w, a_log, dt_bias, out_norm, batch, seq):
    t = proj.shape[0]
    nh, d = LA_HEADS, LA_D
    blk = GDN_BLOCK
    assert seq % blk == 0
    nblk = seq // blk
    hp = jnp.zeros((8, LANES), F32)
    hp = hp.at[0, nh:2 * nh].set(a_log.astype(F32)).at[1, nh:2 * nh].set(dt_bias.astype(F32))

    def in_map(b, n):
        return (b * nblk + jnp.minimum(n, nblk - 1), 0)

    return pl.pallas_call(
        _gdn_kernel,
        out_shape=jax.ShapeDtypeStruct((t, nh * d), BF16),
        grid=(batch, nblk + 1),
        in_specs=[pl.BlockSpec((blk, 4 * nh * d), in_map),
                  pl.BlockSpec((blk, LANES), in_map),
                  pl.BlockSpec((CONV_W, 3 * nh * d), lambda b, n: (0, 0)),
                  pl.BlockSpec((8, LANES), lambda b, n: (0, 0)),
                  pl.BlockSpec((1, d), lambda b, n: (0, 0))],
        out_specs=pl.BlockSpec((blk, nh * d), lambda b, n: (b * nblk + jnp.maximum(n - 1, 0), 0)),
        scratch_shapes=[pltpu.VMEM((HALO + blk, 3 * nh * d), BF16),
                        pltpu.VMEM((nh, d, d), F32),
                        pltpu.VMEM((2, blk, nh * d), F32), pltpu.VMEM((2, blk, nh * d), F32),
                        pltpu.VMEM((2, blk, nh * d), F32), pltpu.VMEM((2, blk, nh * d), BF16),
                        pltpu.VMEM((2, blk, LANES), F32), pltpu.VMEM((2, LANES, blk), F32),
                        pltpu.VMEM((2, blk, LANES), F32)],
        compiler_params=_cparams(("arbitrary", "arbitrary")),
        name="gdn_core",
    )(proj, gates, conv_w.astype(F32), hp, out_norm.reshape(1, d).astype(F32))


TILE_FULL, TILE_FIRST, TILE_HALF = 0, 1, 2


def _swiglu_kernel(te_ref, mode_ref, nv_ref, x_ref, g_ref, a_ref, wp_ref, wg_hbm, wu_hbm, wd_hbm, o_ref,
                   wg_c, wu_c, wd_c, stage_in, stage_out, sems, xres_s, *, pre_norm, pre_proj, routed, tf):
    i = pl.program_id(0)
    nf = wg_c.shape[0]
    valid = i < nv_ref[0]

    def chunk_copies(j, slot, tile=None):
        e = te_ref[i if tile is None else tile]
        cols = pl.ds(j * tf, tf)
        return (pltpu.make_async_copy(wg_hbm.at[e, :, cols], stage_in.at[slot, 0], sems.at[slot, 0]),
                pltpu.make_async_copy(wu_hbm.at[e, :, cols], stage_in.at[slot, 1], sems.at[slot, 1]),
                pltpu.make_async_copy(wd_hbm.at[e, cols, :], stage_out.at[slot], sems.at[slot, 2]))

    if routed:
        nxt = jnp.minimum(i + 1, pl.num_programs(0) - 1)
        prefetch_next = (valid & (mode_ref[i] != TILE_FIRST) & (i + 1 < nv_ref[0]) & (mode_ref[nxt] == TILE_FIRST))
        prv = jnp.maximum(i - 1, 0)
        chunk0_requested = (i > 0) & (mode_ref[prv] != TILE_FIRST)

        @pl.when(prefetch_next)
        def _():
            for c in chunk_copies(0, 0, tile=nxt):
                c.start()
    else:
        chunk0_requested = False

    tile_rows = x_ref.shape[0]

    d_model = o_ref.shape[1]

    def prepare_rows(rows):
        x = x_ref[0:rows, 0:d_model].astype(F32)
        if pre_proj:
            x = x + _dot(a_ref[0:rows, :], wp_ref[...])
            xres_s[0:rows, :] = x
        if pre_norm:
            ms = jnp.mean(x * x, axis=-1, keepdims=True)
            x = (x * lax.rsqrt(ms + EPS)) * g_ref[...]
        return x.astype(BF16)

    def chunk(xb, j):
        hid = _silu(_dot(xb, wg_c[j]), base2=False) * _dot(xb, wu_c[j])
        return _dot(hid.astype(BF16), wd_c[j])

    def finish(acc, rows):
        if pre_norm:
            res = xres_s[0:rows, :] if pre_proj else x_ref[0:rows, 0:d_model]
            acc = res + acc
        o_ref[0:rows, :] = acc.astype(o_ref.dtype)
        if rows < tile_rows:
            o_ref[rows:tile_rows, :] = jnp.zeros((tile_rows - rows, o_ref.shape[1]), o_ref.dtype)

    mode = mode_ref[i]

    @pl.when(valid & (mode == TILE_FIRST) & jnp.logical_not(chunk0_requested))
    def _():
        for c in chunk_copies(0, 0):
            c.start()


    @pl.when(valid & (mode == TILE_FIRST))
    def _():
        xb = prepare_rows(tile_rows)
        acc = None
        for j in range(nf):
            slot = j % 2
            if j + 1 < nf:
                for c in chunk_copies(j + 1, 1 - slot):
                    c.start()
            for c in chunk_copies(j, slot):
                c.wait()
            wg_c[j] = stage_in[slot, 0].astype(BF16)
            wu_c[j] = stage_in[slot, 1].astype(BF16)
            wd_c[j] = stage_out[slot].astype(BF16)
            y = chunk(xb, j)
            acc = y if acc is None else acc + y
        finish(acc, tile_rows)

    def steady(rows):
        xb = prepare_rows(rows)
        acc = None
        for j in range(nf):
            y = chunk(xb, j)
            acc = y if acc is None else acc + y
        finish(acc, rows)

    @pl.when(valid & (mode == TILE_FULL))
    def _():
        steady(tile_rows)

    if routed:
        @pl.when(valid & (mode == TILE_HALF))
        def _():
            steady(tile_rows // 2)

    @pl.when(jnp.logical_not(valid))
    def _():
        o_ref[...] = jnp.zeros_like(o_ref)


def expert_swiglu(x, gain, tile_expert, tile_mode, n_valid, wg, wu, wd, tile_rows, tf, out_dtype, pre_norm, name,
                  proj=None, routed=False):
    n_rows = x.shape[0]
    ne, d, f = wg.shape
    assert n_rows % tile_rows == 0 and f % tf == 0
    n_tiles = n_rows // tile_rows
    nf = f // tf
    pre_proj = proj is not None
    if pre_proj:
        a, wp = proj
        a_spec = pl.BlockSpec((tile_rows, a.shape[1]), lambda i, te, fi, nv: (jnp.minimum(i, nv[0] - 1), 0))
    else:
        a, wp = jnp.zeros((8, LANES), BF16), jnp.zeros((LANES, d), BF16)
        a_spec = pl.BlockSpec(a.shape, lambda i, te, fi, nv: (0, 0))
    grid_spec = pltpu.PrefetchScalarGridSpec(
        num_scalar_prefetch=3,
        grid=(n_tiles,),
        in_specs=[pl.BlockSpec((tile_rows, x.shape[1]), lambda i, te, fi, nv: (jnp.minimum(i, nv[0] - 1), 0)),
                  pl.BlockSpec((1, d), lambda i, te, fi, nv: (0, 0)),
                  a_spec,
                  pl.BlockSpec(wp.shape, lambda i, te, fi, nv: (0, 0)),
                  pl.BlockSpec(memory_space=pl.ANY),
                  pl.BlockSpec(memory_space=pl.ANY),
                  pl.BlockSpec(memory_space=pl.ANY)],
        out_specs=pl.BlockSpec((tile_rows, d), lambda i, te, fi, nv: (i, 0)),
        scratch_shapes=[pltpu.VMEM((nf, d, tf), BF16), pltpu.VMEM((nf, d, tf), BF16), pltpu.VMEM((nf, tf, d), BF16),
                        pltpu.VMEM((2, 2, d, tf), F32), pltpu.VMEM((2, tf, d), F32),
                        pltpu.SemaphoreType.DMA((2, 3)),
                        pltpu.VMEM((tile_rows, d) if pre_proj else (8, LANES), F32)],
    )
    return pl.pallas_call(
        functools.partial(_swiglu_kernel, pre_norm=pre_norm, pre_proj=pre_proj, routed=routed, tf=tf),
        out_shape=jax.ShapeDtypeStruct((n_rows, d), out_dtype),
        grid_spec=grid_spec,
        compiler_params=pltpu.CompilerParams(dimension_semantics=("arbitrary",), vmem_limit_bytes=EXPERT_VMEM_LIMIT),
        name=name,
    )(tile_expert, tile_mode, n_valid, x, gain.reshape(1, d).astype(F32), a, wp, wg, wu, wd)


def ffn_dense(x, gain, wg, wu, wd, tm, tf, proj=None):
    t = x.shape[0]
    n_tiles = t // tm
    tile_mode = jnp.full((n_tiles,), TILE_FULL, jnp.int32).at[0].set(TILE_FIRST)
    return expert_swiglu(x, gain, jnp.zeros((n_tiles,), jnp.int32), tile_mode, jnp.full((1,), n_tiles, jnp.int32),
                         wg[None], wu[None], wd[None], tm, tf, F32, True, "ffn_dense", proj=proj)


def _t5_bucket_np(dist):
    max_exact = N_BUCKETS // 2
    n = np.maximum(dist, 0)
    safe = np.maximum(n, 1).astype(np.float32)
    large = max_exact + (np.log(safe / max_exact) / np.log(MAX_DIST / max_exact)
                         * (N_BUCKETS - max_exact)).astype(np.int32)
    large = np.minimum(large, N_BUCKETS - 1)
    return np.where(n < max_exact, n, large).astype(np.int32)


def _bias_kernel(bucket_ref, valid_ref, rb_ref, o_ref):
    bucket = bucket_ref[...]
    for h in range(SW_HEADS):
        acc = jnp.zeros(bucket.shape, F32)
        for b in range(N_BUCKETS):
            acc = jnp.where(bucket == b, rb_ref[b, h], acc)
        for v in range(valid_ref.shape[0]):
            o_ref[v, h] = jnp.where(valid_ref[v] > 0, acc * LOG2E, NEG_INF)


def bias_table(rel_bias):
    qi = np.arange(WINDOW)[:, None] + WINDOW
    kj = np.arange(2 * WINDOW)[None, :]
    dist = qi - kj
    band = (dist >= 0) & (dist < WINDOW)
    valid = np.stack([band, band & (kj >= WINDOW)]).astype(np.int32)
    return pl.pallas_call(
        _bias_kernel,
        out_shape=jax.ShapeDtypeStruct((2, SW_HEADS, WINDOW, 2 * WINDOW), F32),
        in_specs=[pl.BlockSpec(memory_space=pltpu.VMEM), pl.BlockSpec(memory_space=pltpu.VMEM),
                  pl.BlockSpec(memory_space=pltpu.SMEM)],
        out_specs=pl.BlockSpec(memory_space=pltpu.VMEM),
        name="t5_bias_table",
    )(jnp.asarray(_t5_bucket_np(dist)), jnp.asarray(valid), rel_bias.astype(F32))


def _swa_kernel(q_ref, kvp_ref, kvc_ref, bias_ref, qn_ref, kn_ref, sink_ref, o_ref):
    blk, hd = WINDOW, SW_HD
    kv_w = SW_KV_HEADS * hd
    first = jnp.where(pl.program_id(1) == 0, 1, 0)
    gw = 2 * LANES
    gi = lax.broadcasted_iota(jnp.int32, (gw, gw), 0)
    gj = lax.broadcasted_iota(jnp.int32, (gw, gw), 1)
    group_ones = jnp.where((gi // hd) == (gj // hd), 1.0, 0.0).astype(BF16)
    lane = lax.broadcasted_iota(jnp.int32, (1, LANES), 1)
    low_half = lane < hd

    def head_norm(x, gain):
        cols = []
        for c0 in range(0, x.shape[1], gw):
            xc = x[:, c0:c0 + gw]
            ss = _dot((xc * xc).astype(BF16), group_ones)
            cols.append(xc * lax.rsqrt(ss * (1.0 / hd) + EPS))
        return jnp.concatenate(cols, axis=1) * gain

    def dup_half(x, half):
        swapped = pltpu.roll(x, hd, 1)
        return jnp.where(low_half == (half == 0), x, swapped)

    n_qb = q_ref.shape[0] // blk
    qn = head_norm(q_ref[...].astype(F32), qn_ref[...]) * ((hd ** -0.5) * LOG2E)
    half_sel = [jnp.where(low_half, 1.0, 0.0), jnp.where(low_half, 0.0, 1.0)]
    k_all = jnp.concatenate([kvp_ref[:, 0:kv_w], kvc_ref[:, 0:kv_w]], axis=0).astype(F32)
    kn = head_norm(k_all, kn_ref[...])
    v_all = jnp.concatenate([kvp_ref[:, kv_w:2 * kv_w], kvc_ref[:, kv_w:2 * kv_w]], axis=0).astype(F32)
    ks, vs = [], []
    for g in range(SW_KV_HEADS):
        c0 = (g // 2) * LANES
        ks.append(dup_half(kn[:, c0:c0 + LANES], g % 2).astype(BF16))
        vs.append(dup_half(v_all[:, c0:c0 + LANES], g % 2).astype(BF16))

    pairs = [(j, hq) for j in range(n_qb) for hq in range(SW_HEADS)]
    scores = {}
    for (j, hq) in pairs:
        c0 = (hq // 2) * LANES
        q_h = (qn[j * blk:(j + 1) * blk, c0:c0 + LANES] * half_sel[hq % 2]).astype(BF16)
        scores[(j, hq)] = _dot_nt(q_h, ks[hq // SW_GROUP][j * blk:(j + 2) * blk])
    probs = {}
    for (j, hq) in pairs:
        variant = first if j == 0 else 0
        s = scores[(j, hq)] + bias_ref[variant, hq]
        sink = sink_ref[hq] * LOG2E
        mx = jnp.maximum(jnp.max(s, axis=-1, keepdims=True), sink)
        p = jnp.exp2(s - mx)
        denom = jnp.sum(p, axis=-1, keepdims=True) + jnp.exp2(sink - mx)
        probs[(j, hq)] = (p * (1.0 / denom)).astype(BF16)
    outs = {key: _dot(probs[key], vs[key[1] // SW_GROUP][key[0] * blk:(key[0] + 2) * blk]) for key in pairs}
    for j in range(n_qb):
        for c in range(SW_HEADS // 2):
            o_ref[j * blk:(j + 1) * blk, c * LANES:(c + 1) * LANES] = jnp.where(
                low_half, outs[(j, 2 * c)], outs[(j, 2 * c + 1)]).astype(o_ref.dtype)


def swa_attention(q, kv, bias, q_norm, k_norm, sinks, batch, seq):
    t = q.shape[0]
    blk = WINDOW
    step = SWA_QBLOCKS * blk
    assert seq % step == 0
    nb = seq // step
    qw = SW_HEADS * SW_HD
    kvw = 2 * SW_KV_HEADS * SW_HD
    return pl.pallas_call(
        _swa_kernel,
        out_shape=jax.ShapeDtypeStruct((t, qw), BF16),
        grid=(batch, nb),
        in_specs=[pl.BlockSpec((step, qw), lambda b, n: (b * nb + n, 0)),
                  pl.BlockSpec((blk, kvw), lambda b, n: (jnp.maximum((b * nb + n) * SWA_QBLOCKS - 1, b * nb * SWA_QBLOCKS), 0)),
                  pl.BlockSpec((step, kvw), lambda b, n: (b * nb + n, 0)),
                  pl.BlockSpec((2, SW_HEADS, blk, 2 * blk), lambda b, n: (0, 0, 0, 0)),
                  pl.BlockSpec((1, qw), lambda b, n: (0, 0)),
                  pl.BlockSpec((1, kvw // 2), lambda b, n: (0, 0)),
                  pl.BlockSpec(memory_space=pltpu.SMEM)],
        out_specs=pl.BlockSpec((step, qw), lambda b, n: (b * nb + n, 0)),
        compiler_params=_cparams(("parallel", "parallel")),
        name="swa_attention",
    )(q, kv, kv, bias, jnp.tile(q_norm.astype(F32), SW_HEADS).reshape(1, qw),
      jnp.tile(k_norm.astype(F32), SW_KV_HEADS).reshape(1, kvw // 2), sinks.astype(F32))


def _route_kernel(x_ref, a_ref, wp_ref, g_ref, wr_ref, h_ref, r_ref, wt_ref, tab_ref, cnt_ref,
                  sel_s, gw_s, cnt_s, start_s, run_s, *, tile_rows):
    ne = N_EXPERTS
    p = pl.program_id(0)
    i = pl.program_id(1)
    tm = x_ref.shape[0]
    sub = lax.broadcasted_iota(jnp.int32, (ne, tm), 0).astype(F32)

    def seg_rows(sel):
        n = jnp.sum(sel, axis=1, keepdims=True)
        return jnp.floor((n + (SEG_ALIGN - 1)) * (1.0 / SEG_ALIGN)) * SEG_ALIGN

    def excl_cumsum_experts(v):
        sub8 = lax.broadcasted_iota(jnp.int32, v.shape, 0)
        out = jnp.zeros_like(v)
        for e in range(ne - 1):
            out = out + jnp.where(sub8 > e, v[e:e + 1, :], 0.0)
        return out

    @pl.when(p == 0)
    def _():
        @pl.when(i == 0)
        def _():
            cnt_s[...] = jnp.zeros_like(cnt_s)

        x = x_ref[...] + _dot(a_ref[...], wp_ref[...])
        h_ref[...] = x
        ms = jnp.mean(x * x, axis=-1, keepdims=True)
        xn32 = (x * lax.rsqrt(ms + EPS)) * g_ref[...]
        xn_hi = xn32.astype(BF16)
        xn_lo = (xn32 - xn_hi.astype(F32)).astype(BF16)
        p_hi = _dot_nt(wr_ref[...], xn_hi)
        p_lo = _dot_nt(wr_ref[...], xn_lo)
        logits = p_hi[0:ne] + p_hi[ne:2 * ne] + p_lo[0:ne]
        m1 = jnp.max(logits, axis=0, keepdims=True)
        i1 = jnp.min(jnp.where(logits == m1, sub, float(ne)), axis=0, keepdims=True)
        l2 = jnp.where(sub == i1, -jnp.inf, logits)
        m2 = jnp.max(l2, axis=0, keepdims=True)
        i2 = jnp.min(jnp.where(l2 == m2, sub, float(ne)), axis=0, keepdims=True)
        e2 = jnp.exp(m2 - m1)
        w1 = 1.0 / (1.0 + e2)
        w2 = e2 / (1.0 + e2)
        sel = jnp.where((sub == i1) | (sub == i2), 1.0, 0.0)
        sel_s[i] = sel
        gw_s[i] = jnp.where(sub == i1, w1, jnp.where(sub == i2, w2, 0.0))
        cnt_s[...] += seg_rows(sel)

    @pl.when(p == 1)
    def _():
        @pl.when(i == 0)
        def _():
            cnt = cnt_s[...]
            padded = jnp.floor((cnt + (tile_rows - 1)) * (1.0 / tile_rows)) * tile_rows
            start_s[...] = excl_cumsum_experts(padded)
            run_s[...] = jnp.zeros_like(run_s)
            cnt_ref[...] = cnt

        sel = sel_s[i]
        gw = gw_s[i]
        ti = lax.broadcasted_iota(jnp.int32, (tm, tm), 0)
        tj = lax.broadcasted_iota(jnp.int32, (tm, tm), 1)
        tri = jnp.where(ti <= tj, 1.0, 0.0).astype(BF16)
        csum = _dot(sel.astype(BF16), tri)
        seg = jnp.broadcast_to(seg_rows(sel), run_s.shape)
        local0 = excl_cumsum_experts(seg)
        tab_ref[0, 0] = start_s[...] + run_s[...]
        tab_ref[0, 1] = seg
        tab_ref[0, 2] = local0
        run_s[...] += seg
        local_row = local0[:, 0:1] + csum - sel
        ia = jnp.min(jnp.where(sel > 0.0, sub, float(ne)), axis=0, keepdims=True)
        ib = jnp.max(jnp.where(sel > 0.0, sub, -1.0), axis=0, keepdims=True)
        pick_a = sub == ia
        pick_b = sub == ib
        rows = [jnp.sum(jnp.where(pick_a, local_row, 0.0), axis=0, keepdims=True),
                jnp.sum(jnp.where(pick_b, local_row, 0.0), axis=0, keepdims=True),
                jnp.sum(jnp.where(pick_a, gw, 0.0), axis=0, keepdims=True),
                jnp.sum(jnp.where(pick_b, gw, 0.0), axis=0, keepdims=True)]
        r_ref[...] = jnp.concatenate(rows + [jnp.zeros((ne - 4, tm), F32)], axis=0)
        wpad = jnp.concatenate(rows[2:4] + rows[0:2] + [jnp.zeros((LANES - 4, tm), F32)], axis=0)
        wt_ref[...] = wpad.T


def moe_route(x, a, wp, gain, w_router, tm, tile_rows):
    t, d = x.shape
    ne = w_router.shape[1]
    assert ne == N_EXPERTS
    w_hi = w_router.astype(BF16)
    w_lo = (w_router - w_hi.astype(F32)).astype(BF16)
    wr = jnp.concatenate([w_hi.T, w_lo.T], axis=0)
    tm = min(tm, t)
    nt = t // tm

    def row_map(p, i):
        return (i * (1 - p) + (nt - 1) * p, 0)

    return pl.pallas_call(
        functools.partial(_route_kernel, tile_rows=tile_rows),
        out_shape=(jax.ShapeDtypeStruct((t, d), F32),
                   jax.ShapeDtypeStruct((ne, t), F32), jax.ShapeDtypeStruct((t, LANES), F32),
                   jax.ShapeDtypeStruct((nt, 3, ne, LANES), F32), jax.ShapeDtypeStruct((ne, LANES), F32)),
        grid=(2, nt),
        in_specs=[pl.BlockSpec((tm, d), row_map),
                  pl.BlockSpec((tm, a.shape[1]), row_map),
                  pl.BlockSpec(wp.shape, lambda p, i: (0, 0)),
                  pl.BlockSpec((1, d), lambda p, i: (0, 0)),
                  pl.BlockSpec((2 * ne, d), lambda p, i: (0, 0))],
        out_specs=(pl.BlockSpec((tm, d), row_map),
                   pl.BlockSpec((ne, tm), lambda p, i: (0, i * p)),
                   pl.BlockSpec((tm, LANES), lambda p, i: (i * p, 0)),
                   pl.BlockSpec((1, 3, ne, LANES), lambda p, i: (i * p, 0, 0, 0)),
                   pl.BlockSpec((ne, LANES), lambda p, i: (0, 0))),
        scratch_shapes=[pltpu.VMEM((nt, ne, tm), F32), pltpu.VMEM((nt, ne, tm), F32),
                        pltpu.VMEM((ne, LANES), F32), pltpu.VMEM((ne, LANES), F32), pltpu.VMEM((ne, LANES), F32)],
        compiler_params=_cparams(("arbitrary", "arbitrary")),
        name="moe_route",
    )(x, a, wp, gain.reshape(1, d), wr)


def _segment_copies(tab_ref, i, e, local_ref, slot_ref, sem, to_slots):
    base = (i * N_EXPERTS + e) * 3
    slot0, rows, local0 = tab_ref[base], tab_ref[base + 1], tab_ref[base + 2]
    out = []
    done = 0
    size = MOE_TOKEN_TILE
    while size >= SEG_ALIGN:
        take = rows & size
        loc = local_ref.at[pl.ds(pl.multiple_of(local0 + done, SEG_ALIGN), size)]
        slt = slot_ref.at[pl.ds(pl.multiple_of(slot0 + done, SEG_ALIGN), size)]
        desc = pltpu.make_async_copy(loc, slt, sem) if to_slots else pltpu.make_async_copy(slt, loc, sem)
        out.append((take != 0, desc))
        done = done + take
        size //= 2
    return out


def _run_segment_copies(tab_ref, tile, slot, rows_s, slot_ref, sems, to_slots, action):
    for e in range(N_EXPERTS):
        for cond, desc in _segment_copies(tab_ref, tile, e, rows_s.at[slot], slot_ref, sems.at[slot], to_slots):
            @pl.when(cond)
            def _():
                getattr(desc, action)()


def _dispatch_kernel(tab_ref, zf_ref, x_ref, g_ref, r_ref, xs_ref, rows_s, zero_s, sem, zsem, *, tile_rows):
    i = pl.program_id(0)
    tm = x_ref.shape[0]
    n_local = rows_s.shape[1]

    @pl.when(i == 0)
    def _():
        zero_s[...] = jnp.zeros_like(zero_s)

        def zero_copy(e):
            row0 = pl.multiple_of(zf_ref[e], tile_rows)
            return pltpu.make_async_copy(zero_s, xs_ref.at[pl.ds(row0, tile_rows)], zsem)

        for e in range(zf_ref.shape[0]):
            @pl.when(zf_ref[e] >= 0)
            def _():
                zero_copy(e).start()
        for e in range(zf_ref.shape[0]):
            @pl.when(zf_ref[e] >= 0)
            def _():
                zero_copy(e).wait()

    x = x_ref[...]
    ms = jnp.mean(x * x, axis=-1, keepdims=True)
    xn = ((x * lax.rsqrt(ms + EPS)) * g_ref[...]).astype(BF16)
    row_id = lax.broadcasted_iota(jnp.int32, (n_local, tm), 0).astype(F32)
    onehot = jnp.where((row_id == r_ref[0:1, :]) | (row_id == r_ref[1:2, :]), 1.0, 0.0).astype(BF16)
    slot = lax.rem(i, 2)
    rows_s[slot] = _dot(onehot, xn)

    _run_segment_copies(tab_ref, i, slot, rows_s, xs_ref, sem, True, "start")

    @pl.when(i > 0)
    def _():
        _run_segment_copies(tab_ref, i - 1, 1 - slot, rows_s, xs_ref, sem, True, "wait")

    @pl.when(i == pl.num_programs(0) - 1)
    def _():
        _run_segment_copies(tab_ref, i, slot, rows_s, xs_ref, sem, True, "wait")


def moe_dispatch(x, gain, r, tab, zf_rows, n_slots, tm, tile_rows):
    t, d = x.shape
    nt = t // tm
    n_local = TOP_K * tm + N_EXPERTS * SEG_ALIGN
    grid_spec = pltpu.PrefetchScalarGridSpec(
        num_scalar_prefetch=2,
        grid=(nt,),
        in_specs=[pl.BlockSpec((tm, d), lambda i, tb, zf: (i, 0)),
                  pl.BlockSpec((1, d), lambda i, tb, zf: (0, 0)),
                  pl.BlockSpec((N_EXPERTS, tm), lambda i, tb, zf: (0, i))],
        out_specs=pl.BlockSpec(memory_space=pl.ANY),
        scratch_shapes=[pltpu.VMEM((2, n_local, d), F32), pltpu.VMEM((tile_rows, d), F32),
                        pltpu.SemaphoreType.DMA((2,)), pltpu.SemaphoreType.DMA],
    )
    return pl.pallas_call(
        functools.partial(_dispatch_kernel, tile_rows=tile_rows),
        out_shape=jax.ShapeDtypeStruct((n_slots, d), F32),
        grid_spec=grid_spec,
        compiler_params=_cparams(("arbitrary",)),
        name="moe_dispatch",
    )(tab, zf_rows, x, gain.reshape(1, d), r)


def _combine_kernel(tab_ref, h_ref, wt_ref, ys_ref, o_ref, rows_s, sems):
    i = pl.program_id(0)
    tm = h_ref.shape[0]
    n_local = rows_s.shape[1]
    slot = lax.rem(i, 2)

    def fetch(tile, into):
        rows_s[into] = jnp.zeros(rows_s.shape[1:], rows_s.dtype)
        _run_segment_copies(tab_ref, tile, into, rows_s, ys_ref, sems, False, "start")

    @pl.when(i == 0)
    def _():
        fetch(i, slot)

    @pl.when(i + 1 < pl.num_programs(0))
    def _():
        fetch(i + 1, 1 - slot)

    _run_segment_copies(tab_ref, i, slot, rows_s, ys_ref, sems, False, "wait")

    wt = wt_ref[...]
    y = rows_s[slot].astype(BF16)
    col_id = lax.broadcasted_iota(jnp.int32, (tm, n_local), 1).astype(F32)
    pick_a = jnp.where(col_id == wt[:, 2:3], 1.0, 0.0).astype(BF16)
    pick_b = jnp.where(col_id == wt[:, 3:4], 1.0, 0.0).astype(BF16)
    o_ref[...] = h_ref[...] + wt[:, 0:1] * _dot(pick_a, y) + wt[:, 1:2] * _dot(pick_b, y)


def moe_combine(h, wt, tab, ys, tm):
    t, d = h.shape
    nt = t // tm
    n_local = TOP_K * tm + N_EXPERTS * SEG_ALIGN
    grid_spec = pltpu.PrefetchScalarGridSpec(
        num_scalar_prefetch=1,
        grid=(nt,),
        in_specs=[pl.BlockSpec((tm, d), lambda i, tb: (i, 0)),
                  pl.BlockSpec((tm, LANES), lambda i, tb: (i, 0)),
                  pl.BlockSpec(memory_space=pl.ANY)],
        out_specs=pl.BlockSpec((tm, d), lambda i, tb: (i, 0)),
        scratch_shapes=[pltpu.VMEM((2, n_local, d), F32), pltpu.SemaphoreType.DMA((2,))],
    )
    return pl.pallas_call(
        _combine_kernel,
        out_shape=jax.ShapeDtypeStruct((t, d), F32),
        grid_spec=grid_spec,
        compiler_params=_cparams(("arbitrary",)),
        name="moe_combine",
    )(tab, h, wt, ys)


def moe_layer(x, a, wp, gain, w_router, wg, wu, wd):
    t, d = x.shape
    ne = w_router.shape[1]
    tr, tm = MOE_TILE_ROWS, MOE_TOKEN_TILE
    assert t % tm == 0 and ne == N_EXPERTS
    nt = t // tm
    n_tiles = -(-(TOP_K * t + nt * ne * (SEG_ALIGN - 1) + ne * (tr - 1)) // tr)
    n_slots = n_tiles * tr

    h, r, wt, tab, cnt = moe_route(x, a, wp, gain, w_router, tm, tr)
    tab = jnp.transpose(tab[:, :, :, 0], (0, 2, 1)).astype(jnp.int32).reshape(-1)

    counts = cnt[:, 0].astype(jnp.int32)
    padded = ((counts + (tr - 1)) // tr) * tr
    ends = jnp.cumsum(padded)
    n_valid = (ends[-1] // tr).astype(jnp.int32)
    tile_row0 = jnp.arange(n_tiles, dtype=jnp.int32) * tr
    tile_expert = jnp.sum((tile_row0[:, None] >= ends[None, :]).astype(jnp.int32), axis=1)
    tile_expert = jnp.minimum(tile_expert, ne - 1)
    tile_expert = jnp.where(jnp.arange(n_tiles) < n_valid, tile_expert, tile_expert[jnp.maximum(n_valid - 1, 0)])
    prev_expert = jnp.concatenate([jnp.full((1,), -1, jnp.int32), tile_expert[:-1]])
    rows_used = (ends - padded + counts)[tile_expert] - tile_row0
    tile_mode = jnp.where(tile_expert != prev_expert, TILE_FIRST,
                          jnp.where(rows_used <= tr // 2, TILE_HALF, TILE_FULL)).astype(jnp.int32)
    tail = jnp.arange(TOP_K * t // tr, n_tiles, dtype=jnp.int32)
    zf_rows = jnp.concatenate([jnp.where(padded > 0, ends - tr, -1),
                               jnp.where(tail >= n_valid, tail * tr, -1)]).astype(jnp.int32)

    xs = moe_dispatch(h, gain, r, tab, zf_rows, n_slots, tm, tr)
    ys = expert_swiglu(xs, gain, tile_expert, tile_mode, n_valid.reshape(1), wg, wu, wd, tr, FFN_CHUNK, F32, False,
                       "moe_experts", routed=True)
    return moe_combine(h, wt, tab, ys, tm)


def kernel(x, a_norm, a_w_in, a_conv, a_log_decay, a_dt_bias, a_out_norm, a_w_out, kv_norm, kv_w, k_norm,
           b_norm, b_w_q, q_norm, b_sinks, b_w_o, rel_bias, ffn_norm, dense_w_gate, dense_w_up, dense_w_down,
           moe_router, moe_w_gate, moe_w_up, moe_w_down):
    batch, seq, d = x.shape
    t = batch * seq
    nh, hd = LA_HEADS, LA_D
    main_w = 4 * nh * hd
    h0 = x.reshape(t, d)

    w_in = a_w_in[0]
    w_main = w_in[:, 0:main_w].astype(BF16)
    w_gate = jnp.zeros((d, LANES), BF16).at[:, 0:2 * nh].set(w_in[:, main_w:main_w + 2 * nh].astype(BF16))
    proj, gates = norm_matmul(h0, [(a_norm[0], w_main, BF16), (a_norm[0], w_gate, F32)], IN_PROJ_TILE,
                              "gdn_in_proj")
    o = gdn_core(proj, gates, a_conv[0], a_log_decay[0], a_dt_bias[0], a_out_norm[0], batch, seq)

    h2 = ffn_dense(h0, ffn_norm[0], dense_w_gate[0], dense_w_up[0], dense_w_down[0], MOE_TILE_ROWS, FFN_CHUNK,
                   proj=(o, a_w_out[0].astype(BF16)))

    kv, q = norm_matmul(h2, [(kv_norm, kv_w.astype(BF16), BF16), (b_norm[0], b_w_q[0].astype(BF16), BF16)],
                        QKV_PROJ_TILE, "qkv_proj")
    bias = bias_table(rel_bias)
    attn = swa_attention(q, kv, bias, q_norm[0], k_norm, b_sinks[0], batch, seq)

    h4 = moe_layer(h2, attn, b_w_o[0].astype(BF16), ffn_norm[1], moe_router[0], moe_w_gate[0], moe_w_up[0],
                   moe_w_down[0])
    return h4.reshape(batch, seq, d)
```

```python
import functools

import numpy as np
import jax
import jax.numpy as jnp
from jax import lax
from jax.experimental import pallas as pl
from jax.experimental.pallas import tpu as pltpu

F32 = jnp.float32
BF16 = jnp.bfloat16

EPS = 1e-6
NEG_INF = -1e30

LA_HEADS = 8
LA_D = 128
CONV_W = 4
CHUNK = 64
SW_HEADS = 16
SW_KV_HEADS = 4
SW_GROUP = SW_HEADS // SW_KV_HEADS
SW_HD = 64
WINDOW = 128
SWA_QBLOCKS = 2
SWA_GROUP = 16
N_BUCKETS = 32
MAX_DIST = 128
N_EXPERTS = 8
TOP_K = 2
LOG2E = float(np.log2(np.e))

LANES = 128
SEG_ALIGN = 8
GDN_BLOCK = 2 * CHUNK
HALO = 16
MOE_TILE_ROWS = 512
MOE_TOKEN_TILE = 512
FFN_CHUNK = 512
IN_PROJ_TILE = 1024
QKV_PROJ_TILE = 1024

VMEM_LIMIT = 56 * 1024 * 1024
EXPERT_VMEM_LIMIT = 60 * 1024 * 1024


def _cparams(sem):
    return pltpu.CompilerParams(dimension_semantics=sem, vmem_limit_bytes=VMEM_LIMIT)


def _silu(x, base2=True):
    e = jnp.exp2(x * (-LOG2E)) if base2 else jnp.exp(-x)
    return x * (1.0 / (1.0 + e))


def _dot(a, b):
    return jnp.dot(a, b, preferred_element_type=F32)


def _dot_nt(a, b):
    return lax.dot_general(a, b, (((1,), (1,)), ((), ())), preferred_element_type=F32)


def _norm_matmul_kernel(*refs, n_groups):
    x_ref = refs[0]
    g_refs = refs[1:1 + n_groups]
    w_refs = refs[1 + n_groups:1 + 2 * n_groups]
    o_refs = refs[1 + 2 * n_groups:1 + 3 * n_groups]
    x = x_ref[...]
    xr = x * lax.rsqrt(jnp.mean(x * x, axis=-1, keepdims=True) + EPS)
    for g_ref, w_ref, o_ref in zip(g_refs, w_refs, o_refs):
        o_ref[...] = _dot((xr * g_ref[...]).astype(BF16), w_ref[...]).astype(o_ref.dtype)


def norm_matmul(x, groups, tm, name):
    t, d = x.shape
    tm = min(tm, t)
    assert t % tm == 0
    gains = [g.reshape(1, d).astype(F32) for g, _, _ in groups]
    ws = [w for _, w, _ in groups]
    return pl.pallas_call(
        functools.partial(_norm_matmul_kernel, n_groups=len(groups)),
        out_shape=[jax.ShapeDtypeStruct((t, w.shape[1]), dt) for _, w, dt in groups],
        grid=(t // tm,),
        in_specs=([pl.BlockSpec((tm, d), lambda i: (i, 0))]
                  + [pl.BlockSpec((1, d), lambda i: (0, 0)) for _ in groups]
                  + [pl.BlockSpec(w.shape, lambda i: (0, 0)) for w in ws]),
        out_specs=[pl.BlockSpec((tm, w.shape[1]), lambda i: (i, 0)) for w in ws],
        compiler_params=_cparams(("parallel",)),
        name=name,
    )(x, *gains, *ws)


def _gdn_kernel(proj_ref, gates_ref, convw_ref, hp_ref, onorm_ref, o_ref,
                xs_ref, state_ref, q_s, k_s, v_s, z_s, gc_s, gct_s, beta_s):
    n = pl.program_id(1)

    @pl.when(n == 0)
    def _():
        xs_ref[0:HALO, :] = jnp.zeros((HALO, xs_ref.shape[1]), xs_ref.dtype)
        for ref in (q_s, k_s, v_s, z_s, gc_s, gct_s, beta_s):
            ref[1] = jnp.zeros(ref.shape[1:], ref.dtype)

    @pl.when(n <= 1)
    def _():
        state_ref[...] = jnp.zeros_like(state_ref)

    args = (proj_ref, gates_ref, convw_ref, hp_ref, onorm_ref, o_ref, xs_ref, state_ref,
            q_s, k_s, v_s, z_s, gc_s, gct_s, beta_s)

    @pl.when(lax.rem(n, 2) == 0)
    def _():
        _gdn_step(*args, slot_w=0, slot_r=1)

    @pl.when(lax.rem(n, 2) == 1)
    def _():
        _gdn_step(*args, slot_w=1, slot_r=0)


def _gdn_step(proj_ref, gates_ref, convw_ref, hp_ref, onorm_ref, o_ref, xs_ref, state_ref,
              q_s, k_s, v_s, z_s, gc_s, gct_s, beta_s, *, slot_w, slot_r):
    nh, d, c = LA_HEADS, LA_D, CHUNK
    blk = GDN_BLOCK
    qkv_w = 3 * nh * d

    gc = gc_s[slot_r]
    gc_t = gct_s[slot_r]
    beta = beta_s[slot_r]

    xs_ref[HALO:HALO + blk, :] = proj_ref[:, 0:qkv_w]

    def front_gates():
        _gdn_front_gates(gates_ref, hp_ref, gc_s, gct_s, beta_s, slot_w)

    di = lax.broadcasted_iota(jnp.int32, (d, d), 0)
    dj = lax.broadcasted_iota(jnp.int32, (d, d), 1)
    eye_d = jnp.where(di == dj, 1.0, 0.0).astype(BF16)

    onorm = onorm_ref[...]

    n_shift = CONV_W - 1
    sr = lax.broadcasted_iota(jnp.int32, (n_shift * blk, HALO + blk), 0)
    sc = lax.broadcasted_iota(jnp.int32, (n_shift * blk, HALO + blk), 1)
    shift_mat = jnp.where(sc == HALO + (sr % blk) - (sr // blk + 1), 1.0, 0.0).astype(BF16)
    pair_w = 2 * d

    def conv_silu(col0):
        cols = slice(col0, col0 + pair_w)
        shifted = _dot(shift_mat, xs_ref[:, cols])
        acc = convw_ref[CONV_W - 1:CONV_W, cols] * xs_ref[HALO:HALO + blk, cols].astype(F32)
        for s in range(1, CONV_W):
            acc = acc + convw_ref[CONV_W - 1 - s:CONV_W - s, cols] * shifted[(s - 1) * blk:s * blk]
        return _silu(acc)

    def front_pair(hp):
        c0 = hp * pair_w
        qf = conv_silu(c0)
        kf = conv_silu(nh * d + c0)
        v_s[slot_w, :, c0:c0 + pair_w] = conv_silu(2 * nh * d + c0)
        for half in range(2):
            lo, hi = half * d, (half + 1) * d
            qh, kh = qf[:, lo:hi], kf[:, lo:hi]
            q_s[slot_w, :, c0 + lo:c0 + hi] = qh * (lax.rsqrt(jnp.sum(qh * qh, axis=-1, keepdims=True) + EPS)
                                                    * (d ** -0.5))
            k_s[slot_w, :, c0 + lo:c0 + hi] = kh * lax.rsqrt(jnp.sum(kh * kh, axis=-1, keepdims=True) + EPS)
        z_s[slot_w, :, c0:c0 + pair_w] = proj_ref[:, qkv_w + c0:qkv_w + c0 + pair_w]

    front_tasks = [front_gates] + [functools.partial(front_pair, hp) for hp in range(nh // 2)]

    def run_front_task():
        if front_tasks:
            front_tasks.pop(0)()

    assert blk == 2 * c and 2 * c == LANES and d == LANES
    si = lax.broadcasted_iota(jnp.int32, (c, 2 * c), 0)
    sl = lax.broadcasted_iota(jnp.int32, (c, 2 * c), 1)
    first_chunk = sl < c
    sj = jnp.where(first_chunk, sl, sl - c)
    lower_incl = si >= sj
    strict = si > sj
    eye_pair = jnp.where(si == sj, 1.0, 0.0).astype(F32)
    lane_row = lax.broadcasted_iota(jnp.int32, (1, 2 * c), 1) < c
    zeros_cd = jnp.zeros((c, d), BF16)

    def block_diag(m):
        return jnp.concatenate([jnp.where(first_chunk, m, 0.0), jnp.where(first_chunk, 0.0, m)], axis=0).astype(BF16)

    st = []
    for h in range(nh):
        hs = slice(h * d, (h + 1) * d)
        q = q_s[slot_r, :, hs]
        k = k_s[slot_r, :, hs]
        v = v_s[slot_r, :, hs]
        g_col = gc[:, nh + h:nh + h + 1]
        g_row = gc_t[nh + h:nh + h + 1, :]
        b_col = beta[:, h:h + 1]
        g_col_pair = jnp.where(first_chunk, g_col[0:c], g_col[c:2 * c])
        g_last = jnp.where(lane_row, g_col[c - 1:c], g_col[2 * c - 1:2 * c])
        decay = jnp.where(lower_incl, jnp.exp2(jnp.where(lower_incl, g_col_pair - g_row, 0.0)), 0.0)
        k_beta = k * b_col
        e_col = jnp.exp2(g_col)
        kb, qb, kbf = k_beta.astype(BF16), q.astype(BF16), k.astype(BF16)
        lhs = jnp.concatenate([jnp.concatenate([kb[0:c], kb[c:2 * c]], axis=1),
                               jnp.concatenate([qb[0:c], qb[c:2 * c]], axis=1),
                               jnp.concatenate([eye_d, eye_d], axis=1)], axis=0)
        k_diag = jnp.concatenate([jnp.concatenate([kbf[0:c], zeros_cd], axis=1),
                                  jnp.concatenate([zeros_cd, kbf[c:2 * c]], axis=1)], axis=0)
        kk = _dot_nt(lhs, k_diag)
        vb, kbe = (v * b_col).astype(BF16), (k_beta * e_col).astype(BF16)
        zeros_2 = jnp.zeros((c, 2 * d), BF16)
        st.append(dict(
            a=jnp.where(strict, kk[0:c] * decay, 0.0),
            attn=kk[c:2 * c] * decay,
            k_tail_t=kk[2 * c:2 * c + d] * jnp.exp2(g_last - g_row),
            rhs=jnp.concatenate([jnp.concatenate([vb[0:c], kbe[0:c], zeros_2], axis=1),
                                 jnp.concatenate([zeros_2, vb[c:2 * c], kbe[c:2 * c]], axis=1)], axis=0),
            qe=(q * e_col).astype(BF16),
            e_last=[jnp.exp2(g_col[c - 1:c]), jnp.exp2(g_col[2 * c - 1:2 * c])]))
    run_front_task()

    for cur in st:
        x = -cur["a"]
        cur["y"] = _dot(x.astype(BF16), block_diag(x))
        cur["p"] = eye_pair + x
    run_front_task()
    n_levels = int(np.log2(c))
    for lvl in range(1, n_levels):
        for cur in st:
            y_bd = block_diag(cur["y"])
            p = cur["p"]
            if lvl + 1 < n_levels:
                zz = _dot(jnp.concatenate([cur["y"].astype(BF16), p.astype(BF16)], axis=0), y_bd)
                cur["y"] = zz[0:c]
                cur["p"] = p + zz[c:2 * c]
            else:
                cur["p"] = p + _dot(p.astype(BF16), y_bd)
        run_front_task()
    for cur in st:
        cur["uw"] = _dot(cur["p"].astype(BF16), cur["rhs"])
    run_front_task()

    for ck in range(2):
        r = ck * c
        in_chunk = first_chunk if ck == 0 else jnp.logical_not(first_chunk)
        in_chunk_d = lane_row if ck == 0 else jnp.logical_not(lane_row)
        s_old = [state_ref[h] for h in range(nh)]
        ws_qs = []
        for h in range(nh):
            cur = st[h]
            w = cur["uw"][:, (2 * ck + 1) * d:(2 * ck + 2) * d]
            lhs = jnp.concatenate([w.astype(BF16), cur["qe"][r:r + c]], axis=0)
            ws_qs.append(_dot(lhs, s_old[h].astype(BF16)))
        run_front_task()
        for h in range(nh):
            cur = st[h]
            v_new = (cur["uw"][:, 2 * ck * d:(2 * ck + 1) * d] - ws_qs[h][0:c]).astype(BF16)
            lhs = jnp.concatenate([jnp.where(in_chunk, cur["attn"], 0.0).astype(BF16),
                                   jnp.where(in_chunk_d, cur["k_tail_t"], 0.0).astype(BF16)], axis=0)
            rhs = jnp.concatenate([v_new, zeros_cd] if ck == 0 else [zeros_cd, v_new], axis=0)
            av_kv = _dot(lhs, rhs)
            state_ref[h] = s_old[h] * cur["e_last"][ck] + av_kv[c:c + d]
            o = ws_qs[h][c:2 * c] + av_kv[0:c]
            o = (o * lax.rsqrt(jnp.mean(o * o, axis=-1, keepdims=True) + EPS)) * onorm
            z = z_s[slot_r, r:r + c, h * d:(h + 1) * d].astype(F32)
            o_ref[r:r + c, h * d:(h + 1) * d] = (o * _silu(z)).astype(o_ref.dtype)
    while front_tasks:
        run_front_task()

    xs_ref[0:HALO, :] = xs_ref[blk:blk + HALO, :]


def _gdn_front_gates(gates_ref, hp_ref, gc_s, gct_s, beta_s, slot_w):
    blk, c = GDN_BLOCK, CHUNK
    gates = gates_ref[...]
    a_log = hp_ref[0:1, :]
    dt_bias = hp_ref[1:2, :]
    beta = 1.0 / (1.0 + jnp.exp(-gates))
    sp_in = gates + dt_bias
    softplus = jnp.maximum(sp_in, 0.0) + jnp.log(1.0 + jnp.exp(-jnp.abs(sp_in)))
    g = (-jnp.exp(a_log) * softplus) * float(np.log2(np.e))

    row = lax.broadcasted_iota(jnp.int32, (blk, blk), 0)
    col = lax.broadcasted_iota(jnp.int32, (blk, blk), 1)
    tri = jnp.where((row >= col) & ((row // c) == (col // c)), 1.0, 0.0).astype(BF16)
    g_hi = g.astype(BF16)
    g_r1 = g - g_hi.astype(F32)
    g_mid = g_r1.astype(BF16)
    g_lo = (g_r1 - g_mid.astype(F32)).astype(BF16)
    gc = _dot(tri, g_hi) + _dot(tri, g_mid) + _dot(tri, g_lo)
    gc_s[slot_w] = gc
    gct_s[slot_w] = gc.T
    beta_s[slot_w] = beta


def gdn_core(proj, gates, conv_w, a_log, dt_bias, out_norm, batch, seq):
    t = proj.shape[0]
    nh, d = LA_HEADS, LA_D
    blk = GDN_BLOCK
    assert seq % blk == 0
    nblk = seq // blk
    hp = jnp.zeros((8, LANES), F32)
    hp = hp.at[0, nh:2 * nh].set(a_log.astype(F32)).at[1, nh:2 * nh].set(dt_bias.astype(F32))

    def in_map(b, n):
        return (b * nblk + jnp.minimum(n, nblk - 1), 0)

    return pl.pallas_call(
        _gdn_kernel,
        out_shape=jax.ShapeDtypeStruct((t, nh * d), BF16),
        grid=(batch, nblk + 1),
        in_specs=[pl.BlockSpec((blk, 4 * nh * d), in_map),
                  pl.BlockSpec((blk, LANES), in_map),
                  pl.BlockSpec((CONV_W, 3 * nh * d), lambda b, n: (0, 0)),
                  pl.BlockSpec((8, LANES), lambda b, n: (0, 0)),
                  pl.BlockSpec((1, d), lambda b, n: (0, 0))],
        out_specs=pl.BlockSpec((blk, nh * d), lambda b, n: (b * nblk + jnp.maximum(n - 1, 0), 0)),
        scratch_shapes=[pltpu.VMEM((HALO + blk, 3 * nh * d), BF16),
                        pltpu.VMEM((nh, d, d), F32),
                        pltpu.VMEM((2, blk, nh * d), F32), pltpu.VMEM((2, blk, nh * d), F32),
                        pltpu.VMEM((2, blk, nh * d), F32), pltpu.VMEM((2, blk, nh * d), BF16),
                        pltpu.VMEM((2, blk, LANES), F32), pltpu.VMEM((2, LANES, blk), F32),
                        pltpu.VMEM((2, blk, LANES), F32)],
        compiler_params=_cparams(("arbitrary", "arbitrary")),
        name="gdn_core",
    )(proj, gates, conv_w.astype(F32), hp, out_norm.reshape(1, d).astype(F32))


TILE_FULL, TILE_FIRST, TILE_HALF = 0, 1, 2


def _swiglu_kernel(te_ref, mode_ref, nv_ref, x_ref, g_ref, a_ref, wp_ref, wg_hbm, wu_hbm, wd_hbm, o_ref,
                   wg_c, wu_c, wd_c, stage_in, stage_out, sems, xres_s, *, pre_norm, pre_proj, routed, tf):
    i = pl.program_id(0)
    nf = wg_c.shape[0]
    valid = i < nv_ref[0]

    def chunk_copies(j, slot, tile=None):
        e = te_ref[i if tile is None else tile]
        cols = pl.ds(j * tf, tf)
        return (pltpu.make_async_copy(wg_hbm.at[e, :, cols], stage_in.at[slot, 0], sems.at[slot, 0]),
                pltpu.make_async_copy(wu_hbm.at[e, :, cols], stage_in.at[slot, 1], sems.at[slot, 1]),
                pltpu.make_async_copy(wd_hbm.at[e, cols, :], stage_out.at[slot], sems.at[slot, 2]))

    if routed:
        nxt = jnp.minimum(i + 1, pl.num_programs(0) - 1)
        prefetch_next = (valid & (mode_ref[i] != TILE_FIRST) & (i + 1 < nv_ref[0]) & (mode_ref[nxt] == TILE_FIRST))
        prv = jnp.maximum(i - 1, 0)
        chunk0_requested = (i > 0) & (mode_ref[prv] != TILE_FIRST)

        @pl.when(prefetch_next)
        def _():
            for c in chunk_copies(0, 0, tile=nxt):
                c.start()
    else:
        chunk0_requested = False

    tile_rows = x_ref.shape[0]

    d_model = o_ref.shape[1]

    def prepare_rows(rows):
        x = x_ref[0:rows, 0:d_model].astype(F32)
        if pre_proj:
            x = x + _dot(a_ref[0:rows, :], wp_ref[...])
            xres_s[0:rows, :] = x
        if pre_norm:
            ms = jnp.mean(x * x, axis=-1, keepdims=True)
            x = (x * lax.rsqrt(ms + EPS)) * g_ref[...]
        return x.astype(BF16)

    def chunk(xb, j):
        hid = _silu(_dot(xb, wg_c[j]), base2=False) * _dot(xb, wu_c[j])
        return _dot(hid.astype(BF16), wd_c[j])

    def finish(acc, rows):
        if pre_norm:
            res = xres_s[0:rows, :] if pre_proj else x_ref[0:rows, 0:d_model]
            acc = res + acc
        o_ref[0:rows, :] = acc.astype(o_ref.dtype)
        if rows < tile_rows:
            o_ref[rows:tile_rows, :] = jnp.zeros((tile_rows - rows, o_ref.shape[1]), o_ref.dtype)

    mode = mode_ref[i]

    @pl.when(valid & (mode == TILE_FIRST) & jnp.logical_not(chunk0_requested))
    def _():
        for c in chunk_copies(0, 0):
            c.start()


    @pl.when(valid & (mode == TILE_FIRST))
    def _():
        xb = prepare_rows(tile_rows)
        acc = None
        for j in range(nf):
            slot = j % 2
            if j + 1 < nf:
                for c in chunk_copies(j + 1, 1 - slot):
                    c.start()
            for c in chunk_copies(j, slot):
                c.wait()
            wg_c[j] = stage_in[slot, 0].astype(BF16)
            wu_c[j] = stage_in[slot, 1].astype(BF16)
            wd_c[j] = stage_out[slot].astype(BF16)
            y = chunk(xb, j)
            acc = y if acc is None else acc + y
        finish(acc, tile_rows)

    def steady(rows):
        xb = prepare_rows(rows)
        acc = None
        for j in range(nf):
            y = chunk(xb, j)
            acc = y if acc is None else acc + y
        finish(acc, rows)

    @pl.when(valid & (mode == TILE_FULL))
    def _():
        steady(tile_rows)

    if routed:
        @pl.when(valid & (mode == TILE_HALF))
        def _():
            steady(tile_rows // 2)

    @pl.when(jnp.logical_not(valid))
    def _():
        o_ref[...] = jnp.zeros_like(o_ref)


def expert_swiglu(x, gain, tile_expert, tile_mode, n_valid, wg, wu, wd, tile_rows, tf, out_dtype, pre_norm, name,
                  proj=None, routed=False):
    n_rows = x.shape[0]
    ne, d, f = wg.shape
    assert n_rows % tile_rows == 0 and f % tf == 0
    n_tiles = n_rows // tile_rows
    nf = f // tf
    pre_proj = proj is not None
    if pre_proj:
        a, wp = proj
        a_spec = pl.BlockSpec((tile_rows, a.shape[1]), lambda i, te, fi, nv: (jnp.minimum(i, nv[0] - 1), 0))
    else:
        a, wp = jnp.zeros((8, LANES), BF16), jnp.zeros((LANES, d), BF16)
        a_spec = pl.BlockSpec(a.shape, lambda i, te, fi, nv: (0, 0))
    grid_spec = pltpu.PrefetchScalarGridSpec(
        num_scalar_prefetch=3,
        grid=(n_tiles,),
        in_specs=[pl.BlockSpec((tile_rows, x.shape[1]), lambda i, te, fi, nv: (jnp.minimum(i, nv[0] - 1), 0)),
                  pl.BlockSpec((1, d), lambda i, te, fi, nv: (0, 0)),
                  a_spec,
                  pl.BlockSpec(wp.shape, lambda i, te, fi, nv: (0, 0)),
                  pl.BlockSpec(memory_space=pl.ANY),
                  pl.BlockSpec(memory_space=pl.ANY),
                  pl.BlockSpec(memory_space=pl.ANY)],
        out_specs=pl.BlockSpec((tile_rows, d), lambda i, te, fi, nv: (i, 0)),
        scratch_shapes=[pltpu.VMEM((nf, d, tf), BF16), pltpu.VMEM((nf, d, tf), BF16), pltpu.VMEM((nf, tf, d), BF16),
                        pltpu.VMEM((2, 2, d, tf), F32), pltpu.VMEM((2, tf, d), F32),
                        pltpu.SemaphoreType.DMA((2, 3)),
                        pltpu.VMEM((tile_rows, d) if pre_proj else (8, LANES), F32)],
    )
    return pl.pallas_call(
        functools.partial(_swiglu_kernel, pre_norm=pre_norm, pre_proj=pre_proj, routed=routed, tf=tf),
        out_shape=jax.ShapeDtypeStruct((n_rows, d), out_dtype),
        grid_spec=grid_spec,
        compiler_params=pltpu.CompilerParams(dimension_semantics=("arbitrary",), vmem_limit_bytes=EXPERT_VMEM_LIMIT),
        name=name,
    )(tile_expert, tile_mode, n_valid, x, gain.reshape(1, d).astype(F32), a, wp, wg, wu, wd)


def ffn_dense(x, gain, wg, wu, wd, tm, tf, proj=None):
    t = x.shape[0]
    n_tiles = t // tm
    tile_mode = jnp.full((n_tiles,), TILE_FULL, jnp.int32).at[0].set(TILE_FIRST)
    return expert_swiglu(x, gain, jnp.zeros((n_tiles,), jnp.int32), tile_mode, jnp.full((1,), n_tiles, jnp.int32),
                         wg[None], wu[None], wd[None], tm, tf, F32, True, "ffn_dense", proj=proj)


def _t5_bucket_np(dist):
    max_exact = N_BUCKETS // 2
    n = np.maximum(dist, 0)
    safe = np.maximum(n, 1).astype(np.float32)
    large = max_exact + (np.log(safe / max_exact) / np.log(MAX_DIST / max_exact)
                         * (N_BUCKETS - max_exact)).astype(np.int32)
    large = np.minimum(large, N_BUCKETS - 1)
    return np.where(n < max_exact, n, large).astype(np.int32)


def _bias_kernel(bucket_ref, valid_ref, rb_ref, o_ref):
    bucket = bucket_ref[...]
    for h in range(SW_HEADS):
        acc = jnp.zeros(bucket.shape, F32)
        for b in range(N_BUCKETS):
            acc = jnp.where(bucket == b, rb_ref[b, h], acc)
        for v in range(valid_ref.shape[0]):
            o_ref[v, h] = jnp.where(valid_ref[v] > 0, acc * LOG2E, NEG_INF)


def bias_table(rel_bias):
    qi = np.arange(WINDOW)[:, None] + WINDOW
    kj = np.arange(2 * WINDOW)[None, :]
    dist = qi - kj
    band = (dist >= 0) & (dist < WINDOW)
    valid = np.stack([band, band & (kj >= WINDOW)]).astype(np.int32)
    return pl.pallas_call(
        _bias_kernel,
        out_shape=jax.ShapeDtypeStruct((2, SW_HEADS, WINDOW, 2 * WINDOW), F32),
        in_specs=[pl.BlockSpec(memory_space=pltpu.VMEM), pl.BlockSpec(memory_space=pltpu.VMEM),
                  pl.BlockSpec(memory_space=pltpu.SMEM)],
        out_specs=pl.BlockSpec(memory_space=pltpu.VMEM),
        name="t5_bias_table",
    )(jnp.asarray(_t5_bucket_np(dist)), jnp.asarray(valid), rel_bias.astype(F32))


def _swa_kernel(q_ref, kvp_ref, kvc_ref, bias_ref, qn_ref, kn_ref, sink_ref, o_ref):
    blk, hd = WINDOW, SW_HD
    kv_w = SW_KV_HEADS * hd
    first = jnp.where(pl.program_id(1) == 0, 1, 0)
    gw = 2 * LANES
    gi = lax.broadcasted_iota(jnp.int32, (gw, gw), 0)
    gj = lax.broadcasted_iota(jnp.int32, (gw, gw), 1)
    group_ones = jnp.where((gi // hd) == (gj // hd), 1.0, 0.0).astype(BF16)
    lane = lax.broadcasted_iota(jnp.int32, (1, LANES), 1)
    low_half = lane < hd

    def head_norm(x, gain):
        cols = []
        for c0 in range(0, x.shape[1], gw):
            xc = x[:, c0:c0 + gw]
            ss = _dot((xc * xc).astype(BF16), group_ones)
            cols.append(xc * lax.rsqrt(ss * (1.0 / hd) + EPS))
        return jnp.concatenate(cols, axis=1) * gain

    def dup_half(x, half):
        swapped = pltpu.roll(x, hd, 1)
        return jnp.where(low_half == (half == 0), x, swapped)

    n_qb = q_ref.shape[0] // blk
    qn = head_norm(q_ref[...].astype(F32), qn_ref[...]) * ((hd ** -0.5) * LOG2E)
    half_sel = [jnp.where(low_half, 1.0, 0.0), jnp.where(low_half, 0.0, 1.0)]
    k_all = jnp.concatenate([kvp_ref[:, 0:kv_w], kvc_ref[:, 0:kv_w]], axis=0).astype(F32)
    kn = head_norm(k_all, kn_ref[...])
    v_all = jnp.concatenate([kvp_ref[:, kv_w:2 * kv_w], kvc_ref[:, kv_w:2 * kv_w]], axis=0).astype(F32)
    ks, vs = [], []
    for g in range(SW_KV_HEADS):
        c0 = (g // 2) * LANES
        ks.append(dup_half(kn[:, c0:c0 + LANES], g % 2).astype(BF16))
        vs.append(dup_half(v_all[:, c0:c0 + LANES], g % 2).astype(BF16))

    pairs = [(j, hq) for j in range(n_qb) for hq in range(SW_HEADS)]
    for g0 in range(0, len(pairs), SWA_GROUP):
        group = pairs[g0:g0 + SWA_GROUP]
        scores = {}
        for (j, hq) in group:
            c0 = (hq // 2) * LANES
            q_h = (qn[j * blk:(j + 1) * blk, c0:c0 + LANES] * half_sel[hq % 2]).astype(BF16)
            scores[(j, hq)] = _dot_nt(q_h, ks[hq // SW_GROUP][j * blk:(j + 2) * blk])
        probs = {}
        for (j, hq) in group:
            variant = first if j == 0 else 0
            s = scores[(j, hq)] + bias_ref[variant, hq]
            sink = sink_ref[hq] * LOG2E
            mx = jnp.maximum(jnp.max(s, axis=-1, keepdims=True), sink)
            p = jnp.exp2(s - mx)
            denom = jnp.sum(p, axis=-1, keepdims=True) + jnp.exp2(sink - mx)
            probs[(j, hq)] = (p * (1.0 / denom)).astype(BF16)
        outs = {key: _dot(probs[key], vs[key[1] // SW_GROUP][key[0] * blk:(key[0] + 2) * blk]) for key in group}
        for (j, hq) in group[0::2]:
            c = hq // 2
            o_ref[j * blk:(j + 1) * blk, c * LANES:(c + 1) * LANES] = jnp.where(
                low_half, outs[(j, hq)], outs[(j, hq + 1)]).astype(o_ref.dtype)


def swa_attention(q, kv, bias, q_norm, k_norm, sinks, batch, seq):
    t = q.shape[0]
    blk = WINDOW
    step = SWA_QBLOCKS * blk
    assert seq % step == 0
    nb = seq // step
    qw = SW_HEADS * SW_HD
    kvw = 2 * SW_KV_HEADS * SW_HD
    return pl.pallas_call(
        _swa_kernel,
        out_shape=jax.ShapeDtypeStruct((t, qw), BF16),
        grid=(batch, nb),
        in_specs=[pl.BlockSpec((step, qw), lambda b, n: (b * nb + n, 0)),
                  pl.BlockSpec((blk, kvw), lambda b, n: (jnp.maximum((b * nb + n) * SWA_QBLOCKS - 1, b * nb * SWA_QBLOCKS), 0)),
                  pl.BlockSpec((step, kvw), lambda b, n: (b * nb + n, 0)),
                  pl.BlockSpec((2, SW_HEADS, blk, 2 * blk), lambda b, n: (0, 0, 0, 0)),
                  pl.BlockSpec((1, qw), lambda b, n: (0, 0)),
                  pl.BlockSpec((1, kvw // 2), lambda b, n: (0, 0)),
                  pl.BlockSpec(memory_space=pltpu.SMEM)],
        out_specs=pl.BlockSpec((step, qw), lambda b, n: (b * nb + n, 0)),
        compiler_params=_cparams(("parallel", "parallel")),
        name="swa_attention",
    )(q, kv, kv, bias, jnp.tile(q_norm.astype(F32), SW_HEADS).reshape(1, qw),
      jnp.tile(k_norm.astype(F32), SW_KV_HEADS).reshape(1, kvw // 2), sinks.astype(F32))


def _route_kernel(x_ref, a_ref, wp_ref, g_ref, wr_ref, h_ref, r_ref, wt_ref, tab_ref, cnt_ref,
                  sel_s, gw_s, cnt_s, start_s, run_s, *, tile_rows):
    ne = N_EXPERTS
    p = pl.program_id(0)
    i = pl.program_id(1)
    tm = x_ref.shape[0]
    sub = lax.broadcasted_iota(jnp.int32, (ne, tm), 0).astype(F32)

    def seg_rows(sel):
        n = jnp.sum(sel, axis=1, keepdims=True)
        return jnp.floor((n + (SEG_ALIGN - 1)) * (1.0 / SEG_ALIGN)) * SEG_ALIGN

    def excl_cumsum_experts(v):
        sub8 = lax.broadcasted_iota(jnp.int32, v.shape, 0)
        out = jnp.zeros_like(v)
        for e in range(ne - 1):
            out = out + jnp.where(sub8 > e, v[e:e + 1, :], 0.0)
        return out

    @pl.when(p == 0)
    def _():
        @pl.when(i == 0)
        def _():
            cnt_s[...] = jnp.zeros_like(cnt_s)

        x = x_ref[...] + _dot(a_ref[...], wp_ref[...])
        h_ref[...] = x
        ms = jnp.mean(x * x, axis=-1, keepdims=True)
        xn32 = (x * lax.rsqrt(ms + EPS)) * g_ref[...]
        xn_hi = xn32.astype(BF16)
        xn_lo = (xn32 - xn_hi.astype(F32)).astype(BF16)
        p_hi = _dot_nt(wr_ref[...], xn_hi)
        p_lo = _dot_nt(wr_ref[...], xn_lo)
        logits = p_hi[0:ne] + p_hi[ne:2 * ne] + p_lo[0:ne]
        m1 = jnp.max(logits, axis=0, keepdims=True)
        i1 = jnp.min(jnp.where(logits == m1, sub, float(ne)), axis=0, keepdims=True)
        l2 = jnp.where(sub == i1, -jnp.inf, logits)
        m2 = jnp.max(l2, axis=0, keepdims=True)
        i2 = jnp.min(jnp.where(l2 == m2, sub, float(ne)), axis=0, keepdims=True)
        e2 = jnp.exp(m2 - m1)
        w1 = 1.0 / (1.0 + e2)
        w2 = e2 / (1.0 + e2)
        sel = jnp.where((sub == i1) | (sub == i2), 1.0, 0.0)
        sel_s[i] = sel
        gw_s[i] = jnp.where(sub == i1, w1, jnp.where(sub == i2, w2, 0.0))
        cnt_s[...] += seg_rows(sel)

    @pl.when(p == 1)
    def _():
        @pl.when(i == 0)
        def _():
            cnt = cnt_s[...]
            padded = jnp.floor((cnt + (tile_rows - 1)) * (1.0 / tile_rows)) * tile_rows
            start_s[...] = excl_cumsum_experts(padded)
            run_s[...] = jnp.zeros_like(run_s)
            cnt_ref[...] = cnt

        sel = sel_s[i]
        gw = gw_s[i]
        ti = lax.broadcasted_iota(jnp.int32, (tm, tm), 0)
        tj = lax.broadcasted_iota(jnp.int32, (tm, tm), 1)
        tri = jnp.where(ti <= tj, 1.0, 0.0).astype(BF16)
        csum = _dot(sel.astype(BF16), tri)
        seg = jnp.broadcast_to(seg_rows(sel), run_s.shape)
        local0 = excl_cumsum_experts(seg)
        tab_ref[0, 0] = start_s[...] + run_s[...]
        tab_ref[0, 1] = seg
        tab_ref[0, 2] = local0
        run_s[...] += seg
        local_row = local0[:, 0:1] + csum - sel
        ia = jnp.min(jnp.where(sel > 0.0, sub, float(ne)), axis=0, keepdims=True)
        ib = jnp.max(jnp.where(sel > 0.0, sub, -1.0), axis=0, keepdims=True)
        pick_a = sub == ia
        pick_b = sub == ib
        rows = [jnp.sum(jnp.where(pick_a, local_row, 0.0), axis=0, keepdims=True),
                jnp.sum(jnp.where(pick_b, local_row, 0.0), axis=0, keepdims=True),
                jnp.sum(jnp.where(pick_a, gw, 0.0), axis=0, keepdims=True),
                jnp.sum(jnp.where(pick_b, gw, 0.0), axis=0, keepdims=True)]
        r_ref[...] = jnp.concatenate(rows + [jnp.zeros((ne - 4, tm), F32)], axis=0)
        wpad = jnp.concatenate(rows[2:4] + rows[0:2] + [jnp.zeros((LANES - 4, tm), F32)], axis=0)
        wt_ref[...] = wpad.T


def moe_route(x, a, wp, gain, w_router, tm, tile_rows):
    t, d = x.shape
    ne = w_router.shape[1]
    assert ne == N_EXPERTS
    w_hi = w_router.astype(BF16)
    w_lo = (w_router - w_hi.astype(F32)).astype(BF16)
    wr = jnp.concatenate([w_hi.T, w_lo.T], axis=0)
    tm = min(tm, t)
    nt = t // tm

    def row_map(p, i):
        return (i * (1 - p) + (nt - 1) * p, 0)

    return pl.pallas_call(
        functools.partial(_route_kernel, tile_rows=tile_rows),
        out_shape=(jax.ShapeDtypeStruct((t, d), F32),
                   jax.ShapeDtypeStruct((ne, t), F32), jax.ShapeDtypeStruct((t, LANES), F32),
                   jax.ShapeDtypeStruct((nt, 3, ne, LANES), F32), jax.ShapeDtypeStruct((ne, LANES), F32)),
        grid=(2, nt),
        in_specs=[pl.BlockSpec((tm, d), row_map),
                  pl.BlockSpec((tm, a.shape[1]), row_map),
                  pl.BlockSpec(wp.shape, lambda p, i: (0, 0)),
                  pl.BlockSpec((1, d), lambda p, i: (0, 0)),
                  pl.BlockSpec((2 * ne, d), lambda p, i: (0, 0))],
        out_specs=(pl.BlockSpec((tm, d), row_map),
                   pl.BlockSpec((ne, tm), lambda p, i: (0, i * p)),
                   pl.BlockSpec((tm, LANES), lambda p, i: (i * p, 0)),
                   pl.BlockSpec((1, 3, ne, LANES), lambda p, i: (i * p, 0, 0, 0)),
                   pl.BlockSpec((ne, LANES), lambda p, i: (0, 0))),
        scratch_shapes=[pltpu.VMEM((nt, ne, tm), F32), pltpu.VMEM((nt, ne, tm), F32),
                        pltpu.VMEM((ne, LANES), F32), pltpu.VMEM((ne, LANES), F32), pltpu.VMEM((ne, LANES), F32)],
        compiler_params=_cparams(("arbitrary", "arbitrary")),
        name="moe_route",
    )(x, a, wp, gain.reshape(1, d), wr)


def _segment_copies(tab_ref, i, e, local_ref, slot_ref, sem, to_slots):
    base = (i * N_EXPERTS + e) * 3
    slot0, rows, local0 = tab_ref[base], tab_ref[base + 1], tab_ref[base + 2]
    out = []
    done = 0
    size = MOE_TOKEN_TILE
    while size >= SEG_ALIGN:
        take = rows & size
        loc = local_ref.at[pl.ds(pl.multiple_of(local0 + done, SEG_ALIGN), size)]
        slt = slot_ref.at[pl.ds(pl.multiple_of(slot0 + done, SEG_ALIGN), size)]
        desc = pltpu.make_async_copy(loc, slt, sem) if to_slots else pltpu.make_async_copy(slt, loc, sem)
        out.append((take != 0, desc))
        done = done + take
        size //= 2
    return out


def _run_segment_copies(tab_ref, tile, slot, rows_s, slot_ref, sems, to_slots, action):
    for e in range(N_EXPERTS):
        for cond, desc in _segment_copies(tab_ref, tile, e, rows_s.at[slot], slot_ref, sems.at[slot], to_slots):
            @pl.when(cond)
            def _():
                getattr(desc, action)()


def _dispatch_kernel(tab_ref, zf_ref, x_ref, g_ref, r_ref, xs_ref, rows_s, zero_s, sem, zsem, *, tile_rows):
    i = pl.program_id(0)
    tm = x_ref.shape[0]
    n_local = rows_s.shape[1]

    @pl.when(i == 0)
    def _():
        zero_s[...] = jnp.zeros_like(zero_s)

        def zero_copy(e):
            row0 = pl.multiple_of(zf_ref[e], tile_rows)
            return pltpu.make_async_copy(zero_s, xs_ref.at[pl.ds(row0, tile_rows)], zsem)

        for e in range(zf_ref.shape[0]):
            @pl.when(zf_ref[e] >= 0)
            def _():
                zero_copy(e).start()
        for e in range(zf_ref.shape[0]):
            @pl.when(zf_ref[e] >= 0)
            def _():
                zero_copy(e).wait()

    x = x_ref[...]
    ms = jnp.mean(x * x, axis=-1, keepdims=True)
    xn = ((x * lax.rsqrt(ms + EPS)) * g_ref[...]).astype(BF16)
    row_id = lax.broadcasted_iota(jnp.int32, (n_local, tm), 0).astype(F32)
    onehot = jnp.where((row_id == r_ref[0:1, :]) | (row_id == r_ref[1:2, :]), 1.0, 0.0).astype(BF16)
    slot = lax.rem(i, 2)
    rows_s[slot] = _dot(onehot, xn)

    _run_segment_copies(tab_ref, i, slot, rows_s, xs_ref, sem, True, "start")

    @pl.when(i > 0)
    def _():
        _run_segment_copies(tab_ref, i - 1, 1 - slot, rows_s, xs_ref, sem, True, "wait")

    @pl.when(i == pl.num_programs(0) - 1)
    def _():
        _run_segment_copies(tab_ref, i, slot, rows_s, xs_ref, sem, True, "wait")


def moe_dispatch(x, gain, r, tab, zf_rows, n_slots, tm, tile_rows):
    t, d = x.shape
    nt = t // tm
    n_local = TOP_K * tm + N_EXPERTS * SEG_ALIGN
    grid_spec = pltpu.PrefetchScalarGridSpec(
        num_scalar_prefetch=2,
        grid=(nt,),
        in_specs=[pl.BlockSpec((tm, d), lambda i, tb, zf: (i, 0)),
                  pl.BlockSpec((1, d), lambda i, tb, zf: (0, 0)),
                  pl.BlockSpec((N_EXPERTS, tm), lambda i, tb, zf: (0, i))],
        out_specs=pl.BlockSpec(memory_space=pl.ANY),
        scratch_shapes=[pltpu.VMEM((2, n_local, d), F32), pltpu.VMEM((tile_rows, d), F32),
                        pltpu.SemaphoreType.DMA((2,)), pltpu.SemaphoreType.DMA],
    )
    return pl.pallas_call(
        functools.partial(_dispatch_kernel, tile_rows=tile_rows),
        out_shape=jax.ShapeDtypeStruct((n_slots, d), F32),
        grid_spec=grid_spec,
        compiler_params=_cparams(("arbitrary",)),
        name="moe_dispatch",
    )(tab, zf_rows, x, gain.reshape(1, d), r)


def _combine_kernel(tab_ref, h_ref, wt_ref, ys_ref, o_ref, rows_s, sems):
    i = pl.program_id(0)
    tm = h_ref.shape[0]
    n_local = rows_s.shape[1]
    slot = lax.rem(i, 2)

    def fetch(tile, into):
        rows_s[into] = jnp.zeros(rows_s.shape[1:], rows_s.dtype)
        _run_segment_copies(tab_ref, tile, into, rows_s, ys_ref, sems, False, "start")

    @pl.when(i == 0)
    def _():
        fetch(i, slot)

    @pl.when(i + 1 < pl.num_programs(0))
    def _():
        fetch(i + 1, 1 - slot)

    _run_segment_copies(tab_ref, i, slot, rows_s, ys_ref, sems, False, "wait")

    wt = wt_ref[...]
    y = rows_s[slot].astype(BF16)
    col_id = lax.broadcasted_iota(jnp.int32, (tm, n_local), 1).astype(F32)
    pick_a = jnp.where(col_id == wt[:, 2:3], 1.0, 0.0).astype(BF16)
    pick_b = jnp.where(col_id == wt[:, 3:4], 1.0, 0.0).astype(BF16)
    o_ref[...] = h_ref[...] + wt[:, 0:1] * _dot(pick_a, y) + wt[:, 1:2] * _dot(pick_b, y)


def moe_combine(h, wt, tab, ys, tm):
    t, d = h.shape
    nt = t // tm
    n_local = TOP_K * tm + N_EXPERTS * SEG_ALIGN
    grid_spec = pltpu.PrefetchScalarGridSpec(
        num_scalar_prefetch=1,
        grid=(nt,),
        in_specs=[pl.BlockSpec((tm, d), lambda i, tb: (i, 0)),
                  pl.BlockSpec((tm, LANES), lambda i, tb: (i, 0)),
                  pl.BlockSpec(memory_space=pl.ANY)],
        out_specs=pl.BlockSpec((tm, d), lambda i, tb: (i, 0)),
        scratch_shapes=[pltpu.VMEM((2, n_local, d), F32), pltpu.SemaphoreType.DMA((2,))],
    )
    return pl.pallas_call(
        _combine_kernel,
        out_shape=jax.ShapeDtypeStruct((t, d), F32),
        grid_spec=grid_spec,
        compiler_params=_cparams(("arbitrary",)),
        name="moe_combine",
    )(tab, h, wt, ys)


def moe_layer(x, a, wp, gain, w_router, wg, wu, wd):
    t, d = x.shape
    ne = w_router.shape[1]
    tr, tm = MOE_TILE_ROWS, MOE_TOKEN_TILE
    assert t % tm == 0 and ne == N_EXPERTS
    nt = t // tm
    n_tiles = -(-(TOP_K * t + nt * ne * (SEG_ALIGN - 1) + ne * (tr - 1)) // tr)
    n_slots = n_tiles * tr

    h, r, wt, tab, cnt = moe_route(x, a, wp, gain, w_router, tm, tr)
    tab = jnp.transpose(tab[:, :, :, 0], (0, 2, 1)).astype(jnp.int32).reshape(-1)

    counts = cnt[:, 0].astype(jnp.int32)
    padded = ((counts + (tr - 1)) // tr) * tr
    ends = jnp.cumsum(padded)
    n_valid = (ends[-1] // tr).astype(jnp.int32)
    tile_row0 = jnp.arange(n_tiles, dtype=jnp.int32) * tr
    tile_expert = jnp.sum((tile_row0[:, None] >= ends[None, :]).astype(jnp.int32), axis=1)
    tile_expert = jnp.minimum(tile_expert, ne - 1)
    tile_expert = jnp.where(jnp.arange(n_tiles) < n_valid, tile_expert, tile_expert[jnp.maximum(n_valid - 1, 0)])
    prev_expert = jnp.concatenate([jnp.full((1,), -1, jnp.int32), tile_expert[:-1]])
    rows_used = (ends - padded + counts)[tile_expert] - tile_row0
    tile_mode = jnp.where(tile_expert != prev_expert, TILE_FIRST,
                          jnp.where(rows_used <= tr // 2, TILE_HALF, TILE_FULL)).astype(jnp.int32)
    tail = jnp.arange(TOP_K * t // tr, n_tiles, dtype=jnp.int32)
    zf_rows = jnp.concatenate([jnp.where(padded > 0, ends - tr, -1),
                               jnp.where(tail >= n_valid, tail * tr, -1)]).astype(jnp.int32)

    xs = moe_dispatch(h, gain, r, tab, zf_rows, n_slots, tm, tr)
    ys = expert_swiglu(xs, gain, tile_expert, tile_mode, n_valid.reshape(1), wg, wu, wd, tr, FFN_CHUNK, F32, False,
                       "moe_experts", routed=True)
    return moe_combine(h, wt, tab, ys, tm)


def kernel(x, a_norm, a_w_in, a_conv, a_log_decay, a_dt_bias, a_out_norm, a_w_out, kv_norm, kv_w, k_norm,
           b_norm, b_w_q, q_norm, b_sinks, b_w_o, rel_bias, ffn_norm, dense_w_gate, dense_w_up, dense_w_down,
           moe_router, moe_w_gate, moe_w_up, moe_w_down):
    batch, seq, d = x.shape
    t = batch * seq
    nh, hd = LA_HEADS, LA_D
    main_w = 4 * nh * hd
    h0 = x.reshape(t, d)

    w_in = a_w_in[0]
    w_main = w_in[:, 0:main_w].astype(BF16)
    w_gate = jnp.zeros((d, LANES), BF16).at[:, 0:2 * nh].set(w_in[:, main_w:main_w + 2 * nh].astype(BF16))
    proj, gates = norm_matmul(h0, [(a_norm[0], w_main, BF16), (a_norm[0], w_gate, F32)], IN_PROJ_TILE,
                              "gdn_in_proj")
    o = gdn_core(proj, gates, a_conv[0], a_log_decay[0], a_dt_bias[0], a_out_norm[0], batch, seq)

    h2 = ffn_dense(h0, ffn_norm[0], dense_w_gate[0], dense_w_up[0], dense_w_down[0], MOE_TILE_ROWS, FFN_CHUNK,
                   proj=(o, a_w_out[0].astype(BF16)))

    kv, q = norm_matmul(h2, [(kv_norm, kv_w.astype(BF16), BF16), (b_norm[0], b_w_q[0].astype(BF16), BF16)],
                        QKV_PROJ_TILE, "qkv_proj")
    bias = bias_table(rel_bias)
    attn = swa_attention(q, kv, bias, q_norm[0], k_norm, b_sinks[0], batch, seq)

    h4 = moe_layer(h2, attn, b_w_o[0].astype(BF16), ffn_norm[1], moe_router[0], moe_w_gate[0], moe_w_up[0],
                   moe_w_down[0])
    return h4.reshape(batch, seq, d)
```

```python
import functools

import numpy as np
import jax
import jax.numpy as jnp
from jax import lax
from jax.experimental import pallas as pl
from jax.experimental.pallas import tpu as pltpu

F32 = jnp.float32
BF16 = jnp.bfloat16

EPS = 1e-6
NEG_INF = -1e30

LA_HEADS = 8
LA_D = 128
CONV_W = 4
CHUNK = 64
SW_HEADS = 16
SW_KV_HEADS = 4
SW_GROUP = SW_HEADS // SW_KV_HEADS
SW_HD = 64
WINDOW = 128
SWA_QBLOCKS = 4
SWA_GROUP = 16
N_BUCKETS = 32
MAX_DIST = 128
N_EXPERTS = 8
TOP_K = 2
LOG2E = float(np.log2(np.e))

LANES = 128
SEG_ALIGN = 8
GDN_BLOCK = 2 * CHUNK
HALO = 16
MOE_TILE_ROWS = 512
MOE_TOKEN_TILE = 512
FFN_CHUNK = 512
IN_PROJ_TILE = 1024
QKV_PROJ_TILE = 1024

VMEM_LIMIT = 56 * 1024 * 1024
EXPERT_VMEM_LIMIT = 60 * 1024 * 1024


def _cparams(sem):
    return pltpu.CompilerParams(dimension_semantics=sem, vmem_limit_bytes=VMEM_LIMIT)


def _silu(x, base2=True):
    e = jnp.exp2(x * (-LOG2E)) if base2 else jnp.exp(-x)
    return x * (1.0 / (1.0 + e))


def _dot(a, b):
    return jnp.dot(a, b, preferred_element_type=F32)


def _dot_nt(a, b):
    return lax.dot_general(a, b, (((1,), (1,)), ((), ())), preferred_element_type=F32)


def _norm_matmul_kernel(*refs, n_groups):
    x_ref = refs[0]
    g_refs = refs[1:1 + n_groups]
    w_refs = refs[1 + n_groups:1 + 2 * n_groups]
    o_refs = refs[1 + 2 * n_groups:1 + 3 * n_groups]
    x = x_ref[...]
    xr = x * lax.rsqrt(jnp.mean(x * x, axis=-1, keepdims=True) + EPS)
    for g_ref, w_ref, o_ref in zip(g_refs, w_refs, o_refs):
        o_ref[...] = _dot((xr * g_ref[...]).astype(BF16), w_ref[...]).astype(o_ref.dtype)


def norm_matmul(x, groups, tm, name):
    t, d = x.shape
    tm = min(tm, t)
    assert t % tm == 0
    gains = [g.reshape(1, d).astype(F32) for g, _, _ in groups]
    ws = [w for _, w, _ in groups]
    return pl.pallas_call(
        functools.partial(_norm_matmul_kernel, n_groups=len(groups)),
        out_shape=[jax.ShapeDtypeStruct((t, w.shape[1]), dt) for _, w, dt in groups],
        grid=(t // tm,),
        in_specs=([pl.BlockSpec((tm, d), lambda i: (i, 0))]
                  + [pl.BlockSpec((1, d), lambda i: (0, 0)) for _ in groups]
                  + [pl.BlockSpec(w.shape, lambda i: (0, 0)) for w in ws]),
        out_specs=[pl.BlockSpec((tm, w.shape[1]), lambda i: (i, 0)) for w in ws],
        compiler_params=_cparams(("parallel",)),
        name=name,
    )(x, *gains, *ws)


def _gdn_kernel(proj_ref, gates_ref, convw_ref, hp_ref, onorm_ref, o_ref,
                xs_ref, state_ref, q_s, k_s, v_s, z_s, gc_s, gct_s, beta_s):
    n = pl.program_id(1)

    @pl.when(n == 0)
    def _():
        xs_ref[0:HALO, :] = jnp.zeros((HALO, xs_ref.shape[1]), xs_ref.dtype)
        for ref in (q_s, k_s, v_s, z_s, gc_s, gct_s, beta_s):
            ref[1] = jnp.zeros(ref.shape[1:], ref.dtype)

    @pl.when(n <= 1)
    def _():
        state_ref[...] = jnp.zeros_like(state_ref)

    args = (proj_ref, gates_ref, convw_ref, hp_ref, onorm_ref, o_ref, xs_ref, state_ref,
            q_s, k_s, v_s, z_s, gc_s, gct_s, beta_s)

    @pl.when(lax.rem(n, 2) == 0)
    def _():
        _gdn_step(*args, slot_w=0, slot_r=1)

    @pl.when(lax.rem(n, 2) == 1)
    def _():
        _gdn_step(*args, slot_w=1, slot_r=0)


def _gdn_step(proj_ref, gates_ref, convw_ref, hp_ref, onorm_ref, o_ref, xs_ref, state_ref,
              q_s, k_s, v_s, z_s, gc_s, gct_s, beta_s, *, slot_w, slot_r):
    nh, d, c = LA_HEADS, LA_D, CHUNK
    blk = GDN_BLOCK
    qkv_w = 3 * nh * d

    gc = gc_s[slot_r]
    gc_t = gct_s[slot_r]
    beta = beta_s[slot_r]

    xs_ref[HALO:HALO + blk, :] = proj_ref[:, 0:qkv_w]

    def front_gates():
        _gdn_front_gates(gates_ref, hp_ref, gc_s, gct_s, beta_s, slot_w)

    di = lax.broadcasted_iota(jnp.int32, (d, d), 0)
    dj = lax.broadcasted_iota(jnp.int32, (d, d), 1)
    eye_d = jnp.where(di == dj, 1.0, 0.0).astype(BF16)

    onorm = onorm_ref[...]

    n_shift = CONV_W - 1
    sr = lax.broadcasted_iota(jnp.int32, (n_shift * blk, HALO + blk), 0)
    sc = lax.broadcasted_iota(jnp.int32, (n_shift * blk, HALO + blk), 1)
    shift_mat = jnp.where(sc == HALO + (sr % blk) - (sr // blk + 1), 1.0, 0.0).astype(BF16)
    pair_w = 2 * d

    def conv_silu(col0):
        cols = slice(col0, col0 + pair_w)
        shifted = _dot(shift_mat, xs_ref[:, cols])
        acc = convw_ref[CONV_W - 1:CONV_W, cols] * xs_ref[HALO:HALO + blk, cols].astype(F32)
        for s in range(1, CONV_W):
            acc = acc + convw_ref[CONV_W - 1 - s:CONV_W - s, cols] * shifted[(s - 1) * blk:s * blk]
        return _silu(acc)

    def front_pair(hp):
        c0 = hp * pair_w
        qf = conv_silu(c0)
        kf = conv_silu(nh * d + c0)
        v_s[slot_w, :, c0:c0 + pair_w] = conv_silu(2 * nh * d + c0)
        for half in range(2):
            lo, hi = half * d, (half + 1) * d
            qh, kh = qf[:, lo:hi], kf[:, lo:hi]
            q_s[slot_w, :, c0 + lo:c0 + hi] = qh * (lax.rsqrt(jnp.sum(qh * qh, axis=-1, keepdims=True) + EPS)
                                                    * (d ** -0.5))
            k_s[slot_w, :, c0 + lo:c0 + hi] = kh * lax.rsqrt(jnp.sum(kh * kh, axis=-1, keepdims=True) + EPS)
        z_s[slot_w, :, c0:c0 + pair_w] = proj_ref[:, qkv_w + c0:qkv_w + c0 + pair_w]

    front_tasks = [front_gates] + [functools.partial(front_pair, hp) for hp in range(nh // 2)]

    def run_front_task():
        if front_tasks:
            front_tasks.pop(0)()

    assert blk == 2 * c and 2 * c == LANES and d == LANES
    si = lax.broadcasted_iota(jnp.int32, (c, 2 * c), 0)
    sl = lax.broadcasted_iota(jnp.int32, (c, 2 * c), 1)
    first_chunk = sl < c
    sj = jnp.where(first_chunk, sl, sl - c)
    lower_incl = si >= sj
    strict = si > sj
    eye_pair = jnp.where(si == sj, 1.0, 0.0).astype(F32)
    lane_row = lax.broadcasted_iota(jnp.int32, (1, 2 * c), 1) < c
    zeros_cd = jnp.zeros((c, d), BF16)

    def block_diag(m):
        return jnp.concatenate([jnp.where(first_chunk, m, 0.0), jnp.where(first_chunk, 0.0, m)], axis=0).astype(BF16)

    st = []
    for h in range(nh):
        hs = slice(h * d, (h + 1) * d)
        q = q_s[slot_r, :, hs]
        k = k_s[slot_r, :, hs]
        v = v_s[slot_r, :, hs]
        g_col = gc[:, nh + h:nh + h + 1]
        g_row = gc_t[nh + h:nh + h + 1, :]
        b_col = beta[:, h:h + 1]
        g_col_pair = jnp.where(first_chunk, g_col[0:c], g_col[c:2 * c])
        g_last = jnp.where(lane_row, g_col[c - 1:c], g_col[2 * c - 1:2 * c])
        decay = jnp.where(lower_incl, jnp.exp2(jnp.where(lower_incl, g_col_pair - g_row, 0.0)), 0.0)
        k_beta = k * b_col
        e_col = jnp.exp2(g_col)
        kb, qb, kbf = k_beta.astype(BF16), q.astype(BF16), k.astype(BF16)
        lhs = jnp.concatenate([jnp.concatenate([kb[0:c], kb[c:2 * c]], axis=1),
                               jnp.concatenate([qb[0:c], qb[c:2 * c]], axis=1),
                               jnp.concatenate([eye_d, eye_d], axis=1)], axis=0)
        k_diag = jnp.concatenate([jnp.concatenate([kbf[0:c], zeros_cd], axis=1),
                                  jnp.concatenate([zeros_cd, kbf[c:2 * c]], axis=1)], axis=0)
        kk = _dot_nt(lhs, k_diag)
        vb, kbe = (v * b_col).astype(BF16), (k_beta * e_col).astype(BF16)
        zeros_2 = jnp.zeros((c, 2 * d), BF16)
        st.append(dict(
            a=jnp.where(strict, kk[0:c] * decay, 0.0),
            attn=kk[c:2 * c] * decay,
            k_tail_t=kk[2 * c:2 * c + d] * jnp.exp2(g_last - g_row),
            rhs=jnp.concatenate([jnp.concatenate([vb[0:c], kbe[0:c], zeros_2], axis=1),
                                 jnp.concatenate([zeros_2, vb[c:2 * c], kbe[c:2 * c]], axis=1)], axis=0),
            qe=(q * e_col).astype(BF16),
            e_last=[jnp.exp2(g_col[c - 1:c]), jnp.exp2(g_col[2 * c - 1:2 * c])]))
    run_front_task()

    for cur in st:
        x = -cur["a"]
        cur["y"] = _dot(x.astype(BF16), block_diag(x))
        cur["p"] = eye_pair + x
    run_front_task()
    n_levels = int(np.log2(c))
    for lvl in range(1, n_levels):
        for cur in st:
            y_bd = block_diag(cur["y"])
            p = cur["p"]
            if lvl + 1 < n_levels:
                zz = _dot(jnp.concatenate([cur["y"].astype(BF16), p.astype(BF16)], axis=0), y_bd)
                cur["y"] = zz[0:c]
                cur["p"] = p + zz[c:2 * c]
            else:
                cur["p"] = p + _dot(p.astype(BF16), y_bd)
        run_front_task()
    for cur in st:
        cur["uw"] = _dot(cur["p"].astype(BF16), cur["rhs"])
    run_front_task()

    for ck in range(2):
        r = ck * c
        in_chunk = first_chunk if ck == 0 else jnp.logical_not(first_chunk)
        in_chunk_d = lane_row if ck == 0 else jnp.logical_not(lane_row)
        s_old = [state_ref[h] for h in range(nh)]
        ws_qs = []
        for h in range(nh):
            cur = st[h]
            w = cur["uw"][:, (2 * ck + 1) * d:(2 * ck + 2) * d]
            lhs = jnp.concatenate([w.astype(BF16), cur["qe"][r:r + c]], axis=0)
            ws_qs.append(_dot(lhs, s_old[h].astype(BF16)))
        run_front_task()
        for h in range(nh):
            cur = st[h]
            v_new = (cur["uw"][:, 2 * ck * d:(2 * ck + 1) * d] - ws_qs[h][0:c]).astype(BF16)
            lhs = jnp.concatenate([jnp.where(in_chunk, cur["attn"], 0.0).astype(BF16),
                                   jnp.where(in_chunk_d, cur["k_tail_t"], 0.0).astype(BF16)], axis=0)
            rhs = jnp.concatenate([v_new, zeros_cd] if ck == 0 else [zeros_cd, v_new], axis=0)
            av_kv = _dot(lhs, rhs)
            state_ref[h] = s_old[h] * cur["e_last"][ck] + av_kv[c:c + d]
            o = ws_qs[h][c:2 * c] + av_kv[0:c]
            o = (o * lax.rsqrt(jnp.mean(o * o, axis=-1, keepdims=True) + EPS)) * onorm
            z = z_s[slot_r, r:r + c, h * d:(h + 1) * d].astype(F32)
            o_ref[r:r + c, h * d:(h + 1) * d] = (o * _silu(z)).astype(o_ref.dtype)
    while front_tasks:
        run_front_task()

    xs_ref[0:HALO, :] = xs_ref[blk:blk + HALO, :]


def _gdn_front_gates(gates_ref, hp_ref, gc_s, gct_s, beta_s, slot_w):
    blk, c = GDN_BLOCK, CHUNK
    gates = gates_ref[...]
    a_log = hp_ref[0:1, :]
    dt_bias = hp_ref[1:2, :]
    beta = 1.0 / (1.0 + jnp.exp(-gates))
    sp_in = gates + dt_bias
    softplus = jnp.maximum(sp_in, 0.0) + jnp.log(1.0 + jnp.exp(-jnp.abs(sp_in)))
    g = (-jnp.exp(a_log) * softplus) * float(np.log2(np.e))

    row = lax.broadcasted_iota(jnp.int32, (blk, blk), 0)
    col = lax.broadcasted_iota(jnp.int32, (blk, blk), 1)
    tri = jnp.where((row >= col) & ((row // c) == (col // c)), 1.0, 0.0).astype(BF16)
    g_hi = g.astype(BF16)
    g_r1 = g - g_hi.astype(F32)
    g_mid = g_r1.astype(BF16)
    g_lo = (g_r1 - g_mid.astype(F32)).astype(BF16)
    gc = _dot(tri, g_hi) + _dot(tri, g_mid) + _dot(tri, g_lo)
    gc_s[slot_w] = gc
    gct_s[slot_w] = gc.T
    beta_s[slot_w] = beta


def gdn_core(proj, gates, conv_w, a_log, dt_bias, out_norm, batch, seq):
    t = proj.shape[0]
    nh, d = LA_HEADS, LA_D
    blk = GDN_BLOCK
    assert seq % blk == 0
    nblk = seq // blk
    hp = jnp.zeros((8, LANES), F32)
    hp = hp.at[0, nh:2 * nh].set(a_log.astype(F32)).at[1, nh:2 * nh].set(dt_bias.astype(F32))

    def in_map(b, n):
        return (b * nblk + jnp.minimum(n, nblk - 1), 0)

    return pl.pallas_call(
        _gdn_kernel,
        out_shape=jax.ShapeDtypeStruct((t, nh * d), BF16),
        grid=(batch, nblk + 1),
        in_specs=[pl.BlockSpec((blk, 4 * nh * d), in_map),
                  pl.BlockSpec((blk, LANES), in_map),
                  pl.BlockSpec((CONV_W, 3 * nh * d), lambda b, n: (0, 0)),
                  pl.BlockSpec((8, LANES), lambda b, n: (0, 0)),
                  pl.BlockSpec((1, d), lambda b, n: (0, 0))],
        out_specs=pl.BlockSpec((blk, nh * d), lambda b, n: (b * nblk + jnp.maximum(n - 1, 0), 0)),
        scratch_shapes=[pltpu.VMEM((HALO + blk, 3 * nh * d), BF16),
                        pltpu.VMEM((nh, d, d), F32),
                        pltpu.VMEM((2, blk, nh * d), F32), pltpu.VMEM((2, blk, nh * d), F32),
                        pltpu.VMEM((2, blk, nh * d), F32), pltpu.VMEM((2, blk, nh * d), BF16),
                        pltpu.VMEM((2, blk, LANES), F32), pltpu.VMEM((2, LANES, blk), F32),
                        pltpu.VMEM((2, blk, LANES), F32)],
        compiler_params=_cparams(("arbitrary", "arbitrary")),
        name="gdn_core",
    )(proj, gates, conv_w.astype(F32), hp, out_norm.reshape(1, d).astype(F32))


TILE_FULL, TILE_FIRST, TILE_HALF = 0, 1, 2


def _swiglu_kernel(te_ref, mode_ref, nv_ref, x_ref, g_ref, a_ref, wp_ref, wg_hbm, wu_hbm, wd_hbm, o_ref,
                   wg_c, wu_c, wd_c, stage_in, stage_out, sems, xres_s, *, pre_norm, pre_proj, routed, tf):
    i = pl.program_id(0)
    nf = wg_c.shape[0]
    valid = i < nv_ref[0]

    def chunk_copies(j, slot, tile=None):
        e = te_ref[i if tile is None else tile]
        cols = pl.ds(j * tf, tf)
        return (pltpu.make_async_copy(wg_hbm.at[e, :, cols], stage_in.at[slot, 0], sems.at[slot, 0]),
                pltpu.make_async_copy(wu_hbm.at[e, :, cols], stage_in.at[slot, 1], sems.at[slot, 1]),
                pltpu.make_async_copy(wd_hbm.at[e, cols, :], stage_out.at[slot], sems.at[slot, 2]))

    if routed:
        nxt = jnp.minimum(i + 1, pl.num_programs(0) - 1)
        prefetch_next = (valid & (mode_ref[i] != TILE_FIRST) & (i + 1 < nv_ref[0]) & (mode_ref[nxt] == TILE_FIRST))
        prv = jnp.maximum(i - 1, 0)
        chunk0_requested = (i > 0) & (mode_ref[prv] != TILE_FIRST)

        @pl.when(prefetch_next)
        def _():
            for c in chunk_copies(0, 0, tile=nxt):
                c.start()
    else:
        chunk0_requested = False

    tile_rows = x_ref.shape[0]

    d_model = o_ref.shape[1]

    def prepare_rows(rows):
        x = x_ref[0:rows, 0:d_model].astype(F32)
        if pre_proj:
            x = x + _dot(a_ref[0:rows, :], wp_ref[...])
            xres_s[0:rows, :] = x
        if pre_norm:
            ms = jnp.mean(x * x, axis=-1, keepdims=True)
            x = (x * lax.rsqrt(ms + EPS)) * g_ref[...]
        return x.astype(BF16)

    def chunk(xb, j):
        hid = _silu(_dot(xb, wg_c[j]), base2=False) * _dot(xb, wu_c[j])
        return _dot(hid.astype(BF16), wd_c[j])

    def finish(acc, rows):
        if pre_norm:
            res = xres_s[0:rows, :] if pre_proj else x_ref[0:rows, 0:d_model]
            acc = res + acc
        o_ref[0:rows, :] = acc.astype(o_ref.dtype)
        if rows < tile_rows:
            o_ref[rows:tile_rows, :] = jnp.zeros((tile_rows - rows, o_ref.shape[1]), o_ref.dtype)

    mode = mode_ref[i]

    @pl.when(valid & (mode == TILE_FIRST) & jnp.logical_not(chunk0_requested))
    def _():
        for c in chunk_copies(0, 0):
            c.start()


    @pl.when(valid & (mode == TILE_FIRST))
    def _():
        xb = prepare_rows(tile_rows)
        acc = None
        for j in range(nf):
            slot = j % 2
            if j + 1 < nf:
                for c in chunk_copies(j + 1, 1 - slot):
                    c.start()
            for c in chunk_copies(j, slot):
                c.wait()
            wg_c[j] = stage_in[slot, 0].astype(BF16)
            wu_c[j] = stage_in[slot, 1].astype(BF16)
            wd_c[j] = stage_out[slot].astype(BF16)
            y = chunk(xb, j)
            acc = y if acc is None else acc + y
        finish(acc, tile_rows)

    def steady(rows):
        xb = prepare_rows(rows)
        acc = None
        for j in range(nf):
            y = chunk(xb, j)
            acc = y if acc is None else acc + y
        finish(acc, rows)

    @pl.when(valid & (mode == TILE_FULL))
    def _():
        steady(tile_rows)

    if routed:
        @pl.when(valid & (mode == TILE_HALF))
        def _():
            steady(tile_rows // 2)

    @pl.when(jnp.logical_not(valid))
    def _():
        o_ref[...] = jnp.zeros_like(o_ref)


def expert_swiglu(x, gain, tile_expert, tile_mode, n_valid, wg, wu, wd, tile_rows, tf, out_dtype, pre_norm, name,
                  proj=None, routed=False):
    n_rows = x.shape[0]
    ne, d, f = wg.shape
    assert n_rows % tile_rows == 0 and f % tf == 0
    n_tiles = n_rows // tile_rows
    nf = f // tf
    pre_proj = proj is not None
    if pre_proj:
        a, wp = proj
        a_spec = pl.BlockSpec((tile_rows, a.shape[1]), lambda i, te, fi, nv: (jnp.minimum(i, nv[0] - 1), 0))
    else:
        a, wp = jnp.zeros((8, LANES), BF16), jnp.zeros((LANES, d), BF16)
        a_spec = pl.BlockSpec(a.shape, lambda i, te, fi, nv: (0, 0))
    grid_spec = pltpu.PrefetchScalarGridSpec(
        num_scalar_prefetch=3,
        grid=(n_tiles,),
        in_specs=[pl.BlockSpec((tile_rows, x.shape[1]), lambda i, te, fi, nv: (jnp.minimum(i, nv[0] - 1), 0)),
                  pl.BlockSpec((1, d), lambda i, te, fi, nv: (0, 0)),
                  a_spec,
                  pl.BlockSpec(wp.shape, lambda i, te, fi, nv: (0, 0)),
                  pl.BlockSpec(memory_space=pl.ANY),
                  pl.BlockSpec(memory_space=pl.ANY),
                  pl.BlockSpec(memory_space=pl.ANY)],
        out_specs=pl.BlockSpec((tile_rows, d), lambda i, te, fi, nv: (i, 0)),
        scratch_shapes=[pltpu.VMEM((nf, d, tf), BF16), pltpu.VMEM((nf, d, tf), BF16), pltpu.VMEM((nf, tf, d), BF16),
                        pltpu.VMEM((2, 2, d, tf), F32), pltpu.VMEM((2, tf, d), F32),
                        pltpu.SemaphoreType.DMA((2, 3)),
                        pltpu.VMEM((tile_rows, d) if pre_proj else (8, LANES), F32)],
    )
    return pl.pallas_call(
        functools.partial(_swiglu_kernel, pre_norm=pre_norm, pre_proj=pre_proj, routed=routed, tf=tf),
        out_shape=jax.ShapeDtypeStruct((n_rows, d), out_dtype),
        grid_spec=grid_spec,
        compiler_params=pltpu.CompilerParams(dimension_semantics=("arbitrary",), vmem_limit_bytes=EXPERT_VMEM_LIMIT),
        name=name,
    )(tile_expert, tile_mode, n_valid, x, gain.reshape(1, d).astype(F32), a, wp, wg, wu, wd)


def ffn_dense(x, gain, wg, wu, wd, tm, tf, proj=None):
    t = x.shape[0]
    n_tiles = t // tm
    tile_mode = jnp.full((n_tiles,), TILE_FULL, jnp.int32).at[0].set(TILE_FIRST)
    return expert_swiglu(x, gain, jnp.zeros((n_tiles,), jnp.int32), tile_mode, jnp.full((1,), n_tiles, jnp.int32),
                         wg[None], wu[None], wd[None], tm, tf, F32, True, "ffn_dense", proj=proj)


def _t5_bucket_np(dist):
    max_exact = N_BUCKETS // 2
    n = np.maximum(dist, 0)
    safe = np.maximum(n, 1).astype(np.float32)
    large = max_exact + (np.log(safe / max_exact) / np.log(MAX_DIST / max_exact)
                         * (N_BUCKETS - max_exact)).astype(np.int32)
    large = np.minimum(large, N_BUCKETS - 1)
    return np.where(n < max_exact, n, large).astype(np.int32)


def _bias_kernel(bucket_ref, valid_ref, rb_ref, o_ref):
    bucket = bucket_ref[...]
    for h in range(SW_HEADS):
        acc = jnp.zeros(bucket.shape, F32)
        for b in range(N_BUCKETS):
            acc = jnp.where(bucket == b, rb_ref[b, h], acc)
        for v in range(valid_ref.shape[0]):
            o_ref[v, h] = jnp.where(valid_ref[v] > 0, acc * LOG2E, NEG_INF)


def bias_table(rel_bias):
    qi = np.arange(WINDOW)[:, None] + WINDOW
    kj = np.arange(2 * WINDOW)[None, :]
    dist = qi - kj
    band = (dist >= 0) & (dist < WINDOW)
    valid = np.stack([band, band & (kj >= WINDOW)]).astype(np.int32)
    return pl.pallas_call(
        _bias_kernel,
        out_shape=jax.ShapeDtypeStruct((2, SW_HEADS, WINDOW, 2 * WINDOW), F32),
        in_specs=[pl.BlockSpec(memory_space=pltpu.VMEM), pl.BlockSpec(memory_space=pltpu.VMEM),
                  pl.BlockSpec(memory_space=pltpu.SMEM)],
        out_specs=pl.BlockSpec(memory_space=pltpu.VMEM),
        name="t5_bias_table",
    )(jnp.asarray(_t5_bucket_np(dist)), jnp.asarray(valid), rel_bias.astype(F32))


def _swa_kernel(q_ref, kvp_ref, kvc_ref, bias_ref, qn_ref, kn_ref, sink_ref, o_ref):
    blk, hd = WINDOW, SW_HD
    kv_w = SW_KV_HEADS * hd
    first = jnp.where(pl.program_id(1) == 0, 1, 0)
    gw = 2 * LANES
    gi = lax.broadcasted_iota(jnp.int32, (gw, gw), 0)
    gj = lax.broadcasted_iota(jnp.int32, (gw, gw), 1)
    group_ones = jnp.where((gi // hd) == (gj // hd), 1.0, 0.0).astype(BF16)
    lane = lax.broadcasted_iota(jnp.int32, (1, LANES), 1)
    low_half = lane < hd

    def head_norm(x, gain):
        cols = []
        for c0 in range(0, x.shape[1], gw):
            xc = x[:, c0:c0 + gw]
            ss = _dot((xc * xc).astype(BF16), group_ones)
            cols.append(xc * lax.rsqrt(ss * (1.0 / hd) + EPS))
        return jnp.concatenate(cols, axis=1) * gain

    def dup_half(x, half):
        swapped = pltpu.roll(x, hd, 1)
        return jnp.where(low_half == (half == 0), x, swapped)

    n_qb = q_ref.shape[0] // blk
    qn = head_norm(q_ref[...].astype(F32), qn_ref[...]) * ((hd ** -0.5) * LOG2E)
    half_sel = [jnp.where(low_half, 1.0, 0.0), jnp.where(low_half, 0.0, 1.0)]
    k_all = jnp.concatenate([kvp_ref[:, 0:kv_w], kvc_ref[:, 0:kv_w]], axis=0).astype(F32)
    kn = head_norm(k_all, kn_ref[...])
    v_all = jnp.concatenate([kvp_ref[:, kv_w:2 * kv_w], kvc_ref[:, kv_w:2 * kv_w]], axis=0).astype(F32)
    ks, vs = [], []
    for g in range(SW_KV_HEADS):
        c0 = (g // 2) * LANES
        ks.append(dup_half(kn[:, c0:c0 + LANES], g % 2).astype(BF16))
        vs.append(dup_half(v_all[:, c0:c0 + LANES], g % 2).astype(BF16))

    pairs = [(j, hq) for j in range(n_qb) for hq in range(SW_HEADS)]
    for g0 in range(0, len(pairs), SWA_GROUP):
        group = pairs[g0:g0 + SWA_GROUP]
        scores = {}
        for (j, hq) in group:
            c0 = (hq // 2) * LANES
            q_h = (qn[j * blk:(j + 1) * blk, c0:c0 + LANES] * half_sel[hq % 2]).astype(BF16)
            scores[(j, hq)] = _dot_nt(q_h, ks[hq // SW_GROUP][j * blk:(j + 2) * blk])
        probs = {}
        for (j, hq) in group:
            variant = first if j == 0 else 0
            s = scores[(j, hq)] + bias_ref[variant, hq]
            sink = sink_ref[hq] * LOG2E
            mx = jnp.maximum(jnp.max(s, axis=-1, keepdims=True), sink)
            p = jnp.exp2(s - mx)
            denom = jnp.sum(p, axis=-1, keepdims=True) + jnp.exp2(sink - mx)
            probs[(j, hq)] = (p * (1.0 / denom)).astype(BF16)
        outs = {key: _dot(probs[key], vs[key[1] // SW_GROUP][key[0] * blk:(key[0] + 2) * blk]) for key in group}
        for (j, hq) in group[0::2]:
            c = hq // 2
            o_ref[j * blk:(j + 1) * blk, c * LANES:(c + 1) * LANES] = jnp.where(
                low_half, outs[(j, hq)], outs[(j, hq + 1)]).astype(o_ref.dtype)


def swa_attention(q, kv, bias, q_norm, k_norm, sinks, batch, seq):
    t = q.shape[0]
    blk = WINDOW
    step = SWA_QBLOCKS * blk
    assert seq % step == 0
    nb = seq // step
    qw = SW_HEADS * SW_HD
    kvw = 2 * SW_KV_HEADS * SW_HD
    return pl.pallas_call(
        _swa_kernel,
        out_shape=jax.ShapeDtypeStruct((t, qw), BF16),
        grid=(batch, nb),
        in_specs=[pl.BlockSpec((step, qw), lambda b, n: (b * nb + n, 0)),
                  pl.BlockSpec((blk, kvw), lambda b, n: (jnp.maximum((b * nb + n) * SWA_QBLOCKS - 1, b * nb * SWA_QBLOCKS), 0)),
                  pl.BlockSpec((step, kvw), lambda b, n: (b * nb + n, 0)),
                  pl.BlockSpec((2, SW_HEADS, blk, 2 * blk), lambda b, n: (0, 0, 0, 0)),
                  pl.BlockSpec((1, qw), lambda b, n: (0, 0)),
                  pl.BlockSpec((1, kvw // 2), lambda b, n: (0, 0)),
                  pl.BlockSpec(memory_space=pltpu.SMEM)],
        out_specs=pl.BlockSpec((step, qw), lambda b, n: (b * nb + n, 0)),
        compiler_params=_cparams(("parallel", "parallel")),
        name="swa_attention",
    )(q, kv, kv, bias, jnp.tile(q_norm.astype(F32), SW_HEADS).reshape(1, qw),
      jnp.tile(k_norm.astype(F32), SW_KV_HEADS).reshape(1, kvw // 2), sinks.astype(F32))


def _route_kernel(x_ref, a_ref, wp_ref, g_ref, wr_ref, h_ref, r_ref, wt_ref, tab_ref, cnt_ref,
                  sel_s, gw_s, cnt_s, start_s, run_s, *, tile_rows):
    ne = N_EXPERTS
    p = pl.program_id(0)
    i = pl.program_id(1)
    tm = x_ref.shape[0]
    sub = lax.broadcasted_iota(jnp.int32, (ne, tm), 0).astype(F32)

    def seg_rows(sel):
        n = jnp.sum(sel, axis=1, keepdims=True)
        return jnp.floor((n + (SEG_ALIGN - 1)) * (1.0 / SEG_ALIGN)) * SEG_ALIGN

    def excl_cumsum_experts(v):
        sub8 = lax.broadcasted_iota(jnp.int32, v.shape, 0)
        out = jnp.zeros_like(v)
        for e in range(ne - 1):
            out = out + jnp.where(sub8 > e, v[e:e + 1, :], 0.0)
        return out

    @pl.when(p == 0)
    def _():
        @pl.when(i == 0)
        def _():
            cnt_s[...] = jnp.zeros_like(cnt_s)

        x = x_ref[...] + _dot(a_ref[...], wp_ref[...])
        h_ref[...] = x
        ms = jnp.mean(x * x, axis=-1, keepdims=True)
        xn32 = (x * lax.rsqrt(ms + EPS)) * g_ref[...]
        xn_hi = xn32.astype(BF16)
        xn_lo = (xn32 - xn_hi.astype(F32)).astype(BF16)
        p_hi = _dot_nt(wr_ref[...], xn_hi)
        p_lo = _dot_nt(wr_ref[...], xn_lo)
        logits = p_hi[0:ne] + p_hi[ne:2 * ne] + p_lo[0:ne]
        m1 = jnp.max(logits, axis=0, keepdims=True)
        i1 = jnp.min(jnp.where(logits == m1, sub, float(ne)), axis=0, keepdims=True)
        l2 = jnp.where(sub == i1, -jnp.inf, logits)
        m2 = jnp.max(l2, axis=0, keepdims=True)
        i2 = jnp.min(jnp.where(l2 == m2, sub, float(ne)), axis=0, keepdims=True)
        e2 = jnp.exp(m2 - m1)
        w1 = 1.0 / (1.0 + e2)
        w2 = e2 / (1.0 + e2)
        sel = jnp.where((sub == i1) | (sub == i2), 1.0, 0.0)
        sel_s[i] = sel
        gw_s[i] = jnp.where(sub == i1, w1, jnp.where(sub == i2, w2, 0.0))
        cnt_s[...] += seg_rows(sel)

    @pl.when(p == 1)
    def _():
        @pl.when(i == 0)
        def _():
            cnt = cnt_s[...]
            padded = jnp.floor((cnt + (tile_rows - 1)) * (1.0 / tile_rows)) * tile_rows
            start_s[...] = excl_cumsum_experts(padded)
            run_s[...] = jnp.zeros_like(run_s)
            cnt_ref[...] = cnt

        sel = sel_s[i]
        gw = gw_s[i]
        ti = lax.broadcasted_iota(jnp.int32, (tm, tm), 0)
        tj = lax.broadcasted_iota(jnp.int32, (tm, tm), 1)
        tri = jnp.where(ti <= tj, 1.0, 0.0).astype(BF16)
        csum = _dot(sel.astype(BF16), tri)
        seg = jnp.broadcast_to(seg_rows(sel), run_s.shape)
        local0 = excl_cumsum_experts(seg)
        tab_ref[0, 0] = start_s[...] + run_s[...]
        tab_ref[0, 1] = seg
        tab_ref[0, 2] = local0
        run_s[...] += seg
        local_row = local0[:, 0:1] + csum - sel
        ia = jnp.min(jnp.where(sel > 0.0, sub, float(ne)), axis=0, keepdims=True)
        ib = jnp.max(jnp.where(sel > 0.0, sub, -1.0), axis=0, keepdims=True)
        pick_a = sub == ia
        pick_b = sub == ib
        rows = [jnp.sum(jnp.where(pick_a, local_row, 0.0), axis=0, keepdims=True),
                jnp.sum(jnp.where(pick_b, local_row, 0.0), axis=0, keepdims=True),
                jnp.sum(jnp.where(pick_a, gw, 0.0), axis=0, keepdims=True),
                jnp.sum(jnp.where(pick_b, gw, 0.0), axis=0, keepdims=True)]
        r_ref[...] = jnp.concatenate(rows + [jnp.zeros((ne - 4, tm), F32)], axis=0)
        wpad = jnp.concatenate(rows[2:4] + rows[0:2] + [jnp.zeros((LANES - 4, tm), F32)], axis=0)
        wt_ref[...] = wpad.T


def moe_route(x, a, wp, gain, w_router, tm, tile_rows):
    t, d = x.shape
    ne = w_router.shape[1]
    assert ne == N_EXPERTS
    w_hi = w_router.astype(BF16)
    w_lo = (w_router - w_hi.astype(F32)).astype(BF16)
    wr = jnp.concatenate([w_hi.T, w_lo.T], axis=0)
    tm = min(tm, t)
    nt = t // tm

    def row_map(p, i):
        return (i * (1 - p) + (nt - 1) * p, 0)

    return pl.pallas_call(
        functools.partial(_route_kernel, tile_rows=tile_rows),
        out_shape=(jax.ShapeDtypeStruct((t, d), F32),
                   jax.ShapeDtypeStruct((ne, t), F32), jax.ShapeDtypeStruct((t, LANES), F32),
                   jax.ShapeDtypeStruct((nt, 3, ne, LANES), F32), jax.ShapeDtypeStruct((ne, LANES), F32)),
        grid=(2, nt),
        in_specs=[pl.BlockSpec((tm, d), row_map),
                  pl.BlockSpec((tm, a.shape[1]), row_map),
                  pl.BlockSpec(wp.shape, lambda p, i: (0, 0)),
                  pl.BlockSpec((1, d), lambda p, i: (0, 0)),
                  pl.BlockSpec((2 * ne, d), lambda p, i: (0, 0))],
        out_specs=(pl.BlockSpec((tm, d), row_map),
                   pl.BlockSpec((ne, tm), lambda p, i: (0, i * p)),
                   pl.BlockSpec((tm, LANES), lambda p, i: (i * p, 0)),
                   pl.BlockSpec((1, 3, ne, LANES), lambda p, i: (i * p, 0, 0, 0)),
                   pl.BlockSpec((ne, LANES), lambda p, i: (0, 0))),
        scratch_shapes=[pltpu.VMEM((nt, ne, tm), F32), pltpu.VMEM((nt, ne, tm), F32),
                        pltpu.VMEM((ne, LANES), F32), pltpu.VMEM((ne, LANES), F32), pltpu.VMEM((ne, LANES), F32)],
        compiler_params=_cparams(("arbitrary", "arbitrary")),
        name="moe_route",
    )(x, a, wp, gain.reshape(1, d), wr)


def _segment_copies(tab_ref, i, e, local_ref, slot_ref, sem, to_slots):
    base = (i * N_EXPERTS + e) * 3
    slot0, rows, local0 = tab_ref[base], tab_ref[base + 1], tab_ref[base + 2]
    out = []
    done = 0
    size = MOE_TOKEN_TILE
    while size >= SEG_ALIGN:
        take = rows & size
        loc = local_ref.at[pl.ds(pl.multiple_of(local0 + done, SEG_ALIGN), size)]
        slt = slot_ref.at[pl.ds(pl.multiple_of(slot0 + done, SEG_ALIGN), size)]
        desc = pltpu.make_async_copy(loc, slt, sem) if to_slots else pltpu.make_async_copy(slt, loc, sem)
        out.append((take != 0, desc))
        done = done + take
        size //= 2
    return out


def _run_segment_copies(tab_ref, tile, slot, rows_s, slot_ref, sems, to_slots, action):
    for e in range(N_EXPERTS):
        for cond, desc in _segment_copies(tab_ref, tile, e, rows_s.at[slot], slot_ref, sems.at[slot], to_slots):
            @pl.when(cond)
            def _():
                getattr(desc, action)()


def _dispatch_kernel(tab_ref, zf_ref, x_ref, g_ref, r_ref, xs_ref, rows_s, zero_s, sem, zsem, *, tile_rows):
    i = pl.program_id(0)
    tm = x_ref.shape[0]
    n_local = rows_s.shape[1]

    @pl.when(i == 0)
    def _():
        zero_s[...] = jnp.zeros_like(zero_s)

        def zero_copy(e):
            row0 = pl.multiple_of(zf_ref[e], tile_rows)
            return pltpu.make_async_copy(zero_s, xs_ref.at[pl.ds(row0, tile_rows)], zsem)

        for e in range(zf_ref.shape[0]):
            @pl.when(zf_ref[e] >= 0)
            def _():
                zero_copy(e).start()
        for e in range(zf_ref.shape[0]):
            @pl.when(zf_ref[e] >= 0)
            def _():
                zero_copy(e).wait()

    x = x_ref[...]
    ms = jnp.mean(x * x, axis=-1, keepdims=True)
    xn = ((x * lax.rsqrt(ms + EPS)) * g_ref[...]).astype(BF16)
    row_id = lax.broadcasted_iota(jnp.int32, (n_local, tm), 0).astype(F32)
    onehot = jnp.where((row_id == r_ref[0:1, :]) | (row_id == r_ref[1:2, :]), 1.0, 0.0).astype(BF16)
    slot = lax.rem(i, 2)
    rows_s[slot] = _dot(onehot, xn)

    _run_segment_copies(tab_ref, i, slot, rows_s, xs_ref, sem, True, "start")

    @pl.when(i > 0)
    def _():
        _run_segment_copies(tab_ref, i - 1, 1 - slot, rows_s, xs_ref, sem, True, "wait")

    @pl.when(i == pl.num_programs(0) - 1)
    def _():
        _run_segment_copies(tab_ref, i, slot, rows_s, xs_ref, sem, True, "wait")


def moe_dispatch(x, gain, r, tab, zf_rows, n_slots, tm, tile_rows):
    t, d = x.shape
    nt = t // tm
    n_local = TOP_K * tm + N_EXPERTS * SEG_ALIGN
    grid_spec = pltpu.PrefetchScalarGridSpec(
        num_scalar_prefetch=2,
        grid=(nt,),
        in_specs=[pl.BlockSpec((tm, d), lambda i, tb, zf: (i, 0)),
                  pl.BlockSpec((1, d), lambda i, tb, zf: (0, 0)),
                  pl.BlockSpec((N_EXPERTS, tm), lambda i, tb, zf: (0, i))],
        out_specs=pl.BlockSpec(memory_space=pl.ANY),
        scratch_shapes=[pltpu.VMEM((2, n_local, d), F32), pltpu.VMEM((tile_rows, d), F32),
                        pltpu.SemaphoreType.DMA((2,)), pltpu.SemaphoreType.DMA],
    )
    return pl.pallas_call(
        functools.partial(_dispatch_kernel, tile_rows=tile_rows),
        out_shape=jax.ShapeDtypeStruct((n_slots, d), F32),
        grid_spec=grid_spec,
        compiler_params=_cparams(("arbitrary",)),
        name="moe_dispatch",
    )(tab, zf_rows, x, gain.reshape(1, d), r)


def _combine_kernel(tab_ref, h_ref, wt_ref, ys_ref, o_ref, rows_s, sems):
    i = pl.program_id(0)
    tm = h_ref.shape[0]
    n_local = rows_s.shape[1]
    slot = lax.rem(i, 2)

    def fetch(tile, into):
        rows_s[into] = jnp.zeros(rows_s.shape[1:], rows_s.dtype)
        _run_segment_copies(tab_ref, tile, into, rows_s, ys_ref, sems, False, "start")

    @pl.when(i == 0)
    def _():
        fetch(i, slot)

    @pl.when(i + 1 < pl.num_programs(0))
    def _():
        fetch(i + 1, 1 - slot)

    _run_segment_copies(tab_ref, i, slot, rows_s, ys_ref, sems, False, "wait")

    wt = wt_ref[...]
    y = rows_s[slot].astype(BF16)
    col_id = lax.broadcasted_iota(jnp.int32, (tm, n_local), 1).astype(F32)
    pick_a = jnp.where(col_id == wt[:, 2:3], 1.0, 0.0).astype(BF16)
    pick_b = jnp.where(col_id == wt[:, 3:4], 1.0, 0.0).astype(BF16)
    o_ref[...] = h_ref[...] + wt[:, 0:1] * _dot(pick_a, y) + wt[:, 1:2] * _dot(pick_b, y)


def moe_combine(h, wt, tab, ys, tm):
    t, d = h.shape
    nt = t // tm
    n_local = TOP_K * tm + N_EXPERTS * SEG_ALIGN
    grid_spec = pltpu.PrefetchScalarGridSpec(
        num_scalar_prefetch=1,
        grid=(nt,),
        in_specs=[pl.BlockSpec((tm, d), lambda i, tb: (i, 0)),
                  pl.BlockSpec((tm, LANES), lambda i, tb: (i, 0)),
                  pl.BlockSpec(memory_space=pl.ANY)],
        out_specs=pl.BlockSpec((tm, d), lambda i, tb: (i, 0)),
        scratch_shapes=[pltpu.VMEM((2, n_local, d), F32), pltpu.SemaphoreType.DMA((2,))],
    )
    return pl.pallas_call(
        _combine_kernel,
        out_shape=jax.ShapeDtypeStruct((t, d), F32),
        grid_spec=grid_spec,
        compiler_params=_cparams(("arbitrary",)),
        name="moe_combine",
    )(tab, h, wt, ys)


def moe_layer(x, a, wp, gain, w_router, wg, wu, wd):
    t, d = x.shape
    ne = w_router.shape[1]
    tr, tm = MOE_TILE_ROWS, MOE_TOKEN_TILE
    assert t % tm == 0 and ne == N_EXPERTS
    nt = t // tm
    n_tiles = -(-(TOP_K * t + nt * ne * (SEG_ALIGN - 1) + ne * (tr - 1)) // tr)
    n_slots = n_tiles * tr

    h, r, wt, tab, cnt = moe_route(x, a, wp, gain, w_router, tm, tr)
    tab = jnp.transpose(tab[:, :, :, 0], (0, 2, 1)).astype(jnp.int32).reshape(-1)

    counts = cnt[:, 0].astype(jnp.int32)
    padded = ((counts + (tr - 1)) // tr) * tr
    ends = jnp.cumsum(padded)
    n_valid = (ends[-1] // tr).astype(jnp.int32)
    tile_row0 = jnp.arange(n_tiles, dtype=jnp.int32) * tr
    tile_expert = jnp.sum((tile_row0[:, None] >= ends[None, :]).astype(jnp.int32), axis=1)
    tile_expert = jnp.minimum(tile_expert, ne - 1)
    tile_expert = jnp.where(jnp.arange(n_tiles) < n_valid, tile_expert, tile_expert[jnp.maximum(n_valid - 1, 0)])
    prev_expert = jnp.concatenate([jnp.full((1,), -1, jnp.int32), tile_expert[:-1]])
    rows_used = (ends - padded + counts)[tile_expert] - tile_row0
    tile_mode = jnp.where(tile_expert != prev_expert, TILE_FIRST,
                          jnp.where(rows_used <= tr // 2, TILE_HALF, TILE_FULL)).astype(jnp.int32)
    tail = jnp.arange(TOP_K * t // tr, n_tiles, dtype=jnp.int32)
    zf_rows = jnp.concatenate([jnp.where(padded > 0, ends - tr, -1),
                               jnp.where(tail >= n_valid, tail * tr, -1)]).astype(jnp.int32)

    xs = moe_dispatch(h, gain, r, tab, zf_rows, n_slots, tm, tr)
    ys = expert_swiglu(xs, gain, tile_expert, tile_mode, n_valid.reshape(1), wg, wu, wd, tr, FFN_CHUNK, F32, False,
                       "moe_experts", routed=True)
    return moe_combine(h, wt, tab, ys, tm)


def kernel(x, a_norm, a_w_in, a_conv, a_log_decay, a_dt_bias, a_out_norm, a_w_out, kv_norm, kv_w, k_norm,
           b_norm, b_w_q, q_norm, b_sinks, b_w_o, rel_bias, ffn_norm, dense_w_gate, dense_w_up, dense_w_down,
           moe_router, moe_w_gate, moe_w_up, moe_w_down):
    batch, seq, d = x.shape
    t = batch * seq
    nh, hd = LA_HEADS, LA_D
    main_w = 4 * nh * hd
    h0 = x.reshape(t, d)

    w_in = a_w_in[0]
    w_main = w_in[:, 0:main_w].astype(BF16)
    w_gate = jnp.zeros((d, LANES), BF16).at[:, 0:2 * nh].set(w_in[:, main_w:main_w + 2 * nh].astype(BF16))
    proj, gates = norm_matmul(h0, [(a_norm[0], w_main, BF16), (a_norm[0], w_gate, F32)], IN_PROJ_TILE,
                              "gdn_in_proj")
    o = gdn_core(proj, gates, a_conv[0], a_log_decay[0], a_dt_bias[0], a_out_norm[0], batch, seq)

    h2 = ffn_dense(h0, ffn_norm[0], dense_w_gate[0], dense_w_up[0], dense_w_down[0], MOE_TILE_ROWS, FFN_CHUNK,
                   proj=(o, a_w_out[0].astype(BF16)))

    kv, q = norm_matmul(h2, [(kv_norm, kv_w.astype(BF16), BF16), (b_norm[0], b_w_q[0].astype(BF16), BF16)],
                        QKV_PROJ_TILE, "qkv_proj")
    bias = bias_table(rel_bias)
    attn = swa_attention(q, kv, bias, q_norm[0], k_norm, b_sinks[0], batch, seq)

    h4 = moe_layer(h2, attn, b_w_o[0].astype(BF16), ffn_norm[1], moe_router[0], moe_w_gate[0], moe_w_up[0],
                   moe_w_down[0])
    return h4.reshape(batch, seq, d)
```

```python
import functools

import numpy as np
import jax
import jax.numpy as jnp
from jax import lax
from jax.experimental import pallas as pl
from jax.experimental.pallas import tpu as pltpu

F32 = jnp.float32
BF16 = jnp.bfloat16

EPS = 1e-6
NEG_INF = -1e30

LA_HEADS = 8
LA_D = 128
CONV_W = 4
CHUNK = 64
SW_HEADS = 16
SW_KV_HEADS = 4
SW_GROUP = SW_HEADS // SW_KV_HEADS
SW_HD = 64
WINDOW = 128
SWA_QBLOCKS = 8
SWA_GROUP = 16
N_BUCKETS = 32
MAX_DIST = 128
N_EXPERTS = 8
TOP_K = 2
LOG2E = float(np.log2(np.e))

LANES = 128
SEG_ALIGN = 8
GDN_BLOCK = 2 * CHUNK
HALO = 16
MOE_TILE_ROWS = 512
MOE_TOKEN_TILE = 512
FFN_CHUNK = 512
IN_PROJ_TILE = 1024
QKV_PROJ_TILE = 1024

VMEM_LIMIT = 56 * 1024 * 1024
EXPERT_VMEM_LIMIT = 60 * 1024 * 1024


def _cparams(sem):
    return pltpu.CompilerParams(dimension_semantics=sem, vmem_limit_bytes=VMEM_LIMIT)


def _silu(x, base2=True):
    e = jnp.exp2(x * (-LOG2E)) if base2 else jnp.exp(-x)
    return x * (1.0 / (1.0 + e))


def _dot(a, b):
    return jnp.dot(a, b, preferred_element_type=F32)


def _dot_nt(a, b):
    return lax.dot_general(a, b, (((1,), (1,)), ((), ())), preferred_element_type=F32)


def _norm_matmul_kernel(*refs, n_groups):
    x_ref = refs[0]
    g_refs = refs[1:1 + n_groups]
    w_refs = refs[1 + n_groups:1 + 2 * n_groups]
    o_refs = refs[1 + 2 * n_groups:1 + 3 * n_groups]
    x = x_ref[...]
    xr = x * lax.rsqrt(jnp.mean(x * x, axis=-1, keepdims=True) + EPS)
    for g_ref, w_ref, o_ref in zip(g_refs, w_refs, o_refs):
        o_ref[...] = _dot((xr * g_ref[...]).astype(BF16), w_ref[...]).astype(o_ref.dtype)


def norm_matmul(x, groups, tm, name):
    t, d = x.shape
    tm = min(tm, t)
    assert t % tm == 0
    gains = [g.reshape(1, d).astype(F32) for g, _, _ in groups]
    ws = [w for _, w, _ in groups]
    return pl.pallas_call(
        functools.partial(_norm_matmul_kernel, n_groups=len(groups)),
        out_shape=[jax.ShapeDtypeStruct((t, w.shape[1]), dt) for _, w, dt in groups],
        grid=(t // tm,),
        in_specs=([pl.BlockSpec((tm, d), lambda i: (i, 0))]
                  + [pl.BlockSpec((1, d), lambda i: (0, 0)) for _ in groups]
                  + [pl.BlockSpec(w.shape, lambda i: (0, 0)) for w in ws]),
        out_specs=[pl.BlockSpec((tm, w.shape[1]), lambda i: (i, 0)) for w in ws],
        compiler_params=_cparams(("parallel",)),
        name=name,
    )(x, *gains, *ws)


def _gdn_kernel(proj_ref, gates_ref, convw_ref, hp_ref, onorm_ref, o_ref,
                xs_ref, state_ref, q_s, k_s, v_s, z_s, gc_s, gct_s, beta_s):
    n = pl.program_id(1)

    @pl.when(n == 0)
    def _():
        xs_ref[0:HALO, :] = jnp.zeros((HALO, xs_ref.shape[1]), xs_ref.dtype)
        for ref in (q_s, k_s, v_s, z_s, gc_s, gct_s, beta_s):
            ref[1] = jnp.zeros(ref.shape[1:], ref.dtype)

    @pl.when(n <= 1)
    def _():
        state_ref[...] = jnp.zeros_like(state_ref)

    args = (proj_ref, gates_ref, convw_ref, hp_ref, onorm_ref, o_ref, xs_ref, state_ref,
            q_s, k_s, v_s, z_s, gc_s, gct_s, beta_s)

    @pl.when(lax.rem(n, 2) == 0)
    def _():
        _gdn_step(*args, slot_w=0, slot_r=1)

    @pl.when(lax.rem(n, 2) == 1)
    def _():
        _gdn_step(*args, slot_w=1, slot_r=0)


def _gdn_step(proj_ref, gates_ref, convw_ref, hp_ref, onorm_ref, o_ref, xs_ref, state_ref,
              q_s, k_s, v_s, z_s, gc_s, gct_s, beta_s, *, slot_w, slot_r):
    nh, d, c = LA_HEADS, LA_D, CHUNK
    blk = GDN_BLOCK
    qkv_w = 3 * nh * d

    gc = gc_s[slot_r]
    gc_t = gct_s[slot_r]
    beta = beta_s[slot_r]

    xs_ref[HALO:HALO + blk, :] = proj_ref[:, 0:qkv_w]

    def front_gates():
        _gdn_front_gates(gates_ref, hp_ref, gc_s, gct_s, beta_s, slot_w)

    di = lax.broadcasted_iota(jnp.int32, (d, d), 0)
    dj = lax.broadcasted_iota(jnp.int32, (d, d), 1)
    eye_d = jnp.where(di == dj, 1.0, 0.0).astype(BF16)

    onorm = onorm_ref[...]

    n_shift = CONV_W - 1
    sr = lax.broadcasted_iota(jnp.int32, (n_shift * blk, HALO + blk), 0)
    sc = lax.broadcasted_iota(jnp.int32, (n_shift * blk, HALO + blk), 1)
    shift_mat = jnp.where(sc == HALO + (sr % blk) - (sr // blk + 1), 1.0, 0.0).astype(BF16)
    pair_w = 2 * d

    def conv_silu(col0):
        cols = slice(col0, col0 + pair_w)
        shifted = _dot(shift_mat, xs_ref[:, cols])
        acc = convw_ref[CONV_W - 1:CONV_W, cols] * xs_ref[HALO:HALO + blk, cols].astype(F32)
        for s in range(1, CONV_W):
            acc = acc + convw_ref[CONV_W - 1 - s:CONV_W - s, cols] * shifted[(s - 1) * blk:s * blk]
        return _silu(acc)

    def front_pair(hp):
        c0 = hp * pair_w
        qf = conv_silu(c0)
        kf = conv_silu(nh * d + c0)
        v_s[slot_w, :, c0:c0 + pair_w] = conv_silu(2 * nh * d + c0)
        for half in range(2):
            lo, hi = half * d, (half + 1) * d
            qh, kh = qf[:, lo:hi], kf[:, lo:hi]
            q_s[slot_w, :, c0 + lo:c0 + hi] = qh * (lax.rsqrt(jnp.sum(qh * qh, axis=-1, keepdims=True) + EPS)
                                                    * (d ** -0.5))
            k_s[slot_w, :, c0 + lo:c0 + hi] = kh * lax.rsqrt(jnp.sum(kh * kh, axis=-1, keepdims=True) + EPS)
        z_s[slot_w, :, c0:c0 + pair_w] = proj_ref[:, qkv_w + c0:qkv_w + c0 + pair_w]

    front_tasks = [front_gates] + [functools.partial(front_pair, hp) for hp in range(nh // 2)]

    def run_front_task():
        if front_tasks:
            front_tasks.pop(0)()

    assert blk == 2 * c and 2 * c == LANES and d == LANES
    si = lax.broadcasted_iota(jnp.int32, (c, 2 * c), 0)
    sl = lax.broadcasted_iota(jnp.int32, (c, 2 * c), 1)
    first_chunk = sl < c
    sj = jnp.where(first_chunk, sl, sl - c)
    lower_incl = si >= sj
    strict = si > sj
    eye_pair = jnp.where(si == sj, 1.0, 0.0).astype(F32)
    lane_row = lax.broadcasted_iota(jnp.int32, (1, 2 * c), 1) < c
    zeros_cd = jnp.zeros((c, d), BF16)

    def block_diag(m):
        return jnp.concatenate([jnp.where(first_chunk, m, 0.0), jnp.where(first_chunk, 0.0, m)], axis=0).astype(BF16)

    st = []
    for h in range(nh):
        hs = slice(h * d, (h + 1) * d)
        q = q_s[slot_r, :, hs]
        k = k_s[slot_r, :, hs]
        v = v_s[slot_r, :, hs]
        g_col = gc[:, nh + h:nh + h + 1]
        g_row = gc_t[nh + h:nh + h + 1, :]
        b_col = beta[:, h:h + 1]
        g_col_pair = jnp.where(first_chunk, g_col[0:c], g_col[c:2 * c])
        g_last = jnp.where(lane_row, g_col[c - 1:c], g_col[2 * c - 1:2 * c])
        decay = jnp.where(lower_incl, jnp.exp2(jnp.where(lower_incl, g_col_pair - g_row, 0.0)), 0.0)
        k_beta = k * b_col
        e_col = jnp.exp2(g_col)
        kb, qb, kbf = k_beta.astype(BF16), q.astype(BF16), k.astype(BF16)
        lhs = jnp.concatenate([jnp.concatenate([kb[0:c], kb[c:2 * c]], axis=1),
                               jnp.concatenate([qb[0:c], qb[c:2 * c]], axis=1),
                               jnp.concatenate([eye_d, eye_d], axis=1)], axis=0)
        k_diag = jnp.concatenate([jnp.concatenate([kbf[0:c], zeros_cd], axis=1),
                                  jnp.concatenate([zeros_cd, kbf[c:2 * c]], axis=1)], axis=0)
        kk = _dot_nt(lhs, k_diag)
        vb, kbe = (v * b_col).astype(BF16), (k_beta * e_col).astype(BF16)
        zeros_2 = jnp.zeros((c, 2 * d), BF16)
        st.append(dict(
            a=jnp.where(strict, kk[0:c] * decay, 0.0),
            attn=kk[c:2 * c] * decay,
            k_tail_t=kk[2 * c:2 * c + d] * jnp.exp2(g_last - g_row),
            rhs=jnp.concatenate([jnp.concatenate([vb[0:c], kbe[0:c], zeros_2], axis=1),
                                 jnp.concatenate([zeros_2, vb[c:2 * c], kbe[c:2 * c]], axis=1)], axis=0),
            qe=(q * e_col).astype(BF16),
            e_last=[jnp.exp2(g_col[c - 1:c]), jnp.exp2(g_col[2 * c - 1:2 * c])]))
    run_front_task()

    for cur in st:
        x = -cur["a"]
        cur["y"] = _dot(x.astype(BF16), block_diag(x))
        cur["p"] = eye_pair + x
    run_front_task()
    n_levels = int(np.log2(c))
    for lvl in range(1, n_levels):
        for cur in st:
            y_bd = block_diag(cur["y"])
            p = cur["p"]
            if lvl + 1 < n_levels:
                zz = _dot(jnp.concatenate([cur["y"].astype(BF16), p.astype(BF16)], axis=0), y_bd)
                cur["y"] = zz[0:c]
                cur["p"] = p + zz[c:2 * c]
            else:
                cur["p"] = p + _dot(p.astype(BF16), y_bd)
        run_front_task()
    for cur in st:
        cur["uw"] = _dot(cur["p"].astype(BF16), cur["rhs"])
    run_front_task()

    for ck in range(2):
        r = ck * c
        in_chunk = first_chunk if ck == 0 else jnp.logical_not(first_chunk)
        in_chunk_d = lane_row if ck == 0 else jnp.logical_not(lane_row)
        s_old = [state_ref[h] for h in range(nh)]
        ws_qs = []
        for h in range(nh):
            cur = st[h]
            w = cur["uw"][:, (2 * ck + 1) * d:(2 * ck + 2) * d]
            lhs = jnp.concatenate([w.astype(BF16), cur["qe"][r:r + c]], axis=0)
            ws_qs.append(_dot(lhs, s_old[h].astype(BF16)))
        run_front_task()
        for h in range(nh):
            cur = st[h]
            v_new = (cur["uw"][:, 2 * ck * d:(2 * ck + 1) * d] - ws_qs[h][0:c]).astype(BF16)
            lhs = jnp.concatenate([jnp.where(in_chunk, cur["attn"], 0.0).astype(BF16),
                                   jnp.where(in_chunk_d, cur["k_tail_t"], 0.0).astype(BF16)], axis=0)
            rhs = jnp.concatenate([v_new, zeros_cd] if ck == 0 else [zeros_cd, v_new], axis=0)
            av_kv = _dot(lhs, rhs)
            state_ref[h] = s_old[h] * cur["e_last"][ck] + av_kv[c:c + d]
            o = ws_qs[h][c:2 * c] + av_kv[0:c]
            o = (o * lax.rsqrt(jnp.mean(o * o, axis=-1, keepdims=True) + EPS)) * onorm
            z = z_s[slot_r, r:r + c, h * d:(h + 1) * d].astype(F32)
            o_ref[r:r + c, h * d:(h + 1) * d] = (o * _silu(z)).astype(o_ref.dtype)
    while front_tasks:
        run_front_task()

    xs_ref[0:HALO, :] = xs_ref[blk:blk + HALO, :]


def _gdn_front_gates(gates_ref, hp_ref, gc_s, gct_s, beta_s, slot_w):
    blk, c = GDN_BLOCK, CHUNK
    gates = gates_ref[...]
    a_log = hp_ref[0:1, :]
    dt_bias = hp_ref[1:2, :]
    beta = 1.0 / (1.0 + jnp.exp(-gates))
    sp_in = gates + dt_bias
    softplus = jnp.maximum(sp_in, 0.0) + jnp.log(1.0 + jnp.exp(-jnp.abs(sp_in)))
    g = (-jnp.exp(a_log) * softplus) * float(np.log2(np.e))

    row = lax.broadcasted_iota(jnp.int32, (blk, blk), 0)
    col = lax.broadcasted_iota(jnp.int32, (blk, blk), 1)
    tri = jnp.where((row >= col) & ((row // c) == (col // c)), 1.0, 0.0).astype(BF16)
    g_hi = g.astype(BF16)
    g_r1 = g - g_hi.astype(F32)
    g_mid = g_r1.astype(BF16)
    g_lo = (g_r1 - g_mid.astype(F32)).astype(BF16)
    gc = _dot(tri, g_hi) + _dot(tri, g_mid) + _dot(tri, g_lo)
    gc_s[slot_w] = gc
    gct_s[slot_w] = gc.T
    beta_s[slot_w] = beta


def gdn_core(proj, gates, conv_w, a_log, dt_bias, out_norm, batch, seq):
    t = proj.shape[0]
    nh, d = LA_HEADS, LA_D
    blk = GDN_BLOCK
    assert seq % blk == 0
    nblk = seq // blk
    hp = jnp.zeros((8, LANES), F32)
    hp = hp.at[0, nh:2 * nh].set(a_log.astype(F32)).at[1, nh:2 * nh].set(dt_bias.astype(F32))

    def in_map(b, n):
        return (b * nblk + jnp.minimum(n, nblk - 1), 0)

    return pl.pallas_call(
        _gdn_kernel,
        out_shape=jax.ShapeDtypeStruct((t, nh * d), BF16),
        grid=(batch, nblk + 1),
        in_specs=[pl.BlockSpec((blk, 4 * nh * d), in_map),
                  pl.BlockSpec((blk, LANES), in_map),
                  pl.BlockSpec((CONV_W, 3 * nh * d), lambda b, n: (0, 0)),
                  pl.BlockSpec((8, LANES), lambda b, n: (0, 0)),
                  pl.BlockSpec((1, d), lambda b, n: (0, 0))],
        out_specs=pl.BlockSpec((blk, nh * d), lambda b, n: (b * nblk + jnp.maximum(n - 1, 0), 0)),
        scratch_shapes=[pltpu.VMEM((HALO + blk, 3 * nh * d), BF16),
                        pltpu.VMEM((nh, d, d), F32),
                        pltpu.VMEM((2, blk, nh * d), F32), pltpu.VMEM((2, blk, nh * d), F32),
                        pltpu.VMEM((2, blk, nh * d), F32), pltpu.VMEM((2, blk, nh * d), BF16),
                        pltpu.VMEM((2, blk, LANES), F32), pltpu.VMEM((2, LANES, blk), F32),
                        pltpu.VMEM((2, blk, LANES), F32)],
        compiler_params=_cparams(("arbitrary", "arbitrary")),
        name="gdn_core",
    )(proj, gates, conv_w.astype(F32), hp, out_norm.reshape(1, d).astype(F32))


TILE_FULL, TILE_FIRST, TILE_HALF = 0, 1, 2


def _swiglu_kernel(te_ref, mode_ref, nv_ref, x_ref, g_ref, a_ref, wp_ref, wg_hbm, wu_hbm, wd_hbm, o_ref,
                   wg_c, wu_c, wd_c, stage_in, stage_out, sems, xres_s, *, pre_norm, pre_proj, routed, tf):
    i = pl.program_id(0)
    nf = wg_c.shape[0]
    valid = i < nv_ref[0]

    def chunk_copies(j, slot, tile=None):
        e = te_ref[i if tile is None else tile]
        cols = pl.ds(j * tf, tf)
        return (pltpu.make_async_copy(wg_hbm.at[e, :, cols], stage_in.at[slot, 0], sems.at[slot, 0]),
                pltpu.make_async_copy(wu_hbm.at[e, :, cols], stage_in.at[slot, 1], sems.at[slot, 1]),
                pltpu.make_async_copy(wd_hbm.at[e, cols, :], stage_out.at[slot], sems.at[slot, 2]))

    if routed:
        nxt = jnp.minimum(i + 1, pl.num_programs(0) - 1)
        prefetch_next = (valid & (mode_ref[i] != TILE_FIRST) & (i + 1 < nv_ref[0]) & (mode_ref[nxt] == TILE_FIRST))
        prv = jnp.maximum(i - 1, 0)
        chunk0_requested = (i > 0) & (mode_ref[prv] != TILE_FIRST)

        @pl.when(prefetch_next)
        def _():
            for k, c in enumerate(chunk_copies(0, 0, tile=nxt)):
                c.start(priority=k % 2)
    else:
        chunk0_requested = False

    tile_rows = x_ref.shape[0]

    d_model = o_ref.shape[1]

    def prepare_rows(rows):
        x = x_ref[0:rows, 0:d_model].astype(F32)
        if pre_proj:
            x = x + _dot(a_ref[0:rows, :], wp_ref[...])
            xres_s[0:rows, :] = x
        if pre_norm:
            ms = jnp.mean(x * x, axis=-1, keepdims=True)
            x = (x * lax.rsqrt(ms + EPS)) * g_ref[...]
        return x.astype(BF16)

    def chunk(xb, j):
        hid = _silu(_dot(xb, wg_c[j]), base2=False) * _dot(xb, wu_c[j])
        return _dot(hid.astype(BF16), wd_c[j])

    def finish(acc, rows):
        if pre_norm:
            res = xres_s[0:rows, :] if pre_proj else x_ref[0:rows, 0:d_model]
            acc = res + acc
        o_ref[0:rows, :] = acc.astype(o_ref.dtype)
        if rows < tile_rows:
            o_ref[rows:tile_rows, :] = jnp.zeros((tile_rows - rows, o_ref.shape[1]), o_ref.dtype)

    mode = mode_ref[i]

    @pl.when(valid & (mode == TILE_FIRST) & jnp.logical_not(chunk0_requested))
    def _():
        for k, c in enumerate(chunk_copies(0, 0)):
            c.start(priority=k % 2)


    @pl.when(valid & (mode == TILE_FIRST))
    def _():
        xb = prepare_rows(tile_rows)
        acc = None
        for j in range(nf):
            slot = j % 2
            if j + 1 < nf:
                for k, c in enumerate(chunk_copies(j + 1, 1 - slot)):
                    c.start(priority=k % 2)
            for c in chunk_copies(j, slot):
                c.wait()
            wg_c[j] = stage_in[slot, 0].astype(BF16)
            wu_c[j] = stage_in[slot, 1].astype(BF16)
            wd_c[j] = stage_out[slot].astype(BF16)
            y = chunk(xb, j)
            acc = y if acc is None else acc + y
        finish(acc, tile_rows)

    def steady(rows):
        xb = prepare_rows(rows)
        acc = None
        for j in range(nf):
            y = chunk(xb, j)
            acc = y if acc is None else acc + y
        finish(acc, rows)

    @pl.when(valid & (mode == TILE_FULL))
    def _():
        steady(tile_rows)

    if routed:
        @pl.when(valid & (mode == TILE_HALF))
        def _():
            steady(tile_rows // 2)

    @pl.when(jnp.logical_not(valid))
    def _():
        o_ref[...] = jnp.zeros_like(o_ref)


def expert_swiglu(x, gain, tile_expert, tile_mode, n_valid, wg, wu, wd, tile_rows, tf, out_dtype, pre_norm, name,
                  proj=None, routed=False):
    n_rows = x.shape[0]
    ne, d, f = wg.shape
    assert n_rows % tile_rows == 0 and f % tf == 0
    n_tiles = n_rows // tile_rows
    nf = f // tf
    pre_proj = proj is not None
    if pre_proj:
        a, wp = proj
        a_spec = pl.BlockSpec((tile_rows, a.shape[1]), lambda i, te, fi, nv: (jnp.minimum(i, nv[0] - 1), 0))
    else:
        a, wp = jnp.zeros((8, LANES), BF16), jnp.zeros((LANES, d), BF16)
        a_spec = pl.BlockSpec(a.shape, lambda i, te, fi, nv: (0, 0))
    grid_spec = pltpu.PrefetchScalarGridSpec(
        num_scalar_prefetch=3,
        grid=(n_tiles,),
        in_specs=[pl.BlockSpec((tile_rows, x.shape[1]), lambda i, te, fi, nv: (jnp.minimum(i, nv[0] - 1), 0)),
                  pl.BlockSpec((1, d), lambda i, te, fi, nv: (0, 0)),
                  a_spec,
                  pl.BlockSpec(wp.shape, lambda i, te, fi, nv: (0, 0)),
                  pl.BlockSpec(memory_space=pl.ANY),
                  pl.BlockSpec(memory_space=pl.ANY),
                  pl.BlockSpec(memory_space=pl.ANY)],
        out_specs=pl.BlockSpec((tile_rows, d), lambda i, te, fi, nv: (i, 0)),
        scratch_shapes=[pltpu.VMEM((nf, d, tf), BF16), pltpu.VMEM((nf, d, tf), BF16), pltpu.VMEM((nf, tf, d), BF16),
                        pltpu.VMEM((2, 2, d, tf), F32), pltpu.VMEM((2, tf, d), F32),
                        pltpu.SemaphoreType.DMA((2, 3)),
                        pltpu.VMEM((tile_rows, d) if pre_proj else (8, LANES), F32)],
    )
    return pl.pallas_call(
        functools.partial(_swiglu_kernel, pre_norm=pre_norm, pre_proj=pre_proj, routed=routed, tf=tf),
        out_shape=jax.ShapeDtypeStruct((n_rows, d), out_dtype),
        grid_spec=grid_spec,
        compiler_params=pltpu.CompilerParams(dimension_semantics=("arbitrary",), vmem_limit_bytes=EXPERT_VMEM_LIMIT),
        name=name,
    )(tile_expert, tile_mode, n_valid, x, gain.reshape(1, d).astype(F32), a, wp, wg, wu, wd)


def ffn_dense(x, gain, wg, wu, wd, tm, tf, proj=None):
    t = x.shape[0]
    n_tiles = t // tm
    tile_mode = jnp.full((n_tiles,), TILE_FULL, jnp.int32).at[0].set(TILE_FIRST)
    return expert_swiglu(x, gain, jnp.zeros((n_tiles,), jnp.int32), tile_mode, jnp.full((1,), n_tiles, jnp.int32),
                         wg[None], wu[None], wd[None], tm, tf, F32, True, "ffn_dense", proj=proj)


def _t5_bucket_np(dist):
    max_exact = N_BUCKETS // 2
    n = np.maximum(dist, 0)
    safe = np.maximum(n, 1).astype(np.float32)
    large = max_exact + (np.log(safe / max_exact) / np.log(MAX_DIST / max_exact)
                         * (N_BUCKETS - max_exact)).astype(np.int32)
    large = np.minimum(large, N_BUCKETS - 1)
    return np.where(n < max_exact, n, large).astype(np.int32)


def _bias_kernel(bucket_ref, valid_ref, rb_ref, o_ref):
    bucket = bucket_ref[...]
    for h in range(SW_HEADS):
        acc = jnp.zeros(bucket.shape, F32)
        for b in range(N_BUCKETS):
            acc = jnp.where(bucket == b, rb_ref[b, h], acc)
        for v in range(valid_ref.shape[0]):
            o_ref[v, h] = jnp.where(valid_ref[v] > 0, acc * LOG2E, NEG_INF)


def bias_table(rel_bias):
    qi = np.arange(WINDOW)[:, None] + WINDOW
    kj = np.arange(2 * WINDOW)[None, :]
    dist = qi - kj
    band = (dist >= 0) & (dist < WINDOW)
    valid = np.stack([band, band & (kj >= WINDOW)]).astype(np.int32)
    return pl.pallas_call(
        _bias_kernel,
        out_shape=jax.ShapeDtypeStruct((2, SW_HEADS, WINDOW, 2 * WINDOW), F32),
        in_specs=[pl.BlockSpec(memory_space=pltpu.VMEM), pl.BlockSpec(memory_space=pltpu.VMEM),
                  pl.BlockSpec(memory_space=pltpu.SMEM)],
        out_specs=pl.BlockSpec(memory_space=pltpu.VMEM),
        name="t5_bias_table",
    )(jnp.asarray(_t5_bucket_np(dist)), jnp.asarray(valid), rel_bias.astype(F32))


def _swa_kernel(q_ref, kvp_ref, kvc_ref, bias_ref, qn_ref, kn_ref, sink_ref, o_ref):
    blk, hd = WINDOW, SW_HD
    kv_w = SW_KV_HEADS * hd
    first = jnp.where(pl.program_id(1) == 0, 1, 0)
    gw = 2 * LANES
    gi = lax.broadcasted_iota(jnp.int32, (gw, gw), 0)
    gj = lax.broadcasted_iota(jnp.int32, (gw, gw), 1)
    group_ones = jnp.where((gi // hd) == (gj // hd), 1.0, 0.0).astype(BF16)
    lane = lax.broadcasted_iota(jnp.int32, (1, LANES), 1)
    low_half = lane < hd

    def head_norm(x, gain):
        cols = []
        for c0 in range(0, x.shape[1], gw):
            xc = x[:, c0:c0 + gw]
            ss = _dot((xc * xc).astype(BF16), group_ones)
            cols.append(xc * lax.rsqrt(ss * (1.0 / hd) + EPS))
        return jnp.concatenate(cols, axis=1) * gain

    def dup_half(x, half):
        swapped = pltpu.roll(x, hd, 1)
        return jnp.where(low_half == (half == 0), x, swapped)

    n_qb = q_ref.shape[0] // blk
    qn = head_norm(q_ref[...].astype(F32), qn_ref[...]) * ((hd ** -0.5) * LOG2E)
    half_sel = [jnp.where(low_half, 1.0, 0.0), jnp.where(low_half, 0.0, 1.0)]
    k_all = jnp.concatenate([kvp_ref[:, 0:kv_w], kvc_ref[:, 0:kv_w]], axis=0).astype(F32)
    kn = head_norm(k_all, kn_ref[...])
    v_all = jnp.concatenate([kvp_ref[:, kv_w:2 * kv_w], kvc_ref[:, kv_w:2 * kv_w]], axis=0).astype(F32)
    ks, vs = [], []
    for g in range(SW_KV_HEADS):
        c0 = (g // 2) * LANES
        ks.append(dup_half(kn[:, c0:c0 + LANES], g % 2).astype(BF16))
        vs.append(dup_half(v_all[:, c0:c0 + LANES], g % 2).astype(BF16))

    pairs = [(j, hq) for j in range(n_qb) for hq in range(SW_HEADS)]
    for g0 in range(0, len(pairs), SWA_GROUP):
        group = pairs[g0:g0 + SWA_GROUP]
        scores = {}
        for (j, hq) in group:
            c0 = (hq // 2) * LANES
            q_h = (qn[j * blk:(j + 1) * blk, c0:c0 + LANES] * half_sel[hq % 2]).astype(BF16)
            scores[(j, hq)] = _dot_nt(q_h, ks[hq // SW_GROUP][j * blk:(j + 2) * blk])
        probs = {}
        for (j, hq) in group:
            variant = first if j == 0 else 0
            s = scores[(j, hq)] + bias_ref[variant, hq]
            sink = sink_ref[hq] * LOG2E
            mx = jnp.maximum(jnp.max(s, axis=-1, keepdims=True), sink)
            p = jnp.exp2(s - mx)
            denom = jnp.sum(p, axis=-1, keepdims=True) + jnp.exp2(sink - mx)
            probs[(j, hq)] = (p * (1.0 / denom)).astype(BF16)
        outs = {key: _dot(probs[key], vs[key[1] // SW_GROUP][key[0] * blk:(key[0] + 2) * blk]) for key in group}
        for (j, hq) in group[0::2]:
            c = hq // 2
            o_ref[j * blk:(j + 1) * blk, c * LANES:(c + 1) * LANES] = jnp.where(
                low_half, outs[(j, hq)], outs[(j, hq + 1)]).astype(o_ref.dtype)


def swa_attention(q, kv, bias, q_norm, k_norm, sinks, batch, seq):
    t = q.shape[0]
    blk = WINDOW
    step = SWA_QBLOCKS * blk
    assert seq % step == 0
    nb = seq // step
    qw = SW_HEADS * SW_HD
    kvw = 2 * SW_KV_HEADS * SW_HD
    return pl.pallas_call(
        _swa_kernel,
        out_shape=jax.ShapeDtypeStruct((t, qw), BF16),
        grid=(batch, nb),
        in_specs=[pl.BlockSpec((step, qw), lambda b, n: (b * nb + n, 0)),
                  pl.BlockSpec((blk, kvw), lambda b, n: (jnp.maximum((b * nb + n) * SWA_QBLOCKS - 1, b * nb * SWA_QBLOCKS), 0)),
                  pl.BlockSpec((step, kvw), lambda b, n: (b * nb + n, 0)),
                  pl.BlockSpec((2, SW_HEADS, blk, 2 * blk), lambda b, n: (0, 0, 0, 0)),
                  pl.BlockSpec((1, qw), lambda b, n: (0, 0)),
                  pl.BlockSpec((1, kvw // 2), lambda b, n: (0, 0)),
                  pl.BlockSpec(memory_space=pltpu.SMEM)],
        out_specs=pl.BlockSpec((step, qw), lambda b, n: (b * nb + n, 0)),
        compiler_params=_cparams(("parallel", "parallel")),
        name="swa_attention",
    )(q, kv, kv, bias, jnp.tile(q_norm.astype(F32), SW_HEADS).reshape(1, qw),
      jnp.tile(k_norm.astype(F32), SW_KV_HEADS).reshape(1, kvw // 2), sinks.astype(F32))


def _route_kernel(x_ref, a_ref, wp_ref, g_ref, wr_ref, h_ref, r_ref, wt_ref, tab_ref, cnt_ref,
                  sel_s, gw_s, cnt_s, start_s, run_s, *, tile_rows):
    ne = N_EXPERTS
    p = pl.program_id(0)
    i = pl.program_id(1)
    tm = x_ref.shape[0]
    sub = lax.broadcasted_iota(jnp.int32, (ne, tm), 0).astype(F32)

    def seg_rows(sel):
        n = jnp.sum(sel, axis=1, keepdims=True)
        return jnp.floor((n + (SEG_ALIGN - 1)) * (1.0 / SEG_ALIGN)) * SEG_ALIGN

    def excl_cumsum_experts(v):
        sub8 = lax.broadcasted_iota(jnp.int32, v.shape, 0)
        out = jnp.zeros_like(v)
        for e in range(ne - 1):
            out = out + jnp.where(sub8 > e, v[e:e + 1, :], 0.0)
        return out

    @pl.when(p == 0)
    def _():
        @pl.when(i == 0)
        def _():
            cnt_s[...] = jnp.zeros_like(cnt_s)

        x = x_ref[...] + _dot(a_ref[...], wp_ref[...])
        h_ref[...] = x
        ms = jnp.mean(x * x, axis=-1, keepdims=True)
        xn32 = (x * lax.rsqrt(ms + EPS)) * g_ref[...]
        xn_hi = xn32.astype(BF16)
        xn_lo = (xn32 - xn_hi.astype(F32)).astype(BF16)
        p_hi = _dot_nt(wr_ref[...], xn_hi)
        p_lo = _dot_nt(wr_ref[...], xn_lo)
        logits = p_hi[0:ne] + p_hi[ne:2 * ne] + p_lo[0:ne]
        m1 = jnp.max(logits, axis=0, keepdims=True)
        i1 = jnp.min(jnp.where(logits == m1, sub, float(ne)), axis=0, keepdims=True)
        l2 = jnp.where(sub == i1, -jnp.inf, logits)
        m2 = jnp.max(l2, axis=0, keepdims=True)
        i2 = jnp.min(jnp.where(l2 == m2, sub, float(ne)), axis=0, keepdims=True)
        e2 = jnp.exp(m2 - m1)
        w1 = 1.0 / (1.0 + e2)
        w2 = e2 / (1.0 + e2)
        sel = jnp.where((sub == i1) | (sub == i2), 1.0, 0.0)
        sel_s[i] = sel
        gw_s[i] = jnp.where(sub == i1, w1, jnp.where(sub == i2, w2, 0.0))
        cnt_s[...] += seg_rows(sel)

    @pl.when(p == 1)
    def _():
        @pl.when(i == 0)
        def _():
            cnt = cnt_s[...]
            padded = jnp.floor((cnt + (tile_rows - 1)) * (1.0 / tile_rows)) * tile_rows
            start_s[...] = excl_cumsum_experts(padded)
            run_s[...] = jnp.zeros_like(run_s)
            cnt_ref[...] = cnt

        sel = sel_s[i]
        gw = gw_s[i]
        ti = lax.broadcasted_iota(jnp.int32, (tm, tm), 0)
        tj = lax.broadcasted_iota(jnp.int32, (tm, tm), 1)
        tri = jnp.where(ti <= tj, 1.0, 0.0).astype(BF16)
        csum = _dot(sel.astype(BF16), tri)
        seg = jnp.broadcast_to(seg_rows(sel), run_s.shape)
        local0 = excl_cumsum_experts(seg)
        tab_ref[0, 0] = start_s[...] + run_s[...]
        tab_ref[0, 1] = seg
        tab_ref[0, 2] = local0
        run_s[...] += seg
        local_row = local0[:, 0:1] + csum - sel
        ia = jnp.min(jnp.where(sel > 0.0, sub, float(ne)), axis=0, keepdims=True)
        ib = jnp.max(jnp.where(sel > 0.0, sub, -1.0), axis=0, keepdims=True)
        pick_a = sub == ia
        pick_b = sub == ib
        rows = [jnp.sum(jnp.where(pick_a, local_row, 0.0), axis=0, keepdims=True),
                jnp.sum(jnp.where(pick_b, local_row, 0.0), axis=0, keepdims=True),
                jnp.sum(jnp.where(pick_a, gw, 0.0), axis=0, keepdims=True),
                jnp.sum(jnp.where(pick_b, gw, 0.0), axis=0, keepdims=True)]
        r_ref[...] = jnp.concatenate(rows + [jnp.zeros((ne - 4, tm), F32)], axis=0)
        wpad = jnp.concatenate(rows[2:4] + rows[0:2] + [jnp.zeros((LANES - 4, tm), F32)], axis=0)
        wt_ref[...] = wpad.T


def moe_route(x, a, wp, gain, w_router, tm, tile_rows):
    t, d = x.shape
    ne = w_router.shape[1]
    assert ne == N_EXPERTS
    w_hi = w_router.astype(BF16)
    w_lo = (w_router - w_hi.astype(F32)).astype(BF16)
    wr = jnp.concatenate([w_hi.T, w_lo.T], axis=0)
    tm = min(tm, t)
    nt = t // tm

    def row_map(p, i):
        return (i * (1 - p) + (nt - 1) * p, 0)

    return pl.pallas_call(
        functools.partial(_route_kernel, tile_rows=tile_rows),
        out_shape=(jax.ShapeDtypeStruct((t, d), F32),
                   jax.ShapeDtypeStruct((ne, t), F32), jax.ShapeDtypeStruct((t, LANES), F32),
                   jax.ShapeDtypeStruct((nt, 3, ne, LANES), F32), jax.ShapeDtypeStruct((ne, LANES), F32)),
        grid=(2, nt),
        in_specs=[pl.BlockSpec((tm, d), row_map),
                  pl.BlockSpec((tm, a.shape[1]), row_map),
                  pl.BlockSpec(wp.shape, lambda p, i: (0, 0)),
                  pl.BlockSpec((1, d), lambda p, i: (0, 0)),
                  pl.BlockSpec((2 * ne, d), lambda p, i: (0, 0))],
        out_specs=(pl.BlockSpec((tm, d), row_map),
                   pl.BlockSpec((ne, tm), lambda p, i: (0, i * p)),
                   pl.BlockSpec((tm, LANES), lambda p, i: (i * p, 0)),
                   pl.BlockSpec((1, 3, ne, LANES), lambda p, i: (i * p, 0, 0, 0)),
                   pl.BlockSpec((ne, LANES), lambda p, i: (0, 0))),
        scratch_shapes=[pltpu.VMEM((nt, ne, tm), F32), pltpu.VMEM((nt, ne, tm), F32),
                        pltpu.VMEM((ne, LANES), F32), pltpu.VMEM((ne, LANES), F32), pltpu.VMEM((ne, LANES), F32)],
        compiler_params=_cparams(("arbitrary", "arbitrary")),
        name="moe_route",
    )(x, a, wp, gain.reshape(1, d), wr)


def _segment_copies(tab_ref, i, e, local_ref, slot_ref, sem, to_slots):
    base = (i * N_EXPERTS + e) * 3
    slot0, rows, local0 = tab_ref[base], tab_ref[base + 1], tab_ref[base + 2]
    out = []
    done = 0
    size = MOE_TOKEN_TILE
    while size >= SEG_ALIGN:
        take = rows & size
        loc = local_ref.at[pl.ds(pl.multiple_of(local0 + done, SEG_ALIGN), size)]
        slt = slot_ref.at[pl.ds(pl.multiple_of(slot0 + done, SEG_ALIGN), size)]
        desc = pltpu.make_async_copy(loc, slt, sem) if to_slots else pltpu.make_async_copy(slt, loc, sem)
        out.append((take != 0, desc))
        done = done + take
        size //= 2
    return out


def _run_segment_copies(tab_ref, tile, slot, rows_s, slot_ref, sems, to_slots, action):
    for e in range(N_EXPERTS):
        for cond, desc in _segment_copies(tab_ref, tile, e, rows_s.at[slot], slot_ref, sems.at[slot], to_slots):
            @pl.when(cond)
            def _():
                getattr(desc, action)()


def _dispatch_kernel(tab_ref, zf_ref, x_ref, g_ref, r_ref, xs_ref, rows_s, zero_s, sem, zsem, *, tile_rows):
    i = pl.program_id(0)
    tm = x_ref.shape[0]
    n_local = rows_s.shape[1]

    @pl.when(i == 0)
    def _():
        zero_s[...] = jnp.zeros_like(zero_s)

        def zero_copy(e):
            row0 = pl.multiple_of(zf_ref[e], tile_rows)
            return pltpu.make_async_copy(zero_s, xs_ref.at[pl.ds(row0, tile_rows)], zsem)

        for e in range(zf_ref.shape[0]):
            @pl.when(zf_ref[e] >= 0)
            def _():
                zero_copy(e).start()
        for e in range(zf_ref.shape[0]):
            @pl.when(zf_ref[e] >= 0)
            def _():
                zero_copy(e).wait()

    x = x_ref[...]
    ms = jnp.mean(x * x, axis=-1, keepdims=True)
    xn = ((x * lax.rsqrt(ms + EPS)) * g_ref[...]).astype(BF16)
    row_id = lax.broadcasted_iota(jnp.int32, (n_local, tm), 0).astype(F32)
    onehot = jnp.where((row_id == r_ref[0:1, :]) | (row_id == r_ref[1:2, :]), 1.0, 0.0).astype(BF16)
    slot = lax.rem(i, 2)
    rows_s[slot] = _dot(onehot, xn)

    _run_segment_copies(tab_ref, i, slot, rows_s, xs_ref, sem, True, "start")

    @pl.when(i > 0)
    def _():
        _run_segment_copies(tab_ref, i - 1, 1 - slot, rows_s, xs_ref, sem, True, "wait")

    @pl.when(i == pl.num_programs(0) - 1)
    def _():
        _run_segment_copies(tab_ref, i, slot, rows_s, xs_ref, sem, True, "wait")


def moe_dispatch(x, gain, r, tab, zf_rows, n_slots, tm, tile_rows):
    t, d = x.shape
    nt = t // tm
    n_local = TOP_K * tm + N_EXPERTS * SEG_ALIGN
    grid_spec = pltpu.PrefetchScalarGridSpec(
        num_scalar_prefetch=2,
        grid=(nt,),
        in_specs=[pl.BlockSpec((tm, d), lambda i, tb, zf: (i, 0)),
                  pl.BlockSpec((1, d), lambda i, tb, zf: (0, 0)),
                  pl.BlockSpec((N_EXPERTS, tm), lambda i, tb, zf: (0, i))],
        out_specs=pl.BlockSpec(memory_space=pl.ANY),
        scratch_shapes=[pltpu.VMEM((2, n_local, d), F32), pltpu.VMEM((tile_rows, d), F32),
                        pltpu.SemaphoreType.DMA((2,)), pltpu.SemaphoreType.DMA],
    )
    return pl.pallas_call(
        functools.partial(_dispatch_kernel, tile_rows=tile_rows),
        out_shape=jax.ShapeDtypeStruct((n_slots, d), F32),
        grid_spec=grid_spec,
        compiler_params=_cparams(("arbitrary",)),
        name="moe_dispatch",
    )(tab, zf_rows, x, gain.reshape(1, d), r)


def _combine_kernel(tab_ref, h_ref, wt_ref, ys_ref, o_ref, rows_s, sems):
    i = pl.program_id(0)
    tm = h_ref.shape[0]
    n_local = rows_s.shape[1]
    slot = lax.rem(i, 2)

    def fetch(tile, into):
        rows_s[into] = jnp.zeros(rows_s.shape[1:], rows_s.dtype)
        _run_segment_copies(tab_ref, tile, into, rows_s, ys_ref, sems, False, "start")

    @pl.when(i == 0)
    def _():
        fetch(i, slot)

    @pl.when(i + 1 < pl.num_programs(0))
    def _():
        fetch(i + 1, 1 - slot)

    _run_segment_copies(tab_ref, i, slot, rows_s, ys_ref, sems, False, "wait")

    wt = wt_ref[...]
    y = rows_s[slot].astype(BF16)
    col_id = lax.broadcasted_iota(jnp.int32, (tm, n_local), 1).astype(F32)
    pick_a = jnp.where(col_id == wt[:, 2:3], 1.0, 0.0).astype(BF16)
    pick_b = jnp.where(col_id == wt[:, 3:4], 1.0, 0.0).astype(BF16)
    o_ref[...] = h_ref[...] + wt[:, 0:1] * _dot(pick_a, y) + wt[:, 1:2] * _dot(pick_b, y)


def moe_combine(h, wt, tab, ys, tm):
    t, d = h.shape
    nt = t // tm
    n_local = TOP_K * tm + N_EXPERTS * SEG_ALIGN
    grid_spec = pltpu.PrefetchScalarGridSpec(
        num_scalar_prefetch=1,
        grid=(nt,),
        in_specs=[pl.BlockSpec((tm, d), lambda i, tb: (i, 0)),
                  pl.BlockSpec((tm, LANES), lambda i, tb: (i, 0)),
                  pl.BlockSpec(memory_space=pl.ANY)],
        out_specs=pl.BlockSpec((tm, d), lambda i, tb: (i, 0)),
        scratch_shapes=[pltpu.VMEM((2, n_local, d), F32), pltpu.SemaphoreType.DMA((2,))],
    )
    return pl.pallas_call(
        _combine_kernel,
        out_shape=jax.ShapeDtypeStruct((t, d), F32),
        grid_spec=grid_spec,
        compiler_params=_cparams(("arbitrary",)),
        name="moe_combine",
    )(tab, h, wt, ys)


def moe_layer(x, a, wp, gain, w_router, wg, wu, wd):
    t, d = x.shape
    ne = w_router.shape[1]
    tr, tm = MOE_TILE_ROWS, MOE_TOKEN_TILE
    assert t % tm == 0 and ne == N_EXPERTS
    nt = t // tm
    n_tiles = -(-(TOP_K * t + nt * ne * (SEG_ALIGN - 1) + ne * (tr - 1)) // tr)
    n_slots = n_tiles * tr

    h, r, wt, tab, cnt = moe_route(x, a, wp, gain, w_router, tm, tr)
    tab = jnp.transpose(tab[:, :, :, 0], (0, 2, 1)).astype(jnp.int32).reshape(-1)

    counts = cnt[:, 0].astype(jnp.int32)
    padded = ((counts + (tr - 1)) // tr) * tr
    ends = jnp.cumsum(padded)
    n_valid = (ends[-1] // tr).astype(jnp.int32)
    tile_row0 = jnp.arange(n_tiles, dtype=jnp.int32) * tr
    tile_expert = jnp.sum((tile_row0[:, None] >= ends[None, :]).astype(jnp.int32), axis=1)
    tile_expert = jnp.minimum(tile_expert, ne - 1)
    tile_expert = jnp.where(jnp.arange(n_tiles) < n_valid, tile_expert, tile_expert[jnp.maximum(n_valid - 1, 0)])
    prev_expert = jnp.concatenate([jnp.full((1,), -1, jnp.int32), tile_expert[:-1]])
    rows_used = (ends - padded + counts)[tile_expert] - tile_row0
    tile_mode = jnp.where(tile_expert != prev_expert, TILE_FIRST,
                          jnp.where(rows_used <= tr // 2, TILE_HALF, TILE_FULL)).astype(jnp.int32)
    tail = jnp.arange(TOP_K * t // tr, n_tiles, dtype=jnp.int32)
    zf_rows = jnp.concatenate([jnp.where(padded > 0, ends - tr, -1),
                               jnp.where(tail >= n_valid, tail * tr, -1)]).astype(jnp.int32)

    xs = moe_dispatch(h, gain, r, tab, zf_rows, n_slots, tm, tr)
    ys = expert_swiglu(xs, gain, tile_expert, tile_mode, n_valid.reshape(1), wg, wu, wd, tr, FFN_CHUNK, F32, False,
                       "moe_experts", routed=True)
    return moe_combine(h, wt, tab, ys, tm)


def kernel(x, a_norm, a_w_in, a_conv, a_log_decay, a_dt_bias, a_out_norm, a_w_out, kv_norm, kv_w, k_norm,
           b_norm, b_w_q, q_norm, b_sinks, b_w_o, rel_bias, ffn_norm, dense_w_gate, dense_w_up, dense_w_down,
           moe_router, moe_w_gate, moe_w_up, moe_w_down):
    batch, seq, d = x.shape
    t = batch * seq
    nh, hd = LA_HEADS, LA_D
    main_w = 4 * nh * hd
    h0 = x.reshape(t, d)

    w_in = a_w_in[0]
    w_main = w_in[:, 0:main_w].astype(BF16)
    w_gate = jnp.zeros((d, LANES), BF16).at[:, 0:2 * nh].set(w_in[:, main_w:main_w + 2 * nh].astype(BF16))
    proj, gates = norm_matmul(h0, [(a_norm[0], w_main, BF16), (a_norm[0], w_gate, F32)], IN_PROJ_TILE,
                              "gdn_in_proj")
    o = gdn_core(proj, gates, a_conv[0], a_log_decay[0], a_dt_bias[0], a_out_norm[0], batch, seq)

    h2 = ffn_dense(h0, ffn_norm[0], dense_w_gate[0], dense_w_up[0], dense_w_down[0], MOE_TILE_ROWS, FFN_CHUNK,
                   proj=(o, a_w_out[0].astype(BF16)))

    kv, q = norm_matmul(h2, [(kv_norm, kv_w.astype(BF16), BF16), (b_norm[0], b_w_q[0].astype(BF16), BF16)],
                        QKV_PROJ_TILE, "qkv_proj")
    bias = bias_table(rel_bias)
    attn = swa_attention(q, kv, bias, q_norm[0], k_norm, b_sinks[0], batch, seq)

    h4 = moe_layer(h2, attn, b_w_o[0].astype(BF16), ffn_norm[1], moe_router[0], moe_w_gate[0], moe_w_up[0],
                   moe_w_down[0])
    return h4.reshape(batch, seq, d)
```
